```python
import jax, jax.numpy as jnp
from jax import lax
import numpy as np

D_MODEL = 1024
BATCH = 8
SEQ = 4096
DEPTH = 2

HG_HEADS = 4
HG_DK = 128
HG_W = HG_HEADS * HG_DK
RET_HEADS = 4
RET_DK = 128
RET_W = RET_HEADS * RET_DK
RW_HEADS = 8
RW_N = 64
RW_W = RW_HEADS * RW_N
RW_DECAY_LORA = 64
RW_A_LORA = 64
RW_GATE_LORA = 128
RW_GN_EPS = 64e-5
N_BRANCH = 3
HG_COLS = 4 * HG_W
RET_COLS = 4 * RET_W
RW_COLS = 3 * RW_W + RW_DECAY_LORA + RW_A_LORA + RW_GATE_LORA
GATE_COLS = N_BRANCH * D_MODEL
IN_COLS = HG_COLS + RET_COLS + RW_COLS + GATE_COLS
CHUNK = 64
N_GROUPS = 4
EXPERTS_PER_GROUP = 8
N_EXPERTS = N_GROUPS * EXPERTS_PER_GROUP
TOP_K_IN_GROUP = 2
D_EXPERT = 512

ROPE_THETA = 10000.0
NORM_EPS = 1e-6

kernel_name = 'hybrid_hgrn2_retnet_rwkv7_hmoe'


def split_cols(z, sizes):
    offs = np.cumsum([0] + list(sizes))
    return [z[..., int(offs[i]):int(offs[i + 1])] for i in range(len(sizes))]


def split_heads(z, n_heads):
    b, t, w = z.shape
    return z.reshape(b, t, n_heads, w // n_heads)


def to_chunks(z):
    b, t, h, d = z.shape
    return z.reshape(b, t // CHUNK, CHUNK, h, d).transpose(0, 3, 1, 2, 4)


def from_chunks(z):
    b, h, nc, l, d = z.shape
    return z.transpose(0, 2, 3, 1, 4).reshape(b, nc * l, h, d)


def rmsnorm(x, gain):
    xf = x.astype(jnp.float32)
    y = xf * lax.rsqrt(jnp.mean(xf * xf, axis=-1, keepdims=True) + NORM_EPS)
    return (y * gain.astype(jnp.float32)).astype(x.dtype)


def head_rmsnorm(o):
    return o * lax.rsqrt(jnp.mean(o * o, axis=-1, keepdims=True) + NORM_EPS)


def rope(z, positions):
    d = z.shape[-1]
    inv_freq = ROPE_THETA ** (-jnp.arange(0, d, 2, dtype=jnp.float32) / d)
    ang = positions.astype(jnp.float32)[:, :, None, None] * inv_freq
    cos, sin = jnp.cos(ang), jnp.sin(ang)
    z1, z2 = z[..., : d // 2], z[..., d // 2:]
    return jnp.concatenate([z1 * cos - z2 * sin, z1 * sin + z2 * cos], axis=-1)


def hgrn2_mixer(zq, zf, zi, zg, lower_bound, norm_w):
    f32 = jnp.float32
    dtype = zq.dtype
    f = lower_bound + (1.0 - lower_bound) * jax.nn.sigmoid(zf.astype(f32))
    log_f = jnp.log(jnp.maximum(f, 1e-30))
    q = jax.nn.silu(zq.astype(f32)) * (HG_DK ** -0.5)
    k = 1.0 - f
    v = zi.astype(f32)
    qc, kc, gc, vc = (to_chunks(split_heads(a, HG_HEADS)) for a in (q, k, log_f, v))
    b, h, nc, l, dk = qc.shape
    causal = jnp.tril(jnp.ones((CHUNK, CHUNK), dtype=bool))[:, :, None]

    def chunk_step(S, inp):
        q_, k_, g_, v_ = inp
        cum = jnp.cumsum(g_, axis=2)
        diff = cum[:, :, :, None, :] - cum[:, :, None, :, :]
        decay = jnp.exp(jnp.where(causal, diff, -jnp.inf))
        scores = jnp.einsum('bhtd,bhsd,bhtsd->bhts', q_, k_, decay)
        o = jnp.einsum('bhts,bhsv->bhtv', scores, v_) + jnp.einsum('bhtd,bhdv->bhtv', q_ * jnp.exp(cum), S)
        last = cum[:, :, -1:, :]
        S = jnp.exp(last[:, :, 0, :])[..., None] * S + jnp.einsum('bhsd,bhsv->bhdv', k_ * jnp.exp(last - cum), v_)
        return S, o

    S0 = jnp.zeros((b, h, dk, vc.shape[-1]), f32)
    _, o = lax.scan(chunk_step, S0, tuple(jnp.moveaxis(a, 2, 0) for a in (qc, kc, gc, vc)))
    o = from_chunks(jnp.moveaxis(o, 0, 2))
    o = head_rmsnorm(o) * norm_w.astype(f32) * jax.nn.silu(split_heads(zg.astype(f32), HG_HEADS))
    bb, t = o.shape[:2]
    return o.reshape(bb, t, HG_W).astype(dtype)


def retention_mixer(zq, zk, zv, zg, positions):
    f32 = jnp.float32
    dtype = zq.dtype
    q = rope(split_heads(zq.astype(f32), RET_HEADS), positions) * (RET_DK ** -0.5)
    k = rope(split_heads(zk.astype(f32), RET_HEADS), positions)
    v = split_heads(zv.astype(f32), RET_HEADS)
    qc, kc, vc = (to_chunks(a) for a in (q, k, v))
    log_gamma = jnp.log(1.0 - 2.0 ** (-5.0 - jnp.arange(RET_HEADS, dtype=f32)))
    j = jnp.arange(CHUNK, dtype=f32)
    rel = j[:, None] - j[None, :]
    causal = rel >= 0
    dmask = jnp.where(causal, jnp.exp(jnp.where(causal, rel, 0.0)[None] * log_gamma[:, None, None]), 0.0)
    scores = jnp.einsum('bhctd,bhcsd->bhcts', qc, kc) * dmask[None, :, None]
    inner = jnp.einsum('bhcts,bhcsv->bhctv', scores, vc)
    zeta = jnp.exp((CHUNK - 1.0 - j)[None, :] * log_gamma[:, None])
    kv = jnp.einsum('bhcsd,hs,bhcsv->cbhdv', kc, zeta, vc)
    chunk_decay = jnp.exp(CHUNK * log_gamma)[None, :, None, None]

    def step(R, kv_c):
        return chunk_decay * R + kv_c, R

    _, r_before = lax.scan(step, jnp.zeros(kv.shape[1:], f32), kv)
    xi = jnp.exp((j + 1.0)[None, :] * log_gamma[:, None])
    cross = jnp.einsum('bhctd,ht,cbhdv->bhctv', qc, xi, r_before)
    o = from_chunks(inner + cross)
    o = head_rmsnorm(o) * jax.nn.silu(split_heads(zg.astype(f32), RET_HEADS))
    bb, t = o.shape[:2]
    return o.reshape(bb, t, RET_W).astype(dtype)


def rwkv7_mixer(z, mu, w0, w2, a0, a2, g2, k_k, k_a, r_k, ln_w, ln_b):
    f32 = jnp.float32
    dtype = z.dtype
    z = z.astype(f32)
    z_prev = jnp.pad(z[:, :-1], ((0, 0), (1, 0), (0, 0)))
    z = z + mu * (z_prev - z)
    r, k, v, w_lo, a_lo, g_lo = split_cols(z, (RW_W, RW_W, RW_W, RW_DECAY_LORA, RW_A_LORA, RW_GATE_LORA))
    w = -jax.nn.softplus(-(w0 + jnp.tanh(w_lo) @ w2)) - 0.5
    decay = jnp.exp(-jnp.exp(w))
    a = jax.nn.sigmoid(a0 + a_lo @ a2)
    g = jax.nn.sigmoid(g_lo) @ g2
    kk = split_heads(k * k_k, RW_HEADS)
    kk = kk * lax.rsqrt(jnp.maximum(jnp.sum(kk * kk, axis=-1, keepdims=True), 1e-24))
    k = k * (1.0 + (a - 1.0) * k_a)
    rh, wh, kh, vh, ah = (split_heads(t, RW_HEADS) for t in (r, decay, k, v, a))

    def step(S, inp):
        r_, w_, k_, v_, kk_, a_ = inp
        sa = jnp.einsum('bhvk,bhk->bhv', S, -kk_)
        S = S * w_[:, :, None, :] + sa[..., None] * (kk_ * a_)[:, :, None, :] + v_[..., None] * k_[:, :, None, :]
        return S, jnp.einsum('bhvk,bhk->bhv', S, r_)

    bb, t = z.shape[:2]
    S0 = jnp.zeros((bb, RW_HEADS, RW_N, RW_N), f32)
    _, o = lax.scan(step, S0, tuple(jnp.moveaxis(a_, 1, 0) for a_ in (rh, wh, kh, vh, kk, ah)))
    o = jnp.moveaxis(o, 0, 1)
    mean = jnp.mean(o, axis=-1, keepdims=True)
    var = jnp.mean(jnp.square(o - mean), axis=-1, keepdims=True)
    o = ((o - mean) * lax.rsqrt(var + RW_GN_EPS)).reshape(bb, t, RW_W) * ln_w + ln_b
    bonus = (jnp.sum(rh * kh * r_k, axis=-1, keepdims=True) * vh).reshape(bb, t, RW_W)
    return ((o + bonus) * g).astype(dtype)


def token_mixer(h, positions, lower_bound, w_in, hg_norm_w, rw_mu, rw_w0, rw_w2, rw_a0, rw_a2, rw_g2,
                rw_k_k, rw_k_a, rw_r_k, rw_ln_w, rw_ln_b, br_hg, br_ret, br_rw, w_out):
    z = h @ w_in
    z_hg, z_ret, z_rw, z_gate = split_cols(z, (HG_COLS, RET_COLS, RW_COLS, GATE_COLS))
    y_hg = hgrn2_mixer(*split_cols(z_hg, (HG_W,) * 4), lower_bound, hg_norm_w) @ br_hg
    y_ret = retention_mixer(*split_cols(z_ret, (RET_W,) * 4), positions) @ br_ret
    y_rw = rwkv7_mixer(z_rw, rw_mu, rw_w0, rw_w2, rw_a0, rw_a2, rw_g2, rw_k_k, rw_k_a, rw_r_k,
                       rw_ln_w, rw_ln_b) @ br_rw
    gates = jax.nn.sigmoid(z_gate.astype(jnp.float32)).astype(h.dtype)
    g_hg, g_ret, g_rw = split_cols(gates, (D_MODEL,) * N_BRANCH)
    return (g_hg * y_hg + g_ret * y_ret + g_rw * y_rw) @ w_out


def hier_moe(h, router_g, router_e, w1, w3, w2):
    b, t, d = h.shape
    hf = h.reshape(b * t, d)
    g_logits = (hf @ router_g).astype(jnp.float32)
    g_idx = jnp.argmax(g_logits, axis=-1)
    g_w = jnp.take_along_axis(jax.nn.softmax(g_logits, axis=-1), g_idx[:, None], axis=-1)
    e_logits = (hf @ router_e).astype(jnp.float32).reshape(b * t, N_GROUPS, EXPERTS_PER_GROUP)
    e_logits = jnp.take_along_axis(e_logits, g_idx[:, None, None], axis=1)[:, 0]
    top_p, top_i = lax.top_k(jax.nn.softmax(e_logits, axis=-1), TOP_K_IN_GROUP)
    top_p = top_p / jnp.sum(top_p, axis=-1, keepdims=True)
    expert_id = g_idx[:, None] * EXPERTS_PER_GROUP + top_i
    combine = jnp.sum(jax.nn.one_hot(expert_id, N_EXPERTS, dtype=jnp.float32) * (g_w * top_p)[..., None], axis=1)
    combine = combine.astype(h.dtype).T

    def expert_step(acc, p):
        w1_e, w3_e, w2_e, c_e = p
        y = (jax.nn.silu(hf @ w1_e) * (hf @ w3_e)) @ w2_e
        return acc + c_e[:, None] * y, None

    y, _ = lax.scan(expert_step, jnp.zeros_like(hf), (w1, w3, w2, combine))
    return y.reshape(b, t, d)


def setup_inputs(seed: int = 0) -> dict:
    key = jax.random.key(seed)
    keys = iter(jax.random.split(key, 48))

    def nrm(shape, scale):
        return jax.random.normal(next(keys), shape, jnp.float32) * scale

    def unif(shape, lo, hi):
        return jax.random.uniform(next(keys), shape, jnp.float32, lo, hi)

    L = DEPTH
    x = nrm((BATCH, SEQ, D_MODEL), 1.0)
    c = nrm((BATCH, D_MODEL), 1.0)
    offsets = jax.random.randint(next(keys), (BATCH, 1), 0, SEQ, dtype=jnp.int32)
    positions = offsets + jnp.arange(SEQ, dtype=jnp.int32)[None, :]
    return {
        'x': x,
        'c': c,
        'positions': positions,
        'ada_w': nrm((L, D_MODEL, 6 * D_MODEL), 0.5 * D_MODEL ** -0.5),
        'ada_b': nrm((L, 6 * D_MODEL), 0.02),
        'norm1_g': 1.0 + nrm((L, D_MODEL), 0.05),
        'norm2_g': 1.0 + nrm((L, D_MODEL), 0.05),
        'w_in': nrm((L, D_MODEL, IN_COLS), D_MODEL ** -0.5),
        'hg_lb_table': nrm((L, HG_W), 0.5),
        'hg_norm_w': 1.0 + nrm((L, HG_DK), 0.05),
        'rw_mu': unif((L, RW_COLS), 0.0, 1.0),
        'rw_w0': unif((L, RW_W), -6.5, -1.5),
        'rw_w2': nrm((L, RW_DECAY_LORA, RW_W), 0.1 * RW_DECAY_LORA ** -0.5),
        'rw_a0': nrm((L, RW_W), 0.5),
        'rw_a2': nrm((L, RW_A_LORA, RW_W), 0.5 * RW_A_LORA ** -0.5),
        'rw_g2': nrm((L, RW_GATE_LORA, RW_W), RW_GATE_LORA ** -0.5),
        'rw_k_k': 0.85 + nrm((L, RW_W), 0.05),
        'rw_k_a': 1.0 + nrm((L, RW_W), 0.05),
        'rw_r_k': nrm((L, RW_HEADS, RW_N), 0.1),
        'rw_ln_w': 1.0 + nrm((L, RW_W), 0.05),
        'rw_ln_b': nrm((L, RW_W), 0.02),
        'br_hg': nrm((L, HG_W, D_MODEL), HG_W ** -0.5),
        'br_ret': nrm((L, RET_W, D_MODEL), RET_W ** -0.5),
        'br_rw': nrm((L, RW_W, D_MODEL), RW_W ** -0.5),
        'w_out': nrm((L, D_MODEL, D_MODEL), D_MODEL ** -0.5),
        'router_g': nrm((L, D_MODEL, N_GROUPS), D_MODEL ** -0.5),
        'router_e': nrm((L, D_MODEL, N_EXPERTS), D_MODEL ** -0.5),
        'moe_w1': nrm((L, N_EXPERTS, D_MODEL, D_EXPERT), D_MODEL ** -0.5),
        'moe_w3': nrm((L, N_EXPERTS, D_MODEL, D_EXPERT), D_MODEL ** -0.5),
        'moe_w2': nrm((L, N_EXPERTS, D_EXPERT, D_MODEL), D_EXPERT ** -0.5),
        'final_g': 1.0 + nrm((D_MODEL,), 0.05),
    }


def reference(x, c, positions, ada_w, ada_b, norm1_g, norm2_g, w_in, hg_lb_table, hg_norm_w, rw_mu, rw_w0,
              rw_w2, rw_a0, rw_a2, rw_g2, rw_k_k, rw_k_a, rw_r_k, rw_ln_w, rw_ln_b, br_hg, br_ret, br_rw, w_out,
              router_g, router_e, moe_w1, moe_w3, moe_w2, final_g):
    lb_p = jax.nn.softmax(hg_lb_table.astype(jnp.float32), axis=0)
    lower_bounds = jnp.cumsum(lb_p, axis=0) - lb_p[0]
    cond = jax.nn.silu(c)
    for l in range(DEPTH):
        mod = cond @ ada_w[l] + ada_b[l]
        shift1, scale1, gate1, shift2, scale2, gate2 = jnp.split(mod[:, None, :], 6, axis=-1)
        h = rmsnorm(x, norm1_g[l]) * (1.0 + scale1) + shift1
        y = token_mixer(h, positions, lower_bounds[l], w_in[l], hg_norm_w[l], rw_mu[l], rw_w0[l], rw_w2[l],
                        rw_a0[l], rw_a2[l], rw_g2[l], rw_k_k[l], rw_k_a[l], rw_r_k[l], rw_ln_w[l], rw_ln_b[l],
                        br_hg[l], br_ret[l], br_rw[l], w_out[l])
        x = x + gate1 * y
        h = rmsnorm(x, norm2_g[l]) * (1.0 + scale2) + shift2
        x = x + gate2 * hier_moe(h, router_g[l], router_e[l], moe_w1[l], moe_w3[l], moe_w2[l])
    return rmsnorm(x, final_g)
```

```python
import functools

import jax
import jax.numpy as jnp
from jax import lax
from jax.experimental import pallas as pl
from jax.experimental.pallas import tpu as pltpu

F32 = jnp.float32
BF16 = jnp.bfloat16
HIGHEST = lax.Precision.HIGHEST

HG_HEADS = 4
HG_DK = 128
HG_W = HG_HEADS * HG_DK
RET_HEADS = 4
RET_DK = 128
RET_W = RET_HEADS * RET_DK
RW_HEADS = 8
RW_N = 64
RW_W = RW_HEADS * RW_N
RW_DECAY_LORA = 64
RW_A_LORA = 64
RW_GATE_LORA = 128
RW_COLS = 3 * RW_W + RW_DECAY_LORA + RW_A_LORA + RW_GATE_LORA
RW_GN_EPS = 64e-5
N_GROUPS = 4
EXPERTS_PER_GROUP = 8
N_EXPERTS = N_GROUPS * EXPERTS_PER_GROUP
ROPE_THETA = 10000.0
NORM_EPS = 1e-6

LANES = 128
VMEM_LIMIT = 56 * 1024 * 1024

GATE_OFF = 0
HG_OFF = 3 * 1024
RET_OFF = HG_OFF + 4 * HG_W
RW_OFF = RET_OFF + 4 * RET_W
IN_COLS = RW_OFF + RW_COLS

HG_CHUNK = 64
HG_SUB = 16
RW_CHUNK = 64
RW_BLK = 16


def _cparams(sem):
    return pltpu.CompilerParams(dimension_semantics=sem, vmem_limit_bytes=VMEM_LIMIT)


def _dot(a, b, precision=None):
    return jnp.dot(a, b, preferred_element_type=F32, precision=precision)


def _dot_nt(a, b, precision=None):
    return lax.dot_general(a, b, (((1,), (1,)), ((), ())), preferred_element_type=F32, precision=precision)


def _dot_tn(a, b, precision=None):
    return lax.dot_general(a, b, (((0,), (0,)), ((), ())), preferred_element_type=F32, precision=precision)


def _sigmoid(x):
    return 1.0 / (1.0 + jnp.exp(-x))


def _silu(x):
    return x * _sigmoid(x)


def _rms_mod(x, gain, scale, shift):
    y = x * lax.rsqrt(jnp.mean(x * x, axis=-1, keepdims=True) + NORM_EPS)
    return (y * gain) * (1.0 + scale) + shift


def _mod_kernel(c_ref, w_ref, b_ref, o_ref):
    c = c_ref[...]
    o_ref[0] = _dot(_silu(c), w_ref[0], HIGHEST) + b_ref[0]


def _mod_call(c, ada_w, ada_b):
    depth, d, d6 = ada_w.shape
    b = c.shape[0]
    nblk = d6 // d
    return pl.pallas_call(
        _mod_kernel,
        grid=(depth, nblk),
        in_specs=[
            pl.BlockSpec((b, d), lambda l, j: (0, 0)),
            pl.BlockSpec((1, d, d), lambda l, j: (l, 0, j)),
            pl.BlockSpec((1, 1, d), lambda l, j: (l, 0, j)),
        ],
        out_specs=pl.BlockSpec((1, b, d), lambda l, j: (l, 0, j)),
        out_shape=jax.ShapeDtypeStruct((depth, b, d6), F32),
        compiler_params=_cparams(("parallel", "parallel")),
        name="adaln_mod",
    )(c, ada_w, ada_b.reshape(depth, 1, d6))


def _rope_kernel(pos_ref, freq_ref, sign_ref, cos_ref, sin_ref):
    ang = pos_ref[0].astype(F32) * freq_ref[...]
    cos_ref[0] = jnp.cos(ang)
    sin_ref[0] = jnp.sin(ang) * sign_ref[...]


def _rope_call(positions, d):
    b, t = positions.shape
    tb = min(t, 512)
    inv_freq = ROPE_THETA ** (-jnp.arange(0, d, 2, dtype=F32) / d)
    freq2 = jnp.concatenate([inv_freq, inv_freq]).reshape(1, d)
    sign2 = jnp.concatenate([-jnp.ones((d // 2,), F32), jnp.ones((d // 2,), F32)]).reshape(1, d)
    out = jax.ShapeDtypeStruct((b, t, d), F32)
    return pl.pallas_call(
        _rope_kernel,
        grid=(b, t // tb),
        in_specs=[
            pl.BlockSpec((1, tb, 1), lambda i, j: (i, j, 0)),
            pl.BlockSpec((1, d), lambda i, j: (0, 0)),
            pl.BlockSpec((1, d), lambda i, j: (0, 0)),
        ],
        out_specs=[pl.BlockSpec((1, tb, d), lambda i, j: (i, j, 0))] * 2,
        out_shape=[out, out],
        compiler_params=_cparams(("parallel", "parallel")),
        name="rope_tables",
    )(positions.reshape(b, t, 1), freq2, sign2)


def _inproj_kernel(x_ref, g_ref, scale_ref, shift_ref, w_ref, o_ref, h_ref):
    @pl.when(pl.program_id(1) == 0)
    def _():
        h = _rms_mod(x_ref[...], g_ref[...], scale_ref[0], shift_ref[0])
        h_ref[...] = h.astype(BF16)

    o_ref[...] = _dot(h_ref[...], w_ref[...])


def _inproj_call(x2, gain, mod3, w_bf16, seq, scale_blk, shift_blk, tm=1024, tn=1280):
    n, d = x2.shape
    cols = w_bf16.shape[1]
    tpb = seq // tm
    return pl.pallas_call(
        _inproj_kernel,
        grid=(n // tm, cols // tn),
        in_specs=[
            pl.BlockSpec((tm, d), lambda i, j: (i, 0)),
            pl.BlockSpec((1, d), lambda i, j: (0, 0)),
            pl.BlockSpec((1, 1, d), lambda i, j: (i // tpb, 0, scale_blk)),
            pl.BlockSpec((1, 1, d), lambda i, j: (i // tpb, 0, shift_blk)),
            pl.BlockSpec((d, tn), lambda i, j: (0, j)),
        ],
        out_specs=pl.BlockSpec((tm, tn), lambda i, j: (i, j)),
        out_shape=jax.ShapeDtypeStruct((n, cols), F32),
        scratch_shapes=[pltpu.VMEM((tm, d), BF16)],
        compiler_params=_cparams(("parallel", "arbitrary")),
        name="norm_inproj",
    )(x2, gain.reshape(1, d), mod3, mod3, w_bf16)


def _hgrn2_chunk(zq, zf, zi, zg, lb, nw, st):
    c, sub = HG_CHUNK, HG_SUB
    f = lb + (1.0 - lb) * _sigmoid(zf)
    logf = jnp.log(jnp.maximum(f, 1e-30))
    q = _silu(zq) * (HG_DK ** -0.5)
    k = 1.0 - f
    v = zi
    row = lax.broadcasted_iota(jnp.int32, (c, c), 0)
    col = lax.broadcasted_iota(jnp.int32, (c, c), 1)
    tri = (row >= col).astype(F32)
    cum = _dot(tri, logf, HIGHEST)
    st_b = st.astype(BF16)
    v_b = v.astype(BF16)
    o_inter = _dot_nt((q * jnp.exp(cum)).astype(BF16), st_b)
    trow = lax.broadcasted_iota(jnp.int32, (sub, 1), 0)
    outs = []
    for i in range(c // sub):
        lo = sub * i
        cum_i, q_i, k_i, v_i = cum[lo:lo + sub], q[lo:lo + sub], k[lo:lo + sub], v[lo:lo + sub]
        acc = o_inter[lo:lo + sub]
        if i > 0:
            base = cum[lo - 1:lo]
            qt = (q_i * jnp.exp(cum_i - base)).astype(BF16)
            kt = (k[:lo] * jnp.exp(base - cum[:lo])).astype(BF16)
            a = _dot_nt(qt, kt)
            acc = acc + _dot(a.astype(BF16), v_b[:lo])
        for s in range(sub):
            e = jnp.exp(jnp.minimum(cum_i - cum_i[s:s + 1], 0.0))
            a_col = jnp.sum(q_i * k_i[s:s + 1] * e, axis=-1, keepdims=True)
            a_col = jnp.where(trow >= s, a_col, 0.0)
            acc = acc + a_col * v_i[s:s + 1]
        outs.append(acc)
    o = jnp.concatenate(outs, axis=0)
    last = cum[c - 1:c]
    kd = (k * jnp.exp(last - cum)).astype(BF16)
    st_new = st * jnp.exp(last) + _dot_tn(v_b, kd)
    o = o * lax.rsqrt(jnp.mean(o * o, axis=-1, keepdims=True) + NORM_EPS)
    o = o * nw * _silu(zg)
    return o, st_new


def _hgrn2_kernel(zq_ref, zf_ref, zi_ref, zg_ref, lb_ref, nw_ref, o_ref, st_ref, *, nchunks):
    @pl.when(pl.program_id(2) == 0)
    def _():
        st_ref[...] = jnp.zeros_like(st_ref)

    lb = lb_ref[...]
    nw = nw_ref[...]

    def body(ci, carry):
        r0 = pl.multiple_of(ci * HG_CHUNK, HG_CHUNK)
        rows = pl.ds(r0, HG_CHUNK)
        o, st_new = _hgrn2_chunk(zq_ref[0, rows, :], zf_ref[0, rows, :], zi_ref[0, rows, :],
                                 zg_ref[0, rows, :], lb, nw, st_ref[...])
        st_ref[...] = st_new
        o_ref[0, rows, :] = o.astype(o_ref.dtype)
        return carry

    lax.fori_loop(0, nchunks, body, 0)


def _hgrn2_call(z3, lower_bound, norm_w, tb=256):
    b, t, _ = z3.shape
    tb = min(tb, t)
    base = HG_OFF // LANES

    def zspec(part):
        return pl.BlockSpec((1, tb, LANES), lambda i, h, j: (i, j, base + part * HG_HEADS + h))

    return pl.pallas_call(
        functools.partial(_hgrn2_kernel, nchunks=tb // HG_CHUNK),
        grid=(b, HG_HEADS, t // tb),
        in_specs=[
            zspec(0), zspec(1), zspec(2), zspec(3),
            pl.BlockSpec((1, LANES), lambda i, h, j: (0, h)),
            pl.BlockSpec((1, LANES), lambda i, h, j: (0, 0)),
        ],
        out_specs=pl.BlockSpec((1, tb, LANES), lambda i, h, j: (i, j, h)),
        out_shape=jax.ShapeDtypeStruct((b, t, HG_W), BF16),
        scratch_shapes=[pltpu.VMEM((HG_DK, HG_DK), F32)],
        compiler_params=_cparams(("parallel", "parallel", "arbitrary")),
        name="hgrn2_mixer",
    )(z3, z3, z3, z3, lower_bound.reshape(1, HG_W), norm_w.reshape(1, HG_DK))


def _ret_kernel(zq_ref, zk_ref, zv_ref, zg_ref, cos_ref, sin_ref, o_ref, st_ref, *, chunk):
    h = pl.program_id(1)

    @pl.when(pl.program_id(2) == 0)
    def _():
        st_ref[...] = jnp.zeros_like(st_ref)

    gamma = jnp.float32(0.0)
    for hh in range(RET_HEADS):
        gamma = jnp.where(h == hh, jnp.float32(1.0 - 2.0 ** (-5.0 - hh)), gamma)
    lg = jnp.log(jnp.full((1, 1), gamma, F32))

    cos2 = cos_ref[0]
    sin2 = sin_ref[0]
    half = RET_DK // 2

    def rope(z):
        return z * cos2 + pltpu.roll(z, half, 1) * sin2

    q = rope(zq_ref[0]) * (RET_DK ** -0.5)
    k = rope(zk_ref[0])
    v_b = zv_ref[0].astype(BF16)
    row = lax.broadcasted_iota(jnp.int32, (chunk, chunk), 0)
    col = lax.broadcasted_iota(jnp.int32, (chunk, chunk), 1)
    rel = (row - col).astype(F32)
    dmask = jnp.where(rel >= 0.0, jnp.exp(jnp.maximum(rel, 0.0) * lg), 0.0)
    tcol = lax.broadcasted_iota(jnp.int32, (chunk, 1), 0).astype(F32)
    xi = jnp.exp((tcol + 1.0) * lg)
    zeta = jnp.exp((chunk - 1.0 - tcol) * lg)
    st = st_ref[...]
    scores = _dot_nt(q.astype(BF16), k.astype(BF16)) * dmask
    o = _dot(scores.astype(BF16), v_b) + _dot_nt((q * xi).astype(BF16), st.astype(BF16))
    st_ref[...] = st * jnp.exp(chunk * lg) + _dot_tn(v_b, (k * zeta).astype(BF16))
    o = o * lax.rsqrt(jnp.mean(o * o, axis=-1, keepdims=True) + NORM_EPS)
    o_ref[0] = (o * _silu(zg_ref[0])).astype(o_ref.dtype)


def _ret_call(z3, cos2, sin2, chunk=256):
    b, t, _ = z3.shape
    chunk = min(chunk, t)
    base = RET_OFF // LANES

    def zspec(part):
        return pl.BlockSpec((1, chunk, LANES), lambda i, h, j: (i, j, base + part * RET_HEADS + h))

    tab = pl.BlockSpec((1, chunk, RET_DK), lambda i, h, j: (i, j, 0))
    return pl.pallas_call(
        functools.partial(_ret_kernel, chunk=chunk),
        grid=(b, RET_HEADS, t // chunk),
        in_specs=[zspec(0), zspec(1), zspec(2), zspec(3), tab, tab],
        out_specs=pl.BlockSpec((1, chunk, LANES), lambda i, h, j: (i, j, h)),
        out_shape=jax.ShapeDtypeStruct((b, t, RET_W), BF16),
        scratch_shapes=[pltpu.VMEM((RET_DK, RET_DK), F32)],
        compiler_params=_cparams(("parallel", "parallel", "arbitrary")),
        name="retention_mixer",
    )(z3, z3, z3, z3, cos2, sin2)


def _inv_unit_lower(a, eye, blk_mask, prec):
    a_bd = jnp.where(blk_mask, a, 0.0)
    a_off = a - a_bd
    p = eye + a_bd
    a2 = _dot(a_bd, a_bd, prec)
    p = p + _dot(p, a2, prec)
    a4 = _dot(a2, a2, prec)
    p = p + _dot(p, a4, prec)
    a8 = _dot(a4, a4, prec)
    t_bd = p + _dot(p, a8, prec)
    n = _dot(t_bd, a_off, prec)
    n2 = _dot(n, n, prec)
    z = t_bd + _dot(n2, t_bd, prec)
    return z + _dot(n, z, prec)


def _rwkv_kernel(z_ref, mu_ref, w0_ref, w2_ref, a0_ref, a2_ref, g2_ref, kk_ref, ka_ref, rk_ref,
                 lnw_ref, lnb_ref, seg_ref, o_ref, s_ref, prev_ref):
    c = RW_CHUNK
    prec = HIGHEST

    @pl.when(pl.program_id(1) == 0)
    def _():
        s_ref[...] = jnp.zeros_like(s_ref)
        prev_ref[...] = jnp.zeros_like(prev_ref)

    z = z_ref[0]
    rows = lax.broadcasted_iota(jnp.int32, (c, 1), 0)
    z_prev = jnp.where(rows == 0, prev_ref[...], pltpu.roll(z, 1, 0))
    prev_ref[...] = z[c - 1:c]
    zs = z + mu_ref[...] * (z_prev - z)
    r = zs[:, 0:RW_W]
    k = zs[:, RW_W:2 * RW_W]
    v = zs[:, 2 * RW_W:3 * RW_W]
    off = 3 * RW_W
    w_lo = zs[:, off:off + RW_DECAY_LORA]
    a_lo = zs[:, off + RW_DECAY_LORA:off + RW_DECAY_LORA + RW_A_LORA]
    g_lo = zs[:, off + RW_DECAY_LORA + RW_A_LORA:]

    wx = -(w0_ref[...] + _dot(jnp.tanh(w_lo), w2_ref[...], prec))
    softplus = jnp.maximum(wx, 0.0) + jnp.log(1.0 + jnp.exp(-jnp.abs(wx)))
    logw = -jnp.exp(-softplus - 0.5)
    a = _sigmoid(a0_ref[...] + _dot(a_lo, a2_ref[...], prec))
    g = _dot(_sigmoid(g_lo), g2_ref[...], prec)
    seg = seg_ref[...]
    kk = k * kk_ref[...]
    kk = kk * lax.rsqrt(jnp.maximum(_dot(kk * kk, seg, prec), 1e-24))
    k2 = k * (1.0 + (a - 1.0) * ka_ref[...])

    row = lax.broadcasted_iota(jnp.int32, (c, c), 0)
    col = lax.broadcasted_iota(jnp.int32, (c, c), 1)
    incl = row >= col
    strict = row > col
    blk_mask = (row // RW_BLK) == (col // RW_BLK)
    eye = (row == col).astype(F32)
    cw = _dot(incl.astype(F32), logw, prec)
    w_in = jnp.exp(cw)
    w_ex = jnp.exp(cw - logw)
    w_inv = jnp.exp(-cw)
    last = cw[c - 1:c]
    w_rest = jnp.exp(last - cw)
    w_last = jnp.exp(last)
    beta = a * kk
    alpha_t = -kk * w_ex
    r_t = r * w_in
    beta_h = beta * w_inv
    k_h = k2 * w_inv
    beta_d = beta * w_rest
    k_d = k2 * w_rest

    outs = []
    for h in range(RW_HEADS):
        sl = slice(h * RW_N, (h + 1) * RW_N)
        s0 = s_ref[h]
        v_h = v[:, sl]
        a_ab = jnp.where(strict, _dot_nt(alpha_t[:, sl], beta_h[:, sl], prec), 0.0)
        a_ak = jnp.where(strict, _dot_nt(alpha_t[:, sl], k_h[:, sl], prec), 0.0)
        a_rb = jnp.where(incl, _dot_nt(r_t[:, sl], beta_h[:, sl], prec), 0.0)
        a_rk = jnp.where(incl, _dot_nt(r_t[:, sl], k_h[:, sl], prec), 0.0)
        t_inv = _inv_unit_lower(a_ab, eye, blk_mask, prec)
        u = _dot(t_inv, _dot_nt(alpha_t[:, sl], s0, prec) + _dot(a_ak, v_h, prec), prec)
        o_h = _dot_nt(r_t[:, sl], s0, prec) + _dot(a_rb, u, prec) + _dot(a_rk, v_h, prec)
        s_ref[h] = s0 * w_last[:, sl] + _dot_tn(u, beta_d[:, sl], prec) + _dot_tn(v_h, k_d[:, sl], prec)
        outs.append(o_h)
    o = jnp.concatenate(outs, axis=1)

    mean = _dot(o, seg, prec) * (1.0 / RW_N)
    dev = o - mean
    var = _dot(dev * dev, seg, prec) * (1.0 / RW_N)
    o = dev * lax.rsqrt(var + RW_GN_EPS) * lnw_ref[...] + lnb_ref[...]
    bonus = _dot(r * k2 * rk_ref[...], seg, prec) * v
    o_ref[0] = ((o + bonus) * g).astype(o_ref.dtype)


def _rwkv_call(z3, mu, w0, w2, a0, a2, g2, k_k, k_a, r_k, ln_w, ln_b):
    b, t, _ = z3.shape
    c = RW_CHUNK
    hid = lax.broadcasted_iota(jnp.int32, (RW_W, RW_W), 0) // RW_N
    seg = (hid == hid.T).astype(F32)

    def vec(n):
        return pl.BlockSpec((1, n), lambda i, j: (0, 0))

    def mat(m, n):
        return pl.BlockSpec((m, n), lambda i, j: (0, 0))

    return pl.pallas_call(
        _rwkv_kernel,
        grid=(b, t // c),
        in_specs=[
            pl.BlockSpec((1, c, RW_COLS), lambda i, j: (i, j, RW_OFF // RW_COLS)),
            vec(RW_COLS), vec(RW_W), mat(RW_DECAY_LORA, RW_W), vec(RW_W), mat(RW_A_LORA, RW_W),
            mat(RW_GATE_LORA, RW_W), vec(RW_W), vec(RW_W), vec(RW_W), vec(RW_W), vec(RW_W),
            mat(RW_W, RW_W),
        ],
        out_specs=pl.BlockSpec((1, c, RW_W), lambda i, j: (i, j, 0)),
        out_shape=jax.ShapeDtypeStruct((b, t, RW_W), BF16),
        scratch_shapes=[pltpu.VMEM((RW_HEADS, RW_N, RW_N), F32), pltpu.VMEM((1, RW_COLS), F32)],
        compiler_params=_cparams(("parallel", "arbitrary")),
        name="rwkv7_mixer",
    )(z3, mu.reshape(1, -1), w0.reshape(1, -1), w2, a0.reshape(1, -1), a2, g2, k_k.reshape(1, -1),
      k_a.reshape(1, -1), r_k.reshape(1, -1), ln_w.reshape(1, -1), ln_b.reshape(1, -1), seg)


def _merge_kernel(ohg_ref, oret_ref, orw_ref, zg_ref, x_ref, gate_ref, bhg_ref, bret_ref, brw_ref,
                  wout_ref, o_ref):
    d = x_ref.shape[1]
    y = _sigmoid(zg_ref[:, 0:d]) * _dot(ohg_ref[...], bhg_ref[...])
    y = y + _sigmoid(zg_ref[:, d:2 * d]) * _dot(oret_ref[...], bret_ref[...])
    y = y + _sigmoid(zg_ref[:, 2 * d:3 * d]) * _dot(orw_ref[...], brw_ref[...])
    o_ref[...] = x_ref[...] + gate_ref[0] * _dot(y.astype(BF16), wout_ref[...])


def _merge_call(o_hg, o_ret, o_rw, z2, x2, mod3, br_hg, br_ret, br_rw, w_out, seq, gate_blk, tm=512):
    n, d = x2.shape
    tpb = seq // tm

    def rows(w):
        return pl.BlockSpec((tm, w), lambda i: (i, 0))

    def full(m, k):
        return pl.BlockSpec((m, k), lambda i: (0, 0))

    return pl.pallas_call(
        _merge_kernel,
        grid=(n // tm,),
        in_specs=[
            rows(HG_W), rows(RET_W), rows(RW_W), rows(3 * d), rows(d),
            pl.BlockSpec((1, 1, d), lambda i: (i // tpb, 0, gate_blk)),
            full(HG_W, d), full(RET_W, d), full(RW_W, d), full(d, d),
        ],
        out_specs=rows(d),
        out_shape=jax.ShapeDtypeStruct((n, d), F32),
        compiler_params=_cparams(("parallel",)),
        name="merge_outproj",
    )(o_hg, o_ret, o_rw, z2, x2, mod3, br_hg, br_ret, br_rw, w_out)


def _route_kernel(x_ref, g_ref, scale_ref, shift_ref, rg_ref, re_ref, h_ref, comb_ref):
    h = _rms_mod(x_ref[...], g_ref[...], scale_ref[0], shift_ref[0])
    h_ref[...] = h.astype(BF16)
    tm = h.shape[0]
    lane = lax.broadcasted_iota(jnp.int32, (tm, LANES), 1)
    neg = -jnp.inf
    gl = jnp.where(lane < N_GROUPS, _dot(h, rg_ref[...], HIGHEST), neg)
    gmax = jnp.max(gl, axis=-1, keepdims=True)
    gidx = jnp.min(jnp.where(gl == gmax, lane, LANES), axis=-1, keepdims=True)
    gw = 1.0 / jnp.sum(jnp.exp(gl - gmax), axis=-1, keepdims=True)
    lo = gidx * EXPERTS_PER_GROUP
    el = _dot(h, re_ref[...], HIGHEST)
    el = jnp.where(lane >= lo, jnp.where(lane < lo + EXPERTS_PER_GROUP, el, neg), neg)
    m1 = jnp.max(el, axis=-1, keepdims=True)
    i1 = jnp.min(jnp.where(el == m1, lane, LANES), axis=-1, keepdims=True)
    el2 = jnp.where(lane == i1, neg, el)
    m2 = jnp.max(el2, axis=-1, keepdims=True)
    i2 = jnp.min(jnp.where(el2 == m2, lane, LANES), axis=-1, keepdims=True)
    e2 = jnp.exp(m2 - m1)
    p1 = 1.0 / (1.0 + e2)
    p2 = e2 * p1
    comb_ref[...] = gw * (jnp.where(lane == i1, p1, 0.0) + jnp.where(lane == i2, p2, 0.0))


def _route_call(x2, gain, mod3, router_g, router_e, seq, scale_blk, shift_blk, tm=512):
    n, d = x2.shape
    tpb = seq // tm
    rg = jnp.pad(router_g, ((0, 0), (0, LANES - N_GROUPS)))
    re = jnp.pad(router_e, ((0, 0), (0, LANES - N_EXPERTS)))
    return pl.pallas_call(
        _route_kernel,
        grid=(n // tm,),
        in_specs=[
            pl.BlockSpec((tm, d), lambda i: (i, 0)),
            pl.BlockSpec((1, d), lambda i: (0, 0)),
            pl.BlockSpec((1, 1, d), lambda i: (i // tpb, 0, scale_blk)),
            pl.BlockSpec((1, 1, d), lambda i: (i // tpb, 0, shift_blk)),
            pl.BlockSpec((d, LANES), lambda i: (0, 0)),
            pl.BlockSpec((d, LANES), lambda i: (0, 0)),
        ],
        out_specs=[pl.BlockSpec((tm, d), lambda i: (i, 0)), pl.BlockSpec((tm, LANES), lambda i: (i, 0))],
        out_shape=[jax.ShapeDtypeStruct((n, d), BF16), jax.ShapeDtypeStruct((n, LANES), F32)],
        compiler_params=_cparams(("parallel",)),
        name="moe_route",
    )(x2, gain.reshape(1, d), mod3, mod3, rg, re)


def _experts_kernel(h_ref, comb_ref, x_ref, gate_ref, w1_ref, w3_ref, w2_ref, fg_ref, o_ref, acc_ref, *,
                    final_norm):
    e = pl.program_id(1)

    @pl.when(e == 0)
    def _():
        acc_ref[...] = jnp.zeros_like(acc_ref)

    hb = h_ref[...]
    act = _silu(_dot(hb, w1_ref[0])) * _dot(hb, w3_ref[0])
    y = _dot(act.astype(BF16), w2_ref[0])
    lane = lax.broadcasted_iota(jnp.int32, comb_ref.shape, 1)
    c_e = jnp.sum(jnp.where(lane == e, comb_ref[...], 0.0), axis=-1, keepdims=True)
    acc_ref[...] += c_e * y

    @pl.when(e == pl.num_programs(1) - 1)
    def _():
        xn = x_ref[...] + gate_ref[0] * acc_ref[...]
        if final_norm:
            xn = xn * lax.rsqrt(jnp.mean(xn * xn, axis=-1, keepdims=True) + NORM_EPS) * fg_ref[...]
        o_ref[...] = xn


def _experts_call(h2, comb, x2, mod3, w1, w3, w2, final_g, seq, gate_blk, final_norm, tm=1024):
    n, d = x2.shape
    ne, _, de = w1.shape
    tpb = seq // tm
    return pl.pallas_call(
        functools.partial(_experts_kernel, final_norm=final_norm),
        grid=(n // tm, ne),
        in_specs=[
            pl.BlockSpec((tm, d), lambda i, e: (i, 0)),
            pl.BlockSpec((tm, LANES), lambda i, e: (i, 0)),
            pl.BlockSpec((tm, d), lambda i, e: (i, 0)),
            pl.BlockSpec((1, 1, d), lambda i, e: (i // tpb, 0, gate_blk)),
            pl.BlockSpec((1, d, de), lambda i, e: (e, 0, 0)),
            pl.BlockSpec((1, d, de), lambda i, e: (e, 0, 0)),
            pl.BlockSpec((1, de, d), lambda i, e: (e, 0, 0)),
            pl.BlockSpec((1, d), lambda i, e: (0, 0)),
        ],
        out_specs=pl.BlockSpec((tm, d), lambda i, e: (i, 0)),
        out_shape=jax.ShapeDtypeStruct((n, d), F32),
        scratch_shapes=[pltpu.VMEM((tm, d), F32)],
        compiler_params=_cparams(("parallel", "arbitrary")),
        name="moe_experts",
    )(h2, comb, x2, mod3, w1, w3, w2, final_g.reshape(1, d))


def kernel(x, c, positions, ada_w, ada_b, norm1_g, norm2_g, w_in, hg_lb_table, hg_norm_w, rw_mu, rw_w0, rw_w2,
           rw_a0, rw_a2, rw_g2, rw_k_k, rw_k_a, rw_r_k, rw_ln_w, rw_ln_b, br_hg, br_ret, br_rw, w_out,
           router_g, router_e, moe_w1, moe_w3, moe_w2, final_g):
    b, t, d = x.shape
    depth = ada_w.shape[0]
    n = b * t
    assert w_in.shape[2] == IN_COLS and d == 1024

    lb_p = jax.nn.softmax(hg_lb_table.astype(F32), axis=0)
    lower_bounds = jnp.cumsum(lb_p, axis=0) - lb_p[0]

    mod = _mod_call(c, ada_w, ada_b)
    cos2, sin2 = _rope_call(positions, RET_DK)
    n_gate = 3 * d
    x2 = x.reshape(n, d)
    for l in range(depth):
        mod3 = mod[l].reshape(b, 1, 6 * d)
        w_perm = jnp.concatenate([w_in[l][:, IN_COLS - n_gate:], w_in[l][:, :IN_COLS - n_gate]], axis=1)
        z2 = _inproj_call(x2, norm1_g[l], mod3, w_perm.astype(BF16), t, scale_blk=1, shift_blk=0)
        z3 = z2.reshape(b, t, IN_COLS)
        o_hg = _hgrn2_call(z3, lower_bounds[l], hg_norm_w[l])
        o_ret = _ret_call(z3, cos2, sin2)
        o_rw = _rwkv_call(z3, rw_mu[l], rw_w0[l], rw_w2[l], rw_a0[l], rw_a2[l], rw_g2[l], rw_k_k[l],
                          rw_k_a[l], rw_r_k[l], rw_ln_w[l], rw_ln_b[l])
        x2 = _merge_call(o_hg.reshape(n, HG_W), o_ret.reshape(n, RET_W), o_rw.reshape(n, RW_W), z2, x2, mod3,
                         br_hg[l].astype(BF16), br_ret[l].astype(BF16), br_rw[l].astype(BF16),
                         w_out[l].astype(BF16), t, gate_blk=2)
        h2, comb = _route_call(x2, norm2_g[l], mod3, router_g[l], router_e[l], t, scale_blk=4, shift_blk=3)
        x2 = _experts_call(h2, comb, x2, mod3, moe_w1[l].astype(BF16), moe_w3[l].astype(BF16),
                           moe_w2[l].astype(BF16), final_g, t, gate_blk=5, final_norm=(l == depth - 1))
    return x2.reshape(b, t, d)
```

```python
import functools

import jax
import jax.numpy as jnp
from jax import lax
from jax.experimental import pallas as pl
from jax.experimental.pallas import tpu as pltpu

F32 = jnp.float32
BF16 = jnp.bfloat16
HIGHEST = lax.Precision.HIGHEST

HG_HEADS = 4
HG_DK = 128
HG_W = HG_HEADS * HG_DK
RET_HEADS = 4
RET_DK = 128
RET_W = RET_HEADS * RET_DK
RW_HEADS = 8
RW_N = 64
RW_W = RW_HEADS * RW_N
RW_DECAY_LORA = 64
RW_A_LORA = 64
RW_GATE_LORA = 128
RW_COLS = 3 * RW_W + RW_DECAY_LORA + RW_A_LORA + RW_GATE_LORA
RW_GN_EPS = 64e-5
N_GROUPS = 4
EXPERTS_PER_GROUP = 8
N_EXPERTS = N_GROUPS * EXPERTS_PER_GROUP
ROPE_THETA = 10000.0
NORM_EPS = 1e-6

LANES = 128
VMEM_LIMIT = 56 * 1024 * 1024

GATE_OFF = 0
HG_OFF = 3 * 1024
RET_OFF = HG_OFF + 4 * HG_W
RW_OFF = RET_OFF + 4 * RET_W
IN_COLS = RW_OFF + RW_COLS

HG_CHUNK = 64
HG_SUB = 16
RW_CHUNK = 64
RW_BLK = 16


def _cparams(sem):
    return pltpu.CompilerParams(dimension_semantics=sem, vmem_limit_bytes=VMEM_LIMIT)


def _dot(a, b, precision=None):
    return jnp.dot(a, b, preferred_element_type=F32, precision=precision)


def _dot_nt(a, b, precision=None):
    return lax.dot_general(a, b, (((1,), (1,)), ((), ())), preferred_element_type=F32, precision=precision)


def _dot_tn(a, b, precision=None):
    return lax.dot_general(a, b, (((0,), (0,)), ((), ())), preferred_element_type=F32, precision=precision)


def _sigmoid(x):
    return 1.0 / (1.0 + jnp.exp(-x))


def _silu(x):
    return x * _sigmoid(x)


def _rms_mod(x, gain, scale, shift):
    y = x * lax.rsqrt(jnp.mean(x * x, axis=-1, keepdims=True) + NORM_EPS)
    return (y * gain) * (1.0 + scale) + shift


def _mod_kernel(c_ref, w_ref, b_ref, o_ref):
    c = c_ref[...]
    o_ref[0] = _dot(_silu(c), w_ref[0], HIGHEST) + b_ref[0]


def _mod_call(c, ada_w, ada_b):
    depth, d, d6 = ada_w.shape
    b = c.shape[0]
    nblk = d6 // d
    return pl.pallas_call(
        _mod_kernel,
        grid=(depth, nblk),
        in_specs=[
            pl.BlockSpec((b, d), lambda l, j: (0, 0)),
            pl.BlockSpec((1, d, d), lambda l, j: (l, 0, j)),
            pl.BlockSpec((1, 1, d), lambda l, j: (l, 0, j)),
        ],
        out_specs=pl.BlockSpec((1, b, d), lambda l, j: (l, 0, j)),
        out_shape=jax.ShapeDtypeStruct((depth, b, d6), F32),
        compiler_params=_cparams(("parallel", "parallel")),
        name="adaln_mod",
    )(c, ada_w, ada_b.reshape(depth, 1, d6))


def _rope_kernel(pos_ref, freq_ref, sign_ref, cos_ref, sin_ref):
    ang = pos_ref[0].astype(F32) * freq_ref[...]
    cos_ref[0] = jnp.cos(ang)
    sin_ref[0] = jnp.sin(ang) * sign_ref[...]


def _rope_call(positions, d):
    b, t = positions.shape
    tb = min(t, 512)
    inv_freq = ROPE_THETA ** (-jnp.arange(0, d, 2, dtype=F32) / d)
    freq2 = jnp.concatenate([inv_freq, inv_freq]).reshape(1, d)
    sign2 = jnp.concatenate([-jnp.ones((d // 2,), F32), jnp.ones((d // 2,), F32)]).reshape(1, d)
    out = jax.ShapeDtypeStruct((b, t, d), F32)
    return pl.pallas_call(
        _rope_kernel,
        grid=(b, t // tb),
        in_specs=[
            pl.BlockSpec((1, tb, 1), lambda i, j: (i, j, 0)),
            pl.BlockSpec((1, d), lambda i, j: (0, 0)),
            pl.BlockSpec((1, d), lambda i, j: (0, 0)),
        ],
        out_specs=[pl.BlockSpec((1, tb, d), lambda i, j: (i, j, 0))] * 2,
        out_shape=[out, out],
        compiler_params=_cparams(("parallel", "parallel")),
        name="rope_tables",
    )(positions.reshape(b, t, 1), freq2, sign2)


def _inproj_kernel(x_ref, g_ref, scale_ref, shift_ref, w_ref, o_ref, h_ref):
    @pl.when(pl.program_id(1) == 0)
    def _():
        h = _rms_mod(x_ref[...], g_ref[...], scale_ref[0], shift_ref[0])
        h_ref[...] = h.astype(BF16)

    o_ref[...] = _dot(h_ref[...], w_ref[...])


def _inproj_call(x2, gain, mod3, w_bf16, seq, scale_blk, shift_blk, tm=1024, tn=1280):
    n, d = x2.shape
    cols = w_bf16.shape[1]
    tpb = seq // tm
    return pl.pallas_call(
        _inproj_kernel,
        grid=(n // tm, cols // tn),
        in_specs=[
            pl.BlockSpec((tm, d), lambda i, j: (i, 0)),
            pl.BlockSpec((1, d), lambda i, j: (0, 0)),
            pl.BlockSpec((1, 1, d), lambda i, j: (i // tpb, 0, scale_blk)),
            pl.BlockSpec((1, 1, d), lambda i, j: (i // tpb, 0, shift_blk)),
            pl.BlockSpec((d, tn), lambda i, j: (0, j)),
        ],
        out_specs=pl.BlockSpec((tm, tn), lambda i, j: (i, j)),
        out_shape=jax.ShapeDtypeStruct((n, cols), F32),
        scratch_shapes=[pltpu.VMEM((tm, d), BF16)],
        compiler_params=_cparams(("parallel", "arbitrary")),
        name="norm_inproj",
    )(x2, gain.reshape(1, d), mod3, mod3, w_bf16)


def _hgrn2_block(zq, zf, zi, zg, lb, nw, st):
    tb = zq.shape[0]
    c, sub = HG_CHUNK, HG_SUB
    nc, ns = tb // c, c // sub
    f = lb + (1.0 - lb) * _sigmoid(zf)
    logf = jnp.log(jnp.maximum(f, 1e-30))
    q = _silu(zq) * (HG_DK ** -0.5)
    k = 1.0 - f
    v = zi
    v_b = v.astype(BF16)
    row = lax.broadcasted_iota(jnp.int32, (tb, tb), 0)
    col = lax.broadcasted_iota(jnp.int32, (tb, tb), 1)
    tri = jnp.where(col >= (row // c) * c, jnp.where(row >= col, 1.0, 0.0), 0.0).astype(BF16)
    cum = _dot_x2_rhs(tri, logf)
    qe = (q * jnp.exp(cum)).astype(BF16)

    offd = [(ci * c, ci * c + sub * i) for ci in range(nc) for i in range(1, ns)]
    base = [cum[lo - 1:lo] for _, lo in offd]
    qt = [(q[lo:lo + sub] * jnp.exp(cum[lo:lo + sub] - base[j])).astype(BF16) for j, (_, lo) in enumerate(offd)]
    kt = [(k[r0:lo] * jnp.exp(base[j] - cum[r0:lo])).astype(BF16) for j, (r0, lo) in enumerate(offd)]
    a = [_dot_nt(qt[j], kt[j]).astype(BF16) for j in range(len(offd))]
    av = {lo: _dot(a[j], v_b[r0:lo]) for j, (r0, lo) in enumerate(offd)}

    nb = tb // sub
    cum3, q3, k3, v3 = (x.reshape(nb, sub, HG_DK) for x in (cum, q, k, v))
    trow = lax.broadcasted_iota(jnp.int32, (nb, sub, 1), 1)
    diag = jnp.zeros((nb, sub, HG_DK), F32)
    for s in range(sub):
        e = jnp.exp(jnp.minimum(cum3 - cum3[:, s:s + 1, :], 0.0))
        a_col = jnp.sum(q3 * k3[:, s:s + 1, :] * e, axis=-1, keepdims=True)
        diag = diag + jnp.where(trow >= s, a_col, 0.0) * v3[:, s:s + 1, :]
    diag = diag.reshape(tb, HG_DK)

    outs = []
    for ci in range(nc):
        r0 = ci * c
        o_inter = _dot_nt(qe[r0:r0 + c], st.astype(BF16))
        for i in range(ns):
            lo = r0 + sub * i
            piece = o_inter[sub * i:sub * (i + 1)] + diag[lo:lo + sub]
            outs.append(piece + av[lo] if i > 0 else piece)
        last = cum[r0 + c - 1:r0 + c]
        kd = (k[r0:r0 + c] * jnp.exp(last - cum[r0:r0 + c])).astype(BF16)
        st = st * jnp.exp(last) + _dot_tn(v_b[r0:r0 + c], kd)
    o = jnp.concatenate(outs, axis=0)
    o = o * lax.rsqrt(jnp.mean(o * o, axis=-1, keepdims=True) + NORM_EPS)
    o = o * nw * _silu(zg)
    return o, st


def _hgrn2_kernel(zq_ref, zf_ref, zi_ref, zg_ref, lb_ref, nw_ref, o_ref, st_ref):
    @pl.when(pl.program_id(2) == 0)
    def _():
        st_ref[...] = jnp.zeros_like(st_ref)

    o, st_new = _hgrn2_block(zq_ref[0], zf_ref[0], zi_ref[0], zg_ref[0], lb_ref[...], nw_ref[...], st_ref[...])
    st_ref[...] = st_new
    o_ref[0] = o.astype(o_ref.dtype)


def _hgrn2_call(z3, lower_bound, norm_w, tb=256):
    b, t, _ = z3.shape
    tb = min(tb, t)
    base = HG_OFF // LANES

    def zspec(part):
        return pl.BlockSpec((1, tb, LANES), lambda i, h, j: (i, j, base + part * HG_HEADS + h))

    return pl.pallas_call(
        _hgrn2_kernel,
        grid=(b, HG_HEADS, t // tb),
        in_specs=[
            zspec(0), zspec(1), zspec(2), zspec(3),
            pl.BlockSpec((1, LANES), lambda i, h, j: (0, h)),
            pl.BlockSpec((1, LANES), lambda i, h, j: (0, 0)),
        ],
        out_specs=pl.BlockSpec((1, tb, LANES), lambda i, h, j: (i, j, h)),
        out_shape=jax.ShapeDtypeStruct((b, t, HG_W), BF16),
        scratch_shapes=[pltpu.VMEM((HG_DK, HG_DK), F32)],
        compiler_params=_cparams(("parallel", "parallel", "arbitrary")),
        name="hgrn2_mixer",
    )(z3, z3, z3, z3, lower_bound.reshape(1, HG_W), norm_w.reshape(1, HG_DK))


def _ret_kernel(zq_ref, zk_ref, zv_ref, zg_ref, cos_ref, sin_ref, o_ref, st_ref, *, chunk):
    h = pl.program_id(1)

    @pl.when(pl.program_id(2) == 0)
    def _():
        st_ref[...] = jnp.zeros_like(st_ref)

    gamma = jnp.float32(0.0)
    for hh in range(RET_HEADS):
        gamma = jnp.where(h == hh, jnp.float32(1.0 - 2.0 ** (-5.0 - hh)), gamma)
    lg = jnp.log(jnp.full((1, 1), gamma, F32))

    cos2 = cos_ref[0]
    sin2 = sin_ref[0]
    half = RET_DK // 2

    def rope(z):
        return z * cos2 + pltpu.roll(z, half, 1) * sin2

    q = rope(zq_ref[0]) * (RET_DK ** -0.5)
    k = rope(zk_ref[0])
    v_b = zv_ref[0].astype(BF16)
    row = lax.broadcasted_iota(jnp.int32, (chunk, chunk), 0)
    col = lax.broadcasted_iota(jnp.int32, (chunk, chunk), 1)
    rel = (row - col).astype(F32)
    dmask = jnp.where(rel >= 0.0, jnp.exp(jnp.maximum(rel, 0.0) * lg), 0.0)
    tcol = lax.broadcasted_iota(jnp.int32, (chunk, 1), 0).astype(F32)
    xi = jnp.exp((tcol + 1.0) * lg)
    zeta = jnp.exp((chunk - 1.0 - tcol) * lg)
    st = st_ref[...]
    scores = _dot_nt(q.astype(BF16), k.astype(BF16)) * dmask
    o = _dot(scores.astype(BF16), v_b) + _dot_nt((q * xi).astype(BF16), st.astype(BF16))
    st_ref[...] = st * jnp.exp(chunk * lg) + _dot_tn(v_b, (k * zeta).astype(BF16))
    o = o * lax.rsqrt(jnp.mean(o * o, axis=-1, keepdims=True) + NORM_EPS)
    o_ref[0] = (o * _silu(zg_ref[0])).astype(o_ref.dtype)


def _ret_call(z3, cos2, sin2, chunk=256):
    b, t, _ = z3.shape
    chunk = min(chunk, t)
    base = RET_OFF // LANES

    def zspec(part):
        return pl.BlockSpec((1, chunk, LANES), lambda i, h, j: (i, j, base + part * RET_HEADS + h))

    tab = pl.BlockSpec((1, chunk, RET_DK), lambda i, h, j: (i, j, 0))
    return pl.pallas_call(
        functools.partial(_ret_kernel, chunk=chunk),
        grid=(b, RET_HEADS, t // chunk),
        in_specs=[zspec(0), zspec(1), zspec(2), zspec(3), tab, tab],
        out_specs=pl.BlockSpec((1, chunk, LANES), lambda i, h, j: (i, j, h)),
        out_shape=jax.ShapeDtypeStruct((b, t, RET_W), BF16),
        scratch_shapes=[pltpu.VMEM((RET_DK, RET_DK), F32)],
        compiler_params=_cparams(("parallel", "parallel", "arbitrary")),
        name="retention_mixer",
    )(z3, z3, z3, z3, cos2, sin2)


def _split_bf16(x):
    hi = x.astype(BF16)
    return hi, (x - hi.astype(F32)).astype(BF16)


def _dot_x3(a, b):
    ah, al = _split_bf16(a)
    bh, bl = _split_bf16(b)
    return _dot(ah, bh) + _dot(ah, bl) + _dot(al, bh)


def _dot_x2_lhs(a, b_exact):
    ah, al = _split_bf16(a)
    return _dot(ah, b_exact) + _dot(al, b_exact)


def _dot_x2_rhs(a_exact, b):
    bh, bl = _split_bf16(b)
    return _dot(a_exact, bh) + _dot(a_exact, bl)


def _bdot(a, b):
    return _dot(a.astype(BF16), b.astype(BF16))


def _inv_unit_lower(a, eye, blk_mask):
    c = a[0].shape[0]
    m = range(len(a))
    a_bd = [jnp.where(blk_mask, a[i], 0.0) for i in m]
    a_off = [a[i] - a_bd[i] for i in m]
    a2 = [_bdot(a_bd[i], a_bd[i]) for i in m]
    p = [eye + a_bd[i] for i in m]
    r = [_bdot(jnp.concatenate([p[i], a2[i]], axis=0), a2[i]) for i in m]
    p = [p[i] + r[i][:c] for i in m]
    a4 = [r[i][c:] for i in m]
    r = [_bdot(jnp.concatenate([p[i], a4[i]], axis=0), a4[i]) for i in m]
    p = [p[i] + r[i][:c] for i in m]
    a8 = [r[i][c:] for i in m]
    t_bd = [p[i] + _bdot(p[i], a8[i]) for i in m]
    n = [_bdot(t_bd[i], a_off[i]) for i in m]
    r = [_bdot(n[i], jnp.concatenate([n[i], t_bd[i]], axis=1)) for i in m]
    z = [t_bd[i] + r[i][:, c:] for i in m]
    return [z[i] + _bdot(r[i][:, :c], z[i]) for i in m]


def _rwkv_kernel(z_ref, mu_ref, w0_ref, w2_ref, a0_ref, a2_ref, g2_ref, kk_ref, ka_ref, rk_ref,
                 lnw_ref, lnb_ref, seg_ref, o_ref, s_ref, prev_ref):
    c = RW_CHUNK

    @pl.when(pl.program_id(1) == 0)
    def _():
        s_ref[...] = jnp.zeros_like(s_ref)
        prev_ref[...] = jnp.zeros_like(prev_ref)

    z = z_ref[0]
    rows = lax.broadcasted_iota(jnp.int32, (c, 1), 0)
    z_prev = jnp.where(rows == 0, prev_ref[...], pltpu.roll(z, 1, 0))
    prev_ref[...] = z[c - 1:c]
    zs = z + mu_ref[...] * (z_prev - z)
    r = zs[:, 0:RW_W]
    k = zs[:, RW_W:2 * RW_W]
    v = zs[:, 2 * RW_W:3 * RW_W]
    off = 3 * RW_W
    w_lo = zs[:, off:off + RW_DECAY_LORA]
    a_lo = zs[:, off + RW_DECAY_LORA:off + RW_DECAY_LORA + RW_A_LORA]
    g_lo = zs[:, off + RW_DECAY_LORA + RW_A_LORA:]

    wx = -(w0_ref[...] + _dot_x3(jnp.tanh(w_lo), w2_ref[...]))
    softplus = jnp.maximum(wx, 0.0) + jnp.log(1.0 + jnp.exp(-jnp.abs(wx)))
    logw = -jnp.exp(-softplus - 0.5)
    a = _sigmoid(a0_ref[...] + _dot_x3(a_lo, a2_ref[...]))
    g = _dot_x3(_sigmoid(g_lo), g2_ref[...])
    seg = seg_ref[...]
    kk = k * kk_ref[...]
    kk = kk * lax.rsqrt(jnp.maximum(_dot_x2_lhs(kk * kk, seg), 1e-24))
    k2 = k * (1.0 + (a - 1.0) * ka_ref[...])

    row = lax.broadcasted_iota(jnp.int32, (c, c), 0)
    col = lax.broadcasted_iota(jnp.int32, (c, c), 1)
    incl = row >= col
    strict = row > col
    blk_mask = (row // RW_BLK) == (col // RW_BLK)
    eye = (row == col).astype(F32)
    row2 = lax.broadcasted_iota(jnp.int32, (c, 2 * c), 0)
    col2 = lax.broadcasted_iota(jnp.int32, (c, 2 * c), 1) % c
    incl2 = row2 >= col2
    strict2 = row2 > col2
    cw = _dot_x2_rhs(incl.astype(BF16), logw)
    w_inv = jnp.exp(-cw)
    last = cw[c - 1:c]
    w_rest = jnp.exp(last - cw)
    w_last = jnp.exp(last)
    beta = a * kk
    alpha_t = -kk * jnp.exp(cw - logw)
    r_t = r * jnp.exp(cw)
    beta_h = beta * w_inv
    k_h = k2 * w_inv
    beta_d = beta * w_rest
    k_d = k2 * w_rest

    hs = range(RW_HEADS)
    sl = [slice(h * RW_N, (h + 1) * RW_N) for h in hs]
    s0 = [s_ref[h] for h in hs]
    v_h = [v[:, sl[h]] for h in hs]
    lhs = [jnp.concatenate([alpha_t[:, sl[h]], r_t[:, sl[h]]], axis=0).astype(BF16) for h in hs]
    rhs = [jnp.concatenate([beta_h[:, sl[h]], k_h[:, sl[h]]], axis=0).astype(BF16) for h in hs]
    big = [_dot_nt(lhs[h], rhs[h]) for h in hs]
    a_a = [jnp.where(strict2, big[h][:c], 0.0) for h in hs]
    a_r = [jnp.where(incl2, big[h][c:], 0.0).astype(BF16) for h in hs]
    t_inv = _inv_unit_lower([a_a[h][:, :c] for h in hs], eye, blk_mask)
    sd = [_dot_nt(lhs[h], s0[h].astype(BF16)) for h in hs]
    b = [sd[h][:c] + _bdot(a_a[h][:, c:], v_h[h]) for h in hs]
    u = [_bdot(t_inv[h], b[h]) for h in hs]
    uv = [jnp.concatenate([u[h], v_h[h]], axis=0).astype(BF16) for h in hs]
    outs = [sd[h][c:] + _dot(a_r[h], uv[h]) for h in hs]
    bk_d = [jnp.concatenate([beta_d[:, sl[h]], k_d[:, sl[h]]], axis=0).astype(BF16) for h in hs]
    for h in hs:
        s_ref[h] = s0[h] * w_last[:, sl[h]] + _dot_tn(uv[h], bk_d[h])
    o = jnp.concatenate(outs, axis=1)

    mean = _dot_x2_lhs(o, seg) * (1.0 / RW_N)
    dev = o - mean
    var = _dot_x2_lhs(dev * dev, seg) * (1.0 / RW_N)
    o = dev * lax.rsqrt(var + RW_GN_EPS) * lnw_ref[...] + lnb_ref[...]
    bonus = _dot_x2_lhs(r * k2 * rk_ref[...], seg) * v
    o_ref[0] = ((o + bonus) * g).astype(o_ref.dtype)


def _rwkv_call(z3, mu, w0, w2, a0, a2, g2, k_k, k_a, r_k, ln_w, ln_b):
    b, t, _ = z3.shape
    c = RW_CHUNK
    hid = lax.broadcasted_iota(jnp.int32, (RW_W, RW_W), 0) // RW_N
    seg = (hid == hid.T).astype(BF16)

    def vec(n):
        return pl.BlockSpec((1, n), lambda i, j: (0, 0))

    def mat(m, n):
        return pl.BlockSpec((m, n), lambda i, j: (0, 0))

    return pl.pallas_call(
        _rwkv_kernel,
        grid=(b, t // c),
        in_specs=[
            pl.BlockSpec((1, c, RW_COLS), lambda i, j: (i, j, RW_OFF // RW_COLS)),
            vec(RW_COLS), vec(RW_W), mat(RW_DECAY_LORA, RW_W), vec(RW_W), mat(RW_A_LORA, RW_W),
            mat(RW_GATE_LORA, RW_W), vec(RW_W), vec(RW_W), vec(RW_W), vec(RW_W), vec(RW_W),
            mat(RW_W, RW_W),
        ],
        out_specs=pl.BlockSpec((1, c, RW_W), lambda i, j: (i, j, 0)),
        out_shape=jax.ShapeDtypeStruct((b, t, RW_W), BF16),
        scratch_shapes=[pltpu.VMEM((RW_HEADS, RW_N, RW_N), F32), pltpu.VMEM((1, RW_COLS), F32)],
        compiler_params=_cparams(("parallel", "arbitrary")),
        name="rwkv7_mixer",
    )(z3, mu.reshape(1, -1), w0.reshape(1, -1), w2, a0.reshape(1, -1), a2, g2, k_k.reshape(1, -1),
      k_a.reshape(1, -1), r_k.reshape(1, -1), ln_w.reshape(1, -1), ln_b.reshape(1, -1), seg)


def _merge_kernel(ohg_ref, oret_ref, orw_ref, zg_ref, x_ref, gate_ref, bhg_ref, bret_ref, brw_ref,
                  wout_ref, o_ref):
    d = x_ref.shape[1]
    y = _sigmoid(zg_ref[:, 0:d]) * _dot(ohg_ref[...], bhg_ref[...])
    y = y + _sigmoid(zg_ref[:, d:2 * d]) * _dot(oret_ref[...], bret_ref[...])
    y = y + _sigmoid(zg_ref[:, 2 * d:3 * d]) * _dot(orw_ref[...], brw_ref[...])
    o_ref[...] = x_ref[...] + gate_ref[0] * _dot(y.astype(BF16), wout_ref[...])


def _merge_call(o_hg, o_ret, o_rw, z2, x2, mod3, br_hg, br_ret, br_rw, w_out, seq, gate_blk, tm=512):
    n, d = x2.shape
    tpb = seq // tm

    def rows(w):
        return pl.BlockSpec((tm, w), lambda i: (i, 0))

    def full(m, k):
        return pl.BlockSpec((m, k), lambda i: (0, 0))

    return pl.pallas_call(
        _merge_kernel,
        grid=(n // tm,),
        in_specs=[
            rows(HG_W), rows(RET_W), rows(RW_W), rows(3 * d), rows(d),
            pl.BlockSpec((1, 1, d), lambda i: (i // tpb, 0, gate_blk)),
            full(HG_W, d), full(RET_W, d), full(RW_W, d), full(d, d),
        ],
        out_specs=rows(d),
        out_shape=jax.ShapeDtypeStruct((n, d), F32),
        compiler_params=_cparams(("parallel",)),
        name="merge_outproj",
    )(o_hg, o_ret, o_rw, z2, x2, mod3, br_hg, br_ret, br_rw, w_out)


def _route_kernel(x_ref, g_ref, scale_ref, shift_ref, rg_ref, re_ref, h_ref, comb_ref):
    h = _rms_mod(x_ref[...], g_ref[...], scale_ref[0], shift_ref[0])
    h_ref[...] = h.astype(BF16)
    tm = h.shape[0]
    lane = lax.broadcasted_iota(jnp.int32, (tm, LANES), 1)
    neg = -jnp.inf
    gl = jnp.where(lane < N_GROUPS, _dot(h, rg_ref[...], HIGHEST), neg)
    gmax = jnp.max(gl, axis=-1, keepdims=True)
    gidx = jnp.min(jnp.where(gl == gmax, lane, LANES), axis=-1, keepdims=True)
    gw = 1.0 / jnp.sum(jnp.exp(gl - gmax), axis=-1, keepdims=True)
    lo = gidx * EXPERTS_PER_GROUP
    el = _dot(h, re_ref[...], HIGHEST)
    el = jnp.where(lane >= lo, jnp.where(lane < lo + EXPERTS_PER_GROUP, el, neg), neg)
    m1 = jnp.max(el, axis=-1, keepdims=True)
    i1 = jnp.min(jnp.where(el == m1, lane, LANES), axis=-1, keepdims=True)
    el2 = jnp.where(lane == i1, neg, el)
    m2 = jnp.max(el2, axis=-1, keepdims=True)
    i2 = jnp.min(jnp.where(el2 == m2, lane, LANES), axis=-1, keepdims=True)
    e2 = jnp.exp(m2 - m1)
    p1 = 1.0 / (1.0 + e2)
    p2 = e2 * p1
    comb_ref[...] = gw * (jnp.where(lane == i1, p1, 0.0) + jnp.where(lane == i2, p2, 0.0))


def _route_call(x2, gain, mod3, router_g, router_e, seq, scale_blk, shift_blk, tm=512):
    n, d = x2.shape
    tpb = seq // tm
    rg = jnp.pad(router_g, ((0, 0), (0, LANES - N_GROUPS)))
    re = jnp.pad(router_e, ((0, 0), (0, LANES - N_EXPERTS)))
    return pl.pallas_call(
        _route_kernel,
        grid=(n // tm,),
        in_specs=[
            pl.BlockSpec((tm, d), lambda i: (i, 0)),
            pl.BlockSpec((1, d), lambda i: (0, 0)),
            pl.BlockSpec((1, 1, d), lambda i: (i // tpb, 0, scale_blk)),
            pl.BlockSpec((1, 1, d), lambda i: (i // tpb, 0, shift_blk)),
            pl.BlockSpec((d, LANES), lambda i: (0, 0)),
            pl.BlockSpec((d, LANES), lambda i: (0, 0)),
        ],
        out_specs=[pl.BlockSpec((tm, d), lambda i: (i, 0)), pl.BlockSpec((tm, LANES), lambda i: (i, 0))],
        out_shape=[jax.ShapeDtypeStruct((n, d), BF16), jax.ShapeDtypeStruct((n, LANES), F32)],
        compiler_params=_cparams(("parallel",)),
        name="moe_route",
    )(x2, gain.reshape(1, d), mod3, mod3, rg, re)


def _experts_kernel(h_ref, comb_ref, x_ref, gate_ref, w1_ref, w3_ref, w2_ref, fg_ref, o_ref, acc_ref, *,
                    final_norm):
    e = pl.program_id(1)

    @pl.when(e == 0)
    def _():
        acc_ref[...] = jnp.zeros_like(acc_ref)

    hb = h_ref[...]
    act = _silu(_dot(hb, w1_ref[0])) * _dot(hb, w3_ref[0])
    y = _dot(act.astype(BF16), w2_ref[0])
    lane = lax.broadcasted_iota(jnp.int32, comb_ref.shape, 1)
    c_e = jnp.sum(jnp.where(lane == e, comb_ref[...], 0.0), axis=-1, keepdims=True)
    acc_ref[...] += c_e * y

    @pl.when(e == pl.num_programs(1) - 1)
    def _():
        xn = x_ref[...] + gate_ref[0] * acc_ref[...]
        if final_norm:
            xn = xn * lax.rsqrt(jnp.mean(xn * xn, axis=-1, keepdims=True) + NORM_EPS) * fg_ref[...]
        o_ref[...] = xn


def _experts_call(h2, comb, x2, mod3, w1, w3, w2, final_g, seq, gate_blk, final_norm, tm=1024):
    n, d = x2.shape
    ne, _, de = w1.shape
    tpb = seq // tm
    return pl.pallas_call(
        functools.partial(_experts_kernel, final_norm=final_norm),
        grid=(n // tm, ne),
        in_specs=[
            pl.BlockSpec((tm, d), lambda i, e: (i, 0)),
            pl.BlockSpec((tm, LANES), lambda i, e: (i, 0)),
            pl.BlockSpec((tm, d), lambda i, e: (i, 0)),
            pl.BlockSpec((1, 1, d), lambda i, e: (i // tpb, 0, gate_blk)),
            pl.BlockSpec((1, d, de), lambda i, e: (e, 0, 0)),
            pl.BlockSpec((1, d, de), lambda i, e: (e, 0, 0)),
            pl.BlockSpec((1, de, d), lambda i, e: (e, 0, 0)),
            pl.BlockSpec((1, d), lambda i, e: (0, 0)),
        ],
        out_specs=pl.BlockSpec((tm, d), lambda i, e: (i, 0)),
        out_shape=jax.ShapeDtypeStruct((n, d), F32),
        scratch_shapes=[pltpu.VMEM((tm, d), F32)],
        compiler_params=_cparams(("parallel", "arbitrary")),
        name="moe_experts",
    )(h2, comb, x2, mod3, w1, w3, w2, final_g.reshape(1, d))


def kernel(x, c, positions, ada_w, ada_b, norm1_g, norm2_g, w_in, hg_lb_table, hg_norm_w, rw_mu, rw_w0, rw_w2,
           rw_a0, rw_a2, rw_g2, rw_k_k, rw_k_a, rw_r_k, rw_ln_w, rw_ln_b, br_hg, br_ret, br_rw, w_out,
           router_g, router_e, moe_w1, moe_w3, moe_w2, final_g):
    b, t, d = x.shape
    depth = ada_w.shape[0]
    n = b * t
    assert w_in.shape[2] == IN_COLS and d == 1024

    lb_p = jax.nn.softmax(hg_lb_table.astype(F32), axis=0)
    lower_bounds = jnp.cumsum(lb_p, axis=0) - lb_p[0]

    mod = _mod_call(c, ada_w, ada_b)
    cos2, sin2 = _rope_call(positions, RET_DK)
    n_gate = 3 * d
    x2 = x.reshape(n, d)
    for l in range(depth):
        mod3 = mod[l].reshape(b, 1, 6 * d)
        w_perm = jnp.concatenate([w_in[l][:, IN_COLS - n_gate:], w_in[l][:, :IN_COLS - n_gate]], axis=1)
        z2 = _inproj_call(x2, norm1_g[l], mod3, w_perm.astype(BF16), t, scale_blk=1, shift_blk=0)
        z3 = z2.reshape(b, t, IN_COLS)
        o_hg = _hgrn2_call(z3, lower_bounds[l], hg_norm_w[l])
        o_ret = _ret_call(z3, cos2, sin2)
        o_rw = _rwkv_call(z3, rw_mu[l], rw_w0[l], rw_w2[l], rw_a0[l], rw_a2[l], rw_g2[l], rw_k_k[l],
                          rw_k_a[l], rw_r_k[l], rw_ln_w[l], rw_ln_b[l])
        x2 = _merge_call(o_hg.reshape(n, HG_W), o_ret.reshape(n, RET_W), o_rw.reshape(n, RW_W), z2, x2, mod3,
                         br_hg[l].astype(BF16), br_ret[l].astype(BF16), br_rw[l].astype(BF16),
                         w_out[l].astype(BF16), t, gate_blk=2)
        h2, comb = _route_call(x2, norm2_g[l], mod3, router_g[l], router_e[l], t, scale_blk=4, shift_blk=3)
        x2 = _experts_call(h2, comb, x2, mod3, moe_w1[l].astype(BF16), moe_w3[l].astype(BF16),
                           moe_w2[l].astype(BF16), final_g, t, gate_blk=5, final_norm=(l == depth - 1))
    return x2.reshape(b, t, d)
```

```python
import functools

import jax
import jax.numpy as jnp
from jax import lax
from jax.experimental import pallas as pl
from jax.experimental.pallas import tpu as pltpu
from jax.experimental.pallas import tpu_sc as plsc

F32 = jnp.float32
BF16 = jnp.bfloat16
HIGHEST = lax.Precision.HIGHEST

HG_HEADS = 4
HG_DK = 128
HG_W = HG_HEADS * HG_DK
RET_HEADS = 4
RET_DK = 128
RET_W = RET_HEADS * RET_DK
RW_HEADS = 8
RW_N = 64
RW_W = RW_HEADS * RW_N
RW_DECAY_LORA = 64
RW_A_LORA = 64
RW_GATE_LORA = 128
RW_COLS = 3 * RW_W + RW_DECAY_LORA + RW_A_LORA + RW_GATE_LORA
RW_GN_EPS = 64e-5
N_GROUPS = 4
EXPERTS_PER_GROUP = 8
N_EXPERTS = N_GROUPS * EXPERTS_PER_GROUP
ROPE_THETA = 10000.0
NORM_EPS = 1e-6

LANES = 128
VMEM_LIMIT = 56 * 1024 * 1024

GATE_OFF = 0
HG_OFF = 3 * 1024
RET_OFF = HG_OFF + 4 * HG_W
RW_OFF = RET_OFF + 4 * RET_W
IN_COLS = RW_OFF + RW_COLS

HG_CHUNK = 64
HG_SUB = 16
RW_CHUNK = 64
RW_BLK = 16


def _cparams(sem):
    return pltpu.CompilerParams(dimension_semantics=sem, vmem_limit_bytes=VMEM_LIMIT)


def _dot(a, b, precision=None):
    return jnp.dot(a, b, preferred_element_type=F32, precision=precision)


def _dot_nt(a, b, precision=None):
    return lax.dot_general(a, b, (((1,), (1,)), ((), ())), preferred_element_type=F32, precision=precision)


def _dot_tn(a, b, precision=None):
    return lax.dot_general(a, b, (((0,), (0,)), ((), ())), preferred_element_type=F32, precision=precision)


def _sigmoid(x):
    return 1.0 / (1.0 + jnp.exp(-x))


def _silu(x):
    return x * _sigmoid(x)


def _rms_mod(x, gain, scale, shift):
    y = x * lax.rsqrt(jnp.mean(x * x, axis=-1, keepdims=True) + NORM_EPS)
    return (y * gain) * (1.0 + scale) + shift


def _mod_kernel(c_ref, w_ref, b_ref, o_ref):
    c = c_ref[...]
    o_ref[0] = _dot(_silu(c), w_ref[0], HIGHEST) + b_ref[0]


def _mod_call(c, ada_w, ada_b):
    depth, d, d6 = ada_w.shape
    b = c.shape[0]
    nblk = d6 // d
    return pl.pallas_call(
        _mod_kernel,
        grid=(depth, nblk),
        in_specs=[
            pl.BlockSpec((b, d), lambda l, j: (0, 0)),
            pl.BlockSpec((1, d, d), lambda l, j: (l, 0, j)),
            pl.BlockSpec((1, 1, d), lambda l, j: (l, 0, j)),
        ],
        out_specs=pl.BlockSpec((1, b, d), lambda l, j: (l, 0, j)),
        out_shape=jax.ShapeDtypeStruct((depth, b, d6), F32),
        compiler_params=_cparams(("parallel", "parallel")),
        name="adaln_mod",
    )(c, ada_w, ada_b.reshape(depth, 1, d6))


def _rope_kernel(pos_ref, freq_ref, sign_ref, cos_ref, sin_ref):
    ang = pos_ref[0].astype(F32) * freq_ref[...]
    cos_ref[0] = jnp.cos(ang)
    sin_ref[0] = jnp.sin(ang) * sign_ref[...]


def _rope_call(positions, d):
    b, t = positions.shape
    tb = min(t, 512)
    inv_freq = ROPE_THETA ** (-jnp.arange(0, d, 2, dtype=F32) / d)
    freq2 = jnp.concatenate([inv_freq, inv_freq]).reshape(1, d)
    sign2 = jnp.concatenate([-jnp.ones((d // 2,), F32), jnp.ones((d // 2,), F32)]).reshape(1, d)
    out = jax.ShapeDtypeStruct((b, t, d), F32)
    return pl.pallas_call(
        _rope_kernel,
        grid=(b, t // tb),
        in_specs=[
            pl.BlockSpec((1, tb, 1), lambda i, j: (i, j, 0)),
            pl.BlockSpec((1, d), lambda i, j: (0, 0)),
            pl.BlockSpec((1, d), lambda i, j: (0, 0)),
        ],
        out_specs=[pl.BlockSpec((1, tb, d), lambda i, j: (i, j, 0))] * 2,
        out_shape=[out, out],
        compiler_params=_cparams(("parallel", "parallel")),
        name="rope_tables",
    )(positions.reshape(b, t, 1), freq2, sign2)


def _inproj_kernel(x_ref, g_ref, scale_ref, shift_ref, w_ref, o_ref, h_ref):
    @pl.when(pl.program_id(1) == 0)
    def _():
        h = _rms_mod(x_ref[...], g_ref[...], scale_ref[0], shift_ref[0])
        h_ref[...] = h.astype(BF16)

    o_ref[...] = _dot(h_ref[...], w_ref[...])


def _inproj_call(x2, gain, mod3, w_bf16, seq, scale_blk, shift_blk, tm=1024, tn=1280):
    n, d = x2.shape
    cols = w_bf16.shape[1]
    tpb = seq // tm
    return pl.pallas_call(
        _inproj_kernel,
        grid=(n // tm, cols // tn),
        in_specs=[
            pl.BlockSpec((tm, d), lambda i, j: (i, 0)),
            pl.BlockSpec((1, d), lambda i, j: (0, 0)),
            pl.BlockSpec((1, 1, d), lambda i, j: (i // tpb, 0, scale_blk)),
            pl.BlockSpec((1, 1, d), lambda i, j: (i // tpb, 0, shift_blk)),
            pl.BlockSpec((d, tn), lambda i, j: (0, j)),
        ],
        out_specs=pl.BlockSpec((tm, tn), lambda i, j: (i, j)),
        out_shape=jax.ShapeDtypeStruct((n, cols), F32),
        scratch_shapes=[pltpu.VMEM((tm, d), BF16)],
        compiler_params=_cparams(("parallel", "arbitrary")),
        name="norm_inproj",
    )(x2, gain.reshape(1, d), mod3, mod3, w_bf16)


def _hgrn2_block(zq, zf, zi, zg, lb, nw, st):
    tb = zq.shape[0]
    c, sub = HG_CHUNK, HG_SUB
    nc, ns = tb // c, c // sub
    f = lb + (1.0 - lb) * _sigmoid(zf)
    logf = jnp.log(jnp.maximum(f, 1e-30))
    q = _silu(zq) * (HG_DK ** -0.5)
    k = 1.0 - f
    v = zi
    v_b = v.astype(BF16)
    row = lax.broadcasted_iota(jnp.int32, (tb, tb), 0)
    col = lax.broadcasted_iota(jnp.int32, (tb, tb), 1)
    tri = jnp.where(col >= (row // c) * c, jnp.where(row >= col, 1.0, 0.0), 0.0).astype(BF16)
    cum = _dot_x2_rhs(tri, logf)
    qe = (q * jnp.exp(cum)).astype(BF16)

    offd = [(ci * c, ci * c + sub * i) for ci in range(nc) for i in range(1, ns)]
    base = [cum[lo - 1:lo] for _, lo in offd]
    qt = [(q[lo:lo + sub] * jnp.exp(cum[lo:lo + sub] - base[j])).astype(BF16) for j, (_, lo) in enumerate(offd)]
    kt = [(k[r0:lo] * jnp.exp(base[j] - cum[r0:lo])).astype(BF16) for j, (r0, lo) in enumerate(offd)]
    a = [_dot_nt(qt[j], kt[j]).astype(BF16) for j in range(len(offd))]
    av = {lo: _dot(a[j], v_b[r0:lo]) for j, (r0, lo) in enumerate(offd)}

    nb = tb // sub
    cum3, q3, k3, v3 = (x.reshape(nb, sub, HG_DK) for x in (cum, q, k, v))
    trow = lax.broadcasted_iota(jnp.int32, (nb, sub, 1), 1)
    diag = jnp.zeros((nb, sub, HG_DK), F32)
    for s in range(sub):
        e = jnp.exp(jnp.minimum(cum3 - cum3[:, s:s + 1, :], 0.0))
        a_col = jnp.sum(q3 * k3[:, s:s + 1, :] * e, axis=-1, keepdims=True)
        diag = diag + jnp.where(trow >= s, a_col, 0.0) * v3[:, s:s + 1, :]
    diag = diag.reshape(tb, HG_DK)

    outs = []
    for ci in range(nc):
        r0 = ci * c
        o_inter = _dot_nt(qe[r0:r0 + c], st.astype(BF16))
        for i in range(ns):
            lo = r0 + sub * i
            piece = o_inter[sub * i:sub * (i + 1)] + diag[lo:lo + sub]
            outs.append(piece + av[lo] if i > 0 else piece)
        last = cum[r0 + c - 1:r0 + c]
        kd = (k[r0:r0 + c] * jnp.exp(last - cum[r0:r0 + c])).astype(BF16)
        st = st * jnp.exp(last) + _dot_tn(v_b[r0:r0 + c], kd)
    o = jnp.concatenate(outs, axis=0)
    o = o * lax.rsqrt(jnp.mean(o * o, axis=-1, keepdims=True) + NORM_EPS)
    o = o * nw * _silu(zg)
    return o, st


def _hgrn2_kernel(zq_ref, zf_ref, zi_ref, zg_ref, lb_ref, nw_ref, o_ref, st_ref):
    @pl.when(pl.program_id(2) == 0)
    def _():
        st_ref[...] = jnp.zeros_like(st_ref)

    o, st_new = _hgrn2_block(zq_ref[0], zf_ref[0], zi_ref[0], zg_ref[0], lb_ref[...], nw_ref[...], st_ref[...])
    st_ref[...] = st_new
    o_ref[0] = o.astype(o_ref.dtype)


def _hgrn2_call(z3, lower_bound, norm_w, tb=256):
    b, t, _ = z3.shape
    tb = min(tb, t)
    base = HG_OFF // LANES

    def zspec(part):
        return pl.BlockSpec((1, tb, LANES), lambda i, h, j: (i, j, base + part * HG_HEADS + h))

    return pl.pallas_call(
        _hgrn2_kernel,
        grid=(b, HG_HEADS, t // tb),
        in_specs=[
            zspec(0), zspec(1), zspec(2), zspec(3),
            pl.BlockSpec((1, LANES), lambda i, h, j: (0, h)),
            pl.BlockSpec((1, LANES), lambda i, h, j: (0, 0)),
        ],
        out_specs=pl.BlockSpec((1, tb, LANES), lambda i, h, j: (i, j, h)),
        out_shape=jax.ShapeDtypeStruct((b, t, HG_W), BF16),
        scratch_shapes=[pltpu.VMEM((HG_DK, HG_DK), F32)],
        compiler_params=_cparams(("parallel", "parallel", "arbitrary")),
        name="hgrn2_mixer",
    )(z3, z3, z3, z3, lower_bound.reshape(1, HG_W), norm_w.reshape(1, HG_DK))


def _ret_kernel(zq_ref, zk_ref, zv_ref, zg_ref, cos_ref, sin_ref, o_ref, st_ref, *, chunk):
    h = pl.program_id(1)

    @pl.when(pl.program_id(2) == 0)
    def _():
        st_ref[...] = jnp.zeros_like(st_ref)

    gamma = jnp.float32(0.0)
    for hh in range(RET_HEADS):
        gamma = jnp.where(h == hh, jnp.float32(1.0 - 2.0 ** (-5.0 - hh)), gamma)
    lg = jnp.log(jnp.full((1, 1), gamma, F32))

    cos2 = cos_ref[0]
    sin2 = sin_ref[0]
    half = RET_DK // 2

    def rope(z):
        return z * cos2 + pltpu.roll(z, half, 1) * sin2

    q = rope(zq_ref[0]) * (RET_DK ** -0.5)
    k = rope(zk_ref[0])
    v_b = zv_ref[0].astype(BF16)
    row = lax.broadcasted_iota(jnp.int32, (chunk, chunk), 0)
    col = lax.broadcasted_iota(jnp.int32, (chunk, chunk), 1)
    rel = (row - col).astype(F32)
    dmask = jnp.where(rel >= 0.0, jnp.exp(jnp.maximum(rel, 0.0) * lg), 0.0)
    tcol = lax.broadcasted_iota(jnp.int32, (chunk, 1), 0).astype(F32)
    xi = jnp.exp((tcol + 1.0) * lg)
    zeta = jnp.exp((chunk - 1.0 - tcol) * lg)
    st = st_ref[...]
    scores = _dot_nt(q.astype(BF16), k.astype(BF16)) * dmask
    o = _dot(scores.astype(BF16), v_b) + _dot_nt((q * xi).astype(BF16), st.astype(BF16))
    st_ref[...] = st * jnp.exp(chunk * lg) + _dot_tn(v_b, (k * zeta).astype(BF16))
    o = o * lax.rsqrt(jnp.mean(o * o, axis=-1, keepdims=True) + NORM_EPS)
    o_ref[0] = (o * _silu(zg_ref[0])).astype(o_ref.dtype)


def _ret_call(z3, cos2, sin2, chunk=256):
    b, t, _ = z3.shape
    chunk = min(chunk, t)
    base = RET_OFF // LANES

    def zspec(part):
        return pl.BlockSpec((1, chunk, LANES), lambda i, h, j: (i, j, base + part * RET_HEADS + h))

    tab = pl.BlockSpec((1, chunk, RET_DK), lambda i, h, j: (i, j, 0))
    return pl.pallas_call(
        functools.partial(_ret_kernel, chunk=chunk),
        grid=(b, RET_HEADS, t // chunk),
        in_specs=[zspec(0), zspec(1), zspec(2), zspec(3), tab, tab],
        out_specs=pl.BlockSpec((1, chunk, LANES), lambda i, h, j: (i, j, h)),
        out_shape=jax.ShapeDtypeStruct((b, t, RET_W), BF16),
        scratch_shapes=[pltpu.VMEM((RET_DK, RET_DK), F32)],
        compiler_params=_cparams(("parallel", "parallel", "arbitrary")),
        name="retention_mixer",
    )(z3, z3, z3, z3, cos2, sin2)


def _split_bf16(x):
    hi = x.astype(BF16)
    return hi, (x - hi.astype(F32)).astype(BF16)


def _dot_x3(a, b):
    ah, al = _split_bf16(a)
    bh, bl = _split_bf16(b)
    return _dot(ah, bh) + _dot(ah, bl) + _dot(al, bh)


def _dot_x2_lhs(a, b_exact):
    ah, al = _split_bf16(a)
    return _dot(ah, b_exact) + _dot(al, b_exact)


def _dot_x2_rhs(a_exact, b):
    bh, bl = _split_bf16(b)
    return _dot(a_exact, bh) + _dot(a_exact, bl)


def _bdot(a, b):
    return _dot(a.astype(BF16), b.astype(BF16))


def _inv_unit_lower(a, eye, blk_mask):
    c = a[0].shape[0]
    m = range(len(a))
    a_bd = [jnp.where(blk_mask, a[i], 0.0) for i in m]
    a_off = [a[i] - a_bd[i] for i in m]
    a2 = [_bdot(a_bd[i], a_bd[i]) for i in m]
    p = [eye + a_bd[i] for i in m]
    r = [_bdot(jnp.concatenate([p[i], a2[i]], axis=0), a2[i]) for i in m]
    p = [p[i] + r[i][:c] for i in m]
    a4 = [r[i][c:] for i in m]
    r = [_bdot(jnp.concatenate([p[i], a4[i]], axis=0), a4[i]) for i in m]
    p = [p[i] + r[i][:c] for i in m]
    a8 = [r[i][c:] for i in m]
    t_bd = [p[i] + _bdot(p[i], a8[i]) for i in m]
    n = [_bdot(t_bd[i], a_off[i]) for i in m]
    r = [_bdot(n[i], jnp.concatenate([n[i], t_bd[i]], axis=1)) for i in m]
    z = [t_bd[i] + r[i][:, c:] for i in m]
    return [z[i] + _bdot(r[i][:, :c], z[i]) for i in m]


def _rwkv_kernel(z_ref, mu_ref, w0_ref, w2_ref, a0_ref, a2_ref, g2_ref, kk_ref, ka_ref, rk_ref,
                 lnw_ref, lnb_ref, seg_ref, o_ref, s_ref, prev_ref):
    c = RW_CHUNK

    @pl.when(pl.program_id(1) == 0)
    def _():
        s_ref[...] = jnp.zeros_like(s_ref)
        prev_ref[...] = jnp.zeros_like(prev_ref)

    z = z_ref[0]
    rows = lax.broadcasted_iota(jnp.int32, (c, 1), 0)
    z_prev = jnp.where(rows == 0, prev_ref[...], pltpu.roll(z, 1, 0))
    prev_ref[...] = z[c - 1:c]
    zs = z + mu_ref[...] * (z_prev - z)
    r = zs[:, 0:RW_W]
    k = zs[:, RW_W:2 * RW_W]
    v = zs[:, 2 * RW_W:3 * RW_W]
    off = 3 * RW_W
    w_lo = zs[:, off:off + RW_DECAY_LORA]
    a_lo = zs[:, off + RW_DECAY_LORA:off + RW_DECAY_LORA + RW_A_LORA]
    g_lo = zs[:, off + RW_DECAY_LORA + RW_A_LORA:]

    wx = -(w0_ref[...] + _dot_x3(jnp.tanh(w_lo), w2_ref[...]))
    softplus = jnp.maximum(wx, 0.0) + jnp.log(1.0 + jnp.exp(-jnp.abs(wx)))
    logw = -jnp.exp(-softplus - 0.5)
    a = _sigmoid(a0_ref[...] + _dot_x3(a_lo, a2_ref[...]))
    g = _dot_x3(_sigmoid(g_lo), g2_ref[...])
    seg = seg_ref[...]
    kk = k * kk_ref[...]
    kk = kk * lax.rsqrt(jnp.maximum(_dot_x2_lhs(kk * kk, seg), 1e-24))
    k2 = k * (1.0 + (a - 1.0) * ka_ref[...])

    row = lax.broadcasted_iota(jnp.int32, (c, c), 0)
    col = lax.broadcasted_iota(jnp.int32, (c, c), 1)
    incl = row >= col
    strict = row > col
    blk_mask = (row // RW_BLK) == (col // RW_BLK)
    eye = (row == col).astype(F32)
    row2 = lax.broadcasted_iota(jnp.int32, (c, 2 * c), 0)
    col2 = lax.broadcasted_iota(jnp.int32, (c, 2 * c), 1) % c
    incl2 = row2 >= col2
    strict2 = row2 > col2
    cw = _dot_x2_rhs(incl.astype(BF16), logw)
    w_inv = jnp.exp(-cw)
    last = cw[c - 1:c]
    w_rest = jnp.exp(last - cw)
    w_last = jnp.exp(last)
    beta = a * kk
    alpha_t = -kk * jnp.exp(cw - logw)
    r_t = r * jnp.exp(cw)
    beta_h = beta * w_inv
    k_h = k2 * w_inv
    beta_d = beta * w_rest
    k_d = k2 * w_rest

    hs = range(RW_HEADS)
    sl = [slice(h * RW_N, (h + 1) * RW_N) for h in hs]
    s0 = [s_ref[h] for h in hs]
    v_h = [v[:, sl[h]] for h in hs]
    lhs = [jnp.concatenate([alpha_t[:, sl[h]], r_t[:, sl[h]]], axis=0).astype(BF16) for h in hs]
    rhs = [jnp.concatenate([beta_h[:, sl[h]], k_h[:, sl[h]]], axis=0).astype(BF16) for h in hs]
    big = [_dot_nt(lhs[h], rhs[h]) for h in hs]
    a_a = [jnp.where(strict2, big[h][:c], 0.0) for h in hs]
    a_r = [jnp.where(incl2, big[h][c:], 0.0).astype(BF16) for h in hs]
    t_inv = _inv_unit_lower([a_a[h][:, :c] for h in hs], eye, blk_mask)
    sd = [_dot_nt(lhs[h], s0[h].astype(BF16)) for h in hs]
    b = [sd[h][:c] + _bdot(a_a[h][:, c:], v_h[h]) for h in hs]
    u = [_bdot(t_inv[h], b[h]) for h in hs]
    uv = [jnp.concatenate([u[h], v_h[h]], axis=0).astype(BF16) for h in hs]
    outs = [sd[h][c:] + _dot(a_r[h], uv[h]) for h in hs]
    bk_d = [jnp.concatenate([beta_d[:, sl[h]], k_d[:, sl[h]]], axis=0).astype(BF16) for h in hs]
    for h in hs:
        s_ref[h] = s0[h] * w_last[:, sl[h]] + _dot_tn(uv[h], bk_d[h])
    o = jnp.concatenate(outs, axis=1)

    mean = _dot_x2_lhs(o, seg) * (1.0 / RW_N)
    dev = o - mean
    var = _dot_x2_lhs(dev * dev, seg) * (1.0 / RW_N)
    o = dev * lax.rsqrt(var + RW_GN_EPS) * lnw_ref[...] + lnb_ref[...]
    bonus = _dot_x2_lhs(r * k2 * rk_ref[...], seg) * v
    o_ref[0] = ((o + bonus) * g).astype(o_ref.dtype)


def _rwkv_call(z3, mu, w0, w2, a0, a2, g2, k_k, k_a, r_k, ln_w, ln_b):
    b, t, _ = z3.shape
    c = RW_CHUNK
    hid = lax.broadcasted_iota(jnp.int32, (RW_W, RW_W), 0) // RW_N
    seg = (hid == hid.T).astype(BF16)

    def vec(n):
        return pl.BlockSpec((1, n), lambda i, j: (0, 0))

    def mat(m, n):
        return pl.BlockSpec((m, n), lambda i, j: (0, 0))

    return pl.pallas_call(
        _rwkv_kernel,
        grid=(b, t // c),
        in_specs=[
            pl.BlockSpec((1, c, RW_COLS), lambda i, j: (i, j, RW_OFF // RW_COLS)),
            vec(RW_COLS), vec(RW_W), mat(RW_DECAY_LORA, RW_W), vec(RW_W), mat(RW_A_LORA, RW_W),
            mat(RW_GATE_LORA, RW_W), vec(RW_W), vec(RW_W), vec(RW_W), vec(RW_W), vec(RW_W),
            mat(RW_W, RW_W),
        ],
        out_specs=pl.BlockSpec((1, c, RW_W), lambda i, j: (i, j, 0)),
        out_shape=jax.ShapeDtypeStruct((b, t, RW_W), BF16),
        scratch_shapes=[pltpu.VMEM((RW_HEADS, RW_N, RW_N), F32), pltpu.VMEM((1, RW_COLS), F32)],
        compiler_params=_cparams(("parallel", "arbitrary")),
        name="rwkv7_mixer",
    )(z3, mu.reshape(1, -1), w0.reshape(1, -1), w2, a0.reshape(1, -1), a2, g2, k_k.reshape(1, -1),
      k_a.reshape(1, -1), r_k.reshape(1, -1), ln_w.reshape(1, -1), ln_b.reshape(1, -1), seg)


def _merge_kernel(ohg_ref, oret_ref, orw_ref, zg_ref, x_ref, gate_ref, bhg_ref, bret_ref, brw_ref,
                  wout_ref, o_ref):
    d = x_ref.shape[1]
    y = _sigmoid(zg_ref[:, 0:d]) * _dot(ohg_ref[...], bhg_ref[...])
    y = y + _sigmoid(zg_ref[:, d:2 * d]) * _dot(oret_ref[...], bret_ref[...])
    y = y + _sigmoid(zg_ref[:, 2 * d:3 * d]) * _dot(orw_ref[...], brw_ref[...])
    o_ref[...] = x_ref[...] + gate_ref[0] * _dot(y.astype(BF16), wout_ref[...])


def _merge_call(o_hg, o_ret, o_rw, z2, x2, mod3, br_hg, br_ret, br_rw, w_out, seq, gate_blk, tm=512):
    n, d = x2.shape
    tpb = seq // tm

    def rows(w):
        return pl.BlockSpec((tm, w), lambda i: (i, 0))

    def full(m, k):
        return pl.BlockSpec((m, k), lambda i: (0, 0))

    return pl.pallas_call(
        _merge_kernel,
        grid=(n // tm,),
        in_specs=[
            rows(HG_W), rows(RET_W), rows(RW_W), rows(3 * d), rows(d),
            pl.BlockSpec((1, 1, d), lambda i: (i // tpb, 0, gate_blk)),
            full(HG_W, d), full(RET_W, d), full(RW_W, d), full(d, d),
        ],
        out_specs=rows(d),
        out_shape=jax.ShapeDtypeStruct((n, d), F32),
        compiler_params=_cparams(("parallel",)),
        name="merge_outproj",
    )(o_hg, o_ret, o_rw, z2, x2, mod3, br_hg, br_ret, br_rw, w_out)


def _pack_bf16_pairs(x):
    w = x.shape[1] // 2
    hi = pltpu.bitcast(x[:, :w].astype(BF16).astype(F32), jnp.uint32)
    lo = pltpu.bitcast(x[:, w:].astype(BF16).astype(F32), jnp.uint32)
    return pltpu.bitcast(hi | lax.shift_right_logical(lo, jnp.uint32(16)), jnp.int32)


def _unpack_bf16_pairs(p):
    u = pltpu.bitcast(p, jnp.uint32)
    hi = pltpu.bitcast(u & jnp.uint32(0xFFFF0000), F32)
    lo = pltpu.bitcast(lax.shift_left(u, jnp.uint32(16)), F32)
    return jnp.concatenate([hi, lo], axis=1)


def _route_kernel(x_ref, g_ref, scale_ref, shift_ref, rg_ref, re_ref, hp_ref, eid_ref, wts_ref):
    h = _rms_mod(x_ref[...], g_ref[...], scale_ref[0], shift_ref[0])
    hp_ref[...] = _pack_bf16_pairs(h)
    tm = h.shape[0]
    lane = lax.broadcasted_iota(jnp.int32, (tm, LANES), 1)
    neg = -jnp.inf
    gl = jnp.where(lane < N_GROUPS, _dot(h, rg_ref[...], HIGHEST), neg)
    gmax = jnp.max(gl, axis=-1, keepdims=True)
    gidx = jnp.min(jnp.where(gl == gmax, lane, LANES), axis=-1, keepdims=True)
    gw = 1.0 / jnp.sum(jnp.exp(gl - gmax), axis=-1, keepdims=True)
    lo = gidx * EXPERTS_PER_GROUP
    el = _dot(h, re_ref[...], HIGHEST)
    el = jnp.where(lane >= lo, jnp.where(lane < lo + EXPERTS_PER_GROUP, el, neg), neg)
    m1 = jnp.max(el, axis=-1, keepdims=True)
    i1 = jnp.min(jnp.where(el == m1, lane, LANES), axis=-1, keepdims=True)
    el2 = jnp.where(lane == i1, neg, el)
    m2 = jnp.max(el2, axis=-1, keepdims=True)
    i2 = jnp.min(jnp.where(el2 == m2, lane, LANES), axis=-1, keepdims=True)
    e2 = jnp.exp(m2 - m1)
    p1 = 1.0 / (1.0 + e2)
    p2 = e2 * p1
    eid_ref[...] = jnp.where(lane == 0, i1, jnp.where(lane == 1, i2, 0))
    wts_ref[...] = jnp.where(lane == 0, gw * p1, jnp.where(lane == 1, gw * p2, 0.0))


def _route_call(x2, gain, mod3, router_g, router_e, seq, scale_blk, shift_blk, tm=512):
    n, d = x2.shape
    tpb = seq // tm
    rg = jnp.pad(router_g, ((0, 0), (0, LANES - N_GROUPS)))
    re = jnp.pad(router_e, ((0, 0), (0, LANES - N_EXPERTS)))
    return pl.pallas_call(
        _route_kernel,
        grid=(n // tm,),
        in_specs=[
            pl.BlockSpec((tm, d), lambda i: (i, 0)),
            pl.BlockSpec((1, d), lambda i: (0, 0)),
            pl.BlockSpec((1, 1, d), lambda i: (i // tpb, 0, scale_blk)),
            pl.BlockSpec((1, 1, d), lambda i: (i // tpb, 0, shift_blk)),
            pl.BlockSpec((d, LANES), lambda i: (0, 0)),
            pl.BlockSpec((d, LANES), lambda i: (0, 0)),
        ],
        out_specs=[pl.BlockSpec((tm, d // 2), lambda i: (i, 0)), pl.BlockSpec((tm, LANES), lambda i: (i, 0)),
                   pl.BlockSpec((tm, LANES), lambda i: (i, 0))],
        out_shape=[jax.ShapeDtypeStruct((n, d // 2), jnp.int32), jax.ShapeDtypeStruct((n, LANES), jnp.int32),
                   jax.ShapeDtypeStruct((n, LANES), F32)],
        compiler_params=_cparams(("parallel",)),
        name="moe_route",
    )(x2, gain.reshape(1, d), mod3, mod3, rg, re)


SC_CORES = 2
SC_SUBCORES = 16
SC_WORKERS = SC_CORES * SC_SUBCORES
SC_ROWS = 64


def _sc_gather(table, idx):
    m = idx.shape[0]
    w = table.shape[1]
    per_worker = m // SC_WORKERS
    steps = per_worker // SC_ROWS
    assert per_worker * SC_WORKERS == m and steps * SC_ROWS == per_worker
    mesh = plsc.VectorSubcoreMesh(core_axis_name="c", subcore_axis_name="s")

    def body(table_hbm, idx_hbm, out_hbm, idx_v, rows_v, sem):
        wid = lax.axis_index("s") * SC_CORES + lax.axis_index("c")
        pltpu.sync_copy(idx_hbm.at[wid], idx_v)

        @pl.loop(0, steps)
        def _(j):
            pltpu.async_copy(table_hbm.at[idx_v.at[j]], rows_v, sem).wait()
            pltpu.sync_copy(rows_v, out_hbm.at[pl.ds(wid * per_worker + j * SC_ROWS, SC_ROWS)])

    return pl.kernel(
        body,
        out_type=jax.ShapeDtypeStruct((m, w), table.dtype),
        mesh=mesh,
        scratch_types=[pltpu.VMEM((steps, SC_ROWS), jnp.int32), pltpu.VMEM((SC_ROWS, w), table.dtype),
                       pltpu.SemaphoreType.DMA],
        name="sc_row_gather",
    )(table, idx.reshape(SC_WORKERS, steps, SC_ROWS))


MOE_TM = 512


def _gexperts_kernel(te_ref, nu_ref, xs_ref, w1_ref, w3_ref, w2_ref, ys_ref):
    @pl.when(pl.program_id(0) < nu_ref[0])
    def _():
        xb = _unpack_bf16_pairs(xs_ref[...]).astype(BF16)
        act = _silu(_dot(xb, w1_ref[0])) * _dot(xb, w3_ref[0])
        ys_ref[...] = _pack_bf16_pairs(_dot(act.astype(BF16), w2_ref[0]))


def _gexperts_call(xs, tile_expert, n_used, w1, w3, w2):
    p, half = xs.shape
    ne, d, de = w1.shape
    nt = p // MOE_TM

    def rows(i, te, nu):
        return (jnp.minimum(i, nu[0] - 1), 0)

    def wsel(i, te, nu):
        return (te[i], 0, 0)

    return pl.pallas_call(
        _gexperts_kernel,
        grid_spec=pltpu.PrefetchScalarGridSpec(
            num_scalar_prefetch=2,
            grid=(nt,),
            in_specs=[
                pl.BlockSpec((MOE_TM, half), rows),
                pl.BlockSpec((1, d, de), wsel),
                pl.BlockSpec((1, d, de), wsel),
                pl.BlockSpec((1, de, d), wsel),
            ],
            out_specs=pl.BlockSpec((MOE_TM, half), rows),
        ),
        out_shape=jax.ShapeDtypeStruct((p, half), jnp.int32),
        compiler_params=_cparams(("arbitrary",)),
        name="moe_experts",
    )(tile_expert, n_used, xs, w1, w3, w2)


def _combine_kernel(yg_ref, wts_ref, x_ref, gate_ref, fg_ref, o_ref, *, final_norm):
    half = yg_ref.shape[1] // 2
    wts = wts_ref[...]
    moe = wts[:, 0:1] * _unpack_bf16_pairs(yg_ref[:, :half]) + wts[:, 1:2] * _unpack_bf16_pairs(yg_ref[:, half:])
    xn = x_ref[...] + gate_ref[0] * moe
    if final_norm:
        xn = xn * lax.rsqrt(jnp.mean(xn * xn, axis=-1, keepdims=True) + NORM_EPS) * fg_ref[...]
    o_ref[...] = xn


def _combine_call(yg, wts, x2, mod3, final_g, seq, gate_blk, final_norm, tm=512):
    n, d = x2.shape
    tpb = seq // tm
    return pl.pallas_call(
        functools.partial(_combine_kernel, final_norm=final_norm),
        grid=(n // tm,),
        in_specs=[
            pl.BlockSpec((tm, d), lambda i: (i, 0)),
            pl.BlockSpec((tm, LANES), lambda i: (i, 0)),
            pl.BlockSpec((tm, d), lambda i: (i, 0)),
            pl.BlockSpec((1, 1, d), lambda i: (i // tpb, 0, gate_blk)),
            pl.BlockSpec((1, d), lambda i: (0, 0)),
        ],
        out_specs=pl.BlockSpec((tm, d), lambda i: (i, 0)),
        out_shape=jax.ShapeDtypeStruct((n, d), F32),
        compiler_params=_cparams(("parallel",)),
        name="moe_combine",
    )(yg, wts, x2, mod3, final_g.reshape(1, d))


def _moe_plan(eid):
    n = eid.shape[0]
    na = 2 * n
    nt = na // MOE_TM + N_EXPERTS
    e_flat = eid[:, :2].reshape(na)
    order = jnp.argsort(e_flat, stable=True).astype(jnp.int32)
    e_sorted = e_flat[order]
    counts = jnp.sum(e_flat[:, None] == jnp.arange(N_EXPERTS, dtype=jnp.int32)[None, :], axis=0, dtype=jnp.int32)
    tiles = (counts + MOE_TM - 1) // MOE_TM
    tile_end = jnp.cumsum(tiles)
    tile_start = tile_end - tiles
    row_start = jnp.cumsum(counts) - counts
    p_sorted = tile_start[e_sorted] * MOE_TM + jnp.arange(na, dtype=jnp.int32) - row_start[e_sorted]
    src_token = jnp.zeros((nt * MOE_TM,), jnp.int32).at[p_sorted].set(order // 2)
    pos = jnp.zeros((na,), jnp.int32).at[order].set(p_sorted)
    n_used = tile_end[-1:]
    tile_id = jnp.minimum(jnp.arange(nt, dtype=jnp.int32), n_used - 1)
    tile_expert = jnp.sum(tile_id[:, None] >= tile_end[None, :], axis=1, dtype=jnp.int32)
    return src_token, pos, tile_expert, n_used


def kernel(x, c, positions, ada_w, ada_b, norm1_g, norm2_g, w_in, hg_lb_table, hg_norm_w, rw_mu, rw_w0, rw_w2,
           rw_a0, rw_a2, rw_g2, rw_k_k, rw_k_a, rw_r_k, rw_ln_w, rw_ln_b, br_hg, br_ret, br_rw, w_out,
           router_g, router_e, moe_w1, moe_w3, moe_w2, final_g):
    b, t, d = x.shape
    depth = ada_w.shape[0]
    n = b * t
    assert w_in.shape[2] == IN_COLS and d == 1024

    lb_p = jax.nn.softmax(hg_lb_table.astype(F32), axis=0)
    lower_bounds = jnp.cumsum(lb_p, axis=0) - lb_p[0]

    mod = _mod_call(c, ada_w, ada_b)
    cos2, sin2 = _rope_call(positions, RET_DK)
    n_gate = 3 * d
    x2 = x.reshape(n, d)
    for l in range(depth):
        mod3 = mod[l].reshape(b, 1, 6 * d)
        w_perm = jnp.concatenate([w_in[l][:, IN_COLS - n_gate:], w_in[l][:, :IN_COLS - n_gate]], axis=1)
        z2 = _inproj_call(x2, norm1_g[l], mod3, w_perm.astype(BF16), t, scale_blk=1, shift_blk=0)
        z3 = z2.reshape(b, t, IN_COLS)
        o_hg = _hgrn2_call(z3, lower_bounds[l], hg_norm_w[l])
        o_ret = _ret_call(z3, cos2, sin2)
        o_rw = _rwkv_call(z3, rw_mu[l], rw_w0[l], rw_w2[l], rw_a0[l], rw_a2[l], rw_g2[l], rw_k_k[l],
                          rw_k_a[l], rw_r_k[l], rw_ln_w[l], rw_ln_b[l])
        x2 = _merge_call(o_hg.reshape(n, HG_W), o_ret.reshape(n, RET_W), o_rw.reshape(n, RW_W), z2, x2, mod3,
                         br_hg[l].astype(BF16), br_ret[l].astype(BF16), br_rw[l].astype(BF16),
                         w_out[l].astype(BF16), t, gate_blk=2)
        hp, eid, wts = _route_call(x2, norm2_g[l], mod3, router_g[l], router_e[l], t, scale_blk=4, shift_blk=3)
        src_token, pos, tile_expert, n_used = _moe_plan(eid)
        xs = _sc_gather(hp, src_token)
        ys = _gexperts_call(xs, tile_expert, n_used, moe_w1[l].astype(BF16), moe_w3[l].astype(BF16),
                            moe_w2[l].astype(BF16))
        yg = _sc_gather(ys, pos).reshape(n, d)
        x2 = _combine_call(yg, wts, x2, mod3, final_g, t, gate_blk=5, final_norm=(l == depth - 1))
    return x2.reshape(b, t, d)
```

```python
import functools

import jax
import jax.numpy as jnp
from jax import lax
from jax.experimental import pallas as pl
from jax.experimental.pallas import tpu as pltpu
from jax.experimental.pallas import tpu_sc as plsc

F32 = jnp.float32
BF16 = jnp.bfloat16
HIGHEST = lax.Precision.HIGHEST

HG_HEADS = 4
HG_DK = 128
HG_W = HG_HEADS * HG_DK
RET_HEADS = 4
RET_DK = 128
RET_W = RET_HEADS * RET_DK
RW_HEADS = 8
RW_N = 64
RW_W = RW_HEADS * RW_N
RW_DECAY_LORA = 64
RW_A_LORA = 64
RW_GATE_LORA = 128
RW_COLS = 3 * RW_W + RW_DECAY_LORA + RW_A_LORA + RW_GATE_LORA
RW_GN_EPS = 64e-5
N_GROUPS = 4
EXPERTS_PER_GROUP = 8
N_EXPERTS = N_GROUPS * EXPERTS_PER_GROUP
ROPE_THETA = 10000.0
NORM_EPS = 1e-6

LANES = 128
VMEM_LIMIT = 56 * 1024 * 1024

GATE_OFF = 0
HG_OFF = 3 * 1024
RET_OFF = HG_OFF + 4 * HG_W
RW_OFF = RET_OFF + 4 * RET_W
IN_COLS = RW_OFF + RW_COLS

HG_CHUNK = 64
HG_SUB = 16
RW_CHUNK = 64
RW_TB = 128
RW_BLK = 16


def _cparams(sem):
    return pltpu.CompilerParams(dimension_semantics=sem, vmem_limit_bytes=VMEM_LIMIT)


def _dot(a, b, precision=None):
    return jnp.dot(a, b, preferred_element_type=F32, precision=precision)


def _dot_nt(a, b, precision=None):
    return lax.dot_general(a, b, (((1,), (1,)), ((), ())), preferred_element_type=F32, precision=precision)


def _dot_tn(a, b, precision=None):
    return lax.dot_general(a, b, (((0,), (0,)), ((), ())), preferred_element_type=F32, precision=precision)


def _sigmoid(x):
    return 1.0 / (1.0 + jnp.exp(-x))


def _silu(x):
    return x * _sigmoid(x)


def _rms_mod(x, gain, scale, shift):
    y = x * lax.rsqrt(jnp.mean(x * x, axis=-1, keepdims=True) + NORM_EPS)
    return (y * gain) * (1.0 + scale) + shift


def _mod_kernel(c_ref, w_ref, b_ref, o_ref):
    c = c_ref[...]
    o_ref[0] = _dot(_silu(c), w_ref[0], HIGHEST) + b_ref[0]


def _mod_call(c, ada_w, ada_b):
    depth, d, d6 = ada_w.shape
    b = c.shape[0]
    nblk = d6 // d
    return pl.pallas_call(
        _mod_kernel,
        grid=(depth, nblk),
        in_specs=[
            pl.BlockSpec((b, d), lambda l, j: (0, 0)),
            pl.BlockSpec((1, d, d), lambda l, j: (l, 0, j)),
            pl.BlockSpec((1, 1, d), lambda l, j: (l, 0, j)),
        ],
        out_specs=pl.BlockSpec((1, b, d), lambda l, j: (l, 0, j)),
        out_shape=jax.ShapeDtypeStruct((depth, b, d6), F32),
        compiler_params=_cparams(("parallel", "parallel")),
        name="adaln_mod",
    )(c, ada_w, ada_b.reshape(depth, 1, d6))


def _rope_kernel(pos_ref, freq_ref, sign_ref, cos_ref, sin_ref):
    ang = pos_ref[0].astype(F32) * freq_ref[...]
    cos_ref[0] = jnp.cos(ang)
    sin_ref[0] = jnp.sin(ang) * sign_ref[...]


def _rope_call(positions, d):
    b, t = positions.shape
    tb = min(t, 512)
    inv_freq = ROPE_THETA ** (-jnp.arange(0, d, 2, dtype=F32) / d)
    freq2 = jnp.concatenate([inv_freq, inv_freq]).reshape(1, d)
    sign2 = jnp.concatenate([-jnp.ones((d // 2,), F32), jnp.ones((d // 2,), F32)]).reshape(1, d)
    out = jax.ShapeDtypeStruct((b, t, d), F32)
    return pl.pallas_call(
        _rope_kernel,
        grid=(b, t // tb),
        in_specs=[
            pl.BlockSpec((1, tb, 1), lambda i, j: (i, j, 0)),
            pl.BlockSpec((1, d), lambda i, j: (0, 0)),
            pl.BlockSpec((1, d), lambda i, j: (0, 0)),
        ],
        out_specs=[pl.BlockSpec((1, tb, d), lambda i, j: (i, j, 0))] * 2,
        out_shape=[out, out],
        compiler_params=_cparams(("parallel", "parallel")),
        name="rope_tables",
    )(positions.reshape(b, t, 1), freq2, sign2)


def _inproj_kernel(x_ref, g_ref, scale_ref, shift_ref, w_ref, o_ref, h_ref):
    @pl.when(pl.program_id(1) == 0)
    def _():
        h = _rms_mod(x_ref[...], g_ref[...], scale_ref[0], shift_ref[0])
        h_ref[...] = h.astype(BF16)

    o_ref[...] = _dot(h_ref[...], w_ref[...])


def _inproj_call(x2, gain, mod3, w_bf16, seq, scale_blk, shift_blk, tm=1024, tn=1280):
    n, d = x2.shape
    cols = w_bf16.shape[1]
    tpb = seq // tm
    return pl.pallas_call(
        _inproj_kernel,
        grid=(n // tm, cols // tn),
        in_specs=[
            pl.BlockSpec((tm, d), lambda i, j: (i, 0)),
            pl.BlockSpec((1, d), lambda i, j: (0, 0)),
            pl.BlockSpec((1, 1, d), lambda i, j: (i // tpb, 0, scale_blk)),
            pl.BlockSpec((1, 1, d), lambda i, j: (i // tpb, 0, shift_blk)),
            pl.BlockSpec((d, tn), lambda i, j: (0, j)),
        ],
        out_specs=pl.BlockSpec((tm, tn), lambda i, j: (i, j)),
        out_shape=jax.ShapeDtypeStruct((n, cols), F32),
        scratch_shapes=[pltpu.VMEM((tm, d), BF16)],
        compiler_params=_cparams(("parallel", "arbitrary")),
        name="norm_inproj",
    )(x2, gain.reshape(1, d), mod3, mod3, w_bf16)


def _hgrn2_block(zq, zf, zi, zg, lb, nw, st):
    tb = zq.shape[0]
    c, sub = HG_CHUNK, HG_SUB
    nc, ns = tb // c, c // sub
    f = lb + (1.0 - lb) * _sigmoid(zf)
    logf = jnp.log(jnp.maximum(f, 1e-30))
    q = _silu(zq) * (HG_DK ** -0.5)
    k = 1.0 - f
    v = zi
    v_b = v.astype(BF16)
    row = lax.broadcasted_iota(jnp.int32, (tb, tb), 0)
    col = lax.broadcasted_iota(jnp.int32, (tb, tb), 1)
    tri = jnp.where(col >= (row // c) * c, jnp.where(row >= col, 1.0, 0.0), 0.0).astype(BF16)
    cum = _dot_x2_rhs(tri, logf)
    qe = (q * jnp.exp(cum)).astype(BF16)

    offd = [(ci * c, ci * c + sub * i) for ci in range(nc) for i in range(1, ns)]
    base = [cum[lo - 1:lo] for _, lo in offd]
    qt = [(q[lo:lo + sub] * jnp.exp(cum[lo:lo + sub] - base[j])).astype(BF16) for j, (_, lo) in enumerate(offd)]
    kt = [(k[r0:lo] * jnp.exp(base[j] - cum[r0:lo])).astype(BF16) for j, (r0, lo) in enumerate(offd)]
    a = [_dot_nt(qt[j], kt[j]).astype(BF16) for j in range(len(offd))]
    av = {lo: _dot(a[j], v_b[r0:lo]) for j, (r0, lo) in enumerate(offd)}

    nb = tb // sub
    cum3, q3, k3, v3 = (x.reshape(nb, sub, HG_DK) for x in (cum, q, k, v))
    trow = lax.broadcasted_iota(jnp.int32, (nb, sub, 1), 1)
    diag = jnp.zeros((nb, sub, HG_DK), F32)
    for s in range(sub):
        e = jnp.exp(jnp.minimum(cum3 - cum3[:, s:s + 1, :], 0.0))
        a_col = jnp.sum(q3 * k3[:, s:s + 1, :] * e, axis=-1, keepdims=True)
        diag = diag + jnp.where(trow >= s, a_col, 0.0) * v3[:, s:s + 1, :]
    diag = diag.reshape(tb, HG_DK)

    outs = []
    for ci in range(nc):
        r0 = ci * c
        o_inter = _dot_nt(qe[r0:r0 + c], st.astype(BF16))
        for i in range(ns):
            lo = r0 + sub * i
            piece = o_inter[sub * i:sub * (i + 1)] + diag[lo:lo + sub]
            outs.append(piece + av[lo] if i > 0 else piece)
        last = cum[r0 + c - 1:r0 + c]
        kd = (k[r0:r0 + c] * jnp.exp(last - cum[r0:r0 + c])).astype(BF16)
        st = st * jnp.exp(last) + _dot_tn(v_b[r0:r0 + c], kd)
    o = jnp.concatenate(outs, axis=0)
    o = o * lax.rsqrt(jnp.mean(o * o, axis=-1, keepdims=True) + NORM_EPS)
    o = o * nw * _silu(zg)
    return o, st


def _hgrn2_kernel(zq_ref, zf_ref, zi_ref, zg_ref, lb_ref, nw_ref, o_ref, st_ref):
    @pl.when(pl.program_id(2) == 0)
    def _():
        st_ref[...] = jnp.zeros_like(st_ref)

    o, st_new = _hgrn2_block(zq_ref[0], zf_ref[0], zi_ref[0], zg_ref[0], lb_ref[...], nw_ref[...], st_ref[...])
    st_ref[...] = st_new
    o_ref[0] = o.astype(o_ref.dtype)


def _hgrn2_call(z3, lower_bound, norm_w, tb=256):
    b, t, _ = z3.shape
    tb = min(tb, t)
    base = HG_OFF // LANES

    def zspec(part):
        return pl.BlockSpec((1, tb, LANES), lambda i, h, j: (i, j, base + part * HG_HEADS + h))

    return pl.pallas_call(
        _hgrn2_kernel,
        grid=(b, HG_HEADS, t // tb),
        in_specs=[
            zspec(0), zspec(1), zspec(2), zspec(3),
            pl.BlockSpec((1, LANES), lambda i, h, j: (0, h)),
            pl.BlockSpec((1, LANES), lambda i, h, j: (0, 0)),
        ],
        out_specs=pl.BlockSpec((1, tb, LANES), lambda i, h, j: (i, j, h)),
        out_shape=jax.ShapeDtypeStruct((b, t, HG_W), BF16),
        scratch_shapes=[pltpu.VMEM((HG_DK, HG_DK), F32)],
        compiler_params=_cparams(("parallel", "parallel", "arbitrary")),
        name="hgrn2_mixer",
    )(z3, z3, z3, z3, lower_bound.reshape(1, HG_W), norm_w.reshape(1, HG_DK))


def _ret_kernel(zq_ref, zk_ref, zv_ref, zg_ref, cos_ref, sin_ref, o_ref, st_ref, *, chunk):
    h = pl.program_id(1)

    @pl.when(pl.program_id(2) == 0)
    def _():
        st_ref[...] = jnp.zeros_like(st_ref)

    gamma = jnp.float32(0.0)
    for hh in range(RET_HEADS):
        gamma = jnp.where(h == hh, jnp.float32(1.0 - 2.0 ** (-5.0 - hh)), gamma)
    lg = jnp.log(jnp.full((1, 1), gamma, F32))

    cos2 = cos_ref[0]
    sin2 = sin_ref[0]
    half = RET_DK // 2

    def rope(z):
        return z * cos2 + pltpu.roll(z, half, 1) * sin2

    q = rope(zq_ref[0]) * (RET_DK ** -0.5)
    k = rope(zk_ref[0])
    v_b = zv_ref[0].astype(BF16)
    row = lax.broadcasted_iota(jnp.int32, (chunk, chunk), 0)
    col = lax.broadcasted_iota(jnp.int32, (chunk, chunk), 1)
    rel = (row - col).astype(F32)
    dmask = jnp.where(rel >= 0.0, jnp.exp(jnp.maximum(rel, 0.0) * lg), 0.0)
    tcol = lax.broadcasted_iota(jnp.int32, (chunk, 1), 0).astype(F32)
    xi = jnp.exp((tcol + 1.0) * lg)
    zeta = jnp.exp((chunk - 1.0 - tcol) * lg)
    st = st_ref[...]
    scores = _dot_nt(q.astype(BF16), k.astype(BF16)) * dmask
    o = _dot(scores.astype(BF16), v_b) + _dot_nt((q * xi).astype(BF16), st.astype(BF16))
    st_ref[...] = st * jnp.exp(chunk * lg) + _dot_tn(v_b, (k * zeta).astype(BF16))
    o = o * lax.rsqrt(jnp.mean(o * o, axis=-1, keepdims=True) + NORM_EPS)
    o_ref[0] = (o * _silu(zg_ref[0])).astype(o_ref.dtype)


def _ret_call(z3, cos2, sin2, chunk=256):
    b, t, _ = z3.shape
    chunk = min(chunk, t)
    base = RET_OFF // LANES

    def zspec(part):
        return pl.BlockSpec((1, chunk, LANES), lambda i, h, j: (i, j, base + part * RET_HEADS + h))

    tab = pl.BlockSpec((1, chunk, RET_DK), lambda i, h, j: (i, j, 0))
    return pl.pallas_call(
        functools.partial(_ret_kernel, chunk=chunk),
        grid=(b, RET_HEADS, t // chunk),
        in_specs=[zspec(0), zspec(1), zspec(2), zspec(3), tab, tab],
        out_specs=pl.BlockSpec((1, chunk, LANES), lambda i, h, j: (i, j, h)),
        out_shape=jax.ShapeDtypeStruct((b, t, RET_W), BF16),
        scratch_shapes=[pltpu.VMEM((RET_DK, RET_DK), F32)],
        compiler_params=_cparams(("parallel", "parallel", "arbitrary")),
        name="retention_mixer",
    )(z3, z3, z3, z3, cos2, sin2)


def _split_bf16(x):
    hi = x.astype(BF16)
    return hi, (x - hi.astype(F32)).astype(BF16)


def _dot_x3(a, b):
    ah, al = _split_bf16(a)
    bh, bl = _split_bf16(b)
    return _dot(ah, bh) + _dot(ah, bl) + _dot(al, bh)


def _dot_x2_lhs(a, b_exact):
    ah, al = _split_bf16(a)
    return _dot(ah, b_exact) + _dot(al, b_exact)


def _dot_x2_rhs(a_exact, b):
    bh, bl = _split_bf16(b)
    return _dot(a_exact, bh) + _dot(a_exact, bl)


def _bdot(a, b):
    return _dot(a.astype(BF16), b.astype(BF16))


def _inv_unit_lower(a, eye, blk_mask):
    c = a[0].shape[0]
    m = range(len(a))
    a_bd = [jnp.where(blk_mask, a[i], 0.0) for i in m]
    a_off = [a[i] - a_bd[i] for i in m]
    a2 = [_bdot(a_bd[i], a_bd[i]) for i in m]
    p = [eye + a_bd[i] for i in m]
    r = [_bdot(jnp.concatenate([p[i], a2[i]], axis=0), a2[i]) for i in m]
    p = [p[i] + r[i][:c] for i in m]
    a4 = [r[i][c:] for i in m]
    r = [_bdot(jnp.concatenate([p[i], a4[i]], axis=0), a4[i]) for i in m]
    p = [p[i] + r[i][:c] for i in m]
    a8 = [r[i][c:] for i in m]
    t_bd = [p[i] + _bdot(p[i], a8[i]) for i in m]
    n = [_bdot(t_bd[i], a_off[i]) for i in m]
    r = [_bdot(n[i], jnp.concatenate([n[i], t_bd[i]], axis=1)) for i in m]
    z = [t_bd[i] + r[i][:, c:] for i in m]
    return [z[i] + _bdot(r[i][:, :c], z[i]) for i in m]


def _rwkv_kernel(z_ref, mu_ref, w0_ref, w2_ref, a0_ref, a2_ref, g2_ref, kk_ref, ka_ref, rk_ref,
                 lnw_ref, lnb_ref, seg_ref, o_ref, s_ref, prev_ref):
    c = RW_CHUNK
    tb = z_ref.shape[1]
    nck = tb // c

    @pl.when(pl.program_id(1) == 0)
    def _():
        s_ref[...] = jnp.zeros_like(s_ref)
        prev_ref[...] = jnp.zeros_like(prev_ref)

    z = z_ref[0]
    rows = lax.broadcasted_iota(jnp.int32, (tb, 1), 0)
    z_prev = jnp.where(rows == 0, prev_ref[...], pltpu.roll(z, 1, 0))
    prev_ref[...] = z[tb - 1:tb]
    zs = z + mu_ref[...] * (z_prev - z)
    r = zs[:, 0:RW_W]
    k = zs[:, RW_W:2 * RW_W]
    v = zs[:, 2 * RW_W:3 * RW_W]
    off = 3 * RW_W
    w_lo = zs[:, off:off + RW_DECAY_LORA]
    a_lo = zs[:, off + RW_DECAY_LORA:off + RW_DECAY_LORA + RW_A_LORA]
    g_lo = zs[:, off + RW_DECAY_LORA + RW_A_LORA:]

    wx = -(w0_ref[...] + _dot_x3(jnp.tanh(w_lo), w2_ref[...]))
    softplus = jnp.maximum(wx, 0.0) + jnp.log(1.0 + jnp.exp(-jnp.abs(wx)))
    logw = -jnp.exp(-softplus - 0.5)
    a = _sigmoid(a0_ref[...] + _dot_x3(a_lo, a2_ref[...]))
    g = _dot_x3(_sigmoid(g_lo), g2_ref[...])
    seg = seg_ref[...]
    kk = k * kk_ref[...]
    kk = kk * lax.rsqrt(jnp.maximum(_dot_x2_lhs(kk * kk, seg), 1e-24))
    k2 = k * (1.0 + (a - 1.0) * ka_ref[...])

    row = lax.broadcasted_iota(jnp.int32, (c, c), 0)
    col = lax.broadcasted_iota(jnp.int32, (c, c), 1)
    blk_mask = (row // RW_BLK) == (col // RW_BLK)
    eye = (row == col).astype(F32)
    row2 = lax.broadcasted_iota(jnp.int32, (c, 2 * c), 0)
    col2 = lax.broadcasted_iota(jnp.int32, (c, 2 * c), 1) % c
    incl2 = row2 >= col2
    strict2 = row2 > col2
    rowb = lax.broadcasted_iota(jnp.int32, (tb, tb), 0)
    colb = lax.broadcasted_iota(jnp.int32, (tb, tb), 1)
    tri = jnp.where(colb >= (rowb // c) * c, jnp.where(rowb >= colb, 1.0, 0.0), 0.0).astype(BF16)
    cw = _dot_x2_rhs(tri, logw)
    w_inv = jnp.exp(-cw)
    last = jnp.concatenate([jnp.broadcast_to(cw[(ci + 1) * c - 1:(ci + 1) * c], (c, RW_W)) for ci in range(nck)],
                           axis=0)
    w_rest = jnp.exp(last - cw)
    beta = a * kk
    alpha_t = -kk * jnp.exp(cw - logw)
    r_t = r * jnp.exp(cw)
    beta_h = beta * w_inv
    k_h = k2 * w_inv
    beta_d = beta * w_rest
    k_d = k2 * w_rest

    hs = range(RW_HEADS)
    ph = [(ci, h) for ci in range(nck) for h in hs]
    m = range(len(ph))
    rs = [slice(ci * c, (ci + 1) * c) for ci, _ in ph]
    sl = [slice(h * RW_N, (h + 1) * RW_N) for _, h in ph]
    v_h = [v[rs[i], sl[i]] for i in m]
    lhs = [jnp.concatenate([alpha_t[rs[i], sl[i]], r_t[rs[i], sl[i]]], axis=0).astype(BF16) for i in m]
    rhs = [jnp.concatenate([beta_h[rs[i], sl[i]], k_h[rs[i], sl[i]]], axis=0).astype(BF16) for i in m]
    big = [_dot_nt(lhs[i], rhs[i]) for i in m]
    a_a = [jnp.where(strict2, big[i][:c], 0.0) for i in m]
    a_r = [jnp.where(incl2, big[i][c:], 0.0).astype(BF16) for i in m]
    t_inv = _inv_unit_lower([a_a[i][:, :c] for i in m], eye, blk_mask)
    av = [_bdot(a_a[i][:, c:], v_h[i]) for i in m]
    bk_d = [jnp.concatenate([beta_d[rs[i], sl[i]], k_d[rs[i], sl[i]]], axis=0).astype(BF16) for i in m]
    s_cur = [s_ref[h] for h in hs]
    o_chunks = []
    for ci in range(nck):
        ix = [ci * RW_HEADS + h for h in hs]
        sd = [_dot_nt(lhs[ix[h]], s_cur[h].astype(BF16)) for h in hs]
        u = [_bdot(t_inv[ix[h]], sd[h][:c] + av[ix[h]]) for h in hs]
        uv = [jnp.concatenate([u[h], v_h[ix[h]]], axis=0).astype(BF16) for h in hs]
        o_chunks.append(jnp.concatenate([sd[h][c:] + _dot(a_r[ix[h]], uv[h]) for h in hs], axis=1))
        w_last = jnp.exp(cw[(ci + 1) * c - 1:(ci + 1) * c])
        s_cur = [s_cur[h] * w_last[:, sl[h]] + _dot_tn(uv[h], bk_d[ix[h]]) for h in hs]
    for h in hs:
        s_ref[h] = s_cur[h]
    o = jnp.concatenate(o_chunks, axis=0)

    mean = _dot_x2_lhs(o, seg) * (1.0 / RW_N)
    dev = o - mean
    var = _dot_x2_lhs(dev * dev, seg) * (1.0 / RW_N)
    o = dev * lax.rsqrt(var + RW_GN_EPS) * lnw_ref[...] + lnb_ref[...]
    bonus = _dot_x2_lhs(r * k2 * rk_ref[...], seg) * v
    o_ref[0] = ((o + bonus) * g).astype(o_ref.dtype)


def _rwkv_call(z3, mu, w0, w2, a0, a2, g2, k_k, k_a, r_k, ln_w, ln_b):
    b, t, _ = z3.shape
    c = min(RW_TB, t)
    hid = lax.broadcasted_iota(jnp.int32, (RW_W, RW_W), 0) // RW_N
    seg = (hid == hid.T).astype(BF16)

    def vec(n):
        return pl.BlockSpec((1, n), lambda i, j: (0, 0))

    def mat(m, n):
        return pl.BlockSpec((m, n), lambda i, j: (0, 0))

    return pl.pallas_call(
        _rwkv_kernel,
        grid=(b, t // c),
        in_specs=[
            pl.BlockSpec((1, c, RW_COLS), lambda i, j: (i, j, RW_OFF // RW_COLS)),
            vec(RW_COLS), vec(RW_W), mat(RW_DECAY_LORA, RW_W), vec(RW_W), mat(RW_A_LORA, RW_W),
            mat(RW_GATE_LORA, RW_W), vec(RW_W), vec(RW_W), vec(RW_W), vec(RW_W), vec(RW_W),
            mat(RW_W, RW_W),
        ],
        out_specs=pl.BlockSpec((1, c, RW_W), lambda i, j: (i, j, 0)),
        out_shape=jax.ShapeDtypeStruct((b, t, RW_W), BF16),
        scratch_shapes=[pltpu.VMEM((RW_HEADS, RW_N, RW_N), F32), pltpu.VMEM((1, RW_COLS), F32)],
        compiler_params=_cparams(("parallel", "arbitrary")),
        name="rwkv7_mixer",
    )(z3, mu.reshape(1, -1), w0.reshape(1, -1), w2, a0.reshape(1, -1), a2, g2, k_k.reshape(1, -1),
      k_a.reshape(1, -1), r_k.reshape(1, -1), ln_w.reshape(1, -1), ln_b.reshape(1, -1), seg)


def _merge_kernel(ohg_ref, oret_ref, orw_ref, zg_ref, x_ref, gate_ref, bhg_ref, bret_ref, brw_ref,
                  wout_ref, o_ref):
    d = x_ref.shape[1]
    y = _sigmoid(zg_ref[:, 0:d]) * _dot(ohg_ref[...], bhg_ref[...])
    y = y + _sigmoid(zg_ref[:, d:2 * d]) * _dot(oret_ref[...], bret_ref[...])
    y = y + _sigmoid(zg_ref[:, 2 * d:3 * d]) * _dot(orw_ref[...], brw_ref[...])
    o_ref[...] = x_ref[...] + gate_ref[0] * _dot(y.astype(BF16), wout_ref[...])


def _merge_call(o_hg, o_ret, o_rw, z2, x2, mod3, br_hg, br_ret, br_rw, w_out, seq, gate_blk, tm=512):
    n, d = x2.shape
    tpb = seq // tm

    def rows(w):
        return pl.BlockSpec((tm, w), lambda i: (i, 0))

    def full(m, k):
        return pl.BlockSpec((m, k), lambda i: (0, 0))

    return pl.pallas_call(
        _merge_kernel,
        grid=(n // tm,),
        in_specs=[
            rows(HG_W), rows(RET_W), rows(RW_W), rows(3 * d), rows(d),
            pl.BlockSpec((1, 1, d), lambda i: (i // tpb, 0, gate_blk)),
            full(HG_W, d), full(RET_W, d), full(RW_W, d), full(d, d),
        ],
        out_specs=rows(d),
        out_shape=jax.ShapeDtypeStruct((n, d), F32),
        compiler_params=_cparams(("parallel",)),
        name="merge_outproj",
    )(o_hg, o_ret, o_rw, z2, x2, mod3, br_hg, br_ret, br_rw, w_out)


def _pack_bf16_pairs(x):
    w = x.shape[1] // 2
    hi = pltpu.bitcast(x[:, :w].astype(BF16).astype(F32), jnp.uint32)
    lo = pltpu.bitcast(x[:, w:].astype(BF16).astype(F32), jnp.uint32)
    return pltpu.bitcast(hi | lax.shift_right_logical(lo, jnp.uint32(16)), jnp.int32)


def _unpack_bf16_pairs(p):
    u = pltpu.bitcast(p, jnp.uint32)
    hi = pltpu.bitcast(u & jnp.uint32(0xFFFF0000), F32)
    lo = pltpu.bitcast(lax.shift_left(u, jnp.uint32(16)), F32)
    return jnp.concatenate([hi, lo], axis=1)


def _route_kernel(x_ref, g_ref, scale_ref, shift_ref, rg_ref, re_ref, hp_ref, eid_ref, wts_ref):
    h = _rms_mod(x_ref[...], g_ref[...], scale_ref[0], shift_ref[0])
    hp_ref[...] = _pack_bf16_pairs(h)
    tm = h.shape[0]
    lane = lax.broadcasted_iota(jnp.int32, (tm, LANES), 1)
    neg = -jnp.inf
    gl = jnp.where(lane < N_GROUPS, _dot(h, rg_ref[...], HIGHEST), neg)
    gmax = jnp.max(gl, axis=-1, keepdims=True)
    gidx = jnp.min(jnp.where(gl == gmax, lane, LANES), axis=-1, keepdims=True)
    gw = 1.0 / jnp.sum(jnp.exp(gl - gmax), axis=-1, keepdims=True)
    lo = gidx * EXPERTS_PER_GROUP
    el = _dot(h, re_ref[...], HIGHEST)
    el = jnp.where(lane >= lo, jnp.where(lane < lo + EXPERTS_PER_GROUP, el, neg), neg)
    m1 = jnp.max(el, axis=-1, keepdims=True)
    i1 = jnp.min(jnp.where(el == m1, lane, LANES), axis=-1, keepdims=True)
    el2 = jnp.where(lane == i1, neg, el)
    m2 = jnp.max(el2, axis=-1, keepdims=True)
    i2 = jnp.min(jnp.where(el2 == m2, lane, LANES), axis=-1, keepdims=True)
    e2 = jnp.exp(m2 - m1)
    p1 = 1.0 / (1.0 + e2)
    p2 = e2 * p1
    eid_ref[...] = jnp.where(lane == 0, i1, jnp.where(lane == 1, i2, 0))
    wts_ref[...] = jnp.where(lane == 0, gw * p1, jnp.where(lane == 1, gw * p2, 0.0))


def _route_call(x2, gain, mod3, router_g, router_e, seq, scale_blk, shift_blk, tm=512):
    n, d = x2.shape
    tpb = seq // tm
    rg = jnp.pad(router_g, ((0, 0), (0, LANES - N_GROUPS)))
    re = jnp.pad(router_e, ((0, 0), (0, LANES - N_EXPERTS)))
    return pl.pallas_call(
        _route_kernel,
        grid=(n // tm,),
        in_specs=[
            pl.BlockSpec((tm, d), lambda i: (i, 0)),
            pl.BlockSpec((1, d), lambda i: (0, 0)),
            pl.BlockSpec((1, 1, d), lambda i: (i // tpb, 0, scale_blk)),
            pl.BlockSpec((1, 1, d), lambda i: (i // tpb, 0, shift_blk)),
            pl.BlockSpec((d, LANES), lambda i: (0, 0)),
            pl.BlockSpec((d, LANES), lambda i: (0, 0)),
        ],
        out_specs=[pl.BlockSpec((tm, d // 2), lambda i: (i, 0)), pl.BlockSpec((tm, LANES), lambda i: (i, 0)),
                   pl.BlockSpec((tm, LANES), lambda i: (i, 0))],
        out_shape=[jax.ShapeDtypeStruct((n, d // 2), jnp.int32), jax.ShapeDtypeStruct((n, LANES), jnp.int32),
                   jax.ShapeDtypeStruct((n, LANES), F32)],
        compiler_params=_cparams(("parallel",)),
        name="moe_route",
    )(x2, gain.reshape(1, d), mod3, mod3, rg, re)


SC_CORES = 2
SC_SUBCORES = 16
SC_WORKERS = SC_CORES * SC_SUBCORES
SC_ROWS = 64


def _sc_gather(table, idx):
    m = idx.shape[0]
    w = table.shape[1]
    per_worker = m // SC_WORKERS
    steps = per_worker // SC_ROWS
    assert per_worker * SC_WORKERS == m and steps * SC_ROWS == per_worker and steps % 2 == 0
    mesh = plsc.VectorSubcoreMesh(core_axis_name="c", subcore_axis_name="s")

    def body(table_hbm, idx_hbm, out_hbm, idx_v, rows_a, rows_b, sem_ga, sem_gb, sem_wa, sem_wb):
        wid = lax.axis_index("s") * SC_CORES + lax.axis_index("c")
        pltpu.sync_copy(idx_hbm.at[wid], idx_v)

        @pl.loop(0, steps, step=2)
        def _(j):
            row0 = wid * per_worker + j * SC_ROWS
            ga = pltpu.async_copy(table_hbm.at[idx_v.at[j]], rows_a, sem_ga)
            gb = pltpu.async_copy(table_hbm.at[idx_v.at[j + 1]], rows_b, sem_gb)
            ga.wait()
            wa = pltpu.async_copy(rows_a, out_hbm.at[pl.ds(row0, SC_ROWS)], sem_wa)
            gb.wait()
            wb = pltpu.async_copy(rows_b, out_hbm.at[pl.ds(row0 + SC_ROWS, SC_ROWS)], sem_wb)
            wa.wait()
            wb.wait()

    return pl.kernel(
        body,
        out_type=jax.ShapeDtypeStruct((m, w), table.dtype),
        mesh=mesh,
        scratch_types=[pltpu.VMEM((steps, SC_ROWS), jnp.int32), pltpu.VMEM((SC_ROWS, w), table.dtype),
                       pltpu.VMEM((SC_ROWS, w), table.dtype), pltpu.SemaphoreType.DMA, pltpu.SemaphoreType.DMA,
                       pltpu.SemaphoreType.DMA, pltpu.SemaphoreType.DMA],
        name="sc_row_gather",
    )(table, idx.reshape(SC_WORKERS, steps, SC_ROWS))


MOE_TM = 512


def _gexperts_kernel(te_ref, nu_ref, xs_ref, w1_ref, w3_ref, w2_ref, ys_ref):
    @pl.when(pl.program_id(0) < nu_ref[0])
    def _():
        xb = _unpack_bf16_pairs(xs_ref[...]).astype(BF16)
        act = _silu(_dot(xb, w1_ref[0])) * _dot(xb, w3_ref[0])
        ys_ref[...] = _pack_bf16_pairs(_dot(act.astype(BF16), w2_ref[0]))


def _gexperts_call(xs, tile_expert, n_used, w1, w3, w2):
    p, half = xs.shape
    ne, d, de = w1.shape
    nt = p // MOE_TM

    def rows(i, te, nu):
        return (jnp.minimum(i, nu[0] - 1), 0)

    def wsel(i, te, nu):
        return (te[i], 0, 0)

    return pl.pallas_call(
        _gexperts_kernel,
        grid_spec=pltpu.PrefetchScalarGridSpec(
            num_scalar_prefetch=2,
            grid=(nt,),
            in_specs=[
                pl.BlockSpec((MOE_TM, half), rows),
                pl.BlockSpec((1, d, de), wsel),
                pl.BlockSpec((1, d, de), wsel),
                pl.BlockSpec((1, de, d), wsel),
            ],
            out_specs=pl.BlockSpec((MOE_TM, half), rows),
        ),
        out_shape=jax.ShapeDtypeStruct((p, half), jnp.int32),
        compiler_params=_cparams(("arbitrary",)),
        name="moe_experts",
    )(tile_expert, n_used, xs, w1, w3, w2)


def _combine_kernel(y0_ref, y1_ref, wts_ref, x_ref, gate_ref, fg_ref, o_ref, *, final_norm):
    wts = wts_ref[...]
    moe = wts[:, 0:1] * _unpack_bf16_pairs(y0_ref[...]) + wts[:, 1:2] * _unpack_bf16_pairs(y1_ref[...])
    xn = x_ref[...] + gate_ref[0] * moe
    if final_norm:
        xn = xn * lax.rsqrt(jnp.mean(xn * xn, axis=-1, keepdims=True) + NORM_EPS) * fg_ref[...]
    o_ref[...] = xn


def _combine_call(yg, wts, x2, mod3, final_g, seq, gate_blk, final_norm, tm=512):
    n, d = x2.shape
    tpb = seq // tm
    slot1 = n // tm
    return pl.pallas_call(
        functools.partial(_combine_kernel, final_norm=final_norm),
        grid=(n // tm,),
        in_specs=[
            pl.BlockSpec((tm, d // 2), lambda i: (i, 0)),
            pl.BlockSpec((tm, d // 2), lambda i: (i + slot1, 0)),
            pl.BlockSpec((tm, LANES), lambda i: (i, 0)),
            pl.BlockSpec((tm, d), lambda i: (i, 0)),
            pl.BlockSpec((1, 1, d), lambda i: (i // tpb, 0, gate_blk)),
            pl.BlockSpec((1, d), lambda i: (0, 0)),
        ],
        out_specs=pl.BlockSpec((tm, d), lambda i: (i, 0)),
        out_shape=jax.ShapeDtypeStruct((n, d), F32),
        compiler_params=_cparams(("parallel",)),
        name="moe_combine",
    )(yg, yg, wts, x2, mod3, final_g.reshape(1, d))


def _moe_plan(eid):
    n = eid.shape[0]
    na = 2 * n
    nt = na // MOE_TM + N_EXPERTS
    e_flat = jnp.concatenate([eid[:, 0], eid[:, 1]])
    iota_a = jnp.arange(na, dtype=jnp.int32)
    e_sorted, order = lax.sort((e_flat, iota_a), num_keys=1, is_stable=True)
    experts = jnp.arange(N_EXPERTS, dtype=jnp.int32)
    counts = jnp.sum(e_flat[:, None] == experts[None, :], axis=0, dtype=jnp.int32)
    tiles = (counts + MOE_TM - 1) // MOE_TM
    tile_end = jnp.cumsum(tiles)
    tile_start = tile_end - tiles
    row_start = jnp.cumsum(counts) - counts
    n_used = tile_end[-1:]
    tile_id = jnp.minimum(jnp.arange(nt, dtype=jnp.int32), n_used - 1)
    tile_expert = jnp.sum(tile_id[:, None] >= tile_end[None, :], axis=1, dtype=jnp.int32)
    e_row = jnp.repeat(tile_expert, MOE_TM)
    rank = jnp.arange(nt * MOE_TM, dtype=jnp.int32) - jnp.repeat(tile_start[tile_expert], MOE_TM) * MOE_TM
    valid = rank < counts[e_row]
    src_token = jnp.where(valid, order[jnp.minimum(row_start[e_row] + rank, na - 1)] % n, 0)
    p_sorted = tile_start[e_sorted] * MOE_TM + iota_a - row_start[e_sorted]
    _, pos = lax.sort((order, p_sorted), num_keys=1)
    return src_token, pos, tile_expert, n_used


def kernel(x, c, positions, ada_w, ada_b, norm1_g, norm2_g, w_in, hg_lb_table, hg_norm_w, rw_mu, rw_w0, rw_w2,
           rw_a0, rw_a2, rw_g2, rw_k_k, rw_k_a, rw_r_k, rw_ln_w, rw_ln_b, br_hg, br_ret, br_rw, w_out,
           router_g, router_e, moe_w1, moe_w3, moe_w2, final_g):
    b, t, d = x.shape
    depth = ada_w.shape[0]
    n = b * t
    assert w_in.shape[2] == IN_COLS and d == 1024

    lb_p = jax.nn.softmax(hg_lb_table.astype(F32), axis=0)
    lower_bounds = jnp.cumsum(lb_p, axis=0) - lb_p[0]

    mod = _mod_call(c, ada_w, ada_b)
    cos2, sin2 = _rope_call(positions, RET_DK)
    n_gate = 3 * d
    x2 = x.reshape(n, d)
    for l in range(depth):
        mod3 = mod[l].reshape(b, 1, 6 * d)
        w_perm = jnp.concatenate([w_in[l][:, IN_COLS - n_gate:], w_in[l][:, :IN_COLS - n_gate]], axis=1)
        z2 = _inproj_call(x2, norm1_g[l], mod3, w_perm.astype(BF16), t, scale_blk=1, shift_blk=0)
        z3 = z2.reshape(b, t, IN_COLS)
        o_hg = _hgrn2_call(z3, lower_bounds[l], hg_norm_w[l])
        o_ret = _ret_call(z3, cos2, sin2)
        o_rw = _rwkv_call(z3, rw_mu[l], rw_w0[l], rw_w2[l], rw_a0[l], rw_a2[l], rw_g2[l], rw_k_k[l],
                          rw_k_a[l], rw_r_k[l], rw_ln_w[l], rw_ln_b[l])
        x2 = _merge_call(o_hg.reshape(n, HG_W), o_ret.reshape(n, RET_W), o_rw.reshape(n, RW_W), z2, x2, mod3,
                         br_hg[l].astype(BF16), br_ret[l].astype(BF16), br_rw[l].astype(BF16),
                         w_out[l].astype(BF16), t, gate_blk=2)
        hp, eid, wts = _route_call(x2, norm2_g[l], mod3, router_g[l], router_e[l], t, scale_blk=4, shift_blk=3)
        src_token, pos, tile_expert, n_used = _moe_plan(eid)
        xs = _sc_gather(hp, src_token)
        ys = _gexperts_call(xs, tile_expert, n_used, moe_w1[l].astype(BF16), moe_w3[l].astype(BF16),
                            moe_w2[l].astype(BF16))
        yg = _sc_gather(ys, pos)
        x2 = _combine_call(yg, wts, x2, mod3, final_g, t, gate_blk=5, final_norm=(l == depth - 1))
    return x2.reshape(b, t, d)
```

```python
import functools

import jax
import jax.numpy as jnp
from jax import lax
from jax.experimental import pallas as pl
from jax.experimental.pallas import tpu as pltpu
from jax.experimental.pallas import tpu_sc as plsc

F32 = jnp.float32
BF16 = jnp.bfloat16
HIGHEST = lax.Precision.HIGHEST

HG_HEADS = 4
HG_DK = 128
HG_W = HG_HEADS * HG_DK
RET_HEADS = 4
RET_DK = 128
RET_W = RET_HEADS * RET_DK
RW_HEADS = 8
RW_N = 64
RW_W = RW_HEADS * RW_N
RW_DECAY_LORA = 64
RW_A_LORA = 64
RW_GATE_LORA = 128
RW_COLS = 3 * RW_W + RW_DECAY_LORA + RW_A_LORA + RW_GATE_LORA
RW_GN_EPS = 64e-5
N_GROUPS = 4
EXPERTS_PER_GROUP = 8
N_EXPERTS = N_GROUPS * EXPERTS_PER_GROUP
ROPE_THETA = 10000.0
NORM_EPS = 1e-6

LANES = 128
VMEM_LIMIT = 56 * 1024 * 1024

GATE_OFF = 0
HG_OFF = 3 * 1024
RET_OFF = HG_OFF + 4 * HG_W
RW_OFF = RET_OFF + 4 * RET_W
IN_COLS = RW_OFF + RW_COLS

HG_CHUNK = 64
HG_SUB = 16
RW_CHUNK = 64
RW_TB = 128
RW_BLK = 16


def _cparams(sem):
    return pltpu.CompilerParams(dimension_semantics=sem, vmem_limit_bytes=VMEM_LIMIT)


def _dot(a, b, precision=None):
    return jnp.dot(a, b, preferred_element_type=F32, precision=precision)


def _dot_nt(a, b, precision=None):
    return lax.dot_general(a, b, (((1,), (1,)), ((), ())), preferred_element_type=F32, precision=precision)


def _dot_tn(a, b, precision=None):
    return lax.dot_general(a, b, (((0,), (0,)), ((), ())), preferred_element_type=F32, precision=precision)


def _sigmoid(x):
    return 0.5 * jnp.tanh(0.5 * x) + 0.5


def _silu(x):
    return x * _sigmoid(x)


def _rms_mod(x, gain, scale, shift):
    y = x * lax.rsqrt(jnp.mean(x * x, axis=-1, keepdims=True) + NORM_EPS)
    return (y * gain) * (1.0 + scale) + shift


def _mod_kernel(c_ref, w_ref, b_ref, o_ref):
    c = c_ref[...]
    o_ref[0] = _dot(_silu(c), w_ref[0], HIGHEST) + b_ref[0]


def _mod_call(c, ada_w, ada_b):
    depth, d, d6 = ada_w.shape
    b = c.shape[0]
    nblk = d6 // d
    return pl.pallas_call(
        _mod_kernel,
        grid=(depth, nblk),
        in_specs=[
            pl.BlockSpec((b, d), lambda l, j: (0, 0)),
            pl.BlockSpec((1, d, d), lambda l, j: (l, 0, j)),
            pl.BlockSpec((1, 1, d), lambda l, j: (l, 0, j)),
        ],
        out_specs=pl.BlockSpec((1, b, d), lambda l, j: (l, 0, j)),
        out_shape=jax.ShapeDtypeStruct((depth, b, d6), F32),
        compiler_params=_cparams(("parallel", "parallel")),
        name="adaln_mod",
    )(c, ada_w, ada_b.reshape(depth, 1, d6))


def _rope_kernel(pos_ref, freq_ref, sign_ref, cos_ref, sin_ref):
    ang = pos_ref[0].astype(F32) * freq_ref[...]
    cos_ref[0] = jnp.cos(ang)
    sin_ref[0] = jnp.sin(ang) * sign_ref[...]


def _rope_call(positions, d):
    b, t = positions.shape
    tb = min(t, 512)
    inv_freq = ROPE_THETA ** (-jnp.arange(0, d, 2, dtype=F32) / d)
    freq2 = jnp.concatenate([inv_freq, inv_freq]).reshape(1, d)
    sign2 = jnp.concatenate([-jnp.ones((d // 2,), F32), jnp.ones((d // 2,), F32)]).reshape(1, d)
    out = jax.ShapeDtypeStruct((b, t, d), F32)
    return pl.pallas_call(
        _rope_kernel,
        grid=(b, t // tb),
        in_specs=[
            pl.BlockSpec((1, tb, 1), lambda i, j: (i, j, 0)),
            pl.BlockSpec((1, d), lambda i, j: (0, 0)),
            pl.BlockSpec((1, d), lambda i, j: (0, 0)),
        ],
        out_specs=[pl.BlockSpec((1, tb, d), lambda i, j: (i, j, 0))] * 2,
        out_shape=[out, out],
        compiler_params=_cparams(("parallel", "parallel")),
        name="rope_tables",
    )(positions.reshape(b, t, 1), freq2, sign2)


def _inproj_kernel(x_ref, g_ref, scale_ref, shift_ref, w_ref, o_ref, h_ref):
    @pl.when(pl.program_id(1) == 0)
    def _():
        h = _rms_mod(x_ref[...], g_ref[...], scale_ref[0], shift_ref[0])
        h_ref[...] = h.astype(BF16)

    o_ref[...] = _dot(h_ref[...], w_ref[...])


def _inproj_call(x2, gain, mod3, w_bf16, seq, scale_blk, shift_blk, tm=1024, tn=1280):
    n, d = x2.shape
    cols = w_bf16.shape[1]
    tpb = seq // tm
    return pl.pallas_call(
        _inproj_kernel,
        grid=(n // tm, cols // tn),
        in_specs=[
            pl.BlockSpec((tm, d), lambda i, j: (i, 0)),
            pl.BlockSpec((1, d), lambda i, j: (0, 0)),
            pl.BlockSpec((1, 1, d), lambda i, j: (i // tpb, 0, scale_blk)),
            pl.BlockSpec((1, 1, d), lambda i, j: (i // tpb, 0, shift_blk)),
            pl.BlockSpec((d, tn), lambda i, j: (0, j)),
        ],
        out_specs=pl.BlockSpec((tm, tn), lambda i, j: (i, j)),
        out_shape=jax.ShapeDtypeStruct((n, cols), F32),
        scratch_shapes=[pltpu.VMEM((tm, d), BF16)],
        compiler_params=_cparams(("parallel", "arbitrary")),
        name="norm_inproj",
    )(x2, gain.reshape(1, d), mod3, mod3, w_bf16)


def _hgrn2_block(zq, zf, zi, zg, lb, nw, st):
    tb = zq.shape[0]
    c, sub = HG_CHUNK, HG_SUB
    nc, ns = tb // c, c // sub
    f = lb + (1.0 - lb) * _sigmoid(zf)
    logf = jnp.log(jnp.maximum(f, 1e-30))
    q = _silu(zq) * (HG_DK ** -0.5)
    k = 1.0 - f
    v = zi
    v_b = v.astype(BF16)
    row = lax.broadcasted_iota(jnp.int32, (tb, tb), 0)
    col = lax.broadcasted_iota(jnp.int32, (tb, tb), 1)
    tri = jnp.where(col >= (row // c) * c, jnp.where(row >= col, 1.0, 0.0), 0.0).astype(BF16)
    cum = _dot_x2_rhs(tri, logf)
    qe = (q * jnp.exp(cum)).astype(BF16)

    offd = [(ci * c, ci * c + sub * i) for ci in range(nc) for i in range(1, ns)]
    base = [cum[lo - 1:lo] for _, lo in offd]
    qt = [(q[lo:lo + sub] * jnp.exp(cum[lo:lo + sub] - base[j])).astype(BF16) for j, (_, lo) in enumerate(offd)]
    kt = [(k[r0:lo] * jnp.exp(base[j] - cum[r0:lo])).astype(BF16) for j, (r0, lo) in enumerate(offd)]
    a = [_dot_nt(qt[j], kt[j]).astype(BF16) for j in range(len(offd))]
    av = {lo: _dot(a[j], v_b[r0:lo]) for j, (r0, lo) in enumerate(offd)}

    nb = tb // sub
    cum3, q3, k3, v3 = (x.reshape(nb, sub, HG_DK) for x in (cum, q, k, v))
    trow = lax.broadcasted_iota(jnp.int32, (nb, sub, 1), 1)
    diag = jnp.zeros((nb, sub, HG_DK), F32)
    for s in range(sub):
        e = jnp.exp(jnp.minimum(cum3 - cum3[:, s:s + 1, :], 0.0))
        a_col = jnp.sum(q3 * k3[:, s:s + 1, :] * e, axis=-1, keepdims=True)
        diag = diag + jnp.where(trow >= s, a_col, 0.0) * v3[:, s:s + 1, :]
    diag = diag.reshape(tb, HG_DK)

    outs = []
    for ci in range(nc):
        r0 = ci * c
        o_inter = _dot_nt(qe[r0:r0 + c], st.astype(BF16))
        for i in range(ns):
            lo = r0 + sub * i
            piece = o_inter[sub * i:sub * (i + 1)] + diag[lo:lo + sub]
            outs.append(piece + av[lo] if i > 0 else piece)
        last = cum[r0 + c - 1:r0 + c]
        kd = (k[r0:r0 + c] * jnp.exp(last - cum[r0:r0 + c])).astype(BF16)
        st = st * jnp.exp(last) + _dot_tn(v_b[r0:r0 + c], kd)
    o = jnp.concatenate(outs, axis=0)
    o = o * lax.rsqrt(jnp.mean(o * o, axis=-1, keepdims=True) + NORM_EPS)
    o = o * nw * _silu(zg)
    return o, st


def _hgrn2_kernel(zq_ref, zf_ref, zi_ref, zg_ref, lb_ref, nw_ref, o_ref, st_ref):
    @pl.when(pl.program_id(2) == 0)
    def _():
        st_ref[...] = jnp.zeros_like(st_ref)

    o, st_new = _hgrn2_block(zq_ref[0], zf_ref[0], zi_ref[0], zg_ref[0], lb_ref[...], nw_ref[...], st_ref[...])
    st_ref[...] = st_new
    o_ref[0] = o.astype(o_ref.dtype)


def _hgrn2_call(z3, lower_bound, norm_w, tb=256):
    b, t, _ = z3.shape
    tb = min(tb, t)
    base = HG_OFF // LANES

    def zspec(part):
        return pl.BlockSpec((1, tb, LANES), lambda i, h, j: (i, j, base + part * HG_HEADS + h))

    return pl.pallas_call(
        _hgrn2_kernel,
        grid=(b, HG_HEADS, t // tb),
        in_specs=[
            zspec(0), zspec(1), zspec(2), zspec(3),
            pl.BlockSpec((1, LANES), lambda i, h, j: (0, h)),
            pl.BlockSpec((1, LANES), lambda i, h, j: (0, 0)),
        ],
        out_specs=pl.BlockSpec((1, tb, LANES), lambda i, h, j: (i, j, h)),
        out_shape=jax.ShapeDtypeStruct((b, t, HG_W), BF16),
        scratch_shapes=[pltpu.VMEM((HG_DK, HG_DK), F32)],
        compiler_params=_cparams(("parallel", "parallel", "arbitrary")),
        name="hgrn2_mixer",
    )(z3, z3, z3, z3, lower_bound.reshape(1, HG_W), norm_w.reshape(1, HG_DK))


def _ret_kernel(zq_ref, zk_ref, zv_ref, zg_ref, cos_ref, sin_ref, o_ref, st_ref, *, chunk):
    @pl.when(pl.program_id(1) == 0)
    def _():
        st_ref[...] = jnp.zeros_like(st_ref)

    cos2 = cos_ref[0]
    sin2 = sin_ref[0]
    half = RET_DK // 2
    hs = range(RET_HEADS)
    sl = [slice(h * RET_DK, (h + 1) * RET_DK) for h in hs]
    lg = [jnp.log(jnp.full((1, 1), 1.0 - 2.0 ** (-5.0 - h), F32)) for h in hs]

    def rope(z):
        return z * cos2 + pltpu.roll(z, half, 1) * sin2

    row = lax.broadcasted_iota(jnp.int32, (chunk, chunk), 0)
    col = lax.broadcasted_iota(jnp.int32, (chunk, chunk), 1)
    rel = (row - col).astype(F32)
    relp = jnp.maximum(rel, 0.0)
    tcol = lax.broadcasted_iota(jnp.int32, (chunk, 1), 0).astype(F32)
    q = [rope(zq_ref[0, :, sl[h]]) * (RET_DK ** -0.5) for h in hs]
    k = [rope(zk_ref[0, :, sl[h]]) for h in hs]
    v_b = [zv_ref[0, :, sl[h]].astype(BF16) for h in hs]
    st = [st_ref[h] for h in hs]
    dmask = [jnp.where(rel >= 0.0, jnp.exp(relp * lg[h]), 0.0) for h in hs]
    scores = [(_dot_nt(q[h].astype(BF16), k[h].astype(BF16)) * dmask[h]).astype(BF16) for h in hs]
    qx = [(q[h] * jnp.exp((tcol + 1.0) * lg[h])).astype(BF16) for h in hs]
    kz = [(k[h] * jnp.exp((chunk - 1.0 - tcol) * lg[h])).astype(BF16) for h in hs]
    o = [_dot(scores[h], v_b[h]) + _dot_nt(qx[h], st[h].astype(BF16)) for h in hs]
    for h in hs:
        st_ref[h] = st[h] * jnp.exp(chunk * lg[h]) + _dot_tn(v_b[h], kz[h])
    o = [o[h] * lax.rsqrt(jnp.mean(o[h] * o[h], axis=-1, keepdims=True) + NORM_EPS) for h in hs]
    o_ref[0] = (jnp.concatenate(o, axis=1) * _silu(zg_ref[0])).astype(o_ref.dtype)


def _ret_call(z3, cos2, sin2, chunk=256):
    b, t, _ = z3.shape
    chunk = min(chunk, t)
    base = RET_OFF // RET_W

    def zspec(part):
        return pl.BlockSpec((1, chunk, RET_W), lambda i, j: (i, j, base + part))

    tab = pl.BlockSpec((1, chunk, RET_DK), lambda i, j: (i, j, 0))
    return pl.pallas_call(
        functools.partial(_ret_kernel, chunk=chunk),
        grid=(b, t // chunk),
        in_specs=[zspec(0), zspec(1), zspec(2), zspec(3), tab, tab],
        out_specs=pl.BlockSpec((1, chunk, RET_W), lambda i, j: (i, j, 0)),
        out_shape=jax.ShapeDtypeStruct((b, t, RET_W), BF16),
        scratch_shapes=[pltpu.VMEM((RET_HEADS, RET_DK, RET_DK), F32)],
        compiler_params=_cparams(("parallel", "arbitrary")),
        name="retention_mixer",
    )(z3, z3, z3, z3, cos2, sin2)


def _split_bf16(x):
    hi = x.astype(BF16)
    return hi, (x - hi.astype(F32)).astype(BF16)


def _dot_x3(a, b):
    ah, al = _split_bf16(a)
    bh, bl = _split_bf16(b)
    return _dot(ah, bh) + _dot(ah, bl) + _dot(al, bh)


def _dot_x2_lhs(a, b_exact):
    ah, al = _split_bf16(a)
    return _dot(ah, b_exact) + _dot(al, b_exact)


def _dot_x2_rhs(a_exact, b):
    bh, bl = _split_bf16(b)
    return _dot(a_exact, bh) + _dot(a_exact, bl)


def _bdot(a, b):
    return _dot(a.astype(BF16), b.astype(BF16))


def _inv_unit_lower(a, eye, blk_mask):
    c = a[0].shape[0]
    m = range(len(a))
    a_bd = [jnp.where(blk_mask, a[i], 0.0) for i in m]
    a_off = [a[i] - a_bd[i] for i in m]
    a2 = [_bdot(a_bd[i], a_bd[i]) for i in m]
    p = [eye + a_bd[i] for i in m]
    r = [_bdot(jnp.concatenate([p[i], a2[i]], axis=0), a2[i]) for i in m]
    p = [p[i] + r[i][:c] for i in m]
    a4 = [r[i][c:] for i in m]
    r = [_bdot(jnp.concatenate([p[i], a4[i]], axis=0), a4[i]) for i in m]
    p = [p[i] + r[i][:c] for i in m]
    a8 = [r[i][c:] for i in m]
    t_bd = [p[i] + _bdot(p[i], a8[i]) for i in m]
    n = [_bdot(t_bd[i], a_off[i]) for i in m]
    r = [_bdot(n[i], jnp.concatenate([n[i], t_bd[i]], axis=1)) for i in m]
    z = [t_bd[i] + r[i][:, c:] for i in m]
    return [z[i] + _bdot(r[i][:, :c], z[i]) for i in m]


def _rwkv_kernel(z_ref, mu_ref, w0_ref, w2_ref, a0_ref, a2_ref, g2_ref, kk_ref, ka_ref, rk_ref,
                 lnw_ref, lnb_ref, seg_ref, o_ref, s_ref, prev_ref):
    c = RW_CHUNK
    tb = z_ref.shape[1]
    nck = tb // c

    @pl.when(pl.program_id(1) == 0)
    def _():
        s_ref[...] = jnp.zeros_like(s_ref)
        prev_ref[...] = jnp.zeros_like(prev_ref)

    z = z_ref[0]
    rows = lax.broadcasted_iota(jnp.int32, (tb, 1), 0)
    z_prev = jnp.where(rows == 0, prev_ref[...], pltpu.roll(z, 1, 0))
    prev_ref[...] = z[tb - 1:tb]
    zs = z + mu_ref[...] * (z_prev - z)
    r = zs[:, 0:RW_W]
    k = zs[:, RW_W:2 * RW_W]
    v = zs[:, 2 * RW_W:3 * RW_W]
    off = 3 * RW_W
    w_lo = zs[:, off:off + RW_DECAY_LORA]
    a_lo = zs[:, off + RW_DECAY_LORA:off + RW_DECAY_LORA + RW_A_LORA]
    g_lo = zs[:, off + RW_DECAY_LORA + RW_A_LORA:]

    wx = -(w0_ref[...] + _dot_x3(jnp.tanh(w_lo), w2_ref[...]))
    softplus = jnp.maximum(wx, 0.0) + jnp.log(1.0 + jnp.exp(-jnp.abs(wx)))
    logw = -jnp.exp(-softplus - 0.5)
    a = _sigmoid(a0_ref[...] + _dot_x3(a_lo, a2_ref[...]))
    g = _dot_x3(_sigmoid(g_lo), g2_ref[...])
    seg = seg_ref[...]
    kk = k * kk_ref[...]
    kk = kk * lax.rsqrt(jnp.maximum(_dot_x2_lhs(kk * kk, seg), 1e-24))
    k2 = k * (1.0 + (a - 1.0) * ka_ref[...])

    row = lax.broadcasted_iota(jnp.int32, (c, c), 0)
    col = lax.broadcasted_iota(jnp.int32, (c, c), 1)
    blk_mask = (row // RW_BLK) == (col // RW_BLK)
    eye = (row == col).astype(F32)
    row2 = lax.broadcasted_iota(jnp.int32, (c, 2 * c), 0)
    col2 = lax.broadcasted_iota(jnp.int32, (c, 2 * c), 1) % c
    incl2 = row2 >= col2
    strict2 = row2 > col2
    rowb = lax.broadcasted_iota(jnp.int32, (tb, tb), 0)
    colb = lax.broadcasted_iota(jnp.int32, (tb, tb), 1)
    tri = jnp.where(colb >= (rowb // c) * c, jnp.where(rowb >= colb, 1.0, 0.0), 0.0).astype(BF16)
    cw = _dot_x2_rhs(tri, logw)
    w_inv = jnp.exp(-cw)
    last = jnp.concatenate([jnp.broadcast_to(cw[(ci + 1) * c - 1:(ci + 1) * c], (c, RW_W)) for ci in range(nck)],
                           axis=0)
    w_rest = jnp.exp(last - cw)
    beta = a * kk
    alpha_t = -kk * jnp.exp(cw - logw)
    r_t = r * jnp.exp(cw)
    beta_h = beta * w_inv
    k_h = k2 * w_inv
    beta_d = beta * w_rest
    k_d = k2 * w_rest

    hs = range(RW_HEADS)
    ph = [(ci, h) for ci in range(nck) for h in hs]
    m = range(len(ph))
    rs = [slice(ci * c, (ci + 1) * c) for ci, _ in ph]
    sl = [slice(h * RW_N, (h + 1) * RW_N) for _, h in ph]
    v_h = [v[rs[i], sl[i]] for i in m]
    lhs = [jnp.concatenate([alpha_t[rs[i], sl[i]], r_t[rs[i], sl[i]]], axis=0).astype(BF16) for i in m]
    rhs = [jnp.concatenate([beta_h[rs[i], sl[i]], k_h[rs[i], sl[i]]], axis=0).astype(BF16) for i in m]
    big = [_dot_nt(lhs[i], rhs[i]) for i in m]
    a_a = [jnp.where(strict2, big[i][:c], 0.0) for i in m]
    a_r = [jnp.where(incl2, big[i][c:], 0.0).astype(BF16) for i in m]
    t_inv = _inv_unit_lower([a_a[i][:, :c] for i in m], eye, blk_mask)
    av = [_bdot(a_a[i][:, c:], v_h[i]) for i in m]
    bk_d = [jnp.concatenate([beta_d[rs[i], sl[i]], k_d[rs[i], sl[i]]], axis=0).astype(BF16) for i in m]
    s_cur = [s_ref[h] for h in hs]
    o_chunks = []
    for ci in range(nck):
        ix = [ci * RW_HEADS + h for h in hs]
        sd = [_dot_nt(lhs[ix[h]], s_cur[h].astype(BF16)) for h in hs]
        u = [_bdot(t_inv[ix[h]], sd[h][:c] + av[ix[h]]) for h in hs]
        uv = [jnp.concatenate([u[h], v_h[ix[h]]], axis=0).astype(BF16) for h in hs]
        o_chunks.append(jnp.concatenate([sd[h][c:] + _dot(a_r[ix[h]], uv[h]) for h in hs], axis=1))
        w_last = jnp.exp(cw[(ci + 1) * c - 1:(ci + 1) * c])
        s_cur = [s_cur[h] * w_last[:, sl[h]] + _dot_tn(uv[h], bk_d[ix[h]]) for h in hs]
    for h in hs:
        s_ref[h] = s_cur[h]
    o = jnp.concatenate(o_chunks, axis=0)

    mean = _dot_x2_lhs(o, seg) * (1.0 / RW_N)
    dev = o - mean
    var = _dot_x2_lhs(dev * dev, seg) * (1.0 / RW_N)
    o = dev * lax.rsqrt(var + RW_GN_EPS) * lnw_ref[...] + lnb_ref[...]
    bonus = _dot_x2_lhs(r * k2 * rk_ref[...], seg) * v
    o_ref[0] = ((o + bonus) * g).astype(o_ref.dtype)


def _rwkv_call(z3, mu, w0, w2, a0, a2, g2, k_k, k_a, r_k, ln_w, ln_b):
    b, t, _ = z3.shape
    c = min(RW_TB, t)
    hid = lax.broadcasted_iota(jnp.int32, (RW_W, RW_W), 0) // RW_N
    seg = (hid == hid.T).astype(BF16)

    def vec(n):
        return pl.BlockSpec((1, n), lambda i, j: (0, 0))

    def mat(m, n):
        return pl.BlockSpec((m, n), lambda i, j: (0, 0))

    return pl.pallas_call(
        _rwkv_kernel,
        grid=(b, t // c),
        in_specs=[
            pl.BlockSpec((1, c, RW_COLS), lambda i, j: (i, j, RW_OFF // RW_COLS)),
            vec(RW_COLS), vec(RW_W), mat(RW_DECAY_LORA, RW_W), vec(RW_W), mat(RW_A_LORA, RW_W),
            mat(RW_GATE_LORA, RW_W), vec(RW_W), vec(RW_W), vec(RW_W), vec(RW_W), vec(RW_W),
            mat(RW_W, RW_W),
        ],
        out_specs=pl.BlockSpec((1, c, RW_W), lambda i, j: (i, j, 0)),
        out_shape=jax.ShapeDtypeStruct((b, t, RW_W), BF16),
        scratch_shapes=[pltpu.VMEM((RW_HEADS, RW_N, RW_N), F32), pltpu.VMEM((1, RW_COLS), F32)],
        compiler_params=_cparams(("parallel", "arbitrary")),
        name="rwkv7_mixer",
    )(z3, mu.reshape(1, -1), w0.reshape(1, -1), w2, a0.reshape(1, -1), a2, g2, k_k.reshape(1, -1),
      k_a.reshape(1, -1), r_k.reshape(1, -1), ln_w.reshape(1, -1), ln_b.reshape(1, -1), seg)


def _merge_kernel(ohg_ref, oret_ref, orw_ref, zg_ref, x_ref, gate_ref, bhg_ref, bret_ref, brw_ref,
                  wout_ref, o_ref):
    d = x_ref.shape[1]
    y = _sigmoid(zg_ref[:, 0:d]) * _dot(ohg_ref[...], bhg_ref[...])
    y = y + _sigmoid(zg_ref[:, d:2 * d]) * _dot(oret_ref[...], bret_ref[...])
    y = y + _sigmoid(zg_ref[:, 2 * d:3 * d]) * _dot(orw_ref[...], brw_ref[...])
    o_ref[...] = x_ref[...] + gate_ref[0] * _dot(y.astype(BF16), wout_ref[...])


def _merge_call(o_hg, o_ret, o_rw, z2, x2, mod3, br_hg, br_ret, br_rw, w_out, seq, gate_blk, tm=512):
    n, d = x2.shape
    tpb = seq // tm

    def rows(w):
        return pl.BlockSpec((tm, w), lambda i: (i, 0))

    def full(m, k):
        return pl.BlockSpec((m, k), lambda i: (0, 0))

    return pl.pallas_call(
        _merge_kernel,
        grid=(n // tm,),
        in_specs=[
            rows(HG_W), rows(RET_W), rows(RW_W), rows(3 * d), rows(d),
            pl.BlockSpec((1, 1, d), lambda i: (i // tpb, 0, gate_blk)),
            full(HG_W, d), full(RET_W, d), full(RW_W, d), full(d, d),
        ],
        out_specs=rows(d),
        out_shape=jax.ShapeDtypeStruct((n, d), F32),
        compiler_params=_cparams(("parallel",)),
        name="merge_outproj",
    )(o_hg, o_ret, o_rw, z2, x2, mod3, br_hg, br_ret, br_rw, w_out)


def _pack_bf16_pairs(x):
    w = x.shape[1] // 2
    hi = pltpu.bitcast(x[:, :w].astype(BF16).astype(F32), jnp.uint32)
    lo = pltpu.bitcast(x[:, w:].astype(BF16).astype(F32), jnp.uint32)
    return pltpu.bitcast(hi | lax.shift_right_logical(lo, jnp.uint32(16)), jnp.int32)


def _unpack_bf16_pairs(p):
    u = pltpu.bitcast(p, jnp.uint32)
    hi = pltpu.bitcast(u & jnp.uint32(0xFFFF0000), F32)
    lo = pltpu.bitcast(lax.shift_left(u, jnp.uint32(16)), F32)
    return jnp.concatenate([hi, lo], axis=1)


def _route_kernel(x_ref, g_ref, scale_ref, shift_ref, rc_ref, hp_ref, eid_ref, wts_ref):
    h = _rms_mod(x_ref[...], g_ref[...], scale_ref[0], shift_ref[0])
    hp_ref[...] = _pack_bf16_pairs(h)
    tm = h.shape[0]
    lane = lax.broadcasted_iota(jnp.int32, (tm, LANES), 1)
    neg = -jnp.inf
    logits = _dot_x3(h, rc_ref[...])
    gl = jnp.where(lane < N_GROUPS, logits, neg)
    gmax = jnp.max(gl, axis=-1, keepdims=True)
    gidx = jnp.min(jnp.where(gl == gmax, lane, LANES), axis=-1, keepdims=True)
    gw = 1.0 / jnp.sum(jnp.exp(gl - gmax), axis=-1, keepdims=True)
    lo = N_GROUPS + gidx * EXPERTS_PER_GROUP
    el = jnp.where(lane >= lo, jnp.where(lane < lo + EXPERTS_PER_GROUP, logits, neg), neg)
    m1 = jnp.max(el, axis=-1, keepdims=True)
    l1 = jnp.min(jnp.where(el == m1, lane, LANES), axis=-1, keepdims=True)
    el2 = jnp.where(lane == l1, neg, el)
    m2 = jnp.max(el2, axis=-1, keepdims=True)
    l2 = jnp.min(jnp.where(el2 == m2, lane, LANES), axis=-1, keepdims=True)
    i1 = l1 - N_GROUPS
    i2 = l2 - N_GROUPS
    e2 = jnp.exp(m2 - m1)
    p1 = 1.0 / (1.0 + e2)
    p2 = e2 * p1
    eid_ref[...] = jnp.where(lane == 0, i1, jnp.where(lane == 1, i2, 0))
    wts_ref[...] = jnp.where(lane == 0, gw * p1, jnp.where(lane == 1, gw * p2, 0.0))


def _route_call(x2, gain, mod3, router_g, router_e, seq, scale_blk, shift_blk, tm=512):
    n, d = x2.shape
    tpb = seq // tm
    rc = jnp.pad(jnp.concatenate([router_g, router_e], axis=1), ((0, 0), (0, LANES - N_GROUPS - N_EXPERTS)))
    return pl.pallas_call(
        _route_kernel,
        grid=(n // tm,),
        in_specs=[
            pl.BlockSpec((tm, d), lambda i: (i, 0)),
            pl.BlockSpec((1, d), lambda i: (0, 0)),
            pl.BlockSpec((1, 1, d), lambda i: (i // tpb, 0, scale_blk)),
            pl.BlockSpec((1, 1, d), lambda i: (i // tpb, 0, shift_blk)),
            pl.BlockSpec((d, LANES), lambda i: (0, 0)),
        ],
        out_specs=[pl.BlockSpec((tm, d // 2), lambda i: (i, 0)), pl.BlockSpec((tm, LANES), lambda i: (i, 0)),
                   pl.BlockSpec((tm, LANES), lambda i: (i, 0))],
        out_shape=[jax.ShapeDtypeStruct((n, d // 2), jnp.int32), jax.ShapeDtypeStruct((n, LANES), jnp.int32),
                   jax.ShapeDtypeStruct((n, LANES), F32)],
        compiler_params=_cparams(("parallel",)),
        name="moe_route",
    )(x2, gain.reshape(1, d), mod3, mod3, rc)


SC_CORES = 2
SC_SUBCORES = 16
SC_WORKERS = SC_CORES * SC_SUBCORES
SC_ROWS = 64


def _sc_gather(table, idx):
    m = idx.shape[0]
    w = table.shape[1]
    per_worker = m // SC_WORKERS
    steps = per_worker // SC_ROWS
    assert per_worker * SC_WORKERS == m and steps * SC_ROWS == per_worker and steps % 2 == 0
    mesh = plsc.VectorSubcoreMesh(core_axis_name="c", subcore_axis_name="s")

    def body(table_hbm, idx_hbm, out_hbm, idx_v, rows_a, rows_b, sem_ga, sem_gb, sem_wa, sem_wb):
        wid = lax.axis_index("s") * SC_CORES + lax.axis_index("c")
        pltpu.sync_copy(idx_hbm.at[wid], idx_v)

        @pl.loop(0, steps, step=2)
        def _(j):
            row0 = wid * per_worker + j * SC_ROWS
            ga = pltpu.async_copy(table_hbm.at[idx_v.at[j]], rows_a, sem_ga)
            gb = pltpu.async_copy(table_hbm.at[idx_v.at[j + 1]], rows_b, sem_gb)
            ga.wait()
            wa = pltpu.async_copy(rows_a, out_hbm.at[pl.ds(row0, SC_ROWS)], sem_wa)
            gb.wait()
            wb = pltpu.async_copy(rows_b, out_hbm.at[pl.ds(row0 + SC_ROWS, SC_ROWS)], sem_wb)
            wa.wait()
            wb.wait()

    return pl.kernel(
        body,
        out_type=jax.ShapeDtypeStruct((m, w), table.dtype),
        mesh=mesh,
        scratch_types=[pltpu.VMEM((steps, SC_ROWS), jnp.int32), pltpu.VMEM((SC_ROWS, w), table.dtype),
                       pltpu.VMEM((SC_ROWS, w), table.dtype), pltpu.SemaphoreType.DMA, pltpu.SemaphoreType.DMA,
                       pltpu.SemaphoreType.DMA, pltpu.SemaphoreType.DMA],
        name="sc_row_gather",
    )(table, idx.reshape(SC_WORKERS, steps, SC_ROWS))


MOE_TM = 512


def _gexperts_kernel(te_ref, nu_ref, xs_ref, w1_ref, w3_ref, w2_ref, ys_ref, w1b_ref, w3b_ref, w2b_ref):
    i = pl.program_id(0)

    @pl.when((i == 0) | (te_ref[i] != te_ref[jnp.maximum(i - 1, 0)]))
    def _():
        w1b_ref[...] = w1_ref[0].astype(BF16)
        w3b_ref[...] = w3_ref[0].astype(BF16)
        w2b_ref[...] = w2_ref[0].astype(BF16)

    @pl.when(i < nu_ref[0])
    def _():
        hm = xs_ref.shape[0] // 2
        parts = [pl.ds(0, hm), pl.ds(hm, hm)]
        xb = [_unpack_bf16_pairs(xs_ref[p, :]).astype(BF16) for p in parts]
        up = [_dot(x, w1b_ref[...]) for x in xb]
        gt = [_dot(x, w3b_ref[...]) for x in xb]
        act = [(_silu(u) * g).astype(BF16) for u, g in zip(up, gt)]
        y = [_dot(a, w2b_ref[...]) for a in act]
        for p, yy in zip(parts, y):
            ys_ref[p, :] = _pack_bf16_pairs(yy)


def _gexperts_call(xs, tile_expert, n_used, w1, w3, w2):
    p, half = xs.shape
    ne, d, de = w1.shape
    nt = p // MOE_TM

    def rows(i, te, nu):
        return (jnp.minimum(i, nu[0] - 1), 0)

    def wsel(i, te, nu):
        return (te[i], 0, 0)

    return pl.pallas_call(
        _gexperts_kernel,
        grid_spec=pltpu.PrefetchScalarGridSpec(
            num_scalar_prefetch=2,
            grid=(nt,),
            in_specs=[
                pl.BlockSpec((MOE_TM, half), rows),
                pl.BlockSpec((1, d, de), wsel),
                pl.BlockSpec((1, d, de), wsel),
                pl.BlockSpec((1, de, d), wsel),
            ],
            out_specs=pl.BlockSpec((MOE_TM, half), rows),
            scratch_shapes=[pltpu.VMEM((d, de), BF16), pltpu.VMEM((d, de), BF16), pltpu.VMEM((de, d), BF16)],
        ),
        out_shape=jax.ShapeDtypeStruct((p, half), jnp.int32),
        compiler_params=_cparams(("arbitrary",)),
        name="moe_experts",
    )(tile_expert, n_used, xs, w1, w3, w2)


def _combine_kernel(y0_ref, y1_ref, wts_ref, x_ref, gate_ref, fg_ref, o_ref, *, final_norm):
    wts = wts_ref[...]
    moe = wts[:, 0:1] * _unpack_bf16_pairs(y0_ref[...]) + wts[:, 1:2] * _unpack_bf16_pairs(y1_ref[...])
    xn = x_ref[...] + gate_ref[0] * moe
    if final_norm:
        xn = xn * lax.rsqrt(jnp.mean(xn * xn, axis=-1, keepdims=True) + NORM_EPS) * fg_ref[...]
    o_ref[...] = xn


def _combine_call(yg, wts, x2, mod3, final_g, seq, gate_blk, final_norm, tm=512):
    n, d = x2.shape
    tpb = seq // tm
    slot1 = n // tm
    return pl.pallas_call(
        functools.partial(_combine_kernel, final_norm=final_norm),
        grid=(n // tm,),
        in_specs=[
            pl.BlockSpec((tm, d // 2), lambda i: (i, 0)),
            pl.BlockSpec((tm, d // 2), lambda i: (i + slot1, 0)),
            pl.BlockSpec((tm, LANES), lambda i: (i, 0)),
            pl.BlockSpec((tm, d), lambda i: (i, 0)),
            pl.BlockSpec((1, 1, d), lambda i: (i // tpb, 0, gate_blk)),
            pl.BlockSpec((1, d), lambda i: (0, 0)),
        ],
        out_specs=pl.BlockSpec((tm, d), lambda i: (i, 0)),
        out_shape=jax.ShapeDtypeStruct((n, d), F32),
        compiler_params=_cparams(("parallel",)),
        name="moe_combine",
    )(yg, yg, wts, x2, mod3, final_g.reshape(1, d))


def _moe_plan(eid):
    n = eid.shape[0]
    na = 2 * n
    nt = na // MOE_TM + N_EXPERTS
    e_flat = jnp.concatenate([eid[:, 0], eid[:, 1]])
    iota_a = jnp.arange(na, dtype=jnp.int32)
    e_sorted, order = lax.sort((e_flat, iota_a), num_keys=1, is_stable=True)
    experts = jnp.arange(N_EXPERTS, dtype=jnp.int32)
    counts = jnp.sum(e_flat[:, None] == experts[None, :], axis=0, dtype=jnp.int32)
    tiles = (counts + MOE_TM - 1) // MOE_TM
    tile_end = jnp.cumsum(tiles)
    tile_start = tile_end - tiles
    row_start = jnp.cumsum(counts) - counts
    n_used = tile_end[-1:]
    tile_id = jnp.minimum(jnp.arange(nt, dtype=jnp.int32), n_used - 1)
    tile_expert = jnp.sum(tile_id[:, None] >= tile_end[None, :], axis=1, dtype=jnp.int32)
    e_row = jnp.repeat(tile_expert, MOE_TM)
    p_iota = jnp.arange(nt * MOE_TM, dtype=jnp.int32)
    rank = p_iota - jnp.repeat(tile_start[tile_expert], MOE_TM) * MOE_TM
    valid = rank < counts[e_row]
    src_token = jnp.where(valid, order[jnp.minimum(row_start[e_row] + rank, na - 1)] % n, p_iota % n)
    p_sorted = tile_start[e_sorted] * MOE_TM + iota_a - row_start[e_sorted]
    _, pos = lax.sort((order, p_sorted), num_keys=1)
    return src_token, pos, tile_expert, n_used


def kernel(x, c, positions, ada_w, ada_b, norm1_g, norm2_g, w_in, hg_lb_table, hg_norm_w, rw_mu, rw_w0, rw_w2,
           rw_a0, rw_a2, rw_g2, rw_k_k, rw_k_a, rw_r_k, rw_ln_w, rw_ln_b, br_hg, br_ret, br_rw, w_out,
           router_g, router_e, moe_w1, moe_w3, moe_w2, final_g):
    b, t, d = x.shape
    depth = ada_w.shape[0]
    n = b * t
    assert w_in.shape[2] == IN_COLS and d == 1024

    lb_p = jax.nn.softmax(hg_lb_table.astype(F32), axis=0)
    lower_bounds = jnp.cumsum(lb_p, axis=0) - lb_p[0]

    mod = _mod_call(c, ada_w, ada_b)
    cos2, sin2 = _rope_call(positions, RET_DK)
    n_gate = 3 * d
    x2 = x.reshape(n, d)
    for l in range(depth):
        mod3 = mod[l].reshape(b, 1, 6 * d)
        w_perm = jnp.concatenate([w_in[l][:, IN_COLS - n_gate:], w_in[l][:, :IN_COLS - n_gate]], axis=1)
        z2 = _inproj_call(x2, norm1_g[l], mod3, w_perm.astype(BF16), t, scale_blk=1, shift_blk=0)
        z3 = z2.reshape(b, t, IN_COLS)
        o_hg = _hgrn2_call(z3, lower_bounds[l], hg_norm_w[l])
        o_ret = _ret_call(z3, cos2, sin2)
        o_rw = _rwkv_call(z3, rw_mu[l], rw_w0[l], rw_w2[l], rw_a0[l], rw_a2[l], rw_g2[l], rw_k_k[l],
                          rw_k_a[l], rw_r_k[l], rw_ln_w[l], rw_ln_b[l])
        x2 = _merge_call(o_hg.reshape(n, HG_W), o_ret.reshape(n, RET_W), o_rw.reshape(n, RW_W), z2, x2, mod3,
                         br_hg[l].astype(BF16), br_ret[l].astype(BF16), br_rw[l].astype(BF16),
                         w_out[l].astype(BF16), t, gate_blk=2)
        hp, eid, wts = _route_call(x2, norm2_g[l], mod3, router_g[l], router_e[l], t, scale_blk=4, shift_blk=3)
        src_token, pos, tile_expert, n_used = _moe_plan(eid)
        xs = _sc_gather(hp, src_token)
        ys = _gexperts_call(xs, tile_expert + l * N_EXPERTS, n_used,
                            moe_w1.reshape((-1,) + moe_w1.shape[2:]), moe_w3.reshape((-1,) + moe_w3.shape[2:]),
                            moe_w2.reshape((-1,) + moe_w2.shape[2:]))
        yg = _sc_gather(ys, pos)
        x2 = _combine_call(yg, wts, x2, mod3, final_g, t, gate_blk=5, final_norm=(l == depth - 1))
    return x2.reshape(b, t, d)
```

```python
import functools

import jax
import jax.numpy as jnp
from jax import lax
from jax.experimental import pallas as pl
from jax.experimental.pallas import tpu as pltpu
from jax.experimental.pallas import tpu_sc as plsc

F32 = jnp.float32
BF16 = jnp.bfloat16
HIGHEST = lax.Precision.HIGHEST

HG_HEADS = 4
HG_DK = 128
HG_W = HG_HEADS * HG_DK
RET_HEADS = 4
RET_DK = 128
RET_W = RET_HEADS * RET_DK
RW_HEADS = 8
RW_N = 64
RW_W = RW_HEADS * RW_N
RW_DECAY_LORA = 64
RW_A_LORA = 64
RW_GATE_LORA = 128
RW_COLS = 3 * RW_W + RW_DECAY_LORA + RW_A_LORA + RW_GATE_LORA
RW_GN_EPS = 64e-5
N_GROUPS = 4
EXPERTS_PER_GROUP = 8
N_EXPERTS = N_GROUPS * EXPERTS_PER_GROUP
ROPE_THETA = 10000.0
NORM_EPS = 1e-6

LANES = 128
LOG2E = 1.4426950408889634
VMEM_LIMIT = 56 * 1024 * 1024

GATE_OFF = 0
HG_OFF = 3 * 1024
RET_OFF = HG_OFF + 4 * HG_W
RW_OFF = RET_OFF + 4 * RET_W
IN_COLS = RW_OFF + RW_COLS

HG_CHUNK = 64
HG_SUB = 16
RW_CHUNK = 64
RW_TB = 256
Z_DTYPE = BF16
RW_BLK = 16


def _cparams(sem):
    return pltpu.CompilerParams(dimension_semantics=sem, vmem_limit_bytes=VMEM_LIMIT)


def _dot(a, b, precision=None):
    return jnp.dot(a, b, preferred_element_type=F32, precision=precision)


def _dot_nt(a, b, precision=None):
    return lax.dot_general(a, b, (((1,), (1,)), ((), ())), preferred_element_type=F32, precision=precision)


def _dot_tn(a, b, precision=None):
    return lax.dot_general(a, b, (((0,), (0,)), ((), ())), preferred_element_type=F32, precision=precision)


def _sigmoid(x):
    return 0.5 * jnp.tanh(0.5 * x) + 0.5


def _silu(x):
    return x * _sigmoid(x)


def _rms_mod(x, gain, scale, shift):
    y = x * lax.rsqrt(jnp.mean(x * x, axis=-1, keepdims=True) + NORM_EPS)
    return (y * gain) * (1.0 + scale) + shift


def _mod_kernel(c_ref, w_ref, b_ref, o_ref):
    c = c_ref[...]
    o_ref[0] = _dot(_silu(c), w_ref[0], HIGHEST) + b_ref[0]


def _mod_call(c, ada_w, ada_b):
    depth, d, d6 = ada_w.shape
    b = c.shape[0]
    nblk = d6 // d
    return pl.pallas_call(
        _mod_kernel,
        grid=(depth, nblk),
        in_specs=[
            pl.BlockSpec((b, d), lambda l, j: (0, 0)),
            pl.BlockSpec((1, d, d), lambda l, j: (l, 0, j)),
            pl.BlockSpec((1, 1, d), lambda l, j: (l, 0, j)),
        ],
        out_specs=pl.BlockSpec((1, b, d), lambda l, j: (l, 0, j)),
        out_shape=jax.ShapeDtypeStruct((depth, b, d6), F32),
        compiler_params=_cparams(("parallel", "parallel")),
        name="adaln_mod",
    )(c, ada_w, ada_b.reshape(depth, 1, d6))


def _rope_kernel(pos_ref, freq_ref, sign_ref, cos_ref, sin_ref):
    ang = pos_ref[0].astype(F32) * freq_ref[...]
    cos_ref[0] = jnp.cos(ang)
    sin_ref[0] = jnp.sin(ang) * sign_ref[...]


def _rope_call(positions, d):
    b, t = positions.shape
    tb = min(t, 512)
    inv_freq = ROPE_THETA ** (-jnp.arange(0, d, 2, dtype=F32) / d)
    freq2 = jnp.concatenate([inv_freq, inv_freq]).reshape(1, d)
    sign2 = jnp.concatenate([-jnp.ones((d // 2,), F32), jnp.ones((d // 2,), F32)]).reshape(1, d)
    out = jax.ShapeDtypeStruct((b, t, d), F32)
    return pl.pallas_call(
        _rope_kernel,
        grid=(b, t // tb),
        in_specs=[
            pl.BlockSpec((1, tb, 1), lambda i, j: (i, j, 0)),
            pl.BlockSpec((1, d), lambda i, j: (0, 0)),
            pl.BlockSpec((1, d), lambda i, j: (0, 0)),
        ],
        out_specs=[pl.BlockSpec((1, tb, d), lambda i, j: (i, j, 0))] * 2,
        out_shape=[out, out],
        compiler_params=_cparams(("parallel", "parallel")),
        name="rope_tables",
    )(positions.reshape(b, t, 1), freq2, sign2)


def _inproj_kernel(x_ref, g_ref, scale_ref, shift_ref, w_ref, o_ref, h_ref):
    @pl.when(pl.program_id(1) == 0)
    def _():
        h = _rms_mod(x_ref[...], g_ref[...], scale_ref[0], shift_ref[0])
        h_ref[...] = h.astype(BF16)

    o_ref[...] = _dot(h_ref[...], w_ref[...]).astype(o_ref.dtype)


def _inproj_call(x2, gain, mod3, w_bf16, seq, scale_blk, shift_blk, tm=1024, tn=1280):
    n, d = x2.shape
    cols = w_bf16.shape[1]
    tpb = seq // tm
    return pl.pallas_call(
        _inproj_kernel,
        grid=(n // tm, cols // tn),
        in_specs=[
            pl.BlockSpec((tm, d), lambda i, j: (i, 0)),
            pl.BlockSpec((1, d), lambda i, j: (0, 0)),
            pl.BlockSpec((1, 1, d), lambda i, j: (i // tpb, 0, scale_blk)),
            pl.BlockSpec((1, 1, d), lambda i, j: (i // tpb, 0, shift_blk)),
            pl.BlockSpec((d, tn), lambda i, j: (0, j)),
        ],
        out_specs=pl.BlockSpec((tm, tn), lambda i, j: (i, j)),
        out_shape=jax.ShapeDtypeStruct((n, cols), Z_DTYPE),
        scratch_shapes=[pltpu.VMEM((tm, d), BF16)],
        compiler_params=_cparams(("parallel", "arbitrary")),
        name="norm_inproj",
    )(x2, gain.reshape(1, d), mod3, mod3, w_bf16)


def _hgrn2_block(zq, zf, zi, zg, lb, nw, st):
    tb = zq.shape[0]
    c, sub = HG_CHUNK, HG_SUB
    nc, ns = tb // c, c // sub
    f = lb + (1.0 - lb) * _sigmoid(zf)
    logf = jnp.log(jnp.maximum(f, 1e-30))
    q = _silu(zq) * (HG_DK ** -0.5)
    k = 1.0 - f
    v = zi
    v_b = v.astype(BF16)
    row = lax.broadcasted_iota(jnp.int32, (tb, tb), 0)
    col = lax.broadcasted_iota(jnp.int32, (tb, tb), 1)
    tri = jnp.where(col >= (row // c) * c, jnp.where(row >= col, 1.0, 0.0), 0.0).astype(BF16)
    cum = _dot_x2_rhs(tri, logf)
    qe = (q * jnp.exp(cum)).astype(BF16)

    offd = [(ci * c, ci * c + sub * i) for ci in range(nc) for i in range(1, ns)]
    base = [cum[lo - 1:lo] for _, lo in offd]
    qt = [(q[lo:lo + sub] * jnp.exp(cum[lo:lo + sub] - base[j])).astype(BF16) for j, (_, lo) in enumerate(offd)]
    kt = [(k[r0:lo] * jnp.exp(base[j] - cum[r0:lo])).astype(BF16) for j, (r0, lo) in enumerate(offd)]
    a = [_dot_nt(qt[j], kt[j]).astype(BF16) for j in range(len(offd))]
    av = {lo: _dot(a[j], v_b[r0:lo]) for j, (r0, lo) in enumerate(offd)}

    nb = tb // sub
    c2 = cum * LOG2E
    ks2 = c2 - jnp.log2(k)
    gb = 4
    trow = lax.broadcasted_iota(jnp.int32, (gb, sub, HG_DK), 1)
    diag_parts = []
    for g0 in range(0, nb, gb):
        rws = slice(g0 * sub, (g0 + gb) * sub)
        c23, ks23, q3, v3 = (x[rws].reshape(gb, sub, HG_DK) for x in (c2, ks2, q, v))
        dg = jnp.zeros((gb, sub, HG_DK), F32)
        for s in range(sub):
            e = jnp.exp2(jnp.where(trow >= s, c23 - ks23[:, s:s + 1, :], -jnp.inf))
            a_col = jnp.sum(q3 * e, axis=-1, keepdims=True)
            dg = dg + a_col * v3[:, s:s + 1, :]
        diag_parts.append(dg.reshape(gb * sub, HG_DK))
    diag = jnp.concatenate(diag_parts, axis=0)

    outs = []
    for ci in range(nc):
        r0 = ci * c
        o_inter = _dot_nt(qe[r0:r0 + c], st.astype(BF16))
        for i in range(ns):
            lo = r0 + sub * i
            piece = o_inter[sub * i:sub * (i + 1)] + diag[lo:lo + sub]
            outs.append(piece + av[lo] if i > 0 else piece)
        last = cum[r0 + c - 1:r0 + c]
        kd = (k[r0:r0 + c] * jnp.exp(last - cum[r0:r0 + c])).astype(BF16)
        st = st * jnp.exp(last) + _dot_tn(v_b[r0:r0 + c], kd)
    o = jnp.concatenate(outs, axis=0)
    o = o * lax.rsqrt(jnp.mean(o * o, axis=-1, keepdims=True) + NORM_EPS)
    o = o * nw * _silu(zg)
    return o, st


def _hgrn2_kernel(zq_ref, zf_ref, zi_ref, zg_ref, lb_ref, nw_ref, o_ref, st_ref):
    @pl.when(pl.program_id(2) == 0)
    def _():
        st_ref[...] = jnp.zeros_like(st_ref)

    zq, zf, zi, zg = (r[0].astype(F32) for r in (zq_ref, zf_ref, zi_ref, zg_ref))
    o, st_new = _hgrn2_block(zq, zf, zi, zg, lb_ref[...], nw_ref[...], st_ref[...])
    st_ref[...] = st_new
    o_ref[0] = o.astype(o_ref.dtype)


def _hgrn2_call(z3, lower_bound, norm_w, tb=256):
    b, t, _ = z3.shape
    tb = min(tb, t)
    base = HG_OFF // LANES

    def zspec(part):
        return pl.BlockSpec((1, tb, LANES), lambda i, h, j: (i, j, base + part * HG_HEADS + h))

    return pl.pallas_call(
        _hgrn2_kernel,
        grid=(b, HG_HEADS, t // tb),
        in_specs=[
            zspec(0), zspec(1), zspec(2), zspec(3),
            pl.BlockSpec((1, LANES), lambda i, h, j: (0, h)),
            pl.BlockSpec((1, LANES), lambda i, h, j: (0, 0)),
        ],
        out_specs=pl.BlockSpec((1, tb, LANES), lambda i, h, j: (i, j, h)),
        out_shape=jax.ShapeDtypeStruct((b, t, HG_W), BF16),
        scratch_shapes=[pltpu.VMEM((HG_DK, HG_DK), F32)],
        compiler_params=_cparams(("parallel", "parallel", "arbitrary")),
        name="hgrn2_mixer",
    )(z3, z3, z3, z3, lower_bound.reshape(1, HG_W), norm_w.reshape(1, HG_DK))


def _ret_kernel(zq_ref, zk_ref, zv_ref, zg_ref, cos_ref, sin_ref, o_ref, st_ref, *, chunk):
    @pl.when(pl.program_id(1) == 0)
    def _():
        st_ref[...] = jnp.zeros_like(st_ref)

    cos2 = cos_ref[0]
    sin2 = sin_ref[0]
    half = RET_DK // 2
    hs = range(RET_HEADS)
    sl = [slice(h * RET_DK, (h + 1) * RET_DK) for h in hs]
    lg = [jnp.log(jnp.full((1, 1), 1.0 - 2.0 ** (-5.0 - h), F32)) for h in hs]

    def rope(z):
        return z * cos2 + pltpu.roll(z, half, 1) * sin2

    row = lax.broadcasted_iota(jnp.int32, (chunk, chunk), 0)
    col = lax.broadcasted_iota(jnp.int32, (chunk, chunk), 1)
    rel = (row - col).astype(F32)
    relp = jnp.maximum(rel, 0.0)
    tcol = lax.broadcasted_iota(jnp.int32, (chunk, 1), 0).astype(F32)
    q = [rope(zq_ref[0, :, sl[h]].astype(F32)) * (RET_DK ** -0.5) for h in hs]
    k = [rope(zk_ref[0, :, sl[h]].astype(F32)) for h in hs]
    v_b = [zv_ref[0, :, sl[h]].astype(BF16) for h in hs]
    st = [st_ref[h] for h in hs]
    dmask = [jnp.where(rel >= 0.0, jnp.exp(relp * lg[h]), 0.0) for h in hs]
    scores = [(_dot_nt(q[h].astype(BF16), k[h].astype(BF16)) * dmask[h]).astype(BF16) for h in hs]
    qx = [(q[h] * jnp.exp((tcol + 1.0) * lg[h])).astype(BF16) for h in hs]
    kz = [(k[h] * jnp.exp((chunk - 1.0 - tcol) * lg[h])).astype(BF16) for h in hs]
    o = [_dot(scores[h], v_b[h]) + _dot_nt(qx[h], st[h].astype(BF16)) for h in hs]
    for h in hs:
        st_ref[h] = st[h] * jnp.exp(chunk * lg[h]) + _dot_tn(v_b[h], kz[h])
    o = [o[h] * lax.rsqrt(jnp.mean(o[h] * o[h], axis=-1, keepdims=True) + NORM_EPS) for h in hs]
    o_ref[0] = (jnp.concatenate(o, axis=1) * _silu(zg_ref[0].astype(F32))).astype(o_ref.dtype)


def _ret_call(z3, cos2, sin2, chunk=256):
    b, t, _ = z3.shape
    chunk = min(chunk, t)
    base = RET_OFF // RET_W

    def zspec(part):
        return pl.BlockSpec((1, chunk, RET_W), lambda i, j: (i, j, base + part))

    tab = pl.BlockSpec((1, chunk, RET_DK), lambda i, j: (i, j, 0))
    return pl.pallas_call(
        functools.partial(_ret_kernel, chunk=chunk),
        grid=(b, t // chunk),
        in_specs=[zspec(0), zspec(1), zspec(2), zspec(3), tab, tab],
        out_specs=pl.BlockSpec((1, chunk, RET_W), lambda i, j: (i, j, 0)),
        out_shape=jax.ShapeDtypeStruct((b, t, RET_W), BF16),
        scratch_shapes=[pltpu.VMEM((RET_HEADS, RET_DK, RET_DK), F32)],
        compiler_params=_cparams(("parallel", "arbitrary")),
        name="retention_mixer",
    )(z3, z3, z3, z3, cos2, sin2)


def _split_bf16(x):
    hi = x.astype(BF16)
    return hi, (x - hi.astype(F32)).astype(BF16)


def _dot_x3(a, b):
    ah, al = _split_bf16(a)
    bh, bl = _split_bf16(b)
    return _dot(ah, bh) + _dot(ah, bl) + _dot(al, bh)


def _dot_x2_lhs(a, b_exact):
    ah, al = _split_bf16(a)
    return _dot(ah, b_exact) + _dot(al, b_exact)


def _dot_x2_rhs(a_exact, b):
    bh, bl = _split_bf16(b)
    return _dot(a_exact, bh) + _dot(a_exact, bl)


def _bdot(a, b):
    return _dot(a.astype(BF16), b.astype(BF16))


def _inv_unit_lower(a, eye, blk_mask):
    c = a[0].shape[0]
    m = range(len(a))
    a_bd = [jnp.where(blk_mask, a[i], 0.0) for i in m]
    a_off = [a[i] - a_bd[i] for i in m]
    a2 = [_bdot(a_bd[i], a_bd[i]) for i in m]
    p = [eye + a_bd[i] for i in m]
    r = [_bdot(jnp.concatenate([p[i], a2[i]], axis=0), a2[i]) for i in m]
    p = [p[i] + r[i][:c] for i in m]
    a4 = [r[i][c:] for i in m]
    r = [_bdot(jnp.concatenate([p[i], a4[i]], axis=0), a4[i]) for i in m]
    p = [p[i] + r[i][:c] for i in m]
    a8 = [r[i][c:] for i in m]
    t_bd = [p[i] + _bdot(p[i], a8[i]) for i in m]
    n = [_bdot(t_bd[i], a_off[i]) for i in m]
    r = [_bdot(n[i], jnp.concatenate([n[i], t_bd[i]], axis=1)) for i in m]
    z = [t_bd[i] + r[i][:, c:] for i in m]
    return [z[i] + _bdot(r[i][:, :c], z[i]) for i in m]


def _rwkv_kernel(z_ref, mu_ref, w0_ref, w2_ref, a0_ref, a2_ref, g2_ref, kk_ref, ka_ref, rk_ref,
                 lnw_ref, lnb_ref, seg_ref, o_ref, s_ref, prev_ref):
    c = RW_CHUNK
    tb = z_ref.shape[1]
    nck = tb // c

    @pl.when(pl.program_id(1) == 0)
    def _():
        s_ref[...] = jnp.zeros_like(s_ref)
        prev_ref[...] = jnp.zeros_like(prev_ref)

    z = z_ref[0].astype(F32)
    rows = lax.broadcasted_iota(jnp.int32, (tb, 1), 0)
    z_prev = jnp.where(rows == 0, prev_ref[...], pltpu.roll(z, 1, 0))
    prev_ref[...] = z[tb - 1:tb]
    zs = z + mu_ref[...] * (z_prev - z)
    r = zs[:, 0:RW_W]
    k = zs[:, RW_W:2 * RW_W]
    v = zs[:, 2 * RW_W:3 * RW_W]
    off = 3 * RW_W
    w_lo = zs[:, off:off + RW_DECAY_LORA]
    a_lo = zs[:, off + RW_DECAY_LORA:off + RW_DECAY_LORA + RW_A_LORA]
    g_lo = zs[:, off + RW_DECAY_LORA + RW_A_LORA:]

    wx = -(w0_ref[...] + _dot_x3(jnp.tanh(w_lo), w2_ref[...]))
    softplus = jnp.maximum(wx, 0.0) + jnp.log(1.0 + jnp.exp(-jnp.abs(wx)))
    logw = -jnp.exp(-softplus - 0.5)
    a = _sigmoid(a0_ref[...] + _dot_x3(a_lo, a2_ref[...]))
    g = _dot_x3(_sigmoid(g_lo), g2_ref[...])
    seg = seg_ref[...]
    kk = k * kk_ref[...]
    kk = kk * lax.rsqrt(jnp.maximum(_dot_x2_lhs(kk * kk, seg), 1e-24))
    k2 = k * (1.0 + (a - 1.0) * ka_ref[...])

    row = lax.broadcasted_iota(jnp.int32, (c, c), 0)
    col = lax.broadcasted_iota(jnp.int32, (c, c), 1)
    blk_mask = (row // RW_BLK) == (col // RW_BLK)
    eye = (row == col).astype(F32)
    row2 = lax.broadcasted_iota(jnp.int32, (c, 2 * c), 0)
    col2 = lax.broadcasted_iota(jnp.int32, (c, 2 * c), 1) % c
    incl2 = row2 >= col2
    strict2 = row2 > col2
    rowb = lax.broadcasted_iota(jnp.int32, (tb, tb), 0)
    colb = lax.broadcasted_iota(jnp.int32, (tb, tb), 1)
    tri = jnp.where(colb >= (rowb // c) * c, jnp.where(rowb >= colb, 1.0, 0.0), 0.0).astype(BF16)
    cw = _dot_x2_rhs(tri, logw)
    w_inv = jnp.exp(-cw)
    last = jnp.concatenate([jnp.broadcast_to(cw[(ci + 1) * c - 1:(ci + 1) * c], (c, RW_W)) for ci in range(nck)],
                           axis=0)
    w_rest = jnp.exp(last - cw)
    beta = a * kk
    alpha_t = -kk * jnp.exp(cw - logw)
    r_t = r * jnp.exp(cw)
    beta_h = beta * w_inv
    k_h = k2 * w_inv
    beta_d = beta * w_rest
    k_d = k2 * w_rest

    hs = range(RW_HEADS)
    ph = [(ci, h) for ci in range(nck) for h in hs]
    m = range(len(ph))
    rs = [slice(ci * c, (ci + 1) * c) for ci, _ in ph]
    sl = [slice(h * RW_N, (h + 1) * RW_N) for _, h in ph]
    v_h = [v[rs[i], sl[i]] for i in m]
    lhs = [jnp.concatenate([alpha_t[rs[i], sl[i]], r_t[rs[i], sl[i]]], axis=0).astype(BF16) for i in m]
    rhs = [jnp.concatenate([beta_h[rs[i], sl[i]], k_h[rs[i], sl[i]]], axis=0).astype(BF16) for i in m]
    big = [_dot_nt(lhs[i], rhs[i]) for i in m]
    a_a = [jnp.where(strict2, big[i][:c], 0.0) for i in m]
    a_r = [jnp.where(incl2, big[i][c:], 0.0).astype(BF16) for i in m]
    t_inv = _inv_unit_lower([a_a[i][:, :c] for i in m], eye, blk_mask)
    av = [_bdot(a_a[i][:, c:], v_h[i]) for i in m]
    bk_d = [jnp.concatenate([beta_d[rs[i], sl[i]], k_d[rs[i], sl[i]]], axis=0).astype(BF16) for i in m]
    s_cur = [s_ref[h] for h in hs]
    o_chunks = []
    for ci in range(nck):
        ix = [ci * RW_HEADS + h for h in hs]
        sd = [_dot_nt(lhs[ix[h]], s_cur[h].astype(BF16)) for h in hs]
        u = [_bdot(t_inv[ix[h]], sd[h][:c] + av[ix[h]]) for h in hs]
        uv = [jnp.concatenate([u[h], v_h[ix[h]]], axis=0).astype(BF16) for h in hs]
        o_chunks.append(jnp.concatenate([sd[h][c:] + _dot(a_r[ix[h]], uv[h]) for h in hs], axis=1))
        w_last = jnp.exp(cw[(ci + 1) * c - 1:(ci + 1) * c])
        s_cur = [s_cur[h] * w_last[:, sl[h]] + _dot_tn(uv[h], bk_d[ix[h]]) for h in hs]
    for h in hs:
        s_ref[h] = s_cur[h]
    o = jnp.concatenate(o_chunks, axis=0)

    mean = _dot_x2_lhs(o, seg) * (1.0 / RW_N)
    dev = o - mean
    var = _dot_x2_lhs(dev * dev, seg) * (1.0 / RW_N)
    o = dev * lax.rsqrt(var + RW_GN_EPS) * lnw_ref[...] + lnb_ref[...]
    bonus = _dot_x2_lhs(r * k2 * rk_ref[...], seg) * v
    o_ref[0] = ((o + bonus) * g).astype(o_ref.dtype)


def _rwkv_call(z3, mu, w0, w2, a0, a2, g2, k_k, k_a, r_k, ln_w, ln_b):
    b, t, _ = z3.shape
    c = min(RW_TB, t)
    hid = lax.broadcasted_iota(jnp.int32, (RW_W, RW_W), 0) // RW_N
    seg = (hid == hid.T).astype(BF16)

    def vec(n):
        return pl.BlockSpec((1, n), lambda i, j: (0, 0))

    def mat(m, n):
        return pl.BlockSpec((m, n), lambda i, j: (0, 0))

    return pl.pallas_call(
        _rwkv_kernel,
        grid=(b, t // c),
        in_specs=[
            pl.BlockSpec((1, c, RW_COLS), lambda i, j: (i, j, RW_OFF // RW_COLS)),
            vec(RW_COLS), vec(RW_W), mat(RW_DECAY_LORA, RW_W), vec(RW_W), mat(RW_A_LORA, RW_W),
            mat(RW_GATE_LORA, RW_W), vec(RW_W), vec(RW_W), vec(RW_W), vec(RW_W), vec(RW_W),
            mat(RW_W, RW_W),
        ],
        out_specs=pl.BlockSpec((1, c, RW_W), lambda i, j: (i, j, 0)),
        out_shape=jax.ShapeDtypeStruct((b, t, RW_W), BF16),
        scratch_shapes=[pltpu.VMEM((RW_HEADS, RW_N, RW_N), F32), pltpu.VMEM((1, RW_COLS), F32)],
        compiler_params=_cparams(("parallel", "arbitrary")),
        name="rwkv7_mixer",
    )(z3, mu.reshape(1, -1), w0.reshape(1, -1), w2, a0.reshape(1, -1), a2, g2, k_k.reshape(1, -1),
      k_a.reshape(1, -1), r_k.reshape(1, -1), ln_w.reshape(1, -1), ln_b.reshape(1, -1), seg)


def _merge_kernel(ohg_ref, oret_ref, orw_ref, zg_ref, x_ref, gate_ref, bhg_ref, bret_ref, brw_ref,
                  wout_ref, o_ref):
    d = x_ref.shape[1]
    y = _sigmoid(zg_ref[:, 0:d].astype(F32)) * _dot(ohg_ref[...], bhg_ref[...])
    y = y + _sigmoid(zg_ref[:, d:2 * d].astype(F32)) * _dot(oret_ref[...], bret_ref[...])
    y = y + _sigmoid(zg_ref[:, 2 * d:3 * d].astype(F32)) * _dot(orw_ref[...], brw_ref[...])
    o_ref[...] = x_ref[...] + gate_ref[0] * _dot(y.astype(BF16), wout_ref[...])


def _merge_call(o_hg, o_ret, o_rw, z2, x2, mod3, br_hg, br_ret, br_rw, w_out, seq, gate_blk, tm=512):
    n, d = x2.shape
    tpb = seq // tm

    def rows(w):
        return pl.BlockSpec((tm, w), lambda i: (i, 0))

    def full(m, k):
        return pl.BlockSpec((m, k), lambda i: (0, 0))

    return pl.pallas_call(
        _merge_kernel,
        grid=(n // tm,),
        in_specs=[
            rows(HG_W), rows(RET_W), rows(RW_W), rows(3 * d), rows(d),
            pl.BlockSpec((1, 1, d), lambda i: (i // tpb, 0, gate_blk)),
            full(HG_W, d), full(RET_W, d), full(RW_W, d), full(d, d),
        ],
        out_specs=rows(d),
        out_shape=jax.ShapeDtypeStruct((n, d), F32),
        compiler_params=_cparams(("parallel",)),
        name="merge_outproj",
    )(o_hg, o_ret, o_rw, z2, x2, mod3, br_hg, br_ret, br_rw, w_out)


def _pack_bf16_pairs(x):
    w = x.shape[1] // 2
    hi = pltpu.bitcast(x[:, :w].astype(BF16).astype(F32), jnp.uint32)
    lo = pltpu.bitcast(x[:, w:].astype(BF16).astype(F32), jnp.uint32)
    return pltpu.bitcast(hi | lax.shift_right_logical(lo, jnp.uint32(16)), jnp.int32)


def _unpack_bf16_pairs(p):
    u = pltpu.bitcast(p, jnp.uint32)
    hi = pltpu.bitcast(u & jnp.uint32(0xFFFF0000), F32)
    lo = pltpu.bitcast(lax.shift_left(u, jnp.uint32(16)), F32)
    return jnp.concatenate([hi, lo], axis=1)


def _route_kernel(x_ref, g_ref, scale_ref, shift_ref, rc_ref, hp_ref, eid_ref, wts_ref):
    h = _rms_mod(x_ref[...], g_ref[...], scale_ref[0], shift_ref[0])
    hp_ref[...] = _pack_bf16_pairs(h)
    tm = h.shape[0]
    lane = lax.broadcasted_iota(jnp.int32, (tm, LANES), 1)
    neg = -jnp.inf
    logits = _dot_x3(h, rc_ref[...])
    gl = jnp.where(lane < N_GROUPS, logits, neg)
    gmax = jnp.max(gl, axis=-1, keepdims=True)
    gidx = jnp.min(jnp.where(gl == gmax, lane, LANES), axis=-1, keepdims=True)
    gw = 1.0 / jnp.sum(jnp.exp(gl - gmax), axis=-1, keepdims=True)
    lo = N_GROUPS + gidx * EXPERTS_PER_GROUP
    el = jnp.where(lane >= lo, jnp.where(lane < lo + EXPERTS_PER_GROUP, logits, neg), neg)
    m1 = jnp.max(el, axis=-1, keepdims=True)
    l1 = jnp.min(jnp.where(el == m1, lane, LANES), axis=-1, keepdims=True)
    el2 = jnp.where(lane == l1, neg, el)
    m2 = jnp.max(el2, axis=-1, keepdims=True)
    l2 = jnp.min(jnp.where(el2 == m2, lane, LANES), axis=-1, keepdims=True)
    i1 = l1 - N_GROUPS
    i2 = l2 - N_GROUPS
    e2 = jnp.exp(m2 - m1)
    p1 = 1.0 / (1.0 + e2)
    p2 = e2 * p1
    eid_ref[...] = jnp.where(lane == 0, i1, jnp.where(lane == 1, i2, 0))
    wts_ref[...] = jnp.where(lane == 0, gw * p1, jnp.where(lane == 1, gw * p2, 0.0))


def _route_call(x2, gain, mod3, router_g, router_e, seq, scale_blk, shift_blk, tm=512):
    n, d = x2.shape
    tpb = seq // tm
    rc = jnp.pad(jnp.concatenate([router_g, router_e], axis=1), ((0, 0), (0, LANES - N_GROUPS - N_EXPERTS)))
    return pl.pallas_call(
        _route_kernel,
        grid=(n // tm,),
        in_specs=[
            pl.BlockSpec((tm, d), lambda i: (i, 0)),
            pl.BlockSpec((1, d), lambda i: (0, 0)),
            pl.BlockSpec((1, 1, d), lambda i: (i // tpb, 0, scale_blk)),
            pl.BlockSpec((1, 1, d), lambda i: (i // tpb, 0, shift_blk)),
            pl.BlockSpec((d, LANES), lambda i: (0, 0)),
        ],
        out_specs=[pl.BlockSpec((tm, d // 2), lambda i: (i, 0)), pl.BlockSpec((tm, LANES), lambda i: (i, 0)),
                   pl.BlockSpec((tm, LANES), lambda i: (i, 0))],
        out_shape=[jax.ShapeDtypeStruct((n, d // 2), jnp.int32), jax.ShapeDtypeStruct((n, LANES), jnp.int32),
                   jax.ShapeDtypeStruct((n, LANES), F32)],
        compiler_params=_cparams(("parallel",)),
        name="moe_route",
    )(x2, gain.reshape(1, d), mod3, mod3, rc)


SC_CORES = 2
SC_SUBCORES = 16
SC_WORKERS = SC_CORES * SC_SUBCORES
SC_ROWS = 64


def _sc_gather(table, idx):
    m = idx.shape[0]
    w = table.shape[1]
    per_worker = m // SC_WORKERS
    steps = per_worker // SC_ROWS
    assert per_worker * SC_WORKERS == m and steps * SC_ROWS == per_worker and steps % 2 == 0
    mesh = plsc.VectorSubcoreMesh(core_axis_name="c", subcore_axis_name="s")

    def body(table_hbm, idx_hbm, out_hbm, idx_v, rows_a, rows_b, sem_ga, sem_gb, sem_wa, sem_wb):
        wid = lax.axis_index("s") * SC_CORES + lax.axis_index("c")
        pltpu.sync_copy(idx_hbm.at[wid], idx_v)

        @pl.loop(0, steps, step=2)
        def _(j):
            row0 = wid * per_worker + j * SC_ROWS
            ga = pltpu.async_copy(table_hbm.at[idx_v.at[j]], rows_a, sem_ga)
            gb = pltpu.async_copy(table_hbm.at[idx_v.at[j + 1]], rows_b, sem_gb)
            ga.wait()
            wa = pltpu.async_copy(rows_a, out_hbm.at[pl.ds(row0, SC_ROWS)], sem_wa)
            gb.wait()
            wb = pltpu.async_copy(rows_b, out_hbm.at[pl.ds(row0 + SC_ROWS, SC_ROWS)], sem_wb)
            wa.wait()
            wb.wait()

    return pl.kernel(
        body,
        out_type=jax.ShapeDtypeStruct((m, w), table.dtype),
        mesh=mesh,
        scratch_types=[pltpu.VMEM((steps, SC_ROWS), jnp.int32), pltpu.VMEM((SC_ROWS, w), table.dtype),
                       pltpu.VMEM((SC_ROWS, w), table.dtype), pltpu.SemaphoreType.DMA, pltpu.SemaphoreType.DMA,
                       pltpu.SemaphoreType.DMA, pltpu.SemaphoreType.DMA],
        name="sc_row_gather",
    )(table, idx.reshape(SC_WORKERS, steps, SC_ROWS))


MOE_TM = 512


def _gexperts_kernel(te_ref, nu_ref, xs_ref, w1_ref, w3_ref, w2_ref, ys_ref, w1b_ref, w3b_ref, w2b_ref):
    i = pl.program_id(0)

    @pl.when((i == 0) | (te_ref[i] != te_ref[jnp.maximum(i - 1, 0)]))
    def _():
        w1b_ref[...] = w1_ref[0].astype(BF16)
        w3b_ref[...] = w3_ref[0].astype(BF16)
        w2b_ref[...] = w2_ref[0].astype(BF16)

    @pl.when(i < nu_ref[0])
    def _():
        hm = xs_ref.shape[0] // 2
        parts = [pl.ds(0, hm), pl.ds(hm, hm)]
        xb = [_unpack_bf16_pairs(xs_ref[p, :]).astype(BF16) for p in parts]
        up = [_dot(x, w1b_ref[...]) for x in xb]
        gt = [_dot(x, w3b_ref[...]) for x in xb]
        act = [(_silu(u) * g).astype(BF16) for u, g in zip(up, gt)]
        y = [_dot(a, w2b_ref[...]) for a in act]
        for p, yy in zip(parts, y):
            ys_ref[p, :] = _pack_bf16_pairs(yy)


def _gexperts_call(xs, tile_expert, n_used, w1, w3, w2):
    p, half = xs.shape
    ne, d, de = w1.shape
    nt = p // MOE_TM

    def rows(i, te, nu):
        return (jnp.minimum(i, nu[0] - 1), 0)

    def wsel(i, te, nu):
        return (te[i], 0, 0)

    return pl.pallas_call(
        _gexperts_kernel,
        grid_spec=pltpu.PrefetchScalarGridSpec(
            num_scalar_prefetch=2,
            grid=(nt,),
            in_specs=[
                pl.BlockSpec((MOE_TM, half), rows),
                pl.BlockSpec((1, d, de), wsel),
                pl.BlockSpec((1, d, de), wsel),
                pl.BlockSpec((1, de, d), wsel),
            ],
            out_specs=pl.BlockSpec((MOE_TM, half), rows),
            scratch_shapes=[pltpu.VMEM((d, de), BF16), pltpu.VMEM((d, de), BF16), pltpu.VMEM((de, d), BF16)],
        ),
        out_shape=jax.ShapeDtypeStruct((p, half), jnp.int32),
        compiler_params=_cparams(("arbitrary",)),
        name="moe_experts",
    )(tile_expert, n_used, xs, w1, w3, w2)


def _combine_kernel(y0_ref, y1_ref, wts_ref, x_ref, gate_ref, fg_ref, o_ref, *, final_norm):
    wts = wts_ref[...]
    moe = wts[:, 0:1] * _unpack_bf16_pairs(y0_ref[...]) + wts[:, 1:2] * _unpack_bf16_pairs(y1_ref[...])
    xn = x_ref[...] + gate_ref[0] * moe
    if final_norm:
        xn = xn * lax.rsqrt(jnp.mean(xn * xn, axis=-1, keepdims=True) + NORM_EPS) * fg_ref[...]
    o_ref[...] = xn


def _combine_call(yg, wts, x2, mod3, final_g, seq, gate_blk, final_norm, tm=512):
    n, d = x2.shape
    tpb = seq // tm
    slot1 = n // tm
    return pl.pallas_call(
        functools.partial(_combine_kernel, final_norm=final_norm),
        grid=(n // tm,),
        in_specs=[
            pl.BlockSpec((tm, d // 2), lambda i: (i, 0)),
            pl.BlockSpec((tm, d // 2), lambda i: (i + slot1, 0)),
            pl.BlockSpec((tm, LANES), lambda i: (i, 0)),
            pl.BlockSpec((tm, d), lambda i: (i, 0)),
            pl.BlockSpec((1, 1, d), lambda i: (i // tpb, 0, gate_blk)),
            pl.BlockSpec((1, d), lambda i: (0, 0)),
        ],
        out_specs=pl.BlockSpec((tm, d), lambda i: (i, 0)),
        out_shape=jax.ShapeDtypeStruct((n, d), F32),
        compiler_params=_cparams(("parallel",)),
        name="moe_combine",
    )(yg, yg, wts, x2, mod3, final_g.reshape(1, d))


def _moe_plan(eid):
    n = eid.shape[0]
    na = 2 * n
    nt = na // MOE_TM + N_EXPERTS
    e_flat = jnp.concatenate([eid[:, 0], eid[:, 1]])
    iota_a = jnp.arange(na, dtype=jnp.int32)
    e_sorted, order = lax.sort((e_flat, iota_a), num_keys=1, is_stable=True)
    experts = jnp.arange(N_EXPERTS, dtype=jnp.int32)
    counts = jnp.sum(e_flat[:, None] == experts[None, :], axis=0, dtype=jnp.int32)
    tiles = (counts + MOE_TM - 1) // MOE_TM
    tile_end = jnp.cumsum(tiles)
    tile_start = tile_end - tiles
    row_start = jnp.cumsum(counts) - counts
    n_used = tile_end[-1:]
    tile_id = jnp.minimum(jnp.arange(nt, dtype=jnp.int32), n_used - 1)
    tile_expert = jnp.sum(tile_id[:, None] >= tile_end[None, :], axis=1, dtype=jnp.int32)
    e_row = jnp.repeat(tile_expert, MOE_TM)
    p_iota = jnp.arange(nt * MOE_TM, dtype=jnp.int32)
    rank = p_iota - jnp.repeat(tile_start[tile_expert], MOE_TM) * MOE_TM
    valid = rank < counts[e_row]
    src_token = jnp.where(valid, order[jnp.minimum(row_start[e_row] + rank, na - 1)] % n, p_iota % n)
    p_sorted = tile_start[e_sorted] * MOE_TM + iota_a - row_start[e_sorted]
    _, pos = lax.sort((order, p_sorted), num_keys=1)
    return src_token, pos, tile_expert, n_used


def kernel(x, c, positions, ada_w, ada_b, norm1_g, norm2_g, w_in, hg_lb_table, hg_norm_w, rw_mu, rw_w0, rw_w2,
           rw_a0, rw_a2, rw_g2, rw_k_k, rw_k_a, rw_r_k, rw_ln_w, rw_ln_b, br_hg, br_ret, br_rw, w_out,
           router_g, router_e, moe_w1, moe_w3, moe_w2, final_g):
    b, t, d = x.shape
    depth = ada_w.shape[0]
    n = b * t
    assert w_in.shape[2] == IN_COLS and d == 1024

    lb_p = jax.nn.softmax(hg_lb_table.astype(F32), axis=0)
    lower_bounds = jnp.cumsum(lb_p, axis=0) - lb_p[0]

    mod = _mod_call(c, ada_w, ada_b)
    cos2, sin2 = _rope_call(positions, RET_DK)
    n_gate = 3 * d
    x2 = x.reshape(n, d)
    for l in range(depth):
        mod3 = mod[l].reshape(b, 1, 6 * d)
        w_perm = jnp.concatenate([w_in[l][:, IN_COLS - n_gate:], w_in[l][:, :IN_COLS - n_gate]], axis=1)
        z2 = _inproj_call(x2, norm1_g[l], mod3, w_perm.astype(BF16), t, scale_blk=1, shift_blk=0)
        z3 = z2.reshape(b, t, IN_COLS)
        o_hg = _hgrn2_call(z3, lower_bounds[l], hg_norm_w[l])
        o_ret = _ret_call(z3, cos2, sin2)
        o_rw = _rwkv_call(z3, rw_mu[l], rw_w0[l], rw_w2[l], rw_a0[l], rw_a2[l], rw_g2[l], rw_k_k[l],
                          rw_k_a[l], rw_r_k[l], rw_ln_w[l], rw_ln_b[l])
        x2 = _merge_call(o_hg.reshape(n, HG_W), o_ret.reshape(n, RET_W), o_rw.reshape(n, RW_W), z2, x2, mod3,
                         br_hg[l].astype(BF16), br_ret[l].astype(BF16), br_rw[l].astype(BF16),
                         w_out[l].astype(BF16), t, gate_blk=2)
        hp, eid, wts = _route_call(x2, norm2_g[l], mod3, router_g[l], router_e[l], t, scale_blk=4, shift_blk=3)
        src_token, pos, tile_expert, n_used = _moe_plan(eid)
        xs = _sc_gather(hp, src_token)
        ys = _gexperts_call(xs, tile_expert + l * N_EXPERTS, n_used,
                            moe_w1.reshape((-1,) + moe_w1.shape[2:]), moe_w3.reshape((-1,) + moe_w3.shape[2:]),
                            moe_w2.reshape((-1,) + moe_w2.shape[2:]))
        yg = _sc_gather(ys, pos)
        x2 = _combine_call(yg, wts, x2, mod3, final_g, t, gate_blk=5, final_norm=(l == depth - 1))
    return x2.reshape(b, t, d)
```

```python
import functools

import jax
import jax.numpy as jnp
from jax import lax
from jax.experimental import pallas as pl
from jax.experimental.pallas import tpu as pltpu
from jax.experimental.pallas import tpu_sc as plsc

F32 = jnp.float32
BF16 = jnp.bfloat16
HIGHEST = lax.Precision.HIGHEST

HG_HEADS = 4
HG_DK = 128
HG_W = HG_HEADS * HG_DK
RET_HEADS = 4
RET_DK = 128
RET_W = RET_HEADS * RET_DK
RW_HEADS = 8
RW_N = 64
RW_W = RW_HEADS * RW_N
RW_DECAY_LORA = 64
RW_A_LORA = 64
RW_GATE_LORA = 128
RW_COLS = 3 * RW_W + RW_DECAY_LORA + RW_A_LORA + RW_GATE_LORA
RW_GN_EPS = 64e-5
N_GROUPS = 4
EXPERTS_PER_GROUP = 8
N_EXPERTS = N_GROUPS * EXPERTS_PER_GROUP
ROPE_THETA = 10000.0
NORM_EPS = 1e-6

LANES = 128
LOG2E = 1.4426950408889634
VMEM_LIMIT = 56 * 1024 * 1024

GATE_OFF = 0
HG_OFF = 3 * 1024
RET_OFF = HG_OFF + 4 * HG_W
RW_OFF = RET_OFF + 4 * RET_W
IN_COLS = RW_OFF + RW_COLS

HG_CHUNK = 64
HG_SUB = 16
RW_CHUNK = 64
RW_TB = 256
Z_DTYPE = BF16
RW_BLK = 16


def _cparams(sem):
    return pltpu.CompilerParams(dimension_semantics=sem, vmem_limit_bytes=VMEM_LIMIT)


def _dot(a, b, precision=None):
    return jnp.dot(a, b, preferred_element_type=F32, precision=precision)


def _dot_nt(a, b, precision=None):
    return lax.dot_general(a, b, (((1,), (1,)), ((), ())), preferred_element_type=F32, precision=precision)


def _dot_tn(a, b, precision=None):
    return lax.dot_general(a, b, (((0,), (0,)), ((), ())), preferred_element_type=F32, precision=precision)


def _sigmoid(x):
    return 0.5 * jnp.tanh(0.5 * x) + 0.5


def _silu(x):
    return x * _sigmoid(x)


def _rms_mod(x, gain, scale, shift):
    y = x * lax.rsqrt(jnp.mean(x * x, axis=-1, keepdims=True) + NORM_EPS)
    return (y * gain) * (1.0 + scale) + shift


def _mod_kernel(c_ref, w_ref, b_ref, o_ref):
    c = c_ref[...]
    o_ref[0] = _dot(_silu(c), w_ref[0], HIGHEST) + b_ref[0]


def _mod_call(c, ada_w, ada_b):
    depth, d, d6 = ada_w.shape
    b = c.shape[0]
    nblk = d6 // d
    return pl.pallas_call(
        _mod_kernel,
        grid=(depth, nblk),
        in_specs=[
            pl.BlockSpec((b, d), lambda l, j: (0, 0)),
            pl.BlockSpec((1, d, d), lambda l, j: (l, 0, j)),
            pl.BlockSpec((1, 1, d), lambda l, j: (l, 0, j)),
        ],
        out_specs=pl.BlockSpec((1, b, d), lambda l, j: (l, 0, j)),
        out_shape=jax.ShapeDtypeStruct((depth, b, d6), F32),
        compiler_params=_cparams(("parallel", "parallel")),
        name="adaln_mod",
    )(c, ada_w, ada_b.reshape(depth, 1, d6))


def _rope_kernel(pos_ref, freq_ref, sign_ref, cos_ref, sin_ref):
    ang = pos_ref[0].astype(F32) * freq_ref[...]
    cos_ref[0] = jnp.cos(ang)
    sin_ref[0] = jnp.sin(ang) * sign_ref[...]


def _rope_call(positions, d):
    b, t = positions.shape
    tb = min(t, 512)
    inv_freq = ROPE_THETA ** (-jnp.arange(0, d, 2, dtype=F32) / d)
    freq2 = jnp.concatenate([inv_freq, inv_freq]).reshape(1, d)
    sign2 = jnp.concatenate([-jnp.ones((d // 2,), F32), jnp.ones((d // 2,), F32)]).reshape(1, d)
    out = jax.ShapeDtypeStruct((b, t, d), F32)
    return pl.pallas_call(
        _rope_kernel,
        grid=(b, t // tb),
        in_specs=[
            pl.BlockSpec((1, tb, 1), lambda i, j: (i, j, 0)),
            pl.BlockSpec((1, d), lambda i, j: (0, 0)),
            pl.BlockSpec((1, d), lambda i, j: (0, 0)),
        ],
        out_specs=[pl.BlockSpec((1, tb, d), lambda i, j: (i, j, 0))] * 2,
        out_shape=[out, out],
        compiler_params=_cparams(("parallel", "parallel")),
        name="rope_tables",
    )(positions.reshape(b, t, 1), freq2, sign2)


def _inproj_kernel(x_ref, g_ref, scale_ref, shift_ref, w_ref, o_ref, h_ref):
    @pl.when(pl.program_id(1) == 0)
    def _():
        h = _rms_mod(x_ref[...], g_ref[...], scale_ref[0], shift_ref[0])
        h_ref[...] = h.astype(BF16)

    o_ref[...] = _dot(h_ref[...], w_ref[...]).astype(o_ref.dtype)


def _inproj_call(x2, gain, mod3, w_bf16, seq, scale_blk, shift_blk, tm=1024, tn=1280):
    n, d = x2.shape
    cols = w_bf16.shape[1]
    tpb = seq // tm
    return pl.pallas_call(
        _inproj_kernel,
        grid=(n // tm, cols // tn),
        in_specs=[
            pl.BlockSpec((tm, d), lambda i, j: (i, 0)),
            pl.BlockSpec((1, d), lambda i, j: (0, 0)),
            pl.BlockSpec((1, 1, d), lambda i, j: (i // tpb, 0, scale_blk)),
            pl.BlockSpec((1, 1, d), lambda i, j: (i // tpb, 0, shift_blk)),
            pl.BlockSpec((d, tn), lambda i, j: (0, j)),
        ],
        out_specs=pl.BlockSpec((tm, tn), lambda i, j: (i, j)),
        out_shape=jax.ShapeDtypeStruct((n, cols), Z_DTYPE),
        scratch_shapes=[pltpu.VMEM((tm, d), BF16)],
        compiler_params=_cparams(("parallel", "arbitrary")),
        name="norm_inproj",
    )(x2, gain.reshape(1, d), mod3, mod3, w_bf16)


def _hgrn2_block(zq, zf, zi, zg, lb, nw, st):
    tb = zq.shape[0]
    c, sub = HG_CHUNK, HG_SUB
    nc, ns = tb // c, c // sub
    f = lb + (1.0 - lb) * _sigmoid(zf)
    logf = jnp.log(jnp.maximum(f, 1e-30))
    q = _silu(zq) * (HG_DK ** -0.5)
    k = 1.0 - f
    v = zi
    v_b = v.astype(BF16)
    row = lax.broadcasted_iota(jnp.int32, (tb, tb), 0)
    col = lax.broadcasted_iota(jnp.int32, (tb, tb), 1)
    tri = jnp.where(col >= (row // c) * c, jnp.where(row >= col, 1.0, 0.0), 0.0).astype(BF16)
    cum = _dot_x2_rhs(tri, logf)
    qe = (q * jnp.exp(cum)).astype(BF16)

    offd = [(ci * c, ci * c + sub * i) for ci in range(nc) for i in range(1, ns)]
    base = [cum[lo - 1:lo] for _, lo in offd]
    qt = [(q[lo:lo + sub] * jnp.exp(cum[lo:lo + sub] - base[j])).astype(BF16) for j, (_, lo) in enumerate(offd)]
    kt = [(k[r0:lo] * jnp.exp(base[j] - cum[r0:lo])).astype(BF16) for j, (r0, lo) in enumerate(offd)]
    a = [_dot_nt(qt[j], kt[j]).astype(BF16) for j in range(len(offd))]
    av = {lo: _dot(a[j], v_b[r0:lo]) for j, (r0, lo) in enumerate(offd)}

    nb = tb // sub
    c2 = cum * LOG2E
    ks2 = c2 - jnp.log2(k)
    gb = 4
    trow = lax.broadcasted_iota(jnp.int32, (gb, sub, HG_DK), 1)
    diag_parts = []
    for g0 in range(0, nb, gb):
        rws = slice(g0 * sub, (g0 + gb) * sub)
        c23, ks23, q3, v3 = (x[rws].reshape(gb, sub, HG_DK) for x in (c2, ks2, q, v))
        dg = jnp.zeros((gb, sub, HG_DK), F32)
        for s in range(sub):
            e = jnp.exp2(jnp.where(trow >= s, c23 - ks23[:, s:s + 1, :], -jnp.inf))
            a_col = jnp.sum(q3 * e, axis=-1, keepdims=True)
            dg = dg + a_col * v3[:, s:s + 1, :]
        diag_parts.append(dg.reshape(gb * sub, HG_DK))
    diag = jnp.concatenate(diag_parts, axis=0)

    outs = []
    for ci in range(nc):
        r0 = ci * c
        o_inter = _dot_nt(qe[r0:r0 + c], st.astype(BF16))
        for i in range(ns):
            lo = r0 + sub * i
            piece = o_inter[sub * i:sub * (i + 1)] + diag[lo:lo + sub]
            outs.append(piece + av[lo] if i > 0 else piece)
        last = cum[r0 + c - 1:r0 + c]
        kd = (k[r0:r0 + c] * jnp.exp(last - cum[r0:r0 + c])).astype(BF16)
        st = st * jnp.exp(last) + _dot_tn(v_b[r0:r0 + c], kd)
    o = jnp.concatenate(outs, axis=0)
    o = o * lax.rsqrt(jnp.mean(o * o, axis=-1, keepdims=True) + NORM_EPS)
    o = o * nw * _silu(zg)
    return o, st


def _hgrn2_kernel(zq_ref, zf_ref, zi_ref, zg_ref, lb_ref, nw_ref, o_ref, st_ref):
    @pl.when(pl.program_id(2) == 0)
    def _():
        st_ref[...] = jnp.zeros_like(st_ref)

    zq, zf, zi, zg = (r[0].astype(F32) for r in (zq_ref, zf_ref, zi_ref, zg_ref))
    o, st_new = _hgrn2_block(zq, zf, zi, zg, lb_ref[...], nw_ref[...], st_ref[...])
    st_ref[...] = st_new
    o_ref[0] = o.astype(o_ref.dtype)


def _hgrn2_call(z3, lower_bound, norm_w, tb=256):
    b, t, _ = z3.shape
    tb = min(tb, t)
    base = HG_OFF // LANES

    def zspec(part):
        return pl.BlockSpec((1, tb, LANES), lambda i, h, j: (i, j, base + part * HG_HEADS + h))

    return pl.pallas_call(
        _hgrn2_kernel,
        grid=(b, HG_HEADS, t // tb),
        in_specs=[
            zspec(0), zspec(1), zspec(2), zspec(3),
            pl.BlockSpec((1, LANES), lambda i, h, j: (0, h)),
            pl.BlockSpec((1, LANES), lambda i, h, j: (0, 0)),
        ],
        out_specs=pl.BlockSpec((1, tb, LANES), lambda i, h, j: (i, j, h)),
        out_shape=jax.ShapeDtypeStruct((b, t, HG_W), BF16),
        scratch_shapes=[pltpu.VMEM((HG_DK, HG_DK), F32)],
        compiler_params=_cparams(("parallel", "parallel", "arbitrary")),
        name="hgrn2_mixer",
    )(z3, z3, z3, z3, lower_bound.reshape(1, HG_W), norm_w.reshape(1, HG_DK))


def _ret_kernel(zq_ref, zk_ref, zv_ref, zg_ref, cos_ref, sin_ref, o_ref, st_ref, *, chunk):
    @pl.when(pl.program_id(1) == 0)
    def _():
        st_ref[...] = jnp.zeros_like(st_ref)

    cos2 = cos_ref[0]
    sin2 = sin_ref[0]
    half = RET_DK // 2
    hs = range(RET_HEADS)
    sl = [slice(h * RET_DK, (h + 1) * RET_DK) for h in hs]
    lg = [jnp.log(jnp.full((1, 1), 1.0 - 2.0 ** (-5.0 - h), F32)) for h in hs]

    def rope(z):
        return z * cos2 + pltpu.roll(z, half, 1) * sin2

    row = lax.broadcasted_iota(jnp.int32, (chunk, chunk), 0)
    col = lax.broadcasted_iota(jnp.int32, (chunk, chunk), 1)
    rel = (row - col).astype(F32)
    relp = jnp.maximum(rel, 0.0)
    tcol = lax.broadcasted_iota(jnp.int32, (chunk, 1), 0).astype(F32)
    q = [rope(zq_ref[0, :, sl[h]].astype(F32)) * (RET_DK ** -0.5) for h in hs]
    k = [rope(zk_ref[0, :, sl[h]].astype(F32)) for h in hs]
    v_b = [zv_ref[0, :, sl[h]].astype(BF16) for h in hs]
    st = [st_ref[h] for h in hs]
    dmask = [jnp.where(rel >= 0.0, jnp.exp(relp * lg[h]), 0.0) for h in hs]
    scores = [(_dot_nt(q[h].astype(BF16), k[h].astype(BF16)) * dmask[h]).astype(BF16) for h in hs]
    qx = [(q[h] * jnp.exp((tcol + 1.0) * lg[h])).astype(BF16) for h in hs]
    kz = [(k[h] * jnp.exp((chunk - 1.0 - tcol) * lg[h])).astype(BF16) for h in hs]
    o = [_dot(scores[h], v_b[h]) + _dot_nt(qx[h], st[h].astype(BF16)) for h in hs]
    for h in hs:
        st_ref[h] = st[h] * jnp.exp(chunk * lg[h]) + _dot_tn(v_b[h], kz[h])
    o = [o[h] * lax.rsqrt(jnp.mean(o[h] * o[h], axis=-1, keepdims=True) + NORM_EPS) for h in hs]
    o_ref[0] = (jnp.concatenate(o, axis=1) * _silu(zg_ref[0].astype(F32))).astype(o_ref.dtype)


def _ret_call(z3, cos2, sin2, chunk=256):
    b, t, _ = z3.shape
    chunk = min(chunk, t)
    base = RET_OFF // RET_W

    def zspec(part):
        return pl.BlockSpec((1, chunk, RET_W), lambda i, j: (i, j, base + part))

    tab = pl.BlockSpec((1, chunk, RET_DK), lambda i, j: (i, j, 0))
    return pl.pallas_call(
        functools.partial(_ret_kernel, chunk=chunk),
        grid=(b, t // chunk),
        in_specs=[zspec(0), zspec(1), zspec(2), zspec(3), tab, tab],
        out_specs=pl.BlockSpec((1, chunk, RET_W), lambda i, j: (i, j, 0)),
        out_shape=jax.ShapeDtypeStruct((b, t, RET_W), BF16),
        scratch_shapes=[pltpu.VMEM((RET_HEADS, RET_DK, RET_DK), F32)],
        compiler_params=_cparams(("parallel", "arbitrary")),
        name="retention_mixer",
    )(z3, z3, z3, z3, cos2, sin2)


def _split_bf16(x):
    hi = x.astype(BF16)
    return hi, (x - hi.astype(F32)).astype(BF16)


def _dot_x3(a, b):
    ah, al = _split_bf16(a)
    bh, bl = _split_bf16(b)
    return _dot(ah, bh) + _dot(ah, bl) + _dot(al, bh)


def _dot_x2_lhs(a, b_exact):
    ah, al = _split_bf16(a)
    return _dot(ah, b_exact) + _dot(al, b_exact)


def _dot_x2_rhs(a_exact, b):
    bh, bl = _split_bf16(b)
    return _dot(a_exact, bh) + _dot(a_exact, bl)


def _bdot(a, b):
    return _dot(a.astype(BF16), b.astype(BF16))


def _inv_unit_lower(a, eye, blk_mask):
    c = a[0].shape[0]
    m = range(len(a))
    a_bd = [jnp.where(blk_mask, a[i], 0.0) for i in m]
    a_off = [a[i] - a_bd[i] for i in m]
    a2 = [_bdot(a_bd[i], a_bd[i]) for i in m]
    p = [eye + a_bd[i] for i in m]
    r = [_bdot(jnp.concatenate([p[i], a2[i]], axis=0), a2[i]) for i in m]
    p = [p[i] + r[i][:c] for i in m]
    a4 = [r[i][c:] for i in m]
    r = [_bdot(jnp.concatenate([p[i], a4[i]], axis=0), a4[i]) for i in m]
    p = [p[i] + r[i][:c] for i in m]
    a8 = [r[i][c:] for i in m]
    t_bd = [p[i] + _bdot(p[i], a8[i]) for i in m]
    n = [_bdot(t_bd[i], a_off[i]) for i in m]
    r = [_bdot(n[i], jnp.concatenate([n[i], t_bd[i]], axis=1)) for i in m]
    z = [t_bd[i] + r[i][:, c:] for i in m]
    return [z[i] + _bdot(r[i][:, :c], z[i]) for i in m]


def _rwkv_kernel(z_ref, mu_ref, w0_ref, w2_ref, a0_ref, a2_ref, g2_ref, kk_ref, ka_ref, rk_ref,
                 lnw_ref, lnb_ref, seg_ref, o_ref, s_ref, prev_ref):
    c = RW_CHUNK
    tb = z_ref.shape[1]
    nck = tb // c

    @pl.when(pl.program_id(1) == 0)
    def _():
        s_ref[...] = jnp.zeros_like(s_ref)
        prev_ref[...] = jnp.zeros_like(prev_ref)

    z = z_ref[0].astype(F32)
    rows = lax.broadcasted_iota(jnp.int32, (tb, 1), 0)
    z_prev = jnp.where(rows == 0, prev_ref[...], pltpu.roll(z, 1, 0))
    prev_ref[...] = z[tb - 1:tb]
    zs = z + mu_ref[...] * (z_prev - z)
    r = zs[:, 0:RW_W]
    k = zs[:, RW_W:2 * RW_W]
    v = zs[:, 2 * RW_W:3 * RW_W]
    off = 3 * RW_W
    w_lo = zs[:, off:off + RW_DECAY_LORA]
    a_lo = zs[:, off + RW_DECAY_LORA:off + RW_DECAY_LORA + RW_A_LORA]
    g_lo = zs[:, off + RW_DECAY_LORA + RW_A_LORA:]

    wx = -(w0_ref[...] + _dot_x3(jnp.tanh(w_lo), w2_ref[...]))
    softplus = jnp.maximum(wx, 0.0) + jnp.log(1.0 + jnp.exp(-jnp.abs(wx)))
    logw = -jnp.exp(-softplus - 0.5)
    a = _sigmoid(a0_ref[...] + _dot_x3(a_lo, a2_ref[...]))
    g = _dot_x3(_sigmoid(g_lo), g2_ref[...])
    seg = seg_ref[...]
    kk = k * kk_ref[...]
    kk = kk * lax.rsqrt(jnp.maximum(_dot_x2_lhs(kk * kk, seg), 1e-24))
    k2 = k * (1.0 + (a - 1.0) * ka_ref[...])

    row = lax.broadcasted_iota(jnp.int32, (c, c), 0)
    col = lax.broadcasted_iota(jnp.int32, (c, c), 1)
    blk_mask = (row // RW_BLK) == (col // RW_BLK)
    eye = (row == col).astype(F32)
    row2 = lax.broadcasted_iota(jnp.int32, (c, 2 * c), 0)
    col2 = lax.broadcasted_iota(jnp.int32, (c, 2 * c), 1) % c
    incl2 = row2 >= col2
    strict2 = row2 > col2
    rowb = lax.broadcasted_iota(jnp.int32, (tb, tb), 0)
    colb = lax.broadcasted_iota(jnp.int32, (tb, tb), 1)
    tri = jnp.where(colb >= (rowb // c) * c, jnp.where(rowb >= colb, 1.0, 0.0), 0.0).astype(BF16)
    cw = _dot_x2_rhs(tri, logw)
    w_inv = jnp.exp(-cw)
    last = jnp.concatenate([jnp.broadcast_to(cw[(ci + 1) * c - 1:(ci + 1) * c], (c, RW_W)) for ci in range(nck)],
                           axis=0)
    w_rest = jnp.exp(last - cw)
    beta = a * kk
    alpha_t = -kk * jnp.exp(cw - logw)
    r_t = r * jnp.exp(cw)
    beta_h = beta * w_inv
    k_h = k2 * w_inv
    beta_d = beta * w_rest
    k_d = k2 * w_rest

    hs = range(RW_HEADS)
    ph = [(ci, h) for ci in range(nck) for h in hs]
    m = range(len(ph))
    rs = [slice(ci * c, (ci + 1) * c) for ci, _ in ph]
    sl = [slice(h * RW_N, (h + 1) * RW_N) for _, h in ph]
    v_h = [v[rs[i], sl[i]] for i in m]
    lhs = [jnp.concatenate([alpha_t[rs[i], sl[i]], r_t[rs[i], sl[i]]], axis=0).astype(BF16) for i in m]
    rhs = [jnp.concatenate([beta_h[rs[i], sl[i]], k_h[rs[i], sl[i]]], axis=0).astype(BF16) for i in m]
    big = [_dot_nt(lhs[i], rhs[i]) for i in m]
    a_a = [jnp.where(strict2, big[i][:c], 0.0) for i in m]
    a_r = [jnp.where(incl2, big[i][c:], 0.0).astype(BF16) for i in m]
    t_inv = _inv_unit_lower([a_a[i][:, :c] for i in m], eye, blk_mask)
    av = [_bdot(a_a[i][:, c:], v_h[i]) for i in m]
    bk_d = [jnp.concatenate([beta_d[rs[i], sl[i]], k_d[rs[i], sl[i]]], axis=0).astype(BF16) for i in m]
    s_cur = [s_ref[h] for h in hs]
    o_chunks = []
    for ci in range(nck):
        ix = [ci * RW_HEADS + h for h in hs]
        sd = [_dot_nt(lhs[ix[h]], s_cur[h].astype(BF16)) for h in hs]
        u = [_bdot(t_inv[ix[h]], sd[h][:c] + av[ix[h]]) for h in hs]
        uv = [jnp.concatenate([u[h], v_h[ix[h]]], axis=0).astype(BF16) for h in hs]
        o_chunks.append(jnp.concatenate([sd[h][c:] + _dot(a_r[ix[h]], uv[h]) for h in hs], axis=1))
        w_last = jnp.exp(cw[(ci + 1) * c - 1:(ci + 1) * c])
        s_cur = [s_cur[h] * w_last[:, sl[h]] + _dot_tn(uv[h], bk_d[ix[h]]) for h in hs]
    for h in hs:
        s_ref[h] = s_cur[h]
    o = jnp.concatenate(o_chunks, axis=0)

    mean = _dot_x2_lhs(o, seg) * (1.0 / RW_N)
    dev = o - mean
    var = _dot_x2_lhs(dev * dev, seg) * (1.0 / RW_N)
    o = dev * lax.rsqrt(var + RW_GN_EPS) * lnw_ref[...] + lnb_ref[...]
    bonus = _dot_x2_lhs(r * k2 * rk_ref[...], seg) * v
    o_ref[0] = ((o + bonus) * g).astype(o_ref.dtype)


def _rwkv_call(z3, mu, w0, w2, a0, a2, g2, k_k, k_a, r_k, ln_w, ln_b):
    b, t, _ = z3.shape
    c = min(RW_TB, t)
    hid = lax.broadcasted_iota(jnp.int32, (RW_W, RW_W), 0) // RW_N
    seg = (hid == hid.T).astype(BF16)

    def vec(n):
        return pl.BlockSpec((1, n), lambda i, j: (0, 0))

    def mat(m, n):
        return pl.BlockSpec((m, n), lambda i, j: (0, 0))

    return pl.pallas_call(
        _rwkv_kernel,
        grid=(b, t // c),
        in_specs=[
            pl.BlockSpec((1, c, RW_COLS), lambda i, j: (i, j, RW_OFF // RW_COLS)),
            vec(RW_COLS), vec(RW_W), mat(RW_DECAY_LORA, RW_W), vec(RW_W), mat(RW_A_LORA, RW_W),
            mat(RW_GATE_LORA, RW_W), vec(RW_W), vec(RW_W), vec(RW_W), vec(RW_W), vec(RW_W),
            mat(RW_W, RW_W),
        ],
        out_specs=pl.BlockSpec((1, c, RW_W), lambda i, j: (i, j, 0)),
        out_shape=jax.ShapeDtypeStruct((b, t, RW_W), BF16),
        scratch_shapes=[pltpu.VMEM((RW_HEADS, RW_N, RW_N), F32), pltpu.VMEM((1, RW_COLS), F32)],
        compiler_params=_cparams(("parallel", "arbitrary")),
        name="rwkv7_mixer",
    )(z3, mu.reshape(1, -1), w0.reshape(1, -1), w2, a0.reshape(1, -1), a2, g2, k_k.reshape(1, -1),
      k_a.reshape(1, -1), r_k.reshape(1, -1), ln_w.reshape(1, -1), ln_b.reshape(1, -1), seg)


def _merge_kernel(ohg_ref, oret_ref, orw_ref, zg_ref, x_ref, gate_ref, bhg_ref, bret_ref, brw_ref,
                  wout_ref, o_ref):
    d = x_ref.shape[1]
    y = _sigmoid(zg_ref[:, 0:d].astype(F32)) * _dot(ohg_ref[...], bhg_ref[...])
    y = y + _sigmoid(zg_ref[:, d:2 * d].astype(F32)) * _dot(oret_ref[...], bret_ref[...])
    y = y + _sigmoid(zg_ref[:, 2 * d:3 * d].astype(F32)) * _dot(orw_ref[...], brw_ref[...])
    o_ref[...] = x_ref[...] + gate_ref[0] * _dot(y.astype(BF16), wout_ref[...])


def _merge_call(o_hg, o_ret, o_rw, z2, x2, mod3, br_hg, br_ret, br_rw, w_out, seq, gate_blk, tm=512):
    n, d = x2.shape
    tpb = seq // tm

    def rows(w):
        return pl.BlockSpec((tm, w), lambda i: (i, 0))

    def full(m, k):
        return pl.BlockSpec((m, k), lambda i: (0, 0))

    return pl.pallas_call(
        _merge_kernel,
        grid=(n // tm,),
        in_specs=[
            rows(HG_W), rows(RET_W), rows(RW_W), rows(3 * d), rows(d),
            pl.BlockSpec((1, 1, d), lambda i: (i // tpb, 0, gate_blk)),
            full(HG_W, d), full(RET_W, d), full(RW_W, d), full(d, d),
        ],
        out_specs=rows(d),
        out_shape=jax.ShapeDtypeStruct((n, d), F32),
        compiler_params=_cparams(("parallel",)),
        name="merge_outproj",
    )(o_hg, o_ret, o_rw, z2, x2, mod3, br_hg, br_ret, br_rw, w_out)


def _pack_bf16_pairs(x):
    w = x.shape[1] // 2
    hi = pltpu.bitcast(x[:, :w].astype(BF16).astype(F32), jnp.uint32)
    lo = pltpu.bitcast(x[:, w:].astype(BF16).astype(F32), jnp.uint32)
    return pltpu.bitcast(hi | lax.shift_right_logical(lo, jnp.uint32(16)), jnp.int32)


def _unpack_bf16_pairs(p):
    u = pltpu.bitcast(p, jnp.uint32)
    hi = pltpu.bitcast(u & jnp.uint32(0xFFFF0000), F32)
    lo = pltpu.bitcast(lax.shift_left(u, jnp.uint32(16)), F32)
    return jnp.concatenate([hi, lo], axis=1)


def _route_kernel(x_ref, g_ref, scale_ref, shift_ref, rc_ref, hp_ref, eid_ref, wts_ref, cnt_ref):
    @pl.when(pl.program_id(0) == 0)
    def _():
        cnt_ref[...] = jnp.zeros_like(cnt_ref)

    h = _rms_mod(x_ref[...], g_ref[...], scale_ref[0], shift_ref[0])
    hp_ref[...] = _pack_bf16_pairs(h)
    tm = h.shape[0]
    lane = lax.broadcasted_iota(jnp.int32, (tm, LANES), 1)
    neg = -jnp.inf
    logits = _dot_x3(h, rc_ref[...])
    gl = jnp.where(lane < N_GROUPS, logits, neg)
    gmax = jnp.max(gl, axis=-1, keepdims=True)
    gidx = jnp.min(jnp.where(gl == gmax, lane, LANES), axis=-1, keepdims=True)
    gw = 1.0 / jnp.sum(jnp.exp(gl - gmax), axis=-1, keepdims=True)
    lo = N_GROUPS + gidx * EXPERTS_PER_GROUP
    el = jnp.where(lane >= lo, jnp.where(lane < lo + EXPERTS_PER_GROUP, logits, neg), neg)
    m1 = jnp.max(el, axis=-1, keepdims=True)
    l1 = jnp.min(jnp.where(el == m1, lane, LANES), axis=-1, keepdims=True)
    el2 = jnp.where(lane == l1, neg, el)
    m2 = jnp.max(el2, axis=-1, keepdims=True)
    l2 = jnp.min(jnp.where(el2 == m2, lane, LANES), axis=-1, keepdims=True)
    i1 = l1 - N_GROUPS
    i2 = l2 - N_GROUPS
    e2 = jnp.exp(m2 - m1)
    p1 = 1.0 / (1.0 + e2)
    p2 = e2 * p1
    oh1 = jnp.where(lane == i1, 1.0, 0.0)
    oh2 = jnp.where(lane == i2, 1.0, 0.0)
    row = lax.broadcasted_iota(jnp.int32, (tm, tm), 0)
    col = lax.broadcasted_iota(jnp.int32, (tm, tm), 1)
    earlier = jnp.where(row > col, 1.0, 0.0).astype(BF16)
    before = _dot(earlier, jnp.concatenate([oh1, oh2], axis=1).astype(BF16))
    tot1 = jnp.sum(oh1, axis=0, keepdims=True)
    carry = cnt_ref[...]
    r1 = jnp.sum(oh1 * (before[:, :LANES] + carry), axis=-1, keepdims=True).astype(jnp.int32)
    r2 = jnp.sum(oh2 * (before[:, LANES:] + (carry + tot1)), axis=-1, keepdims=True).astype(jnp.int32)
    cnt_ref[...] = carry + tot1 + jnp.sum(oh2, axis=0, keepdims=True)
    eid_ref[...] = jnp.where(lane == 0, i1, jnp.where(lane == 1, i2, jnp.where(lane == 2, r1,
                                                                             jnp.where(lane == 3, r2, 0))))
    wts_ref[...] = jnp.where(lane == 0, gw * p1, jnp.where(lane == 1, gw * p2, 0.0))


def _route_call(x2, gain, mod3, router_g, router_e, seq, scale_blk, shift_blk, tm=512):
    n, d = x2.shape
    tpb = seq // tm
    rc = jnp.pad(jnp.concatenate([router_g, router_e], axis=1), ((0, 0), (0, LANES - N_GROUPS - N_EXPERTS)))
    return pl.pallas_call(
        _route_kernel,
        grid=(n // tm,),
        in_specs=[
            pl.BlockSpec((tm, d), lambda i: (i, 0)),
            pl.BlockSpec((1, d), lambda i: (0, 0)),
            pl.BlockSpec((1, 1, d), lambda i: (i // tpb, 0, scale_blk)),
            pl.BlockSpec((1, 1, d), lambda i: (i // tpb, 0, shift_blk)),
            pl.BlockSpec((d, LANES), lambda i: (0, 0)),
        ],
        out_specs=[pl.BlockSpec((tm, d // 2), lambda i: (i, 0)), pl.BlockSpec((tm, LANES), lambda i: (i, 0)),
                   pl.BlockSpec((tm, LANES), lambda i: (i, 0)), pl.BlockSpec((1, LANES), lambda i: (0, 0))],
        out_shape=[jax.ShapeDtypeStruct((n, d // 2), jnp.int32), jax.ShapeDtypeStruct((n, LANES), jnp.int32),
                   jax.ShapeDtypeStruct((n, LANES), F32), jax.ShapeDtypeStruct((1, LANES), F32)],
        compiler_params=_cparams(("arbitrary",)),
        name="moe_route",
    )(x2, gain.reshape(1, d), mod3, mod3, rc)


SC_CORES = 2
SC_SUBCORES = 16
SC_WORKERS = SC_CORES * SC_SUBCORES
SC_ROWS = 64


def _sc_gather(table, idx):
    m = idx.shape[0]
    w = table.shape[1]
    per_worker = m // SC_WORKERS
    steps = per_worker // SC_ROWS
    assert per_worker * SC_WORKERS == m and steps * SC_ROWS == per_worker and steps % 2 == 0
    mesh = plsc.VectorSubcoreMesh(core_axis_name="c", subcore_axis_name="s")

    def body(table_hbm, idx_hbm, out_hbm, idx_v, rows_a, rows_b, sem_ga, sem_gb, sem_wa, sem_wb):
        wid = lax.axis_index("s") * SC_CORES + lax.axis_index("c")
        pltpu.sync_copy(idx_hbm.at[wid], idx_v)

        @pl.loop(0, steps, step=2)
        def _(j):
            row0 = wid * per_worker + j * SC_ROWS
            ga = pltpu.async_copy(table_hbm.at[idx_v.at[j]], rows_a, sem_ga)
            gb = pltpu.async_copy(table_hbm.at[idx_v.at[j + 1]], rows_b, sem_gb)
            ga.wait()
            wa = pltpu.async_copy(rows_a, out_hbm.at[pl.ds(row0, SC_ROWS)], sem_wa)
            gb.wait()
            wb = pltpu.async_copy(rows_b, out_hbm.at[pl.ds(row0 + SC_ROWS, SC_ROWS)], sem_wb)
            wa.wait()
            wb.wait()

    return pl.kernel(
        body,
        out_type=jax.ShapeDtypeStruct((m, w), table.dtype),
        mesh=mesh,
        scratch_types=[pltpu.VMEM((steps, SC_ROWS), jnp.int32), pltpu.VMEM((SC_ROWS, w), table.dtype),
                       pltpu.VMEM((SC_ROWS, w), table.dtype), pltpu.SemaphoreType.DMA, pltpu.SemaphoreType.DMA,
                       pltpu.SemaphoreType.DMA, pltpu.SemaphoreType.DMA],
        name="sc_row_gather",
    )(table, idx.reshape(SC_WORKERS, steps, SC_ROWS))


def _sc_scatter2(rows, idx0, idx1, p):
    n, w = rows.shape
    per_worker = n // SC_WORKERS
    steps = per_worker // SC_ROWS
    assert per_worker * SC_WORKERS == n and steps * SC_ROWS == per_worker and steps % 2 == 0
    mesh = plsc.VectorSubcoreMesh(core_axis_name="c", subcore_axis_name="s")

    def body(rows_hbm, i0_hbm, i1_hbm, out_hbm, i0_v, i1_v, buf_a, buf_b, s_ra, s_rb, s_a0, s_a1, s_b0, s_b1):
        wid = lax.axis_index("s") * SC_CORES + lax.axis_index("c")
        pltpu.sync_copy(i0_hbm.at[wid], i0_v)
        pltpu.sync_copy(i1_hbm.at[wid], i1_v)

        @pl.loop(0, steps, step=2)
        def _(j):
            row0 = wid * per_worker + j * SC_ROWS
            ra = pltpu.async_copy(rows_hbm.at[pl.ds(row0, SC_ROWS)], buf_a, s_ra)
            rb = pltpu.async_copy(rows_hbm.at[pl.ds(row0 + SC_ROWS, SC_ROWS)], buf_b, s_rb)
            ra.wait()
            a0 = pltpu.async_copy(buf_a, out_hbm.at[i0_v.at[j]], s_a0)
            a1 = pltpu.async_copy(buf_a, out_hbm.at[i1_v.at[j]], s_a1)
            rb.wait()
            b0 = pltpu.async_copy(buf_b, out_hbm.at[i0_v.at[j + 1]], s_b0)
            b1 = pltpu.async_copy(buf_b, out_hbm.at[i1_v.at[j + 1]], s_b1)
            a0.wait()
            a1.wait()
            b0.wait()
            b1.wait()

    return pl.kernel(
        body,
        out_type=jax.ShapeDtypeStruct((p, w), rows.dtype),
        mesh=mesh,
        scratch_types=[pltpu.VMEM((steps, SC_ROWS), jnp.int32), pltpu.VMEM((steps, SC_ROWS), jnp.int32),
                       pltpu.VMEM((SC_ROWS, w), rows.dtype), pltpu.VMEM((SC_ROWS, w), rows.dtype)]
        + [pltpu.SemaphoreType.DMA] * 6,
        name="sc_row_scatter",
    )(rows, idx0.reshape(SC_WORKERS, steps, SC_ROWS), idx1.reshape(SC_WORKERS, steps, SC_ROWS))


MOE_TM = 512


def _gexperts_kernel(te_ref, tv_ref, nu_ref, xs_ref, w1_ref, w3_ref, w2_ref, ys_ref, w1b_ref, w3b_ref, w2b_ref):
    i = pl.program_id(0)

    @pl.when((i == 0) | (te_ref[i] != te_ref[jnp.maximum(i - 1, 0)]))
    def _():
        w1b_ref[...] = w1_ref[0].astype(BF16)
        w3b_ref[...] = w3_ref[0].astype(BF16)
        w2b_ref[...] = w2_ref[0].astype(BF16)

    @pl.when(i < nu_ref[0])
    def _():
        hm = xs_ref.shape[0] // 2
        parts = [pl.ds(0, hm), pl.ds(hm, hm)]
        left = [tv_ref[i], tv_ref[i] - hm]
        rid = lax.broadcasted_iota(jnp.int32, (hm, xs_ref.shape[1]), 0)
        xb = [_unpack_bf16_pairs(jnp.where(rid < left[q], xs_ref[p, :], 0)).astype(BF16) for q, p in enumerate(parts)]
        up = [_dot(x, w1b_ref[...]) for x in xb]
        gt = [_dot(x, w3b_ref[...]) for x in xb]
        act = [(_silu(u) * g).astype(BF16) for u, g in zip(up, gt)]
        y = [_dot(a, w2b_ref[...]) for a in act]
        for p, yy in zip(parts, y):
            ys_ref[p, :] = _pack_bf16_pairs(yy)


def _gexperts_call(xs, tile_expert, tile_valid, n_used, w1, w3, w2):
    p, half = xs.shape
    ne, d, de = w1.shape
    nt = p // MOE_TM

    def rows(i, te, tv, nu):
        return (jnp.minimum(i, nu[0] - 1), 0)

    def wsel(i, te, tv, nu):
        return (te[i], 0, 0)

    return pl.pallas_call(
        _gexperts_kernel,
        grid_spec=pltpu.PrefetchScalarGridSpec(
            num_scalar_prefetch=3,
            grid=(nt,),
            in_specs=[
                pl.BlockSpec((MOE_TM, half), rows),
                pl.BlockSpec((1, d, de), wsel),
                pl.BlockSpec((1, d, de), wsel),
                pl.BlockSpec((1, de, d), wsel),
            ],
            out_specs=pl.BlockSpec((MOE_TM, half), rows),
            scratch_shapes=[pltpu.VMEM((d, de), BF16), pltpu.VMEM((d, de), BF16), pltpu.VMEM((de, d), BF16)],
        ),
        out_shape=jax.ShapeDtypeStruct((p, half), jnp.int32),
        compiler_params=_cparams(("arbitrary",)),
        name="moe_experts",
    )(tile_expert, tile_valid, n_used, xs, w1, w3, w2)


def _combine_kernel(y0_ref, y1_ref, wts_ref, x_ref, gate_ref, fg_ref, o_ref, *, final_norm):
    wts = wts_ref[...]
    moe = wts[:, 0:1] * _unpack_bf16_pairs(y0_ref[...]) + wts[:, 1:2] * _unpack_bf16_pairs(y1_ref[...])
    xn = x_ref[...] + gate_ref[0] * moe
    if final_norm:
        xn = xn * lax.rsqrt(jnp.mean(xn * xn, axis=-1, keepdims=True) + NORM_EPS) * fg_ref[...]
    o_ref[...] = xn


def _combine_call(yg, wts, x2, mod3, final_g, seq, gate_blk, final_norm, tm=512):
    n, d = x2.shape
    tpb = seq // tm
    slot1 = n // tm
    return pl.pallas_call(
        functools.partial(_combine_kernel, final_norm=final_norm),
        grid=(n // tm,),
        in_specs=[
            pl.BlockSpec((tm, d // 2), lambda i: (i, 0)),
            pl.BlockSpec((tm, d // 2), lambda i: (i + slot1, 0)),
            pl.BlockSpec((tm, LANES), lambda i: (i, 0)),
            pl.BlockSpec((tm, d), lambda i: (i, 0)),
            pl.BlockSpec((1, 1, d), lambda i: (i // tpb, 0, gate_blk)),
            pl.BlockSpec((1, d), lambda i: (0, 0)),
        ],
        out_specs=pl.BlockSpec((tm, d), lambda i: (i, 0)),
        out_shape=jax.ShapeDtypeStruct((n, d), F32),
        compiler_params=_cparams(("parallel",)),
        name="moe_combine",
    )(yg, yg, wts, x2, mod3, final_g.reshape(1, d))


def _moe_plan(eid, counts_f):
    n = eid.shape[0]
    nt = (2 * n) // MOE_TM + N_EXPERTS
    counts = counts_f[0, :N_EXPERTS].astype(jnp.int32)
    tiles = (counts + MOE_TM - 1) // MOE_TM
    tile_end = jnp.cumsum(tiles)
    tile_start = tile_end - tiles
    n_used = tile_end[-1:]
    tile_iota = jnp.arange(nt, dtype=jnp.int32)
    tile_expert = jnp.sum(jnp.minimum(tile_iota, n_used - 1)[:, None] >= tile_end[None, :], axis=1, dtype=jnp.int32)
    tile_valid = jnp.clip(counts[tile_expert] - (tile_iota - tile_start[tile_expert]) * MOE_TM, 0, MOE_TM)
    experts = jnp.arange(N_EXPERTS, dtype=jnp.int32)
    row0 = jnp.sum(jnp.where(eid[:, 0:2, None] == experts[None, None, :], tile_start * MOE_TM, 0), axis=-1)
    pos = row0 + eid[:, 2:4]
    return pos[:, 0], pos[:, 1], tile_expert, tile_valid, n_used


def kernel(x, c, positions, ada_w, ada_b, norm1_g, norm2_g, w_in, hg_lb_table, hg_norm_w, rw_mu, rw_w0, rw_w2,
           rw_a0, rw_a2, rw_g2, rw_k_k, rw_k_a, rw_r_k, rw_ln_w, rw_ln_b, br_hg, br_ret, br_rw, w_out,
           router_g, router_e, moe_w1, moe_w3, moe_w2, final_g):
    b, t, d = x.shape
    depth = ada_w.shape[0]
    n = b * t
    assert w_in.shape[2] == IN_COLS and d == 1024

    lb_p = jax.nn.softmax(hg_lb_table.astype(F32), axis=0)
    lower_bounds = jnp.cumsum(lb_p, axis=0) - lb_p[0]

    mod = _mod_call(c, ada_w, ada_b)
    cos2, sin2 = _rope_call(positions, RET_DK)
    n_gate = 3 * d
    x2 = x.reshape(n, d)
    for l in range(depth):
        mod3 = mod[l].reshape(b, 1, 6 * d)
        w_perm = jnp.concatenate([w_in[l][:, IN_COLS - n_gate:], w_in[l][:, :IN_COLS - n_gate]], axis=1)
        z2 = _inproj_call(x2, norm1_g[l], mod3, w_perm.astype(BF16), t, scale_blk=1, shift_blk=0)
        z3 = z2.reshape(b, t, IN_COLS)
        o_hg = _hgrn2_call(z3, lower_bounds[l], hg_norm_w[l])
        o_ret = _ret_call(z3, cos2, sin2)
        o_rw = _rwkv_call(z3, rw_mu[l], rw_w0[l], rw_w2[l], rw_a0[l], rw_a2[l], rw_g2[l], rw_k_k[l],
                          rw_k_a[l], rw_r_k[l], rw_ln_w[l], rw_ln_b[l])
        x2 = _merge_call(o_hg.reshape(n, HG_W), o_ret.reshape(n, RET_W), o_rw.reshape(n, RW_W), z2, x2, mod3,
                         br_hg[l].astype(BF16), br_ret[l].astype(BF16), br_rw[l].astype(BF16),
                         w_out[l].astype(BF16), t, gate_blk=2)
        hp, eid, wts, counts = _route_call(x2, norm2_g[l], mod3, router_g[l], router_e[l], t, scale_blk=4,
                                           shift_blk=3)
        pos0, pos1, tile_expert, tile_valid, n_used = _moe_plan(eid, counts)
        xs = _sc_scatter2(hp, pos0, pos1, (2 * n // MOE_TM + N_EXPERTS) * MOE_TM)
        ys = _gexperts_call(xs, tile_expert + l * N_EXPERTS, tile_valid, n_used,
                            moe_w1.reshape((-1,) + moe_w1.shape[2:]), moe_w3.reshape((-1,) + moe_w3.shape[2:]),
                            moe_w2.reshape((-1,) + moe_w2.shape[2:]))
        yg = _sc_gather(ys, jnp.concatenate([pos0, pos1]))
        x2 = _combine_call(yg, wts, x2, mod3, final_g, t, gate_blk=5, final_norm=(l == depth - 1))
    return x2.reshape(b, t, d)
```

```python
import functools

import jax
import jax.numpy as jnp
from jax import lax
from jax.experimental import pallas as pl
from jax.experimental.pallas import tpu as pltpu
from jax.experimental.pallas import tpu_sc as plsc

F32 = jnp.float32
BF16 = jnp.bfloat16
HIGHEST = lax.Precision.HIGHEST

HG_HEADS = 4
HG_DK = 128
HG_W = HG_HEADS * HG_DK
RET_HEADS = 4
RET_DK = 128
RET_W = RET_HEADS * RET_DK
RW_HEADS = 8
RW_N = 64
RW_W = RW_HEADS * RW_N
RW_DECAY_LORA = 64
RW_A_LORA = 64
RW_GATE_LORA = 128
RW_COLS = 3 * RW_W + RW_DECAY_LORA + RW_A_LORA + RW_GATE_LORA
RW_GN_EPS = 64e-5
N_GROUPS = 4
EXPERTS_PER_GROUP = 8
N_EXPERTS = N_GROUPS * EXPERTS_PER_GROUP
ROPE_THETA = 10000.0
NORM_EPS = 1e-6

LANES = 128
LOG2E = 1.4426950408889634
VMEM_LIMIT = 56 * 1024 * 1024

GATE_OFF = 0
HG_OFF = 3 * 1024
RET_OFF = HG_OFF + 4 * HG_W
RW_OFF = RET_OFF + 4 * RET_W
IN_COLS = RW_OFF + RW_COLS

HG_CHUNK = 64
HG_SUB = 16
RW_CHUNK = 64
RW_TB = 256
Z_DTYPE = BF16
RW_BLK = 16


def _cparams(sem):
    return pltpu.CompilerParams(dimension_semantics=sem, vmem_limit_bytes=VMEM_LIMIT)


def _dot(a, b, precision=None):
    return jnp.dot(a, b, preferred_element_type=F32, precision=precision)


def _dot_nt(a, b, precision=None):
    return lax.dot_general(a, b, (((1,), (1,)), ((), ())), preferred_element_type=F32, precision=precision)


def _dot_tn(a, b, precision=None):
    return lax.dot_general(a, b, (((0,), (0,)), ((), ())), preferred_element_type=F32, precision=precision)


def _sigmoid(x):
    return 0.5 * jnp.tanh(0.5 * x) + 0.5


def _silu(x):
    return x * _sigmoid(x)


def _rms_mod(x, gain, scale, shift):
    y = x * lax.rsqrt(jnp.mean(x * x, axis=-1, keepdims=True) + NORM_EPS)
    return (y * gain) * (1.0 + scale) + shift


def _mod_kernel(c_ref, w_ref, b_ref, o_ref):
    c = c_ref[...]
    o_ref[0] = _dot(_silu(c), w_ref[0], HIGHEST) + b_ref[0]


def _mod_call(c, ada_w, ada_b):
    depth, d, d6 = ada_w.shape
    b = c.shape[0]
    nblk = d6 // d
    return pl.pallas_call(
        _mod_kernel,
        grid=(depth, nblk),
        in_specs=[
            pl.BlockSpec((b, d), lambda l, j: (0, 0)),
            pl.BlockSpec((1, d, d), lambda l, j: (l, 0, j)),
            pl.BlockSpec((1, 1, d), lambda l, j: (l, 0, j)),
        ],
        out_specs=pl.BlockSpec((1, b, d), lambda l, j: (l, 0, j)),
        out_shape=jax.ShapeDtypeStruct((depth, b, d6), F32),
        compiler_params=_cparams(("parallel", "parallel")),
        name="adaln_mod",
    )(c, ada_w, ada_b.reshape(depth, 1, d6))


def _rope_kernel(pos_ref, freq_ref, sign_ref, cos_ref, sin_ref):
    ang = pos_ref[0].astype(F32) * freq_ref[...]
    cos_ref[0] = jnp.cos(ang)
    sin_ref[0] = jnp.sin(ang) * sign_ref[...]


def _rope_call(positions, d):
    b, t = positions.shape
    tb = min(t, 512)
    inv_freq = ROPE_THETA ** (-jnp.arange(0, d, 2, dtype=F32) / d)
    freq2 = jnp.concatenate([inv_freq, inv_freq]).reshape(1, d)
    sign2 = jnp.concatenate([-jnp.ones((d // 2,), F32), jnp.ones((d // 2,), F32)]).reshape(1, d)
    out = jax.ShapeDtypeStruct((b, t, d), F32)
    return pl.pallas_call(
        _rope_kernel,
        grid=(b, t // tb),
        in_specs=[
            pl.BlockSpec((1, tb, 1), lambda i, j: (i, j, 0)),
            pl.BlockSpec((1, d), lambda i, j: (0, 0)),
            pl.BlockSpec((1, d), lambda i, j: (0, 0)),
        ],
        out_specs=[pl.BlockSpec((1, tb, d), lambda i, j: (i, j, 0))] * 2,
        out_shape=[out, out],
        compiler_params=_cparams(("parallel", "parallel")),
        name="rope_tables",
    )(positions.reshape(b, t, 1), freq2, sign2)


def _inproj_kernel(x_ref, g_ref, scale_ref, shift_ref, w_ref, o_ref, h_ref):
    @pl.when(pl.program_id(1) == 0)
    def _():
        h = _rms_mod(x_ref[...], g_ref[...], scale_ref[0], shift_ref[0])
        h_ref[...] = h.astype(BF16)

    o_ref[...] = _dot(h_ref[...], w_ref[...]).astype(o_ref.dtype)


def _inproj_call(x2, gain, mod3, w_bf16, seq, scale_blk, shift_blk, tm=2048, tn=1792):
    n, d = x2.shape
    cols = w_bf16.shape[1]
    tpb = seq // tm
    return pl.pallas_call(
        _inproj_kernel,
        grid=(n // tm, cols // tn),
        in_specs=[
            pl.BlockSpec((tm, d), lambda i, j: (i, 0)),
            pl.BlockSpec((1, d), lambda i, j: (0, 0)),
            pl.BlockSpec((1, 1, d), lambda i, j: (i // tpb, 0, scale_blk)),
            pl.BlockSpec((1, 1, d), lambda i, j: (i // tpb, 0, shift_blk)),
            pl.BlockSpec((d, tn), lambda i, j: (0, j)),
        ],
        out_specs=pl.BlockSpec((tm, tn), lambda i, j: (i, j)),
        out_shape=jax.ShapeDtypeStruct((n, cols), Z_DTYPE),
        scratch_shapes=[pltpu.VMEM((tm, d), BF16)],
        compiler_params=_cparams(("parallel", "arbitrary")),
        name="norm_inproj",
    )(x2, gain.reshape(1, d), mod3, mod3, w_bf16)


def _hgrn2_block(zq, zf, zi, zg, lb, nw, st):
    tb = zq.shape[0]
    c, sub = HG_CHUNK, HG_SUB
    nc, ns = tb // c, c // sub
    f = lb + (1.0 - lb) * _sigmoid(zf)
    logf = jnp.log(jnp.maximum(f, 1e-30))
    q = _silu(zq) * (HG_DK ** -0.5)
    k = 1.0 - f
    v = zi
    v_b = v.astype(BF16)
    row = lax.broadcasted_iota(jnp.int32, (tb, tb), 0)
    col = lax.broadcasted_iota(jnp.int32, (tb, tb), 1)
    tri = jnp.where(col >= (row // c) * c, jnp.where(row >= col, 1.0, 0.0), 0.0).astype(BF16)
    cum = _dot_x2_rhs(tri, logf)
    qe = (q * jnp.exp(cum)).astype(BF16)

    offd = [(ci * c, ci * c + sub * i) for ci in range(nc) for i in range(1, ns)]
    base = [cum[lo - 1:lo] for _, lo in offd]
    qt = [(q[lo:lo + sub] * jnp.exp(cum[lo:lo + sub] - base[j])).astype(BF16) for j, (_, lo) in enumerate(offd)]
    kt = [(k[r0:lo] * jnp.exp(base[j] - cum[r0:lo])).astype(BF16) for j, (r0, lo) in enumerate(offd)]
    a = [_dot_nt(qt[j], kt[j]).astype(BF16) for j in range(len(offd))]
    av = {lo: _dot(a[j], v_b[r0:lo]) for j, (r0, lo) in enumerate(offd)}

    nb = tb // sub
    c2 = cum * LOG2E
    ks2 = c2 - jnp.log2(k)
    gb = 4
    trow = lax.broadcasted_iota(jnp.int32, (gb, sub, HG_DK), 1)
    diag_parts = []
    for g0 in range(0, nb, gb):
        rws = slice(g0 * sub, (g0 + gb) * sub)
        c23, ks23, q3, v3 = (x[rws].reshape(gb, sub, HG_DK) for x in (c2, ks2, q, v))
        dg = jnp.zeros((gb, sub, HG_DK), F32)
        for s in range(sub):
            e = jnp.exp2(jnp.where(trow >= s, c23 - ks23[:, s:s + 1, :], -jnp.inf))
            a_col = jnp.sum(q3 * e, axis=-1, keepdims=True)
            dg = dg + a_col * v3[:, s:s + 1, :]
        diag_parts.append(dg.reshape(gb * sub, HG_DK))
    diag = jnp.concatenate(diag_parts, axis=0)

    outs = []
    for ci in range(nc):
        r0 = ci * c
        o_inter = _dot_nt(qe[r0:r0 + c], st.astype(BF16))
        for i in range(ns):
            lo = r0 + sub * i
            piece = o_inter[sub * i:sub * (i + 1)] + diag[lo:lo + sub]
            outs.append(piece + av[lo] if i > 0 else piece)
        last = cum[r0 + c - 1:r0 + c]
        kd = (k[r0:r0 + c] * jnp.exp(last - cum[r0:r0 + c])).astype(BF16)
        st = st * jnp.exp(last) + _dot_tn(v_b[r0:r0 + c], kd)
    o = jnp.concatenate(outs, axis=0)
    o = o * lax.rsqrt(jnp.mean(o * o, axis=-1, keepdims=True) + NORM_EPS)
    o = o * nw * _silu(zg)
    return o, st


def _hgrn2_kernel(zq_ref, zf_ref, zi_ref, zg_ref, lb_ref, nw_ref, o_ref, st_ref):
    @pl.when(pl.program_id(2) == 0)
    def _():
        st_ref[...] = jnp.zeros_like(st_ref)

    zq, zf, zi, zg = (r[0].astype(F32) for r in (zq_ref, zf_ref, zi_ref, zg_ref))
    o, st_new = _hgrn2_block(zq, zf, zi, zg, lb_ref[...], nw_ref[...], st_ref[...])
    st_ref[...] = st_new
    o_ref[0] = o.astype(o_ref.dtype)


def _hgrn2_call(z3, lower_bound, norm_w, tb=256):
    b, t, _ = z3.shape
    tb = min(tb, t)
    base = HG_OFF // LANES

    def zspec(part):
        return pl.BlockSpec((1, tb, LANES), lambda i, h, j: (i, j, base + part * HG_HEADS + h))

    return pl.pallas_call(
        _hgrn2_kernel,
        grid=(b, HG_HEADS, t // tb),
        in_specs=[
            zspec(0), zspec(1), zspec(2), zspec(3),
            pl.BlockSpec((1, LANES), lambda i, h, j: (0, h)),
            pl.BlockSpec((1, LANES), lambda i, h, j: (0, 0)),
        ],
        out_specs=pl.BlockSpec((1, tb, LANES), lambda i, h, j: (i, j, h)),
        out_shape=jax.ShapeDtypeStruct((b, t, HG_W), BF16),
        scratch_shapes=[pltpu.VMEM((HG_DK, HG_DK), F32)],
        compiler_params=_cparams(("parallel", "parallel", "arbitrary")),
        name="hgrn2_mixer",
    )(z3, z3, z3, z3, lower_bound.reshape(1, HG_W), norm_w.reshape(1, HG_DK))


def _ret_kernel(zq_ref, zk_ref, zv_ref, zg_ref, cos_ref, sin_ref, o_ref, st_ref, *, chunk):
    @pl.when(pl.program_id(1) == 0)
    def _():
        st_ref[...] = jnp.zeros_like(st_ref)

    cos2 = cos_ref[0]
    sin2 = sin_ref[0]
    half = RET_DK // 2
    hs = range(RET_HEADS)
    sl = [slice(h * RET_DK, (h + 1) * RET_DK) for h in hs]
    lg = [jnp.log(jnp.full((1, 1), 1.0 - 2.0 ** (-5.0 - h), F32)) for h in hs]

    def rope(z):
        return z * cos2 + pltpu.roll(z, half, 1) * sin2

    row = lax.broadcasted_iota(jnp.int32, (chunk, chunk), 0)
    col = lax.broadcasted_iota(jnp.int32, (chunk, chunk), 1)
    rel = (row - col).astype(F32)
    relp = jnp.maximum(rel, 0.0)
    tcol = lax.broadcasted_iota(jnp.int32, (chunk, 1), 0).astype(F32)
    q = [rope(zq_ref[0, :, sl[h]].astype(F32)) * (RET_DK ** -0.5) for h in hs]
    k = [rope(zk_ref[0, :, sl[h]].astype(F32)) for h in hs]
    v_b = [zv_ref[0, :, sl[h]].astype(BF16) for h in hs]
    st = [st_ref[h] for h in hs]
    dmask = [jnp.where(rel >= 0.0, jnp.exp(relp * lg[h]), 0.0) for h in hs]
    scores = [(_dot_nt(q[h].astype(BF16), k[h].astype(BF16)) * dmask[h]).astype(BF16) for h in hs]
    qx = [(q[h] * jnp.exp((tcol + 1.0) * lg[h])).astype(BF16) for h in hs]
    kz = [(k[h] * jnp.exp((chunk - 1.0 - tcol) * lg[h])).astype(BF16) for h in hs]
    o = [_dot(scores[h], v_b[h]) + _dot_nt(qx[h], st[h].astype(BF16)) for h in hs]
    for h in hs:
        st_ref[h] = st[h] * jnp.exp(chunk * lg[h]) + _dot_tn(v_b[h], kz[h])
    o = [o[h] * lax.rsqrt(jnp.mean(o[h] * o[h], axis=-1, keepdims=True) + NORM_EPS) for h in hs]
    o_ref[0] = (jnp.concatenate(o, axis=1) * _silu(zg_ref[0].astype(F32))).astype(o_ref.dtype)


def _ret_call(z3, cos2, sin2, chunk=256):
    b, t, _ = z3.shape
    chunk = min(chunk, t)
    base = RET_OFF // RET_W

    def zspec(part):
        return pl.BlockSpec((1, chunk, RET_W), lambda i, j: (i, j, base + part))

    tab = pl.BlockSpec((1, chunk, RET_DK), lambda i, j: (i, j, 0))
    return pl.pallas_call(
        functools.partial(_ret_kernel, chunk=chunk),
        grid=(b, t // chunk),
        in_specs=[zspec(0), zspec(1), zspec(2), zspec(3), tab, tab],
        out_specs=pl.BlockSpec((1, chunk, RET_W), lambda i, j: (i, j, 0)),
        out_shape=jax.ShapeDtypeStruct((b, t, RET_W), BF16),
        scratch_shapes=[pltpu.VMEM((RET_HEADS, RET_DK, RET_DK), F32)],
        compiler_params=_cparams(("parallel", "arbitrary")),
        name="retention_mixer",
    )(z3, z3, z3, z3, cos2, sin2)


def _split_bf16(x):
    hi = x.astype(BF16)
    return hi, (x - hi.astype(F32)).astype(BF16)


def _dot_x3(a, b):
    ah, al = _split_bf16(a)
    bh, bl = _split_bf16(b)
    return _dot(ah, bh) + _dot(ah, bl) + _dot(al, bh)


def _dot_x2_lhs(a, b_exact):
    ah, al = _split_bf16(a)
    return _dot(ah, b_exact) + _dot(al, b_exact)


def _dot_x2_rhs(a_exact, b):
    bh, bl = _split_bf16(b)
    return _dot(a_exact, bh) + _dot(a_exact, bl)


def _bdot(a, b):
    return _dot(a.astype(BF16), b.astype(BF16))


def _inv_unit_lower(a, eye, blk_mask):
    c = a[0].shape[0]
    m = range(len(a))
    a_bd = [jnp.where(blk_mask, a[i], 0.0) for i in m]
    a_off = [a[i] - a_bd[i] for i in m]
    a2 = [_bdot(a_bd[i], a_bd[i]) for i in m]
    p = [eye + a_bd[i] for i in m]
    r = [_bdot(jnp.concatenate([p[i], a2[i]], axis=0), a2[i]) for i in m]
    p = [p[i] + r[i][:c] for i in m]
    a4 = [r[i][c:] for i in m]
    r = [_bdot(jnp.concatenate([p[i], a4[i]], axis=0), a4[i]) for i in m]
    p = [p[i] + r[i][:c] for i in m]
    a8 = [r[i][c:] for i in m]
    t_bd = [p[i] + _bdot(p[i], a8[i]) for i in m]
    n = [_bdot(t_bd[i], a_off[i]) for i in m]
    r = [_bdot(n[i], jnp.concatenate([n[i], t_bd[i]], axis=1)) for i in m]
    z = [t_bd[i] + r[i][:, c:] for i in m]
    return [z[i] + _bdot(r[i][:, :c], z[i]) for i in m]


def _rwkv_kernel(z_ref, mu_ref, w0_ref, w2_ref, a0_ref, a2_ref, g2_ref, kk_ref, ka_ref, rk_ref,
                 lnw_ref, lnb_ref, seg_ref, o_ref, s_ref, prev_ref):
    c = RW_CHUNK
    tb = z_ref.shape[1]
    nck = tb // c

    @pl.when(pl.program_id(1) == 0)
    def _():
        s_ref[...] = jnp.zeros_like(s_ref)
        prev_ref[...] = jnp.zeros_like(prev_ref)

    z = z_ref[0].astype(F32)
    rows = lax.broadcasted_iota(jnp.int32, (tb, 1), 0)
    z_prev = jnp.where(rows == 0, prev_ref[...], pltpu.roll(z, 1, 0))
    prev_ref[...] = z[tb - 1:tb]
    zs = z + mu_ref[...] * (z_prev - z)
    r = zs[:, 0:RW_W]
    k = zs[:, RW_W:2 * RW_W]
    v = zs[:, 2 * RW_W:3 * RW_W]
    off = 3 * RW_W
    w_lo = zs[:, off:off + RW_DECAY_LORA]
    a_lo = zs[:, off + RW_DECAY_LORA:off + RW_DECAY_LORA + RW_A_LORA]
    g_lo = zs[:, off + RW_DECAY_LORA + RW_A_LORA:]

    wx = -(w0_ref[...] + _dot_x3(jnp.tanh(w_lo), w2_ref[...]))
    softplus = jnp.maximum(wx, 0.0) + jnp.log(1.0 + jnp.exp(-jnp.abs(wx)))
    logw = -jnp.exp(-softplus - 0.5)
    a = _sigmoid(a0_ref[...] + _dot_x3(a_lo, a2_ref[...]))
    g = _dot_x3(_sigmoid(g_lo), g2_ref[...])
    seg = seg_ref[...]
    kk = k * kk_ref[...]
    kk = kk * lax.rsqrt(jnp.maximum(_dot_x2_lhs(kk * kk, seg), 1e-24))
    k2 = k * (1.0 + (a - 1.0) * ka_ref[...])

    row = lax.broadcasted_iota(jnp.int32, (c, c), 0)
    col = lax.broadcasted_iota(jnp.int32, (c, c), 1)
    blk_mask = (row // RW_BLK) == (col // RW_BLK)
    eye = (row == col).astype(F32)
    row2 = lax.broadcasted_iota(jnp.int32, (c, 2 * c), 0)
    col2 = lax.broadcasted_iota(jnp.int32, (c, 2 * c), 1) % c
    incl2 = row2 >= col2
    strict2 = row2 > col2
    rowb = lax.broadcasted_iota(jnp.int32, (tb, tb), 0)
    colb = lax.broadcasted_iota(jnp.int32, (tb, tb), 1)
    tri = jnp.where(colb >= (rowb // c) * c, jnp.where(rowb >= colb, 1.0, 0.0), 0.0).astype(BF16)
    cw = _dot_x2_rhs(tri, logw)
    w_inv = jnp.exp(-cw)
    last = jnp.concatenate([jnp.broadcast_to(cw[(ci + 1) * c - 1:(ci + 1) * c], (c, RW_W)) for ci in range(nck)],
                           axis=0)
    w_rest = jnp.exp(last - cw)
    beta = a * kk
    alpha_t = -kk * jnp.exp(cw - logw)
    r_t = r * jnp.exp(cw)
    beta_h = beta * w_inv
    k_h = k2 * w_inv
    beta_d = beta * w_rest
    k_d = k2 * w_rest

    hs = range(RW_HEADS)
    ph = [(ci, h) for ci in range(nck) for h in hs]
    m = range(len(ph))
    rs = [slice(ci * c, (ci + 1) * c) for ci, _ in ph]
    sl = [slice(h * RW_N, (h + 1) * RW_N) for _, h in ph]
    v_h = [v[rs[i], sl[i]] for i in m]
    lhs = [jnp.concatenate([alpha_t[rs[i], sl[i]], r_t[rs[i], sl[i]]], axis=0).astype(BF16) for i in m]
    rhs = [jnp.concatenate([beta_h[rs[i], sl[i]], k_h[rs[i], sl[i]]], axis=0).astype(BF16) for i in m]
    big = [_dot_nt(lhs[i], rhs[i]) for i in m]
    a_a = [jnp.where(strict2, big[i][:c], 0.0) for i in m]
    a_r = [jnp.where(incl2, big[i][c:], 0.0).astype(BF16) for i in m]
    t_inv = _inv_unit_lower([a_a[i][:, :c] for i in m], eye, blk_mask)
    av = [_bdot(a_a[i][:, c:], v_h[i]) for i in m]
    bk_d = [jnp.concatenate([beta_d[rs[i], sl[i]], k_d[rs[i], sl[i]]], axis=0).astype(BF16) for i in m]
    s_cur = [s_ref[h] for h in hs]
    o_chunks = []
    for ci in range(nck):
        ix = [ci * RW_HEADS + h for h in hs]
        sd = [_dot_nt(lhs[ix[h]], s_cur[h].astype(BF16)) for h in hs]
        u = [_bdot(t_inv[ix[h]], sd[h][:c] + av[ix[h]]) for h in hs]
        uv = [jnp.concatenate([u[h], v_h[ix[h]]], axis=0).astype(BF16) for h in hs]
        o_chunks.append(jnp.concatenate([sd[h][c:] + _dot(a_r[ix[h]], uv[h]) for h in hs], axis=1))
        w_last = jnp.exp(cw[(ci + 1) * c - 1:(ci + 1) * c])
        s_cur = [s_cur[h] * w_last[:, sl[h]] + _dot_tn(uv[h], bk_d[ix[h]]) for h in hs]
    for h in hs:
        s_ref[h] = s_cur[h]
    o = jnp.concatenate(o_chunks, axis=0)

    mean = _dot_x2_lhs(o, seg) * (1.0 / RW_N)
    dev = o - mean
    var = _dot_x2_lhs(dev * dev, seg) * (1.0 / RW_N)
    o = dev * lax.rsqrt(var + RW_GN_EPS) * lnw_ref[...] + lnb_ref[...]
    bonus = _dot_x2_lhs(r * k2 * rk_ref[...], seg) * v
    o_ref[0] = ((o + bonus) * g).astype(o_ref.dtype)


def _rwkv_call(z3, mu, w0, w2, a0, a2, g2, k_k, k_a, r_k, ln_w, ln_b):
    b, t, _ = z3.shape
    c = min(RW_TB, t)
    hid = lax.broadcasted_iota(jnp.int32, (RW_W, RW_W), 0) // RW_N
    seg = (hid == hid.T).astype(BF16)

    def vec(n):
        return pl.BlockSpec((1, n), lambda i, j: (0, 0))

    def mat(m, n):
        return pl.BlockSpec((m, n), lambda i, j: (0, 0))

    return pl.pallas_call(
        _rwkv_kernel,
        grid=(b, t // c),
        in_specs=[
            pl.BlockSpec((1, c, RW_COLS), lambda i, j: (i, j, RW_OFF // RW_COLS)),
            vec(RW_COLS), vec(RW_W), mat(RW_DECAY_LORA, RW_W), vec(RW_W), mat(RW_A_LORA, RW_W),
            mat(RW_GATE_LORA, RW_W), vec(RW_W), vec(RW_W), vec(RW_W), vec(RW_W), vec(RW_W),
            mat(RW_W, RW_W),
        ],
        out_specs=pl.BlockSpec((1, c, RW_W), lambda i, j: (i, j, 0)),
        out_shape=jax.ShapeDtypeStruct((b, t, RW_W), BF16),
        scratch_shapes=[pltpu.VMEM((RW_HEADS, RW_N, RW_N), F32), pltpu.VMEM((1, RW_COLS), F32)],
        compiler_params=_cparams(("parallel", "arbitrary")),
        name="rwkv7_mixer",
    )(z3, mu.reshape(1, -1), w0.reshape(1, -1), w2, a0.reshape(1, -1), a2, g2, k_k.reshape(1, -1),
      k_a.reshape(1, -1), r_k.reshape(1, -1), ln_w.reshape(1, -1), ln_b.reshape(1, -1), seg)


def _merge_kernel(ohg_ref, oret_ref, orw_ref, zg_ref, x_ref, gate_ref, bhg_ref, bret_ref, brw_ref,
                  wout_ref, o_ref):
    d = x_ref.shape[1]
    y = _sigmoid(zg_ref[:, 0:d].astype(F32)) * _dot(ohg_ref[...], bhg_ref[...])
    y = y + _sigmoid(zg_ref[:, d:2 * d].astype(F32)) * _dot(oret_ref[...], bret_ref[...])
    y = y + _sigmoid(zg_ref[:, 2 * d:3 * d].astype(F32)) * _dot(orw_ref[...], brw_ref[...])
    o_ref[...] = x_ref[...] + gate_ref[0] * _dot(y.astype(BF16), wout_ref[...])


def _merge_call(o_hg, o_ret, o_rw, z2, x2, mod3, br_hg, br_ret, br_rw, w_out, seq, gate_blk, tm=512):
    n, d = x2.shape
    tpb = seq // tm

    def rows(w):
        return pl.BlockSpec((tm, w), lambda i: (i, 0))

    def full(m, k):
        return pl.BlockSpec((m, k), lambda i: (0, 0))

    return pl.pallas_call(
        _merge_kernel,
        grid=(n // tm,),
        in_specs=[
            rows(HG_W), rows(RET_W), rows(RW_W), rows(3 * d), rows(d),
            pl.BlockSpec((1, 1, d), lambda i: (i // tpb, 0, gate_blk)),
            full(HG_W, d), full(RET_W, d), full(RW_W, d), full(d, d),
        ],
        out_specs=rows(d),
        out_shape=jax.ShapeDtypeStruct((n, d), F32),
        compiler_params=_cparams(("parallel",)),
        name="merge_outproj",
    )(o_hg, o_ret, o_rw, z2, x2, mod3, br_hg, br_ret, br_rw, w_out)


def _pack_bf16_pairs(x):
    w = x.shape[1] // 2
    hi = pltpu.bitcast(x[:, :w].astype(BF16).astype(F32), jnp.uint32)
    lo = pltpu.bitcast(x[:, w:].astype(BF16).astype(F32), jnp.uint32)
    return pltpu.bitcast(hi | lax.shift_right_logical(lo, jnp.uint32(16)), jnp.int32)


def _unpack_bf16_pairs(p):
    u = pltpu.bitcast(p, jnp.uint32)
    hi = pltpu.bitcast(u & jnp.uint32(0xFFFF0000), F32)
    lo = pltpu.bitcast(lax.shift_left(u, jnp.uint32(16)), F32)
    return jnp.concatenate([hi, lo], axis=1)


def _route_kernel(x_ref, g_ref, scale_ref, shift_ref, rc_ref, hp_ref, eid_ref, wts_ref, cnt_ref):
    @pl.when(pl.program_id(0) == 0)
    def _():
        cnt_ref[...] = jnp.zeros_like(cnt_ref)

    h = _rms_mod(x_ref[...], g_ref[...], scale_ref[0], shift_ref[0])
    hp_ref[...] = _pack_bf16_pairs(h)
    tm = h.shape[0]
    lane = lax.broadcasted_iota(jnp.int32, (tm, LANES), 1)
    neg = -jnp.inf
    logits = _dot_x3(h, rc_ref[...])
    gl = jnp.where(lane < N_GROUPS, logits, neg)
    gmax = jnp.max(gl, axis=-1, keepdims=True)
    gidx = jnp.min(jnp.where(gl == gmax, lane, LANES), axis=-1, keepdims=True)
    gw = 1.0 / jnp.sum(jnp.exp(gl - gmax), axis=-1, keepdims=True)
    lo = N_GROUPS + gidx * EXPERTS_PER_GROUP
    el = jnp.where(lane >= lo, jnp.where(lane < lo + EXPERTS_PER_GROUP, logits, neg), neg)
    m1 = jnp.max(el, axis=-1, keepdims=True)
    l1 = jnp.min(jnp.where(el == m1, lane, LANES), axis=-1, keepdims=True)
    el2 = jnp.where(lane == l1, neg, el)
    m2 = jnp.max(el2, axis=-1, keepdims=True)
    l2 = jnp.min(jnp.where(el2 == m2, lane, LANES), axis=-1, keepdims=True)
    i1 = l1 - N_GROUPS
    i2 = l2 - N_GROUPS
    e2 = jnp.exp(m2 - m1)
    p1 = 1.0 / (1.0 + e2)
    p2 = e2 * p1
    oh1 = jnp.where(lane == i1, 1.0, 0.0)
    oh2 = jnp.where(lane == i2, 1.0, 0.0)
    row = lax.broadcasted_iota(jnp.int32, (tm, tm), 0)
    col = lax.broadcasted_iota(jnp.int32, (tm, tm), 1)
    earlier = jnp.where(row > col, 1.0, 0.0).astype(BF16)
    before = _dot(earlier, jnp.concatenate([oh1, oh2], axis=1).astype(BF16))
    tot1 = jnp.sum(oh1, axis=0, keepdims=True)
    carry = cnt_ref[...]
    r1 = jnp.sum(oh1 * (before[:, :LANES] + carry), axis=-1, keepdims=True).astype(jnp.int32)
    r2 = jnp.sum(oh2 * (before[:, LANES:] + (carry + tot1)), axis=-1, keepdims=True).astype(jnp.int32)
    cnt_ref[...] = carry + tot1 + jnp.sum(oh2, axis=0, keepdims=True)
    eid_ref[...] = jnp.where(lane == 0, i1, jnp.where(lane == 1, i2, jnp.where(lane == 2, r1,
                                                                             jnp.where(lane == 3, r2, 0))))
    wts_ref[...] = jnp.where(lane == 0, gw * p1, jnp.where(lane == 1, gw * p2, 0.0))


def _route_call(x2, gain, mod3, router_g, router_e, seq, scale_blk, shift_blk, tm=512):
    n, d = x2.shape
    tpb = seq // tm
    rc = jnp.pad(jnp.concatenate([router_g, router_e], axis=1), ((0, 0), (0, LANES - N_GROUPS - N_EXPERTS)))
    return pl.pallas_call(
        _route_kernel,
        grid=(n // tm,),
        in_specs=[
            pl.BlockSpec((tm, d), lambda i: (i, 0)),
            pl.BlockSpec((1, d), lambda i: (0, 0)),
            pl.BlockSpec((1, 1, d), lambda i: (i // tpb, 0, scale_blk)),
            pl.BlockSpec((1, 1, d), lambda i: (i // tpb, 0, shift_blk)),
            pl.BlockSpec((d, LANES), lambda i: (0, 0)),
        ],
        out_specs=[pl.BlockSpec((tm, d // 2), lambda i: (i, 0)), pl.BlockSpec((tm, LANES), lambda i: (i, 0)),
                   pl.BlockSpec((tm, LANES), lambda i: (i, 0)), pl.BlockSpec((1, LANES), lambda i: (0, 0))],
        out_shape=[jax.ShapeDtypeStruct((n, d // 2), jnp.int32), jax.ShapeDtypeStruct((n, LANES), jnp.int32),
                   jax.ShapeDtypeStruct((n, LANES), F32), jax.ShapeDtypeStruct((1, LANES), F32)],
        compiler_params=_cparams(("arbitrary",)),
        name="moe_route",
    )(x2, gain.reshape(1, d), mod3, mod3, rc)


SC_CORES = 2
SC_SUBCORES = 16
SC_WORKERS = SC_CORES * SC_SUBCORES
SC_ROWS = 64


def _sc_gather(table, idx):
    m = idx.shape[0]
    w = table.shape[1]
    per_worker = m // SC_WORKERS
    steps = per_worker // SC_ROWS
    assert per_worker * SC_WORKERS == m and steps * SC_ROWS == per_worker and steps % 2 == 0
    mesh = plsc.VectorSubcoreMesh(core_axis_name="c", subcore_axis_name="s")

    def body(table_hbm, idx_hbm, out_hbm, idx_v, rows_a, rows_b, sem_ga, sem_gb, sem_wa, sem_wb):
        wid = lax.axis_index("s") * SC_CORES + lax.axis_index("c")
        pltpu.sync_copy(idx_hbm.at[wid], idx_v)

        @pl.loop(0, steps, step=2)
        def _(j):
            row0 = wid * per_worker + j * SC_ROWS
            ga = pltpu.async_copy(table_hbm.at[idx_v.at[j]], rows_a, sem_ga)
            gb = pltpu.async_copy(table_hbm.at[idx_v.at[j + 1]], rows_b, sem_gb)
            ga.wait()
            wa = pltpu.async_copy(rows_a, out_hbm.at[pl.ds(row0, SC_ROWS)], sem_wa)
            gb.wait()
            wb = pltpu.async_copy(rows_b, out_hbm.at[pl.ds(row0 + SC_ROWS, SC_ROWS)], sem_wb)
            wa.wait()
            wb.wait()

    return pl.kernel(
        body,
        out_type=jax.ShapeDtypeStruct((m, w), table.dtype),
        mesh=mesh,
        scratch_types=[pltpu.VMEM((steps, SC_ROWS), jnp.int32), pltpu.VMEM((SC_ROWS, w), table.dtype),
                       pltpu.VMEM((SC_ROWS, w), table.dtype), pltpu.SemaphoreType.DMA, pltpu.SemaphoreType.DMA,
                       pltpu.SemaphoreType.DMA, pltpu.SemaphoreType.DMA],
        name="sc_row_gather",
    )(table, idx.reshape(SC_WORKERS, steps, SC_ROWS))


def _sc_scatter2(rows, idx0, idx1, p):
    n, w = rows.shape
    per_worker = n // SC_WORKERS
    steps = per_worker // SC_ROWS
    assert per_worker * SC_WORKERS == n and steps * SC_ROWS == per_worker and steps % 2 == 0
    mesh = plsc.VectorSubcoreMesh(core_axis_name="c", subcore_axis_name="s")

    def body(rows_hbm, i0_hbm, i1_hbm, out_hbm, i0_v, i1_v, buf_a, buf_b, s_ra, s_rb, s_a0, s_a1, s_b0, s_b1):
        wid = lax.axis_index("s") * SC_CORES + lax.axis_index("c")
        pltpu.sync_copy(i0_hbm.at[wid], i0_v)
        pltpu.sync_copy(i1_hbm.at[wid], i1_v)

        @pl.loop(0, steps, step=2)
        def _(j):
            row0 = wid * per_worker + j * SC_ROWS
            ra = pltpu.async_copy(rows_hbm.at[pl.ds(row0, SC_ROWS)], buf_a, s_ra)
            rb = pltpu.async_copy(rows_hbm.at[pl.ds(row0 + SC_ROWS, SC_ROWS)], buf_b, s_rb)
            ra.wait()
            a0 = pltpu.async_copy(buf_a, out_hbm.at[i0_v.at[j]], s_a0)
            a1 = pltpu.async_copy(buf_a, out_hbm.at[i1_v.at[j]], s_a1)
            rb.wait()
            b0 = pltpu.async_copy(buf_b, out_hbm.at[i0_v.at[j + 1]], s_b0)
            b1 = pltpu.async_copy(buf_b, out_hbm.at[i1_v.at[j + 1]], s_b1)
            a0.wait()
            a1.wait()
            b0.wait()
            b1.wait()

    return pl.kernel(
        body,
        out_type=jax.ShapeDtypeStruct((p, w), rows.dtype),
        mesh=mesh,
        scratch_types=[pltpu.VMEM((steps, SC_ROWS), jnp.int32), pltpu.VMEM((steps, SC_ROWS), jnp.int32),
                       pltpu.VMEM((SC_ROWS, w), rows.dtype), pltpu.VMEM((SC_ROWS, w), rows.dtype)]
        + [pltpu.SemaphoreType.DMA] * 6,
        name="sc_row_scatter",
    )(rows, idx0.reshape(SC_WORKERS, steps, SC_ROWS), idx1.reshape(SC_WORKERS, steps, SC_ROWS))


MOE_TM = 512


def _gexperts_kernel(te_ref, tv_ref, nu_ref, xs_ref, w1_ref, w3_ref, w2_ref, ys_ref, w1b_ref, w3b_ref, w2b_ref):
    i = pl.program_id(0)

    @pl.when((i == 0) | (te_ref[i] != te_ref[jnp.maximum(i - 1, 0)]))
    def _():
        w1b_ref[...] = w1_ref[0].astype(BF16)
        w3b_ref[...] = w3_ref[0].astype(BF16)
        w2b_ref[...] = w2_ref[0].astype(BF16)

    @pl.when(i < nu_ref[0])
    def _():
        hm = xs_ref.shape[0] // 2
        parts = [pl.ds(0, hm), pl.ds(hm, hm)]
        left = [tv_ref[i], tv_ref[i] - hm]
        rid = lax.broadcasted_iota(jnp.int32, (hm, xs_ref.shape[1]), 0)
        xb = [_unpack_bf16_pairs(jnp.where(rid < left[q], xs_ref[p, :], 0)).astype(BF16) for q, p in enumerate(parts)]
        up = [_dot(x, w1b_ref[...]) for x in xb]
        gt = [_dot(x, w3b_ref[...]) for x in xb]
        act = [(_silu(u) * g).astype(BF16) for u, g in zip(up, gt)]
        y = [_dot(a, w2b_ref[...]) for a in act]
        for p, yy in zip(parts, y):
            ys_ref[p, :] = _pack_bf16_pairs(yy)


def _gexperts_call(xs, tile_expert, tile_valid, n_used, w1, w3, w2):
    p, half = xs.shape
    ne, d, de = w1.shape
    nt = p // MOE_TM

    def rows(i, te, tv, nu):
        return (jnp.minimum(i, nu[0] - 1), 0)

    def wsel(i, te, tv, nu):
        return (te[i], 0, 0)

    return pl.pallas_call(
        _gexperts_kernel,
        grid_spec=pltpu.PrefetchScalarGridSpec(
            num_scalar_prefetch=3,
            grid=(nt,),
            in_specs=[
                pl.BlockSpec((MOE_TM, half), rows),
                pl.BlockSpec((1, d, de), wsel),
                pl.BlockSpec((1, d, de), wsel),
                pl.BlockSpec((1, de, d), wsel),
            ],
            out_specs=pl.BlockSpec((MOE_TM, half), rows),
            scratch_shapes=[pltpu.VMEM((d, de), BF16), pltpu.VMEM((d, de), BF16), pltpu.VMEM((de, d), BF16)],
        ),
        out_shape=jax.ShapeDtypeStruct((p, half), jnp.int32),
        compiler_params=_cparams(("arbitrary",)),
        name="moe_experts",
    )(tile_expert, tile_valid, n_used, xs, w1, w3, w2)


def _combine_kernel(y0_ref, y1_ref, wts_ref, x_ref, gate_ref, fg_ref, o_ref, *, final_norm):
    wts = wts_ref[...]
    moe = wts[:, 0:1] * _unpack_bf16_pairs(y0_ref[...]) + wts[:, 1:2] * _unpack_bf16_pairs(y1_ref[...])
    xn = x_ref[...] + gate_ref[0] * moe
    if final_norm:
        xn = xn * lax.rsqrt(jnp.mean(xn * xn, axis=-1, keepdims=True) + NORM_EPS) * fg_ref[...]
    o_ref[...] = xn


def _combine_call(yg, wts, x2, mod3, final_g, seq, gate_blk, final_norm, tm=512):
    n, d = x2.shape
    tpb = seq // tm
    slot1 = n // tm
    return pl.pallas_call(
        functools.partial(_combine_kernel, final_norm=final_norm),
        grid=(n // tm,),
        in_specs=[
            pl.BlockSpec((tm, d // 2), lambda i: (i, 0)),
            pl.BlockSpec((tm, d // 2), lambda i: (i + slot1, 0)),
            pl.BlockSpec((tm, LANES), lambda i: (i, 0)),
            pl.BlockSpec((tm, d), lambda i: (i, 0)),
            pl.BlockSpec((1, 1, d), lambda i: (i // tpb, 0, gate_blk)),
            pl.BlockSpec((1, d), lambda i: (0, 0)),
        ],
        out_specs=pl.BlockSpec((tm, d), lambda i: (i, 0)),
        out_shape=jax.ShapeDtypeStruct((n, d), F32),
        compiler_params=_cparams(("parallel",)),
        name="moe_combine",
    )(yg, yg, wts, x2, mod3, final_g.reshape(1, d))


def _moe_plan(eid, counts_f):
    n = eid.shape[0]
    nt = (2 * n) // MOE_TM + N_EXPERTS
    counts = counts_f[0, :N_EXPERTS].astype(jnp.int32)
    tiles = (counts + MOE_TM - 1) // MOE_TM
    tile_end = jnp.cumsum(tiles)
    tile_start = tile_end - tiles
    n_used = tile_end[-1:]
    tile_iota = jnp.arange(nt, dtype=jnp.int32)
    tile_expert = jnp.sum(jnp.minimum(tile_iota, n_used - 1)[:, None] >= tile_end[None, :], axis=1, dtype=jnp.int32)
    tile_valid = jnp.clip(counts[tile_expert] - (tile_iota - tile_start[tile_expert]) * MOE_TM, 0, MOE_TM)
    experts = jnp.arange(N_EXPERTS, dtype=jnp.int32)
    row0 = jnp.sum(jnp.where(eid[:, 0:2, None] == experts[None, None, :], tile_start * MOE_TM, 0), axis=-1)
    pos = row0 + eid[:, 2:4]
    return pos[:, 0], pos[:, 1], tile_expert, tile_valid, n_used


def kernel(x, c, positions, ada_w, ada_b, norm1_g, norm2_g, w_in, hg_lb_table, hg_norm_w, rw_mu, rw_w0, rw_w2,
           rw_a0, rw_a2, rw_g2, rw_k_k, rw_k_a, rw_r_k, rw_ln_w, rw_ln_b, br_hg, br_ret, br_rw, w_out,
           router_g, router_e, moe_w1, moe_w3, moe_w2, final_g):
    b, t, d = x.shape
    depth = ada_w.shape[0]
    n = b * t
    assert w_in.shape[2] == IN_COLS and d == 1024

    lb_p = jax.nn.softmax(hg_lb_table.astype(F32), axis=0)
    lower_bounds = jnp.cumsum(lb_p, axis=0) - lb_p[0]

    mod = _mod_call(c, ada_w, ada_b)
    cos2, sin2 = _rope_call(positions, RET_DK)
    n_gate = 3 * d
    x2 = x.reshape(n, d)
    for l in range(depth):
        mod3 = mod[l].reshape(b, 1, 6 * d)
        w_perm = jnp.concatenate([w_in[l][:, IN_COLS - n_gate:], w_in[l][:, :IN_COLS - n_gate]], axis=1)
        z2 = _inproj_call(x2, norm1_g[l], mod3, w_perm.astype(BF16), t, scale_blk=1, shift_blk=0)
        z3 = z2.reshape(b, t, IN_COLS)
        o_hg = _hgrn2_call(z3, lower_bounds[l], hg_norm_w[l])
        o_ret = _ret_call(z3, cos2, sin2)
        o_rw = _rwkv_call(z3, rw_mu[l], rw_w0[l], rw_w2[l], rw_a0[l], rw_a2[l], rw_g2[l], rw_k_k[l],
                          rw_k_a[l], rw_r_k[l], rw_ln_w[l], rw_ln_b[l])
        x2 = _merge_call(o_hg.reshape(n, HG_W), o_ret.reshape(n, RET_W), o_rw.reshape(n, RW_W), z2, x2, mod3,
                         br_hg[l].astype(BF16), br_ret[l].astype(BF16), br_rw[l].astype(BF16),
                         w_out[l].astype(BF16), t, gate_blk=2)
        hp, eid, wts, counts = _route_call(x2, norm2_g[l], mod3, router_g[l], router_e[l], t, scale_blk=4,
                                           shift_blk=3)
        pos0, pos1, tile_expert, tile_valid, n_used = _moe_plan(eid, counts)
        xs = _sc_scatter2(hp, pos0, pos1, (2 * n // MOE_TM + N_EXPERTS) * MOE_TM)
        ys = _gexperts_call(xs, tile_expert + l * N_EXPERTS, tile_valid, n_used,
                            moe_w1.reshape((-1,) + moe_w1.shape[2:]), moe_w3.reshape((-1,) + moe_w3.shape[2:]),
                            moe_w2.reshape((-1,) + moe_w2.shape[2:]))
        yg = _sc_gather(ys, jnp.concatenate([pos0, pos1]))
        x2 = _combine_call(yg, wts, x2, mod3, final_g, t, gate_blk=5, final_norm=(l == depth - 1))
    return x2.reshape(b, t, d)
```

```python
import functools

import jax
import jax.numpy as jnp
from jax import lax
from jax.experimental import pallas as pl
from jax.experimental.pallas import tpu as pltpu
from jax.experimental.pallas import tpu_sc as plsc

F32 = jnp.float32
BF16 = jnp.bfloat16
HIGHEST = lax.Precision.HIGHEST

HG_HEADS = 4
HG_DK = 128
HG_W = HG_HEADS * HG_DK
RET_HEADS = 4
RET_DK = 128
RET_W = RET_HEADS * RET_DK
RW_HEADS = 8
RW_N = 64
RW_W = RW_HEADS * RW_N
RW_DECAY_LORA = 64
RW_A_LORA = 64
RW_GATE_LORA = 128
RW_COLS = 3 * RW_W + RW_DECAY_LORA + RW_A_LORA + RW_GATE_LORA
RW_GN_EPS = 64e-5
N_GROUPS = 4
EXPERTS_PER_GROUP = 8
N_EXPERTS = N_GROUPS * EXPERTS_PER_GROUP
ROPE_THETA = 10000.0
NORM_EPS = 1e-6

LANES = 128
LOG2E = 1.4426950408889634
VMEM_LIMIT = 56 * 1024 * 1024

GATE_OFF = 0
HG_OFF = 3 * 1024
RET_OFF = HG_OFF + 4 * HG_W
RW_OFF = RET_OFF + 4 * RET_W
IN_COLS = RW_OFF + RW_COLS

HG_CHUNK = 64
HG_SUB = 16
HG_SAFE_SPAN = 60.0
RW_CHUNK = 64
RW_BLK = 16
RW_TB = 256
Z_DTYPE = BF16


def _cparams(sem):
    return pltpu.CompilerParams(dimension_semantics=sem, vmem_limit_bytes=VMEM_LIMIT)


def _dot(a, b, precision=None):
    return jnp.dot(a, b, preferred_element_type=F32, precision=precision)


def _dot_nt(a, b, precision=None):
    return lax.dot_general(a, b, (((1,), (1,)), ((), ())), preferred_element_type=F32, precision=precision)


def _dot_tn(a, b, precision=None):
    return lax.dot_general(a, b, (((0,), (0,)), ((), ())), preferred_element_type=F32, precision=precision)


def _split_bf16(x):
    hi = x.astype(BF16)
    return hi, (x - hi.astype(F32)).astype(BF16)


def _dot_x3(a, b):
    ah, al = _split_bf16(a)
    bh, bl = _split_bf16(b)
    return _dot(ah, bh) + _dot(ah, bl) + _dot(al, bh)


def _dot_x2_lhs(a, b_exact):
    ah, al = _split_bf16(a)
    return _dot(ah, b_exact) + _dot(al, b_exact)


def _dot_x2_rhs(a_exact, b):
    bh, bl = _split_bf16(b)
    return _dot(a_exact, bh) + _dot(a_exact, bl)


def _bdot(a, b):
    return _dot(a.astype(BF16), b.astype(BF16))


def _sigmoid(x):
    return 0.5 * jnp.tanh(0.5 * x) + 0.5


def _silu(x):
    return x * _sigmoid(x)


def _rms_mod(x, gain, scale, shift):
    y = x * lax.rsqrt(jnp.mean(x * x, axis=-1, keepdims=True) + NORM_EPS)
    return (y * gain) * (1.0 + scale) + shift


def _mod_kernel(c_ref, w_ref, b_ref, o_ref):
    c = c_ref[...]
    o_ref[0] = _dot(_silu(c), w_ref[0], HIGHEST) + b_ref[0]


def _mod_call(c, ada_w, ada_b):
    depth, d, d6 = ada_w.shape
    b = c.shape[0]
    nblk = d6 // d
    return pl.pallas_call(
        _mod_kernel,
        grid=(depth, nblk),
        in_specs=[
            pl.BlockSpec((b, d), lambda l, j: (0, 0)),
            pl.BlockSpec((1, d, d), lambda l, j: (l, 0, j)),
            pl.BlockSpec((1, 1, d), lambda l, j: (l, 0, j)),
        ],
        out_specs=pl.BlockSpec((1, b, d), lambda l, j: (l, 0, j)),
        out_shape=jax.ShapeDtypeStruct((depth, b, d6), F32),
        compiler_params=_cparams(("parallel", "parallel")),
        name="adaln_mod",
    )(c, ada_w, ada_b.reshape(depth, 1, d6))


def _rope_kernel(pos_ref, freq_ref, sign_ref, cos_ref, sin_ref):
    ang = pos_ref[0].astype(F32) * freq_ref[...]
    cos_ref[0] = jnp.cos(ang)
    sin_ref[0] = jnp.sin(ang) * sign_ref[...]


def _rope_call(positions, d):
    b, t = positions.shape
    tb = min(t, 512)
    inv_freq = ROPE_THETA ** (-jnp.arange(0, d, 2, dtype=F32) / d)
    freq2 = jnp.concatenate([inv_freq, inv_freq]).reshape(1, d)
    sign2 = jnp.concatenate([-jnp.ones((d // 2,), F32), jnp.ones((d // 2,), F32)]).reshape(1, d)
    out = jax.ShapeDtypeStruct((b, t, d), F32)
    return pl.pallas_call(
        _rope_kernel,
        grid=(b, t // tb),
        in_specs=[
            pl.BlockSpec((1, tb, 1), lambda i, j: (i, j, 0)),
            pl.BlockSpec((1, d), lambda i, j: (0, 0)),
            pl.BlockSpec((1, d), lambda i, j: (0, 0)),
        ],
        out_specs=[pl.BlockSpec((1, tb, d), lambda i, j: (i, j, 0))] * 2,
        out_shape=[out, out],
        compiler_params=_cparams(("parallel", "parallel")),
        name="rope_tables",
    )(positions.reshape(b, t, 1), freq2, sign2)


def _inproj_kernel(x_ref, g_ref, scale_ref, shift_ref, w_ref, o_ref, h_ref):
    @pl.when(pl.program_id(1) == 0)
    def _():
        h = _rms_mod(x_ref[...], g_ref[...], scale_ref[0], shift_ref[0])
        h_ref[...] = h.astype(BF16)

    o_ref[...] = _dot(h_ref[...], w_ref[...]).astype(o_ref.dtype)


def _inproj_call(x2, gain, mod3, w_bf16, seq, scale_blk, shift_blk, tm=2048, tn=1792):
    n, d = x2.shape
    cols = w_bf16.shape[1]
    tpb = seq // tm
    return pl.pallas_call(
        _inproj_kernel,
        grid=(n // tm, cols // tn),
        in_specs=[
            pl.BlockSpec((tm, d), lambda i, j: (i, 0)),
            pl.BlockSpec((1, d), lambda i, j: (0, 0)),
            pl.BlockSpec((1, 1, d), lambda i, j: (i // tpb, 0, scale_blk)),
            pl.BlockSpec((1, 1, d), lambda i, j: (i // tpb, 0, shift_blk)),
            pl.BlockSpec((d, tn), lambda i, j: (0, j)),
        ],
        out_specs=pl.BlockSpec((tm, tn), lambda i, j: (i, j)),
        out_shape=jax.ShapeDtypeStruct((n, cols), Z_DTYPE),
        scratch_shapes=[pltpu.VMEM((tm, d), BF16)],
        compiler_params=_cparams(("parallel", "arbitrary")),
        name="norm_inproj",
    )(x2, gain.reshape(1, d), mod3, mod3, w_bf16)


def _hgrn2_block(zs, lbs, nw, sts, factored):
    hs = range(len(zs))
    tb = zs[0][0].shape[0]
    c, sub = HG_CHUNK, HG_SUB
    nc, ns, nb = tb // c, c // sub, tb // sub
    f = [lbs[h] + (1.0 - lbs[h]) * _sigmoid(zs[h][1]) for h in hs]
    logf = [jnp.log(jnp.maximum(f[h], 1e-30)) for h in hs]
    q = [_silu(zs[h][0]) * (HG_DK ** -0.5) for h in hs]
    k = [1.0 - f[h] for h in hs]
    v = [zs[h][2] for h in hs]
    v_b = [v[h].astype(BF16) for h in hs]
    row = lax.broadcasted_iota(jnp.int32, (tb, tb), 0)
    col = lax.broadcasted_iota(jnp.int32, (tb, tb), 1)
    tri = jnp.where(col >= (row // c) * c, jnp.where(row >= col, 1.0, 0.0), 0.0).astype(BF16)
    cum = [_dot_x2_rhs(tri, logf[h]) for h in hs]
    cum3 = [cum[h].reshape(nb, sub, HG_DK) for h in hs]
    ref3 = [cum3[h][:, 0:1, :] - logf[h].reshape(nb, sub, HG_DK)[:, 0:1, :] for h in hs]
    span = functools.reduce(jnp.maximum, [jnp.max(ref3[h] - cum3[h][:, sub - 1:sub, :]) for h in hs])
    qe = [(q[h] * jnp.exp(cum[h])).astype(BF16) for h in hs]

    offd = [(h, ci * c, ci * c + sub * i) for h in hs for ci in range(nc) for i in range(1, ns)]
    base = [cum[h][lo - 1:lo] for h, _, lo in offd]
    qt = [(q[h][lo:lo + sub] * jnp.exp(cum[h][lo:lo + sub] - base[j])).astype(BF16)
          for j, (h, _, lo) in enumerate(offd)]
    kt = [(k[h][r0:lo] * jnp.exp(base[j] - cum[h][r0:lo])).astype(BF16) for j, (h, r0, lo) in enumerate(offd)]
    a = [_dot_nt(qt[j], kt[j]).astype(BF16) for j in range(len(offd))]
    av = {(h, lo): _dot(a[j], v_b[h][r0:lo]) for j, (h, r0, lo) in enumerate(offd)}

    cs = [slice(ci * c, (ci + 1) * c) for ci in range(nc)]
    hc = [(h, ci) for h in hs for ci in range(nc)]
    last = {(h, ci): cum[h][(ci + 1) * c - 1:(ci + 1) * c] for h, ci in hc}
    kd = {(h, ci): (k[h][cs[ci]] * jnp.exp(last[h, ci] - cum[h][cs[ci]])).astype(BF16) for h, ci in hc}
    inc = {(h, ci): _dot_tn(v_b[h][cs[ci]], kd[h, ci]) for h, ci in hc}
    s_in = {(h, 0): sts[h] for h in hs}
    for ci in range(nc):
        for h in hs:
            s_in[h, ci + 1] = s_in[h, ci] * jnp.exp(last[h, ci]) + inc[h, ci]
    o_inter = {(h, ci): _dot_nt(qe[h][cs[ci]], s_in[h, ci].astype(BF16)) for h, ci in hc}

    if factored:
        qf = [(q[h] * jnp.exp(cum3[h] - ref3[h]).reshape(tb, HG_DK)).astype(BF16) for h in hs]
        kf = [(k[h] * jnp.exp(ref3[h] - cum3[h]).reshape(tb, HG_DK)).astype(BF16) for h in hs]
        rc = lax.broadcasted_iota(jnp.int32, (c, c), 0)
        cc = lax.broadcasted_iota(jnp.int32, (c, c), 1)
        keep = (rc >= cc) & (rc // sub == cc // sub)
        a_d = {(h, ci): jnp.where(keep, _dot_nt(qf[h][cs[ci]], kf[h][cs[ci]]), 0.0).astype(BF16) for h, ci in hc}
        dg = {(h, ci): _dot(a_d[h, ci], v_b[h][cs[ci]]) for h, ci in hc}
        diag = [jnp.concatenate([dg[h, ci] for ci in range(nc)], axis=0) for h in hs]
    else:
        gb = 4
        trow = lax.broadcasted_iota(jnp.int32, (gb, sub, HG_DK), 1)
        diag = []
        for h in hs:
            c2 = cum[h] * LOG2E
            ks2 = c2 - jnp.log2(k[h])
            parts = []
            for g0 in range(0, nb, gb):
                rws = slice(g0 * sub, (g0 + gb) * sub)
                c23, ks23, q3, v3 = (x[rws].reshape(gb, sub, HG_DK) for x in (c2, ks2, q[h], v[h]))
                acc = jnp.zeros((gb, sub, HG_DK), F32)
                for s in range(sub):
                    e = jnp.exp2(jnp.where(trow >= s, c23 - ks23[:, s:s + 1, :], -jnp.inf))
                    a_col = jnp.sum(q3 * e, axis=-1, keepdims=True)
                    acc = acc + a_col * v3[:, s:s + 1, :]
                parts.append(acc.reshape(gb * sub, HG_DK))
            diag.append(jnp.concatenate(parts, axis=0))

    outs = []
    for h in hs:
        pieces = []
        for ci in range(nc):
            for i in range(ns):
                lo = ci * c + sub * i
                piece = o_inter[h, ci][sub * i:sub * (i + 1)] + diag[h][lo:lo + sub]
                pieces.append(piece + av[h, lo] if i > 0 else piece)
        o = jnp.concatenate(pieces, axis=0)
        o = o * lax.rsqrt(jnp.mean(o * o, axis=-1, keepdims=True) + NORM_EPS)
        outs.append(o * nw * _silu(zs[h][3]))
    return outs, [s_in[h, nc] for h in hs], span


def _hgrn2_kernel(zq_ref, zf_ref, zi_ref, zg_ref, lb_ref, nw_ref, o_ref, st_ref):
    @pl.when(pl.program_id(1) == 0)
    def _():
        st_ref[...] = jnp.zeros_like(st_ref)

    hs = range(HG_HEADS)
    sl = [slice(h * HG_DK, (h + 1) * HG_DK) for h in hs]

    def run(factored):
        zs = [tuple(r[0, :, sl[h]].astype(F32) for r in (zq_ref, zf_ref, zi_ref, zg_ref)) for h in hs]
        outs, sts, span = _hgrn2_block(zs, [lb_ref[:, sl[h]] for h in hs], nw_ref[...],
                                       [st_ref[h] for h in hs], factored)
        return jnp.concatenate(outs, axis=1), sts, span

    st_old = [st_ref[h] for h in hs]
    o, st_new, span = run(True)
    for h in hs:
        st_ref[h] = st_new[h]
    o_ref[0] = o.astype(o_ref.dtype)

    @pl.when(span > HG_SAFE_SPAN)
    def _():
        for h in hs:
            st_ref[h] = st_old[h]
        o2, st2, _ = run(False)
        for h in hs:
            st_ref[h] = st2[h]
        o_ref[0] = o2.astype(o_ref.dtype)


def _hgrn2_call(z3, lower_bound, norm_w, tb=256):
    b, t, _ = z3.shape
    tb = min(tb, t)
    base = HG_OFF // HG_W

    def zspec(part):
        return pl.BlockSpec((1, tb, HG_W), lambda i, j: (i, j, base + part))

    return pl.pallas_call(
        _hgrn2_kernel,
        grid=(b, t // tb),
        in_specs=[
            zspec(0), zspec(1), zspec(2), zspec(3),
            pl.BlockSpec((1, HG_W), lambda i, j: (0, 0)),
            pl.BlockSpec((1, LANES), lambda i, j: (0, 0)),
        ],
        out_specs=pl.BlockSpec((1, tb, HG_W), lambda i, j: (i, j, 0)),
        out_shape=jax.ShapeDtypeStruct((b, t, HG_W), BF16),
        scratch_shapes=[pltpu.VMEM((HG_HEADS, HG_DK, HG_DK), F32)],
        compiler_params=_cparams(("parallel", "arbitrary")),
        name="hgrn2_mixer",
    )(z3, z3, z3, z3, lower_bound.reshape(1, HG_W), norm_w.reshape(1, HG_DK))


def _ret_kernel(zq_ref, zk_ref, zv_ref, zg_ref, cos_ref, sin_ref, o_ref, st_ref, *, chunk):
    @pl.when(pl.program_id(1) == 0)
    def _():
        st_ref[...] = jnp.zeros_like(st_ref)

    cos2 = cos_ref[0]
    sin2 = sin_ref[0]
    half = RET_DK // 2
    hs = range(RET_HEADS)
    sl = [slice(h * RET_DK, (h + 1) * RET_DK) for h in hs]
    lg = [jnp.log(jnp.full((1, 1), 1.0 - 2.0 ** (-5.0 - h), F32)) for h in hs]

    def rope(z):
        return z * cos2 + pltpu.roll(z, half, 1) * sin2

    row = lax.broadcasted_iota(jnp.int32, (chunk, chunk), 0)
    col = lax.broadcasted_iota(jnp.int32, (chunk, chunk), 1)
    rel = (row - col).astype(F32)
    relp = jnp.maximum(rel, 0.0)
    tcol = lax.broadcasted_iota(jnp.int32, (chunk, 1), 0).astype(F32)
    q = [rope(zq_ref[0, :, sl[h]].astype(F32)) * (RET_DK ** -0.5) for h in hs]
    k = [rope(zk_ref[0, :, sl[h]].astype(F32)) for h in hs]
    v_b = [zv_ref[0, :, sl[h]].astype(BF16) for h in hs]
    st = [st_ref[h] for h in hs]
    dmask = [jnp.where(rel >= 0.0, jnp.exp(relp * lg[h]), 0.0) for h in hs]
    scores = [(_dot_nt(q[h].astype(BF16), k[h].astype(BF16)) * dmask[h]).astype(BF16) for h in hs]
    qx = [(q[h] * jnp.exp((tcol + 1.0) * lg[h])).astype(BF16) for h in hs]
    kz = [(k[h] * jnp.exp((chunk - 1.0 - tcol) * lg[h])).astype(BF16) for h in hs]
    o = [_dot(scores[h], v_b[h]) + _dot_nt(qx[h], st[h].astype(BF16)) for h in hs]
    for h in hs:
        st_ref[h] = st[h] * jnp.exp(chunk * lg[h]) + _dot_tn(v_b[h], kz[h])
    o = [o[h] * lax.rsqrt(jnp.mean(o[h] * o[h], axis=-1, keepdims=True) + NORM_EPS) for h in hs]
    o_ref[0] = (jnp.concatenate(o, axis=1) * _silu(zg_ref[0].astype(F32))).astype(o_ref.dtype)


def _ret_call(z3, cos2, sin2, chunk=256):
    b, t, _ = z3.shape
    chunk = min(chunk, t)
    base = RET_OFF // RET_W

    def zspec(part):
        return pl.BlockSpec((1, chunk, RET_W), lambda i, j: (i, j, base + part))

    tab = pl.BlockSpec((1, chunk, RET_DK), lambda i, j: (i, j, 0))
    return pl.pallas_call(
        functools.partial(_ret_kernel, chunk=chunk),
        grid=(b, t // chunk),
        in_specs=[zspec(0), zspec(1), zspec(2), zspec(3), tab, tab],
        out_specs=pl.BlockSpec((1, chunk, RET_W), lambda i, j: (i, j, 0)),
        out_shape=jax.ShapeDtypeStruct((b, t, RET_W), BF16),
        scratch_shapes=[pltpu.VMEM((RET_HEADS, RET_DK, RET_DK), F32)],
        compiler_params=_cparams(("parallel", "arbitrary")),
        name="retention_mixer",
    )(z3, z3, z3, z3, cos2, sin2)


def _inv_unit_lower(a, eye, blk_mask):
    c = a[0].shape[0]
    m = range(len(a))
    a_bd = [jnp.where(blk_mask, a[i], 0.0) for i in m]
    a_off = [a[i] - a_bd[i] for i in m]
    a2 = [_bdot(a_bd[i], a_bd[i]) for i in m]
    p = [eye + a_bd[i] for i in m]
    r = [_bdot(jnp.concatenate([p[i], a2[i]], axis=0), a2[i]) for i in m]
    p = [p[i] + r[i][:c] for i in m]
    a4 = [r[i][c:] for i in m]
    r = [_bdot(jnp.concatenate([p[i], a4[i]], axis=0), a4[i]) for i in m]
    p = [p[i] + r[i][:c] for i in m]
    a8 = [r[i][c:] for i in m]
    t_bd = [p[i] + _bdot(p[i], a8[i]) for i in m]
    n = [_bdot(t_bd[i], a_off[i]) for i in m]
    r = [_bdot(n[i], jnp.concatenate([n[i], t_bd[i]], axis=1)) for i in m]
    z = [t_bd[i] + r[i][:, c:] for i in m]
    return [z[i] + _bdot(r[i][:, :c], z[i]) for i in m]


def _rwkv_kernel(z_ref, mu_ref, w0_ref, w2_ref, a0_ref, a2_ref, g2_ref, kk_ref, ka_ref, rk_ref,
                 lnw_ref, lnb_ref, seg_ref, o_ref, s_ref, prev_ref):
    c = RW_CHUNK
    tb = z_ref.shape[1]
    nck = tb // c

    @pl.when(pl.program_id(1) == 0)
    def _():
        s_ref[...] = jnp.zeros_like(s_ref)
        prev_ref[...] = jnp.zeros_like(prev_ref)

    z = z_ref[0].astype(F32)
    rows = lax.broadcasted_iota(jnp.int32, (tb, 1), 0)
    z_prev = jnp.where(rows == 0, prev_ref[...], pltpu.roll(z, 1, 0))
    prev_ref[...] = z[tb - 1:tb]
    zs = z + mu_ref[...] * (z_prev - z)
    r = zs[:, 0:RW_W]
    k = zs[:, RW_W:2 * RW_W]
    v = zs[:, 2 * RW_W:3 * RW_W]
    off = 3 * RW_W
    w_lo = zs[:, off:off + RW_DECAY_LORA]
    a_lo = zs[:, off + RW_DECAY_LORA:off + RW_DECAY_LORA + RW_A_LORA]
    g_lo = zs[:, off + RW_DECAY_LORA + RW_A_LORA:]

    wx = -(w0_ref[...] + _dot_x3(jnp.tanh(w_lo), w2_ref[...]))
    softplus = jnp.maximum(wx, 0.0) + jnp.log(1.0 + jnp.exp(-jnp.abs(wx)))
    logw = -jnp.exp(-softplus - 0.5)
    a = _sigmoid(a0_ref[...] + _dot_x3(a_lo, a2_ref[...]))
    g = _dot_x3(_sigmoid(g_lo), g2_ref[...])
    seg = seg_ref[...]
    kk = k * kk_ref[...]
    kk = kk * lax.rsqrt(jnp.maximum(_dot_x2_lhs(kk * kk, seg), 1e-24))
    k2 = k * (1.0 + (a - 1.0) * ka_ref[...])

    row = lax.broadcasted_iota(jnp.int32, (c, c), 0)
    col = lax.broadcasted_iota(jnp.int32, (c, c), 1)
    blk_mask = (row // RW_BLK) == (col // RW_BLK)
    eye = (row == col).astype(F32)
    row2 = lax.broadcasted_iota(jnp.int32, (c, 2 * c), 0)
    col2 = lax.broadcasted_iota(jnp.int32, (c, 2 * c), 1) % c
    incl2 = row2 >= col2
    strict2 = row2 > col2
    rowb = lax.broadcasted_iota(jnp.int32, (tb, tb), 0)
    colb = lax.broadcasted_iota(jnp.int32, (tb, tb), 1)
    tri = jnp.where(colb >= (rowb // c) * c, jnp.where(rowb >= colb, 1.0, 0.0), 0.0).astype(BF16)
    cw = _dot_x2_rhs(tri, logw)
    w_inv = jnp.exp(-cw)
    last = jnp.concatenate([jnp.broadcast_to(cw[(ci + 1) * c - 1:(ci + 1) * c], (c, RW_W)) for ci in range(nck)],
                           axis=0)
    w_rest = jnp.exp(last - cw)
    beta = a * kk
    alpha_t = -kk * jnp.exp(cw - logw)
    r_t = r * jnp.exp(cw)
    beta_h = beta * w_inv
    k_h = k2 * w_inv
    beta_d = beta * w_rest
    k_d = k2 * w_rest

    hs = range(RW_HEADS)
    ph = [(ci, h) for ci in range(nck) for h in hs]
    m = range(len(ph))
    rs = [slice(ci * c, (ci + 1) * c) for ci, _ in ph]
    sl = [slice(h * RW_N, (h + 1) * RW_N) for _, h in ph]
    v_h = [v[rs[i], sl[i]] for i in m]
    lhs = [jnp.concatenate([alpha_t[rs[i], sl[i]], r_t[rs[i], sl[i]]], axis=0).astype(BF16) for i in m]
    rhs = [jnp.concatenate([beta_h[rs[i], sl[i]], k_h[rs[i], sl[i]]], axis=0).astype(BF16) for i in m]
    big = [_dot_nt(lhs[i], rhs[i]) for i in m]
    a_a = [jnp.where(strict2, big[i][:c], 0.0) for i in m]
    a_r = [jnp.where(incl2, big[i][c:], 0.0).astype(BF16) for i in m]
    t_inv = _inv_unit_lower([a_a[i][:, :c] for i in m], eye, blk_mask)
    av = [_bdot(a_a[i][:, c:], v_h[i]) for i in m]
    bk_d = [jnp.concatenate([beta_d[rs[i], sl[i]], k_d[rs[i], sl[i]]], axis=0).astype(BF16) for i in m]
    s_cur = [s_ref[h] for h in hs]
    o_chunks = []
    for ci in range(nck):
        ix = [ci * RW_HEADS + h for h in hs]
        sd = [_dot_nt(lhs[ix[h]], s_cur[h].astype(BF16)) for h in hs]
        u = [_bdot(t_inv[ix[h]], sd[h][:c] + av[ix[h]]) for h in hs]
        uv = [jnp.concatenate([u[h], v_h[ix[h]]], axis=0).astype(BF16) for h in hs]
        o_chunks.append(jnp.concatenate([sd[h][c:] + _dot(a_r[ix[h]], uv[h]) for h in hs], axis=1))
        w_last = jnp.exp(cw[(ci + 1) * c - 1:(ci + 1) * c])
        s_cur = [s_cur[h] * w_last[:, sl[h]] + _dot_tn(uv[h], bk_d[ix[h]]) for h in hs]
    for h in hs:
        s_ref[h] = s_cur[h]
    o = jnp.concatenate(o_chunks, axis=0)

    mean = _dot_x2_lhs(o, seg) * (1.0 / RW_N)
    dev = o - mean
    var = _dot_x2_lhs(dev * dev, seg) * (1.0 / RW_N)
    o = dev * lax.rsqrt(var + RW_GN_EPS) * lnw_ref[...] + lnb_ref[...]
    bonus = _dot_x2_lhs(r * k2 * rk_ref[...], seg) * v
    o_ref[0] = ((o + bonus) * g).astype(o_ref.dtype)


def _rwkv_call(z3, mu, w0, w2, a0, a2, g2, k_k, k_a, r_k, ln_w, ln_b):
    b, t, _ = z3.shape
    c = min(RW_TB, t)
    hid = lax.broadcasted_iota(jnp.int32, (RW_W, RW_W), 0) // RW_N
    seg = (hid == hid.T).astype(BF16)

    def vec(n):
        return pl.BlockSpec((1, n), lambda i, j: (0, 0))

    def mat(m, n):
        return pl.BlockSpec((m, n), lambda i, j: (0, 0))

    return pl.pallas_call(
        _rwkv_kernel,
        grid=(b, t // c),
        in_specs=[
            pl.BlockSpec((1, c, RW_COLS), lambda i, j: (i, j, RW_OFF // RW_COLS)),
            vec(RW_COLS), vec(RW_W), mat(RW_DECAY_LORA, RW_W), vec(RW_W), mat(RW_A_LORA, RW_W),
            mat(RW_GATE_LORA, RW_W), vec(RW_W), vec(RW_W), vec(RW_W), vec(RW_W), vec(RW_W),
            mat(RW_W, RW_W),
        ],
        out_specs=pl.BlockSpec((1, c, RW_W), lambda i, j: (i, j, 0)),
        out_shape=jax.ShapeDtypeStruct((b, t, RW_W), BF16),
        scratch_shapes=[pltpu.VMEM((RW_HEADS, RW_N, RW_N), F32), pltpu.VMEM((1, RW_COLS), F32)],
        compiler_params=_cparams(("parallel", "arbitrary")),
        name="rwkv7_mixer",
    )(z3, mu.reshape(1, -1), w0.reshape(1, -1), w2, a0.reshape(1, -1), a2, g2, k_k.reshape(1, -1),
      k_a.reshape(1, -1), r_k.reshape(1, -1), ln_w.reshape(1, -1), ln_b.reshape(1, -1), seg)


def _merge_kernel(ohg_ref, oret_ref, orw_ref, zg_ref, x_ref, gate_ref, bhg_ref, bret_ref, brw_ref,
                  wout_ref, o_ref):
    d = x_ref.shape[1]
    y = _sigmoid(zg_ref[:, 0:d].astype(F32)) * _dot(ohg_ref[...], bhg_ref[...])
    y = y + _sigmoid(zg_ref[:, d:2 * d].astype(F32)) * _dot(oret_ref[...], bret_ref[...])
    y = y + _sigmoid(zg_ref[:, 2 * d:3 * d].astype(F32)) * _dot(orw_ref[...], brw_ref[...])
    o_ref[...] = x_ref[...] + gate_ref[0] * _dot(y.astype(BF16), wout_ref[...])


def _merge_call(o_hg, o_ret, o_rw, z2, x2, mod3, br_hg, br_ret, br_rw, w_out, seq, gate_blk, tm=512):
    n, d = x2.shape
    tpb = seq // tm

    def rows(w):
        return pl.BlockSpec((tm, w), lambda i: (i, 0))

    def full(m, k):
        return pl.BlockSpec((m, k), lambda i: (0, 0))

    return pl.pallas_call(
        _merge_kernel,
        grid=(n // tm,),
        in_specs=[
            rows(HG_W), rows(RET_W), rows(RW_W), rows(3 * d), rows(d),
            pl.BlockSpec((1, 1, d), lambda i: (i // tpb, 0, gate_blk)),
            full(HG_W, d), full(RET_W, d), full(RW_W, d), full(d, d),
        ],
        out_specs=rows(d),
        out_shape=jax.ShapeDtypeStruct((n, d), F32),
        compiler_params=_cparams(("parallel",)),
        name="merge_outproj",
    )(o_hg, o_ret, o_rw, z2, x2, mod3, br_hg, br_ret, br_rw, w_out)


def _pack_bf16_pairs(x):
    w = x.shape[1] // 2
    hi = pltpu.bitcast(x[:, :w].astype(BF16).astype(F32), jnp.uint32)
    lo = pltpu.bitcast(x[:, w:].astype(BF16).astype(F32), jnp.uint32)
    return pltpu.bitcast(hi | lax.shift_right_logical(lo, jnp.uint32(16)), jnp.int32)


def _unpack_bf16_pairs(p):
    u = pltpu.bitcast(p, jnp.uint32)
    hi = pltpu.bitcast(u & jnp.uint32(0xFFFF0000), F32)
    lo = pltpu.bitcast(lax.shift_left(u, jnp.uint32(16)), F32)
    return jnp.concatenate([hi, lo], axis=1)


def _route_kernel(x_ref, g_ref, scale_ref, shift_ref, rc_ref, hp_ref, eid_ref, wts_ref, cnt_ref):
    @pl.when(pl.program_id(0) == 0)
    def _():
        cnt_ref[...] = jnp.zeros_like(cnt_ref)

    h = _rms_mod(x_ref[...], g_ref[...], scale_ref[0], shift_ref[0])
    hp_ref[...] = _pack_bf16_pairs(h)
    tm = h.shape[0]
    lane = lax.broadcasted_iota(jnp.int32, (tm, LANES), 1)
    neg = -jnp.inf
    logits = _dot_x3(h, rc_ref[...])
    gl = jnp.where(lane < N_GROUPS, logits, neg)
    gmax = jnp.max(gl, axis=-1, keepdims=True)
    gidx = jnp.min(jnp.where(gl == gmax, lane, LANES), axis=-1, keepdims=True)
    gw = 1.0 / jnp.sum(jnp.exp(gl - gmax), axis=-1, keepdims=True)
    lo = N_GROUPS + gidx * EXPERTS_PER_GROUP
    el = jnp.where(lane >= lo, jnp.where(lane < lo + EXPERTS_PER_GROUP, logits, neg), neg)
    m1 = jnp.max(el, axis=-1, keepdims=True)
    l1 = jnp.min(jnp.where(el == m1, lane, LANES), axis=-1, keepdims=True)
    el2 = jnp.where(lane == l1, neg, el)
    m2 = jnp.max(el2, axis=-1, keepdims=True)
    l2 = jnp.min(jnp.where(el2 == m2, lane, LANES), axis=-1, keepdims=True)
    i1 = l1 - N_GROUPS
    i2 = l2 - N_GROUPS
    e2 = jnp.exp(m2 - m1)
    p1 = 1.0 / (1.0 + e2)
    p2 = e2 * p1
    oh1 = jnp.where(lane == i1, 1.0, 0.0)
    oh2 = jnp.where(lane == i2, 1.0, 0.0)
    row = lax.broadcasted_iota(jnp.int32, (tm, tm), 0)
    col = lax.broadcasted_iota(jnp.int32, (tm, tm), 1)
    earlier = jnp.where(row > col, 1.0, 0.0).astype(BF16)
    before = _dot(earlier, jnp.concatenate([oh1, oh2], axis=1).astype(BF16))
    tot1 = jnp.sum(oh1, axis=0, keepdims=True)
    carry = cnt_ref[...]
    r1 = jnp.sum(oh1 * (before[:, :LANES] + carry), axis=-1, keepdims=True).astype(jnp.int32)
    r2 = jnp.sum(oh2 * (before[:, LANES:] + (carry + tot1)), axis=-1, keepdims=True).astype(jnp.int32)
    cnt_ref[...] = carry + tot1 + jnp.sum(oh2, axis=0, keepdims=True)
    eid_ref[...] = jnp.where(lane == 0, i1, jnp.where(lane == 1, i2, jnp.where(lane == 2, r1,
                                                                             jnp.where(lane == 3, r2, 0))))
    wts_ref[...] = jnp.where(lane == 0, gw * p1, jnp.where(lane == 1, gw * p2, 0.0))


def _route_call(x2, gain, mod3, router_g, router_e, seq, scale_blk, shift_blk, tm=512):
    n, d = x2.shape
    tpb = seq // tm
    rc = jnp.pad(jnp.concatenate([router_g, router_e], axis=1), ((0, 0), (0, LANES - N_GROUPS - N_EXPERTS)))
    return pl.pallas_call(
        _route_kernel,
        grid=(n // tm,),
        in_specs=[
            pl.BlockSpec((tm, d), lambda i: (i, 0)),
            pl.BlockSpec((1, d), lambda i: (0, 0)),
            pl.BlockSpec((1, 1, d), lambda i: (i // tpb, 0, scale_blk)),
            pl.BlockSpec((1, 1, d), lambda i: (i // tpb, 0, shift_blk)),
            pl.BlockSpec((d, LANES), lambda i: (0, 0)),
        ],
        out_specs=[pl.BlockSpec((tm, d // 2), lambda i: (i, 0)), pl.BlockSpec((tm, LANES), lambda i: (i, 0)),
                   pl.BlockSpec((tm, LANES), lambda i: (i, 0)), pl.BlockSpec((1, LANES), lambda i: (0, 0))],
        out_shape=[jax.ShapeDtypeStruct((n, d // 2), jnp.int32), jax.ShapeDtypeStruct((n, LANES), jnp.int32),
                   jax.ShapeDtypeStruct((n, LANES), F32), jax.ShapeDtypeStruct((1, LANES), F32)],
        compiler_params=_cparams(("arbitrary",)),
        name="moe_route",
    )(x2, gain.reshape(1, d), mod3, mod3, rc)


SC_CORES = 2
SC_SUBCORES = 16
SC_WORKERS = SC_CORES * SC_SUBCORES
SC_ROWS = 64


def _sc_gather(table, idx):
    m = idx.shape[0]
    w = table.shape[1]
    per_worker = m // SC_WORKERS
    steps = per_worker // SC_ROWS
    assert per_worker * SC_WORKERS == m and steps * SC_ROWS == per_worker and steps % 2 == 0
    mesh = plsc.VectorSubcoreMesh(core_axis_name="c", subcore_axis_name="s")

    def body(table_hbm, idx_hbm, out_hbm, idx_v, rows_a, rows_b, sem_ga, sem_gb, sem_wa, sem_wb):
        wid = lax.axis_index("s") * SC_CORES + lax.axis_index("c")
        pltpu.sync_copy(idx_hbm.at[wid], idx_v)

        @pl.loop(0, steps, step=2)
        def _(j):
            row0 = wid * per_worker + j * SC_ROWS
            ga = pltpu.async_copy(table_hbm.at[idx_v.at[j]], rows_a, sem_ga)
            gb = pltpu.async_copy(table_hbm.at[idx_v.at[j + 1]], rows_b, sem_gb)
            ga.wait()
            wa = pltpu.async_copy(rows_a, out_hbm.at[pl.ds(row0, SC_ROWS)], sem_wa)
            gb.wait()
            wb = pltpu.async_copy(rows_b, out_hbm.at[pl.ds(row0 + SC_ROWS, SC_ROWS)], sem_wb)
            wa.wait()
            wb.wait()

    return pl.kernel(
        body,
        out_type=jax.ShapeDtypeStruct((m, w), table.dtype),
        mesh=mesh,
        scratch_types=[pltpu.VMEM((steps, SC_ROWS), jnp.int32), pltpu.VMEM((SC_ROWS, w), table.dtype),
                       pltpu.VMEM((SC_ROWS, w), table.dtype), pltpu.SemaphoreType.DMA, pltpu.SemaphoreType.DMA,
                       pltpu.SemaphoreType.DMA, pltpu.SemaphoreType.DMA],
        name="sc_row_gather",
    )(table, idx.reshape(SC_WORKERS, steps, SC_ROWS))


def _sc_scatter2(rows, idx0, idx1, p):
    n, w = rows.shape
    per_worker = n // SC_WORKERS
    steps = per_worker // SC_ROWS
    assert per_worker * SC_WORKERS == n and steps * SC_ROWS == per_worker and steps % 2 == 0
    mesh = plsc.VectorSubcoreMesh(core_axis_name="c", subcore_axis_name="s")

    def body(rows_hbm, i0_hbm, i1_hbm, out_hbm, i0_v, i1_v, buf_a, buf_b, s_ra, s_rb, s_a0, s_a1, s_b0, s_b1):
        wid = lax.axis_index("s") * SC_CORES + lax.axis_index("c")
        pltpu.sync_copy(i0_hbm.at[wid], i0_v)
        pltpu.sync_copy(i1_hbm.at[wid], i1_v)

        @pl.loop(0, steps, step=2)
        def _(j):
            row0 = wid * per_worker + j * SC_ROWS
            ra = pltpu.async_copy(rows_hbm.at[pl.ds(row0, SC_ROWS)], buf_a, s_ra)
            rb = pltpu.async_copy(rows_hbm.at[pl.ds(row0 + SC_ROWS, SC_ROWS)], buf_b, s_rb)
            ra.wait()
            a0 = pltpu.async_copy(buf_a, out_hbm.at[i0_v.at[j]], s_a0)
            a1 = pltpu.async_copy(buf_a, out_hbm.at[i1_v.at[j]], s_a1)
            rb.wait()
            b0 = pltpu.async_copy(buf_b, out_hbm.at[i0_v.at[j + 1]], s_b0)
            b1 = pltpu.async_copy(buf_b, out_hbm.at[i1_v.at[j + 1]], s_b1)
            a0.wait()
            a1.wait()
            b0.wait()
            b1.wait()

    return pl.kernel(
        body,
        out_type=jax.ShapeDtypeStruct((p, w), rows.dtype),
        mesh=mesh,
        scratch_types=[pltpu.VMEM((steps, SC_ROWS), jnp.int32), pltpu.VMEM((steps, SC_ROWS), jnp.int32),
                       pltpu.VMEM((SC_ROWS, w), rows.dtype), pltpu.VMEM((SC_ROWS, w), rows.dtype)]
        + [pltpu.SemaphoreType.DMA] * 6,
        name="sc_row_scatter",
    )(rows, idx0.reshape(SC_WORKERS, steps, SC_ROWS), idx1.reshape(SC_WORKERS, steps, SC_ROWS))


MOE_TM = 512


def _gexperts_kernel(te_ref, tv_ref, nu_ref, xs_ref, w1_ref, w3_ref, w2_ref, ys_ref, w1b_ref, w3b_ref, w2b_ref):
    i = pl.program_id(0)

    @pl.when((i == 0) | (te_ref[i] != te_ref[jnp.maximum(i - 1, 0)]))
    def _():
        w1b_ref[...] = w1_ref[0].astype(BF16)
        w3b_ref[...] = w3_ref[0].astype(BF16)
        w2b_ref[...] = w2_ref[0].astype(BF16)

    @pl.when(i < nu_ref[0])
    def _():
        hm = xs_ref.shape[0] // 2
        parts = [pl.ds(0, hm), pl.ds(hm, hm)]
        left = [tv_ref[i], tv_ref[i] - hm]
        rid = lax.broadcasted_iota(jnp.int32, (hm, xs_ref.shape[1]), 0)
        xb = [_unpack_bf16_pairs(jnp.where(rid < left[q], xs_ref[p, :], 0)).astype(BF16) for q, p in enumerate(parts)]
        up = [_dot(x, w1b_ref[...]) for x in xb]
        gt = [_dot(x, w3b_ref[...]) for x in xb]
        act = [(_silu(u) * g).astype(BF16) for u, g in zip(up, gt)]
        y = [_dot(a, w2b_ref[...]) for a in act]
        for p, yy in zip(parts, y):
            ys_ref[p, :] = _pack_bf16_pairs(yy)


def _gexperts_call(xs, tile_expert, tile_valid, n_used, w1, w3, w2):
    p, half = xs.shape
    ne, d, de = w1.shape
    nt = p // MOE_TM

    def rows(i, te, tv, nu):
        return (jnp.minimum(i, nu[0] - 1), 0)

    def wsel(i, te, tv, nu):
        return (te[i], 0, 0)

    return pl.pallas_call(
        _gexperts_kernel,
        grid_spec=pltpu.PrefetchScalarGridSpec(
            num_scalar_prefetch=3,
            grid=(nt,),
            in_specs=[
                pl.BlockSpec((MOE_TM, half), rows),
                pl.BlockSpec((1, d, de), wsel),
                pl.BlockSpec((1, d, de), wsel),
                pl.BlockSpec((1, de, d), wsel),
            ],
            out_specs=pl.BlockSpec((MOE_TM, half), rows),
            scratch_shapes=[pltpu.VMEM((d, de), BF16), pltpu.VMEM((d, de), BF16), pltpu.VMEM((de, d), BF16)],
        ),
        out_shape=jax.ShapeDtypeStruct((p, half), jnp.int32),
        compiler_params=_cparams(("arbitrary",)),
        name="moe_experts",
    )(tile_expert, tile_valid, n_used, xs, w1, w3, w2)


def _combine_kernel(y0_ref, y1_ref, wts_ref, x_ref, gate_ref, fg_ref, o_ref, *, final_norm):
    wts = wts_ref[...]
    moe = wts[:, 0:1] * _unpack_bf16_pairs(y0_ref[...]) + wts[:, 1:2] * _unpack_bf16_pairs(y1_ref[...])
    xn = x_ref[...] + gate_ref[0] * moe
    if final_norm:
        xn = xn * lax.rsqrt(jnp.mean(xn * xn, axis=-1, keepdims=True) + NORM_EPS) * fg_ref[...]
    o_ref[...] = xn


def _combine_call(yg, wts, x2, mod3, final_g, seq, gate_blk, final_norm, tm=512):
    n, d = x2.shape
    tpb = seq // tm
    slot1 = n // tm
    return pl.pallas_call(
        functools.partial(_combine_kernel, final_norm=final_norm),
        grid=(n // tm,),
        in_specs=[
            pl.BlockSpec((tm, d // 2), lambda i: (i, 0)),
            pl.BlockSpec((tm, d // 2), lambda i: (i + slot1, 0)),
            pl.BlockSpec((tm, LANES), lambda i: (i, 0)),
            pl.BlockSpec((tm, d), lambda i: (i, 0)),
            pl.BlockSpec((1, 1, d), lambda i: (i // tpb, 0, gate_blk)),
            pl.BlockSpec((1, d), lambda i: (0, 0)),
        ],
        out_specs=pl.BlockSpec((tm, d), lambda i: (i, 0)),
        out_shape=jax.ShapeDtypeStruct((n, d), F32),
        compiler_params=_cparams(("parallel",)),
        name="moe_combine",
    )(yg, yg, wts, x2, mod3, final_g.reshape(1, d))


def _moe_plan(eid, counts_f):
    n = eid.shape[0]
    nt = (2 * n) // MOE_TM + N_EXPERTS
    counts = counts_f[0, :N_EXPERTS].astype(jnp.int32)
    tiles = (counts + MOE_TM - 1) // MOE_TM
    tile_end = jnp.cumsum(tiles)
    tile_start = tile_end - tiles
    n_used = tile_end[-1:]
    tile_iota = jnp.arange(nt, dtype=jnp.int32)
    tile_expert = jnp.sum(jnp.minimum(tile_iota, n_used - 1)[:, None] >= tile_end[None, :], axis=1, dtype=jnp.int32)
    tile_valid = jnp.clip(counts[tile_expert] - (tile_iota - tile_start[tile_expert]) * MOE_TM, 0, MOE_TM)
    experts = jnp.arange(N_EXPERTS, dtype=jnp.int32)
    row0 = jnp.sum(jnp.where(eid[:, 0:2, None] == experts[None, None, :], tile_start * MOE_TM, 0), axis=-1)
    pos = row0 + eid[:, 2:4]
    return pos[:, 0], pos[:, 1], tile_expert, tile_valid, n_used


def kernel(x, c, positions, ada_w, ada_b, norm1_g, norm2_g, w_in, hg_lb_table, hg_norm_w, rw_mu, rw_w0, rw_w2,
           rw_a0, rw_a2, rw_g2, rw_k_k, rw_k_a, rw_r_k, rw_ln_w, rw_ln_b, br_hg, br_ret, br_rw, w_out,
           router_g, router_e, moe_w1, moe_w3, moe_w2, final_g):
    b, t, d = x.shape
    depth = ada_w.shape[0]
    n = b * t
    assert w_in.shape[2] == IN_COLS and d == 1024

    lb_p = jax.nn.softmax(hg_lb_table.astype(F32), axis=0)
    lower_bounds = jnp.cumsum(lb_p, axis=0) - lb_p[0]

    mod = _mod_call(c, ada_w, ada_b)
    cos2, sin2 = _rope_call(positions, RET_DK)
    n_gate = 3 * d
    x2 = x.reshape(n, d)
    for l in range(depth):
        mod3 = mod[l].reshape(b, 1, 6 * d)
        w_perm = jnp.concatenate([w_in[l][:, IN_COLS - n_gate:], w_in[l][:, :IN_COLS - n_gate]], axis=1)
        z2 = _inproj_call(x2, norm1_g[l], mod3, w_perm.astype(BF16), t, scale_blk=1, shift_blk=0)
        z3 = z2.reshape(b, t, IN_COLS)
        o_hg = _hgrn2_call(z3, lower_bounds[l], hg_norm_w[l])
        o_ret = _ret_call(z3, cos2, sin2)
        o_rw = _rwkv_call(z3, rw_mu[l], rw_w0[l], rw_w2[l], rw_a0[l], rw_a2[l], rw_g2[l], rw_k_k[l],
                          rw_k_a[l], rw_r_k[l], rw_ln_w[l], rw_ln_b[l])
        x2 = _merge_call(o_hg.reshape(n, HG_W), o_ret.reshape(n, RET_W), o_rw.reshape(n, RW_W), z2, x2, mod3,
                         br_hg[l].astype(BF16), br_ret[l].astype(BF16), br_rw[l].astype(BF16),
                         w_out[l].astype(BF16), t, gate_blk=2)
        hp, eid, wts, counts = _route_call(x2, norm2_g[l], mod3, router_g[l], router_e[l], t, scale_blk=4,
                                           shift_blk=3)
        pos0, pos1, tile_expert, tile_valid, n_used = _moe_plan(eid, counts)
        xs = _sc_scatter2(hp, pos0, pos1, (2 * n // MOE_TM + N_EXPERTS) * MOE_TM)
        ys = _gexperts_call(xs, tile_expert + l * N_EXPERTS, tile_valid, n_used,
                            moe_w1.reshape((-1,) + moe_w1.shape[2:]), moe_w3.reshape((-1,) + moe_w3.shape[2:]),
                            moe_w2.reshape((-1,) + moe_w2.shape[2:]))
        yg = _sc_gather(ys, jnp.concatenate([pos0, pos1]))
        x2 = _combine_call(yg, wts, x2, mod3, final_g, t, gate_blk=5, final_norm=(l == depth - 1))
    return x2.reshape(b, t, d)
```

```python
import functools

import jax
import jax.numpy as jnp
from jax import lax
from jax.experimental import pallas as pl
from jax.experimental.pallas import tpu as pltpu
from jax.experimental.pallas import tpu_sc as plsc

F32 = jnp.float32
BF16 = jnp.bfloat16
HIGHEST = lax.Precision.HIGHEST

HG_HEADS = 4
HG_DK = 128
HG_W = HG_HEADS * HG_DK
RET_HEADS = 4
RET_DK = 128
RET_W = RET_HEADS * RET_DK
RW_HEADS = 8
RW_N = 64
RW_W = RW_HEADS * RW_N
RW_DECAY_LORA = 64
RW_A_LORA = 64
RW_GATE_LORA = 128
RW_COLS = 3 * RW_W + RW_DECAY_LORA + RW_A_LORA + RW_GATE_LORA
RW_GN_EPS = 64e-5
N_GROUPS = 4
EXPERTS_PER_GROUP = 8
N_EXPERTS = N_GROUPS * EXPERTS_PER_GROUP
ROPE_THETA = 10000.0
NORM_EPS = 1e-6

LANES = 128
LOG2E = 1.4426950408889634
VMEM_LIMIT = 56 * 1024 * 1024

GATE_OFF = 0
HG_OFF = 3 * 1024
RET_OFF = HG_OFF + 4 * HG_W
RW_OFF = RET_OFF + 4 * RET_W
IN_COLS = RW_OFF + RW_COLS

HG_CHUNK = 64
HG_SUB = 16
HG_SAFE_SPAN = 60.0
RW_CHUNK = 64
RW_BLK = 16
RW_TB = 256
Z_DTYPE = BF16


def _cparams(sem):
    return pltpu.CompilerParams(dimension_semantics=sem, vmem_limit_bytes=VMEM_LIMIT)


def _dot(a, b, precision=None):
    return jnp.dot(a, b, preferred_element_type=F32, precision=precision)


def _dot_nt(a, b, precision=None):
    return lax.dot_general(a, b, (((1,), (1,)), ((), ())), preferred_element_type=F32, precision=precision)


def _dot_tn(a, b, precision=None):
    return lax.dot_general(a, b, (((0,), (0,)), ((), ())), preferred_element_type=F32, precision=precision)


def _split_bf16(x):
    hi = x.astype(BF16)
    return hi, (x - hi.astype(F32)).astype(BF16)


def _dot_x3(a, b):
    ah, al = _split_bf16(a)
    bh, bl = _split_bf16(b)
    return _dot(ah, bh) + _dot(ah, bl) + _dot(al, bh)


def _dot_x2_lhs(a, b_exact):
    ah, al = _split_bf16(a)
    return _dot(ah, b_exact) + _dot(al, b_exact)


def _dot_x2_rhs(a_exact, b):
    bh, bl = _split_bf16(b)
    return _dot(a_exact, bh) + _dot(a_exact, bl)


def _bdot(a, b):
    return _dot(a.astype(BF16), b.astype(BF16))


def _sigmoid(x):
    return 0.5 * jnp.tanh(0.5 * x) + 0.5


def _silu(x):
    return x * _sigmoid(x)


def _rms_mod(x, gain, scale, shift):
    y = x * lax.rsqrt(jnp.mean(x * x, axis=-1, keepdims=True) + NORM_EPS)
    return (y * gain) * (1.0 + scale) + shift


def _mod_kernel(c_ref, w_ref, b_ref, o_ref):
    c = c_ref[...]
    o_ref[0] = _dot(_silu(c), w_ref[0], HIGHEST) + b_ref[0]


def _mod_call(c, ada_w, ada_b):
    depth, d, d6 = ada_w.shape
    b = c.shape[0]
    nblk = d6 // d
    return pl.pallas_call(
        _mod_kernel,
        grid=(depth, nblk),
        in_specs=[
            pl.BlockSpec((b, d), lambda l, j: (0, 0)),
            pl.BlockSpec((1, d, d), lambda l, j: (l, 0, j)),
            pl.BlockSpec((1, 1, d), lambda l, j: (l, 0, j)),
        ],
        out_specs=pl.BlockSpec((1, b, d), lambda l, j: (l, 0, j)),
        out_shape=jax.ShapeDtypeStruct((depth, b, d6), F32),
        compiler_params=_cparams(("parallel", "parallel")),
        name="adaln_mod",
    )(c, ada_w, ada_b.reshape(depth, 1, d6))


def _rope_kernel(pos_ref, freq_ref, sign_ref, cos_ref, sin_ref):
    ang = pos_ref[0].astype(F32) * freq_ref[...]
    cos_ref[0] = jnp.cos(ang)
    sin_ref[0] = jnp.sin(ang) * sign_ref[...]


def _rope_call(positions, d):
    b, t = positions.shape
    tb = min(t, 512)
    inv_freq = ROPE_THETA ** (-jnp.arange(0, d, 2, dtype=F32) / d)
    freq2 = jnp.concatenate([inv_freq, inv_freq]).reshape(1, d)
    sign2 = jnp.concatenate([-jnp.ones((d // 2,), F32), jnp.ones((d // 2,), F32)]).reshape(1, d)
    out = jax.ShapeDtypeStruct((b, t, d), F32)
    return pl.pallas_call(
        _rope_kernel,
        grid=(b, t // tb),
        in_specs=[
            pl.BlockSpec((1, tb, 1), lambda i, j: (i, j, 0)),
            pl.BlockSpec((1, d), lambda i, j: (0, 0)),
            pl.BlockSpec((1, d), lambda i, j: (0, 0)),
        ],
        out_specs=[pl.BlockSpec((1, tb, d), lambda i, j: (i, j, 0))] * 2,
        out_shape=[out, out],
        compiler_params=_cparams(("parallel", "parallel")),
        name="rope_tables",
    )(positions.reshape(b, t, 1), freq2, sign2)


def _inproj_kernel(x_ref, g_ref, scale_ref, shift_ref, w_ref, o_ref, h_ref):
    @pl.when(pl.program_id(1) == 0)
    def _():
        h = _rms_mod(x_ref[...], g_ref[...], scale_ref[0], shift_ref[0])
        h_ref[...] = h.astype(BF16)

    o_ref[...] = _dot(h_ref[...], w_ref[...]).astype(o_ref.dtype)


def _inproj_call(x2, gain, mod3, w_bf16, seq, scale_blk, shift_blk, tm=2048, tn=1792):
    n, d = x2.shape
    cols = w_bf16.shape[1]
    tpb = seq // tm
    return pl.pallas_call(
        _inproj_kernel,
        grid=(n // tm, cols // tn),
        in_specs=[
            pl.BlockSpec((tm, d), lambda i, j: (i, 0)),
            pl.BlockSpec((1, d), lambda i, j: (0, 0)),
            pl.BlockSpec((1, 1, d), lambda i, j: (i // tpb, 0, scale_blk)),
            pl.BlockSpec((1, 1, d), lambda i, j: (i // tpb, 0, shift_blk)),
            pl.BlockSpec((d, tn), lambda i, j: (0, j)),
        ],
        out_specs=pl.BlockSpec((tm, tn), lambda i, j: (i, j)),
        out_shape=jax.ShapeDtypeStruct((n, cols), Z_DTYPE),
        scratch_shapes=[pltpu.VMEM((tm, d), BF16)],
        compiler_params=_cparams(("parallel", "arbitrary")),
        name="norm_inproj",
    )(x2, gain.reshape(1, d), mod3, mod3, w_bf16)


def _hgrn2_block(zs, lbs, nw, sts, factored):
    hs = range(len(zs))
    tb = zs[0][0].shape[0]
    c, sub = HG_CHUNK, HG_SUB
    nc, ns, nb = tb // c, c // sub, tb // sub
    f = [lbs[h] + (1.0 - lbs[h]) * _sigmoid(zs[h][1]) for h in hs]
    logf = [jnp.log(jnp.maximum(f[h], 1e-30)) for h in hs]
    q = [_silu(zs[h][0]) * (HG_DK ** -0.5) for h in hs]
    k = [1.0 - f[h] for h in hs]
    v = [zs[h][2] for h in hs]
    v_b = [v[h].astype(BF16) for h in hs]
    row = lax.broadcasted_iota(jnp.int32, (tb, tb), 0)
    col = lax.broadcasted_iota(jnp.int32, (tb, tb), 1)
    tri = jnp.where(col >= (row // c) * c, jnp.where(row >= col, 1.0, 0.0), 0.0).astype(BF16)
    cum = [_dot_x2_rhs(tri, logf[h]) for h in hs]
    cum3 = [cum[h].reshape(nb, sub, HG_DK) for h in hs]
    ref3 = [cum3[h][:, 0:1, :] - logf[h].reshape(nb, sub, HG_DK)[:, 0:1, :] for h in hs]
    span = functools.reduce(jnp.maximum, [jnp.max(ref3[h] - cum3[h][:, sub - 1:sub, :]) for h in hs])
    qe = [(q[h] * jnp.exp(cum[h])).astype(BF16) for h in hs]

    offd = [(h, ci * c, ci * c + sub * i) for h in hs for ci in range(nc) for i in range(1, ns)]
    base = [cum[h][lo - 1:lo] for h, _, lo in offd]
    qt = [(q[h][lo:lo + sub] * jnp.exp(cum[h][lo:lo + sub] - base[j])).astype(BF16)
          for j, (h, _, lo) in enumerate(offd)]
    kt = [(k[h][r0:lo] * jnp.exp(base[j] - cum[h][r0:lo])).astype(BF16) for j, (h, r0, lo) in enumerate(offd)]
    a = [_dot_nt(qt[j], kt[j]).astype(BF16) for j in range(len(offd))]
    av = {(h, lo): _dot(a[j], v_b[h][r0:lo]) for j, (h, r0, lo) in enumerate(offd)}

    cs = [slice(ci * c, (ci + 1) * c) for ci in range(nc)]
    hc = [(h, ci) for h in hs for ci in range(nc)]
    last = {(h, ci): cum[h][(ci + 1) * c - 1:(ci + 1) * c] for h, ci in hc}
    kd = {(h, ci): (k[h][cs[ci]] * jnp.exp(last[h, ci] - cum[h][cs[ci]])).astype(BF16) for h, ci in hc}
    inc = {(h, ci): _dot_tn(v_b[h][cs[ci]], kd[h, ci]) for h, ci in hc}
    s_in = {(h, 0): sts[h] for h in hs}
    for ci in range(nc):
        for h in hs:
            s_in[h, ci + 1] = s_in[h, ci] * jnp.exp(last[h, ci]) + inc[h, ci]
    o_inter = {(h, ci): _dot_nt(qe[h][cs[ci]], s_in[h, ci].astype(BF16)) for h, ci in hc}

    if factored:
        qf = [(q[h] * jnp.exp(cum3[h] - ref3[h]).reshape(tb, HG_DK)).astype(BF16) for h in hs]
        kf = [(k[h] * jnp.exp(ref3[h] - cum3[h]).reshape(tb, HG_DK)).astype(BF16) for h in hs]
        rc = lax.broadcasted_iota(jnp.int32, (c, c), 0)
        cc = lax.broadcasted_iota(jnp.int32, (c, c), 1)
        keep = (rc >= cc) & (rc // sub == cc // sub)
        a_d = {(h, ci): jnp.where(keep, _dot_nt(qf[h][cs[ci]], kf[h][cs[ci]]), 0.0).astype(BF16) for h, ci in hc}
        dg = {(h, ci): _dot(a_d[h, ci], v_b[h][cs[ci]]) for h, ci in hc}
        diag = [jnp.concatenate([dg[h, ci] for ci in range(nc)], axis=0) for h in hs]
    else:
        gb = 4
        trow = lax.broadcasted_iota(jnp.int32, (gb, sub, HG_DK), 1)
        diag = []
        for h in hs:
            c2 = cum[h] * LOG2E
            ks2 = c2 - jnp.log2(k[h])
            parts = []
            for g0 in range(0, nb, gb):
                rws = slice(g0 * sub, (g0 + gb) * sub)
                c23, ks23, q3, v3 = (x[rws].reshape(gb, sub, HG_DK) for x in (c2, ks2, q[h], v[h]))
                acc = jnp.zeros((gb, sub, HG_DK), F32)
                for s in range(sub):
                    e = jnp.exp2(jnp.where(trow >= s, c23 - ks23[:, s:s + 1, :], -jnp.inf))
                    a_col = jnp.sum(q3 * e, axis=-1, keepdims=True)
                    acc = acc + a_col * v3[:, s:s + 1, :]
                parts.append(acc.reshape(gb * sub, HG_DK))
            diag.append(jnp.concatenate(parts, axis=0))

    outs = []
    for h in hs:
        pieces = []
        for ci in range(nc):
            for i in range(ns):
                lo = ci * c + sub * i
                piece = o_inter[h, ci][sub * i:sub * (i + 1)] + diag[h][lo:lo + sub]
                pieces.append(piece + av[h, lo] if i > 0 else piece)
        o = jnp.concatenate(pieces, axis=0)
        o = o * lax.rsqrt(jnp.mean(o * o, axis=-1, keepdims=True) + NORM_EPS)
        outs.append(o * nw * _silu(zs[h][3]))
    return outs, [s_in[h, nc] for h in hs], span


def _hgrn2_kernel(zq_ref, zf_ref, zi_ref, zg_ref, lb_ref, nw_ref, o_ref, st_ref):
    @pl.when(pl.program_id(1) == 0)
    def _():
        st_ref[...] = jnp.zeros_like(st_ref)

    hs = range(HG_HEADS)
    sl = [slice(h * HG_DK, (h + 1) * HG_DK) for h in hs]

    def run(factored):
        zs = [tuple(r[0, :, sl[h]].astype(F32) for r in (zq_ref, zf_ref, zi_ref, zg_ref)) for h in hs]
        outs, sts, span = _hgrn2_block(zs, [lb_ref[:, sl[h]] for h in hs], nw_ref[...],
                                       [st_ref[h] for h in hs], factored)
        return jnp.concatenate(outs, axis=1), sts, span

    st_old = [st_ref[h] for h in hs]
    o, st_new, span = run(True)
    for h in hs:
        st_ref[h] = st_new[h]
    o_ref[0] = o.astype(o_ref.dtype)

    @pl.when(span > HG_SAFE_SPAN)
    def _():
        for h in hs:
            st_ref[h] = st_old[h]
        o2, st2, _ = run(False)
        for h in hs:
            st_ref[h] = st2[h]
        o_ref[0] = o2.astype(o_ref.dtype)


def _hgrn2_call(z3, lower_bound, norm_w, tb=256):
    b, t, _ = z3.shape
    tb = min(tb, t)
    base = HG_OFF // HG_W

    def zspec(part):
        return pl.BlockSpec((1, tb, HG_W), lambda i, j: (i, j, base + part))

    return pl.pallas_call(
        _hgrn2_kernel,
        grid=(b, t // tb),
        in_specs=[
            zspec(0), zspec(1), zspec(2), zspec(3),
            pl.BlockSpec((1, HG_W), lambda i, j: (0, 0)),
            pl.BlockSpec((1, LANES), lambda i, j: (0, 0)),
        ],
        out_specs=pl.BlockSpec((1, tb, HG_W), lambda i, j: (i, j, 0)),
        out_shape=jax.ShapeDtypeStruct((b, t, HG_W), BF16),
        scratch_shapes=[pltpu.VMEM((HG_HEADS, HG_DK, HG_DK), F32)],
        compiler_params=_cparams(("parallel", "arbitrary")),
        name="hgrn2_mixer",
    )(z3, z3, z3, z3, lower_bound.reshape(1, HG_W), norm_w.reshape(1, HG_DK))


def _ret_kernel(zq_ref, zk_ref, zv_ref, zg_ref, cos_ref, sin_ref, o_ref, st_ref, dmask_ref, *, chunk):
    hs = range(RET_HEADS)
    sl = [slice(h * RET_DK, (h + 1) * RET_DK) for h in hs]
    lg = [jnp.log(jnp.full((1, 1), 1.0 - 2.0 ** (-5.0 - h), F32)) for h in hs]

    @pl.when(pl.program_id(1) == 0)
    def _():
        st_ref[...] = jnp.zeros_like(st_ref)
        row = lax.broadcasted_iota(jnp.int32, (chunk, chunk), 0)
        col = lax.broadcasted_iota(jnp.int32, (chunk, chunk), 1)
        rel = (row - col).astype(F32)
        for h in hs:
            dmask_ref[h] = jnp.where(rel >= 0.0, jnp.exp(jnp.maximum(rel, 0.0) * lg[h]), 0.0)

    cos2 = cos_ref[0]
    sin2 = sin_ref[0]
    half = RET_DK // 2

    def rope(z):
        return z * cos2 + pltpu.roll(z, half, 1) * sin2

    tcol = lax.broadcasted_iota(jnp.int32, (chunk, 1), 0).astype(F32)
    q = [rope(zq_ref[0, :, sl[h]].astype(F32)) * (RET_DK ** -0.5) for h in hs]
    k = [rope(zk_ref[0, :, sl[h]].astype(F32)) for h in hs]
    v_b = [zv_ref[0, :, sl[h]].astype(BF16) for h in hs]
    st = [st_ref[h] for h in hs]
    scores = [(_dot_nt(q[h].astype(BF16), k[h].astype(BF16)) * dmask_ref[h]).astype(BF16) for h in hs]
    qx = [(q[h] * jnp.exp((tcol + 1.0) * lg[h])).astype(BF16) for h in hs]
    kz = [(k[h] * jnp.exp((chunk - 1.0 - tcol) * lg[h])).astype(BF16) for h in hs]
    o = [_dot(scores[h], v_b[h]) + _dot_nt(qx[h], st[h].astype(BF16)) for h in hs]
    for h in hs:
        st_ref[h] = st[h] * jnp.exp(chunk * lg[h]) + _dot_tn(v_b[h], kz[h])
    o = [o[h] * lax.rsqrt(jnp.mean(o[h] * o[h], axis=-1, keepdims=True) + NORM_EPS) for h in hs]
    o_ref[0] = (jnp.concatenate(o, axis=1) * _silu(zg_ref[0].astype(F32))).astype(o_ref.dtype)


def _ret_call(z3, cos2, sin2, chunk=256):
    b, t, _ = z3.shape
    chunk = min(chunk, t)
    base = RET_OFF // RET_W

    def zspec(part):
        return pl.BlockSpec((1, chunk, RET_W), lambda i, j: (i, j, base + part))

    tab = pl.BlockSpec((1, chunk, RET_DK), lambda i, j: (i, j, 0))
    return pl.pallas_call(
        functools.partial(_ret_kernel, chunk=chunk),
        grid=(b, t // chunk),
        in_specs=[zspec(0), zspec(1), zspec(2), zspec(3), tab, tab],
        out_specs=pl.BlockSpec((1, chunk, RET_W), lambda i, j: (i, j, 0)),
        out_shape=jax.ShapeDtypeStruct((b, t, RET_W), BF16),
        scratch_shapes=[pltpu.VMEM((RET_HEADS, RET_DK, RET_DK), F32), pltpu.VMEM((RET_HEADS, chunk, chunk), F32)],
        compiler_params=_cparams(("parallel", "arbitrary")),
        name="retention_mixer",
    )(z3, z3, z3, z3, cos2, sin2)


def _inv_unit_lower(a, eye, blk_mask):
    c = a[0].shape[0]
    m = range(len(a))
    a_bd = [jnp.where(blk_mask, a[i], 0.0) for i in m]
    a_off = [a[i] - a_bd[i] for i in m]
    a2 = [_bdot(a_bd[i], a_bd[i]) for i in m]
    p = [eye + a_bd[i] for i in m]
    r = [_bdot(jnp.concatenate([p[i], a2[i]], axis=0), a2[i]) for i in m]
    p = [p[i] + r[i][:c] for i in m]
    a4 = [r[i][c:] for i in m]
    r = [_bdot(jnp.concatenate([p[i], a4[i]], axis=0), a4[i]) for i in m]
    p = [p[i] + r[i][:c] for i in m]
    a8 = [r[i][c:] for i in m]
    t_bd = [p[i] + _bdot(p[i], a8[i]) for i in m]
    n = [_bdot(t_bd[i], a_off[i]) for i in m]
    r = [_bdot(n[i], jnp.concatenate([n[i], t_bd[i]], axis=1)) for i in m]
    z = [t_bd[i] + r[i][:, c:] for i in m]
    return [z[i] + _bdot(r[i][:, :c], z[i]) for i in m]


def _rwkv_kernel(z_ref, mu_ref, w0_ref, w2_ref, a0_ref, a2_ref, g2_ref, kk_ref, ka_ref, rk_ref,
                 lnw_ref, lnb_ref, seg_ref, o_ref, s_ref, prev_ref):
    c = RW_CHUNK
    tb = z_ref.shape[1]
    nck = tb // c

    @pl.when(pl.program_id(1) == 0)
    def _():
        s_ref[...] = jnp.zeros_like(s_ref)
        prev_ref[...] = jnp.zeros_like(prev_ref)

    z = z_ref[0].astype(F32)
    rows = lax.broadcasted_iota(jnp.int32, (tb, 1), 0)
    z_prev = jnp.where(rows == 0, prev_ref[...], pltpu.roll(z, 1, 0))
    prev_ref[...] = z[tb - 1:tb]
    zs = z + mu_ref[...] * (z_prev - z)
    r = zs[:, 0:RW_W]
    k = zs[:, RW_W:2 * RW_W]
    v = zs[:, 2 * RW_W:3 * RW_W]
    off = 3 * RW_W
    w_lo = zs[:, off:off + RW_DECAY_LORA]
    a_lo = zs[:, off + RW_DECAY_LORA:off + RW_DECAY_LORA + RW_A_LORA]
    g_lo = zs[:, off + RW_DECAY_LORA + RW_A_LORA:]

    wx = -(w0_ref[...] + _dot_x3(jnp.tanh(w_lo), w2_ref[...]))
    softplus = jnp.maximum(wx, 0.0) + jnp.log(1.0 + jnp.exp(-jnp.abs(wx)))
    logw = -jnp.exp(-softplus - 0.5)
    a = _sigmoid(a0_ref[...] + _dot_x3(a_lo, a2_ref[...]))
    g = _dot_x3(_sigmoid(g_lo), g2_ref[...])
    seg = seg_ref[...]
    kk = k * kk_ref[...]
    kk = kk * lax.rsqrt(jnp.maximum(_dot_x2_lhs(kk * kk, seg), 1e-24))
    k2 = k * (1.0 + (a - 1.0) * ka_ref[...])

    row = lax.broadcasted_iota(jnp.int32, (c, c), 0)
    col = lax.broadcasted_iota(jnp.int32, (c, c), 1)
    blk_mask = (row // RW_BLK) == (col // RW_BLK)
    eye = (row == col).astype(F32)
    row2 = lax.broadcasted_iota(jnp.int32, (c, 2 * c), 0)
    col2 = lax.broadcasted_iota(jnp.int32, (c, 2 * c), 1) % c
    incl2 = row2 >= col2
    strict2 = row2 > col2
    rowb = lax.broadcasted_iota(jnp.int32, (tb, tb), 0)
    colb = lax.broadcasted_iota(jnp.int32, (tb, tb), 1)
    tri = jnp.where(colb >= (rowb // c) * c, jnp.where(rowb >= colb, 1.0, 0.0), 0.0).astype(BF16)
    cw = _dot_x2_rhs(tri, logw)
    w_inv = jnp.exp(-cw)
    last = jnp.concatenate([jnp.broadcast_to(cw[(ci + 1) * c - 1:(ci + 1) * c], (c, RW_W)) for ci in range(nck)],
                           axis=0)
    w_rest = jnp.exp(last - cw)
    beta = a * kk
    alpha_t = -kk * jnp.exp(cw - logw)
    r_t = r * jnp.exp(cw)
    beta_h = beta * w_inv
    k_h = k2 * w_inv
    beta_d = beta * w_rest
    k_d = k2 * w_rest

    hs = range(RW_HEADS)
    ph = [(ci, h) for ci in range(nck) for h in hs]
    m = range(len(ph))
    rs = [slice(ci * c, (ci + 1) * c) for ci, _ in ph]
    sl = [slice(h * RW_N, (h + 1) * RW_N) for _, h in ph]
    v_h = [v[rs[i], sl[i]] for i in m]
    lhs = [jnp.concatenate([alpha_t[rs[i], sl[i]], r_t[rs[i], sl[i]]], axis=0).astype(BF16) for i in m]
    rhs = [jnp.concatenate([beta_h[rs[i], sl[i]], k_h[rs[i], sl[i]]], axis=0).astype(BF16) for i in m]
    big = [_dot_nt(lhs[i], rhs[i]) for i in m]
    a_a = [jnp.where(strict2, big[i][:c], 0.0) for i in m]
    a_r = [jnp.where(incl2, big[i][c:], 0.0).astype(BF16) for i in m]
    t_inv = _inv_unit_lower([a_a[i][:, :c] for i in m], eye, blk_mask)
    av = [_bdot(a_a[i][:, c:], v_h[i]) for i in m]
    bk_d = [jnp.concatenate([beta_d[rs[i], sl[i]], k_d[rs[i], sl[i]]], axis=0).astype(BF16) for i in m]
    s_cur = [s_ref[h] for h in hs]
    o_chunks = []
    for ci in range(nck):
        ix = [ci * RW_HEADS + h for h in hs]
        sd = [_dot_nt(lhs[ix[h]], s_cur[h].astype(BF16)) for h in hs]
        u = [_bdot(t_inv[ix[h]], sd[h][:c] + av[ix[h]]) for h in hs]
        uv = [jnp.concatenate([u[h], v_h[ix[h]]], axis=0).astype(BF16) for h in hs]
        o_chunks.append(jnp.concatenate([sd[h][c:] + _dot(a_r[ix[h]], uv[h]) for h in hs], axis=1))
        w_last = jnp.exp(cw[(ci + 1) * c - 1:(ci + 1) * c])
        s_cur = [s_cur[h] * w_last[:, sl[h]] + _dot_tn(uv[h], bk_d[ix[h]]) for h in hs]
    for h in hs:
        s_ref[h] = s_cur[h]
    o = jnp.concatenate(o_chunks, axis=0)

    mean = _dot_x2_lhs(o, seg) * (1.0 / RW_N)
    dev = o - mean
    var = _dot_x2_lhs(dev * dev, seg) * (1.0 / RW_N)
    o = dev * lax.rsqrt(var + RW_GN_EPS) * lnw_ref[...] + lnb_ref[...]
    bonus = _dot_x2_lhs(r * k2 * rk_ref[...], seg) * v
    o_ref[0] = ((o + bonus) * g).astype(o_ref.dtype)


def _rwkv_call(z3, mu, w0, w2, a0, a2, g2, k_k, k_a, r_k, ln_w, ln_b):
    b, t, _ = z3.shape
    c = min(RW_TB, t)
    hid = lax.broadcasted_iota(jnp.int32, (RW_W, RW_W), 0) // RW_N
    seg = (hid == hid.T).astype(BF16)

    def vec(n):
        return pl.BlockSpec((1, n), lambda i, j: (0, 0))

    def mat(m, n):
        return pl.BlockSpec((m, n), lambda i, j: (0, 0))

    return pl.pallas_call(
        _rwkv_kernel,
        grid=(b, t // c),
        in_specs=[
            pl.BlockSpec((1, c, RW_COLS), lambda i, j: (i, j, RW_OFF // RW_COLS)),
            vec(RW_COLS), vec(RW_W), mat(RW_DECAY_LORA, RW_W), vec(RW_W), mat(RW_A_LORA, RW_W),
            mat(RW_GATE_LORA, RW_W), vec(RW_W), vec(RW_W), vec(RW_W), vec(RW_W), vec(RW_W),
            mat(RW_W, RW_W),
        ],
        out_specs=pl.BlockSpec((1, c, RW_W), lambda i, j: (i, j, 0)),
        out_shape=jax.ShapeDtypeStruct((b, t, RW_W), BF16),
        scratch_shapes=[pltpu.VMEM((RW_HEADS, RW_N, RW_N), F32), pltpu.VMEM((1, RW_COLS), F32)],
        compiler_params=_cparams(("parallel", "arbitrary")),
        name="rwkv7_mixer",
    )(z3, mu.reshape(1, -1), w0.reshape(1, -1), w2, a0.reshape(1, -1), a2, g2, k_k.reshape(1, -1),
      k_a.reshape(1, -1), r_k.reshape(1, -1), ln_w.reshape(1, -1), ln_b.reshape(1, -1), seg)


def _merge_kernel(ohg_ref, oret_ref, orw_ref, zg_ref, x_ref, gate_ref, bhg_ref, bret_ref, brw_ref,
                  wout_ref, o_ref):
    d = x_ref.shape[1]
    y = _sigmoid(zg_ref[:, 0:d].astype(F32)) * _dot(ohg_ref[...], bhg_ref[...])
    y = y + _sigmoid(zg_ref[:, d:2 * d].astype(F32)) * _dot(oret_ref[...], bret_ref[...])
    y = y + _sigmoid(zg_ref[:, 2 * d:3 * d].astype(F32)) * _dot(orw_ref[...], brw_ref[...])
    o_ref[...] = x_ref[...] + gate_ref[0] * _dot(y.astype(BF16), wout_ref[...])


def _merge_call(o_hg, o_ret, o_rw, z2, x2, mod3, br_hg, br_ret, br_rw, w_out, seq, gate_blk, tm=512):
    n, d = x2.shape
    tpb = seq // tm

    def rows(w):
        return pl.BlockSpec((tm, w), lambda i: (i, 0))

    def full(m, k):
        return pl.BlockSpec((m, k), lambda i: (0, 0))

    return pl.pallas_call(
        _merge_kernel,
        grid=(n // tm,),
        in_specs=[
            rows(HG_W), rows(RET_W), rows(RW_W), rows(3 * d), rows(d),
            pl.BlockSpec((1, 1, d), lambda i: (i // tpb, 0, gate_blk)),
            full(HG_W, d), full(RET_W, d), full(RW_W, d), full(d, d),
        ],
        out_specs=rows(d),
        out_shape=jax.ShapeDtypeStruct((n, d), F32),
        compiler_params=_cparams(("parallel",)),
        name="merge_outproj",
    )(o_hg, o_ret, o_rw, z2, x2, mod3, br_hg, br_ret, br_rw, w_out)


def _pack_bf16_pairs(x):
    w = x.shape[1] // 2
    hi = pltpu.bitcast(x[:, :w].astype(BF16).astype(F32), jnp.uint32)
    lo = pltpu.bitcast(x[:, w:].astype(BF16).astype(F32), jnp.uint32)
    return pltpu.bitcast(hi | lax.shift_right_logical(lo, jnp.uint32(16)), jnp.int32)


def _unpack_bf16_pairs(p):
    u = pltpu.bitcast(p, jnp.uint32)
    hi = pltpu.bitcast(u & jnp.uint32(0xFFFF0000), F32)
    lo = pltpu.bitcast(lax.shift_left(u, jnp.uint32(16)), F32)
    return jnp.concatenate([hi, lo], axis=1)


def _route_kernel(x_ref, g_ref, scale_ref, shift_ref, rc_ref, hp_ref, eid_ref, wts_ref, cnt_ref):
    @pl.when(pl.program_id(0) == 0)
    def _():
        cnt_ref[...] = jnp.zeros_like(cnt_ref)

    h = _rms_mod(x_ref[...], g_ref[...], scale_ref[0], shift_ref[0])
    hp_ref[...] = _pack_bf16_pairs(h)
    tm = h.shape[0]
    lane = lax.broadcasted_iota(jnp.int32, (tm, LANES), 1)
    neg = -jnp.inf
    logits = _dot_x3(h, rc_ref[...])
    gl = jnp.where(lane < N_GROUPS, logits, neg)
    gmax = jnp.max(gl, axis=-1, keepdims=True)
    gidx = jnp.min(jnp.where(gl == gmax, lane, LANES), axis=-1, keepdims=True)
    gw = 1.0 / jnp.sum(jnp.exp(gl - gmax), axis=-1, keepdims=True)
    lo = N_GROUPS + gidx * EXPERTS_PER_GROUP
    el = jnp.where(lane >= lo, jnp.where(lane < lo + EXPERTS_PER_GROUP, logits, neg), neg)
    m1 = jnp.max(el, axis=-1, keepdims=True)
    l1 = jnp.min(jnp.where(el == m1, lane, LANES), axis=-1, keepdims=True)
    el2 = jnp.where(lane == l1, neg, el)
    m2 = jnp.max(el2, axis=-1, keepdims=True)
    l2 = jnp.min(jnp.where(el2 == m2, lane, LANES), axis=-1, keepdims=True)
    i1 = l1 - N_GROUPS
    i2 = l2 - N_GROUPS
    e2 = jnp.exp(m2 - m1)
    p1 = 1.0 / (1.0 + e2)
    p2 = e2 * p1
    oh1 = jnp.where(lane == i1, 1.0, 0.0)
    oh2 = jnp.where(lane == i2, 1.0, 0.0)
    row = lax.broadcasted_iota(jnp.int32, (tm, tm), 0)
    col = lax.broadcasted_iota(jnp.int32, (tm, tm), 1)
    earlier = jnp.where(row > col, 1.0, 0.0).astype(BF16)
    before = _dot(earlier, jnp.concatenate([oh1, oh2], axis=1).astype(BF16))
    tot1 = jnp.sum(oh1, axis=0, keepdims=True)
    carry = cnt_ref[...]
    r1 = jnp.sum(oh1 * (before[:, :LANES] + carry), axis=-1, keepdims=True).astype(jnp.int32)
    r2 = jnp.sum(oh2 * (before[:, LANES:] + (carry + tot1)), axis=-1, keepdims=True).astype(jnp.int32)
    cnt_ref[...] = carry + tot1 + jnp.sum(oh2, axis=0, keepdims=True)
    eid_ref[...] = jnp.where(lane == 0, i1, jnp.where(lane == 1, i2, jnp.where(lane == 2, r1,
                                                                             jnp.where(lane == 3, r2, 0))))
    wts_ref[...] = jnp.where(lane == 0, gw * p1, jnp.where(lane == 1, gw * p2, 0.0))


def _route_call(x2, gain, mod3, router_g, router_e, seq, scale_blk, shift_blk, tm=512):
    n, d = x2.shape
    tpb = seq // tm
    rc = jnp.pad(jnp.concatenate([router_g, router_e], axis=1), ((0, 0), (0, LANES - N_GROUPS - N_EXPERTS)))
    return pl.pallas_call(
        _route_kernel,
        grid=(n // tm,),
        in_specs=[
            pl.BlockSpec((tm, d), lambda i: (i, 0)),
            pl.BlockSpec((1, d), lambda i: (0, 0)),
            pl.BlockSpec((1, 1, d), lambda i: (i // tpb, 0, scale_blk)),
            pl.BlockSpec((1, 1, d), lambda i: (i // tpb, 0, shift_blk)),
            pl.BlockSpec((d, LANES), lambda i: (0, 0)),
        ],
        out_specs=[pl.BlockSpec((tm, d // 2), lambda i: (i, 0)), pl.BlockSpec((tm, LANES), lambda i: (i, 0)),
                   pl.BlockSpec((tm, LANES), lambda i: (i, 0)), pl.BlockSpec((1, LANES), lambda i: (0, 0))],
        out_shape=[jax.ShapeDtypeStruct((n, d // 2), jnp.int32), jax.ShapeDtypeStruct((n, LANES), jnp.int32),
                   jax.ShapeDtypeStruct((n, LANES), F32), jax.ShapeDtypeStruct((1, LANES), F32)],
        compiler_params=_cparams(("arbitrary",)),
        name="moe_route",
    )(x2, gain.reshape(1, d), mod3, mod3, rc)


SC_CORES = 2
SC_SUBCORES = 16
SC_WORKERS = SC_CORES * SC_SUBCORES
SC_ROWS = 64


def _sc_gather(table, idx):
    m = idx.shape[0]
    w = table.shape[1]
    per_worker = m // SC_WORKERS
    steps = per_worker // SC_ROWS
    assert per_worker * SC_WORKERS == m and steps * SC_ROWS == per_worker and steps % 2 == 0
    mesh = plsc.VectorSubcoreMesh(core_axis_name="c", subcore_axis_name="s")

    def body(table_hbm, idx_hbm, out_hbm, idx_v, rows_a, rows_b, sem_ga, sem_gb, sem_wa, sem_wb):
        wid = lax.axis_index("s") * SC_CORES + lax.axis_index("c")
        pltpu.sync_copy(idx_hbm.at[wid], idx_v)

        @pl.loop(0, steps, step=2)
        def _(j):
            row0 = wid * per_worker + j * SC_ROWS
            ga = pltpu.async_copy(table_hbm.at[idx_v.at[j]], rows_a, sem_ga)
            gb = pltpu.async_copy(table_hbm.at[idx_v.at[j + 1]], rows_b, sem_gb)
            ga.wait()
            wa = pltpu.async_copy(rows_a, out_hbm.at[pl.ds(row0, SC_ROWS)], sem_wa)
            gb.wait()
            wb = pltpu.async_copy(rows_b, out_hbm.at[pl.ds(row0 + SC_ROWS, SC_ROWS)], sem_wb)
            wa.wait()
            wb.wait()

    return pl.kernel(
        body,
        out_type=jax.ShapeDtypeStruct((m, w), table.dtype),
        mesh=mesh,
        scratch_types=[pltpu.VMEM((steps, SC_ROWS), jnp.int32), pltpu.VMEM((SC_ROWS, w), table.dtype),
                       pltpu.VMEM((SC_ROWS, w), table.dtype), pltpu.SemaphoreType.DMA, pltpu.SemaphoreType.DMA,
                       pltpu.SemaphoreType.DMA, pltpu.SemaphoreType.DMA],
        name="sc_row_gather",
    )(table, idx.reshape(SC_WORKERS, steps, SC_ROWS))


def _sc_scatter2(rows, idx0, idx1, p):
    n, w = rows.shape
    per_worker = n // SC_WORKERS
    steps = per_worker // SC_ROWS
    assert per_worker * SC_WORKERS == n and steps * SC_ROWS == per_worker and steps % 2 == 0
    mesh = plsc.VectorSubcoreMesh(core_axis_name="c", subcore_axis_name="s")

    def body(rows_hbm, i0_hbm, i1_hbm, out_hbm, i0_v, i1_v, buf_a, buf_b, s_ra, s_rb, s_a0, s_a1, s_b0, s_b1):
        wid = lax.axis_index("s") * SC_CORES + lax.axis_index("c")
        pltpu.sync_copy(i0_hbm.at[wid], i0_v)
        pltpu.sync_copy(i1_hbm.at[wid], i1_v)

        @pl.loop(0, steps, step=2)
        def _(j):
            row0 = wid * per_worker + j * SC_ROWS
            ra = pltpu.async_copy(rows_hbm.at[pl.ds(row0, SC_ROWS)], buf_a, s_ra)
            rb = pltpu.async_copy(rows_hbm.at[pl.ds(row0 + SC_ROWS, SC_ROWS)], buf_b, s_rb)
            ra.wait()
            a0 = pltpu.async_copy(buf_a, out_hbm.at[i0_v.at[j]], s_a0)
            a1 = pltpu.async_copy(buf_a, out_hbm.at[i1_v.at[j]], s_a1)
            rb.wait()
            b0 = pltpu.async_copy(buf_b, out_hbm.at[i0_v.at[j + 1]], s_b0)
            b1 = pltpu.async_copy(buf_b, out_hbm.at[i1_v.at[j + 1]], s_b1)
            a0.wait()
            a1.wait()
            b0.wait()
            b1.wait()

    return pl.kernel(
        body,
        out_type=jax.ShapeDtypeStruct((p, w), rows.dtype),
        mesh=mesh,
        scratch_types=[pltpu.VMEM((steps, SC_ROWS), jnp.int32), pltpu.VMEM((steps, SC_ROWS), jnp.int32),
                       pltpu.VMEM((SC_ROWS, w), rows.dtype), pltpu.VMEM((SC_ROWS, w), rows.dtype)]
        + [pltpu.SemaphoreType.DMA] * 6,
        name="sc_row_scatter",
    )(rows, idx0.reshape(SC_WORKERS, steps, SC_ROWS), idx1.reshape(SC_WORKERS, steps, SC_ROWS))


MOE_TM = 512


def _gexperts_kernel(te_ref, tv_ref, nu_ref, xs_ref, w1_ref, w3_ref, w2_ref, ys_ref, w1b_ref, w3b_ref, w2b_ref):
    i = pl.program_id(0)

    @pl.when((i == 0) | (te_ref[i] != te_ref[jnp.maximum(i - 1, 0)]))
    def _():
        w1b_ref[...] = w1_ref[0].astype(BF16)
        w3b_ref[...] = w3_ref[0].astype(BF16)
        w2b_ref[...] = w2_ref[0].astype(BF16)

    @pl.when(i < nu_ref[0])
    def _():
        rid = lax.broadcasted_iota(jnp.int32, xs_ref.shape, 0)
        xb = _unpack_bf16_pairs(jnp.where(rid < tv_ref[i], xs_ref[...], 0)).astype(BF16)
        act = (_silu(_dot(xb, w1b_ref[...])) * _dot(xb, w3b_ref[...])).astype(BF16)
        ys_ref[...] = _pack_bf16_pairs(_dot(act, w2b_ref[...]))


def _gexperts_call(xs, tile_expert, tile_valid, n_used, w1, w3, w2):
    p, half = xs.shape
    ne, d, de = w1.shape
    nt = p // MOE_TM

    def rows(i, te, tv, nu):
        return (jnp.minimum(i, nu[0] - 1), 0)

    def wsel(i, te, tv, nu):
        return (te[i], 0, 0)

    return pl.pallas_call(
        _gexperts_kernel,
        grid_spec=pltpu.PrefetchScalarGridSpec(
            num_scalar_prefetch=3,
            grid=(nt,),
            in_specs=[
                pl.BlockSpec((MOE_TM, half), rows),
                pl.BlockSpec((1, d, de), wsel),
                pl.BlockSpec((1, d, de), wsel),
                pl.BlockSpec((1, de, d), wsel),
            ],
            out_specs=pl.BlockSpec((MOE_TM, half), rows),
            scratch_shapes=[pltpu.VMEM((d, de), BF16), pltpu.VMEM((d, de), BF16), pltpu.VMEM((de, d), BF16)],
        ),
        out_shape=jax.ShapeDtypeStruct((p, half), jnp.int32),
        compiler_params=_cparams(("arbitrary",)),
        name="moe_experts",
    )(tile_expert, tile_valid, n_used, xs, w1, w3, w2)


def _combine_kernel(y0_ref, y1_ref, wts_ref, x_ref, gate_ref, fg_ref, o_ref, *, final_norm):
    wts = wts_ref[...]
    moe = wts[:, 0:1] * _unpack_bf16_pairs(y0_ref[...]) + wts[:, 1:2] * _unpack_bf16_pairs(y1_ref[...])
    xn = x_ref[...] + gate_ref[0] * moe
    if final_norm:
        xn = xn * lax.rsqrt(jnp.mean(xn * xn, axis=-1, keepdims=True) + NORM_EPS) * fg_ref[...]
    o_ref[...] = xn


def _combine_call(yg, wts, x2, mod3, final_g, seq, gate_blk, final_norm, tm=512):
    n, d = x2.shape
    tpb = seq // tm
    slot1 = n // tm
    return pl.pallas_call(
        functools.partial(_combine_kernel, final_norm=final_norm),
        grid=(n // tm,),
        in_specs=[
            pl.BlockSpec((tm, d // 2), lambda i: (i, 0)),
            pl.BlockSpec((tm, d // 2), lambda i: (i + slot1, 0)),
            pl.BlockSpec((tm, LANES), lambda i: (i, 0)),
            pl.BlockSpec((tm, d), lambda i: (i, 0)),
            pl.BlockSpec((1, 1, d), lambda i: (i // tpb, 0, gate_blk)),
            pl.BlockSpec((1, d), lambda i: (0, 0)),
        ],
        out_specs=pl.BlockSpec((tm, d), lambda i: (i, 0)),
        out_shape=jax.ShapeDtypeStruct((n, d), F32),
        compiler_params=_cparams(("parallel",)),
        name="moe_combine",
    )(yg, yg, wts, x2, mod3, final_g.reshape(1, d))


def _moe_plan(eid, counts_f):
    n = eid.shape[0]
    nt = (2 * n) // MOE_TM + N_EXPERTS
    counts = counts_f[0, :N_EXPERTS].astype(jnp.int32)
    tiles = (counts + MOE_TM - 1) // MOE_TM
    tile_end = jnp.cumsum(tiles)
    tile_start = tile_end - tiles
    n_used = tile_end[-1:]
    tile_iota = jnp.arange(nt, dtype=jnp.int32)
    tile_expert = jnp.sum(jnp.minimum(tile_iota, n_used - 1)[:, None] >= tile_end[None, :], axis=1, dtype=jnp.int32)
    tile_valid = jnp.clip(counts[tile_expert] - (tile_iota - tile_start[tile_expert]) * MOE_TM, 0, MOE_TM)
    experts = jnp.arange(N_EXPERTS, dtype=jnp.int32)
    row0 = jnp.sum(jnp.where(eid[:, 0:2, None] == experts[None, None, :], tile_start * MOE_TM, 0), axis=-1)
    pos = row0 + eid[:, 2:4]
    return pos[:, 0], pos[:, 1], tile_expert, tile_valid, n_used


def kernel(x, c, positions, ada_w, ada_b, norm1_g, norm2_g, w_in, hg_lb_table, hg_norm_w, rw_mu, rw_w0, rw_w2,
           rw_a0, rw_a2, rw_g2, rw_k_k, rw_k_a, rw_r_k, rw_ln_w, rw_ln_b, br_hg, br_ret, br_rw, w_out,
           router_g, router_e, moe_w1, moe_w3, moe_w2, final_g):
    b, t, d = x.shape
    depth = ada_w.shape[0]
    n = b * t
    assert w_in.shape[2] == IN_COLS and d == 1024

    lb_p = jax.nn.softmax(hg_lb_table.astype(F32), axis=0)
    lower_bounds = jnp.cumsum(lb_p, axis=0) - lb_p[0]

    mod = _mod_call(c, ada_w, ada_b)
    cos2, sin2 = _rope_call(positions, RET_DK)
    n_gate = 3 * d
    x2 = x.reshape(n, d)
    for l in range(depth):
        mod3 = mod[l].reshape(b, 1, 6 * d)
        w_perm = jnp.concatenate([w_in[l][:, IN_COLS - n_gate:], w_in[l][:, :IN_COLS - n_gate]], axis=1)
        z2 = _inproj_call(x2, norm1_g[l], mod3, w_perm.astype(BF16), t, scale_blk=1, shift_blk=0)
        z3 = z2.reshape(b, t, IN_COLS)
        o_hg = _hgrn2_call(z3, lower_bounds[l], hg_norm_w[l])
        o_ret = _ret_call(z3, cos2, sin2)
        o_rw = _rwkv_call(z3, rw_mu[l], rw_w0[l], rw_w2[l], rw_a0[l], rw_a2[l], rw_g2[l], rw_k_k[l],
                          rw_k_a[l], rw_r_k[l], rw_ln_w[l], rw_ln_b[l])
        x2 = _merge_call(o_hg.reshape(n, HG_W), o_ret.reshape(n, RET_W), o_rw.reshape(n, RW_W), z2, x2, mod3,
                         br_hg[l].astype(BF16), br_ret[l].astype(BF16), br_rw[l].astype(BF16),
                         w_out[l].astype(BF16), t, gate_blk=2)
        hp, eid, wts, counts = _route_call(x2, norm2_g[l], mod3, router_g[l], router_e[l], t, scale_blk=4,
                                           shift_blk=3)
        pos0, pos1, tile_expert, tile_valid, n_used = _moe_plan(eid, counts)
        xs = _sc_scatter2(hp, pos0, pos1, (2 * n // MOE_TM + N_EXPERTS) * MOE_TM)
        ys = _gexperts_call(xs, tile_expert + l * N_EXPERTS, tile_valid, n_used,
                            moe_w1.reshape((-1,) + moe_w1.shape[2:]), moe_w3.reshape((-1,) + moe_w3.shape[2:]),
                            moe_w2.reshape((-1,) + moe_w2.shape[2:]))
        yg = _sc_gather(ys, jnp.concatenate([pos0, pos1]))
        x2 = _combine_call(yg, wts, x2, mod3, final_g, t, gate_blk=5, final_norm=(l == depth - 1))
    return x2.reshape(b, t, d)
```

```python
import functools

import jax
import jax.numpy as jnp
from jax import lax
from jax.experimental import pallas as pl
from jax.experimental.pallas import tpu as pltpu
from jax.experimental.pallas import tpu_sc as plsc

F32 = jnp.float32
BF16 = jnp.bfloat16
HIGHEST = lax.Precision.HIGHEST

HG_HEADS = 4
HG_DK = 128
HG_W = HG_HEADS * HG_DK
RET_HEADS = 4
RET_DK = 128
RET_W = RET_HEADS * RET_DK
RW_HEADS = 8
RW_N = 64
RW_W = RW_HEADS * RW_N
RW_DECAY_LORA = 64
RW_A_LORA = 64
RW_GATE_LORA = 128
RW_COLS = 3 * RW_W + RW_DECAY_LORA + RW_A_LORA + RW_GATE_LORA
RW_GN_EPS = 64e-5
N_GROUPS = 4
EXPERTS_PER_GROUP = 8
N_EXPERTS = N_GROUPS * EXPERTS_PER_GROUP
ROPE_THETA = 10000.0
NORM_EPS = 1e-6

LANES = 128
LOG2E = 1.4426950408889634
VMEM_LIMIT = 56 * 1024 * 1024

GATE_OFF = 0
HG_OFF = 3 * 1024
RET_OFF = HG_OFF + 4 * HG_W
RW_OFF = RET_OFF + 4 * RET_W
IN_COLS = RW_OFF + RW_COLS

HG_CHUNK = 64
HG_SUB = 16
HG_SAFE_SPAN = 60.0
RW_CHUNK = 64
RW_BLK = 16
RW_TB = 256
Z_DTYPE = BF16


def _cparams(sem):
    return pltpu.CompilerParams(dimension_semantics=sem, vmem_limit_bytes=VMEM_LIMIT)


def _dot(a, b, precision=None):
    return jnp.dot(a, b, preferred_element_type=F32, precision=precision)


def _dot_nt(a, b, precision=None):
    return lax.dot_general(a, b, (((1,), (1,)), ((), ())), preferred_element_type=F32, precision=precision)


def _dot_tn(a, b, precision=None):
    return lax.dot_general(a, b, (((0,), (0,)), ((), ())), preferred_element_type=F32, precision=precision)


def _split_bf16(x):
    hi = x.astype(BF16)
    return hi, (x - hi.astype(F32)).astype(BF16)


def _dot_x3(a, b):
    ah, al = _split_bf16(a)
    bh, bl = _split_bf16(b)
    return _dot(ah, bh) + _dot(ah, bl) + _dot(al, bh)


def _dot_x2_lhs(a, b_exact):
    ah, al = _split_bf16(a)
    return _dot(ah, b_exact) + _dot(al, b_exact)


def _dot_x2_rhs(a_exact, b):
    bh, bl = _split_bf16(b)
    return _dot(a_exact, bh) + _dot(a_exact, bl)


def _bdot(a, b):
    return _dot(a.astype(BF16), b.astype(BF16))


def _sigmoid(x):
    return 0.5 * jnp.tanh(0.5 * x) + 0.5


def _silu(x):
    return x * _sigmoid(x)


def _rms_mod(x, gain, scale, shift):
    y = x * lax.rsqrt(jnp.mean(x * x, axis=-1, keepdims=True) + NORM_EPS)
    return (y * gain) * (1.0 + scale) + shift


def _mod_kernel(c_ref, w_ref, b_ref, o_ref):
    c = c_ref[...]
    o_ref[0] = _dot(_silu(c), w_ref[0], HIGHEST) + b_ref[0]


def _mod_call(c, ada_w, ada_b):
    depth, d, d6 = ada_w.shape
    b = c.shape[0]
    nblk = d6 // d
    return pl.pallas_call(
        _mod_kernel,
        grid=(depth, nblk),
        in_specs=[
            pl.BlockSpec((b, d), lambda l, j: (0, 0)),
            pl.BlockSpec((1, d, d), lambda l, j: (l, 0, j)),
            pl.BlockSpec((1, 1, d), lambda l, j: (l, 0, j)),
        ],
        out_specs=pl.BlockSpec((1, b, d), lambda l, j: (l, 0, j)),
        out_shape=jax.ShapeDtypeStruct((depth, b, d6), F32),
        compiler_params=_cparams(("parallel", "parallel")),
        name="adaln_mod",
    )(c, ada_w, ada_b.reshape(depth, 1, d6))


def _rope_kernel(pos_ref, freq_ref, sign_ref, cos_ref, sin_ref):
    ang = pos_ref[0].astype(F32) * freq_ref[...]
    cos_ref[0] = jnp.cos(ang)
    sin_ref[0] = jnp.sin(ang) * sign_ref[...]


def _rope_call(positions, d):
    b, t = positions.shape
    tb = min(t, 512)
    inv_freq = ROPE_THETA ** (-jnp.arange(0, d, 2, dtype=F32) / d)
    freq2 = jnp.concatenate([inv_freq, inv_freq]).reshape(1, d)
    sign2 = jnp.concatenate([-jnp.ones((d // 2,), F32), jnp.ones((d // 2,), F32)]).reshape(1, d)
    out = jax.ShapeDtypeStruct((b, t, d), F32)
    return pl.pallas_call(
        _rope_kernel,
        grid=(b, t // tb),
        in_specs=[
            pl.BlockSpec((1, tb, 1), lambda i, j: (i, j, 0)),
            pl.BlockSpec((1, d), lambda i, j: (0, 0)),
            pl.BlockSpec((1, d), lambda i, j: (0, 0)),
        ],
        out_specs=[pl.BlockSpec((1, tb, d), lambda i, j: (i, j, 0))] * 2,
        out_shape=[out, out],
        compiler_params=_cparams(("parallel", "parallel")),
        name="rope_tables",
    )(positions.reshape(b, t, 1), freq2, sign2)


W_BLK = 256


def _wprep_kernel(w_ref, o_ref):
    o_ref[...] = w_ref[...].astype(o_ref.dtype)


def _wprep_call(w_in):
    depth, d, cols = w_in.shape
    nblk = cols // W_BLK
    first = (cols - 3 * d) // W_BLK
    return pl.pallas_call(
        _wprep_kernel,
        grid=(depth, nblk),
        in_specs=[pl.BlockSpec((1, d, W_BLK), lambda l, j: (l, 0, (j + first) % nblk))],
        out_specs=pl.BlockSpec((1, d, W_BLK), lambda l, j: (l, 0, j)),
        out_shape=jax.ShapeDtypeStruct((depth, d, cols), BF16),
        compiler_params=_cparams(("parallel", "parallel")),
        name="w_in_layout",
    )(w_in)


def _inproj_kernel(x_ref, g_ref, scale_ref, shift_ref, w_ref, o_ref, h_ref):
    @pl.when(pl.program_id(1) == 0)
    def _():
        h = _rms_mod(x_ref[...], g_ref[...], scale_ref[0], shift_ref[0])
        h_ref[...] = h.astype(BF16)

    o_ref[...] = _dot(h_ref[...], w_ref[0]).astype(o_ref.dtype)


def _inproj_call(x2, gain, mod3, w_bf16, layer, seq, scale_blk, shift_blk, tm=2048, tn=1792):
    n, d = x2.shape
    cols = w_bf16.shape[2]
    tpb = seq // tm
    return pl.pallas_call(
        _inproj_kernel,
        grid=(n // tm, cols // tn),
        in_specs=[
            pl.BlockSpec((tm, d), lambda i, j: (i, 0)),
            pl.BlockSpec((1, d), lambda i, j: (0, 0)),
            pl.BlockSpec((1, 1, d), lambda i, j: (i // tpb, 0, scale_blk)),
            pl.BlockSpec((1, 1, d), lambda i, j: (i // tpb, 0, shift_blk)),
            pl.BlockSpec((1, d, tn), lambda i, j: (layer, 0, j)),
        ],
        out_specs=pl.BlockSpec((tm, tn), lambda i, j: (i, j)),
        out_shape=jax.ShapeDtypeStruct((n, cols), Z_DTYPE),
        scratch_shapes=[pltpu.VMEM((tm, d), BF16)],
        compiler_params=_cparams(("parallel", "arbitrary")),
        name="norm_inproj",
    )(x2, gain.reshape(1, d), mod3, mod3, w_bf16)


def _hgrn2_block(zs, lbs, nw, sts, factored):
    hs = range(len(zs))
    tb = zs[0][0].shape[0]
    c, sub = HG_CHUNK, HG_SUB
    nc, ns, nb = tb // c, c // sub, tb // sub
    f = [lbs[h] + (1.0 - lbs[h]) * _sigmoid(zs[h][1]) for h in hs]
    logf = [jnp.log(jnp.maximum(f[h], 1e-30)) for h in hs]
    q = [_silu(zs[h][0]) * (HG_DK ** -0.5) for h in hs]
    k = [1.0 - f[h] for h in hs]
    v = [zs[h][2] for h in hs]
    v_b = [v[h].astype(BF16) for h in hs]
    row = lax.broadcasted_iota(jnp.int32, (tb, tb), 0)
    col = lax.broadcasted_iota(jnp.int32, (tb, tb), 1)
    tri = jnp.where(col >= (row // c) * c, jnp.where(row >= col, 1.0, 0.0), 0.0).astype(BF16)
    cum = [_dot_x2_rhs(tri, logf[h]) for h in hs]
    cum3 = [cum[h].reshape(nb, sub, HG_DK) for h in hs]
    ref3 = [cum3[h][:, 0:1, :] - logf[h].reshape(nb, sub, HG_DK)[:, 0:1, :] for h in hs]
    span = functools.reduce(jnp.maximum, [jnp.max(ref3[h] - cum3[h][:, sub - 1:sub, :]) for h in hs])
    qe = [(q[h] * jnp.exp(cum[h])).astype(BF16) for h in hs]

    offd = [(h, ci * c, ci * c + sub * i) for h in hs for ci in range(nc) for i in range(1, ns)]
    base = [cum[h][lo - 1:lo] for h, _, lo in offd]
    qt = [(q[h][lo:lo + sub] * jnp.exp(cum[h][lo:lo + sub] - base[j])).astype(BF16)
          for j, (h, _, lo) in enumerate(offd)]
    kt = [(k[h][r0:lo] * jnp.exp(base[j] - cum[h][r0:lo])).astype(BF16) for j, (h, r0, lo) in enumerate(offd)]
    a = [_dot_nt(qt[j], kt[j]).astype(BF16) for j in range(len(offd))]
    av = {(h, lo): _dot(a[j], v_b[h][r0:lo]) for j, (h, r0, lo) in enumerate(offd)}

    cs = [slice(ci * c, (ci + 1) * c) for ci in range(nc)]
    hc = [(h, ci) for h in hs for ci in range(nc)]
    last = {(h, ci): cum[h][(ci + 1) * c - 1:(ci + 1) * c] for h, ci in hc}
    kd = {(h, ci): (k[h][cs[ci]] * jnp.exp(last[h, ci] - cum[h][cs[ci]])).astype(BF16) for h, ci in hc}
    inc = {(h, ci): _dot_tn(v_b[h][cs[ci]], kd[h, ci]) for h, ci in hc}
    s_in = {(h, 0): sts[h] for h in hs}
    for ci in range(nc):
        for h in hs:
            s_in[h, ci + 1] = s_in[h, ci] * jnp.exp(last[h, ci]) + inc[h, ci]
    o_inter = {(h, ci): _dot_nt(qe[h][cs[ci]], s_in[h, ci].astype(BF16)) for h, ci in hc}

    if factored:
        qf = [(q[h] * jnp.exp(cum3[h] - ref3[h]).reshape(tb, HG_DK)).astype(BF16) for h in hs]
        kf = [(k[h] * jnp.exp(ref3[h] - cum3[h]).reshape(tb, HG_DK)).astype(BF16) for h in hs]
        rc = lax.broadcasted_iota(jnp.int32, (c, c), 0)
        cc = lax.broadcasted_iota(jnp.int32, (c, c), 1)
        keep = (rc >= cc) & (rc // sub == cc // sub)
        a_d = {(h, ci): jnp.where(keep, _dot_nt(qf[h][cs[ci]], kf[h][cs[ci]]), 0.0).astype(BF16) for h, ci in hc}
        dg = {(h, ci): _dot(a_d[h, ci], v_b[h][cs[ci]]) for h, ci in hc}
        diag = [jnp.concatenate([dg[h, ci] for ci in range(nc)], axis=0) for h in hs]
    else:
        gb = 4
        trow = lax.broadcasted_iota(jnp.int32, (gb, sub, HG_DK), 1)
        diag = []
        for h in hs:
            c2 = cum[h] * LOG2E
            ks2 = c2 - jnp.log2(k[h])
            parts = []
            for g0 in range(0, nb, gb):
                rws = slice(g0 * sub, (g0 + gb) * sub)
                c23, ks23, q3, v3 = (x[rws].reshape(gb, sub, HG_DK) for x in (c2, ks2, q[h], v[h]))
                acc = jnp.zeros((gb, sub, HG_DK), F32)
                for s in range(sub):
                    e = jnp.exp2(jnp.where(trow >= s, c23 - ks23[:, s:s + 1, :], -jnp.inf))
                    a_col = jnp.sum(q3 * e, axis=-1, keepdims=True)
                    acc = acc + a_col * v3[:, s:s + 1, :]
                parts.append(acc.reshape(gb * sub, HG_DK))
            diag.append(jnp.concatenate(parts, axis=0))

    outs = []
    for h in hs:
        pieces = []
        for ci in range(nc):
            for i in range(ns):
                lo = ci * c + sub * i
                piece = o_inter[h, ci][sub * i:sub * (i + 1)] + diag[h][lo:lo + sub]
                pieces.append(piece + av[h, lo] if i > 0 else piece)
        o = jnp.concatenate(pieces, axis=0)
        o = o * lax.rsqrt(jnp.mean(o * o, axis=-1, keepdims=True) + NORM_EPS)
        outs.append(o * nw * _silu(zs[h][3]))
    return outs, [s_in[h, nc] for h in hs], span


def _hgrn2_kernel(zq_ref, zf_ref, zi_ref, zg_ref, lb_ref, nw_ref, o_ref, st_ref):
    @pl.when(pl.program_id(1) == 0)
    def _():
        st_ref[...] = jnp.zeros_like(st_ref)

    hs = range(HG_HEADS)
    sl = [slice(h * HG_DK, (h + 1) * HG_DK) for h in hs]

    def run(factored):
        zs = [tuple(r[0, :, sl[h]].astype(F32) for r in (zq_ref, zf_ref, zi_ref, zg_ref)) for h in hs]
        outs, sts, span = _hgrn2_block(zs, [lb_ref[:, sl[h]] for h in hs], nw_ref[...],
                                       [st_ref[h] for h in hs], factored)
        return jnp.concatenate(outs, axis=1), sts, span

    st_old = [st_ref[h] for h in hs]
    o, st_new, span = run(True)
    for h in hs:
        st_ref[h] = st_new[h]
    o_ref[0] = o.astype(o_ref.dtype)

    @pl.when(span > HG_SAFE_SPAN)
    def _():
        for h in hs:
            st_ref[h] = st_old[h]
        o2, st2, _ = run(False)
        for h in hs:
            st_ref[h] = st2[h]
        o_ref[0] = o2.astype(o_ref.dtype)


def _hgrn2_call(z3, lower_bound, norm_w, tb=256):
    b, t, _ = z3.shape
    tb = min(tb, t)
    base = HG_OFF // HG_W

    def zspec(part):
        return pl.BlockSpec((1, tb, HG_W), lambda i, j: (i, j, base + part))

    return pl.pallas_call(
        _hgrn2_kernel,
        grid=(b, t // tb),
        in_specs=[
            zspec(0), zspec(1), zspec(2), zspec(3),
            pl.BlockSpec((1, HG_W), lambda i, j: (0, 0)),
            pl.BlockSpec((1, LANES), lambda i, j: (0, 0)),
        ],
        out_specs=pl.BlockSpec((1, tb, HG_W), lambda i, j: (i, j, 0)),
        out_shape=jax.ShapeDtypeStruct((b, t, HG_W), BF16),
        scratch_shapes=[pltpu.VMEM((HG_HEADS, HG_DK, HG_DK), F32)],
        compiler_params=_cparams(("parallel", "arbitrary")),
        name="hgrn2_mixer",
    )(z3, z3, z3, z3, lower_bound.reshape(1, HG_W), norm_w.reshape(1, HG_DK))


def _ret_kernel(zq_ref, zk_ref, zv_ref, zg_ref, cos_ref, sin_ref, o_ref, st_ref, dmask_ref, *, chunk):
    hs = range(RET_HEADS)
    sl = [slice(h * RET_DK, (h + 1) * RET_DK) for h in hs]
    lg = [jnp.log(jnp.full((1, 1), 1.0 - 2.0 ** (-5.0 - h), F32)) for h in hs]

    @pl.when(pl.program_id(1) == 0)
    def _():
        st_ref[...] = jnp.zeros_like(st_ref)
        row = lax.broadcasted_iota(jnp.int32, (chunk, chunk), 0)
        col = lax.broadcasted_iota(jnp.int32, (chunk, chunk), 1)
        rel = (row - col).astype(F32)
        for h in hs:
            dmask_ref[h] = jnp.where(rel >= 0.0, jnp.exp(jnp.maximum(rel, 0.0) * lg[h]), 0.0)

    cos2 = cos_ref[0]
    sin2 = sin_ref[0]
    half = RET_DK // 2

    def rope(z):
        return z * cos2 + pltpu.roll(z, half, 1) * sin2

    tcol = lax.broadcasted_iota(jnp.int32, (chunk, 1), 0).astype(F32)
    q = [rope(zq_ref[0, :, sl[h]].astype(F32)) * (RET_DK ** -0.5) for h in hs]
    k = [rope(zk_ref[0, :, sl[h]].astype(F32)) for h in hs]
    v_b = [zv_ref[0, :, sl[h]].astype(BF16) for h in hs]
    st = [st_ref[h] for h in hs]
    scores = [(_dot_nt(q[h].astype(BF16), k[h].astype(BF16)) * dmask_ref[h]).astype(BF16) for h in hs]
    qx = [(q[h] * jnp.exp((tcol + 1.0) * lg[h])).astype(BF16) for h in hs]
    kz = [(k[h] * jnp.exp((chunk - 1.0 - tcol) * lg[h])).astype(BF16) for h in hs]
    o = [_dot(scores[h], v_b[h]) + _dot_nt(qx[h], st[h].astype(BF16)) for h in hs]
    for h in hs:
        st_ref[h] = st[h] * jnp.exp(chunk * lg[h]) + _dot_tn(v_b[h], kz[h])
    o = [o[h] * lax.rsqrt(jnp.mean(o[h] * o[h], axis=-1, keepdims=True) + NORM_EPS) for h in hs]
    o_ref[0] = (jnp.concatenate(o, axis=1) * _silu(zg_ref[0].astype(F32))).astype(o_ref.dtype)


def _ret_call(z3, cos2, sin2, chunk=256):
    b, t, _ = z3.shape
    chunk = min(chunk, t)
    base = RET_OFF // RET_W

    def zspec(part):
        return pl.BlockSpec((1, chunk, RET_W), lambda i, j: (i, j, base + part))

    tab = pl.BlockSpec((1, chunk, RET_DK), lambda i, j: (i, j, 0))
    return pl.pallas_call(
        functools.partial(_ret_kernel, chunk=chunk),
        grid=(b, t // chunk),
        in_specs=[zspec(0), zspec(1), zspec(2), zspec(3), tab, tab],
        out_specs=pl.BlockSpec((1, chunk, RET_W), lambda i, j: (i, j, 0)),
        out_shape=jax.ShapeDtypeStruct((b, t, RET_W), BF16),
        scratch_shapes=[pltpu.VMEM((RET_HEADS, RET_DK, RET_DK), F32), pltpu.VMEM((RET_HEADS, chunk, chunk), F32)],
        compiler_params=_cparams(("parallel", "arbitrary")),
        name="retention_mixer",
    )(z3, z3, z3, z3, cos2, sin2)


def _inv_unit_lower(a, eye, blk_mask):
    c = a[0].shape[0]
    m = range(len(a))
    a_bd = [jnp.where(blk_mask, a[i], 0.0) for i in m]
    a_off = [a[i] - a_bd[i] for i in m]
    a2 = [_bdot(a_bd[i], a_bd[i]) for i in m]
    p = [eye + a_bd[i] for i in m]
    r = [_bdot(jnp.concatenate([p[i], a2[i]], axis=0), a2[i]) for i in m]
    p = [p[i] + r[i][:c] for i in m]
    a4 = [r[i][c:] for i in m]
    r = [_bdot(jnp.concatenate([p[i], a4[i]], axis=0), a4[i]) for i in m]
    p = [p[i] + r[i][:c] for i in m]
    a8 = [r[i][c:] for i in m]
    t_bd = [p[i] + _bdot(p[i], a8[i]) for i in m]
    n = [_bdot(t_bd[i], a_off[i]) for i in m]
    r = [_bdot(n[i], jnp.concatenate([n[i], t_bd[i]], axis=1)) for i in m]
    z = [t_bd[i] + r[i][:, c:] for i in m]
    return [z[i] + _bdot(r[i][:, :c], z[i]) for i in m]


def _rwkv_kernel(z_ref, mu_ref, w0_ref, w2_ref, a0_ref, a2_ref, g2_ref, kk_ref, ka_ref, rk_ref,
                 lnw_ref, lnb_ref, seg_ref, o_ref, s_ref, prev_ref):
    c = RW_CHUNK
    tb = z_ref.shape[1]
    nck = tb // c

    @pl.when(pl.program_id(1) == 0)
    def _():
        s_ref[...] = jnp.zeros_like(s_ref)
        prev_ref[...] = jnp.zeros_like(prev_ref)

    z = z_ref[0].astype(F32)
    rows = lax.broadcasted_iota(jnp.int32, (tb, 1), 0)
    z_prev = jnp.where(rows == 0, prev_ref[...], pltpu.roll(z, 1, 0))
    prev_ref[...] = z[tb - 1:tb]
    zs = z + mu_ref[...] * (z_prev - z)
    r = zs[:, 0:RW_W]
    k = zs[:, RW_W:2 * RW_W]
    v = zs[:, 2 * RW_W:3 * RW_W]
    off = 3 * RW_W
    w_lo = zs[:, off:off + RW_DECAY_LORA]
    a_lo = zs[:, off + RW_DECAY_LORA:off + RW_DECAY_LORA + RW_A_LORA]
    g_lo = zs[:, off + RW_DECAY_LORA + RW_A_LORA:]

    wx = -(w0_ref[...] + _dot_x3(jnp.tanh(w_lo), w2_ref[...]))
    softplus = jnp.maximum(wx, 0.0) + jnp.log(1.0 + jnp.exp(-jnp.abs(wx)))
    logw = -jnp.exp(-softplus - 0.5)
    a = _sigmoid(a0_ref[...] + _dot_x3(a_lo, a2_ref[...]))
    g = _dot_x3(_sigmoid(g_lo), g2_ref[...])
    seg = seg_ref[...]
    kk = k * kk_ref[...]
    kk = kk * lax.rsqrt(jnp.maximum(_dot_x2_lhs(kk * kk, seg), 1e-24))
    k2 = k * (1.0 + (a - 1.0) * ka_ref[...])

    row = lax.broadcasted_iota(jnp.int32, (c, c), 0)
    col = lax.broadcasted_iota(jnp.int32, (c, c), 1)
    blk_mask = (row // RW_BLK) == (col // RW_BLK)
    eye = (row == col).astype(F32)
    row2 = lax.broadcasted_iota(jnp.int32, (c, 2 * c), 0)
    col2 = lax.broadcasted_iota(jnp.int32, (c, 2 * c), 1) % c
    incl2 = row2 >= col2
    strict2 = row2 > col2
    rowb = lax.broadcasted_iota(jnp.int32, (tb, tb), 0)
    colb = lax.broadcasted_iota(jnp.int32, (tb, tb), 1)
    tri = jnp.where(colb >= (rowb // c) * c, jnp.where(rowb >= colb, 1.0, 0.0), 0.0).astype(BF16)
    cw = _dot_x2_rhs(tri, logw)
    w_inv = jnp.exp(-cw)
    last = jnp.concatenate([jnp.broadcast_to(cw[(ci + 1) * c - 1:(ci + 1) * c], (c, RW_W)) for ci in range(nck)],
                           axis=0)
    w_rest = jnp.exp(last - cw)
    beta = a * kk
    alpha_t = -kk * jnp.exp(cw - logw)
    r_t = r * jnp.exp(cw)
    beta_h = beta * w_inv
    k_h = k2 * w_inv
    beta_d = beta * w_rest
    k_d = k2 * w_rest

    hs = range(RW_HEADS)
    ph = [(ci, h) for ci in range(nck) for h in hs]
    m = range(len(ph))
    rs = [slice(ci * c, (ci + 1) * c) for ci, _ in ph]
    sl = [slice(h * RW_N, (h + 1) * RW_N) for _, h in ph]
    v_h = [v[rs[i], sl[i]] for i in m]
    lhs = [jnp.concatenate([alpha_t[rs[i], sl[i]], r_t[rs[i], sl[i]]], axis=0).astype(BF16) for i in m]
    rhs = [jnp.concatenate([beta_h[rs[i], sl[i]], k_h[rs[i], sl[i]]], axis=0).astype(BF16) for i in m]
    big = [_dot_nt(lhs[i], rhs[i]) for i in m]
    a_a = [jnp.where(strict2, big[i][:c], 0.0) for i in m]
    a_r = [jnp.where(incl2, big[i][c:], 0.0).astype(BF16) for i in m]
    t_inv = _inv_unit_lower([a_a[i][:, :c] for i in m], eye, blk_mask)
    av = [_bdot(a_a[i][:, c:], v_h[i]) for i in m]
    bk_d = [jnp.concatenate([beta_d[rs[i], sl[i]], k_d[rs[i], sl[i]]], axis=0).astype(BF16) for i in m]
    s_cur = [s_ref[h] for h in hs]
    o_chunks = []
    for ci in range(nck):
        ix = [ci * RW_HEADS + h for h in hs]
        sd = [_dot_nt(lhs[ix[h]], s_cur[h].astype(BF16)) for h in hs]
        u = [_bdot(t_inv[ix[h]], sd[h][:c] + av[ix[h]]) for h in hs]
        uv = [jnp.concatenate([u[h], v_h[ix[h]]], axis=0).astype(BF16) for h in hs]
        o_chunks.append(jnp.concatenate([sd[h][c:] + _dot(a_r[ix[h]], uv[h]) for h in hs], axis=1))
        w_last = jnp.exp(cw[(ci + 1) * c - 1:(ci + 1) * c])
        s_cur = [s_cur[h] * w_last[:, sl[h]] + _dot_tn(uv[h], bk_d[ix[h]]) for h in hs]
    for h in hs:
        s_ref[h] = s_cur[h]
    o = jnp.concatenate(o_chunks, axis=0)

    mean = _dot_x2_lhs(o, seg) * (1.0 / RW_N)
    dev = o - mean
    var = _dot_x2_lhs(dev * dev, seg) * (1.0 / RW_N)
    o = dev * lax.rsqrt(var + RW_GN_EPS) * lnw_ref[...] + lnb_ref[...]
    bonus = _dot_x2_lhs(r * k2 * rk_ref[...], seg) * v
    o_ref[0] = ((o + bonus) * g).astype(o_ref.dtype)


def _rwkv_call(z3, mu, w0, w2, a0, a2, g2, k_k, k_a, r_k, ln_w, ln_b):
    b, t, _ = z3.shape
    c = min(RW_TB, t)
    hid = lax.broadcasted_iota(jnp.int32, (RW_W, RW_W), 0) // RW_N
    seg = (hid == hid.T).astype(BF16)

    def vec(n):
        return pl.BlockSpec((1, n), lambda i, j: (0, 0))

    def mat(m, n):
        return pl.BlockSpec((m, n), lambda i, j: (0, 0))

    return pl.pallas_call(
        _rwkv_kernel,
        grid=(b, t // c),
        in_specs=[
            pl.BlockSpec((1, c, RW_COLS), lambda i, j: (i, j, RW_OFF // RW_COLS)),
            vec(RW_COLS), vec(RW_W), mat(RW_DECAY_LORA, RW_W), vec(RW_W), mat(RW_A_LORA, RW_W),
            mat(RW_GATE_LORA, RW_W), vec(RW_W), vec(RW_W), vec(RW_W), vec(RW_W), vec(RW_W),
            mat(RW_W, RW_W),
        ],
        out_specs=pl.BlockSpec((1, c, RW_W), lambda i, j: (i, j, 0)),
        out_shape=jax.ShapeDtypeStruct((b, t, RW_W), BF16),
        scratch_shapes=[pltpu.VMEM((RW_HEADS, RW_N, RW_N), F32), pltpu.VMEM((1, RW_COLS), F32)],
        compiler_params=_cparams(("parallel", "arbitrary")),
        name="rwkv7_mixer",
    )(z3, mu.reshape(1, -1), w0.reshape(1, -1), w2, a0.reshape(1, -1), a2, g2, k_k.reshape(1, -1),
      k_a.reshape(1, -1), r_k.reshape(1, -1), ln_w.reshape(1, -1), ln_b.reshape(1, -1), seg)


def _merge_kernel(ohg_ref, oret_ref, orw_ref, zg_ref, x_ref, gate_ref, bhg_ref, bret_ref, brw_ref,
                  wout_ref, o_ref):
    d = x_ref.shape[1]
    y = _sigmoid(zg_ref[:, 0:d].astype(F32)) * _dot(ohg_ref[...], bhg_ref[...])
    y = y + _sigmoid(zg_ref[:, d:2 * d].astype(F32)) * _dot(oret_ref[...], bret_ref[...])
    y = y + _sigmoid(zg_ref[:, 2 * d:3 * d].astype(F32)) * _dot(orw_ref[...], brw_ref[...])
    o_ref[...] = x_ref[...] + gate_ref[0] * _dot(y.astype(BF16), wout_ref[...])


def _merge_call(o_hg, o_ret, o_rw, z2, x2, mod3, br_hg, br_ret, br_rw, w_out, seq, gate_blk, tm=512):
    n, d = x2.shape
    tpb = seq // tm

    def rows(w):
        return pl.BlockSpec((tm, w), lambda i: (i, 0))

    def full(m, k):
        return pl.BlockSpec((m, k), lambda i: (0, 0))

    return pl.pallas_call(
        _merge_kernel,
        grid=(n // tm,),
        in_specs=[
            rows(HG_W), rows(RET_W), rows(RW_W), rows(3 * d), rows(d),
            pl.BlockSpec((1, 1, d), lambda i: (i // tpb, 0, gate_blk)),
            full(HG_W, d), full(RET_W, d), full(RW_W, d), full(d, d),
        ],
        out_specs=rows(d),
        out_shape=jax.ShapeDtypeStruct((n, d), F32),
        compiler_params=_cparams(("parallel",)),
        name="merge_outproj",
    )(o_hg, o_ret, o_rw, z2, x2, mod3, br_hg, br_ret, br_rw, w_out)


def _pack_bf16_pairs(x):
    w = x.shape[1] // 2
    hi = pltpu.bitcast(x[:, :w].astype(BF16).astype(F32), jnp.uint32)
    lo = pltpu.bitcast(x[:, w:].astype(BF16).astype(F32), jnp.uint32)
    return pltpu.bitcast(hi | lax.shift_right_logical(lo, jnp.uint32(16)), jnp.int32)


def _unpack_bf16_pairs(p):
    u = pltpu.bitcast(p, jnp.uint32)
    hi = pltpu.bitcast(u & jnp.uint32(0xFFFF0000), F32)
    lo = pltpu.bitcast(lax.shift_left(u, jnp.uint32(16)), F32)
    return jnp.concatenate([hi, lo], axis=1)


def _route_kernel(x_ref, g_ref, scale_ref, shift_ref, rc_ref, hp_ref, eid_ref, wts_ref, cnt_ref):
    @pl.when(pl.program_id(0) == 0)
    def _():
        cnt_ref[...] = jnp.zeros_like(cnt_ref)

    h = _rms_mod(x_ref[...], g_ref[...], scale_ref[0], shift_ref[0])
    hp_ref[...] = _pack_bf16_pairs(h)
    tm = h.shape[0]
    lane = lax.broadcasted_iota(jnp.int32, (tm, LANES), 1)
    neg = -jnp.inf
    logits = _dot_x3(h, rc_ref[...])
    gl = jnp.where(lane < N_GROUPS, logits, neg)
    gmax = jnp.max(gl, axis=-1, keepdims=True)
    gidx = jnp.min(jnp.where(gl == gmax, lane, LANES), axis=-1, keepdims=True)
    gw = 1.0 / jnp.sum(jnp.exp(gl - gmax), axis=-1, keepdims=True)
    lo = N_GROUPS + gidx * EXPERTS_PER_GROUP
    el = jnp.where(lane >= lo, jnp.where(lane < lo + EXPERTS_PER_GROUP, logits, neg), neg)
    m1 = jnp.max(el, axis=-1, keepdims=True)
    l1 = jnp.min(jnp.where(el == m1, lane, LANES), axis=-1, keepdims=True)
    el2 = jnp.where(lane == l1, neg, el)
    m2 = jnp.max(el2, axis=-1, keepdims=True)
    l2 = jnp.min(jnp.where(el2 == m2, lane, LANES), axis=-1, keepdims=True)
    i1 = l1 - N_GROUPS
    i2 = l2 - N_GROUPS
    e2 = jnp.exp(m2 - m1)
    p1 = 1.0 / (1.0 + e2)
    p2 = e2 * p1
    oh1 = jnp.where(lane == i1, 1.0, 0.0)
    oh2 = jnp.where(lane == i2, 1.0, 0.0)
    row = lax.broadcasted_iota(jnp.int32, (tm, tm), 0)
    col = lax.broadcasted_iota(jnp.int32, (tm, tm), 1)
    earlier = jnp.where(row > col, 1.0, 0.0).astype(BF16)
    before = _dot(earlier, jnp.concatenate([oh1, oh2], axis=1).astype(BF16))
    tot1 = jnp.sum(oh1, axis=0, keepdims=True)
    carry = cnt_ref[...]
    r1 = jnp.sum(oh1 * (before[:, :LANES] + carry), axis=-1, keepdims=True).astype(jnp.int32)
    r2 = jnp.sum(oh2 * (before[:, LANES:] + (carry + tot1)), axis=-1, keepdims=True).astype(jnp.int32)
    cnt_ref[...] = carry + tot1 + jnp.sum(oh2, axis=0, keepdims=True)
    eid_ref[...] = jnp.where(lane == 0, i1, jnp.where(lane == 1, i2, jnp.where(lane == 2, r1,
                                                                             jnp.where(lane == 3, r2, 0))))
    wts_ref[...] = jnp.where(lane == 0, gw * p1, jnp.where(lane == 1, gw * p2, 0.0))


def _route_call(x2, gain, mod3, router_g, router_e, seq, scale_blk, shift_blk, tm=512):
    n, d = x2.shape
    tpb = seq // tm
    rc = jnp.pad(jnp.concatenate([router_g, router_e], axis=1), ((0, 0), (0, LANES - N_GROUPS - N_EXPERTS)))
    return pl.pallas_call(
        _route_kernel,
        grid=(n // tm,),
        in_specs=[
            pl.BlockSpec((tm, d), lambda i: (i, 0)),
            pl.BlockSpec((1, d), lambda i: (0, 0)),
            pl.BlockSpec((1, 1, d), lambda i: (i // tpb, 0, scale_blk)),
            pl.BlockSpec((1, 1, d), lambda i: (i // tpb, 0, shift_blk)),
            pl.BlockSpec((d, LANES), lambda i: (0, 0)),
        ],
        out_specs=[pl.BlockSpec((tm, d // 2), lambda i: (i, 0)), pl.BlockSpec((tm, LANES), lambda i: (i, 0)),
                   pl.BlockSpec((tm, LANES), lambda i: (i, 0)), pl.BlockSpec((1, LANES), lambda i: (0, 0))],
        out_shape=[jax.ShapeDtypeStruct((n, d // 2), jnp.int32), jax.ShapeDtypeStruct((n, LANES), jnp.int32),
                   jax.ShapeDtypeStruct((n, LANES), F32), jax.ShapeDtypeStruct((1, LANES), F32)],
        compiler_params=_cparams(("arbitrary",)),
        name="moe_route",
    )(x2, gain.reshape(1, d), mod3, mod3, rc)


SC_CORES = 2
SC_SUBCORES = 16
SC_WORKERS = SC_CORES * SC_SUBCORES
SC_ROWS = 64


def _sc_gather(table, idx):
    m = idx.shape[0]
    w = table.shape[1]
    per_worker = m // SC_WORKERS
    steps = per_worker // SC_ROWS
    assert per_worker * SC_WORKERS == m and steps * SC_ROWS == per_worker and steps % 2 == 0
    mesh = plsc.VectorSubcoreMesh(core_axis_name="c", subcore_axis_name="s")

    def body(table_hbm, idx_hbm, out_hbm, idx_v, rows_a, rows_b, sem_ga, sem_gb, sem_wa, sem_wb):
        wid = lax.axis_index("s") * SC_CORES + lax.axis_index("c")
        pltpu.sync_copy(idx_hbm.at[wid], idx_v)

        @pl.loop(0, steps, step=2)
        def _(j):
            row0 = wid * per_worker + j * SC_ROWS
            ga = pltpu.async_copy(table_hbm.at[idx_v.at[j]], rows_a, sem_ga)
            gb = pltpu.async_copy(table_hbm.at[idx_v.at[j + 1]], rows_b, sem_gb)
            ga.wait()
            wa = pltpu.async_copy(rows_a, out_hbm.at[pl.ds(row0, SC_ROWS)], sem_wa)
            gb.wait()
            wb = pltpu.async_copy(rows_b, out_hbm.at[pl.ds(row0 + SC_ROWS, SC_ROWS)], sem_wb)
            wa.wait()
            wb.wait()

    return pl.kernel(
        body,
        out_type=jax.ShapeDtypeStruct((m, w), table.dtype),
        mesh=mesh,
        scratch_types=[pltpu.VMEM((steps, SC_ROWS), jnp.int32), pltpu.VMEM((SC_ROWS, w), table.dtype),
                       pltpu.VMEM((SC_ROWS, w), table.dtype), pltpu.SemaphoreType.DMA, pltpu.SemaphoreType.DMA,
                       pltpu.SemaphoreType.DMA, pltpu.SemaphoreType.DMA],
        name="sc_row_gather",
    )(table, idx.reshape(SC_WORKERS, steps, SC_ROWS))


def _sc_scatter2(rows, idx0, idx1, p):
    n, w = rows.shape
    per_worker = n // SC_WORKERS
    steps = per_worker // SC_ROWS
    assert per_worker * SC_WORKERS == n and steps * SC_ROWS == per_worker and steps % 2 == 0
    mesh = plsc.VectorSubcoreMesh(core_axis_name="c", subcore_axis_name="s")

    def body(rows_hbm, i0_hbm, i1_hbm, out_hbm, i0_v, i1_v, buf_a, buf_b, s_ra, s_rb, s_a0, s_a1, s_b0, s_b1):
        wid = lax.axis_index("s") * SC_CORES + lax.axis_index("c")
        pltpu.sync_copy(i0_hbm.at[wid], i0_v)
        pltpu.sync_copy(i1_hbm.at[wid], i1_v)

        @pl.loop(0, steps, step=2)
        def _(j):
            row0 = wid * per_worker + j * SC_ROWS
            ra = pltpu.async_copy(rows_hbm.at[pl.ds(row0, SC_ROWS)], buf_a, s_ra)
            rb = pltpu.async_copy(rows_hbm.at[pl.ds(row0 + SC_ROWS, SC_ROWS)], buf_b, s_rb)
            ra.wait()
            a0 = pltpu.async_copy(buf_a, out_hbm.at[i0_v.at[j]], s_a0)
            a1 = pltpu.async_copy(buf_a, out_hbm.at[i1_v.at[j]], s_a1)
            rb.wait()
            b0 = pltpu.async_copy(buf_b, out_hbm.at[i0_v.at[j + 1]], s_b0)
            b1 = pltpu.async_copy(buf_b, out_hbm.at[i1_v.at[j + 1]], s_b1)
            a0.wait()
            a1.wait()
            b0.wait()
            b1.wait()

    return pl.kernel(
        body,
        out_type=jax.ShapeDtypeStruct((p, w), rows.dtype),
        mesh=mesh,
        scratch_types=[pltpu.VMEM((steps, SC_ROWS), jnp.int32), pltpu.VMEM((steps, SC_ROWS), jnp.int32),
                       pltpu.VMEM((SC_ROWS, w), rows.dtype), pltpu.VMEM((SC_ROWS, w), rows.dtype)]
        + [pltpu.SemaphoreType.DMA] * 6,
        name="sc_row_scatter",
    )(rows, idx0.reshape(SC_WORKERS, steps, SC_ROWS), idx1.reshape(SC_WORKERS, steps, SC_ROWS))


MOE_TM = 512


def _gexperts_kernel(te_ref, tv_ref, nu_ref, xs_ref, w1_ref, w3_ref, w2_ref, ys_ref, w1b_ref, w3b_ref, w2b_ref):
    i = pl.program_id(0)

    @pl.when((i == 0) | (te_ref[i] != te_ref[jnp.maximum(i - 1, 0)]))
    def _():
        w1b_ref[...] = w1_ref[0].astype(BF16)
        w3b_ref[...] = w3_ref[0].astype(BF16)
        w2b_ref[...] = w2_ref[0].astype(BF16)

    @pl.when(i < nu_ref[0])
    def _():
        rid = lax.broadcasted_iota(jnp.int32, xs_ref.shape, 0)
        xb = _unpack_bf16_pairs(jnp.where(rid < tv_ref[i], xs_ref[...], 0)).astype(BF16)
        act = (_silu(_dot(xb, w1b_ref[...])) * _dot(xb, w3b_ref[...])).astype(BF16)
        ys_ref[...] = _pack_bf16_pairs(_dot(act, w2b_ref[...]))


def _gexperts_call(xs, tile_expert, tile_valid, n_used, w1, w3, w2):
    p, half = xs.shape
    ne, d, de = w1.shape
    nt = p // MOE_TM

    def rows(i, te, tv, nu):
        return (jnp.minimum(i, nu[0] - 1), 0)

    def wsel(i, te, tv, nu):
        return (te[i], 0, 0)

    return pl.pallas_call(
        _gexperts_kernel,
        grid_spec=pltpu.PrefetchScalarGridSpec(
            num_scalar_prefetch=3,
            grid=(nt,),
            in_specs=[
                pl.BlockSpec((MOE_TM, half), rows),
                pl.BlockSpec((1, d, de), wsel),
                pl.BlockSpec((1, d, de), wsel),
                pl.BlockSpec((1, de, d), wsel),
            ],
            out_specs=pl.BlockSpec((MOE_TM, half), rows),
            scratch_shapes=[pltpu.VMEM((d, de), BF16), pltpu.VMEM((d, de), BF16), pltpu.VMEM((de, d), BF16)],
        ),
        out_shape=jax.ShapeDtypeStruct((p, half), jnp.int32),
        compiler_params=_cparams(("arbitrary",)),
        name="moe_experts",
    )(tile_expert, tile_valid, n_used, xs, w1, w3, w2)


def _combine_kernel(y0_ref, y1_ref, wts_ref, x_ref, gate_ref, fg_ref, o_ref, *, final_norm):
    wts = wts_ref[...]
    moe = wts[:, 0:1] * _unpack_bf16_pairs(y0_ref[...]) + wts[:, 1:2] * _unpack_bf16_pairs(y1_ref[...])
    xn = x_ref[...] + gate_ref[0] * moe
    if final_norm:
        xn = xn * lax.rsqrt(jnp.mean(xn * xn, axis=-1, keepdims=True) + NORM_EPS) * fg_ref[...]
    o_ref[...] = xn


def _combine_call(yg, wts, x2, mod3, final_g, seq, gate_blk, final_norm, tm=512):
    n, d = x2.shape
    tpb = seq // tm
    slot1 = n // tm
    return pl.pallas_call(
        functools.partial(_combine_kernel, final_norm=final_norm),
        grid=(n // tm,),
        in_specs=[
            pl.BlockSpec((tm, d // 2), lambda i: (i, 0)),
            pl.BlockSpec((tm, d // 2), lambda i: (i + slot1, 0)),
            pl.BlockSpec((tm, LANES), lambda i: (i, 0)),
            pl.BlockSpec((tm, d), lambda i: (i, 0)),
            pl.BlockSpec((1, 1, d), lambda i: (i // tpb, 0, gate_blk)),
            pl.BlockSpec((1, d), lambda i: (0, 0)),
        ],
        out_specs=pl.BlockSpec((tm, d), lambda i: (i, 0)),
        out_shape=jax.ShapeDtypeStruct((n, d), F32),
        compiler_params=_cparams(("parallel",)),
        name="moe_combine",
    )(yg, yg, wts, x2, mod3, final_g.reshape(1, d))


def _pos_kernel(eid_ref, ts_ref, p0_ref, p1_ref):
    eid = eid_ref[...]
    tm = eid.shape[0]
    lane = lax.broadcasted_iota(jnp.int32, (tm, LANES), 1)
    sub = lax.broadcasted_iota(jnp.int32, (tm, LANES), 0) % LANES
    for slot, out_ref in ((0, p0_ref), (1, p1_ref)):
        first_row = jnp.sum(jnp.where(lane == eid[:, slot:slot + 1], ts_ref[...], 0), axis=-1, keepdims=True)
        pos = first_row + eid[:, slot + 2:slot + 3]
        out_ref[...] = jnp.sum(jnp.where(lane == sub, pos, 0).reshape(tm // LANES, LANES, LANES), axis=1)


def _pos_call(eid, first_rows, tm=1024):
    n = eid.shape[0]
    out = jax.ShapeDtypeStruct((n // LANES, LANES), jnp.int32)
    p0, p1 = pl.pallas_call(
        _pos_kernel,
        grid=(n // tm,),
        in_specs=[pl.BlockSpec((tm, LANES), lambda i: (i, 0)), pl.BlockSpec((1, LANES), lambda i: (0, 0))],
        out_specs=[pl.BlockSpec((tm // LANES, LANES), lambda i: (i, 0))] * 2,
        out_shape=[out, out],
        compiler_params=_cparams(("parallel",)),
        name="moe_positions",
    )(eid, first_rows)
    return p0.reshape(n), p1.reshape(n)


def _moe_plan(eid, counts_f):
    n = eid.shape[0]
    nt = (2 * n) // MOE_TM + N_EXPERTS
    counts = counts_f[0, :N_EXPERTS].astype(jnp.int32)
    tiles = (counts + MOE_TM - 1) // MOE_TM
    tile_end = jnp.cumsum(tiles)
    tile_start = tile_end - tiles
    n_used = tile_end[-1:]
    tile_iota = jnp.arange(nt, dtype=jnp.int32)
    tile_expert = jnp.sum(jnp.minimum(tile_iota, n_used - 1)[:, None] >= tile_end[None, :], axis=1, dtype=jnp.int32)
    tile_valid = jnp.clip(counts[tile_expert] - (tile_iota - tile_start[tile_expert]) * MOE_TM, 0, MOE_TM)
    first_rows = jnp.pad(tile_start * MOE_TM, (0, LANES - N_EXPERTS)).reshape(1, LANES)
    pos0, pos1 = _pos_call(eid, first_rows)
    return pos0, pos1, tile_expert, tile_valid, n_used


def kernel(x, c, positions, ada_w, ada_b, norm1_g, norm2_g, w_in, hg_lb_table, hg_norm_w, rw_mu, rw_w0, rw_w2,
           rw_a0, rw_a2, rw_g2, rw_k_k, rw_k_a, rw_r_k, rw_ln_w, rw_ln_b, br_hg, br_ret, br_rw, w_out,
           router_g, router_e, moe_w1, moe_w3, moe_w2, final_g):
    b, t, d = x.shape
    depth = ada_w.shape[0]
    n = b * t
    assert w_in.shape[2] == IN_COLS and d == 1024

    lb_p = jax.nn.softmax(hg_lb_table.astype(F32), axis=0)
    lower_bounds = jnp.cumsum(lb_p, axis=0) - lb_p[0]

    mod = _mod_call(c, ada_w, ada_b)
    cos2, sin2 = _rope_call(positions, RET_DK)
    w_perm = _wprep_call(w_in)
    x2 = x.reshape(n, d)
    for l in range(depth):
        mod3 = mod[l].reshape(b, 1, 6 * d)
        z2 = _inproj_call(x2, norm1_g[l], mod3, w_perm, l, t, scale_blk=1, shift_blk=0)
        z3 = z2.reshape(b, t, IN_COLS)
        o_hg = _hgrn2_call(z3, lower_bounds[l], hg_norm_w[l])
        o_ret = _ret_call(z3, cos2, sin2)
        o_rw = _rwkv_call(z3, rw_mu[l], rw_w0[l], rw_w2[l], rw_a0[l], rw_a2[l], rw_g2[l], rw_k_k[l],
                          rw_k_a[l], rw_r_k[l], rw_ln_w[l], rw_ln_b[l])
        x2 = _merge_call(o_hg.reshape(n, HG_W), o_ret.reshape(n, RET_W), o_rw.reshape(n, RW_W), z2, x2, mod3,
                         br_hg[l].astype(BF16), br_ret[l].astype(BF16), br_rw[l].astype(BF16),
                         w_out[l].astype(BF16), t, gate_blk=2)
        hp, eid, wts, counts = _route_call(x2, norm2_g[l], mod3, router_g[l], router_e[l], t, scale_blk=4,
                                           shift_blk=3)
        pos0, pos1, tile_expert, tile_valid, n_used = _moe_plan(eid, counts)
        xs = _sc_scatter2(hp, pos0, pos1, (2 * n // MOE_TM + N_EXPERTS) * MOE_TM)
        ys = _gexperts_call(xs, tile_expert + l * N_EXPERTS, tile_valid, n_used,
                            moe_w1.reshape((-1,) + moe_w1.shape[2:]), moe_w3.reshape((-1,) + moe_w3.shape[2:]),
                            moe_w2.reshape((-1,) + moe_w2.shape[2:]))
        yg = _sc_gather(ys, jnp.concatenate([pos0, pos1]))
        x2 = _combine_call(yg, wts, x2, mod3, final_g, t, gate_blk=5, final_norm=(l == depth - 1))
    return x2.reshape(b, t, d)
```

```python
import functools

import jax
import jax.numpy as jnp
from jax import lax
from jax.experimental import pallas as pl
from jax.experimental.pallas import tpu as pltpu
from jax.experimental.pallas import tpu_sc as plsc

F32 = jnp.float32
BF16 = jnp.bfloat16
HIGHEST = lax.Precision.HIGHEST

HG_HEADS = 4
HG_DK = 128
HG_W = HG_HEADS * HG_DK
RET_HEADS = 4
RET_DK = 128
RET_W = RET_HEADS * RET_DK
RW_HEADS = 8
RW_N = 64
RW_W = RW_HEADS * RW_N
RW_DECAY_LORA = 64
RW_A_LORA = 64
RW_GATE_LORA = 128
RW_COLS = 3 * RW_W + RW_DECAY_LORA + RW_A_LORA + RW_GATE_LORA
RW_GN_EPS = 64e-5
N_GROUPS = 4
EXPERTS_PER_GROUP = 8
N_EXPERTS = N_GROUPS * EXPERTS_PER_GROUP
ROPE_THETA = 10000.0
NORM_EPS = 1e-6

LANES = 128
LOG2E = 1.4426950408889634
VMEM_LIMIT = 56 * 1024 * 1024

GATE_OFF = 0
HG_OFF = 3 * 1024
RET_OFF = HG_OFF + 4 * HG_W
RW_OFF = RET_OFF + 4 * RET_W
IN_COLS = RW_OFF + RW_COLS

HG_CHUNK = 64
HG_SUB = 16
HG_SAFE_SPAN = 60.0
RW_CHUNK = 64
RW_BLK = 16
RW_TB = 256
Z_DTYPE = BF16


def _cparams(sem):
    return pltpu.CompilerParams(dimension_semantics=sem, vmem_limit_bytes=VMEM_LIMIT)


def _dot(a, b, precision=None):
    return jnp.dot(a, b, preferred_element_type=F32, precision=precision)


def _dot_nt(a, b, precision=None):
    return lax.dot_general(a, b, (((1,), (1,)), ((), ())), preferred_element_type=F32, precision=precision)


def _dot_tn(a, b, precision=None):
    return lax.dot_general(a, b, (((0,), (0,)), ((), ())), preferred_element_type=F32, precision=precision)


def _split_bf16(x):
    hi = x.astype(BF16)
    return hi, (x - hi.astype(F32)).astype(BF16)


def _dot_x3(a, b):
    ah, al = _split_bf16(a)
    bh, bl = _split_bf16(b)
    return _dot(ah, bh) + _dot(ah, bl) + _dot(al, bh)


def _dot_x2_lhs(a, b_exact):
    ah, al = _split_bf16(a)
    return _dot(ah, b_exact) + _dot(al, b_exact)


def _dot_x2_rhs(a_exact, b):
    bh, bl = _split_bf16(b)
    return _dot(a_exact, bh) + _dot(a_exact, bl)


def _bdot(a, b):
    return _dot(a.astype(BF16), b.astype(BF16))


def _sigmoid(x):
    return 0.5 * jnp.tanh(0.5 * x) + 0.5


def _silu(x):
    return x * _sigmoid(x)


def _rms_mod(x, gain, scale, shift):
    y = x * lax.rsqrt(jnp.mean(x * x, axis=-1, keepdims=True) + NORM_EPS)
    return (y * gain) * (1.0 + scale) + shift


def _mod_kernel(c_ref, w_ref, b_ref, o_ref):
    c = c_ref[...]
    o_ref[0] = _dot(_silu(c), w_ref[0], HIGHEST) + b_ref[0]


def _mod_call(c, ada_w, ada_b):
    depth, d, d6 = ada_w.shape
    b = c.shape[0]
    nblk = d6 // d
    return pl.pallas_call(
        _mod_kernel,
        grid=(depth, nblk),
        in_specs=[
            pl.BlockSpec((b, d), lambda l, j: (0, 0)),
            pl.BlockSpec((1, d, d), lambda l, j: (l, 0, j)),
            pl.BlockSpec((1, 1, d), lambda l, j: (l, 0, j)),
        ],
        out_specs=pl.BlockSpec((1, b, d), lambda l, j: (l, 0, j)),
        out_shape=jax.ShapeDtypeStruct((depth, b, d6), F32),
        compiler_params=_cparams(("parallel", "parallel")),
        name="adaln_mod",
    )(c, ada_w, ada_b.reshape(depth, 1, d6))


def _rope_kernel(pos_ref, freq_ref, sign_ref, cos_ref, sin_ref):
    ang = pos_ref[0].astype(F32) * freq_ref[...]
    cos_ref[0] = jnp.cos(ang)
    sin_ref[0] = jnp.sin(ang) * sign_ref[...]


def _rope_call(positions, d):
    b, t = positions.shape
    tb = min(t, 512)
    inv_freq = ROPE_THETA ** (-jnp.arange(0, d, 2, dtype=F32) / d)
    freq2 = jnp.concatenate([inv_freq, inv_freq]).reshape(1, d)
    sign2 = jnp.concatenate([-jnp.ones((d // 2,), F32), jnp.ones((d // 2,), F32)]).reshape(1, d)
    out = jax.ShapeDtypeStruct((b, t, d), F32)
    return pl.pallas_call(
        _rope_kernel,
        grid=(b, t // tb),
        in_specs=[
            pl.BlockSpec((1, tb, 1), lambda i, j: (i, j, 0)),
            pl.BlockSpec((1, d), lambda i, j: (0, 0)),
            pl.BlockSpec((1, d), lambda i, j: (0, 0)),
        ],
        out_specs=[pl.BlockSpec((1, tb, d), lambda i, j: (i, j, 0))] * 2,
        out_shape=[out, out],
        compiler_params=_cparams(("parallel", "parallel")),
        name="rope_tables",
    )(positions.reshape(b, t, 1), freq2, sign2)


W_BLK = 256


def _wprep_kernel(w_ref, o_ref):
    o_ref[...] = w_ref[...].astype(o_ref.dtype)


def _wprep_call(w_in):
    depth, d, cols = w_in.shape
    nblk = cols // W_BLK
    first = (cols - 3 * d) // W_BLK
    return pl.pallas_call(
        _wprep_kernel,
        grid=(depth, nblk),
        in_specs=[pl.BlockSpec((1, d, W_BLK), lambda l, j: (l, 0, (j + first) % nblk))],
        out_specs=pl.BlockSpec((1, d, W_BLK), lambda l, j: (l, 0, j)),
        out_shape=jax.ShapeDtypeStruct((depth, d, cols), BF16),
        compiler_params=_cparams(("parallel", "parallel")),
        name="w_in_layout",
    )(w_in)


def _inproj_kernel(x_ref, g_ref, scale_ref, shift_ref, w_ref, o_ref, h_ref):
    @pl.when(pl.program_id(1) == 0)
    def _():
        h = _rms_mod(x_ref[...], g_ref[...], scale_ref[0], shift_ref[0])
        h_ref[...] = h.astype(BF16)

    o_ref[...] = _dot(h_ref[...], w_ref[0]).astype(o_ref.dtype)


def _inproj_call(x2, gain, mod3, w_bf16, layer, seq, scale_blk, shift_blk, tm=2048, tn=1792):
    n, d = x2.shape
    cols = w_bf16.shape[2]
    tpb = seq // tm
    return pl.pallas_call(
        _inproj_kernel,
        grid=(n // tm, cols // tn),
        in_specs=[
            pl.BlockSpec((tm, d), lambda i, j: (i, 0)),
            pl.BlockSpec((1, d), lambda i, j: (0, 0)),
            pl.BlockSpec((1, 1, d), lambda i, j: (i // tpb, 0, scale_blk)),
            pl.BlockSpec((1, 1, d), lambda i, j: (i // tpb, 0, shift_blk)),
            pl.BlockSpec((1, d, tn), lambda i, j: (layer, 0, j)),
        ],
        out_specs=pl.BlockSpec((tm, tn), lambda i, j: (i, j)),
        out_shape=jax.ShapeDtypeStruct((n, cols), Z_DTYPE),
        scratch_shapes=[pltpu.VMEM((tm, d), BF16)],
        compiler_params=_cparams(("parallel", "arbitrary")),
        name="norm_inproj",
    )(x2, gain.reshape(1, d), mod3, mod3, w_bf16)


def _hgrn2_block(zs, lbs, nw, sts, factored):
    hs = range(len(zs))
    tb = zs[0][0].shape[0]
    c, sub = HG_CHUNK, HG_SUB
    nc, ns, nb = tb // c, c // sub, tb // sub
    f = [lbs[h] + (1.0 - lbs[h]) * _sigmoid(zs[h][1]) for h in hs]
    logf = [jnp.log(jnp.maximum(f[h], 1e-30)) for h in hs]
    q = [_silu(zs[h][0]) * (HG_DK ** -0.5) for h in hs]
    k = [1.0 - f[h] for h in hs]
    v = [zs[h][2] for h in hs]
    v_b = [v[h].astype(BF16) for h in hs]
    row = lax.broadcasted_iota(jnp.int32, (tb, tb), 0)
    col = lax.broadcasted_iota(jnp.int32, (tb, tb), 1)
    tri = jnp.where(col >= (row // c) * c, jnp.where(row >= col, 1.0, 0.0), 0.0).astype(BF16)
    cum = [_dot_x2_rhs(tri, logf[h]) for h in hs]
    cum3 = [cum[h].reshape(nb, sub, HG_DK) for h in hs]
    ref3 = [cum3[h][:, 0:1, :] - logf[h].reshape(nb, sub, HG_DK)[:, 0:1, :] for h in hs]
    span = functools.reduce(jnp.maximum, [jnp.max(ref3[h] - cum3[h][:, sub - 1:sub, :]) for h in hs])
    qe = [(q[h] * jnp.exp(cum[h])).astype(BF16) for h in hs]

    offd = [(h, ci * c, ci * c + sub * i) for h in hs for ci in range(nc) for i in range(1, ns)]
    base = [cum[h][lo - 1:lo] for h, _, lo in offd]
    qt = [(q[h][lo:lo + sub] * jnp.exp(cum[h][lo:lo + sub] - base[j])).astype(BF16)
          for j, (h, _, lo) in enumerate(offd)]
    kt = [(k[h][r0:lo] * jnp.exp(base[j] - cum[h][r0:lo])).astype(BF16) for j, (h, r0, lo) in enumerate(offd)]
    a = [_dot_nt(qt[j], kt[j]).astype(BF16) for j in range(len(offd))]
    av = {(h, lo): _dot(a[j], v_b[h][r0:lo]) for j, (h, r0, lo) in enumerate(offd)}

    cs = [slice(ci * c, (ci + 1) * c) for ci in range(nc)]
    hc = [(h, ci) for h in hs for ci in range(nc)]
    last = {(h, ci): cum[h][(ci + 1) * c - 1:(ci + 1) * c] for h, ci in hc}
    kd = {(h, ci): (k[h][cs[ci]] * jnp.exp(last[h, ci] - cum[h][cs[ci]])).astype(BF16) for h, ci in hc}
    inc = {(h, ci): _dot_tn(v_b[h][cs[ci]], kd[h, ci]) for h, ci in hc}
    s_in = {(h, 0): sts[h] for h in hs}
    for ci in range(nc):
        for h in hs:
            s_in[h, ci + 1] = s_in[h, ci] * jnp.exp(last[h, ci]) + inc[h, ci]
    o_inter = {(h, ci): _dot_nt(qe[h][cs[ci]], s_in[h, ci].astype(BF16)) for h, ci in hc}

    if factored:
        qf = [(q[h] * jnp.exp(cum3[h] - ref3[h]).reshape(tb, HG_DK)).astype(BF16) for h in hs]
        kf = [(k[h] * jnp.exp(ref3[h] - cum3[h]).reshape(tb, HG_DK)).astype(BF16) for h in hs]
        rc = lax.broadcasted_iota(jnp.int32, (c, c), 0)
        cc = lax.broadcasted_iota(jnp.int32, (c, c), 1)
        keep = (rc >= cc) & (rc // sub == cc // sub)
        a_d = {(h, ci): jnp.where(keep, _dot_nt(qf[h][cs[ci]], kf[h][cs[ci]]), 0.0).astype(BF16) for h, ci in hc}
        dg = {(h, ci): _dot(a_d[h, ci], v_b[h][cs[ci]]) for h, ci in hc}
        diag = [jnp.concatenate([dg[h, ci] for ci in range(nc)], axis=0) for h in hs]
    else:
        gb = 4
        trow = lax.broadcasted_iota(jnp.int32, (gb, sub, HG_DK), 1)
        diag = []
        for h in hs:
            c2 = cum[h] * LOG2E
            ks2 = c2 - jnp.log2(k[h])
            parts = []
            for g0 in range(0, nb, gb):
                rws = slice(g0 * sub, (g0 + gb) * sub)
                c23, ks23, q3, v3 = (x[rws].reshape(gb, sub, HG_DK) for x in (c2, ks2, q[h], v[h]))
                acc = jnp.zeros((gb, sub, HG_DK), F32)
                for s in range(sub):
                    e = jnp.exp2(jnp.where(trow >= s, c23 - ks23[:, s:s + 1, :], -jnp.inf))
                    a_col = jnp.sum(q3 * e, axis=-1, keepdims=True)
                    acc = acc + a_col * v3[:, s:s + 1, :]
                parts.append(acc.reshape(gb * sub, HG_DK))
            diag.append(jnp.concatenate(parts, axis=0))

    outs = []
    for h in hs:
        pieces = []
        for ci in range(nc):
            for i in range(ns):
                lo = ci * c + sub * i
                piece = o_inter[h, ci][sub * i:sub * (i + 1)] + diag[h][lo:lo + sub]
                pieces.append(piece + av[h, lo] if i > 0 else piece)
        o = jnp.concatenate(pieces, axis=0)
        o = o * lax.rsqrt(jnp.mean(o * o, axis=-1, keepdims=True) + NORM_EPS)
        outs.append(o * nw * _silu(zs[h][3]))
    return outs, [s_in[h, nc] for h in hs], span


def _hgrn2_kernel(zq_ref, zf_ref, zi_ref, zg_ref, lb_ref, nw_ref, o_ref, st_ref):
    @pl.when(pl.program_id(1) == 0)
    def _():
        st_ref[...] = jnp.zeros_like(st_ref)

    hs = range(HG_HEADS)
    sl = [slice(h * HG_DK, (h + 1) * HG_DK) for h in hs]

    def run(factored):
        zs = [tuple(r[0, :, sl[h]].astype(F32) for r in (zq_ref, zf_ref, zi_ref, zg_ref)) for h in hs]
        outs, sts, span = _hgrn2_block(zs, [lb_ref[:, sl[h]] for h in hs], nw_ref[...],
                                       [st_ref[h] for h in hs], factored)
        return jnp.concatenate(outs, axis=1), sts, span

    st_old = [st_ref[h] for h in hs]
    o, st_new, span = run(True)
    for h in hs:
        st_ref[h] = st_new[h]
    o_ref[0] = o.astype(o_ref.dtype)

    @pl.when(span > HG_SAFE_SPAN)
    def _():
        for h in hs:
            st_ref[h] = st_old[h]
        o2, st2, _ = run(False)
        for h in hs:
            st_ref[h] = st2[h]
        o_ref[0] = o2.astype(o_ref.dtype)


def _hgrn2_call(z3, lower_bound, norm_w, tb=256):
    b, t, _ = z3.shape
    tb = min(tb, t)
    base = HG_OFF // HG_W

    def zspec(part):
        return pl.BlockSpec((1, tb, HG_W), lambda i, j: (i, j, base + part))

    return pl.pallas_call(
        _hgrn2_kernel,
        grid=(b, t // tb),
        in_specs=[
            zspec(0), zspec(1), zspec(2), zspec(3),
            pl.BlockSpec((1, HG_W), lambda i, j: (0, 0)),
            pl.BlockSpec((1, LANES), lambda i, j: (0, 0)),
        ],
        out_specs=pl.BlockSpec((1, tb, HG_W), lambda i, j: (i, j, 0)),
        out_shape=jax.ShapeDtypeStruct((b, t, HG_W), BF16),
        scratch_shapes=[pltpu.VMEM((HG_HEADS, HG_DK, HG_DK), F32)],
        compiler_params=_cparams(("parallel", "arbitrary")),
        name="hgrn2_mixer",
    )(z3, z3, z3, z3, lower_bound.reshape(1, HG_W), norm_w.reshape(1, HG_DK))


def _ret_kernel(zq_ref, zk_ref, zv_ref, zg_ref, cos_ref, sin_ref, o_ref, st_ref, dmask_ref, *, chunk):
    hs = range(RET_HEADS)
    sl = [slice(h * RET_DK, (h + 1) * RET_DK) for h in hs]
    lg = [jnp.log(jnp.full((1, 1), 1.0 - 2.0 ** (-5.0 - h), F32)) for h in hs]

    @pl.when(pl.program_id(1) == 0)
    def _():
        st_ref[...] = jnp.zeros_like(st_ref)
        row = lax.broadcasted_iota(jnp.int32, (chunk, chunk), 0)
        col = lax.broadcasted_iota(jnp.int32, (chunk, chunk), 1)
        rel = (row - col).astype(F32)
        for h in hs:
            dmask_ref[h] = jnp.where(rel >= 0.0, jnp.exp(jnp.maximum(rel, 0.0) * lg[h]), 0.0)

    cos2 = cos_ref[0]
    sin2 = sin_ref[0]
    half = RET_DK // 2

    def rope(z):
        return z * cos2 + pltpu.roll(z, half, 1) * sin2

    tcol = lax.broadcasted_iota(jnp.int32, (chunk, 1), 0).astype(F32)
    q = [rope(zq_ref[0, :, sl[h]].astype(F32)) * (RET_DK ** -0.5) for h in hs]
    k = [rope(zk_ref[0, :, sl[h]].astype(F32)) for h in hs]
    v_b = [zv_ref[0, :, sl[h]].astype(BF16) for h in hs]
    st = [st_ref[h] for h in hs]
    scores = [(_dot_nt(q[h].astype(BF16), k[h].astype(BF16)) * dmask_ref[h]).astype(BF16) for h in hs]
    qx = [(q[h] * jnp.exp((tcol + 1.0) * lg[h])).astype(BF16) for h in hs]
    kz = [(k[h] * jnp.exp((chunk - 1.0 - tcol) * lg[h])).astype(BF16) for h in hs]
    o = [_dot(scores[h], v_b[h]) + _dot_nt(qx[h], st[h].astype(BF16)) for h in hs]
    for h in hs:
        st_ref[h] = st[h] * jnp.exp(chunk * lg[h]) + _dot_tn(v_b[h], kz[h])
    o = [o[h] * lax.rsqrt(jnp.mean(o[h] * o[h], axis=-1, keepdims=True) + NORM_EPS) for h in hs]
    o_ref[0] = (jnp.concatenate(o, axis=1) * _silu(zg_ref[0].astype(F32))).astype(o_ref.dtype)


def _ret_call(z3, cos2, sin2, chunk=256):
    b, t, _ = z3.shape
    chunk = min(chunk, t)
    base = RET_OFF // RET_W

    def zspec(part):
        return pl.BlockSpec((1, chunk, RET_W), lambda i, j: (i, j, base + part))

    tab = pl.BlockSpec((1, chunk, RET_DK), lambda i, j: (i, j, 0))
    return pl.pallas_call(
        functools.partial(_ret_kernel, chunk=chunk),
        grid=(b, t // chunk),
        in_specs=[zspec(0), zspec(1), zspec(2), zspec(3), tab, tab],
        out_specs=pl.BlockSpec((1, chunk, RET_W), lambda i, j: (i, j, 0)),
        out_shape=jax.ShapeDtypeStruct((b, t, RET_W), BF16),
        scratch_shapes=[pltpu.VMEM((RET_HEADS, RET_DK, RET_DK), F32), pltpu.VMEM((RET_HEADS, chunk, chunk), F32)],
        compiler_params=_cparams(("parallel", "arbitrary")),
        name="retention_mixer",
    )(z3, z3, z3, z3, cos2, sin2)


def _inv_unit_lower(a, eye, blk_mask):
    c = a[0].shape[0]
    m = range(len(a))
    a_bd = [jnp.where(blk_mask, a[i], 0.0) for i in m]
    a_off = [a[i] - a_bd[i] for i in m]
    a2 = [_bdot(a_bd[i], a_bd[i]) for i in m]
    p = [eye + a_bd[i] for i in m]
    r = [_bdot(jnp.concatenate([p[i], a2[i]], axis=0), a2[i]) for i in m]
    p = [p[i] + r[i][:c] for i in m]
    a4 = [r[i][c:] for i in m]
    r = [_bdot(jnp.concatenate([p[i], a4[i]], axis=0), a4[i]) for i in m]
    p = [p[i] + r[i][:c] for i in m]
    a8 = [r[i][c:] for i in m]
    t_bd = [p[i] + _bdot(p[i], a8[i]) for i in m]
    n = [_bdot(t_bd[i], a_off[i]) for i in m]
    r = [_bdot(n[i], jnp.concatenate([n[i], t_bd[i]], axis=1)) for i in m]
    z = [t_bd[i] + r[i][:, c:] for i in m]
    return [z[i] + _bdot(r[i][:, :c], z[i]) for i in m]


def _rwkv_kernel(z_ref, mu_ref, w0_ref, w2_ref, a0_ref, a2_ref, g2_ref, kk_ref, ka_ref, rk_ref,
                 lnw_ref, lnb_ref, seg_ref, o_ref, s_ref, prev_ref):
    c = RW_CHUNK
    tb = z_ref.shape[1]
    nck = tb // c

    @pl.when(pl.program_id(1) == 0)
    def _():
        s_ref[...] = jnp.zeros_like(s_ref)
        prev_ref[...] = jnp.zeros_like(prev_ref)

    z = z_ref[0].astype(F32)
    rows = lax.broadcasted_iota(jnp.int32, (tb, 1), 0)
    z_prev = jnp.where(rows == 0, prev_ref[...], pltpu.roll(z, 1, 0))
    prev_ref[...] = z[tb - 1:tb]
    zs = z + mu_ref[...] * (z_prev - z)
    r = zs[:, 0:RW_W]
    k = zs[:, RW_W:2 * RW_W]
    v = zs[:, 2 * RW_W:3 * RW_W]
    off = 3 * RW_W
    w_lo = zs[:, off:off + RW_DECAY_LORA]
    a_lo = zs[:, off + RW_DECAY_LORA:off + RW_DECAY_LORA + RW_A_LORA]
    g_lo = zs[:, off + RW_DECAY_LORA + RW_A_LORA:]

    wx = -(w0_ref[...] + _dot_x3(jnp.tanh(w_lo), w2_ref[...]))
    softplus = jnp.maximum(wx, 0.0) + jnp.log(1.0 + jnp.exp(-jnp.abs(wx)))
    logw = -jnp.exp(-softplus - 0.5)
    a = _sigmoid(a0_ref[...] + _dot_x3(a_lo, a2_ref[...]))
    g = _dot_x3(_sigmoid(g_lo), g2_ref[...])
    seg = seg_ref[...]
    kk = k * kk_ref[...]
    kk = kk * lax.rsqrt(jnp.maximum(_dot_x2_lhs(kk * kk, seg), 1e-24))
    k2 = k * (1.0 + (a - 1.0) * ka_ref[...])

    row = lax.broadcasted_iota(jnp.int32, (c, c), 0)
    col = lax.broadcasted_iota(jnp.int32, (c, c), 1)
    blk_mask = (row // RW_BLK) == (col // RW_BLK)
    eye = (row == col).astype(F32)
    row2 = lax.broadcasted_iota(jnp.int32, (c, 2 * c), 0)
    col2 = lax.broadcasted_iota(jnp.int32, (c, 2 * c), 1) % c
    incl2 = row2 >= col2
    strict2 = row2 > col2
    rowb = lax.broadcasted_iota(jnp.int32, (tb, tb), 0)
    colb = lax.broadcasted_iota(jnp.int32, (tb, tb), 1)
    tri = jnp.where(colb >= (rowb // c) * c, jnp.where(rowb >= colb, 1.0, 0.0), 0.0).astype(BF16)
    cw = _dot_x2_rhs(tri, logw)
    w_inv = jnp.exp(-cw)
    last = jnp.concatenate([jnp.broadcast_to(cw[(ci + 1) * c - 1:(ci + 1) * c], (c, RW_W)) for ci in range(nck)],
                           axis=0)
    w_rest = jnp.exp(last - cw)
    beta = a * kk
    alpha_t = -kk * jnp.exp(cw - logw)
    r_t = r * jnp.exp(cw)
    beta_h = beta * w_inv
    k_h = k2 * w_inv
    beta_d = beta * w_rest
    k_d = k2 * w_rest

    hs = range(RW_HEADS)
    ph = [(ci, h) for ci in range(nck) for h in hs]
    m = range(len(ph))
    rs = [slice(ci * c, (ci + 1) * c) for ci, _ in ph]
    sl = [slice(h * RW_N, (h + 1) * RW_N) for _, h in ph]
    v_h = [v[rs[i], sl[i]] for i in m]
    lhs = [jnp.concatenate([alpha_t[rs[i], sl[i]], r_t[rs[i], sl[i]]], axis=0).astype(BF16) for i in m]
    rhs = [jnp.concatenate([beta_h[rs[i], sl[i]], k_h[rs[i], sl[i]]], axis=0).astype(BF16) for i in m]
    big = [_dot_nt(lhs[i], rhs[i]) for i in m]
    a_a = [jnp.where(strict2, big[i][:c], 0.0) for i in m]
    a_r = [jnp.where(incl2, big[i][c:], 0.0).astype(BF16) for i in m]
    t_inv = _inv_unit_lower([a_a[i][:, :c] for i in m], eye, blk_mask)
    av = [_bdot(a_a[i][:, c:], v_h[i]) for i in m]
    bk_d = [jnp.concatenate([beta_d[rs[i], sl[i]], k_d[rs[i], sl[i]]], axis=0).astype(BF16) for i in m]
    s_cur = [s_ref[h] for h in hs]
    o_chunks = []
    for ci in range(nck):
        ix = [ci * RW_HEADS + h for h in hs]
        sd = [_dot_nt(lhs[ix[h]], s_cur[h].astype(BF16)) for h in hs]
        u = [_bdot(t_inv[ix[h]], sd[h][:c] + av[ix[h]]) for h in hs]
        uv = [jnp.concatenate([u[h], v_h[ix[h]]], axis=0).astype(BF16) for h in hs]
        o_chunks.append(jnp.concatenate([sd[h][c:] + _dot(a_r[ix[h]], uv[h]) for h in hs], axis=1))
        w_last = jnp.exp(cw[(ci + 1) * c - 1:(ci + 1) * c])
        s_cur = [s_cur[h] * w_last[:, sl[h]] + _dot_tn(uv[h], bk_d[ix[h]]) for h in hs]
    for h in hs:
        s_ref[h] = s_cur[h]
    o = jnp.concatenate(o_chunks, axis=0)

    mean = _dot_x2_lhs(o, seg) * (1.0 / RW_N)
    dev = o - mean
    var = _dot_x2_lhs(dev * dev, seg) * (1.0 / RW_N)
    o = dev * lax.rsqrt(var + RW_GN_EPS) * lnw_ref[...] + lnb_ref[...]
    bonus = _dot_x2_lhs(r * k2 * rk_ref[...], seg) * v
    o_ref[0] = ((o + bonus) * g).astype(o_ref.dtype)


def _rwkv_call(z3, mu, w0, w2, a0, a2, g2, k_k, k_a, r_k, ln_w, ln_b):
    b, t, _ = z3.shape
    c = min(RW_TB, t)
    hid = lax.broadcasted_iota(jnp.int32, (RW_W, RW_W), 0) // RW_N
    seg = (hid == hid.T).astype(BF16)

    def vec(n):
        return pl.BlockSpec((1, n), lambda i, j: (0, 0))

    def mat(m, n):
        return pl.BlockSpec((m, n), lambda i, j: (0, 0))

    return pl.pallas_call(
        _rwkv_kernel,
        grid=(b, t // c),
        in_specs=[
            pl.BlockSpec((1, c, RW_COLS), lambda i, j: (i, j, RW_OFF // RW_COLS)),
            vec(RW_COLS), vec(RW_W), mat(RW_DECAY_LORA, RW_W), vec(RW_W), mat(RW_A_LORA, RW_W),
            mat(RW_GATE_LORA, RW_W), vec(RW_W), vec(RW_W), vec(RW_W), vec(RW_W), vec(RW_W),
            mat(RW_W, RW_W),
        ],
        out_specs=pl.BlockSpec((1, c, RW_W), lambda i, j: (i, j, 0)),
        out_shape=jax.ShapeDtypeStruct((b, t, RW_W), BF16),
        scratch_shapes=[pltpu.VMEM((RW_HEADS, RW_N, RW_N), F32), pltpu.VMEM((1, RW_COLS), F32)],
        compiler_params=_cparams(("parallel", "arbitrary")),
        name="rwkv7_mixer",
    )(z3, mu.reshape(1, -1), w0.reshape(1, -1), w2, a0.reshape(1, -1), a2, g2, k_k.reshape(1, -1),
      k_a.reshape(1, -1), r_k.reshape(1, -1), ln_w.reshape(1, -1), ln_b.reshape(1, -1), seg)


def _merge_kernel(ohg_ref, oret_ref, orw_ref, zg_ref, x_ref, gate_ref, bhg_ref, bret_ref, brw_ref,
                  wout_ref, o_ref):
    d = x_ref.shape[1]
    y = _sigmoid(zg_ref[:, 0:d].astype(F32)) * _dot(ohg_ref[...], bhg_ref[...])
    y = y + _sigmoid(zg_ref[:, d:2 * d].astype(F32)) * _dot(oret_ref[...], bret_ref[...])
    y = y + _sigmoid(zg_ref[:, 2 * d:3 * d].astype(F32)) * _dot(orw_ref[...], brw_ref[...])
    o_ref[...] = x_ref[...] + gate_ref[0] * _dot(y.astype(BF16), wout_ref[...])


def _merge_call(o_hg, o_ret, o_rw, z2, x2, mod3, br_hg, br_ret, br_rw, w_out, seq, gate_blk, tm=512):
    n, d = x2.shape
    tpb = seq // tm

    def rows(w):
        return pl.BlockSpec((tm, w), lambda i: (i, 0))

    def full(m, k):
        return pl.BlockSpec((m, k), lambda i: (0, 0))

    return pl.pallas_call(
        _merge_kernel,
        grid=(n // tm,),
        in_specs=[
            rows(HG_W), rows(RET_W), rows(RW_W), rows(3 * d), rows(d),
            pl.BlockSpec((1, 1, d), lambda i: (i // tpb, 0, gate_blk)),
            full(HG_W, d), full(RET_W, d), full(RW_W, d), full(d, d),
        ],
        out_specs=rows(d),
        out_shape=jax.ShapeDtypeStruct((n, d), F32),
        compiler_params=_cparams(("parallel",)),
        name="merge_outproj",
    )(o_hg, o_ret, o_rw, z2, x2, mod3, br_hg, br_ret, br_rw, w_out)


def _pack_bf16_pairs(x):
    w = x.shape[1] // 2
    hi = pltpu.bitcast(x[:, :w].astype(BF16).astype(F32), jnp.uint32)
    lo = pltpu.bitcast(x[:, w:].astype(BF16).astype(F32), jnp.uint32)
    return pltpu.bitcast(hi | lax.shift_right_logical(lo, jnp.uint32(16)), jnp.int32)


def _unpack_bf16_pairs(p):
    u = pltpu.bitcast(p, jnp.uint32)
    hi = pltpu.bitcast(u & jnp.uint32(0xFFFF0000), F32)
    lo = pltpu.bitcast(lax.shift_left(u, jnp.uint32(16)), F32)
    return jnp.concatenate([hi, lo], axis=1)


def _route_kernel(x_ref, g_ref, scale_ref, shift_ref, rc_ref, hp_ref, eid_ref, wts_ref, cnt_ref):
    @pl.when(pl.program_id(0) == 0)
    def _():
        cnt_ref[...] = jnp.zeros_like(cnt_ref)

    h = _rms_mod(x_ref[...], g_ref[...], scale_ref[0], shift_ref[0])
    hp_ref[...] = _pack_bf16_pairs(h)
    tm = h.shape[0]
    lane = lax.broadcasted_iota(jnp.int32, (tm, LANES), 1)
    neg = -jnp.inf
    logits = _dot_x3(h, rc_ref[...])
    gl = jnp.where(lane < N_GROUPS, logits, neg)
    gmax = jnp.max(gl, axis=-1, keepdims=True)
    gidx = jnp.min(jnp.where(gl == gmax, lane, LANES), axis=-1, keepdims=True)
    gw = 1.0 / jnp.sum(jnp.exp(gl - gmax), axis=-1, keepdims=True)
    lo = N_GROUPS + gidx * EXPERTS_PER_GROUP
    el = jnp.where(lane >= lo, jnp.where(lane < lo + EXPERTS_PER_GROUP, logits, neg), neg)
    m1 = jnp.max(el, axis=-1, keepdims=True)
    l1 = jnp.min(jnp.where(el == m1, lane, LANES), axis=-1, keepdims=True)
    el2 = jnp.where(lane == l1, neg, el)
    m2 = jnp.max(el2, axis=-1, keepdims=True)
    l2 = jnp.min(jnp.where(el2 == m2, lane, LANES), axis=-1, keepdims=True)
    i1 = l1 - N_GROUPS
    i2 = l2 - N_GROUPS
    e2 = jnp.exp(m2 - m1)
    p1 = 1.0 / (1.0 + e2)
    p2 = e2 * p1
    oh1 = jnp.where(lane == i1, 1.0, 0.0)
    oh2 = jnp.where(lane == i2, 1.0, 0.0)
    row = lax.broadcasted_iota(jnp.int32, (tm, tm), 0)
    col = lax.broadcasted_iota(jnp.int32, (tm, tm), 1)
    earlier = jnp.where(row > col, 1.0, 0.0).astype(BF16)
    before = _dot(earlier, jnp.concatenate([oh1, oh2], axis=1).astype(BF16))
    tot1 = jnp.sum(oh1, axis=0, keepdims=True)
    carry = cnt_ref[...]
    r1 = jnp.sum(oh1 * (before[:, :LANES] + carry), axis=-1, keepdims=True).astype(jnp.int32)
    r2 = jnp.sum(oh2 * (before[:, LANES:] + (carry + tot1)), axis=-1, keepdims=True).astype(jnp.int32)
    cnt_ref[...] = carry + tot1 + jnp.sum(oh2, axis=0, keepdims=True)
    eid_ref[...] = jnp.where(lane == 0, i1, jnp.where(lane == 1, i2, jnp.where(lane == 2, r1,
                                                                             jnp.where(lane == 3, r2, 0))))
    wts_ref[...] = jnp.where(lane == 0, gw * p1, jnp.where(lane == 1, gw * p2, 0.0))


def _route_call(x2, gain, mod3, router_g, router_e, seq, scale_blk, shift_blk, tm=512):
    n, d = x2.shape
    tpb = seq // tm
    rc = jnp.pad(jnp.concatenate([router_g, router_e], axis=1), ((0, 0), (0, LANES - N_GROUPS - N_EXPERTS)))
    return pl.pallas_call(
        _route_kernel,
        grid=(n // tm,),
        in_specs=[
            pl.BlockSpec((tm, d), lambda i: (i, 0)),
            pl.BlockSpec((1, d), lambda i: (0, 0)),
            pl.BlockSpec((1, 1, d), lambda i: (i // tpb, 0, scale_blk)),
            pl.BlockSpec((1, 1, d), lambda i: (i // tpb, 0, shift_blk)),
            pl.BlockSpec((d, LANES), lambda i: (0, 0)),
        ],
        out_specs=[pl.BlockSpec((tm, d // 2), lambda i: (i, 0)), pl.BlockSpec((tm, LANES), lambda i: (i, 0)),
                   pl.BlockSpec((tm, LANES), lambda i: (i, 0)), pl.BlockSpec((1, LANES), lambda i: (0, 0))],
        out_shape=[jax.ShapeDtypeStruct((n, d // 2), jnp.int32), jax.ShapeDtypeStruct((n, LANES), jnp.int32),
                   jax.ShapeDtypeStruct((n, LANES), F32), jax.ShapeDtypeStruct((1, LANES), F32)],
        compiler_params=_cparams(("arbitrary",)),
        name="moe_route",
    )(x2, gain.reshape(1, d), mod3, mod3, rc)


SC_CORES = 2
SC_SUBCORES = 16
SC_WORKERS = SC_CORES * SC_SUBCORES
SC_ROWS = 32
SC_STREAMS = 4


def _sc_gather(table, idx):
    m = idx.shape[0]
    w = table.shape[1]
    per_worker = m // SC_WORKERS
    steps = per_worker // SC_ROWS
    assert per_worker * SC_WORKERS == m and steps * SC_ROWS == per_worker and steps % SC_STREAMS == 0
    mesh = plsc.VectorSubcoreMesh(core_axis_name="c", subcore_axis_name="s")
    ks = range(SC_STREAMS)

    def body(table_hbm, idx_hbm, out_hbm, idx_v, *rest):
        bufs, g_sems, w_sems = rest[:SC_STREAMS], rest[SC_STREAMS:2 * SC_STREAMS], rest[2 * SC_STREAMS:]
        wid = lax.axis_index("s") * SC_CORES + lax.axis_index("c")
        pltpu.sync_copy(idx_hbm.at[wid], idx_v)

        @pl.loop(0, steps, step=SC_STREAMS)
        def _(j):
            row0 = wid * per_worker + j * SC_ROWS
            gathers = [pltpu.async_copy(table_hbm.at[idx_v.at[j + q]], bufs[q], g_sems[q]) for q in ks]
            writes = []
            for q in ks:
                gathers[q].wait()
                writes.append(pltpu.async_copy(bufs[q], out_hbm.at[pl.ds(row0 + q * SC_ROWS, SC_ROWS)], w_sems[q]))
            for q in ks:
                writes[q].wait()

    return pl.kernel(
        body,
        out_type=jax.ShapeDtypeStruct((m, w), table.dtype),
        mesh=mesh,
        scratch_types=[pltpu.VMEM((steps, SC_ROWS), jnp.int32)] + [pltpu.VMEM((SC_ROWS, w), table.dtype)] * SC_STREAMS
        + [pltpu.SemaphoreType.DMA] * (2 * SC_STREAMS),
        name="sc_row_gather",
    )(table, idx.reshape(SC_WORKERS, steps, SC_ROWS))


def _sc_scatter2(rows, idx0, idx1, p):
    n, w = rows.shape
    per_worker = n // SC_WORKERS
    steps = per_worker // SC_ROWS
    assert per_worker * SC_WORKERS == n and steps * SC_ROWS == per_worker and steps % SC_STREAMS == 0
    mesh = plsc.VectorSubcoreMesh(core_axis_name="c", subcore_axis_name="s")
    ks = range(SC_STREAMS)

    def body(rows_hbm, i0_hbm, i1_hbm, out_hbm, i0_v, i1_v, *rest):
        bufs, r_sems = rest[:SC_STREAMS], rest[SC_STREAMS:2 * SC_STREAMS]
        s0_sems, s1_sems = rest[2 * SC_STREAMS:3 * SC_STREAMS], rest[3 * SC_STREAMS:]
        wid = lax.axis_index("s") * SC_CORES + lax.axis_index("c")
        pltpu.sync_copy(i0_hbm.at[wid], i0_v)
        pltpu.sync_copy(i1_hbm.at[wid], i1_v)

        @pl.loop(0, steps, step=SC_STREAMS)
        def _(j):
            row0 = wid * per_worker + j * SC_ROWS
            reads = [pltpu.async_copy(rows_hbm.at[pl.ds(row0 + q * SC_ROWS, SC_ROWS)], bufs[q], r_sems[q]) for q in ks]
            writes = []
            for q in ks:
                reads[q].wait()
                writes.append(pltpu.async_copy(bufs[q], out_hbm.at[i0_v.at[j + q]], s0_sems[q]))
                writes.append(pltpu.async_copy(bufs[q], out_hbm.at[i1_v.at[j + q]], s1_sems[q]))
            for wr in writes:
                wr.wait()

    index_block = pltpu.VMEM((steps, SC_ROWS), jnp.int32)
    return pl.kernel(
        body,
        out_type=jax.ShapeDtypeStruct((p, w), rows.dtype),
        mesh=mesh,
        scratch_types=[index_block, index_block] + [pltpu.VMEM((SC_ROWS, w), rows.dtype)] * SC_STREAMS
        + [pltpu.SemaphoreType.DMA] * (3 * SC_STREAMS),
        name="sc_row_scatter",
    )(rows, idx0.reshape(SC_WORKERS, steps, SC_ROWS), idx1.reshape(SC_WORKERS, steps, SC_ROWS))


MOE_TM = 512


def _gexperts_kernel(te_ref, tv_ref, nu_ref, xs_ref, w1_ref, w3_ref, w2_ref, ys_ref, w1b_ref, w3b_ref, w2b_ref):
    i = pl.program_id(0)

    @pl.when((i == 0) | (te_ref[i] != te_ref[jnp.maximum(i - 1, 0)]))
    def _():
        w1b_ref[...] = w1_ref[0].astype(BF16)
        w3b_ref[...] = w3_ref[0].astype(BF16)
        w2b_ref[...] = w2_ref[0].astype(BF16)

    @pl.when(i < nu_ref[0])
    def _():
        rid = lax.broadcasted_iota(jnp.int32, xs_ref.shape, 0)
        xb = _unpack_bf16_pairs(jnp.where(rid < tv_ref[i], xs_ref[...], 0)).astype(BF16)
        act = (_silu(_dot(xb, w1b_ref[...])) * _dot(xb, w3b_ref[...])).astype(BF16)
        ys_ref[...] = _pack_bf16_pairs(_dot(act, w2b_ref[...]))


def _gexperts_call(xs, tile_expert, tile_valid, n_used, w1, w3, w2):
    p, half = xs.shape
    ne, d, de = w1.shape
    nt = p // MOE_TM

    def rows(i, te, tv, nu):
        return (jnp.minimum(i, nu[0] - 1), 0)

    def wsel(i, te, tv, nu):
        return (te[i], 0, 0)

    return pl.pallas_call(
        _gexperts_kernel,
        grid_spec=pltpu.PrefetchScalarGridSpec(
            num_scalar_prefetch=3,
            grid=(nt,),
            in_specs=[
                pl.BlockSpec((MOE_TM, half), rows),
                pl.BlockSpec((1, d, de), wsel),
                pl.BlockSpec((1, d, de), wsel),
                pl.BlockSpec((1, de, d), wsel),
            ],
            out_specs=pl.BlockSpec((MOE_TM, half), rows),
            scratch_shapes=[pltpu.VMEM((d, de), BF16), pltpu.VMEM((d, de), BF16), pltpu.VMEM((de, d), BF16)],
        ),
        out_shape=jax.ShapeDtypeStruct((p, half), jnp.int32),
        compiler_params=_cparams(("arbitrary",)),
        name="moe_experts",
    )(tile_expert, tile_valid, n_used, xs, w1, w3, w2)


def _combine_kernel(y0_ref, y1_ref, wts_ref, x_ref, gate_ref, fg_ref, o_ref, *, final_norm):
    wts = wts_ref[...]
    moe = wts[:, 0:1] * _unpack_bf16_pairs(y0_ref[...]) + wts[:, 1:2] * _unpack_bf16_pairs(y1_ref[...])
    xn = x_ref[...] + gate_ref[0] * moe
    if final_norm:
        xn = xn * lax.rsqrt(jnp.mean(xn * xn, axis=-1, keepdims=True) + NORM_EPS) * fg_ref[...]
    o_ref[...] = xn


def _combine_call(yg, wts, x2, mod3, final_g, seq, gate_blk, final_norm, tm=512):
    n, d = x2.shape
    tpb = seq // tm
    slot1 = n // tm
    return pl.pallas_call(
        functools.partial(_combine_kernel, final_norm=final_norm),
        grid=(n // tm,),
        in_specs=[
            pl.BlockSpec((tm, d // 2), lambda i: (i, 0)),
            pl.BlockSpec((tm, d // 2), lambda i: (i + slot1, 0)),
            pl.BlockSpec((tm, LANES), lambda i: (i, 0)),
            pl.BlockSpec((tm, d), lambda i: (i, 0)),
            pl.BlockSpec((1, 1, d), lambda i: (i // tpb, 0, gate_blk)),
            pl.BlockSpec((1, d), lambda i: (0, 0)),
        ],
        out_specs=pl.BlockSpec((tm, d), lambda i: (i, 0)),
        out_shape=jax.ShapeDtypeStruct((n, d), F32),
        compiler_params=_cparams(("parallel",)),
        name="moe_combine",
    )(yg, yg, wts, x2, mod3, final_g.reshape(1, d))


def _pos_kernel(eid_ref, ts_ref, p0_ref, p1_ref):
    eid = eid_ref[...]
    tm = eid.shape[0]
    lane = lax.broadcasted_iota(jnp.int32, (tm, LANES), 1)
    sub = lax.broadcasted_iota(jnp.int32, (tm, LANES), 0) % LANES
    for slot, out_ref in ((0, p0_ref), (1, p1_ref)):
        first_row = jnp.sum(jnp.where(lane == eid[:, slot:slot + 1], ts_ref[...], 0), axis=-1, keepdims=True)
        pos = first_row + eid[:, slot + 2:slot + 3]
        out_ref[...] = jnp.sum(jnp.where(lane == sub, pos, 0).reshape(tm // LANES, LANES, LANES), axis=1)


def _pos_call(eid, first_rows, tm=1024):
    n = eid.shape[0]
    out = jax.ShapeDtypeStruct((n // LANES, LANES), jnp.int32)
    p0, p1 = pl.pallas_call(
        _pos_kernel,
        grid=(n // tm,),
        in_specs=[pl.BlockSpec((tm, LANES), lambda i: (i, 0)), pl.BlockSpec((1, LANES), lambda i: (0, 0))],
        out_specs=[pl.BlockSpec((tm // LANES, LANES), lambda i: (i, 0))] * 2,
        out_shape=[out, out],
        compiler_params=_cparams(("parallel",)),
        name="moe_positions",
    )(eid, first_rows)
    return p0.reshape(n), p1.reshape(n)


def _moe_plan(eid, counts_f):
    n = eid.shape[0]
    nt = (2 * n) // MOE_TM + N_EXPERTS
    counts = counts_f[0, :N_EXPERTS].astype(jnp.int32)
    tiles = (counts + MOE_TM - 1) // MOE_TM
    tile_end = jnp.cumsum(tiles)
    tile_start = tile_end - tiles
    n_used = tile_end[-1:]
    tile_iota = jnp.arange(nt, dtype=jnp.int32)
    tile_expert = jnp.sum(jnp.minimum(tile_iota, n_used - 1)[:, None] >= tile_end[None, :], axis=1, dtype=jnp.int32)
    own = tile_expert[:, None] == jnp.arange(N_EXPERTS, dtype=jnp.int32)[None, :]
    count_t = jnp.sum(jnp.where(own, counts[None, :], 0), axis=1)
    start_t = jnp.sum(jnp.where(own, tile_start[None, :], 0), axis=1)
    tile_valid = jnp.clip(count_t - (tile_iota - start_t) * MOE_TM, 0, MOE_TM)
    first_rows = jnp.pad(tile_start * MOE_TM, (0, LANES - N_EXPERTS)).reshape(1, LANES)
    pos0, pos1 = _pos_call(eid, first_rows)
    return pos0, pos1, tile_expert, tile_valid, n_used


def kernel(x, c, positions, ada_w, ada_b, norm1_g, norm2_g, w_in, hg_lb_table, hg_norm_w, rw_mu, rw_w0, rw_w2,
           rw_a0, rw_a2, rw_g2, rw_k_k, rw_k_a, rw_r_k, rw_ln_w, rw_ln_b, br_hg, br_ret, br_rw, w_out,
           router_g, router_e, moe_w1, moe_w3, moe_w2, final_g):
    b, t, d = x.shape
    depth = ada_w.shape[0]
    n = b * t
    assert w_in.shape[2] == IN_COLS and d == 1024

    lb_p = jax.nn.softmax(hg_lb_table.astype(F32), axis=0)
    lower_bounds = jnp.cumsum(lb_p, axis=0) - lb_p[0]

    mod = _mod_call(c, ada_w, ada_b)
    cos2, sin2 = _rope_call(positions, RET_DK)
    w_perm = _wprep_call(w_in)
    x2 = x.reshape(n, d)
    for l in range(depth):
        mod3 = mod[l].reshape(b, 1, 6 * d)
        z2 = _inproj_call(x2, norm1_g[l], mod3, w_perm, l, t, scale_blk=1, shift_blk=0)
        z3 = z2.reshape(b, t, IN_COLS)
        o_hg = _hgrn2_call(z3, lower_bounds[l], hg_norm_w[l])
        o_ret = _ret_call(z3, cos2, sin2)
        o_rw = _rwkv_call(z3, rw_mu[l], rw_w0[l], rw_w2[l], rw_a0[l], rw_a2[l], rw_g2[l], rw_k_k[l],
                          rw_k_a[l], rw_r_k[l], rw_ln_w[l], rw_ln_b[l])
        x2 = _merge_call(o_hg.reshape(n, HG_W), o_ret.reshape(n, RET_W), o_rw.reshape(n, RW_W), z2, x2, mod3,
                         br_hg[l].astype(BF16), br_ret[l].astype(BF16), br_rw[l].astype(BF16),
                         w_out[l].astype(BF16), t, gate_blk=2)
        hp, eid, wts, counts = _route_call(x2, norm2_g[l], mod3, router_g[l], router_e[l], t, scale_blk=4,
                                           shift_blk=3)
        pos0, pos1, tile_expert, tile_valid, n_used = _moe_plan(eid, counts)
        xs = _sc_scatter2(hp, pos0, pos1, (2 * n // MOE_TM + N_EXPERTS) * MOE_TM)
        ys = _gexperts_call(xs, tile_expert + l * N_EXPERTS, tile_valid, n_used,
                            moe_w1.reshape((-1,) + moe_w1.shape[2:]), moe_w3.reshape((-1,) + moe_w3.shape[2:]),
                            moe_w2.reshape((-1,) + moe_w2.shape[2:]))
        yg = _sc_gather(ys, jnp.concatenate([pos0, pos1]))
        x2 = _combine_call(yg, wts, x2, mod3, final_g, t, gate_blk=5, final_norm=(l == depth - 1))
    return x2.reshape(b, t, d)
```

```python
import functools

import jax
import jax.numpy as jnp
from jax import lax
from jax.experimental import pallas as pl
from jax.experimental.pallas import tpu as pltpu
from jax.experimental.pallas import tpu_sc as plsc

F32 = jnp.float32
BF16 = jnp.bfloat16
HIGHEST = lax.Precision.HIGHEST

HG_HEADS = 4
HG_DK = 128
HG_W = HG_HEADS * HG_DK
RET_HEADS = 4
RET_DK = 128
RET_W = RET_HEADS * RET_DK
RW_HEADS = 8
RW_N = 64
RW_W = RW_HEADS * RW_N
RW_DECAY_LORA = 64
RW_A_LORA = 64
RW_GATE_LORA = 128
RW_COLS = 3 * RW_W + RW_DECAY_LORA + RW_A_LORA + RW_GATE_LORA
RW_GN_EPS = 64e-5
N_GROUPS = 4
EXPERTS_PER_GROUP = 8
N_EXPERTS = N_GROUPS * EXPERTS_PER_GROUP
ROPE_THETA = 10000.0
NORM_EPS = 1e-6

LANES = 128
LOG2E = 1.4426950408889634
VMEM_LIMIT = 56 * 1024 * 1024

GATE_OFF = 0
HG_OFF = 3 * 1024
RET_OFF = HG_OFF + 4 * HG_W
RW_OFF = RET_OFF + 4 * RET_W
IN_COLS = RW_OFF + RW_COLS

HG_CHUNK = 64
HG_SUB = 16
HG_SAFE_SPAN = 60.0
RW_CHUNK = 64
RW_BLK = 16
RW_TB = 256
Z_DTYPE = BF16


def _cparams(sem):
    return pltpu.CompilerParams(dimension_semantics=sem, vmem_limit_bytes=VMEM_LIMIT)


def _dot(a, b, precision=None):
    return jnp.dot(a, b, preferred_element_type=F32, precision=precision)


def _dot_nt(a, b, precision=None):
    return lax.dot_general(a, b, (((1,), (1,)), ((), ())), preferred_element_type=F32, precision=precision)


def _dot_tn(a, b, precision=None):
    return lax.dot_general(a, b, (((0,), (0,)), ((), ())), preferred_element_type=F32, precision=precision)


def _split_bf16(x):
    hi = x.astype(BF16)
    return hi, (x - hi.astype(F32)).astype(BF16)


def _dot_x3(a, b):
    ah, al = _split_bf16(a)
    bh, bl = _split_bf16(b)
    return _dot(ah, bh) + _dot(ah, bl) + _dot(al, bh)


def _dot_x2_lhs(a, b_exact):
    ah, al = _split_bf16(a)
    return _dot(ah, b_exact) + _dot(al, b_exact)


def _dot_x2_rhs(a_exact, b):
    bh, bl = _split_bf16(b)
    return _dot(a_exact, bh) + _dot(a_exact, bl)


def _bdot(a, b):
    return _dot(a.astype(BF16), b.astype(BF16))


def _sigmoid(x):
    return 0.5 * jnp.tanh(0.5 * x) + 0.5


def _silu(x):
    return x * _sigmoid(x)


def _rms_mod(x, gain, scale, shift):
    y = x * lax.rsqrt(jnp.mean(x * x, axis=-1, keepdims=True) + NORM_EPS)
    return (y * gain) * (1.0 + scale) + shift


def _mod_kernel(c_ref, w_ref, b_ref, o_ref):
    c = c_ref[...]
    o_ref[0] = _dot(_silu(c), w_ref[0], HIGHEST) + b_ref[0]


def _mod_call(c, ada_w, ada_b):
    depth, d, d6 = ada_w.shape
    b = c.shape[0]
    nblk = d6 // d
    return pl.pallas_call(
        _mod_kernel,
        grid=(depth, nblk),
        in_specs=[
            pl.BlockSpec((b, d), lambda l, j: (0, 0)),
            pl.BlockSpec((1, d, d), lambda l, j: (l, 0, j)),
            pl.BlockSpec((1, 1, d), lambda l, j: (l, 0, j)),
        ],
        out_specs=pl.BlockSpec((1, b, d), lambda l, j: (l, 0, j)),
        out_shape=jax.ShapeDtypeStruct((depth, b, d6), F32),
        compiler_params=_cparams(("parallel", "parallel")),
        name="adaln_mod",
    )(c, ada_w, ada_b.reshape(depth, 1, d6))


def _rope_kernel(pos_ref, freq_ref, sign_ref, cos_ref, sin_ref):
    ang = pos_ref[0].astype(F32) * freq_ref[...]
    cos_ref[0] = jnp.cos(ang)
    sin_ref[0] = jnp.sin(ang) * sign_ref[...]


def _rope_call(positions, d):
    b, t = positions.shape
    tb = min(t, 512)
    inv_freq = ROPE_THETA ** (-jnp.arange(0, d, 2, dtype=F32) / d)
    freq2 = jnp.concatenate([inv_freq, inv_freq]).reshape(1, d)
    sign2 = jnp.concatenate([-jnp.ones((d // 2,), F32), jnp.ones((d // 2,), F32)]).reshape(1, d)
    out = jax.ShapeDtypeStruct((b, t, d), F32)
    return pl.pallas_call(
        _rope_kernel,
        grid=(b, t // tb),
        in_specs=[
            pl.BlockSpec((1, tb, 1), lambda i, j: (i, j, 0)),
            pl.BlockSpec((1, d), lambda i, j: (0, 0)),
            pl.BlockSpec((1, d), lambda i, j: (0, 0)),
        ],
        out_specs=[pl.BlockSpec((1, tb, d), lambda i, j: (i, j, 0))] * 2,
        out_shape=[out, out],
        compiler_params=_cparams(("parallel", "parallel")),
        name="rope_tables",
    )(positions.reshape(b, t, 1), freq2, sign2)


W_BLK = 256


def _wprep_kernel(w_ref, o_ref):
    o_ref[...] = w_ref[...].astype(o_ref.dtype)


def _wprep_call(w_in):
    depth, d, cols = w_in.shape
    nblk = cols // W_BLK
    first = (cols - 3 * d) // W_BLK
    return pl.pallas_call(
        _wprep_kernel,
        grid=(depth, nblk),
        in_specs=[pl.BlockSpec((1, d, W_BLK), lambda l, j: (l, 0, (j + first) % nblk))],
        out_specs=pl.BlockSpec((1, d, W_BLK), lambda l, j: (l, 0, j)),
        out_shape=jax.ShapeDtypeStruct((depth, d, cols), BF16),
        compiler_params=_cparams(("parallel", "parallel")),
        name="w_in_layout",
    )(w_in)


def _inproj_kernel(x_ref, g_ref, scale_ref, shift_ref, w_ref, o_ref, h_ref):
    @pl.when(pl.program_id(1) == 0)
    def _():
        h = _rms_mod(x_ref[...], g_ref[...], scale_ref[0], shift_ref[0])
        h_ref[...] = h.astype(BF16)

    o_ref[...] = _dot(h_ref[...], w_ref[0]).astype(o_ref.dtype)


def _inproj_call(x2, gain, mod3, w_bf16, layer, seq, scale_blk, shift_blk, tm=2048, tn=1792):
    n, d = x2.shape
    cols = w_bf16.shape[2]
    tpb = seq // tm
    return pl.pallas_call(
        _inproj_kernel,
        grid=(n // tm, cols // tn),
        in_specs=[
            pl.BlockSpec((tm, d), lambda i, j: (i, 0)),
            pl.BlockSpec((1, d), lambda i, j: (0, 0)),
            pl.BlockSpec((1, 1, d), lambda i, j: (i // tpb, 0, scale_blk)),
            pl.BlockSpec((1, 1, d), lambda i, j: (i // tpb, 0, shift_blk)),
            pl.BlockSpec((1, d, tn), lambda i, j: (layer, 0, j)),
        ],
        out_specs=pl.BlockSpec((tm, tn), lambda i, j: (i, j)),
        out_shape=jax.ShapeDtypeStruct((n, cols), Z_DTYPE),
        scratch_shapes=[pltpu.VMEM((tm, d), BF16)],
        compiler_params=_cparams(("parallel", "arbitrary")),
        name="norm_inproj",
    )(x2, gain.reshape(1, d), mod3, mod3, w_bf16)


def _hgrn2_block(zs, lbs, nw, sts, factored):
    hs = range(len(zs))
    tb = zs[0][0].shape[0]
    c, sub = HG_CHUNK, HG_SUB
    nc, ns, nb = tb // c, c // sub, tb // sub
    f = [lbs[h] + (1.0 - lbs[h]) * _sigmoid(zs[h][1]) for h in hs]
    logf = [jnp.log(jnp.maximum(f[h], 1e-30)) for h in hs]
    q = [_silu(zs[h][0]) * (HG_DK ** -0.5) for h in hs]
    k = [1.0 - f[h] for h in hs]
    v = [zs[h][2] for h in hs]
    v_b = [v[h].astype(BF16) for h in hs]
    row = lax.broadcasted_iota(jnp.int32, (tb, tb), 0)
    col = lax.broadcasted_iota(jnp.int32, (tb, tb), 1)
    tri = jnp.where(col >= (row // c) * c, jnp.where(row >= col, 1.0, 0.0), 0.0).astype(BF16)
    cum = [_dot_x2_rhs(tri, logf[h]) for h in hs]
    cum3 = [cum[h].reshape(nb, sub, HG_DK) for h in hs]
    ref3 = [cum3[h][:, 0:1, :] - logf[h].reshape(nb, sub, HG_DK)[:, 0:1, :] for h in hs]
    span = functools.reduce(jnp.maximum, [jnp.max(ref3[h] - cum3[h][:, sub - 1:sub, :]) for h in hs])
    qe = [(q[h] * jnp.exp(cum[h])).astype(BF16) for h in hs]

    offd = [(h, ci * c, ci * c + sub * i) for h in hs for ci in range(nc) for i in range(1, ns)]
    base = [cum[h][lo - 1:lo] for h, _, lo in offd]
    qt = [(q[h][lo:lo + sub] * jnp.exp(cum[h][lo:lo + sub] - base[j])).astype(BF16)
          for j, (h, _, lo) in enumerate(offd)]
    kt = [(k[h][r0:lo] * jnp.exp(base[j] - cum[h][r0:lo])).astype(BF16) for j, (h, r0, lo) in enumerate(offd)]
    a = [_dot_nt(qt[j], kt[j]).astype(BF16) for j in range(len(offd))]
    av = {(h, lo): _dot(a[j], v_b[h][r0:lo]) for j, (h, r0, lo) in enumerate(offd)}

    cs = [slice(ci * c, (ci + 1) * c) for ci in range(nc)]
    hc = [(h, ci) for h in hs for ci in range(nc)]
    last = {(h, ci): cum[h][(ci + 1) * c - 1:(ci + 1) * c] for h, ci in hc}
    kd = {(h, ci): (k[h][cs[ci]] * jnp.exp(last[h, ci] - cum[h][cs[ci]])).astype(BF16) for h, ci in hc}
    inc = {(h, ci): _dot_tn(v_b[h][cs[ci]], kd[h, ci]) for h, ci in hc}
    s_in = {(h, 0): sts[h] for h in hs}
    for ci in range(nc):
        for h in hs:
            s_in[h, ci + 1] = s_in[h, ci] * jnp.exp(last[h, ci]) + inc[h, ci]
    o_inter = {(h, ci): _dot_nt(qe[h][cs[ci]], s_in[h, ci].astype(BF16)) for h, ci in hc}

    if factored:
        qf = [(q[h] * jnp.exp(cum3[h] - ref3[h]).reshape(tb, HG_DK)).astype(BF16) for h in hs]
        kf = [(k[h] * jnp.exp(ref3[h] - cum3[h]).reshape(tb, HG_DK)).astype(BF16) for h in hs]
        rc = lax.broadcasted_iota(jnp.int32, (c, c), 0)
        cc = lax.broadcasted_iota(jnp.int32, (c, c), 1)
        keep = (rc >= cc) & (rc // sub == cc // sub)
        a_d = {(h, ci): jnp.where(keep, _dot_nt(qf[h][cs[ci]], kf[h][cs[ci]]), 0.0).astype(BF16) for h, ci in hc}
        dg = {(h, ci): _dot(a_d[h, ci], v_b[h][cs[ci]]) for h, ci in hc}
        diag = [jnp.concatenate([dg[h, ci] for ci in range(nc)], axis=0) for h in hs]
    else:
        gb = 4
        trow = lax.broadcasted_iota(jnp.int32, (gb, sub, HG_DK), 1)
        diag = []
        for h in hs:
            c2 = cum[h] * LOG2E
            ks2 = c2 - jnp.log2(k[h])
            parts = []
            for g0 in range(0, nb, gb):
                rws = slice(g0 * sub, (g0 + gb) * sub)
                c23, ks23, q3, v3 = (x[rws].reshape(gb, sub, HG_DK) for x in (c2, ks2, q[h], v[h]))
                acc = jnp.zeros((gb, sub, HG_DK), F32)
                for s in range(sub):
                    e = jnp.exp2(jnp.where(trow >= s, c23 - ks23[:, s:s + 1, :], -jnp.inf))
                    a_col = jnp.sum(q3 * e, axis=-1, keepdims=True)
                    acc = acc + a_col * v3[:, s:s + 1, :]
                parts.append(acc.reshape(gb * sub, HG_DK))
            diag.append(jnp.concatenate(parts, axis=0))

    outs = []
    for h in hs:
        pieces = []
        for ci in range(nc):
            for i in range(ns):
                lo = ci * c + sub * i
                piece = o_inter[h, ci][sub * i:sub * (i + 1)] + diag[h][lo:lo + sub]
                pieces.append(piece + av[h, lo] if i > 0 else piece)
        o = jnp.concatenate(pieces, axis=0)
        o = o * lax.rsqrt(jnp.mean(o * o, axis=-1, keepdims=True) + NORM_EPS)
        outs.append(o * nw * _silu(zs[h][3]))
    return outs, [s_in[h, nc] for h in hs], span


def _hgrn2_kernel(zq_ref, zf_ref, zi_ref, zg_ref, lb_ref, nw_ref, o_ref, st_ref):
    @pl.when(pl.program_id(1) == 0)
    def _():
        st_ref[...] = jnp.zeros_like(st_ref)

    hs = range(HG_HEADS)
    sl = [slice(h * HG_DK, (h + 1) * HG_DK) for h in hs]

    def run(factored):
        zs = [tuple(r[0, :, sl[h]].astype(F32) for r in (zq_ref, zf_ref, zi_ref, zg_ref)) for h in hs]
        outs, sts, span = _hgrn2_block(zs, [lb_ref[:, sl[h]] for h in hs], nw_ref[...],
                                       [st_ref[h] for h in hs], factored)
        return jnp.concatenate(outs, axis=1), sts, span

    st_old = [st_ref[h] for h in hs]
    o, st_new, span = run(True)
    for h in hs:
        st_ref[h] = st_new[h]
    o_ref[0] = o.astype(o_ref.dtype)

    @pl.when(span > HG_SAFE_SPAN)
    def _():
        for h in hs:
            st_ref[h] = st_old[h]
        o2, st2, _ = run(False)
        for h in hs:
            st_ref[h] = st2[h]
        o_ref[0] = o2.astype(o_ref.dtype)


def _hgrn2_call(z3, lower_bound, norm_w, tb=256):
    b, t, _ = z3.shape
    tb = min(tb, t)
    base = HG_OFF // HG_W

    def zspec(part):
        return pl.BlockSpec((1, tb, HG_W), lambda i, j: (i, j, base + part))

    return pl.pallas_call(
        _hgrn2_kernel,
        grid=(b, t // tb),
        in_specs=[
            zspec(0), zspec(1), zspec(2), zspec(3),
            pl.BlockSpec((1, HG_W), lambda i, j: (0, 0)),
            pl.BlockSpec((1, LANES), lambda i, j: (0, 0)),
        ],
        out_specs=pl.BlockSpec((1, tb, HG_W), lambda i, j: (i, j, 0)),
        out_shape=jax.ShapeDtypeStruct((b, t, HG_W), BF16),
        scratch_shapes=[pltpu.VMEM((HG_HEADS, HG_DK, HG_DK), F32)],
        compiler_params=_cparams(("parallel", "arbitrary")),
        name="hgrn2_mixer",
    )(z3, z3, z3, z3, lower_bound.reshape(1, HG_W), norm_w.reshape(1, HG_DK))


def _ret_kernel(zq_ref, zk_ref, zv_ref, zg_ref, cos_ref, sin_ref, o_ref, st_ref, dmask_ref, *, chunk):
    hs = range(RET_HEADS)
    sl = [slice(h * RET_DK, (h + 1) * RET_DK) for h in hs]
    lg = [jnp.log(jnp.full((1, 1), 1.0 - 2.0 ** (-5.0 - h), F32)) for h in hs]

    @pl.when(pl.program_id(1) == 0)
    def _():
        st_ref[...] = jnp.zeros_like(st_ref)
        row = lax.broadcasted_iota(jnp.int32, (chunk, chunk), 0)
        col = lax.broadcasted_iota(jnp.int32, (chunk, chunk), 1)
        rel = (row - col).astype(F32)
        for h in hs:
            dmask_ref[h] = jnp.where(rel >= 0.0, jnp.exp(jnp.maximum(rel, 0.0) * lg[h]), 0.0)

    cos2 = cos_ref[0]
    sin2 = sin_ref[0]
    half = RET_DK // 2

    def rope(z):
        return z * cos2 + pltpu.roll(z, half, 1) * sin2

    tcol = lax.broadcasted_iota(jnp.int32, (chunk, 1), 0).astype(F32)
    q = [rope(zq_ref[0, :, sl[h]].astype(F32)) * (RET_DK ** -0.5) for h in hs]
    k = [rope(zk_ref[0, :, sl[h]].astype(F32)) for h in hs]
    v_b = [zv_ref[0, :, sl[h]].astype(BF16) for h in hs]
    st = [st_ref[h] for h in hs]
    scores = [(_dot_nt(q[h].astype(BF16), k[h].astype(BF16)) * dmask_ref[h]).astype(BF16) for h in hs]
    qx = [(q[h] * jnp.exp((tcol + 1.0) * lg[h])).astype(BF16) for h in hs]
    kz = [(k[h] * jnp.exp((chunk - 1.0 - tcol) * lg[h])).astype(BF16) for h in hs]
    o = [_dot(scores[h], v_b[h]) + _dot_nt(qx[h], st[h].astype(BF16)) for h in hs]
    for h in hs:
        st_ref[h] = st[h] * jnp.exp(chunk * lg[h]) + _dot_tn(v_b[h], kz[h])
    o = [o[h] * lax.rsqrt(jnp.mean(o[h] * o[h], axis=-1, keepdims=True) + NORM_EPS) for h in hs]
    o_ref[0] = (jnp.concatenate(o, axis=1) * _silu(zg_ref[0].astype(F32))).astype(o_ref.dtype)


def _ret_call(z3, cos2, sin2, chunk=256):
    b, t, _ = z3.shape
    chunk = min(chunk, t)
    base = RET_OFF // RET_W

    def zspec(part):
        return pl.BlockSpec((1, chunk, RET_W), lambda i, j: (i, j, base + part))

    tab = pl.BlockSpec((1, chunk, RET_DK), lambda i, j: (i, j, 0))
    return pl.pallas_call(
        functools.partial(_ret_kernel, chunk=chunk),
        grid=(b, t // chunk),
        in_specs=[zspec(0), zspec(1), zspec(2), zspec(3), tab, tab],
        out_specs=pl.BlockSpec((1, chunk, RET_W), lambda i, j: (i, j, 0)),
        out_shape=jax.ShapeDtypeStruct((b, t, RET_W), BF16),
        scratch_shapes=[pltpu.VMEM((RET_HEADS, RET_DK, RET_DK), F32), pltpu.VMEM((RET_HEADS, chunk, chunk), F32)],
        compiler_params=_cparams(("parallel", "arbitrary")),
        name="retention_mixer",
    )(z3, z3, z3, z3, cos2, sin2)


def _pair_blockdiag(y, pair_mask):
    return jnp.where(pair_mask, jnp.concatenate([y, y], axis=0), 0.0).astype(BF16)


def _pair_dot(x, y, pair_mask):
    return _dot(x.astype(BF16), _pair_blockdiag(y, pair_mask))


def _inv_unit_lower(a, eye, blk_mask, pair_mask):
    c = a[0].shape[0]
    m = range(len(a))
    a_bd = [jnp.where(blk_mask, a[i], 0.0) for i in m]
    a_off = [a[i] - a_bd[i] for i in m]
    a2 = [_pair_dot(a_bd[i], a_bd[i], pair_mask) for i in m]
    p = [eye + a_bd[i] for i in m]
    r = [_pair_dot(jnp.concatenate([p[i], a2[i]], axis=0), a2[i], pair_mask) for i in m]
    p = [p[i] + r[i][:c] for i in m]
    a4 = [r[i][c:] for i in m]
    r = [_pair_dot(jnp.concatenate([p[i], a4[i]], axis=0), a4[i], pair_mask) for i in m]
    p = [p[i] + r[i][:c] for i in m]
    a8 = [r[i][c:] for i in m]
    t_bd = [p[i] + _pair_dot(p[i], a8[i], pair_mask) for i in m]
    n = [_pair_dot(t_bd[i], a_off[i], pair_mask) for i in m]
    n2 = [_pair_dot(n[i], n[i], pair_mask) for i in m]
    z = [t_bd[i] + _pair_dot(n[i], t_bd[i], pair_mask) for i in m]
    return [z[i] + _pair_dot(n2[i], z[i], pair_mask) for i in m]


def _rwkv_kernel(z_ref, mu_ref, w0_ref, w2_ref, a0_ref, a2_ref, g2_ref, kk_ref, ka_ref, rk_ref,
                 lnw_ref, lnb_ref, seg_ref, o_ref, s_ref, prev_ref):
    c = RW_CHUNK
    tb = z_ref.shape[1]
    nck = tb // c

    @pl.when(pl.program_id(1) == 0)
    def _():
        s_ref[...] = jnp.zeros_like(s_ref)
        prev_ref[...] = jnp.zeros_like(prev_ref)

    z = z_ref[0].astype(F32)
    rows = lax.broadcasted_iota(jnp.int32, (tb, 1), 0)
    z_prev = jnp.where(rows == 0, prev_ref[...], pltpu.roll(z, 1, 0))
    prev_ref[...] = z[tb - 1:tb]
    zs = z + mu_ref[...] * (z_prev - z)
    r = zs[:, 0:RW_W]
    k = zs[:, RW_W:2 * RW_W]
    v = zs[:, 2 * RW_W:3 * RW_W]
    off = 3 * RW_W
    w_lo = zs[:, off:off + RW_DECAY_LORA]
    a_lo = zs[:, off + RW_DECAY_LORA:off + RW_DECAY_LORA + RW_A_LORA]
    g_lo = zs[:, off + RW_DECAY_LORA + RW_A_LORA:]

    wx = -(w0_ref[...] + _dot_x3(jnp.tanh(w_lo), w2_ref[...]))
    softplus = jnp.maximum(wx, 0.0) + jnp.log(1.0 + jnp.exp(-jnp.abs(wx)))
    logw = -jnp.exp(-softplus - 0.5)
    a = _sigmoid(a0_ref[...] + _dot_x3(a_lo, a2_ref[...]))
    g = _dot_x3(_sigmoid(g_lo), g2_ref[...])
    seg = seg_ref[...]
    kk = k * kk_ref[...]
    kk = kk * lax.rsqrt(jnp.maximum(_dot_x2_lhs(kk * kk, seg), 1e-24))
    k2 = k * (1.0 + (a - 1.0) * ka_ref[...])

    pw = 2 * RW_N
    row2 = lax.broadcasted_iota(jnp.int32, (c, pw), 0)
    col2 = lax.broadcasted_iota(jnp.int32, (c, pw), 1) % c
    incl2 = row2 >= col2
    strict2 = row2 > col2
    blk_mask = (row2 // RW_BLK) == (col2 // RW_BLK)
    eye = (row2 == col2).astype(F32)
    rowp = lax.broadcasted_iota(jnp.int32, (pw, pw), 0)
    colp = lax.broadcasted_iota(jnp.int32, (pw, pw), 1)
    pair_mask = (rowp // RW_N) == (colp // RW_N)
    rowb = lax.broadcasted_iota(jnp.int32, (tb, tb), 0)
    colb = lax.broadcasted_iota(jnp.int32, (tb, tb), 1)
    tri = jnp.where(colb >= (rowb // c) * c, jnp.where(rowb >= colb, 1.0, 0.0), 0.0).astype(BF16)
    cw = _dot_x2_rhs(tri, logw)
    w_inv = jnp.exp(-cw)
    last = jnp.concatenate([jnp.broadcast_to(cw[(ci + 1) * c - 1:(ci + 1) * c], (c, RW_W)) for ci in range(nck)],
                           axis=0)
    w_rest = jnp.exp(last - cw)
    beta = a * kk
    alpha_t = -kk * jnp.exp(cw - logw)
    r_t = r * jnp.exp(cw)
    beta_h = beta * w_inv
    k_h = k2 * w_inv
    beta_d = beta * w_rest
    k_d = k2 * w_rest

    ps = range(RW_HEADS // 2)
    cp = [(ci, p) for ci in range(nck) for p in ps]
    m = range(len(cp))
    rs = [slice(ci * c, (ci + 1) * c) for ci, _ in cp]
    sl = [slice(p * pw, (p + 1) * pw) for _, p in cp]
    v2 = [v[rs[i], sl[i]] for i in m]
    lhs = [jnp.concatenate([alpha_t[rs[i], sl[i]], r_t[rs[i], sl[i]]], axis=0).astype(BF16) for i in m]
    rhs = [jnp.concatenate([_pair_blockdiag(beta_h[rs[i], sl[i]], pair_mask),
                            _pair_blockdiag(k_h[rs[i], sl[i]], pair_mask)], axis=0) for i in m]
    big = [_dot_nt(lhs[i], rhs[i]) for i in m]
    a_ab = [jnp.where(strict2, big[i][:c, :pw], 0.0) for i in m]
    a_ak = [jnp.where(strict2, big[i][:c, pw:], 0.0) for i in m]
    a_rb = [jnp.where(incl2, big[i][c:, :pw], 0.0) for i in m]
    a_rk = [jnp.where(incl2, big[i][c:, pw:], 0.0) for i in m]
    t_inv = _inv_unit_lower(a_ab, eye, blk_mask, pair_mask)
    av = [_pair_dot(a_ak[i], v2[i], pair_mask) for i in m]
    a_r = [jnp.concatenate([a_rb[i], a_rk[i]], axis=1).astype(BF16) for i in m]
    bk_d = [jnp.concatenate([beta_d[rs[i], sl[i]], k_d[rs[i], sl[i]]], axis=0).astype(BF16) for i in m]
    s_cur = [s_ref[p] for p in ps]
    o_chunks = []
    for ci in range(nck):
        ix = [ci * len(ps) + p for p in ps]
        sd = [_dot_nt(lhs[ix[p]], s_cur[p].astype(BF16)) for p in ps]
        u = [_pair_dot(t_inv[ix[p]], sd[p][:c] + av[ix[p]], pair_mask) for p in ps]
        uv = [jnp.concatenate([_pair_blockdiag(u[p], pair_mask), _pair_blockdiag(v2[ix[p]], pair_mask)], axis=0)
              for p in ps]
        o_chunks.append(jnp.concatenate([sd[p][c:] + _dot(a_r[ix[p]], uv[p]) for p in ps], axis=1))
        w_last = jnp.exp(cw[(ci + 1) * c - 1:(ci + 1) * c])
        uvt = [jnp.concatenate([u[p], v2[ix[p]]], axis=0).astype(BF16) for p in ps]
        s_cur = [s_cur[p] * w_last[:, sl[p]] + jnp.where(pair_mask, _dot_tn(uvt[p], bk_d[ix[p]]), 0.0) for p in ps]
    for p in ps:
        s_ref[p] = s_cur[p]
    o = jnp.concatenate(o_chunks, axis=0)

    mean = _dot_x2_lhs(o, seg) * (1.0 / RW_N)
    dev = o - mean
    var = _dot_x2_lhs(dev * dev, seg) * (1.0 / RW_N)
    o = dev * lax.rsqrt(var + RW_GN_EPS) * lnw_ref[...] + lnb_ref[...]
    bonus = _dot_x2_lhs(r * k2 * rk_ref[...], seg) * v
    o_ref[0] = ((o + bonus) * g).astype(o_ref.dtype)


def _rwkv_call(z3, mu, w0, w2, a0, a2, g2, k_k, k_a, r_k, ln_w, ln_b):
    b, t, _ = z3.shape
    c = min(RW_TB, t)
    hid = lax.broadcasted_iota(jnp.int32, (RW_W, RW_W), 0) // RW_N
    seg = (hid == hid.T).astype(BF16)

    def vec(n):
        return pl.BlockSpec((1, n), lambda i, j: (0, 0))

    def mat(m, n):
        return pl.BlockSpec((m, n), lambda i, j: (0, 0))

    return pl.pallas_call(
        _rwkv_kernel,
        grid=(b, t // c),
        in_specs=[
            pl.BlockSpec((1, c, RW_COLS), lambda i, j: (i, j, RW_OFF // RW_COLS)),
            vec(RW_COLS), vec(RW_W), mat(RW_DECAY_LORA, RW_W), vec(RW_W), mat(RW_A_LORA, RW_W),
            mat(RW_GATE_LORA, RW_W), vec(RW_W), vec(RW_W), vec(RW_W), vec(RW_W), vec(RW_W),
            mat(RW_W, RW_W),
        ],
        out_specs=pl.BlockSpec((1, c, RW_W), lambda i, j: (i, j, 0)),
        out_shape=jax.ShapeDtypeStruct((b, t, RW_W), BF16),
        scratch_shapes=[pltpu.VMEM((RW_HEADS // 2, 2 * RW_N, 2 * RW_N), F32), pltpu.VMEM((1, RW_COLS), F32)],
        compiler_params=_cparams(("parallel", "arbitrary")),
        name="rwkv7_mixer",
    )(z3, mu.reshape(1, -1), w0.reshape(1, -1), w2, a0.reshape(1, -1), a2, g2, k_k.reshape(1, -1),
      k_a.reshape(1, -1), r_k.reshape(1, -1), ln_w.reshape(1, -1), ln_b.reshape(1, -1), seg)


def _merge_kernel(ohg_ref, oret_ref, orw_ref, zg_ref, x_ref, gate_ref, bhg_ref, bret_ref, brw_ref,
                  wout_ref, o_ref):
    d = x_ref.shape[1]
    y = _sigmoid(zg_ref[:, 0:d].astype(F32)) * _dot(ohg_ref[...], bhg_ref[...])
    y = y + _sigmoid(zg_ref[:, d:2 * d].astype(F32)) * _dot(oret_ref[...], bret_ref[...])
    y = y + _sigmoid(zg_ref[:, 2 * d:3 * d].astype(F32)) * _dot(orw_ref[...], brw_ref[...])
    o_ref[...] = x_ref[...] + gate_ref[0] * _dot(y.astype(BF16), wout_ref[...])


def _merge_call(o_hg, o_ret, o_rw, z2, x2, mod3, br_hg, br_ret, br_rw, w_out, seq, gate_blk, tm=512):
    n, d = x2.shape
    tpb = seq // tm

    def rows(w):
        return pl.BlockSpec((tm, w), lambda i: (i, 0))

    def full(m, k):
        return pl.BlockSpec((m, k), lambda i: (0, 0))

    return pl.pallas_call(
        _merge_kernel,
        grid=(n // tm,),
        in_specs=[
            rows(HG_W), rows(RET_W), rows(RW_W), rows(3 * d), rows(d),
            pl.BlockSpec((1, 1, d), lambda i: (i // tpb, 0, gate_blk)),
            full(HG_W, d), full(RET_W, d), full(RW_W, d), full(d, d),
        ],
        out_specs=rows(d),
        out_shape=jax.ShapeDtypeStruct((n, d), F32),
        compiler_params=_cparams(("parallel",)),
        name="merge_outproj",
    )(o_hg, o_ret, o_rw, z2, x2, mod3, br_hg, br_ret, br_rw, w_out)


def _pack_bf16_pairs(x):
    w = x.shape[1] // 2
    hi = pltpu.bitcast(x[:, :w].astype(BF16).astype(F32), jnp.uint32)
    lo = pltpu.bitcast(x[:, w:].astype(BF16).astype(F32), jnp.uint32)
    return pltpu.bitcast(hi | lax.shift_right_logical(lo, jnp.uint32(16)), jnp.int32)


def _unpack_bf16_pairs(p):
    u = pltpu.bitcast(p, jnp.uint32)
    hi = pltpu.bitcast(u & jnp.uint32(0xFFFF0000), F32)
    lo = pltpu.bitcast(lax.shift_left(u, jnp.uint32(16)), F32)
    return jnp.concatenate([hi, lo], axis=1)


def _route_kernel(x_ref, g_ref, scale_ref, shift_ref, rc_ref, hp_ref, eid_ref, wts_ref, cnt_ref):
    @pl.when(pl.program_id(0) == 0)
    def _():
        cnt_ref[...] = jnp.zeros_like(cnt_ref)

    h = _rms_mod(x_ref[...], g_ref[...], scale_ref[0], shift_ref[0])
    hp_ref[...] = _pack_bf16_pairs(h)
    tm = h.shape[0]
    lane = lax.broadcasted_iota(jnp.int32, (tm, LANES), 1)
    neg = -jnp.inf
    logits = _dot_x3(h, rc_ref[...])
    gl = jnp.where(lane < N_GROUPS, logits, neg)
    gmax = jnp.max(gl, axis=-1, keepdims=True)
    gidx = jnp.min(jnp.where(gl == gmax, lane, LANES), axis=-1, keepdims=True)
    gw = 1.0 / jnp.sum(jnp.exp(gl - gmax), axis=-1, keepdims=True)
    lo = N_GROUPS + gidx * EXPERTS_PER_GROUP
    el = jnp.where(lane >= lo, jnp.where(lane < lo + EXPERTS_PER_GROUP, logits, neg), neg)
    m1 = jnp.max(el, axis=-1, keepdims=True)
    l1 = jnp.min(jnp.where(el == m1, lane, LANES), axis=-1, keepdims=True)
    el2 = jnp.where(lane == l1, neg, el)
    m2 = jnp.max(el2, axis=-1, keepdims=True)
    l2 = jnp.min(jnp.where(el2 == m2, lane, LANES), axis=-1, keepdims=True)
    i1 = l1 - N_GROUPS
    i2 = l2 - N_GROUPS
    e2 = jnp.exp(m2 - m1)
    p1 = 1.0 / (1.0 + e2)
    p2 = e2 * p1
    oh1 = jnp.where(lane == i1, 1.0, 0.0)
    oh2 = jnp.where(lane == i2, 1.0, 0.0)
    row = lax.broadcasted_iota(jnp.int32, (tm, tm), 0)
    col = lax.broadcasted_iota(jnp.int32, (tm, tm), 1)
    earlier = jnp.where(row > col, 1.0, 0.0).astype(BF16)
    before = _dot(earlier, jnp.concatenate([oh1, oh2], axis=1).astype(BF16))
    tot1 = jnp.sum(oh1, axis=0, keepdims=True)
    carry = cnt_ref[...]
    r1 = jnp.sum(oh1 * (before[:, :LANES] + carry), axis=-1, keepdims=True).astype(jnp.int32)
    r2 = jnp.sum(oh2 * (before[:, LANES:] + (carry + tot1)), axis=-1, keepdims=True).astype(jnp.int32)
    cnt_ref[...] = carry + tot1 + jnp.sum(oh2, axis=0, keepdims=True)
    eid_ref[...] = jnp.where(lane == 0, i1, jnp.where(lane == 1, i2, jnp.where(lane == 2, r1,
                                                                             jnp.where(lane == 3, r2, 0))))
    wts_ref[...] = jnp.where(lane == 0, gw * p1, jnp.where(lane == 1, gw * p2, 0.0))


def _route_call(x2, gain, mod3, router_g, router_e, seq, scale_blk, shift_blk, tm=512):
    n, d = x2.shape
    tpb = seq // tm
    rc = jnp.pad(jnp.concatenate([router_g, router_e], axis=1), ((0, 0), (0, LANES - N_GROUPS - N_EXPERTS)))
    return pl.pallas_call(
        _route_kernel,
        grid=(n // tm,),
        in_specs=[
            pl.BlockSpec((tm, d), lambda i: (i, 0)),
            pl.BlockSpec((1, d), lambda i: (0, 0)),
            pl.BlockSpec((1, 1, d), lambda i: (i // tpb, 0, scale_blk)),
            pl.BlockSpec((1, 1, d), lambda i: (i // tpb, 0, shift_blk)),
            pl.BlockSpec((d, LANES), lambda i: (0, 0)),
        ],
        out_specs=[pl.BlockSpec((tm, d // 2), lambda i: (i, 0)), pl.BlockSpec((tm, LANES), lambda i: (i, 0)),
                   pl.BlockSpec((tm, LANES), lambda i: (i, 0)), pl.BlockSpec((1, LANES), lambda i: (0, 0))],
        out_shape=[jax.ShapeDtypeStruct((n, d // 2), jnp.int32), jax.ShapeDtypeStruct((n, LANES), jnp.int32),
                   jax.ShapeDtypeStruct((n, LANES), F32), jax.ShapeDtypeStruct((1, LANES), F32)],
        compiler_params=_cparams(("arbitrary",)),
        name="moe_route",
    )(x2, gain.reshape(1, d), mod3, mod3, rc)


SC_CORES = 2
SC_SUBCORES = 16
SC_WORKERS = SC_CORES * SC_SUBCORES
SC_ROWS = 32
SC_STREAMS = 4


def _sc_gather(table, idx):
    m = idx.shape[0]
    w = table.shape[1]
    per_worker = m // SC_WORKERS
    steps = per_worker // SC_ROWS
    assert per_worker * SC_WORKERS == m and steps * SC_ROWS == per_worker and steps % SC_STREAMS == 0
    mesh = plsc.VectorSubcoreMesh(core_axis_name="c", subcore_axis_name="s")
    ks = range(SC_STREAMS)

    def body(table_hbm, idx_hbm, out_hbm, idx_v, *rest):
        bufs, g_sems, w_sems = rest[:SC_STREAMS], rest[SC_STREAMS:2 * SC_STREAMS], rest[2 * SC_STREAMS:]
        wid = lax.axis_index("s") * SC_CORES + lax.axis_index("c")
        pltpu.sync_copy(idx_hbm.at[wid], idx_v)

        @pl.loop(0, steps, step=SC_STREAMS)
        def _(j):
            row0 = wid * per_worker + j * SC_ROWS
            gathers = [pltpu.async_copy(table_hbm.at[idx_v.at[j + q]], bufs[q], g_sems[q]) for q in ks]
            writes = []
            for q in ks:
                gathers[q].wait()
                writes.append(pltpu.async_copy(bufs[q], out_hbm.at[pl.ds(row0 + q * SC_ROWS, SC_ROWS)], w_sems[q]))
            for q in ks:
                writes[q].wait()

    return pl.kernel(
        body,
        out_type=jax.ShapeDtypeStruct((m, w), table.dtype),
        mesh=mesh,
        scratch_types=[pltpu.VMEM((steps, SC_ROWS), jnp.int32)] + [pltpu.VMEM((SC_ROWS, w), table.dtype)] * SC_STREAMS
        + [pltpu.SemaphoreType.DMA] * (2 * SC_STREAMS),
        name="sc_row_gather",
    )(table, idx.reshape(SC_WORKERS, steps, SC_ROWS))


def _sc_scatter2(rows, idx0, idx1, p):
    n, w = rows.shape
    per_worker = n // SC_WORKERS
    steps = per_worker // SC_ROWS
    assert per_worker * SC_WORKERS == n and steps * SC_ROWS == per_worker and steps % SC_STREAMS == 0
    mesh = plsc.VectorSubcoreMesh(core_axis_name="c", subcore_axis_name="s")
    ks = range(SC_STREAMS)

    def body(rows_hbm, i0_hbm, i1_hbm, out_hbm, i0_v, i1_v, *rest):
        bufs, r_sems = rest[:SC_STREAMS], rest[SC_STREAMS:2 * SC_STREAMS]
        s0_sems, s1_sems = rest[2 * SC_STREAMS:3 * SC_STREAMS], rest[3 * SC_STREAMS:]
        wid = lax.axis_index("s") * SC_CORES + lax.axis_index("c")
        pltpu.sync_copy(i0_hbm.at[wid], i0_v)
        pltpu.sync_copy(i1_hbm.at[wid], i1_v)

        @pl.loop(0, steps, step=SC_STREAMS)
        def _(j):
            row0 = wid * per_worker + j * SC_ROWS
            reads = [pltpu.async_copy(rows_hbm.at[pl.ds(row0 + q * SC_ROWS, SC_ROWS)], bufs[q], r_sems[q]) for q in ks]
            writes = []
            for q in ks:
                reads[q].wait()
                writes.append(pltpu.async_copy(bufs[q], out_hbm.at[i0_v.at[j + q]], s0_sems[q]))
                writes.append(pltpu.async_copy(bufs[q], out_hbm.at[i1_v.at[j + q]], s1_sems[q]))
            for wr in writes:
                wr.wait()

    index_block = pltpu.VMEM((steps, SC_ROWS), jnp.int32)
    return pl.kernel(
        body,
        out_type=jax.ShapeDtypeStruct((p, w), rows.dtype),
        mesh=mesh,
        scratch_types=[index_block, index_block] + [pltpu.VMEM((SC_ROWS, w), rows.dtype)] * SC_STREAMS
        + [pltpu.SemaphoreType.DMA] * (3 * SC_STREAMS),
        name="sc_row_scatter",
    )(rows, idx0.reshape(SC_WORKERS, steps, SC_ROWS), idx1.reshape(SC_WORKERS, steps, SC_ROWS))


MOE_TM = 512


def _gexperts_kernel(te_ref, tv_ref, nu_ref, xs_ref, w1_ref, w3_ref, w2_ref, ys_ref, w1b_ref, w3b_ref, w2b_ref):
    i = pl.program_id(0)

    @pl.when((i == 0) | (te_ref[i] != te_ref[jnp.maximum(i - 1, 0)]))
    def _():
        w1b_ref[...] = w1_ref[0].astype(BF16)
        w3b_ref[...] = w3_ref[0].astype(BF16)
        w2b_ref[...] = w2_ref[0].astype(BF16)

    @pl.when(i < nu_ref[0])
    def _():
        rid = lax.broadcasted_iota(jnp.int32, xs_ref.shape, 0)
        xb = _unpack_bf16_pairs(jnp.where(rid < tv_ref[i], xs_ref[...], 0)).astype(BF16)
        act = (_silu(_dot(xb, w1b_ref[...])) * _dot(xb, w3b_ref[...])).astype(BF16)
        ys_ref[...] = _pack_bf16_pairs(_dot(act, w2b_ref[...]))


def _gexperts_call(xs, tile_expert, tile_valid, n_used, w1, w3, w2):
    p, half = xs.shape
    ne, d, de = w1.shape
    nt = p // MOE_TM

    def rows(i, te, tv, nu):
        return (jnp.minimum(i, nu[0] - 1), 0)

    def wsel(i, te, tv, nu):
        return (te[i], 0, 0)

    return pl.pallas_call(
        _gexperts_kernel,
        grid_spec=pltpu.PrefetchScalarGridSpec(
            num_scalar_prefetch=3,
            grid=(nt,),
            in_specs=[
                pl.BlockSpec((MOE_TM, half), rows),
                pl.BlockSpec((1, d, de), wsel),
                pl.BlockSpec((1, d, de), wsel),
                pl.BlockSpec((1, de, d), wsel),
            ],
            out_specs=pl.BlockSpec((MOE_TM, half), rows),
            scratch_shapes=[pltpu.VMEM((d, de), BF16), pltpu.VMEM((d, de), BF16), pltpu.VMEM((de, d), BF16)],
        ),
        out_shape=jax.ShapeDtypeStruct((p, half), jnp.int32),
        compiler_params=_cparams(("arbitrary",)),
        name="moe_experts",
    )(tile_expert, tile_valid, n_used, xs, w1, w3, w2)


def _combine_kernel(y0_ref, y1_ref, wts_ref, x_ref, gate_ref, fg_ref, o_ref, *, final_norm):
    wts = wts_ref[...]
    moe = wts[:, 0:1] * _unpack_bf16_pairs(y0_ref[...]) + wts[:, 1:2] * _unpack_bf16_pairs(y1_ref[...])
    xn = x_ref[...] + gate_ref[0] * moe
    if final_norm:
        xn = xn * lax.rsqrt(jnp.mean(xn * xn, axis=-1, keepdims=True) + NORM_EPS) * fg_ref[...]
    o_ref[...] = xn


def _combine_call(yg, wts, x2, mod3, final_g, seq, gate_blk, final_norm, tm=512):
    n, d = x2.shape
    tpb = seq // tm
    slot1 = n // tm
    return pl.pallas_call(
        functools.partial(_combine_kernel, final_norm=final_norm),
        grid=(n // tm,),
        in_specs=[
            pl.BlockSpec((tm, d // 2), lambda i: (i, 0)),
            pl.BlockSpec((tm, d // 2), lambda i: (i + slot1, 0)),
            pl.BlockSpec((tm, LANES), lambda i: (i, 0)),
            pl.BlockSpec((tm, d), lambda i: (i, 0)),
            pl.BlockSpec((1, 1, d), lambda i: (i // tpb, 0, gate_blk)),
            pl.BlockSpec((1, d), lambda i: (0, 0)),
        ],
        out_specs=pl.BlockSpec((tm, d), lambda i: (i, 0)),
        out_shape=jax.ShapeDtypeStruct((n, d), F32),
        compiler_params=_cparams(("parallel",)),
        name="moe_combine",
    )(yg, yg, wts, x2, mod3, final_g.reshape(1, d))


def _pos_kernel(eid_ref, ts_ref, p0_ref, p1_ref):
    eid = eid_ref[...]
    tm = eid.shape[0]
    lane = lax.broadcasted_iota(jnp.int32, (tm, LANES), 1)
    sub = lax.broadcasted_iota(jnp.int32, (tm, LANES), 0) % LANES
    for slot, out_ref in ((0, p0_ref), (1, p1_ref)):
        first_row = jnp.sum(jnp.where(lane == eid[:, slot:slot + 1], ts_ref[...], 0), axis=-1, keepdims=True)
        pos = first_row + eid[:, slot + 2:slot + 3]
        out_ref[...] = jnp.sum(jnp.where(lane == sub, pos, 0).reshape(tm // LANES, LANES, LANES), axis=1)


def _pos_call(eid, first_rows, tm=1024):
    n = eid.shape[0]
    out = jax.ShapeDtypeStruct((n // LANES, LANES), jnp.int32)
    p0, p1 = pl.pallas_call(
        _pos_kernel,
        grid=(n // tm,),
        in_specs=[pl.BlockSpec((tm, LANES), lambda i: (i, 0)), pl.BlockSpec((1, LANES), lambda i: (0, 0))],
        out_specs=[pl.BlockSpec((tm // LANES, LANES), lambda i: (i, 0))] * 2,
        out_shape=[out, out],
        compiler_params=_cparams(("parallel",)),
        name="moe_positions",
    )(eid, first_rows)
    return p0.reshape(n), p1.reshape(n)


def _moe_plan(eid, counts_f):
    n = eid.shape[0]
    nt = (2 * n) // MOE_TM + N_EXPERTS
    counts = counts_f[0, :N_EXPERTS].astype(jnp.int32)
    tiles = (counts + MOE_TM - 1) // MOE_TM
    tile_end = jnp.cumsum(tiles)
    tile_start = tile_end - tiles
    n_used = tile_end[-1:]
    tile_iota = jnp.arange(nt, dtype=jnp.int32)
    tile_expert = jnp.sum(jnp.minimum(tile_iota, n_used - 1)[:, None] >= tile_end[None, :], axis=1, dtype=jnp.int32)
    own = tile_expert[:, None] == jnp.arange(N_EXPERTS, dtype=jnp.int32)[None, :]
    count_t = jnp.sum(jnp.where(own, counts[None, :], 0), axis=1)
    start_t = jnp.sum(jnp.where(own, tile_start[None, :], 0), axis=1)
    tile_valid = jnp.clip(count_t - (tile_iota - start_t) * MOE_TM, 0, MOE_TM)
    first_rows = jnp.pad(tile_start * MOE_TM, (0, LANES - N_EXPERTS)).reshape(1, LANES)
    pos0, pos1 = _pos_call(eid, first_rows)
    return pos0, pos1, tile_expert, tile_valid, n_used


def kernel(x, c, positions, ada_w, ada_b, norm1_g, norm2_g, w_in, hg_lb_table, hg_norm_w, rw_mu, rw_w0, rw_w2,
           rw_a0, rw_a2, rw_g2, rw_k_k, rw_k_a, rw_r_k, rw_ln_w, rw_ln_b, br_hg, br_ret, br_rw, w_out,
           router_g, router_e, moe_w1, moe_w3, moe_w2, final_g):
    b, t, d = x.shape
    depth = ada_w.shape[0]
    n = b * t
    assert w_in.shape[2] == IN_COLS and d == 1024

    lb_p = jax.nn.softmax(hg_lb_table.astype(F32), axis=0)
    lower_bounds = jnp.cumsum(lb_p, axis=0) - lb_p[0]

    mod = _mod_call(c, ada_w, ada_b)
    cos2, sin2 = _rope_call(positions, RET_DK)
    w_perm = _wprep_call(w_in)
    x2 = x.reshape(n, d)
    for l in range(depth):
        mod3 = mod[l].reshape(b, 1, 6 * d)
        z2 = _inproj_call(x2, norm1_g[l], mod3, w_perm, l, t, scale_blk=1, shift_blk=0)
        z3 = z2.reshape(b, t, IN_COLS)
        o_hg = _hgrn2_call(z3, lower_bounds[l], hg_norm_w[l])
        o_ret = _ret_call(z3, cos2, sin2)
        o_rw = _rwkv_call(z3, rw_mu[l], rw_w0[l], rw_w2[l], rw_a0[l], rw_a2[l], rw_g2[l], rw_k_k[l],
                          rw_k_a[l], rw_r_k[l], rw_ln_w[l], rw_ln_b[l])
        x2 = _merge_call(o_hg.reshape(n, HG_W), o_ret.reshape(n, RET_W), o_rw.reshape(n, RW_W), z2, x2, mod3,
                         br_hg[l].astype(BF16), br_ret[l].astype(BF16), br_rw[l].astype(BF16),
                         w_out[l].astype(BF16), t, gate_blk=2)
        hp, eid, wts, counts = _route_call(x2, norm2_g[l], mod3, router_g[l], router_e[l], t, scale_blk=4,
                                           shift_blk=3)
        pos0, pos1, tile_expert, tile_valid, n_used = _moe_plan(eid, counts)
        xs = _sc_scatter2(hp, pos0, pos1, (2 * n // MOE_TM + N_EXPERTS) * MOE_TM)
        ys = _gexperts_call(xs, tile_expert + l * N_EXPERTS, tile_valid, n_used,
                            moe_w1.reshape((-1,) + moe_w1.shape[2:]), moe_w3.reshape((-1,) + moe_w3.shape[2:]),
                            moe_w2.reshape((-1,) + moe_w2.shape[2:]))
        yg = _sc_gather(ys, jnp.concatenate([pos0, pos1]))
        x2 = _combine_call(yg, wts, x2, mod3, final_g, t, gate_blk=5, final_norm=(l == depth - 1))
    return x2.reshape(b, t, d)
```

```python
import functools

import jax
import jax.numpy as jnp
from jax import lax
from jax.experimental import pallas as pl
from jax.experimental.pallas import tpu as pltpu
from jax.experimental.pallas import tpu_sc as plsc

F32 = jnp.float32
BF16 = jnp.bfloat16
HIGHEST = lax.Precision.HIGHEST

HG_HEADS = 4
HG_DK = 128
HG_W = HG_HEADS * HG_DK
RET_HEADS = 4
RET_DK = 128
RET_W = RET_HEADS * RET_DK
RW_HEADS = 8
RW_N = 64
RW_W = RW_HEADS * RW_N
RW_DECAY_LORA = 64
RW_A_LORA = 64
RW_GATE_LORA = 128
RW_COLS = 3 * RW_W + RW_DECAY_LORA + RW_A_LORA + RW_GATE_LORA
RW_GN_EPS = 64e-5
N_GROUPS = 4
EXPERTS_PER_GROUP = 8
N_EXPERTS = N_GROUPS * EXPERTS_PER_GROUP
ROPE_THETA = 10000.0
NORM_EPS = 1e-6

LANES = 128
LOG2E = 1.4426950408889634
VMEM_LIMIT = 56 * 1024 * 1024

GATE_OFF = 0
HG_OFF = 3 * 1024
RET_OFF = HG_OFF + 4 * HG_W
RW_OFF = RET_OFF + 4 * RET_W
IN_COLS = RW_OFF + RW_COLS

HG_CHUNK = 64
HG_SUB = 16
HG_SAFE_SPAN = 60.0
RW_CHUNK = 64
RW_BLK = 16
RW_TB = 256
Z_DTYPE = BF16


def _cparams(sem):
    return pltpu.CompilerParams(dimension_semantics=sem, vmem_limit_bytes=VMEM_LIMIT)


def _dot(a, b, precision=None):
    return jnp.dot(a, b, preferred_element_type=F32, precision=precision)


def _dot_nt(a, b, precision=None):
    return lax.dot_general(a, b, (((1,), (1,)), ((), ())), preferred_element_type=F32, precision=precision)


def _dot_tn(a, b, precision=None):
    return lax.dot_general(a, b, (((0,), (0,)), ((), ())), preferred_element_type=F32, precision=precision)


def _split_bf16(x):
    hi = x.astype(BF16)
    return hi, (x - hi.astype(F32)).astype(BF16)


def _dot_x3(a, b):
    ah, al = _split_bf16(a)
    bh, bl = _split_bf16(b)
    return _dot(ah, bh) + _dot(ah, bl) + _dot(al, bh)


def _dot_x2_lhs(a, b_exact):
    ah, al = _split_bf16(a)
    return _dot(ah, b_exact) + _dot(al, b_exact)


def _dot_x2_rhs(a_exact, b):
    bh, bl = _split_bf16(b)
    return _dot(a_exact, bh) + _dot(a_exact, bl)


def _sigmoid(x):
    return 0.5 * jnp.tanh(0.5 * x) + 0.5


def _silu(x):
    return x * _sigmoid(x)


def _rms_mod(x, gain, scale, shift):
    y = x * lax.rsqrt(jnp.mean(x * x, axis=-1, keepdims=True) + NORM_EPS)
    return (y * gain) * (1.0 + scale) + shift


def _mod_kernel(c_ref, w_ref, b_ref, o_ref):
    c = c_ref[...]
    o_ref[0] = _dot(_silu(c), w_ref[0], HIGHEST) + b_ref[0]


def _mod_call(c, ada_w, ada_b):
    depth, d, d6 = ada_w.shape
    b = c.shape[0]
    nblk = d6 // d
    return pl.pallas_call(
        _mod_kernel,
        grid=(depth, nblk),
        in_specs=[
            pl.BlockSpec((b, d), lambda l, j: (0, 0)),
            pl.BlockSpec((1, d, d), lambda l, j: (l, 0, j)),
            pl.BlockSpec((1, 1, d), lambda l, j: (l, 0, j)),
        ],
        out_specs=pl.BlockSpec((1, b, d), lambda l, j: (l, 0, j)),
        out_shape=jax.ShapeDtypeStruct((depth, b, d6), F32),
        compiler_params=_cparams(("parallel", "parallel")),
        name="adaln_mod",
    )(c, ada_w, ada_b.reshape(depth, 1, d6))


def _rope_kernel(pos_ref, freq_ref, sign_ref, cos_ref, sin_ref):
    ang = pos_ref[0].astype(F32) * freq_ref[...]
    cos_ref[0] = jnp.cos(ang)
    sin_ref[0] = jnp.sin(ang) * sign_ref[...]


def _rope_call(positions, d):
    b, t = positions.shape
    tb = min(t, 512)
    inv_freq = ROPE_THETA ** (-jnp.arange(0, d, 2, dtype=F32) / d)
    freq2 = jnp.concatenate([inv_freq, inv_freq]).reshape(1, d)
    sign2 = jnp.concatenate([-jnp.ones((d // 2,), F32), jnp.ones((d // 2,), F32)]).reshape(1, d)
    out = jax.ShapeDtypeStruct((b, t, d), F32)
    return pl.pallas_call(
        _rope_kernel,
        grid=(b, t // tb),
        in_specs=[
            pl.BlockSpec((1, tb, 1), lambda i, j: (i, j, 0)),
            pl.BlockSpec((1, d), lambda i, j: (0, 0)),
            pl.BlockSpec((1, d), lambda i, j: (0, 0)),
        ],
        out_specs=[pl.BlockSpec((1, tb, d), lambda i, j: (i, j, 0))] * 2,
        out_shape=[out, out],
        compiler_params=_cparams(("parallel", "parallel")),
        name="rope_tables",
    )(positions.reshape(b, t, 1), freq2, sign2)


W_BLK = 256


def _wprep_kernel(w_ref, o_ref):
    o_ref[...] = w_ref[...].astype(o_ref.dtype)


def _wprep_call(w_in):
    depth, d, cols = w_in.shape
    nblk = cols // W_BLK
    first = (cols - 3 * d) // W_BLK
    return pl.pallas_call(
        _wprep_kernel,
        grid=(depth, nblk),
        in_specs=[pl.BlockSpec((1, d, W_BLK), lambda l, j: (l, 0, (j + first) % nblk))],
        out_specs=pl.BlockSpec((1, d, W_BLK), lambda l, j: (l, 0, j)),
        out_shape=jax.ShapeDtypeStruct((depth, d, cols), BF16),
        compiler_params=_cparams(("parallel", "parallel")),
        name="w_in_layout",
    )(w_in)


def _inproj_kernel(x_ref, g_ref, scale_ref, shift_ref, w_ref, o_ref, h_ref):
    @pl.when(pl.program_id(1) == 0)
    def _():
        h = _rms_mod(x_ref[...], g_ref[...], scale_ref[0], shift_ref[0])
        h_ref[...] = h.astype(BF16)

    o_ref[...] = _dot(h_ref[...], w_ref[0]).astype(o_ref.dtype)


def _inproj_call(x2, gain, mod3, w_bf16, layer, seq, scale_blk, shift_blk, tm=2048, tn=1792):
    n, d = x2.shape
    cols = w_bf16.shape[2]
    tpb = seq // tm
    return pl.pallas_call(
        _inproj_kernel,
        grid=(n // tm, cols // tn),
        in_specs=[
            pl.BlockSpec((tm, d), lambda i, j: (i, 0)),
            pl.BlockSpec((1, d), lambda i, j: (0, 0)),
            pl.BlockSpec((1, 1, d), lambda i, j: (i // tpb, 0, scale_blk)),
            pl.BlockSpec((1, 1, d), lambda i, j: (i // tpb, 0, shift_blk)),
            pl.BlockSpec((1, d, tn), lambda i, j: (layer, 0, j)),
        ],
        out_specs=pl.BlockSpec((tm, tn), lambda i, j: (i, j)),
        out_shape=jax.ShapeDtypeStruct((n, cols), Z_DTYPE),
        scratch_shapes=[pltpu.VMEM((tm, d), BF16)],
        compiler_params=_cparams(("parallel", "arbitrary")),
        name="norm_inproj",
    )(x2, gain.reshape(1, d), mod3, mod3, w_bf16)


def _hgrn2_block(zs, lbs, nw, sts, factored):
    hs = range(len(zs))
    tb = zs[0][0].shape[0]
    c, sub = HG_CHUNK, HG_SUB
    nc, ns, nb = tb // c, c // sub, tb // sub
    f = [lbs[h] + (1.0 - lbs[h]) * _sigmoid(zs[h][1]) for h in hs]
    logf = [jnp.log(jnp.maximum(f[h], 1e-30)) for h in hs]
    q = [_silu(zs[h][0]) * (HG_DK ** -0.5) for h in hs]
    k = [1.0 - f[h] for h in hs]
    v = [zs[h][2] for h in hs]
    v_b = [v[h].astype(BF16) for h in hs]
    row = lax.broadcasted_iota(jnp.int32, (tb, tb), 0)
    col = lax.broadcasted_iota(jnp.int32, (tb, tb), 1)
    tri = jnp.where(col >= (row // c) * c, jnp.where(row >= col, 1.0, 0.0), 0.0).astype(BF16)
    cum = [_dot_x2_rhs(tri, logf[h]) for h in hs]
    cum3 = [cum[h].reshape(nb, sub, HG_DK) for h in hs]
    ref3 = [cum3[h][:, 0:1, :] - logf[h].reshape(nb, sub, HG_DK)[:, 0:1, :] for h in hs]
    span = functools.reduce(jnp.maximum, [jnp.max(ref3[h] - cum3[h][:, sub - 1:sub, :]) for h in hs])
    qe = [(q[h] * jnp.exp(cum[h])).astype(BF16) for h in hs]

    offd = [(h, ci * c, ci * c + sub * i) for h in hs for ci in range(nc) for i in range(1, ns)]
    base = [cum[h][lo - 1:lo] for h, _, lo in offd]
    qt = [(q[h][lo:lo + sub] * jnp.exp(cum[h][lo:lo + sub] - base[j])).astype(BF16)
          for j, (h, _, lo) in enumerate(offd)]
    kt = [(k[h][r0:lo] * jnp.exp(base[j] - cum[h][r0:lo])).astype(BF16) for j, (h, r0, lo) in enumerate(offd)]
    a = [_dot_nt(qt[j], kt[j]).astype(BF16) for j in range(len(offd))]
    av = {(h, lo): _dot(a[j], v_b[h][r0:lo]) for j, (h, r0, lo) in enumerate(offd)}

    cs = [slice(ci * c, (ci + 1) * c) for ci in range(nc)]
    hc = [(h, ci) for h in hs for ci in range(nc)]
    last = {(h, ci): cum[h][(ci + 1) * c - 1:(ci + 1) * c] for h, ci in hc}
    kd = {(h, ci): (k[h][cs[ci]] * jnp.exp(last[h, ci] - cum[h][cs[ci]])).astype(BF16) for h, ci in hc}
    inc = {(h, ci): _dot_tn(v_b[h][cs[ci]], kd[h, ci]) for h, ci in hc}
    s_in = {(h, 0): sts[h] for h in hs}
    for ci in range(nc):
        for h in hs:
            s_in[h, ci + 1] = s_in[h, ci] * jnp.exp(last[h, ci]) + inc[h, ci]
    o_inter = {(h, ci): _dot_nt(qe[h][cs[ci]], s_in[h, ci].astype(BF16)) for h, ci in hc}

    if factored:
        qf = [(q[h] * jnp.exp(cum3[h] - ref3[h]).reshape(tb, HG_DK)).astype(BF16) for h in hs]
        kf = [(k[h] * jnp.exp(ref3[h] - cum3[h]).reshape(tb, HG_DK)).astype(BF16) for h in hs]
        rc = lax.broadcasted_iota(jnp.int32, (c, c), 0)
        cc = lax.broadcasted_iota(jnp.int32, (c, c), 1)
        keep = (rc >= cc) & (rc // sub == cc // sub)
        a_d = {(h, ci): jnp.where(keep, _dot_nt(qf[h][cs[ci]], kf[h][cs[ci]]), 0.0).astype(BF16) for h, ci in hc}
        dg = {(h, ci): _dot(a_d[h, ci], v_b[h][cs[ci]]) for h, ci in hc}
        diag = [jnp.concatenate([dg[h, ci] for ci in range(nc)], axis=0) for h in hs]
    else:
        gb = 4
        trow = lax.broadcasted_iota(jnp.int32, (gb, sub, HG_DK), 1)
        diag = []
        for h in hs:
            c2 = cum[h] * LOG2E
            ks2 = c2 - jnp.log2(k[h])
            parts = []
            for g0 in range(0, nb, gb):
                rws = slice(g0 * sub, (g0 + gb) * sub)
                c23, ks23, q3, v3 = (x[rws].reshape(gb, sub, HG_DK) for x in (c2, ks2, q[h], v[h]))
                acc = jnp.zeros((gb, sub, HG_DK), F32)
                for s in range(sub):
                    e = jnp.exp2(jnp.where(trow >= s, c23 - ks23[:, s:s + 1, :], -jnp.inf))
                    a_col = jnp.sum(q3 * e, axis=-1, keepdims=True)
                    acc = acc + a_col * v3[:, s:s + 1, :]
                parts.append(acc.reshape(gb * sub, HG_DK))
            diag.append(jnp.concatenate(parts, axis=0))

    outs = []
    for h in hs:
        pieces = []
        for ci in range(nc):
            for i in range(ns):
                lo = ci * c + sub * i
                piece = o_inter[h, ci][sub * i:sub * (i + 1)] + diag[h][lo:lo + sub]
                pieces.append(piece + av[h, lo] if i > 0 else piece)
        o = jnp.concatenate(pieces, axis=0)
        o = o * lax.rsqrt(jnp.mean(o * o, axis=-1, keepdims=True) + NORM_EPS)
        outs.append(o * nw * _silu(zs[h][3]))
    return outs, [s_in[h, nc] for h in hs], span


def _hgrn2_kernel(zq_ref, zf_ref, zi_ref, zg_ref, lb_ref, nw_ref, o_ref, st_ref):
    @pl.when(pl.program_id(1) == 0)
    def _():
        st_ref[...] = jnp.zeros_like(st_ref)

    hs = range(HG_HEADS)
    sl = [slice(h * HG_DK, (h + 1) * HG_DK) for h in hs]

    def run(factored):
        zs = [tuple(r[0, :, sl[h]].astype(F32) for r in (zq_ref, zf_ref, zi_ref, zg_ref)) for h in hs]
        outs, sts, span = _hgrn2_block(zs, [lb_ref[:, sl[h]] for h in hs], nw_ref[...],
                                       [st_ref[h] for h in hs], factored)
        return jnp.concatenate(outs, axis=1), sts, span

    st_old = [st_ref[h] for h in hs]
    o, st_new, span = run(True)
    for h in hs:
        st_ref[h] = st_new[h]
    o_ref[0] = o.astype(o_ref.dtype)

    @pl.when(span > HG_SAFE_SPAN)
    def _():
        for h in hs:
            st_ref[h] = st_old[h]
        o2, st2, _ = run(False)
        for h in hs:
            st_ref[h] = st2[h]
        o_ref[0] = o2.astype(o_ref.dtype)


def _hgrn2_call(z3, lower_bound, norm_w, tb=256):
    b, t, _ = z3.shape
    tb = min(tb, t)
    base = HG_OFF // HG_W

    def zspec(part):
        return pl.BlockSpec((1, tb, HG_W), lambda i, j: (i, j, base + part))

    return pl.pallas_call(
        _hgrn2_kernel,
        grid=(b, t // tb),
        in_specs=[
            zspec(0), zspec(1), zspec(2), zspec(3),
            pl.BlockSpec((1, HG_W), lambda i, j: (0, 0)),
            pl.BlockSpec((1, LANES), lambda i, j: (0, 0)),
        ],
        out_specs=pl.BlockSpec((1, tb, HG_W), lambda i, j: (i, j, 0)),
        out_shape=jax.ShapeDtypeStruct((b, t, HG_W), BF16),
        scratch_shapes=[pltpu.VMEM((HG_HEADS, HG_DK, HG_DK), F32)],
        compiler_params=_cparams(("parallel", "arbitrary")),
        name="hgrn2_mixer",
    )(z3, z3, z3, z3, lower_bound.reshape(1, HG_W), norm_w.reshape(1, HG_DK))


def _ret_kernel(zq_ref, zk_ref, zv_ref, zg_ref, cos_ref, sin_ref, o_ref, st_ref, dmask_ref, *, chunk):
    hs = range(RET_HEADS)
    sl = [slice(h * RET_DK, (h + 1) * RET_DK) for h in hs]
    lg = [jnp.log(jnp.full((1, 1), 1.0 - 2.0 ** (-5.0 - h), F32)) for h in hs]

    @pl.when(pl.program_id(1) == 0)
    def _():
        st_ref[...] = jnp.zeros_like(st_ref)
        row = lax.broadcasted_iota(jnp.int32, (chunk, chunk), 0)
        col = lax.broadcasted_iota(jnp.int32, (chunk, chunk), 1)
        rel = (row - col).astype(F32)
        for h in hs:
            dmask_ref[h] = jnp.where(rel >= 0.0, jnp.exp(jnp.maximum(rel, 0.0) * lg[h]), 0.0)

    cos2 = cos_ref[0]
    sin2 = sin_ref[0]
    half = RET_DK // 2

    def rope(z):
        return z * cos2 + pltpu.roll(z, half, 1) * sin2

    tcol = lax.broadcasted_iota(jnp.int32, (chunk, 1), 0).astype(F32)
    q = [rope(zq_ref[0, :, sl[h]].astype(F32)) * (RET_DK ** -0.5) for h in hs]
    k = [rope(zk_ref[0, :, sl[h]].astype(F32)) for h in hs]
    v_b = [zv_ref[0, :, sl[h]].astype(BF16) for h in hs]
    st = [st_ref[h] for h in hs]
    scores = [(_dot_nt(q[h].astype(BF16), k[h].astype(BF16)) * dmask_ref[h]).astype(BF16) for h in hs]
    qx = [(q[h] * jnp.exp((tcol + 1.0) * lg[h])).astype(BF16) for h in hs]
    kz = [(k[h] * jnp.exp((chunk - 1.0 - tcol) * lg[h])).astype(BF16) for h in hs]
    o = [_dot(scores[h], v_b[h]) + _dot_nt(qx[h], st[h].astype(BF16)) for h in hs]
    for h in hs:
        st_ref[h] = st[h] * jnp.exp(chunk * lg[h]) + _dot_tn(v_b[h], kz[h])
    o = [o[h] * lax.rsqrt(jnp.mean(o[h] * o[h], axis=-1, keepdims=True) + NORM_EPS) for h in hs]
    o_ref[0] = (jnp.concatenate(o, axis=1) * _silu(zg_ref[0].astype(F32))).astype(o_ref.dtype)


def _ret_call(z3, cos2, sin2, chunk=256):
    b, t, _ = z3.shape
    chunk = min(chunk, t)
    base = RET_OFF // RET_W

    def zspec(part):
        return pl.BlockSpec((1, chunk, RET_W), lambda i, j: (i, j, base + part))

    tab = pl.BlockSpec((1, chunk, RET_DK), lambda i, j: (i, j, 0))
    return pl.pallas_call(
        functools.partial(_ret_kernel, chunk=chunk),
        grid=(b, t // chunk),
        in_specs=[zspec(0), zspec(1), zspec(2), zspec(3), tab, tab],
        out_specs=pl.BlockSpec((1, chunk, RET_W), lambda i, j: (i, j, 0)),
        out_shape=jax.ShapeDtypeStruct((b, t, RET_W), BF16),
        scratch_shapes=[pltpu.VMEM((RET_HEADS, RET_DK, RET_DK), F32), pltpu.VMEM((RET_HEADS, chunk, chunk), F32)],
        compiler_params=_cparams(("parallel", "arbitrary")),
        name="retention_mixer",
    )(z3, z3, z3, z3, cos2, sin2)


def _pair_blockdiag(y, pair_mask):
    return jnp.where(pair_mask, jnp.concatenate([y, y], axis=0), 0.0).astype(BF16)


def _pair_dot(x, y, pair_mask):
    return _dot(x.astype(BF16), _pair_blockdiag(y, pair_mask))


def _inv_unit_lower(a, eye, blk_mask, pair_mask):
    c = a[0].shape[0]
    m = range(len(a))
    a_bd = [jnp.where(blk_mask, a[i], 0.0) for i in m]
    a_off = [a[i] - a_bd[i] for i in m]
    a2 = [_pair_dot(a_bd[i], a_bd[i], pair_mask) for i in m]
    p = [eye + a_bd[i] for i in m]
    r = [_pair_dot(jnp.concatenate([p[i], a2[i]], axis=0), a2[i], pair_mask) for i in m]
    p = [p[i] + r[i][:c] for i in m]
    a4 = [r[i][c:] for i in m]
    r = [_pair_dot(jnp.concatenate([p[i], a4[i]], axis=0), a4[i], pair_mask) for i in m]
    p = [p[i] + r[i][:c] for i in m]
    a8 = [r[i][c:] for i in m]
    t_bd = [p[i] + _pair_dot(p[i], a8[i], pair_mask) for i in m]
    n = [_pair_dot(t_bd[i], a_off[i], pair_mask) for i in m]
    n2 = [_pair_dot(n[i], n[i], pair_mask) for i in m]
    z = [t_bd[i] + _pair_dot(n[i], t_bd[i], pair_mask) for i in m]
    return [z[i] + _pair_dot(n2[i], z[i], pair_mask) for i in m]


def _rwkv_kernel(z_ref, mu_ref, w0_ref, w2_ref, a0_ref, a2_ref, g2_ref, kk_ref, ka_ref, rk_ref,
                 lnw_ref, lnb_ref, seg_ref, o_ref, s_ref, prev_ref):
    c = RW_CHUNK
    tb = z_ref.shape[1]
    nck = tb // c

    @pl.when(pl.program_id(1) == 0)
    def _():
        s_ref[...] = jnp.zeros_like(s_ref)
        prev_ref[...] = jnp.zeros_like(prev_ref)

    z = z_ref[0].astype(F32)
    rows = lax.broadcasted_iota(jnp.int32, (tb, 1), 0)
    z_prev = jnp.where(rows == 0, prev_ref[...], pltpu.roll(z, 1, 0))
    prev_ref[...] = z[tb - 1:tb]
    zs = z + mu_ref[...] * (z_prev - z)
    r = zs[:, 0:RW_W]
    k = zs[:, RW_W:2 * RW_W]
    v = zs[:, 2 * RW_W:3 * RW_W]
    off = 3 * RW_W
    w_lo = zs[:, off:off + RW_DECAY_LORA]
    a_lo = zs[:, off + RW_DECAY_LORA:off + RW_DECAY_LORA + RW_A_LORA]
    g_lo = zs[:, off + RW_DECAY_LORA + RW_A_LORA:]

    wx = -(w0_ref[...] + _dot_x3(jnp.tanh(w_lo), w2_ref[...]))
    softplus = jnp.maximum(wx, 0.0) + jnp.log(1.0 + jnp.exp(-jnp.abs(wx)))
    logw = -jnp.exp(-softplus - 0.5)
    a = _sigmoid(a0_ref[...] + _dot_x3(a_lo, a2_ref[...]))
    g = _dot_x3(_sigmoid(g_lo), g2_ref[...])
    seg = seg_ref[...]
    kk = k * kk_ref[...]
    kk = kk * lax.rsqrt(jnp.maximum(_dot_x2_lhs(kk * kk, seg), 1e-24))
    k2 = k * (1.0 + (a - 1.0) * ka_ref[...])

    pw = 2 * RW_N
    row2 = lax.broadcasted_iota(jnp.int32, (c, pw), 0)
    col2 = lax.broadcasted_iota(jnp.int32, (c, pw), 1) % c
    incl2 = row2 >= col2
    strict2 = row2 > col2
    blk_mask = (row2 // RW_BLK) == (col2 // RW_BLK)
    eye = (row2 == col2).astype(F32)
    rowp = lax.broadcasted_iota(jnp.int32, (pw, pw), 0)
    colp = lax.broadcasted_iota(jnp.int32, (pw, pw), 1)
    pair_mask = (rowp // RW_N) == (colp // RW_N)
    rowb = lax.broadcasted_iota(jnp.int32, (tb, tb), 0)
    colb = lax.broadcasted_iota(jnp.int32, (tb, tb), 1)
    tri = jnp.where(colb >= (rowb // c) * c, jnp.where(rowb >= colb, 1.0, 0.0), 0.0).astype(BF16)
    cw = _dot_x2_rhs(tri, logw)
    w_inv = jnp.exp(-cw)
    last = jnp.concatenate([jnp.broadcast_to(cw[(ci + 1) * c - 1:(ci + 1) * c], (c, RW_W)) for ci in range(nck)],
                           axis=0)
    w_rest = jnp.exp(last - cw)
    beta = a * kk
    alpha_t = -kk * jnp.exp(cw - logw)
    r_t = r * jnp.exp(cw)
    beta_h = beta * w_inv
    k_h = k2 * w_inv
    beta_d = beta * w_rest
    k_d = k2 * w_rest

    ps = range(RW_HEADS // 2)
    cp = [(ci, p) for ci in range(nck) for p in ps]
    m = range(len(cp))
    rs = [slice(ci * c, (ci + 1) * c) for ci, _ in cp]
    sl = [slice(p * pw, (p + 1) * pw) for _, p in cp]
    v2 = [v[rs[i], sl[i]] for i in m]
    lhs = [jnp.concatenate([alpha_t[rs[i], sl[i]], r_t[rs[i], sl[i]]], axis=0).astype(BF16) for i in m]
    rhs = [jnp.concatenate([_pair_blockdiag(beta_h[rs[i], sl[i]], pair_mask),
                            _pair_blockdiag(k_h[rs[i], sl[i]], pair_mask)], axis=0) for i in m]
    big = [_dot_nt(lhs[i], rhs[i]) for i in m]
    a_ab = [jnp.where(strict2, big[i][:c, :pw], 0.0) for i in m]
    a_ak = [jnp.where(strict2, big[i][:c, pw:], 0.0) for i in m]
    a_rb = [jnp.where(incl2, big[i][c:, :pw], 0.0) for i in m]
    a_rk = [jnp.where(incl2, big[i][c:, pw:], 0.0) for i in m]
    t_inv = _inv_unit_lower(a_ab, eye, blk_mask, pair_mask)
    av = [_pair_dot(a_ak[i], v2[i], pair_mask) for i in m]
    u_const = [_pair_dot(t_inv[i], av[i], pair_mask) for i in m]
    lhs_s = [jnp.concatenate([_pair_dot(t_inv[i], alpha_t[rs[i], sl[i]], pair_mask).astype(BF16),
                              r_t[rs[i], sl[i]].astype(BF16)], axis=0) for i in m]
    a_r = [jnp.concatenate([a_rb[i], a_rk[i]], axis=1).astype(BF16) for i in m]
    bk_d = [jnp.concatenate([beta_d[rs[i], sl[i]], k_d[rs[i], sl[i]]], axis=0).astype(BF16) for i in m]
    s_cur = [s_ref[p] for p in ps]
    o_chunks = []
    for ci in range(nck):
        ix = [ci * len(ps) + p for p in ps]
        sd = [_dot_nt(lhs_s[ix[p]], s_cur[p].astype(BF16)) for p in ps]
        u = [sd[p][:c] + u_const[ix[p]] for p in ps]
        uv = [jnp.concatenate([_pair_blockdiag(u[p], pair_mask), _pair_blockdiag(v2[ix[p]], pair_mask)], axis=0)
              for p in ps]
        o_chunks.append(jnp.concatenate([sd[p][c:] + _dot(a_r[ix[p]], uv[p]) for p in ps], axis=1))
        w_last = jnp.exp(cw[(ci + 1) * c - 1:(ci + 1) * c])
        uvt = [jnp.concatenate([u[p], v2[ix[p]]], axis=0).astype(BF16) for p in ps]
        s_cur = [s_cur[p] * w_last[:, sl[p]] + jnp.where(pair_mask, _dot_tn(uvt[p], bk_d[ix[p]]), 0.0) for p in ps]
    for p in ps:
        s_ref[p] = s_cur[p]
    o = jnp.concatenate(o_chunks, axis=0)

    mean = _dot_x2_lhs(o, seg) * (1.0 / RW_N)
    dev = o - mean
    var = _dot_x2_lhs(dev * dev, seg) * (1.0 / RW_N)
    o = dev * lax.rsqrt(var + RW_GN_EPS) * lnw_ref[...] + lnb_ref[...]
    bonus = _dot_x2_lhs(r * k2 * rk_ref[...], seg) * v
    o_ref[0] = ((o + bonus) * g).astype(o_ref.dtype)


def _rwkv_call(z3, mu, w0, w2, a0, a2, g2, k_k, k_a, r_k, ln_w, ln_b):
    b, t, _ = z3.shape
    c = min(RW_TB, t)
    hid = lax.broadcasted_iota(jnp.int32, (RW_W, RW_W), 0) // RW_N
    seg = (hid == hid.T).astype(BF16)

    def vec(n):
        return pl.BlockSpec((1, n), lambda i, j: (0, 0))

    def mat(m, n):
        return pl.BlockSpec((m, n), lambda i, j: (0, 0))

    return pl.pallas_call(
        _rwkv_kernel,
        grid=(b, t // c),
        in_specs=[
            pl.BlockSpec((1, c, RW_COLS), lambda i, j: (i, j, RW_OFF // RW_COLS)),
            vec(RW_COLS), vec(RW_W), mat(RW_DECAY_LORA, RW_W), vec(RW_W), mat(RW_A_LORA, RW_W),
            mat(RW_GATE_LORA, RW_W), vec(RW_W), vec(RW_W), vec(RW_W), vec(RW_W), vec(RW_W),
            mat(RW_W, RW_W),
        ],
        out_specs=pl.BlockSpec((1, c, RW_W), lambda i, j: (i, j, 0)),
        out_shape=jax.ShapeDtypeStruct((b, t, RW_W), BF16),
        scratch_shapes=[pltpu.VMEM((RW_HEADS // 2, 2 * RW_N, 2 * RW_N), F32), pltpu.VMEM((1, RW_COLS), F32)],
        compiler_params=_cparams(("parallel", "arbitrary")),
        name="rwkv7_mixer",
    )(z3, mu.reshape(1, -1), w0.reshape(1, -1), w2, a0.reshape(1, -1), a2, g2, k_k.reshape(1, -1),
      k_a.reshape(1, -1), r_k.reshape(1, -1), ln_w.reshape(1, -1), ln_b.reshape(1, -1), seg)


def _merge_kernel(ohg_ref, oret_ref, orw_ref, zg_ref, x_ref, gate_ref, bhg_ref, bret_ref, brw_ref,
                  wout_ref, o_ref):
    d = x_ref.shape[1]
    y = _sigmoid(zg_ref[:, 0:d].astype(F32)) * _dot(ohg_ref[...], bhg_ref[...])
    y = y + _sigmoid(zg_ref[:, d:2 * d].astype(F32)) * _dot(oret_ref[...], bret_ref[...])
    y = y + _sigmoid(zg_ref[:, 2 * d:3 * d].astype(F32)) * _dot(orw_ref[...], brw_ref[...])
    o_ref[...] = x_ref[...] + gate_ref[0] * _dot(y.astype(BF16), wout_ref[...])


def _merge_call(o_hg, o_ret, o_rw, z2, x2, mod3, br_hg, br_ret, br_rw, w_out, seq, gate_blk, tm=512):
    n, d = x2.shape
    tpb = seq // tm

    def rows(w):
        return pl.BlockSpec((tm, w), lambda i: (i, 0))

    def full(m, k):
        return pl.BlockSpec((m, k), lambda i: (0, 0))

    return pl.pallas_call(
        _merge_kernel,
        grid=(n // tm,),
        in_specs=[
            rows(HG_W), rows(RET_W), rows(RW_W), rows(3 * d), rows(d),
            pl.BlockSpec((1, 1, d), lambda i: (i // tpb, 0, gate_blk)),
            full(HG_W, d), full(RET_W, d), full(RW_W, d), full(d, d),
        ],
        out_specs=rows(d),
        out_shape=jax.ShapeDtypeStruct((n, d), F32),
        compiler_params=_cparams(("parallel",)),
        name="merge_outproj",
    )(o_hg, o_ret, o_rw, z2, x2, mod3, br_hg, br_ret, br_rw, w_out)


def _pack_bf16_pairs(x):
    w = x.shape[1] // 2
    hi = pltpu.bitcast(x[:, :w].astype(BF16).astype(F32), jnp.uint32)
    lo = pltpu.bitcast(x[:, w:].astype(BF16).astype(F32), jnp.uint32)
    return pltpu.bitcast(hi | lax.shift_right_logical(lo, jnp.uint32(16)), jnp.int32)


def _unpack_bf16_pairs(p):
    u = pltpu.bitcast(p, jnp.uint32)
    hi = pltpu.bitcast(u & jnp.uint32(0xFFFF0000), F32)
    lo = pltpu.bitcast(lax.shift_left(u, jnp.uint32(16)), F32)
    return jnp.concatenate([hi, lo], axis=1)


def _route_kernel(x_ref, g_ref, scale_ref, shift_ref, rc_ref, hp_ref, eid_ref, wts_ref, cnt_ref):
    @pl.when(pl.program_id(0) == 0)
    def _():
        cnt_ref[...] = jnp.zeros_like(cnt_ref)

    h = _rms_mod(x_ref[...], g_ref[...], scale_ref[0], shift_ref[0])
    hp_ref[...] = _pack_bf16_pairs(h)
    tm = h.shape[0]
    lane = lax.broadcasted_iota(jnp.int32, (tm, LANES), 1)
    neg = -jnp.inf
    logits = _dot_x3(h, rc_ref[...])
    gl = jnp.where(lane < N_GROUPS, logits, neg)
    gmax = jnp.max(gl, axis=-1, keepdims=True)
    gidx = jnp.min(jnp.where(gl == gmax, lane, LANES), axis=-1, keepdims=True)
    gw = 1.0 / jnp.sum(jnp.exp(gl - gmax), axis=-1, keepdims=True)
    lo = N_GROUPS + gidx * EXPERTS_PER_GROUP
    el = jnp.where(lane >= lo, jnp.where(lane < lo + EXPERTS_PER_GROUP, logits, neg), neg)
    m1 = jnp.max(el, axis=-1, keepdims=True)
    l1 = jnp.min(jnp.where(el == m1, lane, LANES), axis=-1, keepdims=True)
    el2 = jnp.where(lane == l1, neg, el)
    m2 = jnp.max(el2, axis=-1, keepdims=True)
    l2 = jnp.min(jnp.where(el2 == m2, lane, LANES), axis=-1, keepdims=True)
    i1 = l1 - N_GROUPS
    i2 = l2 - N_GROUPS
    e2 = jnp.exp(m2 - m1)
    p1 = 1.0 / (1.0 + e2)
    p2 = e2 * p1
    oh1 = jnp.where(lane == i1, 1.0, 0.0)
    oh2 = jnp.where(lane == i2, 1.0, 0.0)
    row = lax.broadcasted_iota(jnp.int32, (tm, tm), 0)
    col = lax.broadcasted_iota(jnp.int32, (tm, tm), 1)
    earlier = jnp.where(row > col, 1.0, 0.0).astype(BF16)
    before = _dot(earlier, jnp.concatenate([oh1, oh2], axis=1).astype(BF16))
    tot1 = jnp.sum(oh1, axis=0, keepdims=True)
    carry = cnt_ref[...]
    r1 = jnp.sum(oh1 * (before[:, :LANES] + carry), axis=-1, keepdims=True).astype(jnp.int32)
    r2 = jnp.sum(oh2 * (before[:, LANES:] + (carry + tot1)), axis=-1, keepdims=True).astype(jnp.int32)
    cnt_ref[...] = carry + tot1 + jnp.sum(oh2, axis=0, keepdims=True)
    eid_ref[...] = jnp.where(lane == 0, i1, jnp.where(lane == 1, i2, jnp.where(lane == 2, r1,
                                                                             jnp.where(lane == 3, r2, 0))))
    wts_ref[...] = jnp.where(lane == 0, gw * p1, jnp.where(lane == 1, gw * p2, 0.0))


def _route_call(x2, gain, mod3, router_g, router_e, seq, scale_blk, shift_blk, tm=512):
    n, d = x2.shape
    tpb = seq // tm
    rc = jnp.pad(jnp.concatenate([router_g, router_e], axis=1), ((0, 0), (0, LANES - N_GROUPS - N_EXPERTS)))
    return pl.pallas_call(
        _route_kernel,
        grid=(n // tm,),
        in_specs=[
            pl.BlockSpec((tm, d), lambda i: (i, 0)),
            pl.BlockSpec((1, d), lambda i: (0, 0)),
            pl.BlockSpec((1, 1, d), lambda i: (i // tpb, 0, scale_blk)),
            pl.BlockSpec((1, 1, d), lambda i: (i // tpb, 0, shift_blk)),
            pl.BlockSpec((d, LANES), lambda i: (0, 0)),
        ],
        out_specs=[pl.BlockSpec((tm, d // 2), lambda i: (i, 0)), pl.BlockSpec((tm, LANES), lambda i: (i, 0)),
                   pl.BlockSpec((tm, LANES), lambda i: (i, 0)), pl.BlockSpec((1, LANES), lambda i: (0, 0))],
        out_shape=[jax.ShapeDtypeStruct((n, d // 2), jnp.int32), jax.ShapeDtypeStruct((n, LANES), jnp.int32),
                   jax.ShapeDtypeStruct((n, LANES), F32), jax.ShapeDtypeStruct((1, LANES), F32)],
        compiler_params=_cparams(("arbitrary",)),
        name="moe_route",
    )(x2, gain.reshape(1, d), mod3, mod3, rc)


SC_CORES = 2
SC_SUBCORES = 16
SC_WORKERS = SC_CORES * SC_SUBCORES
SC_ROWS = 32
SC_STREAMS = 4


def _sc_gather(table, idx):
    m = idx.shape[0]
    w = table.shape[1]
    per_worker = m // SC_WORKERS
    steps = per_worker // SC_ROWS
    assert per_worker * SC_WORKERS == m and steps * SC_ROWS == per_worker and steps % SC_STREAMS == 0
    mesh = plsc.VectorSubcoreMesh(core_axis_name="c", subcore_axis_name="s")
    ks = range(SC_STREAMS)

    def body(table_hbm, idx_hbm, out_hbm, idx_v, *rest):
        bufs, g_sems, w_sems = rest[:SC_STREAMS], rest[SC_STREAMS:2 * SC_STREAMS], rest[2 * SC_STREAMS:]
        wid = lax.axis_index("s") * SC_CORES + lax.axis_index("c")
        pltpu.sync_copy(idx_hbm.at[wid], idx_v)

        @pl.loop(0, steps, step=SC_STREAMS)
        def _(j):
            row0 = wid * per_worker + j * SC_ROWS
            gathers = [pltpu.async_copy(table_hbm.at[idx_v.at[j + q]], bufs[q], g_sems[q]) for q in ks]
            writes = []
            for q in ks:
                gathers[q].wait()
                writes.append(pltpu.async_copy(bufs[q], out_hbm.at[pl.ds(row0 + q * SC_ROWS, SC_ROWS)], w_sems[q]))
            for q in ks:
                writes[q].wait()

    return pl.kernel(
        body,
        out_type=jax.ShapeDtypeStruct((m, w), table.dtype),
        mesh=mesh,
        scratch_types=[pltpu.VMEM((steps, SC_ROWS), jnp.int32)] + [pltpu.VMEM((SC_ROWS, w), table.dtype)] * SC_STREAMS
        + [pltpu.SemaphoreType.DMA] * (2 * SC_STREAMS),
        name="sc_row_gather",
    )(table, idx.reshape(SC_WORKERS, steps, SC_ROWS))


def _sc_scatter2(rows, idx0, idx1, p):
    n, w = rows.shape
    per_worker = n // SC_WORKERS
    steps = per_worker // SC_ROWS
    assert per_worker * SC_WORKERS == n and steps * SC_ROWS == per_worker and steps % SC_STREAMS == 0
    mesh = plsc.VectorSubcoreMesh(core_axis_name="c", subcore_axis_name="s")
    ks = range(SC_STREAMS)

    def body(rows_hbm, i0_hbm, i1_hbm, out_hbm, i0_v, i1_v, *rest):
        bufs, r_sems = rest[:SC_STREAMS], rest[SC_STREAMS:2 * SC_STREAMS]
        s0_sems, s1_sems = rest[2 * SC_STREAMS:3 * SC_STREAMS], rest[3 * SC_STREAMS:]
        wid = lax.axis_index("s") * SC_CORES + lax.axis_index("c")
        pltpu.sync_copy(i0_hbm.at[wid], i0_v)
        pltpu.sync_copy(i1_hbm.at[wid], i1_v)

        @pl.loop(0, steps, step=SC_STREAMS)
        def _(j):
            row0 = wid * per_worker + j * SC_ROWS
            reads = [pltpu.async_copy(rows_hbm.at[pl.ds(row0 + q * SC_ROWS, SC_ROWS)], bufs[q], r_sems[q]) for q in ks]
            writes = []
            for q in ks:
                reads[q].wait()
                writes.append(pltpu.async_copy(bufs[q], out_hbm.at[i0_v.at[j + q]], s0_sems[q]))
                writes.append(pltpu.async_copy(bufs[q], out_hbm.at[i1_v.at[j + q]], s1_sems[q]))
            for wr in writes:
                wr.wait()

    index_block = pltpu.VMEM((steps, SC_ROWS), jnp.int32)
    return pl.kernel(
        body,
        out_type=jax.ShapeDtypeStruct((p, w), rows.dtype),
        mesh=mesh,
        scratch_types=[index_block, index_block] + [pltpu.VMEM((SC_ROWS, w), rows.dtype)] * SC_STREAMS
        + [pltpu.SemaphoreType.DMA] * (3 * SC_STREAMS),
        name="sc_row_scatter",
    )(rows, idx0.reshape(SC_WORKERS, steps, SC_ROWS), idx1.reshape(SC_WORKERS, steps, SC_ROWS))


MOE_TM = 512


def _gexperts_kernel(te_ref, tv_ref, nu_ref, xs_ref, w1_ref, w3_ref, w2_ref, ys_ref, w1b_ref, w3b_ref, w2b_ref):
    i = pl.program_id(0)

    @pl.when((i == 0) | (te_ref[i] != te_ref[jnp.maximum(i - 1, 0)]))
    def _():
        w1b_ref[...] = w1_ref[0].astype(BF16)
        w3b_ref[...] = w3_ref[0].astype(BF16)
        w2b_ref[...] = w2_ref[0].astype(BF16)

    @pl.when(i < nu_ref[0])
    def _():
        rid = lax.broadcasted_iota(jnp.int32, xs_ref.shape, 0)
        xb = _unpack_bf16_pairs(jnp.where(rid < tv_ref[i], xs_ref[...], 0)).astype(BF16)
        act = (_silu(_dot(xb, w1b_ref[...])) * _dot(xb, w3b_ref[...])).astype(BF16)
        ys_ref[...] = _pack_bf16_pairs(_dot(act, w2b_ref[...]))


def _gexperts_call(xs, tile_expert, tile_valid, n_used, w1, w3, w2):
    p, half = xs.shape
    ne, d, de = w1.shape
    nt = p // MOE_TM

    def rows(i, te, tv, nu):
        return (jnp.minimum(i, nu[0] - 1), 0)

    def wsel(i, te, tv, nu):
        return (te[i], 0, 0)

    return pl.pallas_call(
        _gexperts_kernel,
        grid_spec=pltpu.PrefetchScalarGridSpec(
            num_scalar_prefetch=3,
            grid=(nt,),
            in_specs=[
                pl.BlockSpec((MOE_TM, half), rows),
                pl.BlockSpec((1, d, de), wsel),
                pl.BlockSpec((1, d, de), wsel),
                pl.BlockSpec((1, de, d), wsel),
            ],
            out_specs=pl.BlockSpec((MOE_TM, half), rows),
            scratch_shapes=[pltpu.VMEM((d, de), BF16), pltpu.VMEM((d, de), BF16), pltpu.VMEM((de, d), BF16)],
        ),
        out_shape=jax.ShapeDtypeStruct((p, half), jnp.int32),
        compiler_params=_cparams(("arbitrary",)),
        name="moe_experts",
    )(tile_expert, tile_valid, n_used, xs, w1, w3, w2)


def _combine_kernel(y0_ref, y1_ref, wts_ref, x_ref, gate_ref, fg_ref, o_ref, *, final_norm):
    wts = wts_ref[...]
    moe = wts[:, 0:1] * _unpack_bf16_pairs(y0_ref[...]) + wts[:, 1:2] * _unpack_bf16_pairs(y1_ref[...])
    xn = x_ref[...] + gate_ref[0] * moe
    if final_norm:
        xn = xn * lax.rsqrt(jnp.mean(xn * xn, axis=-1, keepdims=True) + NORM_EPS) * fg_ref[...]
    o_ref[...] = xn


def _combine_call(yg, wts, x2, mod3, final_g, seq, gate_blk, final_norm, tm=512):
    n, d = x2.shape
    tpb = seq // tm
    slot1 = n // tm
    return pl.pallas_call(
        functools.partial(_combine_kernel, final_norm=final_norm),
        grid=(n // tm,),
        in_specs=[
            pl.BlockSpec((tm, d // 2), lambda i: (i, 0)),
            pl.BlockSpec((tm, d // 2), lambda i: (i + slot1, 0)),
            pl.BlockSpec((tm, LANES), lambda i: (i, 0)),
            pl.BlockSpec((tm, d), lambda i: (i, 0)),
            pl.BlockSpec((1, 1, d), lambda i: (i // tpb, 0, gate_blk)),
            pl.BlockSpec((1, d), lambda i: (0, 0)),
        ],
        out_specs=pl.BlockSpec((tm, d), lambda i: (i, 0)),
        out_shape=jax.ShapeDtypeStruct((n, d), F32),
        compiler_params=_cparams(("parallel",)),
        name="moe_combine",
    )(yg, yg, wts, x2, mod3, final_g.reshape(1, d))


def _pos_kernel(eid_ref, ts_ref, p0_ref, p1_ref):
    eid = eid_ref[...]
    tm = eid.shape[0]
    lane = lax.broadcasted_iota(jnp.int32, (tm, LANES), 1)
    sub = lax.broadcasted_iota(jnp.int32, (tm, LANES), 0) % LANES
    for slot, out_ref in ((0, p0_ref), (1, p1_ref)):
        first_row = jnp.sum(jnp.where(lane == eid[:, slot:slot + 1], ts_ref[...], 0), axis=-1, keepdims=True)
        pos = first_row + eid[:, slot + 2:slot + 3]
        out_ref[...] = jnp.sum(jnp.where(lane == sub, pos, 0).reshape(tm // LANES, LANES, LANES), axis=1)


def _pos_call(eid, first_rows, tm=1024):
    n = eid.shape[0]
    out = jax.ShapeDtypeStruct((n // LANES, LANES), jnp.int32)
    p0, p1 = pl.pallas_call(
        _pos_kernel,
        grid=(n // tm,),
        in_specs=[pl.BlockSpec((tm, LANES), lambda i: (i, 0)), pl.BlockSpec((1, LANES), lambda i: (0, 0))],
        out_specs=[pl.BlockSpec((tm // LANES, LANES), lambda i: (i, 0))] * 2,
        out_shape=[out, out],
        compiler_params=_cparams(("parallel",)),
        name="moe_positions",
    )(eid, first_rows)
    return p0.reshape(n), p1.reshape(n)


def _moe_plan(eid, counts_f):
    n = eid.shape[0]
    nt = (2 * n) // MOE_TM + N_EXPERTS
    counts = counts_f[0, :N_EXPERTS].astype(jnp.int32)
    tiles = (counts + MOE_TM - 1) // MOE_TM
    tile_end = jnp.cumsum(tiles)
    tile_start = tile_end - tiles
    n_used = tile_end[-1:]
    tile_iota = jnp.arange(nt, dtype=jnp.int32)
    tile_expert = jnp.sum(jnp.minimum(tile_iota, n_used - 1)[:, None] >= tile_end[None, :], axis=1, dtype=jnp.int32)
    own = tile_expert[:, None] == jnp.arange(N_EXPERTS, dtype=jnp.int32)[None, :]
    count_t = jnp.sum(jnp.where(own, counts[None, :], 0), axis=1)
    start_t = jnp.sum(jnp.where(own, tile_start[None, :], 0), axis=1)
    tile_valid = jnp.clip(count_t - (tile_iota - start_t) * MOE_TM, 0, MOE_TM)
    first_rows = jnp.pad(tile_start * MOE_TM, (0, LANES - N_EXPERTS)).reshape(1, LANES)
    pos0, pos1 = _pos_call(eid, first_rows)
    return pos0, pos1, tile_expert, tile_valid, n_used


def kernel(x, c, positions, ada_w, ada_b, norm1_g, norm2_g, w_in, hg_lb_table, hg_norm_w, rw_mu, rw_w0, rw_w2,
           rw_a0, rw_a2, rw_g2, rw_k_k, rw_k_a, rw_r_k, rw_ln_w, rw_ln_b, br_hg, br_ret, br_rw, w_out,
           router_g, router_e, moe_w1, moe_w3, moe_w2, final_g):
    b, t, d = x.shape
    depth = ada_w.shape[0]
    n = b * t
    assert w_in.shape[2] == IN_COLS and d == 1024

    lb_p = jax.nn.softmax(hg_lb_table.astype(F32), axis=0)
    lower_bounds = jnp.cumsum(lb_p, axis=0) - lb_p[0]

    mod = _mod_call(c, ada_w, ada_b)
    cos2, sin2 = _rope_call(positions, RET_DK)
    w_perm = _wprep_call(w_in)
    x2 = x.reshape(n, d)
    for l in range(depth):
        mod3 = mod[l].reshape(b, 1, 6 * d)
        z2 = _inproj_call(x2, norm1_g[l], mod3, w_perm, l, t, scale_blk=1, shift_blk=0)
        z3 = z2.reshape(b, t, IN_COLS)
        o_hg = _hgrn2_call(z3, lower_bounds[l], hg_norm_w[l])
        o_ret = _ret_call(z3, cos2, sin2)
        o_rw = _rwkv_call(z3, rw_mu[l], rw_w0[l], rw_w2[l], rw_a0[l], rw_a2[l], rw_g2[l], rw_k_k[l],
                          rw_k_a[l], rw_r_k[l], rw_ln_w[l], rw_ln_b[l])
        x2 = _merge_call(o_hg.reshape(n, HG_W), o_ret.reshape(n, RET_W), o_rw.reshape(n, RW_W), z2, x2, mod3,
                         br_hg[l].astype(BF16), br_ret[l].astype(BF16), br_rw[l].astype(BF16),
                         w_out[l].astype(BF16), t, gate_blk=2)
        hp, eid, wts, counts = _route_call(x2, norm2_g[l], mod3, router_g[l], router_e[l], t, scale_blk=4,
                                           shift_blk=3)
        pos0, pos1, tile_expert, tile_valid, n_used = _moe_plan(eid, counts)
        xs = _sc_scatter2(hp, pos0, pos1, (2 * n // MOE_TM + N_EXPERTS) * MOE_TM)
        ys = _gexperts_call(xs, tile_expert + l * N_EXPERTS, tile_valid, n_used,
                            moe_w1.reshape((-1,) + moe_w1.shape[2:]), moe_w3.reshape((-1,) + moe_w3.shape[2:]),
                            moe_w2.reshape((-1,) + moe_w2.shape[2:]))
        yg = _sc_gather(ys, jnp.concatenate([pos0, pos1]))
        x2 = _combine_call(yg, wts, x2, mod3, final_g, t, gate_blk=5, final_norm=(l == depth - 1))
    return x2.reshape(b, t, d)
```

```python
import functools

import jax
import jax.numpy as jnp
from jax import lax
from jax.experimental import pallas as pl
from jax.experimental.pallas import tpu as pltpu
from jax.experimental.pallas import tpu_sc as plsc

F32 = jnp.float32
BF16 = jnp.bfloat16
HIGHEST = lax.Precision.HIGHEST

HG_HEADS = 4
HG_DK = 128
HG_W = HG_HEADS * HG_DK
RET_HEADS = 4
RET_DK = 128
RET_W = RET_HEADS * RET_DK
RW_HEADS = 8
RW_N = 64
RW_W = RW_HEADS * RW_N
RW_DECAY_LORA = 64
RW_A_LORA = 64
RW_GATE_LORA = 128
RW_COLS = 3 * RW_W + RW_DECAY_LORA + RW_A_LORA + RW_GATE_LORA
RW_GN_EPS = 64e-5
N_GROUPS = 4
EXPERTS_PER_GROUP = 8
N_EXPERTS = N_GROUPS * EXPERTS_PER_GROUP
ROPE_THETA = 10000.0
NORM_EPS = 1e-6

LANES = 128
LOG2E = 1.4426950408889634
VMEM_LIMIT = 56 * 1024 * 1024

GATE_OFF = 0
HG_OFF = 3 * 1024
RET_OFF = HG_OFF + 4 * HG_W
RW_OFF = RET_OFF + 4 * RET_W
IN_COLS = RW_OFF + RW_COLS

HG_CHUNK = 64
HG_SUB = 16
HG_SAFE_SPAN = 60.0
RW_CHUNK = 64
RW_BLK = 16
RW_TB = 256
Z_DTYPE = BF16


def _cparams(sem):
    return pltpu.CompilerParams(dimension_semantics=sem, vmem_limit_bytes=VMEM_LIMIT)


def _dot(a, b, precision=None):
    return jnp.dot(a, b, preferred_element_type=F32, precision=precision)


def _dot_nt(a, b, precision=None):
    return lax.dot_general(a, b, (((1,), (1,)), ((), ())), preferred_element_type=F32, precision=precision)


def _dot_tn(a, b, precision=None):
    return lax.dot_general(a, b, (((0,), (0,)), ((), ())), preferred_element_type=F32, precision=precision)


def _split_bf16(x):
    hi = x.astype(BF16)
    return hi, (x - hi.astype(F32)).astype(BF16)


def _dot_x3(a, b):
    ah, al = _split_bf16(a)
    bh, bl = _split_bf16(b)
    return _dot(ah, bh) + _dot(ah, bl) + _dot(al, bh)


def _dot_x2_lhs(a, b_exact):
    ah, al = _split_bf16(a)
    return _dot(ah, b_exact) + _dot(al, b_exact)


def _dot_x2_rhs(a_exact, b):
    bh, bl = _split_bf16(b)
    return _dot(a_exact, bh) + _dot(a_exact, bl)


def _sigmoid(x):
    return 0.5 * jnp.tanh(0.5 * x) + 0.5


def _silu(x):
    return x * _sigmoid(x)


def _rms_mod(x, gain, scale, shift):
    y = x * lax.rsqrt(jnp.mean(x * x, axis=-1, keepdims=True) + NORM_EPS)
    return (y * gain) * (1.0 + scale) + shift


def _mod_kernel(c_ref, w_ref, b_ref, o_ref):
    c = c_ref[...]
    o_ref[0] = _dot(_silu(c), w_ref[0], HIGHEST) + b_ref[0]


def _mod_call(c, ada_w, ada_b):
    depth, d, d6 = ada_w.shape
    b = c.shape[0]
    nblk = d6 // d
    return pl.pallas_call(
        _mod_kernel,
        grid=(depth, nblk),
        in_specs=[
            pl.BlockSpec((b, d), lambda l, j: (0, 0)),
            pl.BlockSpec((1, d, d), lambda l, j: (l, 0, j)),
            pl.BlockSpec((1, 1, d), lambda l, j: (l, 0, j)),
        ],
        out_specs=pl.BlockSpec((1, b, d), lambda l, j: (l, 0, j)),
        out_shape=jax.ShapeDtypeStruct((depth, b, d6), F32),
        compiler_params=_cparams(("parallel", "parallel")),
        name="adaln_mod",
    )(c, ada_w, ada_b.reshape(depth, 1, d6))


def _rope_kernel(pos_ref, freq_ref, sign_ref, cos_ref, sin_ref):
    ang = pos_ref[0].astype(F32) * freq_ref[...]
    cos_ref[0] = jnp.cos(ang)
    sin_ref[0] = jnp.sin(ang) * sign_ref[...]


def _rope_call(positions, d):
    b, t = positions.shape
    tb = min(t, 512)
    inv_freq = ROPE_THETA ** (-jnp.arange(0, d, 2, dtype=F32) / d)
    freq2 = jnp.concatenate([inv_freq, inv_freq]).reshape(1, d)
    sign2 = jnp.concatenate([-jnp.ones((d // 2,), F32), jnp.ones((d // 2,), F32)]).reshape(1, d)
    out = jax.ShapeDtypeStruct((b, t, d), F32)
    return pl.pallas_call(
        _rope_kernel,
        grid=(b, t // tb),
        in_specs=[
            pl.BlockSpec((1, tb, 1), lambda i, j: (i, j, 0)),
            pl.BlockSpec((1, d), lambda i, j: (0, 0)),
            pl.BlockSpec((1, d), lambda i, j: (0, 0)),
        ],
        out_specs=[pl.BlockSpec((1, tb, d), lambda i, j: (i, j, 0))] * 2,
        out_shape=[out, out],
        compiler_params=_cparams(("parallel", "parallel")),
        name="rope_tables",
    )(positions.reshape(b, t, 1), freq2, sign2)


W_BLK = 256


def _wprep_kernel(w_ref, o_ref):
    o_ref[...] = w_ref[...].astype(o_ref.dtype)


def _wprep_call(w_in):
    depth, d, cols = w_in.shape
    nblk = cols // W_BLK
    first = (cols - 3 * d) // W_BLK
    return pl.pallas_call(
        _wprep_kernel,
        grid=(depth, nblk),
        in_specs=[pl.BlockSpec((1, d, W_BLK), lambda l, j: (l, 0, (j + first) % nblk))],
        out_specs=pl.BlockSpec((1, d, W_BLK), lambda l, j: (l, 0, j)),
        out_shape=jax.ShapeDtypeStruct((depth, d, cols), BF16),
        compiler_params=_cparams(("parallel", "parallel")),
        name="w_in_layout",
    )(w_in)


def _inproj_kernel(x_ref, g_ref, scale_ref, shift_ref, w_ref, o_ref, h_ref):
    @pl.when(pl.program_id(1) == 0)
    def _():
        h = _rms_mod(x_ref[...], g_ref[...], scale_ref[0], shift_ref[0])
        h_ref[...] = h.astype(BF16)

    o_ref[...] = _dot(h_ref[...], w_ref[0]).astype(o_ref.dtype)


def _inproj_call(x2, gain, mod3, w_bf16, layer, seq, scale_blk, shift_blk, tm=2048, tn=1792):
    n, d = x2.shape
    cols = w_bf16.shape[2]
    tpb = seq // tm
    return pl.pallas_call(
        _inproj_kernel,
        grid=(n // tm, cols // tn),
        in_specs=[
            pl.BlockSpec((tm, d), lambda i, j: (i, 0)),
            pl.BlockSpec((1, d), lambda i, j: (0, 0)),
            pl.BlockSpec((1, 1, d), lambda i, j: (i // tpb, 0, scale_blk)),
            pl.BlockSpec((1, 1, d), lambda i, j: (i // tpb, 0, shift_blk)),
            pl.BlockSpec((1, d, tn), lambda i, j: (layer, 0, j)),
        ],
        out_specs=pl.BlockSpec((tm, tn), lambda i, j: (i, j)),
        out_shape=jax.ShapeDtypeStruct((n, cols), Z_DTYPE),
        scratch_shapes=[pltpu.VMEM((tm, d), BF16)],
        compiler_params=_cparams(("parallel", "arbitrary")),
        name="norm_inproj",
    )(x2, gain.reshape(1, d), mod3, mod3, w_bf16)


def _hgrn2_block(zs, lbs, nw, sts, factored):
    hs = range(len(zs))
    tb = zs[0][0].shape[0]
    c, sub = HG_CHUNK, HG_SUB
    nc, ns, nb = tb // c, c // sub, tb // sub
    f = [lbs[h] + (1.0 - lbs[h]) * _sigmoid(zs[h][1]) for h in hs]
    logf = [jnp.log(jnp.maximum(f[h], 1e-30)) for h in hs]
    q = [_silu(zs[h][0]) * (HG_DK ** -0.5) for h in hs]
    k = [1.0 - f[h] for h in hs]
    v = [zs[h][2] for h in hs]
    v_b = [v[h].astype(BF16) for h in hs]
    row = lax.broadcasted_iota(jnp.int32, (tb, tb), 0)
    col = lax.broadcasted_iota(jnp.int32, (tb, tb), 1)
    tri = jnp.where(col >= (row // c) * c, jnp.where(row >= col, 1.0, 0.0), 0.0).astype(BF16)
    cum = [_dot_x2_rhs(tri, logf[h]) for h in hs]
    cum3 = [cum[h].reshape(nb, sub, HG_DK) for h in hs]
    ref3 = [cum3[h][:, 0:1, :] - logf[h].reshape(nb, sub, HG_DK)[:, 0:1, :] for h in hs]
    span = functools.reduce(jnp.maximum, [jnp.max(ref3[h] - cum3[h][:, sub - 1:sub, :]) for h in hs])
    qe = [(q[h] * jnp.exp(cum[h])).astype(BF16) for h in hs]

    offd = [(h, ci * c, ci * c + sub * i) for h in hs for ci in range(nc) for i in range(1, ns)]
    base = [cum[h][lo - 1:lo] for h, _, lo in offd]
    qt = [(q[h][lo:lo + sub] * jnp.exp(cum[h][lo:lo + sub] - base[j])).astype(BF16)
          for j, (h, _, lo) in enumerate(offd)]
    kt = [(k[h][r0:lo] * jnp.exp(base[j] - cum[h][r0:lo])).astype(BF16) for j, (h, r0, lo) in enumerate(offd)]
    a = [_dot_nt(qt[j], kt[j]).astype(BF16) for j in range(len(offd))]
    av = {(h, lo): _dot(a[j], v_b[h][r0:lo]) for j, (h, r0, lo) in enumerate(offd)}

    cs = [slice(ci * c, (ci + 1) * c) for ci in range(nc)]
    hc = [(h, ci) for h in hs for ci in range(nc)]
    last = {(h, ci): cum[h][(ci + 1) * c - 1:(ci + 1) * c] for h, ci in hc}
    kd = {(h, ci): (k[h][cs[ci]] * jnp.exp(last[h, ci] - cum[h][cs[ci]])).astype(BF16) for h, ci in hc}
    inc = {(h, ci): _dot_tn(v_b[h][cs[ci]], kd[h, ci]) for h, ci in hc}
    s_in = {(h, 0): sts[h] for h in hs}
    for ci in range(nc):
        for h in hs:
            s_in[h, ci + 1] = s_in[h, ci] * jnp.exp(last[h, ci]) + inc[h, ci]
    o_inter = {(h, ci): _dot_nt(qe[h][cs[ci]], s_in[h, ci].astype(BF16)) for h, ci in hc}

    if factored:
        qf = [(q[h] * jnp.exp(cum3[h] - ref3[h]).reshape(tb, HG_DK)).astype(BF16) for h in hs]
        kf = [(k[h] * jnp.exp(ref3[h] - cum3[h]).reshape(tb, HG_DK)).astype(BF16) for h in hs]
        rc = lax.broadcasted_iota(jnp.int32, (c, c), 0)
        cc = lax.broadcasted_iota(jnp.int32, (c, c), 1)
        keep = (rc >= cc) & (rc // sub == cc // sub)
        a_d = {(h, ci): jnp.where(keep, _dot_nt(qf[h][cs[ci]], kf[h][cs[ci]]), 0.0).astype(BF16) for h, ci in hc}
        dg = {(h, ci): _dot(a_d[h, ci], v_b[h][cs[ci]]) for h, ci in hc}
        diag = [jnp.concatenate([dg[h, ci] for ci in range(nc)], axis=0) for h in hs]
    else:
        gb = 4
        trow = lax.broadcasted_iota(jnp.int32, (gb, sub, HG_DK), 1)
        diag = []
        for h in hs:
            c2 = cum[h] * LOG2E
            ks2 = c2 - jnp.log2(k[h])
            parts = []
            for g0 in range(0, nb, gb):
                rws = slice(g0 * sub, (g0 + gb) * sub)
                c23, ks23, q3, v3 = (x[rws].reshape(gb, sub, HG_DK) for x in (c2, ks2, q[h], v[h]))
                acc = jnp.zeros((gb, sub, HG_DK), F32)
                for s in range(sub):
                    e = jnp.exp2(jnp.where(trow >= s, c23 - ks23[:, s:s + 1, :], -jnp.inf))
                    a_col = jnp.sum(q3 * e, axis=-1, keepdims=True)
                    acc = acc + a_col * v3[:, s:s + 1, :]
                parts.append(acc.reshape(gb * sub, HG_DK))
            diag.append(jnp.concatenate(parts, axis=0))

    outs = []
    for h in hs:
        pieces = []
        for ci in range(nc):
            for i in range(ns):
                lo = ci * c + sub * i
                piece = o_inter[h, ci][sub * i:sub * (i + 1)] + diag[h][lo:lo + sub]
                pieces.append(piece + av[h, lo] if i > 0 else piece)
        o = jnp.concatenate(pieces, axis=0)
        o = o * lax.rsqrt(jnp.mean(o * o, axis=-1, keepdims=True) + NORM_EPS)
        outs.append(o * nw * _silu(zs[h][3]))
    return outs, [s_in[h, nc] for h in hs], span


def _hgrn2_kernel(zq_ref, zf_ref, zi_ref, zg_ref, lb_ref, nw_ref, o_ref, st_ref):
    @pl.when(pl.program_id(1) == 0)
    def _():
        st_ref[...] = jnp.zeros_like(st_ref)

    hs = range(HG_HEADS)
    sl = [slice(h * HG_DK, (h + 1) * HG_DK) for h in hs]

    def run(factored):
        zs = [tuple(r[0, :, sl[h]].astype(F32) for r in (zq_ref, zf_ref, zi_ref, zg_ref)) for h in hs]
        outs, sts, span = _hgrn2_block(zs, [lb_ref[:, sl[h]] for h in hs], nw_ref[...],
                                       [st_ref[h] for h in hs], factored)
        return jnp.concatenate(outs, axis=1), sts, span

    st_old = [st_ref[h] for h in hs]
    o, st_new, span = run(True)
    for h in hs:
        st_ref[h] = st_new[h]
    o_ref[0] = o.astype(o_ref.dtype)

    @pl.when(span > HG_SAFE_SPAN)
    def _():
        for h in hs:
            st_ref[h] = st_old[h]
        o2, st2, _ = run(False)
        for h in hs:
            st_ref[h] = st2[h]
        o_ref[0] = o2.astype(o_ref.dtype)


def _hgrn2_call(z3, lower_bound, norm_w, tb=256):
    b, t, _ = z3.shape
    tb = min(tb, t)
    base = HG_OFF // HG_W

    def zspec(part):
        return pl.BlockSpec((1, tb, HG_W), lambda i, j: (i, j, base + part))

    return pl.pallas_call(
        _hgrn2_kernel,
        grid=(b, t // tb),
        in_specs=[
            zspec(0), zspec(1), zspec(2), zspec(3),
            pl.BlockSpec((1, HG_W), lambda i, j: (0, 0)),
            pl.BlockSpec((1, LANES), lambda i, j: (0, 0)),
        ],
        out_specs=pl.BlockSpec((1, tb, HG_W), lambda i, j: (i, j, 0)),
        out_shape=jax.ShapeDtypeStruct((b, t, HG_W), BF16),
        scratch_shapes=[pltpu.VMEM((HG_HEADS, HG_DK, HG_DK), F32)],
        compiler_params=_cparams(("parallel", "arbitrary")),
        name="hgrn2_mixer",
    )(z3, z3, z3, z3, lower_bound.reshape(1, HG_W), norm_w.reshape(1, HG_DK))


def _ret_kernel(zq_ref, zk_ref, zv_ref, zg_ref, cos_ref, sin_ref, o_ref, st_ref, dmask_ref, *, chunk):
    hs = range(RET_HEADS)
    sl = [slice(h * RET_DK, (h + 1) * RET_DK) for h in hs]
    lg = [jnp.log(jnp.full((1, 1), 1.0 - 2.0 ** (-5.0 - h), F32)) for h in hs]

    @pl.when(pl.program_id(1) == 0)
    def _():
        st_ref[...] = jnp.zeros_like(st_ref)
        row = lax.broadcasted_iota(jnp.int32, (chunk, chunk), 0)
        col = lax.broadcasted_iota(jnp.int32, (chunk, chunk), 1)
        rel = (row - col).astype(F32)
        for h in hs:
            dmask_ref[h] = jnp.where(rel >= 0.0, jnp.exp(jnp.maximum(rel, 0.0) * lg[h]), 0.0)

    cos2 = cos_ref[0]
    sin2 = sin_ref[0]
    half = RET_DK // 2

    def rope(z):
        return z * cos2 + pltpu.roll(z, half, 1) * sin2

    tcol = lax.broadcasted_iota(jnp.int32, (chunk, 1), 0).astype(F32)
    q = [rope(zq_ref[0, :, sl[h]].astype(F32)) * (RET_DK ** -0.5) for h in hs]
    k = [rope(zk_ref[0, :, sl[h]].astype(F32)) for h in hs]
    v_b = [zv_ref[0, :, sl[h]].astype(BF16) for h in hs]
    st = [st_ref[h] for h in hs]
    scores = [(_dot_nt(q[h].astype(BF16), k[h].astype(BF16)) * dmask_ref[h]).astype(BF16) for h in hs]
    qx = [(q[h] * jnp.exp((tcol + 1.0) * lg[h])).astype(BF16) for h in hs]
    kz = [(k[h] * jnp.exp((chunk - 1.0 - tcol) * lg[h])).astype(BF16) for h in hs]
    o = [_dot(scores[h], v_b[h]) + _dot_nt(qx[h], st[h].astype(BF16)) for h in hs]
    for h in hs:
        st_ref[h] = st[h] * jnp.exp(chunk * lg[h]) + _dot_tn(v_b[h], kz[h])
    o = [o[h] * lax.rsqrt(jnp.mean(o[h] * o[h], axis=-1, keepdims=True) + NORM_EPS) for h in hs]
    o_ref[0] = (jnp.concatenate(o, axis=1) * _silu(zg_ref[0].astype(F32))).astype(o_ref.dtype)


def _ret_call(z3, cos2, sin2, chunk=256):
    b, t, _ = z3.shape
    chunk = min(chunk, t)
    base = RET_OFF // RET_W

    def zspec(part):
        return pl.BlockSpec((1, chunk, RET_W), lambda i, j: (i, j, base + part))

    tab = pl.BlockSpec((1, chunk, RET_DK), lambda i, j: (i, j, 0))
    return pl.pallas_call(
        functools.partial(_ret_kernel, chunk=chunk),
        grid=(b, t // chunk),
        in_specs=[zspec(0), zspec(1), zspec(2), zspec(3), tab, tab],
        out_specs=pl.BlockSpec((1, chunk, RET_W), lambda i, j: (i, j, 0)),
        out_shape=jax.ShapeDtypeStruct((b, t, RET_W), BF16),
        scratch_shapes=[pltpu.VMEM((RET_HEADS, RET_DK, RET_DK), F32), pltpu.VMEM((RET_HEADS, chunk, chunk), F32)],
        compiler_params=_cparams(("parallel", "arbitrary")),
        name="retention_mixer",
    )(z3, z3, z3, z3, cos2, sin2)


def _pair_blockdiag(y, pair_mask):
    return jnp.where(pair_mask, jnp.concatenate([y, y], axis=0), 0.0).astype(BF16)


def _pair_dot(x, y, pair_mask):
    return _dot(x.astype(BF16), _pair_blockdiag(y, pair_mask))


def _inv_unit_lower(a, eye, blk_mask, pair_mask):
    c = a[0].shape[0]
    m = range(len(a))
    a_bd = [jnp.where(blk_mask, a[i], 0.0) for i in m]
    a_off = [a[i] - a_bd[i] for i in m]
    a2 = [_pair_dot(a_bd[i], a_bd[i], pair_mask) for i in m]
    p = [eye + a_bd[i] for i in m]
    r = [_pair_dot(jnp.concatenate([p[i], a2[i]], axis=0), a2[i], pair_mask) for i in m]
    p = [p[i] + r[i][:c] for i in m]
    a4 = [r[i][c:] for i in m]
    r = [_pair_dot(jnp.concatenate([p[i], a4[i]], axis=0), a4[i], pair_mask) for i in m]
    p = [p[i] + r[i][:c] for i in m]
    a8 = [r[i][c:] for i in m]
    t_bd = [p[i] + _pair_dot(p[i], a8[i], pair_mask) for i in m]
    n = [_pair_dot(t_bd[i], a_off[i], pair_mask) for i in m]
    n2 = [_pair_dot(n[i], n[i], pair_mask) for i in m]
    z = [t_bd[i] + _pair_dot(n[i], t_bd[i], pair_mask) for i in m]
    return [z[i] + _pair_dot(n2[i], z[i], pair_mask) for i in m]


def _rwkv_kernel(z_ref, mu_ref, w0_ref, w2_ref, a0_ref, a2_ref, g2_ref, kk_ref, ka_ref, rk_ref,
                 lnw_ref, lnb_ref, seg_ref, o_ref, s_ref, prev_ref):
    c = RW_CHUNK
    tb = z_ref.shape[1]
    nck = tb // c

    @pl.when(pl.program_id(1) == 0)
    def _():
        s_ref[...] = jnp.zeros_like(s_ref)
        prev_ref[...] = jnp.zeros_like(prev_ref)

    z = z_ref[0].astype(F32)
    rows = lax.broadcasted_iota(jnp.int32, (tb, 1), 0)
    z_prev = jnp.where(rows == 0, prev_ref[...], pltpu.roll(z, 1, 0))
    prev_ref[...] = z[tb - 1:tb]
    zs = z + mu_ref[...] * (z_prev - z)
    r = zs[:, 0:RW_W]
    k = zs[:, RW_W:2 * RW_W]
    v = zs[:, 2 * RW_W:3 * RW_W]
    off = 3 * RW_W
    w_lo = zs[:, off:off + RW_DECAY_LORA]
    a_lo = zs[:, off + RW_DECAY_LORA:off + RW_DECAY_LORA + RW_A_LORA]
    g_lo = zs[:, off + RW_DECAY_LORA + RW_A_LORA:]

    wx = -(w0_ref[...] + _dot_x3(jnp.tanh(w_lo), w2_ref[...]))
    softplus = jnp.maximum(wx, 0.0) + jnp.log(1.0 + jnp.exp(-jnp.abs(wx)))
    logw = -jnp.exp(-softplus - 0.5)
    a = _sigmoid(a0_ref[...] + _dot_x3(a_lo, a2_ref[...]))
    g = _dot_x3(_sigmoid(g_lo), g2_ref[...])
    seg = seg_ref[...]
    kk = k * kk_ref[...]
    kk = kk * lax.rsqrt(jnp.maximum(_dot_x2_lhs(kk * kk, seg), 1e-24))
    k2 = k * (1.0 + (a - 1.0) * ka_ref[...])

    pw = 2 * RW_N
    row2 = lax.broadcasted_iota(jnp.int32, (c, pw), 0)
    col2 = lax.broadcasted_iota(jnp.int32, (c, pw), 1) % c
    incl2 = row2 >= col2
    strict2 = row2 > col2
    blk_mask = (row2 // RW_BLK) == (col2 // RW_BLK)
    eye = (row2 == col2).astype(F32)
    rowp = lax.broadcasted_iota(jnp.int32, (pw, pw), 0)
    colp = lax.broadcasted_iota(jnp.int32, (pw, pw), 1)
    pair_mask = (rowp // RW_N) == (colp // RW_N)
    rowb = lax.broadcasted_iota(jnp.int32, (tb, tb), 0)
    colb = lax.broadcasted_iota(jnp.int32, (tb, tb), 1)
    tri = jnp.where(colb >= (rowb // c) * c, jnp.where(rowb >= colb, 1.0, 0.0), 0.0).astype(BF16)
    cw = _dot_x2_rhs(tri, logw)
    w_inv = jnp.exp(-cw)
    last = jnp.concatenate([jnp.broadcast_to(cw[(ci + 1) * c - 1:(ci + 1) * c], (c, RW_W)) for ci in range(nck)],
                           axis=0)
    w_rest = jnp.exp(last - cw)
    beta = a * kk
    alpha_t = -kk * jnp.exp(cw - logw)
    r_t = r * jnp.exp(cw)
    beta_h = beta * w_inv
    k_h = k2 * w_inv
    beta_d = beta * w_rest
    k_d = k2 * w_rest

    ps = range(RW_HEADS // 2)
    cp = [(ci, p) for ci in range(nck) for p in ps]
    m = range(len(cp))
    rs = [slice(ci * c, (ci + 1) * c) for ci, _ in cp]
    sl = [slice(p * pw, (p + 1) * pw) for _, p in cp]
    v2 = [v[rs[i], sl[i]] for i in m]
    lhs = [jnp.concatenate([alpha_t[rs[i], sl[i]], r_t[rs[i], sl[i]]], axis=0).astype(BF16) for i in m]
    rhs = [jnp.concatenate([_pair_blockdiag(beta_h[rs[i], sl[i]], pair_mask),
                            _pair_blockdiag(k_h[rs[i], sl[i]], pair_mask)], axis=0) for i in m]
    big = [_dot_nt(lhs[i], rhs[i]) for i in m]
    a_ab = [jnp.where(strict2, big[i][:c, :pw], 0.0) for i in m]
    a_ak = [jnp.where(strict2, big[i][:c, pw:], 0.0) for i in m]
    a_rb = [jnp.where(incl2, big[i][c:, :pw], 0.0) for i in m]
    a_rk = [jnp.where(incl2, big[i][c:, pw:], 0.0) for i in m]
    t_inv = _inv_unit_lower(a_ab, eye, blk_mask, pair_mask)
    av = [_pair_dot(a_ak[i], v2[i], pair_mask) for i in m]
    u_const = [_pair_dot(t_inv[i], av[i], pair_mask) for i in m]
    lhs_s = [jnp.concatenate([_pair_dot(t_inv[i], alpha_t[rs[i], sl[i]], pair_mask).astype(BF16),
                              r_t[rs[i], sl[i]].astype(BF16)], axis=0) for i in m]
    a_r = [jnp.concatenate([a_rb[i], a_rk[i]], axis=1).astype(BF16) for i in m]
    bk_d = [jnp.concatenate([beta_d[rs[i], sl[i]], k_d[rs[i], sl[i]]], axis=0).astype(BF16) for i in m]
    s_cur = [s_ref[p] for p in ps]
    o_chunks = []
    for ci in range(nck):
        ix = [ci * len(ps) + p for p in ps]
        sd = [_dot_nt(lhs_s[ix[p]], s_cur[p].astype(BF16)) for p in ps]
        u = [sd[p][:c] + u_const[ix[p]] for p in ps]
        uv = [jnp.concatenate([_pair_blockdiag(u[p], pair_mask), _pair_blockdiag(v2[ix[p]], pair_mask)], axis=0)
              for p in ps]
        o_chunks.append(jnp.concatenate([sd[p][c:] + _dot(a_r[ix[p]], uv[p]) for p in ps], axis=1))
        w_last = jnp.exp(cw[(ci + 1) * c - 1:(ci + 1) * c])
        uvt = [jnp.concatenate([u[p], v2[ix[p]]], axis=0).astype(BF16) for p in ps]
        s_cur = [s_cur[p] * w_last[:, sl[p]] + jnp.where(pair_mask, _dot_tn(uvt[p], bk_d[ix[p]]), 0.0) for p in ps]
    for p in ps:
        s_ref[p] = s_cur[p]
    o = jnp.concatenate(o_chunks, axis=0)

    mean = _dot_x2_lhs(o, seg) * (1.0 / RW_N)
    dev = o - mean
    var = _dot_x2_lhs(dev * dev, seg) * (1.0 / RW_N)
    o = dev * lax.rsqrt(var + RW_GN_EPS) * lnw_ref[...] + lnb_ref[...]
    bonus = _dot_x2_lhs(r * k2 * rk_ref[...], seg) * v
    o_ref[0] = ((o + bonus) * g).astype(o_ref.dtype)


def _rwkv_call(z3, mu, w0, w2, a0, a2, g2, k_k, k_a, r_k, ln_w, ln_b):
    b, t, _ = z3.shape
    c = min(RW_TB, t)
    hid = lax.broadcasted_iota(jnp.int32, (RW_W, RW_W), 0) // RW_N
    seg = (hid == hid.T).astype(BF16)

    def vec(n):
        return pl.BlockSpec((1, n), lambda i, j: (0, 0))

    def mat(m, n):
        return pl.BlockSpec((m, n), lambda i, j: (0, 0))

    return pl.pallas_call(
        _rwkv_kernel,
        grid=(b, t // c),
        in_specs=[
            pl.BlockSpec((1, c, RW_COLS), lambda i, j: (i, j, RW_OFF // RW_COLS)),
            vec(RW_COLS), vec(RW_W), mat(RW_DECAY_LORA, RW_W), vec(RW_W), mat(RW_A_LORA, RW_W),
            mat(RW_GATE_LORA, RW_W), vec(RW_W), vec(RW_W), vec(RW_W), vec(RW_W), vec(RW_W),
            mat(RW_W, RW_W),
        ],
        out_specs=pl.BlockSpec((1, c, RW_W), lambda i, j: (i, j, 0)),
        out_shape=jax.ShapeDtypeStruct((b, t, RW_W), BF16),
        scratch_shapes=[pltpu.VMEM((RW_HEADS // 2, 2 * RW_N, 2 * RW_N), F32), pltpu.VMEM((1, RW_COLS), F32)],
        compiler_params=_cparams(("parallel", "arbitrary")),
        name="rwkv7_mixer",
    )(z3, mu.reshape(1, -1), w0.reshape(1, -1), w2, a0.reshape(1, -1), a2, g2, k_k.reshape(1, -1),
      k_a.reshape(1, -1), r_k.reshape(1, -1), ln_w.reshape(1, -1), ln_b.reshape(1, -1), seg)


def _merge_kernel(ohg_ref, oret_ref, orw_ref, zg_ref, x_ref, gate_ref, bhg_ref, bret_ref, brw_ref,
                  wout_ref, o_ref):
    d = x_ref.shape[1]
    y = _sigmoid(zg_ref[:, 0:d].astype(F32)) * _dot(ohg_ref[...], bhg_ref[...])
    y = y + _sigmoid(zg_ref[:, d:2 * d].astype(F32)) * _dot(oret_ref[...], bret_ref[...])
    y = y + _sigmoid(zg_ref[:, 2 * d:3 * d].astype(F32)) * _dot(orw_ref[...], brw_ref[...])
    o_ref[...] = x_ref[...] + gate_ref[0] * _dot(y.astype(BF16), wout_ref[...])


def _merge_call(o_hg, o_ret, o_rw, z2, x2, mod3, br_hg, br_ret, br_rw, w_out, seq, gate_blk, tm=1024):
    n, d = x2.shape
    tpb = seq // tm

    def rows(w):
        return pl.BlockSpec((tm, w), lambda i: (i, 0))

    def full(m, k):
        return pl.BlockSpec((m, k), lambda i: (0, 0))

    return pl.pallas_call(
        _merge_kernel,
        grid=(n // tm,),
        in_specs=[
            rows(HG_W), rows(RET_W), rows(RW_W), rows(3 * d), rows(d),
            pl.BlockSpec((1, 1, d), lambda i: (i // tpb, 0, gate_blk)),
            full(HG_W, d), full(RET_W, d), full(RW_W, d), full(d, d),
        ],
        out_specs=rows(d),
        out_shape=jax.ShapeDtypeStruct((n, d), F32),
        compiler_params=_cparams(("parallel",)),
        name="merge_outproj",
    )(o_hg, o_ret, o_rw, z2, x2, mod3, br_hg, br_ret, br_rw, w_out)


def _pack_bf16_pairs(x):
    w = x.shape[1] // 2
    hi = pltpu.bitcast(x[:, :w].astype(BF16).astype(F32), jnp.uint32)
    lo = pltpu.bitcast(x[:, w:].astype(BF16).astype(F32), jnp.uint32)
    return pltpu.bitcast(hi | lax.shift_right_logical(lo, jnp.uint32(16)), jnp.int32)


def _unpack_bf16_pairs(p):
    u = pltpu.bitcast(p, jnp.uint32)
    hi = pltpu.bitcast(u & jnp.uint32(0xFFFF0000), F32)
    lo = pltpu.bitcast(lax.shift_left(u, jnp.uint32(16)), F32)
    return jnp.concatenate([hi, lo], axis=1)


def _route_kernel(x_ref, g_ref, scale_ref, shift_ref, rc_ref, hp_ref, eid_ref, wts_ref, cnt_ref):
    @pl.when(pl.program_id(0) == 0)
    def _():
        cnt_ref[...] = jnp.zeros_like(cnt_ref)

    h = _rms_mod(x_ref[...], g_ref[...], scale_ref[0], shift_ref[0])
    hp_ref[...] = _pack_bf16_pairs(h)
    tm = h.shape[0]
    lane = lax.broadcasted_iota(jnp.int32, (tm, LANES), 1)
    neg = -jnp.inf
    logits = _dot_x3(h, rc_ref[...])
    gl = jnp.where(lane < N_GROUPS, logits, neg)
    gmax = jnp.max(gl, axis=-1, keepdims=True)
    gidx = jnp.min(jnp.where(gl == gmax, lane, LANES), axis=-1, keepdims=True)
    gw = 1.0 / jnp.sum(jnp.exp(gl - gmax), axis=-1, keepdims=True)
    lo = N_GROUPS + gidx * EXPERTS_PER_GROUP
    el = jnp.where(lane >= lo, jnp.where(lane < lo + EXPERTS_PER_GROUP, logits, neg), neg)
    m1 = jnp.max(el, axis=-1, keepdims=True)
    l1 = jnp.min(jnp.where(el == m1, lane, LANES), axis=-1, keepdims=True)
    el2 = jnp.where(lane == l1, neg, el)
    m2 = jnp.max(el2, axis=-1, keepdims=True)
    l2 = jnp.min(jnp.where(el2 == m2, lane, LANES), axis=-1, keepdims=True)
    i1 = l1 - N_GROUPS
    i2 = l2 - N_GROUPS
    e2 = jnp.exp(m2 - m1)
    p1 = 1.0 / (1.0 + e2)
    p2 = e2 * p1
    oh1 = jnp.where(lane == i1, 1.0, 0.0)
    oh2 = jnp.where(lane == i2, 1.0, 0.0)
    row = lax.broadcasted_iota(jnp.int32, (tm, tm), 0)
    col = lax.broadcasted_iota(jnp.int32, (tm, tm), 1)
    earlier = jnp.where(row > col, 1.0, 0.0).astype(BF16)
    before = _dot(earlier, jnp.concatenate([oh1, oh2], axis=1).astype(BF16))
    tot1 = jnp.sum(oh1, axis=0, keepdims=True)
    carry = cnt_ref[...]
    r1 = jnp.sum(oh1 * (before[:, :LANES] + carry), axis=-1, keepdims=True).astype(jnp.int32)
    r2 = jnp.sum(oh2 * (before[:, LANES:] + (carry + tot1)), axis=-1, keepdims=True).astype(jnp.int32)
    cnt_ref[...] = carry + tot1 + jnp.sum(oh2, axis=0, keepdims=True)
    eid_ref[...] = jnp.where(lane == 0, i1, jnp.where(lane == 1, i2, jnp.where(lane == 2, r1,
                                                                             jnp.where(lane == 3, r2, 0))))
    wts_ref[...] = jnp.where(lane == 0, gw * p1, jnp.where(lane == 1, gw * p2, 0.0))


def _route_call(x2, gain, mod3, router_g, router_e, seq, scale_blk, shift_blk, tm=1024):
    n, d = x2.shape
    tpb = seq // tm
    rc = jnp.pad(jnp.concatenate([router_g, router_e], axis=1), ((0, 0), (0, LANES - N_GROUPS - N_EXPERTS)))
    return pl.pallas_call(
        _route_kernel,
        grid=(n // tm,),
        in_specs=[
            pl.BlockSpec((tm, d), lambda i: (i, 0)),
            pl.BlockSpec((1, d), lambda i: (0, 0)),
            pl.BlockSpec((1, 1, d), lambda i: (i // tpb, 0, scale_blk)),
            pl.BlockSpec((1, 1, d), lambda i: (i // tpb, 0, shift_blk)),
            pl.BlockSpec((d, LANES), lambda i: (0, 0)),
        ],
        out_specs=[pl.BlockSpec((tm, d // 2), lambda i: (i, 0)), pl.BlockSpec((tm, LANES), lambda i: (i, 0)),
                   pl.BlockSpec((tm, LANES), lambda i: (i, 0)), pl.BlockSpec((1, LANES), lambda i: (0, 0))],
        out_shape=[jax.ShapeDtypeStruct((n, d // 2), jnp.int32), jax.ShapeDtypeStruct((n, LANES), jnp.int32),
                   jax.ShapeDtypeStruct((n, LANES), F32), jax.ShapeDtypeStruct((1, LANES), F32)],
        compiler_params=_cparams(("arbitrary",)),
        name="moe_route",
    )(x2, gain.reshape(1, d), mod3, mod3, rc)


SC_CORES = 2
SC_SUBCORES = 16
SC_WORKERS = SC_CORES * SC_SUBCORES
SC_ROWS = 32
SC_STREAMS = 4


def _sc_gather(table, idx):
    m = idx.shape[0]
    w = table.shape[1]
    per_worker = m // SC_WORKERS
    steps = per_worker // SC_ROWS
    assert per_worker * SC_WORKERS == m and steps * SC_ROWS == per_worker and steps % SC_STREAMS == 0
    mesh = plsc.VectorSubcoreMesh(core_axis_name="c", subcore_axis_name="s")
    ks = range(SC_STREAMS)

    def body(table_hbm, idx_hbm, out_hbm, idx_v, *rest):
        bufs, g_sems, w_sems = rest[:SC_STREAMS], rest[SC_STREAMS:2 * SC_STREAMS], rest[2 * SC_STREAMS:]
        wid = lax.axis_index("s") * SC_CORES + lax.axis_index("c")
        pltpu.sync_copy(idx_hbm.at[wid], idx_v)

        @pl.loop(0, steps, step=SC_STREAMS)
        def _(j):
            row0 = wid * per_worker + j * SC_ROWS
            gathers = [pltpu.async_copy(table_hbm.at[idx_v.at[j + q]], bufs[q], g_sems[q]) for q in ks]
            writes = []
            for q in ks:
                gathers[q].wait()
                writes.append(pltpu.async_copy(bufs[q], out_hbm.at[pl.ds(row0 + q * SC_ROWS, SC_ROWS)], w_sems[q]))
            for q in ks:
                writes[q].wait()

    return pl.kernel(
        body,
        out_type=jax.ShapeDtypeStruct((m, w), table.dtype),
        mesh=mesh,
        scratch_types=[pltpu.VMEM((steps, SC_ROWS), jnp.int32)] + [pltpu.VMEM((SC_ROWS, w), table.dtype)] * SC_STREAMS
        + [pltpu.SemaphoreType.DMA] * (2 * SC_STREAMS),
        name="sc_row_gather",
    )(table, idx.reshape(SC_WORKERS, steps, SC_ROWS))


def _sc_scatter2(rows, idx0, idx1, p):
    n, w = rows.shape
    per_worker = n // SC_WORKERS
    steps = per_worker // SC_ROWS
    assert per_worker * SC_WORKERS == n and steps * SC_ROWS == per_worker and steps % SC_STREAMS == 0
    mesh = plsc.VectorSubcoreMesh(core_axis_name="c", subcore_axis_name="s")
    ks = range(SC_STREAMS)

    def body(rows_hbm, i0_hbm, i1_hbm, out_hbm, i0_v, i1_v, *rest):
        bufs, r_sems = rest[:SC_STREAMS], rest[SC_STREAMS:2 * SC_STREAMS]
        s0_sems, s1_sems = rest[2 * SC_STREAMS:3 * SC_STREAMS], rest[3 * SC_STREAMS:]
        wid = lax.axis_index("s") * SC_CORES + lax.axis_index("c")
        pltpu.sync_copy(i0_hbm.at[wid], i0_v)
        pltpu.sync_copy(i1_hbm.at[wid], i1_v)

        @pl.loop(0, steps, step=SC_STREAMS)
        def _(j):
            row0 = wid * per_worker + j * SC_ROWS
            reads = [pltpu.async_copy(rows_hbm.at[pl.ds(row0 + q * SC_ROWS, SC_ROWS)], bufs[q], r_sems[q]) for q in ks]
            writes = []
            for q in ks:
                reads[q].wait()
                writes.append(pltpu.async_copy(bufs[q], out_hbm.at[i0_v.at[j + q]], s0_sems[q]))
                writes.append(pltpu.async_copy(bufs[q], out_hbm.at[i1_v.at[j + q]], s1_sems[q]))
            for wr in writes:
                wr.wait()

    index_block = pltpu.VMEM((steps, SC_ROWS), jnp.int32)
    return pl.kernel(
        body,
        out_type=jax.ShapeDtypeStruct((p, w), rows.dtype),
        mesh=mesh,
        scratch_types=[index_block, index_block] + [pltpu.VMEM((SC_ROWS, w), rows.dtype)] * SC_STREAMS
        + [pltpu.SemaphoreType.DMA] * (3 * SC_STREAMS),
        name="sc_row_scatter",
    )(rows, idx0.reshape(SC_WORKERS, steps, SC_ROWS), idx1.reshape(SC_WORKERS, steps, SC_ROWS))


MOE_TM = 512


def _gexperts_kernel(te_ref, tv_ref, nu_ref, xs_ref, w1_ref, w3_ref, w2_ref, ys_ref, w1b_ref, w3b_ref, w2b_ref):
    i = pl.program_id(0)

    @pl.when((i == 0) | (te_ref[i] != te_ref[jnp.maximum(i - 1, 0)]))
    def _():
        w1b_ref[...] = w1_ref[0].astype(BF16)
        w3b_ref[...] = w3_ref[0].astype(BF16)
        w2b_ref[...] = w2_ref[0].astype(BF16)

    @pl.when(i < nu_ref[0])
    def _():
        rid = lax.broadcasted_iota(jnp.int32, xs_ref.shape, 0)
        xb = _unpack_bf16_pairs(jnp.where(rid < tv_ref[i], xs_ref[...], 0)).astype(BF16)
        act = (_silu(_dot(xb, w1b_ref[...])) * _dot(xb, w3b_ref[...])).astype(BF16)
        ys_ref[...] = _pack_bf16_pairs(_dot(act, w2b_ref[...]))


def _gexperts_call(xs, tile_expert, tile_valid, n_used, w1, w3, w2):
    p, half = xs.shape
    ne, d, de = w1.shape
    nt = p // MOE_TM

    def rows(i, te, tv, nu):
        return (jnp.minimum(i, nu[0] - 1), 0)

    def wsel(i, te, tv, nu):
        return (te[i], 0, 0)

    return pl.pallas_call(
        _gexperts_kernel,
        grid_spec=pltpu.PrefetchScalarGridSpec(
            num_scalar_prefetch=3,
            grid=(nt,),
            in_specs=[
                pl.BlockSpec((MOE_TM, half), rows),
                pl.BlockSpec((1, d, de), wsel),
                pl.BlockSpec((1, d, de), wsel),
                pl.BlockSpec((1, de, d), wsel),
            ],
            out_specs=pl.BlockSpec((MOE_TM, half), rows),
            scratch_shapes=[pltpu.VMEM((d, de), BF16), pltpu.VMEM((d, de), BF16), pltpu.VMEM((de, d), BF16)],
        ),
        out_shape=jax.ShapeDtypeStruct((p, half), jnp.int32),
        compiler_params=_cparams(("arbitrary",)),
        name="moe_experts",
    )(tile_expert, tile_valid, n_used, xs, w1, w3, w2)


def _combine_kernel(y0_ref, y1_ref, wts_ref, x_ref, gate_ref, fg_ref, o_ref, *, final_norm):
    wts = wts_ref[...]
    moe = wts[:, 0:1] * _unpack_bf16_pairs(y0_ref[...]) + wts[:, 1:2] * _unpack_bf16_pairs(y1_ref[...])
    xn = x_ref[...] + gate_ref[0] * moe
    if final_norm:
        xn = xn * lax.rsqrt(jnp.mean(xn * xn, axis=-1, keepdims=True) + NORM_EPS) * fg_ref[...]
    o_ref[...] = xn


def _combine_call(yg, wts, x2, mod3, final_g, seq, gate_blk, final_norm, tm=1024):
    n, d = x2.shape
    tpb = seq // tm
    slot1 = n // tm
    return pl.pallas_call(
        functools.partial(_combine_kernel, final_norm=final_norm),
        grid=(n // tm,),
        in_specs=[
            pl.BlockSpec((tm, d // 2), lambda i: (i, 0)),
            pl.BlockSpec((tm, d // 2), lambda i: (i + slot1, 0)),
            pl.BlockSpec((tm, LANES), lambda i: (i, 0)),
            pl.BlockSpec((tm, d), lambda i: (i, 0)),
            pl.BlockSpec((1, 1, d), lambda i: (i // tpb, 0, gate_blk)),
            pl.BlockSpec((1, d), lambda i: (0, 0)),
        ],
        out_specs=pl.BlockSpec((tm, d), lambda i: (i, 0)),
        out_shape=jax.ShapeDtypeStruct((n, d), F32),
        compiler_params=_cparams(("parallel",)),
        name="moe_combine",
    )(yg, yg, wts, x2, mod3, final_g.reshape(1, d))


def _pos_kernel(eid_ref, ts_ref, p0_ref, p1_ref):
    eid = eid_ref[...]
    tm = eid.shape[0]
    lane = lax.broadcasted_iota(jnp.int32, (tm, LANES), 1)
    sub = lax.broadcasted_iota(jnp.int32, (tm, LANES), 0) % LANES
    for slot, out_ref in ((0, p0_ref), (1, p1_ref)):
        first_row = jnp.sum(jnp.where(lane == eid[:, slot:slot + 1], ts_ref[...], 0), axis=-1, keepdims=True)
        pos = first_row + eid[:, slot + 2:slot + 3]
        out_ref[...] = jnp.sum(jnp.where(lane == sub, pos, 0).reshape(tm // LANES, LANES, LANES), axis=1)


def _pos_call(eid, first_rows, tm=1024):
    n = eid.shape[0]
    out = jax.ShapeDtypeStruct((n // LANES, LANES), jnp.int32)
    p0, p1 = pl.pallas_call(
        _pos_kernel,
        grid=(n // tm,),
        in_specs=[pl.BlockSpec((tm, LANES), lambda i: (i, 0)), pl.BlockSpec((1, LANES), lambda i: (0, 0))],
        out_specs=[pl.BlockSpec((tm // LANES, LANES), lambda i: (i, 0))] * 2,
        out_shape=[out, out],
        compiler_params=_cparams(("parallel",)),
        name="moe_positions",
    )(eid, first_rows)
    return p0.reshape(n), p1.reshape(n)


def _moe_plan(eid, counts_f):
    n = eid.shape[0]
    nt = (2 * n) // MOE_TM + N_EXPERTS
    counts = counts_f[0, :N_EXPERTS].astype(jnp.int32)
    tiles = (counts + MOE_TM - 1) // MOE_TM
    tile_end = jnp.cumsum(tiles)
    tile_start = tile_end - tiles
    n_used = tile_end[-1:]
    tile_iota = jnp.arange(nt, dtype=jnp.int32)
    tile_expert = jnp.sum(jnp.minimum(tile_iota, n_used - 1)[:, None] >= tile_end[None, :], axis=1, dtype=jnp.int32)
    own = tile_expert[:, None] == jnp.arange(N_EXPERTS, dtype=jnp.int32)[None, :]
    count_t = jnp.sum(jnp.where(own, counts[None, :], 0), axis=1)
    start_t = jnp.sum(jnp.where(own, tile_start[None, :], 0), axis=1)
    tile_valid = jnp.clip(count_t - (tile_iota - start_t) * MOE_TM, 0, MOE_TM)
    first_rows = jnp.pad(tile_start * MOE_TM, (0, LANES - N_EXPERTS)).reshape(1, LANES)
    pos0, pos1 = _pos_call(eid, first_rows)
    return pos0, pos1, tile_expert, tile_valid, n_used


def kernel(x, c, positions, ada_w, ada_b, norm1_g, norm2_g, w_in, hg_lb_table, hg_norm_w, rw_mu, rw_w0, rw_w2,
           rw_a0, rw_a2, rw_g2, rw_k_k, rw_k_a, rw_r_k, rw_ln_w, rw_ln_b, br_hg, br_ret, br_rw, w_out,
           router_g, router_e, moe_w1, moe_w3, moe_w2, final_g):
    b, t, d = x.shape
    depth = ada_w.shape[0]
    n = b * t
    assert w_in.shape[2] == IN_COLS and d == 1024

    lb_p = jax.nn.softmax(hg_lb_table.astype(F32), axis=0)
    lower_bounds = jnp.cumsum(lb_p, axis=0) - lb_p[0]

    mod = _mod_call(c, ada_w, ada_b)
    cos2, sin2 = _rope_call(positions, RET_DK)
    w_perm = _wprep_call(w_in)
    x2 = x.reshape(n, d)
    for l in range(depth):
        mod3 = mod[l].reshape(b, 1, 6 * d)
        z2 = _inproj_call(x2, norm1_g[l], mod3, w_perm, l, t, scale_blk=1, shift_blk=0)
        z3 = z2.reshape(b, t, IN_COLS)
        o_hg = _hgrn2_call(z3, lower_bounds[l], hg_norm_w[l])
        o_ret = _ret_call(z3, cos2, sin2)
        o_rw = _rwkv_call(z3, rw_mu[l], rw_w0[l], rw_w2[l], rw_a0[l], rw_a2[l], rw_g2[l], rw_k_k[l],
                          rw_k_a[l], rw_r_k[l], rw_ln_w[l], rw_ln_b[l])
        x2 = _merge_call(o_hg.reshape(n, HG_W), o_ret.reshape(n, RET_W), o_rw.reshape(n, RW_W), z2, x2, mod3,
                         br_hg[l].astype(BF16), br_ret[l].astype(BF16), br_rw[l].astype(BF16),
                         w_out[l].astype(BF16), t, gate_blk=2)
        hp, eid, wts, counts = _route_call(x2, norm2_g[l], mod3, router_g[l], router_e[l], t, scale_blk=4,
                                           shift_blk=3)
        pos0, pos1, tile_expert, tile_valid, n_used = _moe_plan(eid, counts)
        xs = _sc_scatter2(hp, pos0, pos1, (2 * n // MOE_TM + N_EXPERTS) * MOE_TM)
        ys = _gexperts_call(xs, tile_expert + l * N_EXPERTS, tile_valid, n_used,
                            moe_w1.reshape((-1,) + moe_w1.shape[2:]), moe_w3.reshape((-1,) + moe_w3.shape[2:]),
                            moe_w2.reshape((-1,) + moe_w2.shape[2:]))
        yg = _sc_gather(ys, jnp.concatenate([pos0, pos1]))
        x2 = _combine_call(yg, wts, x2, mod3, final_g, t, gate_blk=5, final_norm=(l == depth - 1))
    return x2.reshape(b, t, d)
```

```python
import functools

import jax
import jax.numpy as jnp
from jax import lax
from jax.experimental import pallas as pl
from jax.experimental.pallas import tpu as pltpu
from jax.experimental.pallas import tpu_sc as plsc

F32 = jnp.float32
BF16 = jnp.bfloat16
HIGHEST = lax.Precision.HIGHEST

HG_HEADS = 4
HG_DK = 128
HG_W = HG_HEADS * HG_DK
RET_HEADS = 4
RET_DK = 128
RET_W = RET_HEADS * RET_DK
RW_HEADS = 8
RW_N = 64
RW_W = RW_HEADS * RW_N
RW_DECAY_LORA = 64
RW_A_LORA = 64
RW_GATE_LORA = 128
RW_COLS = 3 * RW_W + RW_DECAY_LORA + RW_A_LORA + RW_GATE_LORA
RW_GN_EPS = 64e-5
N_GROUPS = 4
EXPERTS_PER_GROUP = 8
N_EXPERTS = N_GROUPS * EXPERTS_PER_GROUP
ROPE_THETA = 10000.0
NORM_EPS = 1e-6

LANES = 128
LOG2E = 1.4426950408889634
VMEM_LIMIT = 56 * 1024 * 1024

GATE_OFF = 0
HG_OFF = 3 * 1024
RET_OFF = HG_OFF + 4 * HG_W
RW_OFF = RET_OFF + 4 * RET_W
IN_COLS = RW_OFF + RW_COLS

HG_CHUNK = 64
HG_SUB = 16
HG_SAFE_SPAN = 60.0
RW_CHUNK = 64
RW_BLK = 16
RW_TB = 256
Z_DTYPE = BF16


def _cparams(sem):
    return pltpu.CompilerParams(dimension_semantics=sem, vmem_limit_bytes=VMEM_LIMIT)


def _dot(a, b, precision=None):
    return jnp.dot(a, b, preferred_element_type=F32, precision=precision)


def _dot_nt(a, b, precision=None):
    return lax.dot_general(a, b, (((1,), (1,)), ((), ())), preferred_element_type=F32, precision=precision)


def _dot_tn(a, b, precision=None):
    return lax.dot_general(a, b, (((0,), (0,)), ((), ())), preferred_element_type=F32, precision=precision)


def _split_bf16(x):
    hi = x.astype(BF16)
    return hi, (x - hi.astype(F32)).astype(BF16)


def _dot_x3(a, b):
    ah, al = _split_bf16(a)
    bh, bl = _split_bf16(b)
    return _dot(ah, bh) + _dot(ah, bl) + _dot(al, bh)


def _dot_x2_lhs(a, b_exact):
    ah, al = _split_bf16(a)
    return _dot(ah, b_exact) + _dot(al, b_exact)


def _dot_x2_rhs(a_exact, b):
    bh, bl = _split_bf16(b)
    return _dot(a_exact, bh) + _dot(a_exact, bl)


def _sigmoid(x):
    return 0.5 * jnp.tanh(0.5 * x) + 0.5


def _silu(x):
    return x * _sigmoid(x)


def _rms_mod(x, gain, scale, shift):
    y = x * lax.rsqrt(jnp.mean(x * x, axis=-1, keepdims=True) + NORM_EPS)
    return (y * gain) * (1.0 + scale) + shift


def _mod_kernel(c_ref, w_ref, b_ref, o_ref):
    c = c_ref[...]
    o_ref[0] = _dot(_silu(c), w_ref[0], HIGHEST) + b_ref[0]


def _mod_call(c, ada_w, ada_b):
    depth, d, d6 = ada_w.shape
    b = c.shape[0]
    nblk = d6 // d
    return pl.pallas_call(
        _mod_kernel,
        grid=(depth, nblk),
        in_specs=[
            pl.BlockSpec((b, d), lambda l, j: (0, 0)),
            pl.BlockSpec((1, d, d), lambda l, j: (l, 0, j)),
            pl.BlockSpec((1, 1, d), lambda l, j: (l, 0, j)),
        ],
        out_specs=pl.BlockSpec((1, b, d), lambda l, j: (l, 0, j)),
        out_shape=jax.ShapeDtypeStruct((depth, b, d6), F32),
        compiler_params=_cparams(("parallel", "parallel")),
        name="adaln_mod",
    )(c, ada_w, ada_b.reshape(depth, 1, d6))


def _rope_kernel(pos_ref, freq_ref, sign_ref, cos_ref, sin_ref):
    ang = pos_ref[0].astype(F32) * freq_ref[...]
    cos_ref[0] = jnp.cos(ang)
    sin_ref[0] = jnp.sin(ang) * sign_ref[...]


def _rope_call(positions, d):
    b, t = positions.shape
    tb = min(t, 512)
    inv_freq = ROPE_THETA ** (-jnp.arange(0, d, 2, dtype=F32) / d)
    freq2 = jnp.concatenate([inv_freq, inv_freq]).reshape(1, d)
    sign2 = jnp.concatenate([-jnp.ones((d // 2,), F32), jnp.ones((d // 2,), F32)]).reshape(1, d)
    out = jax.ShapeDtypeStruct((b, t, d), F32)
    return pl.pallas_call(
        _rope_kernel,
        grid=(b, t // tb),
        in_specs=[
            pl.BlockSpec((1, tb, 1), lambda i, j: (i, j, 0)),
            pl.BlockSpec((1, d), lambda i, j: (0, 0)),
            pl.BlockSpec((1, d), lambda i, j: (0, 0)),
        ],
        out_specs=[pl.BlockSpec((1, tb, d), lambda i, j: (i, j, 0))] * 2,
        out_shape=[out, out],
        compiler_params=_cparams(("parallel", "parallel")),
        name="rope_tables",
    )(positions.reshape(b, t, 1), freq2, sign2)


W_BLK = 256


def _wprep_kernel(w_ref, o_ref):
    o_ref[...] = w_ref[...].astype(o_ref.dtype)


def _wprep_call(w_in, layer):
    _, d, cols = w_in.shape
    nblk = cols // W_BLK
    first = (cols - 3 * d) // W_BLK
    return pl.pallas_call(
        _wprep_kernel,
        grid=(nblk,),
        in_specs=[pl.BlockSpec((1, d, W_BLK), lambda j: (layer, 0, (j + first) % nblk))],
        out_specs=pl.BlockSpec((1, d, W_BLK), lambda j: (0, 0, j)),
        out_shape=jax.ShapeDtypeStruct((1, d, cols), BF16),
        compiler_params=_cparams(("parallel",)),
        name="w_in_layout",
    )(w_in)


def _inproj_kernel(x_ref, g_ref, scale_ref, shift_ref, w_ref, o_ref, h_ref):
    @pl.when(pl.program_id(1) == 0)
    def _():
        h = _rms_mod(x_ref[...], g_ref[...], scale_ref[0], shift_ref[0])
        h_ref[...] = h.astype(BF16)

    o_ref[...] = _dot(h_ref[...], w_ref[0]).astype(o_ref.dtype)


def _inproj_call(x2, gain, mod3, w_bf16, seq, scale_blk, shift_blk, tm=2048, tn=1792):
    n, d = x2.shape
    cols = w_bf16.shape[2]
    tpb = seq // tm
    return pl.pallas_call(
        _inproj_kernel,
        grid=(n // tm, cols // tn),
        in_specs=[
            pl.BlockSpec((tm, d), lambda i, j: (i, 0)),
            pl.BlockSpec((1, d), lambda i, j: (0, 0)),
            pl.BlockSpec((1, 1, d), lambda i, j: (i // tpb, 0, scale_blk)),
            pl.BlockSpec((1, 1, d), lambda i, j: (i // tpb, 0, shift_blk)),
            pl.BlockSpec((1, d, tn), lambda i, j: (0, 0, j)),
        ],
        out_specs=pl.BlockSpec((tm, tn), lambda i, j: (i, j)),
        out_shape=jax.ShapeDtypeStruct((n, cols), Z_DTYPE),
        scratch_shapes=[pltpu.VMEM((tm, d), BF16)],
        compiler_params=_cparams(("parallel", "arbitrary")),
        name="norm_inproj",
    )(x2, gain.reshape(1, d), mod3, mod3, w_bf16)


def _hgrn2_block(zs, lbs, nw, sts, factored):
    hs = range(len(zs))
    tb = zs[0][0].shape[0]
    c, sub = HG_CHUNK, HG_SUB
    nc, ns, nb = tb // c, c // sub, tb // sub
    f = [lbs[h] + (1.0 - lbs[h]) * _sigmoid(zs[h][1]) for h in hs]
    logf = [jnp.log(jnp.maximum(f[h], 1e-30)) for h in hs]
    q = [_silu(zs[h][0]) * (HG_DK ** -0.5) for h in hs]
    k = [1.0 - f[h] for h in hs]
    v = [zs[h][2] for h in hs]
    v_b = [v[h].astype(BF16) for h in hs]
    row = lax.broadcasted_iota(jnp.int32, (tb, tb), 0)
    col = lax.broadcasted_iota(jnp.int32, (tb, tb), 1)
    tri = jnp.where(col >= (row // c) * c, jnp.where(row >= col, 1.0, 0.0), 0.0).astype(BF16)
    cum = [_dot_x2_rhs(tri, logf[h]) for h in hs]
    cum3 = [cum[h].reshape(nb, sub, HG_DK) for h in hs]
    ref3 = [cum3[h][:, 0:1, :] - logf[h].reshape(nb, sub, HG_DK)[:, 0:1, :] for h in hs]
    span = functools.reduce(jnp.maximum, [jnp.max(ref3[h] - cum3[h][:, sub - 1:sub, :]) for h in hs])
    qe = [(q[h] * jnp.exp(cum[h])).astype(BF16) for h in hs]

    offd = [(h, ci * c, ci * c + sub * i) for h in hs for ci in range(nc) for i in range(1, ns)]
    base = [cum[h][lo - 1:lo] for h, _, lo in offd]
    qt = [(q[h][lo:lo + sub] * jnp.exp(cum[h][lo:lo + sub] - base[j])).astype(BF16)
          for j, (h, _, lo) in enumerate(offd)]
    kt = [(k[h][r0:lo] * jnp.exp(base[j] - cum[h][r0:lo])).astype(BF16) for j, (h, r0, lo) in enumerate(offd)]
    a = [_dot_nt(qt[j], kt[j]).astype(BF16) for j in range(len(offd))]
    av = {(h, lo): _dot(a[j], v_b[h][r0:lo]) for j, (h, r0, lo) in enumerate(offd)}

    cs = [slice(ci * c, (ci + 1) * c) for ci in range(nc)]
    hc = [(h, ci) for h in hs for ci in range(nc)]
    last = {(h, ci): cum[h][(ci + 1) * c - 1:(ci + 1) * c] for h, ci in hc}
    kd = {(h, ci): (k[h][cs[ci]] * jnp.exp(last[h, ci] - cum[h][cs[ci]])).astype(BF16) for h, ci in hc}
    inc = {(h, ci): _dot_tn(v_b[h][cs[ci]], kd[h, ci]) for h, ci in hc}
    s_in = {(h, 0): sts[h] for h in hs}
    for ci in range(nc):
        for h in hs:
            s_in[h, ci + 1] = s_in[h, ci] * jnp.exp(last[h, ci]) + inc[h, ci]
    o_inter = {(h, ci): _dot_nt(qe[h][cs[ci]], s_in[h, ci].astype(BF16)) for h, ci in hc}

    if factored:
        qf = [(q[h] * jnp.exp(cum3[h] - ref3[h]).reshape(tb, HG_DK)).astype(BF16) for h in hs]
        kf = [(k[h] * jnp.exp(ref3[h] - cum3[h]).reshape(tb, HG_DK)).astype(BF16) for h in hs]
        rc = lax.broadcasted_iota(jnp.int32, (c, c), 0)
        cc = lax.broadcasted_iota(jnp.int32, (c, c), 1)
        keep = (rc >= cc) & (rc // sub == cc // sub)
        a_d = {(h, ci): jnp.where(keep, _dot_nt(qf[h][cs[ci]], kf[h][cs[ci]]), 0.0).astype(BF16) for h, ci in hc}
        dg = {(h, ci): _dot(a_d[h, ci], v_b[h][cs[ci]]) for h, ci in hc}
        diag = [jnp.concatenate([dg[h, ci] for ci in range(nc)], axis=0) for h in hs]
    else:
        gb = 4
        trow = lax.broadcasted_iota(jnp.int32, (gb, sub, HG_DK), 1)
        diag = []
        for h in hs:
            c2 = cum[h] * LOG2E
            ks2 = c2 - jnp.log2(k[h])
            parts = []
            for g0 in range(0, nb, gb):
                rws = slice(g0 * sub, (g0 + gb) * sub)
                c23, ks23, q3, v3 = (x[rws].reshape(gb, sub, HG_DK) for x in (c2, ks2, q[h], v[h]))
                acc = jnp.zeros((gb, sub, HG_DK), F32)
                for s in range(sub):
                    e = jnp.exp2(jnp.where(trow >= s, c23 - ks23[:, s:s + 1, :], -jnp.inf))
                    a_col = jnp.sum(q3 * e, axis=-1, keepdims=True)
                    acc = acc + a_col * v3[:, s:s + 1, :]
                parts.append(acc.reshape(gb * sub, HG_DK))
            diag.append(jnp.concatenate(parts, axis=0))

    outs = []
    for h in hs:
        pieces = []
        for ci in range(nc):
            for i in range(ns):
                lo = ci * c + sub * i
                piece = o_inter[h, ci][sub * i:sub * (i + 1)] + diag[h][lo:lo + sub]
                pieces.append(piece + av[h, lo] if i > 0 else piece)
        o = jnp.concatenate(pieces, axis=0)
        o = o * lax.rsqrt(jnp.mean(o * o, axis=-1, keepdims=True) + NORM_EPS)
        outs.append(o * nw * _silu(zs[h][3]))
    return outs, [s_in[h, nc] for h in hs], span


def _hgrn2_kernel(zq_ref, zf_ref, zi_ref, zg_ref, lb_ref, nw_ref, o_ref, st_ref):
    @pl.when(pl.program_id(1) == 0)
    def _():
        st_ref[...] = jnp.zeros_like(st_ref)

    hs = range(HG_HEADS)
    sl = [slice(h * HG_DK, (h + 1) * HG_DK) for h in hs]

    def run(factored):
        zs = [tuple(r[0, :, sl[h]].astype(F32) for r in (zq_ref, zf_ref, zi_ref, zg_ref)) for h in hs]
        outs, sts, span = _hgrn2_block(zs, [lb_ref[:, sl[h]] for h in hs], nw_ref[...],
                                       [st_ref[h] for h in hs], factored)
        return jnp.concatenate(outs, axis=1), sts, span

    st_old = [st_ref[h] for h in hs]
    o, st_new, span = run(True)
    for h in hs:
        st_ref[h] = st_new[h]
    o_ref[0] = o.astype(o_ref.dtype)

    @pl.when(span > HG_SAFE_SPAN)
    def _():
        for h in hs:
            st_ref[h] = st_old[h]
        o2, st2, _ = run(False)
        for h in hs:
            st_ref[h] = st2[h]
        o_ref[0] = o2.astype(o_ref.dtype)


def _hgrn2_call(z3, lower_bound, norm_w, tb=256):
    b, t, _ = z3.shape
    tb = min(tb, t)
    base = HG_OFF // HG_W

    def zspec(part):
        return pl.BlockSpec((1, tb, HG_W), lambda i, j: (i, j, base + part))

    return pl.pallas_call(
        _hgrn2_kernel,
        grid=(b, t // tb),
        in_specs=[
            zspec(0), zspec(1), zspec(2), zspec(3),
            pl.BlockSpec((1, HG_W), lambda i, j: (0, 0)),
            pl.BlockSpec((1, LANES), lambda i, j: (0, 0)),
        ],
        out_specs=pl.BlockSpec((1, tb, HG_W), lambda i, j: (i, j, 0)),
        out_shape=jax.ShapeDtypeStruct((b, t, HG_W), BF16),
        scratch_shapes=[pltpu.VMEM((HG_HEADS, HG_DK, HG_DK), F32)],
        compiler_params=_cparams(("parallel", "arbitrary")),
        name="hgrn2_mixer",
    )(z3, z3, z3, z3, lower_bound.reshape(1, HG_W), norm_w.reshape(1, HG_DK))


def _ret_kernel(zq_ref, zk_ref, zv_ref, zg_ref, cos_ref, sin_ref, o_ref, st_ref, dmask_ref, *, chunk):
    hs = range(RET_HEADS)
    sl = [slice(h * RET_DK, (h + 1) * RET_DK) for h in hs]
    lg = [jnp.log(jnp.full((1, 1), 1.0 - 2.0 ** (-5.0 - h), F32)) for h in hs]

    @pl.when(pl.program_id(1) == 0)
    def _():
        st_ref[...] = jnp.zeros_like(st_ref)
        row = lax.broadcasted_iota(jnp.int32, (chunk, chunk), 0)
        col = lax.broadcasted_iota(jnp.int32, (chunk, chunk), 1)
        rel = (row - col).astype(F32)
        for h in hs:
            dmask_ref[h] = jnp.where(rel >= 0.0, jnp.exp(jnp.maximum(rel, 0.0) * lg[h]), 0.0)

    cos2 = cos_ref[0]
    sin2 = sin_ref[0]
    half = RET_DK // 2

    def rope(z):
        return z * cos2 + pltpu.roll(z, half, 1) * sin2

    tcol = lax.broadcasted_iota(jnp.int32, (chunk, 1), 0).astype(F32)
    q = [rope(zq_ref[0, :, sl[h]].astype(F32)) * (RET_DK ** -0.5) for h in hs]
    k = [rope(zk_ref[0, :, sl[h]].astype(F32)) for h in hs]
    v_b = [zv_ref[0, :, sl[h]].astype(BF16) for h in hs]
    st = [st_ref[h] for h in hs]
    scores = [(_dot_nt(q[h].astype(BF16), k[h].astype(BF16)) * dmask_ref[h]).astype(BF16) for h in hs]
    qx = [(q[h] * jnp.exp((tcol + 1.0) * lg[h])).astype(BF16) for h in hs]
    kz = [(k[h] * jnp.exp((chunk - 1.0 - tcol) * lg[h])).astype(BF16) for h in hs]
    o = [_dot(scores[h], v_b[h]) + _dot_nt(qx[h], st[h].astype(BF16)) for h in hs]
    for h in hs:
        st_ref[h] = st[h] * jnp.exp(chunk * lg[h]) + _dot_tn(v_b[h], kz[h])
    o = [o[h] * lax.rsqrt(jnp.mean(o[h] * o[h], axis=-1, keepdims=True) + NORM_EPS) for h in hs]
    o_ref[0] = (jnp.concatenate(o, axis=1) * _silu(zg_ref[0].astype(F32))).astype(o_ref.dtype)


def _ret_call(z3, cos2, sin2, chunk=256):
    b, t, _ = z3.shape
    chunk = min(chunk, t)
    base = RET_OFF // RET_W

    def zspec(part):
        return pl.BlockSpec((1, chunk, RET_W), lambda i, j: (i, j, base + part))

    tab = pl.BlockSpec((1, chunk, RET_DK), lambda i, j: (i, j, 0))
    return pl.pallas_call(
        functools.partial(_ret_kernel, chunk=chunk),
        grid=(b, t // chunk),
        in_specs=[zspec(0), zspec(1), zspec(2), zspec(3), tab, tab],
        out_specs=pl.BlockSpec((1, chunk, RET_W), lambda i, j: (i, j, 0)),
        out_shape=jax.ShapeDtypeStruct((b, t, RET_W), BF16),
        scratch_shapes=[pltpu.VMEM((RET_HEADS, RET_DK, RET_DK), F32), pltpu.VMEM((RET_HEADS, chunk, chunk), F32)],
        compiler_params=_cparams(("parallel", "arbitrary")),
        name="retention_mixer",
    )(z3, z3, z3, z3, cos2, sin2)


def _pair_blockdiag(y, pair_mask):
    return jnp.where(pair_mask, jnp.concatenate([y, y], axis=0), 0.0).astype(BF16)


def _pair_dot(x, y, pair_mask):
    return _dot(x.astype(BF16), _pair_blockdiag(y, pair_mask))


def _inv_unit_lower(a, eye, blk_mask, pair_mask):
    c = a[0].shape[0]
    m = range(len(a))
    a_bd = [jnp.where(blk_mask, a[i], 0.0) for i in m]
    a_off = [a[i] - a_bd[i] for i in m]
    a2 = [_pair_dot(a_bd[i], a_bd[i], pair_mask) for i in m]
    p = [eye + a_bd[i] for i in m]
    r = [_pair_dot(jnp.concatenate([p[i], a2[i]], axis=0), a2[i], pair_mask) for i in m]
    p = [p[i] + r[i][:c] for i in m]
    a4 = [r[i][c:] for i in m]
    r = [_pair_dot(jnp.concatenate([p[i], a4[i]], axis=0), a4[i], pair_mask) for i in m]
    p = [p[i] + r[i][:c] for i in m]
    a8 = [r[i][c:] for i in m]
    t_bd = [p[i] + _pair_dot(p[i], a8[i], pair_mask) for i in m]
    n = [_pair_dot(t_bd[i], a_off[i], pair_mask) for i in m]
    n2 = [_pair_dot(n[i], n[i], pair_mask) for i in m]
    z = [t_bd[i] + _pair_dot(n[i], t_bd[i], pair_mask) for i in m]
    return [z[i] + _pair_dot(n2[i], z[i], pair_mask) for i in m]


def _rwkv_kernel(z_ref, mu_ref, w0_ref, w2_ref, a0_ref, a2_ref, g2_ref, kk_ref, ka_ref, rk_ref,
                 lnw_ref, lnb_ref, seg_ref, o_ref, s_ref, prev_ref):
    c = RW_CHUNK
    tb = z_ref.shape[1]
    nck = tb // c

    @pl.when(pl.program_id(1) == 0)
    def _():
        s_ref[...] = jnp.zeros_like(s_ref)
        prev_ref[...] = jnp.zeros_like(prev_ref)

    z = z_ref[0].astype(F32)
    rows = lax.broadcasted_iota(jnp.int32, (tb, 1), 0)
    z_prev = jnp.where(rows == 0, prev_ref[...], pltpu.roll(z, 1, 0))
    prev_ref[...] = z[tb - 1:tb]
    zs = z + mu_ref[...] * (z_prev - z)
    r = zs[:, 0:RW_W]
    k = zs[:, RW_W:2 * RW_W]
    v = zs[:, 2 * RW_W:3 * RW_W]
    off = 3 * RW_W
    w_lo = zs[:, off:off + RW_DECAY_LORA]
    a_lo = zs[:, off + RW_DECAY_LORA:off + RW_DECAY_LORA + RW_A_LORA]
    g_lo = zs[:, off + RW_DECAY_LORA + RW_A_LORA:]

    wx = -(w0_ref[...] + _dot_x3(jnp.tanh(w_lo), w2_ref[...]))
    softplus = jnp.maximum(wx, 0.0) + jnp.log(1.0 + jnp.exp(-jnp.abs(wx)))
    logw = -jnp.exp(-softplus - 0.5)
    a = _sigmoid(a0_ref[...] + _dot_x3(a_lo, a2_ref[...]))
    g = _dot_x3(_sigmoid(g_lo), g2_ref[...])
    seg = seg_ref[...]
    kk = k * kk_ref[...]
    kk = kk * lax.rsqrt(jnp.maximum(_dot_x2_lhs(kk * kk, seg), 1e-24))
    k2 = k * (1.0 + (a - 1.0) * ka_ref[...])

    pw = 2 * RW_N
    row2 = lax.broadcasted_iota(jnp.int32, (c, pw), 0)
    col2 = lax.broadcasted_iota(jnp.int32, (c, pw), 1) % c
    incl2 = row2 >= col2
    strict2 = row2 > col2
    blk_mask = (row2 // RW_BLK) == (col2 // RW_BLK)
    eye = (row2 == col2).astype(F32)
    rowp = lax.broadcasted_iota(jnp.int32, (pw, pw), 0)
    colp = lax.broadcasted_iota(jnp.int32, (pw, pw), 1)
    pair_mask = (rowp // RW_N) == (colp // RW_N)
    rowb = lax.broadcasted_iota(jnp.int32, (tb, tb), 0)
    colb = lax.broadcasted_iota(jnp.int32, (tb, tb), 1)
    tri = jnp.where(colb >= (rowb // c) * c, jnp.where(rowb >= colb, 1.0, 0.0), 0.0).astype(BF16)
    cw = _dot_x2_rhs(tri, logw)
    w_inv = jnp.exp(-cw)
    last = jnp.concatenate([jnp.broadcast_to(cw[(ci + 1) * c - 1:(ci + 1) * c], (c, RW_W)) for ci in range(nck)],
                           axis=0)
    w_rest = jnp.exp(last - cw)
    beta = a * kk
    alpha_t = -kk * jnp.exp(cw - logw)
    r_t = r * jnp.exp(cw)
    beta_h = beta * w_inv
    k_h = k2 * w_inv
    beta_d = beta * w_rest
    k_d = k2 * w_rest

    ps = range(RW_HEADS // 2)
    cp = [(ci, p) for ci in range(nck) for p in ps]
    m = range(len(cp))
    rs = [slice(ci * c, (ci + 1) * c) for ci, _ in cp]
    sl = [slice(p * pw, (p + 1) * pw) for _, p in cp]
    v2 = [v[rs[i], sl[i]] for i in m]
    lhs = [jnp.concatenate([alpha_t[rs[i], sl[i]], r_t[rs[i], sl[i]]], axis=0).astype(BF16) for i in m]
    rhs = [jnp.concatenate([_pair_blockdiag(beta_h[rs[i], sl[i]], pair_mask),
                            _pair_blockdiag(k_h[rs[i], sl[i]], pair_mask)], axis=0) for i in m]
    big = [_dot_nt(lhs[i], rhs[i]) for i in m]
    a_ab = [jnp.where(strict2, big[i][:c, :pw], 0.0) for i in m]
    a_ak = [jnp.where(strict2, big[i][:c, pw:], 0.0) for i in m]
    a_rb = [jnp.where(incl2, big[i][c:, :pw], 0.0) for i in m]
    a_rk = [jnp.where(incl2, big[i][c:, pw:], 0.0) for i in m]
    t_inv = _inv_unit_lower(a_ab, eye, blk_mask, pair_mask)
    av = [_pair_dot(a_ak[i], v2[i], pair_mask) for i in m]
    u_const = [_pair_dot(t_inv[i], av[i], pair_mask) for i in m]
    lhs_s = [jnp.concatenate([_pair_dot(t_inv[i], alpha_t[rs[i], sl[i]], pair_mask).astype(BF16),
                              r_t[rs[i], sl[i]].astype(BF16)], axis=0) for i in m]
    a_r = [jnp.concatenate([a_rb[i], a_rk[i]], axis=1).astype(BF16) for i in m]
    bk_d = [jnp.concatenate([beta_d[rs[i], sl[i]], k_d[rs[i], sl[i]]], axis=0).astype(BF16) for i in m]
    s_cur = [s_ref[p] for p in ps]
    o_chunks = []
    for ci in range(nck):
        ix = [ci * len(ps) + p for p in ps]
        sd = [_dot_nt(lhs_s[ix[p]], s_cur[p].astype(BF16)) for p in ps]
        u = [sd[p][:c] + u_const[ix[p]] for p in ps]
        uv = [jnp.concatenate([_pair_blockdiag(u[p], pair_mask), _pair_blockdiag(v2[ix[p]], pair_mask)], axis=0)
              for p in ps]
        o_chunks.append(jnp.concatenate([sd[p][c:] + _dot(a_r[ix[p]], uv[p]) for p in ps], axis=1))
        w_last = jnp.exp(cw[(ci + 1) * c - 1:(ci + 1) * c])
        uvt = [jnp.concatenate([u[p], v2[ix[p]]], axis=0).astype(BF16) for p in ps]
        s_cur = [s_cur[p] * w_last[:, sl[p]] + jnp.where(pair_mask, _dot_tn(uvt[p], bk_d[ix[p]]), 0.0) for p in ps]
    for p in ps:
        s_ref[p] = s_cur[p]
    o = jnp.concatenate(o_chunks, axis=0)

    mean = _dot_x2_lhs(o, seg) * (1.0 / RW_N)
    dev = o - mean
    var = _dot_x2_lhs(dev * dev, seg) * (1.0 / RW_N)
    o = dev * lax.rsqrt(var + RW_GN_EPS) * lnw_ref[...] + lnb_ref[...]
    bonus = _dot_x2_lhs(r * k2 * rk_ref[...], seg) * v
    o_ref[0] = ((o + bonus) * g).astype(o_ref.dtype)


def _rwkv_call(z3, mu, w0, w2, a0, a2, g2, k_k, k_a, r_k, ln_w, ln_b):
    b, t, _ = z3.shape
    c = min(RW_TB, t)
    hid = lax.broadcasted_iota(jnp.int32, (RW_W, RW_W), 0) // RW_N
    seg = (hid == hid.T).astype(BF16)

    def vec(n):
        return pl.BlockSpec((1, n), lambda i, j: (0, 0))

    def mat(m, n):
        return pl.BlockSpec((m, n), lambda i, j: (0, 0))

    return pl.pallas_call(
        _rwkv_kernel,
        grid=(b, t // c),
        in_specs=[
            pl.BlockSpec((1, c, RW_COLS), lambda i, j: (i, j, RW_OFF // RW_COLS)),
            vec(RW_COLS), vec(RW_W), mat(RW_DECAY_LORA, RW_W), vec(RW_W), mat(RW_A_LORA, RW_W),
            mat(RW_GATE_LORA, RW_W), vec(RW_W), vec(RW_W), vec(RW_W), vec(RW_W), vec(RW_W),
            mat(RW_W, RW_W),
        ],
        out_specs=pl.BlockSpec((1, c, RW_W), lambda i, j: (i, j, 0)),
        out_shape=jax.ShapeDtypeStruct((b, t, RW_W), BF16),
        scratch_shapes=[pltpu.VMEM((RW_HEADS // 2, 2 * RW_N, 2 * RW_N), F32), pltpu.VMEM((1, RW_COLS), F32)],
        compiler_params=_cparams(("parallel", "arbitrary")),
        name="rwkv7_mixer",
    )(z3, mu.reshape(1, -1), w0.reshape(1, -1), w2, a0.reshape(1, -1), a2, g2, k_k.reshape(1, -1),
      k_a.reshape(1, -1), r_k.reshape(1, -1), ln_w.reshape(1, -1), ln_b.reshape(1, -1), seg)


def _merge_kernel(ohg_ref, oret_ref, orw_ref, zg_ref, x_ref, gate_ref, bhg_ref, bret_ref, brw_ref,
                  wout_ref, o_ref):
    d = x_ref.shape[1]
    y = _sigmoid(zg_ref[:, 0:d].astype(F32)) * _dot(ohg_ref[...], bhg_ref[...])
    y = y + _sigmoid(zg_ref[:, d:2 * d].astype(F32)) * _dot(oret_ref[...], bret_ref[...])
    y = y + _sigmoid(zg_ref[:, 2 * d:3 * d].astype(F32)) * _dot(orw_ref[...], brw_ref[...])
    o_ref[...] = x_ref[...] + gate_ref[0] * _dot(y.astype(BF16), wout_ref[...])


def _merge_call(o_hg, o_ret, o_rw, z2, x2, mod3, br_hg, br_ret, br_rw, w_out, seq, gate_blk, tm=1024):
    n, d = x2.shape
    tpb = seq // tm

    def rows(w):
        return pl.BlockSpec((tm, w), lambda i: (i, 0))

    def full(m, k):
        return pl.BlockSpec((m, k), lambda i: (0, 0))

    return pl.pallas_call(
        _merge_kernel,
        grid=(n // tm,),
        in_specs=[
            rows(HG_W), rows(RET_W), rows(RW_W), rows(3 * d), rows(d),
            pl.BlockSpec((1, 1, d), lambda i: (i // tpb, 0, gate_blk)),
            full(HG_W, d), full(RET_W, d), full(RW_W, d), full(d, d),
        ],
        out_specs=rows(d),
        out_shape=jax.ShapeDtypeStruct((n, d), F32),
        compiler_params=_cparams(("parallel",)),
        name="merge_outproj",
    )(o_hg, o_ret, o_rw, z2, x2, mod3, br_hg, br_ret, br_rw, w_out)


def _pack_bf16_pairs(x):
    w = x.shape[1] // 2
    hi = pltpu.bitcast(x[:, :w].astype(BF16).astype(F32), jnp.uint32)
    lo = pltpu.bitcast(x[:, w:].astype(BF16).astype(F32), jnp.uint32)
    return pltpu.bitcast(hi | lax.shift_right_logical(lo, jnp.uint32(16)), jnp.int32)


def _unpack_bf16_pairs(p):
    u = pltpu.bitcast(p, jnp.uint32)
    hi = pltpu.bitcast(u & jnp.uint32(0xFFFF0000), F32)
    lo = pltpu.bitcast(lax.shift_left(u, jnp.uint32(16)), F32)
    return jnp.concatenate([hi, lo], axis=1)


def _route_kernel(x_ref, g_ref, scale_ref, shift_ref, rc_ref, hp_ref, eid_ref, wts_ref, cnt_ref):
    @pl.when(pl.program_id(0) == 0)
    def _():
        cnt_ref[...] = jnp.zeros_like(cnt_ref)

    h = _rms_mod(x_ref[...], g_ref[...], scale_ref[0], shift_ref[0])
    hp_ref[...] = _pack_bf16_pairs(h)
    tm = h.shape[0]
    lane = lax.broadcasted_iota(jnp.int32, (tm, LANES), 1)
    neg = -jnp.inf
    logits = _dot_x3(h, rc_ref[...])
    gl = jnp.where(lane < N_GROUPS, logits, neg)
    gmax = jnp.max(gl, axis=-1, keepdims=True)
    gidx = jnp.min(jnp.where(gl == gmax, lane, LANES), axis=-1, keepdims=True)
    gw = 1.0 / jnp.sum(jnp.exp(gl - gmax), axis=-1, keepdims=True)
    lo = N_GROUPS + gidx * EXPERTS_PER_GROUP
    el = jnp.where(lane >= lo, jnp.where(lane < lo + EXPERTS_PER_GROUP, logits, neg), neg)
    m1 = jnp.max(el, axis=-1, keepdims=True)
    l1 = jnp.min(jnp.where(el == m1, lane, LANES), axis=-1, keepdims=True)
    el2 = jnp.where(lane == l1, neg, el)
    m2 = jnp.max(el2, axis=-1, keepdims=True)
    l2 = jnp.min(jnp.where(el2 == m2, lane, LANES), axis=-1, keepdims=True)
    i1 = l1 - N_GROUPS
    i2 = l2 - N_GROUPS
    e2 = jnp.exp(m2 - m1)
    p1 = 1.0 / (1.0 + e2)
    p2 = e2 * p1
    oh1 = jnp.where(lane == i1, 1.0, 0.0)
    oh2 = jnp.where(lane == i2, 1.0, 0.0)
    row = lax.broadcasted_iota(jnp.int32, (tm, tm), 0)
    col = lax.broadcasted_iota(jnp.int32, (tm, tm), 1)
    earlier = jnp.where(row > col, 1.0, 0.0).astype(BF16)
    before = _dot(earlier, jnp.concatenate([oh1, oh2], axis=1).astype(BF16))
    tot1 = jnp.sum(oh1, axis=0, keepdims=True)
    carry = cnt_ref[...]
    r1 = jnp.sum(oh1 * (before[:, :LANES] + carry), axis=-1, keepdims=True).astype(jnp.int32)
    r2 = jnp.sum(oh2 * (before[:, LANES:] + (carry + tot1)), axis=-1, keepdims=True).astype(jnp.int32)
    cnt_ref[...] = carry + tot1 + jnp.sum(oh2, axis=0, keepdims=True)
    eid_ref[...] = jnp.where(lane == 0, i1, jnp.where(lane == 1, i2, jnp.where(lane == 2, r1,
                                                                             jnp.where(lane == 3, r2, 0))))
    wts_ref[...] = jnp.where(lane == 0, gw * p1, jnp.where(lane == 1, gw * p2, 0.0))


def _route_call(x2, gain, mod3, router_g, router_e, seq, scale_blk, shift_blk, tm=1024):
    n, d = x2.shape
    tpb = seq // tm
    rc = jnp.pad(jnp.concatenate([router_g, router_e], axis=1), ((0, 0), (0, LANES - N_GROUPS - N_EXPERTS)))
    return pl.pallas_call(
        _route_kernel,
        grid=(n // tm,),
        in_specs=[
            pl.BlockSpec((tm, d), lambda i: (i, 0)),
            pl.BlockSpec((1, d), lambda i: (0, 0)),
            pl.BlockSpec((1, 1, d), lambda i: (i // tpb, 0, scale_blk)),
            pl.BlockSpec((1, 1, d), lambda i: (i // tpb, 0, shift_blk)),
            pl.BlockSpec((d, LANES), lambda i: (0, 0)),
        ],
        out_specs=[pl.BlockSpec((tm, d // 2), lambda i: (i, 0)), pl.BlockSpec((tm, LANES), lambda i: (i, 0)),
                   pl.BlockSpec((tm, LANES), lambda i: (i, 0)), pl.BlockSpec((1, LANES), lambda i: (0, 0))],
        out_shape=[jax.ShapeDtypeStruct((n, d // 2), jnp.int32), jax.ShapeDtypeStruct((n, LANES), jnp.int32),
                   jax.ShapeDtypeStruct((n, LANES), F32), jax.ShapeDtypeStruct((1, LANES), F32)],
        compiler_params=_cparams(("arbitrary",)),
        name="moe_route",
    )(x2, gain.reshape(1, d), mod3, mod3, rc)


SC_CORES = 2
SC_SUBCORES = 16
SC_WORKERS = SC_CORES * SC_SUBCORES
SC_ROWS = 32
SC_STREAMS = 4


def _sc_gather(table, idx):
    m = idx.shape[0]
    w = table.shape[1]
    per_worker = m // SC_WORKERS
    steps = per_worker // SC_ROWS
    assert per_worker * SC_WORKERS == m and steps * SC_ROWS == per_worker and steps % SC_STREAMS == 0
    mesh = plsc.VectorSubcoreMesh(core_axis_name="c", subcore_axis_name="s")
    ks = range(SC_STREAMS)

    def body(table_hbm, idx_hbm, out_hbm, idx_v, *rest):
        bufs, g_sems, w_sems = rest[:SC_STREAMS], rest[SC_STREAMS:2 * SC_STREAMS], rest[2 * SC_STREAMS:]
        wid = lax.axis_index("s") * SC_CORES + lax.axis_index("c")
        pltpu.sync_copy(idx_hbm.at[wid], idx_v)

        @pl.loop(0, steps, step=SC_STREAMS)
        def _(j):
            row0 = wid * per_worker + j * SC_ROWS
            gathers = [pltpu.async_copy(table_hbm.at[idx_v.at[j + q]], bufs[q], g_sems[q]) for q in ks]
            writes = []
            for q in ks:
                gathers[q].wait()
                writes.append(pltpu.async_copy(bufs[q], out_hbm.at[pl.ds(row0 + q * SC_ROWS, SC_ROWS)], w_sems[q]))
            for q in ks:
                writes[q].wait()

    return pl.kernel(
        body,
        out_type=jax.ShapeDtypeStruct((m, w), table.dtype),
        mesh=mesh,
        scratch_types=[pltpu.VMEM((steps, SC_ROWS), jnp.int32)] + [pltpu.VMEM((SC_ROWS, w), table.dtype)] * SC_STREAMS
        + [pltpu.SemaphoreType.DMA] * (2 * SC_STREAMS),
        name="sc_row_gather",
    )(table, idx.reshape(SC_WORKERS, steps, SC_ROWS))


def _sc_scatter2(rows, idx0, idx1, p):
    n, w = rows.shape
    per_worker = n // SC_WORKERS
    steps = per_worker // SC_ROWS
    assert per_worker * SC_WORKERS == n and steps * SC_ROWS == per_worker and steps % SC_STREAMS == 0
    mesh = plsc.VectorSubcoreMesh(core_axis_name="c", subcore_axis_name="s")
    ks = range(SC_STREAMS)

    def body(rows_hbm, i0_hbm, i1_hbm, out_hbm, i0_v, i1_v, *rest):
        bufs, r_sems = rest[:SC_STREAMS], rest[SC_STREAMS:2 * SC_STREAMS]
        s0_sems, s1_sems = rest[2 * SC_STREAMS:3 * SC_STREAMS], rest[3 * SC_STREAMS:]
        wid = lax.axis_index("s") * SC_CORES + lax.axis_index("c")
        pltpu.sync_copy(i0_hbm.at[wid], i0_v)
        pltpu.sync_copy(i1_hbm.at[wid], i1_v)

        @pl.loop(0, steps, step=SC_STREAMS)
        def _(j):
            row0 = wid * per_worker + j * SC_ROWS
            reads = [pltpu.async_copy(rows_hbm.at[pl.ds(row0 + q * SC_ROWS, SC_ROWS)], bufs[q], r_sems[q]) for q in ks]
            writes = []
            for q in ks:
                reads[q].wait()
                writes.append(pltpu.async_copy(bufs[q], out_hbm.at[i0_v.at[j + q]], s0_sems[q]))
                writes.append(pltpu.async_copy(bufs[q], out_hbm.at[i1_v.at[j + q]], s1_sems[q]))
            for wr in writes:
                wr.wait()

    index_block = pltpu.VMEM((steps, SC_ROWS), jnp.int32)
    return pl.kernel(
        body,
        out_type=jax.ShapeDtypeStruct((p, w), rows.dtype),
        mesh=mesh,
        scratch_types=[index_block, index_block] + [pltpu.VMEM((SC_ROWS, w), rows.dtype)] * SC_STREAMS
        + [pltpu.SemaphoreType.DMA] * (3 * SC_STREAMS),
        name="sc_row_scatter",
    )(rows, idx0.reshape(SC_WORKERS, steps, SC_ROWS), idx1.reshape(SC_WORKERS, steps, SC_ROWS))


MOE_TM = 512


def _gexperts_kernel(te_ref, tv_ref, nu_ref, xs_ref, w1_ref, w3_ref, w2_ref, ys_ref, w1b_ref, w3b_ref, w2b_ref):
    i = pl.program_id(0)

    @pl.when((i == 0) | (te_ref[i] != te_ref[jnp.maximum(i - 1, 0)]))
    def _():
        w1b_ref[...] = w1_ref[0].astype(BF16)
        w3b_ref[...] = w3_ref[0].astype(BF16)
        w2b_ref[...] = w2_ref[0].astype(BF16)

    @pl.when(i < nu_ref[0])
    def _():
        rid = lax.broadcasted_iota(jnp.int32, xs_ref.shape, 0)
        xb = _unpack_bf16_pairs(jnp.where(rid < tv_ref[i], xs_ref[...], 0)).astype(BF16)
        act = (_silu(_dot(xb, w1b_ref[...])) * _dot(xb, w3b_ref[...])).astype(BF16)
        ys_ref[...] = _pack_bf16_pairs(_dot(act, w2b_ref[...]))


def _gexperts_call(xs, tile_expert, tile_valid, n_used, w1, w3, w2):
    p, half = xs.shape
    ne, d, de = w1.shape
    nt = p // MOE_TM

    def rows(i, te, tv, nu):
        return (jnp.minimum(i, nu[0] - 1), 0)

    def wsel(i, te, tv, nu):
        return (te[i], 0, 0)

    return pl.pallas_call(
        _gexperts_kernel,
        grid_spec=pltpu.PrefetchScalarGridSpec(
            num_scalar_prefetch=3,
            grid=(nt,),
            in_specs=[
                pl.BlockSpec((MOE_TM, half), rows),
                pl.BlockSpec((1, d, de), wsel),
                pl.BlockSpec((1, d, de), wsel),
                pl.BlockSpec((1, de, d), wsel),
            ],
            out_specs=pl.BlockSpec((MOE_TM, half), rows),
            scratch_shapes=[pltpu.VMEM((d, de), BF16), pltpu.VMEM((d, de), BF16), pltpu.VMEM((de, d), BF16)],
        ),
        out_shape=jax.ShapeDtypeStruct((p, half), jnp.int32),
        compiler_params=_cparams(("arbitrary",)),
        name="moe_experts",
    )(tile_expert, tile_valid, n_used, xs, w1, w3, w2)


def _combine_kernel(y0_ref, y1_ref, wts_ref, x_ref, gate_ref, fg_ref, o_ref, *, final_norm):
    wts = wts_ref[...]
    moe = wts[:, 0:1] * _unpack_bf16_pairs(y0_ref[...]) + wts[:, 1:2] * _unpack_bf16_pairs(y1_ref[...])
    xn = x_ref[...] + gate_ref[0] * moe
    if final_norm:
        xn = xn * lax.rsqrt(jnp.mean(xn * xn, axis=-1, keepdims=True) + NORM_EPS) * fg_ref[...]
    o_ref[...] = xn


def _combine_call(yg, wts, x2, mod3, final_g, seq, gate_blk, final_norm, tm=1024):
    n, d = x2.shape
    tpb = seq // tm
    slot1 = n // tm
    return pl.pallas_call(
        functools.partial(_combine_kernel, final_norm=final_norm),
        grid=(n // tm,),
        in_specs=[
            pl.BlockSpec((tm, d // 2), lambda i: (i, 0)),
            pl.BlockSpec((tm, d // 2), lambda i: (i + slot1, 0)),
            pl.BlockSpec((tm, LANES), lambda i: (i, 0)),
            pl.BlockSpec((tm, d), lambda i: (i, 0)),
            pl.BlockSpec((1, 1, d), lambda i: (i // tpb, 0, gate_blk)),
            pl.BlockSpec((1, d), lambda i: (0, 0)),
        ],
        out_specs=pl.BlockSpec((tm, d), lambda i: (i, 0)),
        out_shape=jax.ShapeDtypeStruct((n, d), F32),
        compiler_params=_cparams(("parallel",)),
        name="moe_combine",
    )(yg, yg, wts, x2, mod3, final_g.reshape(1, d))


def _pos_kernel(eid_ref, ts_ref, p0_ref, p1_ref):
    eid = eid_ref[...]
    tm = eid.shape[0]
    lane = lax.broadcasted_iota(jnp.int32, (tm, LANES), 1)
    sub = lax.broadcasted_iota(jnp.int32, (tm, LANES), 0) % LANES
    for slot, out_ref in ((0, p0_ref), (1, p1_ref)):
        first_row = jnp.sum(jnp.where(lane == eid[:, slot:slot + 1], ts_ref[...], 0), axis=-1, keepdims=True)
        pos = first_row + eid[:, slot + 2:slot + 3]
        out_ref[...] = jnp.sum(jnp.where(lane == sub, pos, 0).reshape(tm // LANES, LANES, LANES), axis=1)


def _pos_call(eid, first_rows, tm=4096):
    n = eid.shape[0]
    out = jax.ShapeDtypeStruct((n // LANES, LANES), jnp.int32)
    p0, p1 = pl.pallas_call(
        _pos_kernel,
        grid=(n // tm,),
        in_specs=[pl.BlockSpec((tm, LANES), lambda i: (i, 0)), pl.BlockSpec((1, LANES), lambda i: (0, 0))],
        out_specs=[pl.BlockSpec((tm // LANES, LANES), lambda i: (i, 0))] * 2,
        out_shape=[out, out],
        compiler_params=_cparams(("parallel",)),
        name="moe_positions",
    )(eid, first_rows)
    return p0.reshape(n), p1.reshape(n)


def _moe_plan(eid, counts_f):
    n = eid.shape[0]
    nt = (2 * n) // MOE_TM + N_EXPERTS
    counts = counts_f[0, :N_EXPERTS].astype(jnp.int32)
    tiles = (counts + MOE_TM - 1) // MOE_TM
    tile_end = jnp.cumsum(tiles)
    tile_start = tile_end - tiles
    n_used = tile_end[-1:]
    tile_iota = jnp.arange(nt, dtype=jnp.int32)
    tile_expert = jnp.sum(jnp.minimum(tile_iota, n_used - 1)[:, None] >= tile_end[None, :], axis=1, dtype=jnp.int32)
    own = tile_expert[:, None] == jnp.arange(N_EXPERTS, dtype=jnp.int32)[None, :]
    count_t = jnp.sum(jnp.where(own, counts[None, :], 0), axis=1)
    start_t = jnp.sum(jnp.where(own, tile_start[None, :], 0), axis=1)
    tile_valid = jnp.clip(count_t - (tile_iota - start_t) * MOE_TM, 0, MOE_TM)
    first_rows = jnp.pad(tile_start * MOE_TM, (0, LANES - N_EXPERTS)).reshape(1, LANES)
    pos0, pos1 = _pos_call(eid, first_rows)
    return pos0, pos1, tile_expert, tile_valid, n_used


def kernel(x, c, positions, ada_w, ada_b, norm1_g, norm2_g, w_in, hg_lb_table, hg_norm_w, rw_mu, rw_w0, rw_w2,
           rw_a0, rw_a2, rw_g2, rw_k_k, rw_k_a, rw_r_k, rw_ln_w, rw_ln_b, br_hg, br_ret, br_rw, w_out,
           router_g, router_e, moe_w1, moe_w3, moe_w2, final_g):
    b, t, d = x.shape
    depth = ada_w.shape[0]
    n = b * t
    assert w_in.shape[2] == IN_COLS and d == 1024

    lb_p = jax.nn.softmax(hg_lb_table.astype(F32), axis=0)
    lower_bounds = jnp.cumsum(lb_p, axis=0) - lb_p[0]

    mod = _mod_call(c, ada_w, ada_b)
    cos2, sin2 = _rope_call(positions, RET_DK)
    x2 = x.reshape(n, d)
    for l in range(depth):
        mod3 = mod[l].reshape(b, 1, 6 * d)
        z2 = _inproj_call(x2, norm1_g[l], mod3, _wprep_call(w_in, l), t, scale_blk=1, shift_blk=0)
        z3 = z2.reshape(b, t, IN_COLS)
        o_hg = _hgrn2_call(z3, lower_bounds[l], hg_norm_w[l])
        o_ret = _ret_call(z3, cos2, sin2)
        o_rw = _rwkv_call(z3, rw_mu[l], rw_w0[l], rw_w2[l], rw_a0[l], rw_a2[l], rw_g2[l], rw_k_k[l],
                          rw_k_a[l], rw_r_k[l], rw_ln_w[l], rw_ln_b[l])
        x2 = _merge_call(o_hg.reshape(n, HG_W), o_ret.reshape(n, RET_W), o_rw.reshape(n, RW_W), z2, x2, mod3,
                         br_hg[l].astype(BF16), br_ret[l].astype(BF16), br_rw[l].astype(BF16),
                         w_out[l].astype(BF16), t, gate_blk=2)
        hp, eid, wts, counts = _route_call(x2, norm2_g[l], mod3, router_g[l], router_e[l], t, scale_blk=4,
                                           shift_blk=3)
        pos0, pos1, tile_expert, tile_valid, n_used = _moe_plan(eid, counts)
        xs = _sc_scatter2(hp, pos0, pos1, (2 * n // MOE_TM + N_EXPERTS) * MOE_TM)
        ys = _gexperts_call(xs, tile_expert + l * N_EXPERTS, tile_valid, n_used,
                            moe_w1.reshape((-1,) + moe_w1.shape[2:]), moe_w3.reshape((-1,) + moe_w3.shape[2:]),
                            moe_w2.reshape((-1,) + moe_w2.shape[2:]))
        yg = _sc_gather(ys, jnp.concatenate([pos0, pos1]))
        x2 = _combine_call(yg, wts, x2, mod3, final_g, t, gate_blk=5, final_norm=(l == depth - 1))
    return x2.reshape(b, t, d)
```

```python
import functools

import jax
import jax.numpy as jnp
from jax import lax
from jax.experimental import pallas as pl
from jax.experimental.pallas import tpu as pltpu
from jax.experimental.pallas import tpu_sc as plsc

F32 = jnp.float32
BF16 = jnp.bfloat16
HIGHEST = lax.Precision.HIGHEST

HG_HEADS = 4
HG_DK = 128
HG_W = HG_HEADS * HG_DK
RET_HEADS = 4
RET_DK = 128
RET_W = RET_HEADS * RET_DK
RW_HEADS = 8
RW_N = 64
RW_W = RW_HEADS * RW_N
RW_DECAY_LORA = 64
RW_A_LORA = 64
RW_GATE_LORA = 128
RW_COLS = 3 * RW_W + RW_DECAY_LORA + RW_A_LORA + RW_GATE_LORA
RW_GN_EPS = 64e-5
N_GROUPS = 4
EXPERTS_PER_GROUP = 8
N_EXPERTS = N_GROUPS * EXPERTS_PER_GROUP
ROPE_THETA = 10000.0
NORM_EPS = 1e-6

LANES = 128
LOG2E = 1.4426950408889634
VMEM_LIMIT = 56 * 1024 * 1024

GATE_OFF = 0
HG_OFF = 3 * 1024
RET_OFF = HG_OFF + 4 * HG_W
RW_OFF = RET_OFF + 4 * RET_W
IN_COLS = RW_OFF + RW_COLS

HG_CHUNK = 64
HG_SUB = 16
HG_SAFE_SPAN = 60.0
RW_CHUNK = 64
RW_BLK = 16
RW_NB = 2
RW_TB = 256
Z_DTYPE = BF16


def _cparams(sem):
    return pltpu.CompilerParams(dimension_semantics=sem, vmem_limit_bytes=VMEM_LIMIT)


def _dot(a, b, precision=None):
    return jnp.dot(a, b, preferred_element_type=F32, precision=precision)


def _dot_nt(a, b, precision=None):
    return lax.dot_general(a, b, (((1,), (1,)), ((), ())), preferred_element_type=F32, precision=precision)


def _dot_tn(a, b, precision=None):
    return lax.dot_general(a, b, (((0,), (0,)), ((), ())), preferred_element_type=F32, precision=precision)


def _split_bf16(x):
    hi = x.astype(BF16)
    return hi, (x - hi.astype(F32)).astype(BF16)


def _dot_x3(a, b):
    ah, al = _split_bf16(a)
    bh, bl = _split_bf16(b)
    return _dot(ah, bh) + _dot(ah, bl) + _dot(al, bh)


def _dot_x2_lhs(a, b_exact):
    ah, al = _split_bf16(a)
    return _dot(ah, b_exact) + _dot(al, b_exact)


def _dot_x2_rhs(a_exact, b):
    bh, bl = _split_bf16(b)
    return _dot(a_exact, bh) + _dot(a_exact, bl)


def _sigmoid(x):
    return 0.5 * jnp.tanh(0.5 * x) + 0.5


def _silu(x):
    return x * _sigmoid(x)


def _rms_mod(x, gain, scale, shift):
    y = x * lax.rsqrt(jnp.mean(x * x, axis=-1, keepdims=True) + NORM_EPS)
    return (y * gain) * (1.0 + scale) + shift


def _mod_kernel(c_ref, w_ref, b_ref, o_ref):
    c = c_ref[...]
    o_ref[0] = _dot(_silu(c), w_ref[0], HIGHEST) + b_ref[0]


def _mod_call(c, ada_w, ada_b):
    depth, d, d6 = ada_w.shape
    b = c.shape[0]
    nblk = d6 // d
    return pl.pallas_call(
        _mod_kernel,
        grid=(depth, nblk),
        in_specs=[
            pl.BlockSpec((b, d), lambda l, j: (0, 0)),
            pl.BlockSpec((1, d, d), lambda l, j: (l, 0, j)),
            pl.BlockSpec((1, 1, d), lambda l, j: (l, 0, j)),
        ],
        out_specs=pl.BlockSpec((1, b, d), lambda l, j: (l, 0, j)),
        out_shape=jax.ShapeDtypeStruct((depth, b, d6), F32),
        compiler_params=_cparams(("parallel", "parallel")),
        name="adaln_mod",
    )(c, ada_w, ada_b.reshape(depth, 1, d6))


def _rope_kernel(pos_ref, freq_ref, sign_ref, cos_ref, sin_ref):
    ang = pos_ref[0].astype(F32) * freq_ref[...]
    cos_ref[0] = jnp.cos(ang)
    sin_ref[0] = jnp.sin(ang) * sign_ref[...]


def _rope_call(positions, d):
    b, t = positions.shape
    tb = min(t, 512)
    inv_freq = ROPE_THETA ** (-jnp.arange(0, d, 2, dtype=F32) / d)
    freq2 = jnp.concatenate([inv_freq, inv_freq]).reshape(1, d)
    sign2 = jnp.concatenate([-jnp.ones((d // 2,), F32), jnp.ones((d // 2,), F32)]).reshape(1, d)
    out = jax.ShapeDtypeStruct((b, t, d), F32)
    return pl.pallas_call(
        _rope_kernel,
        grid=(b, t // tb),
        in_specs=[
            pl.BlockSpec((1, tb, 1), lambda i, j: (i, j, 0)),
            pl.BlockSpec((1, d), lambda i, j: (0, 0)),
            pl.BlockSpec((1, d), lambda i, j: (0, 0)),
        ],
        out_specs=[pl.BlockSpec((1, tb, d), lambda i, j: (i, j, 0))] * 2,
        out_shape=[out, out],
        compiler_params=_cparams(("parallel", "parallel")),
        name="rope_tables",
    )(positions.reshape(b, t, 1), freq2, sign2)


W_BLK = 256


def _wprep_kernel(w_ref, o_ref):
    o_ref[...] = w_ref[...].astype(o_ref.dtype)


def _wprep_call(w_in, layer):
    _, d, cols = w_in.shape
    nblk = cols // W_BLK
    first = (cols - 3 * d) // W_BLK
    return pl.pallas_call(
        _wprep_kernel,
        grid=(nblk,),
        in_specs=[pl.BlockSpec((1, d, W_BLK), lambda j: (layer, 0, (j + first) % nblk))],
        out_specs=pl.BlockSpec((1, d, W_BLK), lambda j: (0, 0, j)),
        out_shape=jax.ShapeDtypeStruct((1, d, cols), BF16),
        compiler_params=_cparams(("parallel",)),
        name="w_in_layout",
    )(w_in)


def _inproj_kernel(x_ref, g_ref, scale_ref, shift_ref, w_ref, o_ref, h_ref):
    @pl.when(pl.program_id(1) == 0)
    def _():
        h = _rms_mod(x_ref[...], g_ref[...], scale_ref[0], shift_ref[0])
        h_ref[...] = h.astype(BF16)

    o_ref[...] = _dot(h_ref[...], w_ref[0]).astype(o_ref.dtype)


def _inproj_call(x2, gain, mod3, w_bf16, seq, scale_blk, shift_blk, tm=2048, tn=1792):
    n, d = x2.shape
    cols = w_bf16.shape[2]
    tpb = seq // tm
    return pl.pallas_call(
        _inproj_kernel,
        grid=(n // tm, cols // tn),
        in_specs=[
            pl.BlockSpec((tm, d), lambda i, j: (i, 0)),
            pl.BlockSpec((1, d), lambda i, j: (0, 0)),
            pl.BlockSpec((1, 1, d), lambda i, j: (i // tpb, 0, scale_blk)),
            pl.BlockSpec((1, 1, d), lambda i, j: (i // tpb, 0, shift_blk)),
            pl.BlockSpec((1, d, tn), lambda i, j: (0, 0, j)),
        ],
        out_specs=pl.BlockSpec((tm, tn), lambda i, j: (i, j)),
        out_shape=jax.ShapeDtypeStruct((n, cols), Z_DTYPE),
        scratch_shapes=[pltpu.VMEM((tm, d), BF16)],
        compiler_params=_cparams(("parallel", "arbitrary")),
        name="norm_inproj",
    )(x2, gain.reshape(1, d), mod3, mod3, w_bf16)


def _hgrn2_block(zs, lbs, nw, sts, factored):
    hs = range(len(zs))
    tb = zs[0][0].shape[0]
    c, sub = HG_CHUNK, HG_SUB
    nc, ns, nb = tb // c, c // sub, tb // sub
    f = [lbs[h] + (1.0 - lbs[h]) * _sigmoid(zs[h][1]) for h in hs]
    logf = [jnp.log(jnp.maximum(f[h], 1e-30)) for h in hs]
    q = [_silu(zs[h][0]) * (HG_DK ** -0.5) for h in hs]
    k = [1.0 - f[h] for h in hs]
    v = [zs[h][2] for h in hs]
    v_b = [v[h].astype(BF16) for h in hs]
    row = lax.broadcasted_iota(jnp.int32, (tb, tb), 0)
    col = lax.broadcasted_iota(jnp.int32, (tb, tb), 1)
    tri = jnp.where(col >= (row // c) * c, jnp.where(row >= col, 1.0, 0.0), 0.0).astype(BF16)
    cum = [_dot_x2_rhs(tri, logf[h]) for h in hs]
    cum3 = [cum[h].reshape(nb, sub, HG_DK) for h in hs]
    ref3 = [cum3[h][:, 0:1, :] - logf[h].reshape(nb, sub, HG_DK)[:, 0:1, :] for h in hs]
    span = functools.reduce(jnp.maximum, [jnp.max(ref3[h] - cum3[h][:, sub - 1:sub, :]) for h in hs])
    qe = [(q[h] * jnp.exp(cum[h])).astype(BF16) for h in hs]

    offd = [(h, ci * c, ci * c + sub * i) for h in hs for ci in range(nc) for i in range(1, ns)]
    base = [cum[h][lo - 1:lo] for h, _, lo in offd]
    qt = [(q[h][lo:lo + sub] * jnp.exp(cum[h][lo:lo + sub] - base[j])).astype(BF16)
          for j, (h, _, lo) in enumerate(offd)]
    kt = [(k[h][r0:lo] * jnp.exp(base[j] - cum[h][r0:lo])).astype(BF16) for j, (h, r0, lo) in enumerate(offd)]
    a = [_dot_nt(qt[j], kt[j]).astype(BF16) for j in range(len(offd))]
    av = {(h, lo): _dot(a[j], v_b[h][r0:lo]) for j, (h, r0, lo) in enumerate(offd)}

    cs = [slice(ci * c, (ci + 1) * c) for ci in range(nc)]
    hc = [(h, ci) for h in hs for ci in range(nc)]
    last = {(h, ci): cum[h][(ci + 1) * c - 1:(ci + 1) * c] for h, ci in hc}
    kd = {(h, ci): (k[h][cs[ci]] * jnp.exp(last[h, ci] - cum[h][cs[ci]])).astype(BF16) for h, ci in hc}
    inc = {(h, ci): _dot_tn(v_b[h][cs[ci]], kd[h, ci]) for h, ci in hc}
    s_in = {(h, 0): sts[h] for h in hs}
    for ci in range(nc):
        for h in hs:
            s_in[h, ci + 1] = s_in[h, ci] * jnp.exp(last[h, ci]) + inc[h, ci]
    o_inter = {(h, ci): _dot_nt(qe[h][cs[ci]], s_in[h, ci].astype(BF16)) for h, ci in hc}

    if factored:
        qf = [(q[h] * jnp.exp(cum3[h] - ref3[h]).reshape(tb, HG_DK)).astype(BF16) for h in hs]
        kf = [(k[h] * jnp.exp(ref3[h] - cum3[h]).reshape(tb, HG_DK)).astype(BF16) for h in hs]
        rc = lax.broadcasted_iota(jnp.int32, (c, c), 0)
        cc = lax.broadcasted_iota(jnp.int32, (c, c), 1)
        keep = (rc >= cc) & (rc // sub == cc // sub)
        a_d = {(h, ci): jnp.where(keep, _dot_nt(qf[h][cs[ci]], kf[h][cs[ci]]), 0.0).astype(BF16) for h, ci in hc}
        dg = {(h, ci): _dot(a_d[h, ci], v_b[h][cs[ci]]) for h, ci in hc}
        diag = [jnp.concatenate([dg[h, ci] for ci in range(nc)], axis=0) for h in hs]
    else:
        gb = 4
        trow = lax.broadcasted_iota(jnp.int32, (gb, sub, HG_DK), 1)
        diag = []
        for h in hs:
            c2 = cum[h] * LOG2E
            ks2 = c2 - jnp.log2(k[h])
            parts = []
            for g0 in range(0, nb, gb):
                rws = slice(g0 * sub, (g0 + gb) * sub)
                c23, ks23, q3, v3 = (x[rws].reshape(gb, sub, HG_DK) for x in (c2, ks2, q[h], v[h]))
                acc = jnp.zeros((gb, sub, HG_DK), F32)
                for s in range(sub):
                    e = jnp.exp2(jnp.where(trow >= s, c23 - ks23[:, s:s + 1, :], -jnp.inf))
                    a_col = jnp.sum(q3 * e, axis=-1, keepdims=True)
                    acc = acc + a_col * v3[:, s:s + 1, :]
                parts.append(acc.reshape(gb * sub, HG_DK))
            diag.append(jnp.concatenate(parts, axis=0))

    outs = []
    for h in hs:
        pieces = []
        for ci in range(nc):
            for i in range(ns):
                lo = ci * c + sub * i
                piece = o_inter[h, ci][sub * i:sub * (i + 1)] + diag[h][lo:lo + sub]
                pieces.append(piece + av[h, lo] if i > 0 else piece)
        o = jnp.concatenate(pieces, axis=0)
        o = o * lax.rsqrt(jnp.mean(o * o, axis=-1, keepdims=True) + NORM_EPS)
        outs.append(o * nw * _silu(zs[h][3]))
    return outs, [s_in[h, nc] for h in hs], span


def _hgrn2_kernel(zq_ref, zf_ref, zi_ref, zg_ref, lb_ref, nw_ref, o_ref, st_ref):
    @pl.when(pl.program_id(1) == 0)
    def _():
        st_ref[...] = jnp.zeros_like(st_ref)

    hs = range(HG_HEADS)
    sl = [slice(h * HG_DK, (h + 1) * HG_DK) for h in hs]

    def run(factored):
        zs = [tuple(r[0, :, sl[h]].astype(F32) for r in (zq_ref, zf_ref, zi_ref, zg_ref)) for h in hs]
        outs, sts, span = _hgrn2_block(zs, [lb_ref[:, sl[h]] for h in hs], nw_ref[...],
                                       [st_ref[h] for h in hs], factored)
        return jnp.concatenate(outs, axis=1), sts, span

    st_old = [st_ref[h] for h in hs]
    o, st_new, span = run(True)
    for h in hs:
        st_ref[h] = st_new[h]
    o_ref[0] = o.astype(o_ref.dtype)

    @pl.when(span > HG_SAFE_SPAN)
    def _():
        for h in hs:
            st_ref[h] = st_old[h]
        o2, st2, _ = run(False)
        for h in hs:
            st_ref[h] = st2[h]
        o_ref[0] = o2.astype(o_ref.dtype)


def _hgrn2_call(z3, lower_bound, norm_w, tb=256):
    b, t, _ = z3.shape
    tb = min(tb, t)
    base = HG_OFF // HG_W

    def zspec(part):
        return pl.BlockSpec((1, tb, HG_W), lambda i, j: (i, j, base + part))

    return pl.pallas_call(
        _hgrn2_kernel,
        grid=(b, t // tb),
        in_specs=[
            zspec(0), zspec(1), zspec(2), zspec(3),
            pl.BlockSpec((1, HG_W), lambda i, j: (0, 0)),
            pl.BlockSpec((1, LANES), lambda i, j: (0, 0)),
        ],
        out_specs=pl.BlockSpec((1, tb, HG_W), lambda i, j: (i, j, 0)),
        out_shape=jax.ShapeDtypeStruct((b, t, HG_W), BF16),
        scratch_shapes=[pltpu.VMEM((HG_HEADS, HG_DK, HG_DK), F32)],
        compiler_params=_cparams(("parallel", "arbitrary")),
        name="hgrn2_mixer",
    )(z3, z3, z3, z3, lower_bound.reshape(1, HG_W), norm_w.reshape(1, HG_DK))


def _ret_kernel(zq_ref, zk_ref, zv_ref, zg_ref, cos_ref, sin_ref, o_ref, st_ref, dmask_ref, *, chunk):
    hs = range(RET_HEADS)
    sl = [slice(h * RET_DK, (h + 1) * RET_DK) for h in hs]
    lg = [jnp.log(jnp.full((1, 1), 1.0 - 2.0 ** (-5.0 - h), F32)) for h in hs]

    @pl.when(pl.program_id(1) == 0)
    def _():
        st_ref[...] = jnp.zeros_like(st_ref)
        row = lax.broadcasted_iota(jnp.int32, (chunk, chunk), 0)
        col = lax.broadcasted_iota(jnp.int32, (chunk, chunk), 1)
        rel = (row - col).astype(F32)
        for h in hs:
            dmask_ref[h] = jnp.where(rel >= 0.0, jnp.exp(jnp.maximum(rel, 0.0) * lg[h]), 0.0)

    cos2 = cos_ref[0]
    sin2 = sin_ref[0]
    half = RET_DK // 2

    def rope(z):
        return z * cos2 + pltpu.roll(z, half, 1) * sin2

    tcol = lax.broadcasted_iota(jnp.int32, (chunk, 1), 0).astype(F32)
    q = [rope(zq_ref[0, :, sl[h]].astype(F32)) * (RET_DK ** -0.5) for h in hs]
    k = [rope(zk_ref[0, :, sl[h]].astype(F32)) for h in hs]
    v_b = [zv_ref[0, :, sl[h]].astype(BF16) for h in hs]
    st = [st_ref[h] for h in hs]
    scores = [(_dot_nt(q[h].astype(BF16), k[h].astype(BF16)) * dmask_ref[h]).astype(BF16) for h in hs]
    qx = [(q[h] * jnp.exp((tcol + 1.0) * lg[h])).astype(BF16) for h in hs]
    kz = [(k[h] * jnp.exp((chunk - 1.0 - tcol) * lg[h])).astype(BF16) for h in hs]
    o = [_dot(scores[h], v_b[h]) + _dot_nt(qx[h], st[h].astype(BF16)) for h in hs]
    for h in hs:
        st_ref[h] = st[h] * jnp.exp(chunk * lg[h]) + _dot_tn(v_b[h], kz[h])
    o = [o[h] * lax.rsqrt(jnp.mean(o[h] * o[h], axis=-1, keepdims=True) + NORM_EPS) for h in hs]
    o_ref[0] = (jnp.concatenate(o, axis=1) * _silu(zg_ref[0].astype(F32))).astype(o_ref.dtype)


def _ret_call(z3, cos2, sin2, chunk=256):
    b, t, _ = z3.shape
    chunk = min(chunk, t)
    base = RET_OFF // RET_W

    def zspec(part):
        return pl.BlockSpec((1, chunk, RET_W), lambda i, j: (i, j, base + part))

    tab = pl.BlockSpec((1, chunk, RET_DK), lambda i, j: (i, j, 0))
    return pl.pallas_call(
        functools.partial(_ret_kernel, chunk=chunk),
        grid=(b, t // chunk),
        in_specs=[zspec(0), zspec(1), zspec(2), zspec(3), tab, tab],
        out_specs=pl.BlockSpec((1, chunk, RET_W), lambda i, j: (i, j, 0)),
        out_shape=jax.ShapeDtypeStruct((b, t, RET_W), BF16),
        scratch_shapes=[pltpu.VMEM((RET_HEADS, RET_DK, RET_DK), F32), pltpu.VMEM((RET_HEADS, chunk, chunk), F32)],
        compiler_params=_cparams(("parallel", "arbitrary")),
        name="retention_mixer",
    )(z3, z3, z3, z3, cos2, sin2)


def _pair_blockdiag(y, pair_mask):
    return jnp.where(pair_mask, jnp.concatenate([y, y], axis=0), 0.0).astype(BF16)


def _pair_dot(x, y, pair_mask):
    return _dot(x.astype(BF16), _pair_blockdiag(y, pair_mask))


def _inv_unit_lower(a, eye, blk_mask, pair_mask):
    c = a[0].shape[0]
    m = range(len(a))
    a_bd = [jnp.where(blk_mask, a[i], 0.0) for i in m]
    a_off = [a[i] - a_bd[i] for i in m]
    a2 = [_pair_dot(a_bd[i], a_bd[i], pair_mask) for i in m]
    p = [eye + a_bd[i] for i in m]
    r = [_pair_dot(jnp.concatenate([p[i], a2[i]], axis=0), a2[i], pair_mask) for i in m]
    p = [p[i] + r[i][:c] for i in m]
    a4 = [r[i][c:] for i in m]
    r = [_pair_dot(jnp.concatenate([p[i], a4[i]], axis=0), a4[i], pair_mask) for i in m]
    p = [p[i] + r[i][:c] for i in m]
    a8 = [r[i][c:] for i in m]
    t_bd = [p[i] + _pair_dot(p[i], a8[i], pair_mask) for i in m]
    n = [_pair_dot(t_bd[i], a_off[i], pair_mask) for i in m]
    n2 = [_pair_dot(n[i], n[i], pair_mask) for i in m]
    z = [t_bd[i] + _pair_dot(n[i], t_bd[i], pair_mask) for i in m]
    return [z[i] + _pair_dot(n2[i], z[i], pair_mask) for i in m]


def _rwkv_kernel(z_ref, mu_ref, w0_ref, w2_ref, a0_ref, a2_ref, g2_ref, kk_ref, ka_ref, rk_ref,
                 lnw_ref, lnb_ref, seg_ref, o_ref, s_ref, prev_ref):
    c = RW_CHUNK
    nbe, tb = z_ref.shape[0], z_ref.shape[1]
    nck = tb // c
    bs = range(nbe)

    @pl.when(pl.program_id(1) == 0)
    def _():
        s_ref[...] = jnp.zeros_like(s_ref)
        prev_ref[...] = jnp.zeros_like(prev_ref)

    seg = seg_ref[...]
    rows = lax.broadcasted_iota(jnp.int32, (tb, 1), 0)
    rowb = lax.broadcasted_iota(jnp.int32, (tb, tb), 0)
    colb = lax.broadcasted_iota(jnp.int32, (tb, tb), 1)
    tri = jnp.where(colb >= (rowb // c) * c, jnp.where(rowb >= colb, 1.0, 0.0), 0.0).astype(BF16)

    def front(bi):
        z = z_ref[bi].astype(F32)
        z_prev = jnp.where(rows == 0, prev_ref[bi:bi + 1, :], pltpu.roll(z, 1, 0))
        prev_ref[bi:bi + 1, :] = z[tb - 1:tb]
        zs = z + mu_ref[...] * (z_prev - z)
        r = zs[:, 0:RW_W]
        k = zs[:, RW_W:2 * RW_W]
        v = zs[:, 2 * RW_W:3 * RW_W]
        off = 3 * RW_W
        w_lo = zs[:, off:off + RW_DECAY_LORA]
        a_lo = zs[:, off + RW_DECAY_LORA:off + RW_DECAY_LORA + RW_A_LORA]
        g_lo = zs[:, off + RW_DECAY_LORA + RW_A_LORA:]
        wx = -(w0_ref[...] + _dot_x3(jnp.tanh(w_lo), w2_ref[...]))
        softplus = jnp.maximum(wx, 0.0) + jnp.log(1.0 + jnp.exp(-jnp.abs(wx)))
        logw = -jnp.exp(-softplus - 0.5)
        a = _sigmoid(a0_ref[...] + _dot_x3(a_lo, a2_ref[...]))
        g = _dot_x3(_sigmoid(g_lo), g2_ref[...])
        kk = k * kk_ref[...]
        kk = kk * lax.rsqrt(jnp.maximum(_dot_x2_lhs(kk * kk, seg), 1e-24))
        k2 = k * (1.0 + (a - 1.0) * ka_ref[...])
        cw = _dot_x2_rhs(tri, logw)
        w_inv = jnp.exp(-cw)
        last = jnp.concatenate([jnp.broadcast_to(cw[(ci + 1) * c - 1:(ci + 1) * c], (c, RW_W)) for ci in range(nck)],
                               axis=0)
        w_rest = jnp.exp(last - cw)
        beta = a * kk
        return dict(alpha_t=-kk * jnp.exp(cw - logw), r_t=r * jnp.exp(cw), beta_h=beta * w_inv, k_h=k2 * w_inv,
                    beta_d=beta * w_rest, k_d=k2 * w_rest, v=v, g=g, rkk=r * k2 * rk_ref[...], cw=cw)

    fr = [front(bi) for bi in bs]

    pw = 2 * RW_N
    row2 = lax.broadcasted_iota(jnp.int32, (c, pw), 0)
    col2 = lax.broadcasted_iota(jnp.int32, (c, pw), 1) % c
    incl2 = row2 >= col2
    strict2 = row2 > col2
    blk_mask = (row2 // RW_BLK) == (col2 // RW_BLK)
    eye = (row2 == col2).astype(F32)
    rowp = lax.broadcasted_iota(jnp.int32, (pw, pw), 0)
    colp = lax.broadcasted_iota(jnp.int32, (pw, pw), 1)
    pair_mask = (rowp // RW_N) == (colp // RW_N)

    ps = range(RW_HEADS // 2)
    items = [(bi, ci, p) for bi in bs for ci in range(nck) for p in ps]
    where = {key: i for i, key in enumerate(items)}
    m = range(len(items))

    def slab(name, i):
        bi, ci, p = items[i]
        return fr[bi][name][ci * c:(ci + 1) * c, p * pw:(p + 1) * pw]

    v2 = [slab("v", i) for i in m]
    lhs = [jnp.concatenate([slab("alpha_t", i), slab("r_t", i)], axis=0).astype(BF16) for i in m]
    rhs = [jnp.concatenate([_pair_blockdiag(slab("beta_h", i), pair_mask),
                            _pair_blockdiag(slab("k_h", i), pair_mask)], axis=0) for i in m]
    big = [_dot_nt(lhs[i], rhs[i]) for i in m]
    a_ab = [jnp.where(strict2, big[i][:c, :pw], 0.0) for i in m]
    a_ak = [jnp.where(strict2, big[i][:c, pw:], 0.0) for i in m]
    a_rb = [jnp.where(incl2, big[i][c:, :pw], 0.0) for i in m]
    a_rk = [jnp.where(incl2, big[i][c:, pw:], 0.0) for i in m]
    t_inv = _inv_unit_lower(a_ab, eye, blk_mask, pair_mask)
    av = [_pair_dot(a_ak[i], v2[i], pair_mask) for i in m]
    u_const = [_pair_dot(t_inv[i], av[i], pair_mask) for i in m]
    lhs_s = [jnp.concatenate([_pair_dot(t_inv[i], slab("alpha_t", i), pair_mask).astype(BF16),
                              slab("r_t", i).astype(BF16)], axis=0) for i in m]
    a_r = [jnp.concatenate([a_rb[i], a_rk[i]], axis=1).astype(BF16) for i in m]
    bk_d = [jnp.concatenate([slab("beta_d", i), slab("k_d", i)], axis=0).astype(BF16) for i in m]
    bp = [(bi, p) for bi in bs for p in ps]
    s_cur = {(bi, p): s_ref[bi * len(ps) + p] for bi, p in bp}
    o_parts = {}
    for ci in range(nck):
        ix = {(bi, p): where[bi, ci, p] for bi, p in bp}
        sd = {q: _dot_nt(lhs_s[ix[q]], s_cur[q].astype(BF16)) for q in bp}
        u = {q: sd[q][:c] + u_const[ix[q]] for q in bp}
        uv = {q: jnp.concatenate([_pair_blockdiag(u[q], pair_mask), _pair_blockdiag(v2[ix[q]], pair_mask)], axis=0)
              for q in bp}
        for q in bp:
            o_parts[q[0], ci, q[1]] = sd[q][c:] + _dot(a_r[ix[q]], uv[q])
        uvt = {q: jnp.concatenate([u[q], v2[ix[q]]], axis=0).astype(BF16) for q in bp}
        for bi, p in bp:
            w_last = jnp.exp(fr[bi]["cw"][(ci + 1) * c - 1:(ci + 1) * c, p * pw:(p + 1) * pw])
            s_cur[bi, p] = s_cur[bi, p] * w_last + jnp.where(pair_mask, _dot_tn(uvt[bi, p], bk_d[ix[bi, p]]), 0.0)
    for bi, p in bp:
        s_ref[bi * len(ps) + p] = s_cur[bi, p]

    for bi in bs:
        o = jnp.concatenate([jnp.concatenate([o_parts[bi, ci, p] for p in ps], axis=1) for ci in range(nck)], axis=0)
        mean = _dot_x2_lhs(o, seg) * (1.0 / RW_N)
        dev = o - mean
        var = _dot_x2_lhs(dev * dev, seg) * (1.0 / RW_N)
        o = dev * lax.rsqrt(var + RW_GN_EPS) * lnw_ref[...] + lnb_ref[...]
        bonus = _dot_x2_lhs(fr[bi]["rkk"], seg) * fr[bi]["v"]
        o_ref[bi] = ((o + bonus) * fr[bi]["g"]).astype(o_ref.dtype)


def _rwkv_call(z3, mu, w0, w2, a0, a2, g2, k_k, k_a, r_k, ln_w, ln_b):
    b, t, _ = z3.shape
    c = min(RW_TB, t)
    hid = lax.broadcasted_iota(jnp.int32, (RW_W, RW_W), 0) // RW_N
    seg = (hid == hid.T).astype(BF16)

    def vec(n):
        return pl.BlockSpec((1, n), lambda i, j: (0, 0))

    def mat(m, n):
        return pl.BlockSpec((m, n), lambda i, j: (0, 0))

    nbe = RW_NB if b % RW_NB == 0 else 1
    return pl.pallas_call(
        _rwkv_kernel,
        grid=(b // nbe, t // c),
        in_specs=[
            pl.BlockSpec((nbe, c, RW_COLS), lambda i, j: (i, j, RW_OFF // RW_COLS)),
            vec(RW_COLS), vec(RW_W), mat(RW_DECAY_LORA, RW_W), vec(RW_W), mat(RW_A_LORA, RW_W),
            mat(RW_GATE_LORA, RW_W), vec(RW_W), vec(RW_W), vec(RW_W), vec(RW_W), vec(RW_W),
            mat(RW_W, RW_W),
        ],
        out_specs=pl.BlockSpec((nbe, c, RW_W), lambda i, j: (i, j, 0)),
        out_shape=jax.ShapeDtypeStruct((b, t, RW_W), BF16),
        scratch_shapes=[pltpu.VMEM((nbe * (RW_HEADS // 2), 2 * RW_N, 2 * RW_N), F32),
                        pltpu.VMEM((nbe, RW_COLS), F32)],
        compiler_params=_cparams(("parallel", "arbitrary")),
        name="rwkv7_mixer",
    )(z3, mu.reshape(1, -1), w0.reshape(1, -1), w2, a0.reshape(1, -1), a2, g2, k_k.reshape(1, -1),
      k_a.reshape(1, -1), r_k.reshape(1, -1), ln_w.reshape(1, -1), ln_b.reshape(1, -1), seg)


def _merge_kernel(ohg_ref, oret_ref, orw_ref, zg_ref, x_ref, gate_ref, bhg_ref, bret_ref, brw_ref,
                  wout_ref, o_ref):
    d = x_ref.shape[1]
    y = _sigmoid(zg_ref[:, 0:d].astype(F32)) * _dot(ohg_ref[...], bhg_ref[...])
    y = y + _sigmoid(zg_ref[:, d:2 * d].astype(F32)) * _dot(oret_ref[...], bret_ref[...])
    y = y + _sigmoid(zg_ref[:, 2 * d:3 * d].astype(F32)) * _dot(orw_ref[...], brw_ref[...])
    o_ref[...] = x_ref[...] + gate_ref[0] * _dot(y.astype(BF16), wout_ref[...])


def _merge_call(o_hg, o_ret, o_rw, z2, x2, mod3, br_hg, br_ret, br_rw, w_out, seq, gate_blk, tm=1024):
    n, d = x2.shape
    tpb = seq // tm

    def rows(w):
        return pl.BlockSpec((tm, w), lambda i: (i, 0))

    def full(m, k):
        return pl.BlockSpec((m, k), lambda i: (0, 0))

    return pl.pallas_call(
        _merge_kernel,
        grid=(n // tm,),
        in_specs=[
            rows(HG_W), rows(RET_W), rows(RW_W), rows(3 * d), rows(d),
            pl.BlockSpec((1, 1, d), lambda i: (i // tpb, 0, gate_blk)),
            full(HG_W, d), full(RET_W, d), full(RW_W, d), full(d, d),
        ],
        out_specs=rows(d),
        out_shape=jax.ShapeDtypeStruct((n, d), F32),
        compiler_params=_cparams(("parallel",)),
        name="merge_outproj",
    )(o_hg, o_ret, o_rw, z2, x2, mod3, br_hg, br_ret, br_rw, w_out)


def _pack_bf16_pairs(x):
    w = x.shape[1] // 2
    hi = pltpu.bitcast(x[:, :w].astype(BF16).astype(F32), jnp.uint32)
    lo = pltpu.bitcast(x[:, w:].astype(BF16).astype(F32), jnp.uint32)
    return pltpu.bitcast(hi | lax.shift_right_logical(lo, jnp.uint32(16)), jnp.int32)


def _unpack_bf16_pairs(p):
    u = pltpu.bitcast(p, jnp.uint32)
    hi = pltpu.bitcast(u & jnp.uint32(0xFFFF0000), F32)
    lo = pltpu.bitcast(lax.shift_left(u, jnp.uint32(16)), F32)
    return jnp.concatenate([hi, lo], axis=1)


def _route_kernel(x_ref, g_ref, scale_ref, shift_ref, rc_ref, hp_ref, eid_ref, wts_ref, cnt_ref):
    @pl.when(pl.program_id(0) == 0)
    def _():
        cnt_ref[...] = jnp.zeros_like(cnt_ref)

    h = _rms_mod(x_ref[...], g_ref[...], scale_ref[0], shift_ref[0])
    hp_ref[...] = _pack_bf16_pairs(h)
    tm = h.shape[0]
    lane = lax.broadcasted_iota(jnp.int32, (tm, LANES), 1)
    neg = -jnp.inf
    logits = _dot_x3(h, rc_ref[...])
    gl = jnp.where(lane < N_GROUPS, logits, neg)
    gmax = jnp.max(gl, axis=-1, keepdims=True)
    gidx = jnp.min(jnp.where(gl == gmax, lane, LANES), axis=-1, keepdims=True)
    gw = 1.0 / jnp.sum(jnp.exp(gl - gmax), axis=-1, keepdims=True)
    lo = N_GROUPS + gidx * EXPERTS_PER_GROUP
    el = jnp.where(lane >= lo, jnp.where(lane < lo + EXPERTS_PER_GROUP, logits, neg), neg)
    m1 = jnp.max(el, axis=-1, keepdims=True)
    l1 = jnp.min(jnp.where(el == m1, lane, LANES), axis=-1, keepdims=True)
    el2 = jnp.where(lane == l1, neg, el)
    m2 = jnp.max(el2, axis=-1, keepdims=True)
    l2 = jnp.min(jnp.where(el2 == m2, lane, LANES), axis=-1, keepdims=True)
    i1 = l1 - N_GROUPS
    i2 = l2 - N_GROUPS
    e2 = jnp.exp(m2 - m1)
    p1 = 1.0 / (1.0 + e2)
    p2 = e2 * p1
    oh1 = jnp.where(lane == i1, 1.0, 0.0)
    oh2 = jnp.where(lane == i2, 1.0, 0.0)
    row = lax.broadcasted_iota(jnp.int32, (tm, tm), 0)
    col = lax.broadcasted_iota(jnp.int32, (tm, tm), 1)
    earlier = jnp.where(row > col, 1.0, 0.0).astype(BF16)
    before = _dot(earlier, jnp.concatenate([oh1, oh2], axis=1).astype(BF16))
    tot1 = jnp.sum(oh1, axis=0, keepdims=True)
    carry = cnt_ref[...]
    r1 = jnp.sum(oh1 * (before[:, :LANES] + carry), axis=-1, keepdims=True).astype(jnp.int32)
    r2 = jnp.sum(oh2 * (before[:, LANES:] + (carry + tot1)), axis=-1, keepdims=True).astype(jnp.int32)
    cnt_ref[...] = carry + tot1 + jnp.sum(oh2, axis=0, keepdims=True)
    eid_ref[...] = jnp.where(lane == 0, i1, jnp.where(lane == 1, i2, jnp.where(lane == 2, r1,
                                                                             jnp.where(lane == 3, r2, 0))))
    wts_ref[...] = jnp.where(lane == 0, gw * p1, jnp.where(lane == 1, gw * p2, 0.0))


def _route_call(x2, gain, mod3, router_g, router_e, seq, scale_blk, shift_blk, tm=1024):
    n, d = x2.shape
    tpb = seq // tm
    rc = jnp.pad(jnp.concatenate([router_g, router_e], axis=1), ((0, 0), (0, LANES - N_GROUPS - N_EXPERTS)))
    return pl.pallas_call(
        _route_kernel,
        grid=(n // tm,),
        in_specs=[
            pl.BlockSpec((tm, d), lambda i: (i, 0)),
            pl.BlockSpec((1, d), lambda i: (0, 0)),
            pl.BlockSpec((1, 1, d), lambda i: (i // tpb, 0, scale_blk)),
            pl.BlockSpec((1, 1, d), lambda i: (i // tpb, 0, shift_blk)),
            pl.BlockSpec((d, LANES), lambda i: (0, 0)),
        ],
        out_specs=[pl.BlockSpec((tm, d // 2), lambda i: (i, 0)), pl.BlockSpec((tm, LANES), lambda i: (i, 0)),
                   pl.BlockSpec((tm, LANES), lambda i: (i, 0)), pl.BlockSpec((1, LANES), lambda i: (0, 0))],
        out_shape=[jax.ShapeDtypeStruct((n, d // 2), jnp.int32), jax.ShapeDtypeStruct((n, LANES), jnp.int32),
                   jax.ShapeDtypeStruct((n, LANES), F32), jax.ShapeDtypeStruct((1, LANES), F32)],
        compiler_params=_cparams(("arbitrary",)),
        name="moe_route",
    )(x2, gain.reshape(1, d), mod3, mod3, rc)


SC_CORES = 2
SC_SUBCORES = 16
SC_WORKERS = SC_CORES * SC_SUBCORES
SC_ROWS = 32
SC_STREAMS = 4


def _sc_gather(table, idx):
    m = idx.shape[0]
    w = table.shape[1]
    per_worker = m // SC_WORKERS
    steps = per_worker // SC_ROWS
    assert per_worker * SC_WORKERS == m and steps * SC_ROWS == per_worker and steps % SC_STREAMS == 0
    mesh = plsc.VectorSubcoreMesh(core_axis_name="c", subcore_axis_name="s")
    ks = range(SC_STREAMS)

    def body(table_hbm, idx_hbm, out_hbm, idx_v, *rest):
        bufs, g_sems, w_sems = rest[:SC_STREAMS], rest[SC_STREAMS:2 * SC_STREAMS], rest[2 * SC_STREAMS:]
        wid = lax.axis_index("s") * SC_CORES + lax.axis_index("c")
        pltpu.sync_copy(idx_hbm.at[wid], idx_v)

        @pl.loop(0, steps, step=SC_STREAMS)
        def _(j):
            row0 = wid * per_worker + j * SC_ROWS
            gathers = [pltpu.async_copy(table_hbm.at[idx_v.at[j + q]], bufs[q], g_sems[q]) for q in ks]
            writes = []
            for q in ks:
                gathers[q].wait()
                writes.append(pltpu.async_copy(bufs[q], out_hbm.at[pl.ds(row0 + q * SC_ROWS, SC_ROWS)], w_sems[q]))
            for q in ks:
                writes[q].wait()

    return pl.kernel(
        body,
        out_type=jax.ShapeDtypeStruct((m, w), table.dtype),
        mesh=mesh,
        scratch_types=[pltpu.VMEM((steps, SC_ROWS), jnp.int32)] + [pltpu.VMEM((SC_ROWS, w), table.dtype)] * SC_STREAMS
        + [pltpu.SemaphoreType.DMA] * (2 * SC_STREAMS),
        name="sc_row_gather",
    )(table, idx.reshape(SC_WORKERS, steps, SC_ROWS))


def _sc_scatter2(rows, idx0, idx1, p):
    n, w = rows.shape
    per_worker = n // SC_WORKERS
    steps = per_worker // SC_ROWS
    assert per_worker * SC_WORKERS == n and steps * SC_ROWS == per_worker and steps % SC_STREAMS == 0
    mesh = plsc.VectorSubcoreMesh(core_axis_name="c", subcore_axis_name="s")
    ks = range(SC_STREAMS)

    def body(rows_hbm, i0_hbm, i1_hbm, out_hbm, i0_v, i1_v, *rest):
        bufs, r_sems = rest[:SC_STREAMS], rest[SC_STREAMS:2 * SC_STREAMS]
        s0_sems, s1_sems = rest[2 * SC_STREAMS:3 * SC_STREAMS], rest[3 * SC_STREAMS:]
        wid = lax.axis_index("s") * SC_CORES + lax.axis_index("c")
        pltpu.sync_copy(i0_hbm.at[wid], i0_v)
        pltpu.sync_copy(i1_hbm.at[wid], i1_v)

        @pl.loop(0, steps, step=SC_STREAMS)
        def _(j):
            row0 = wid * per_worker + j * SC_ROWS
            reads = [pltpu.async_copy(rows_hbm.at[pl.ds(row0 + q * SC_ROWS, SC_ROWS)], bufs[q], r_sems[q]) for q in ks]
            writes = []
            for q in ks:
                reads[q].wait()
                writes.append(pltpu.async_copy(bufs[q], out_hbm.at[i0_v.at[j + q]], s0_sems[q]))
                writes.append(pltpu.async_copy(bufs[q], out_hbm.at[i1_v.at[j + q]], s1_sems[q]))
            for wr in writes:
                wr.wait()

    index_block = pltpu.VMEM((steps, SC_ROWS), jnp.int32)
    return pl.kernel(
        body,
        out_type=jax.ShapeDtypeStruct((p, w), rows.dtype),
        mesh=mesh,
        scratch_types=[index_block, index_block] + [pltpu.VMEM((SC_ROWS, w), rows.dtype)] * SC_STREAMS
        + [pltpu.SemaphoreType.DMA] * (3 * SC_STREAMS),
        name="sc_row_scatter",
    )(rows, idx0.reshape(SC_WORKERS, steps, SC_ROWS), idx1.reshape(SC_WORKERS, steps, SC_ROWS))


MOE_TM = 512


def _gexperts_kernel(te_ref, tv_ref, nu_ref, xs_ref, w1_ref, w3_ref, w2_ref, ys_ref, w1b_ref, w3b_ref, w2b_ref):
    i = pl.program_id(0)

    @pl.when((i == 0) | (te_ref[i] != te_ref[jnp.maximum(i - 1, 0)]))
    def _():
        w1b_ref[...] = w1_ref[0].astype(BF16)
        w3b_ref[...] = w3_ref[0].astype(BF16)
        w2b_ref[...] = w2_ref[0].astype(BF16)

    @pl.when(i < nu_ref[0])
    def _():
        rid = lax.broadcasted_iota(jnp.int32, xs_ref.shape, 0)
        xb = _unpack_bf16_pairs(jnp.where(rid < tv_ref[i], xs_ref[...], 0)).astype(BF16)
        act = (_silu(_dot(xb, w1b_ref[...])) * _dot(xb, w3b_ref[...])).astype(BF16)
        ys_ref[...] = _pack_bf16_pairs(_dot(act, w2b_ref[...]))


def _gexperts_call(xs, tile_expert, tile_valid, n_used, w1, w3, w2):
    p, half = xs.shape
    ne, d, de = w1.shape
    nt = p // MOE_TM

    def rows(i, te, tv, nu):
        return (jnp.minimum(i, nu[0] - 1), 0)

    def wsel(i, te, tv, nu):
        return (te[i], 0, 0)

    return pl.pallas_call(
        _gexperts_kernel,
        grid_spec=pltpu.PrefetchScalarGridSpec(
            num_scalar_prefetch=3,
            grid=(nt,),
            in_specs=[
                pl.BlockSpec((MOE_TM, half), rows),
                pl.BlockSpec((1, d, de), wsel),
                pl.BlockSpec((1, d, de), wsel),
                pl.BlockSpec((1, de, d), wsel),
            ],
            out_specs=pl.BlockSpec((MOE_TM, half), rows),
            scratch_shapes=[pltpu.VMEM((d, de), BF16), pltpu.VMEM((d, de), BF16), pltpu.VMEM((de, d), BF16)],
        ),
        out_shape=jax.ShapeDtypeStruct((p, half), jnp.int32),
        compiler_params=_cparams(("arbitrary",)),
        name="moe_experts",
    )(tile_expert, tile_valid, n_used, xs, w1, w3, w2)


def _combine_kernel(y0_ref, y1_ref, wts_ref, x_ref, gate_ref, fg_ref, o_ref, *, final_norm):
    wts = wts_ref[...]
    moe = wts[:, 0:1] * _unpack_bf16_pairs(y0_ref[...]) + wts[:, 1:2] * _unpack_bf16_pairs(y1_ref[...])
    xn = x_ref[...] + gate_ref[0] * moe
    if final_norm:
        xn = xn * lax.rsqrt(jnp.mean(xn * xn, axis=-1, keepdims=True) + NORM_EPS) * fg_ref[...]
    o_ref[...] = xn


def _combine_call(yg, wts, x2, mod3, final_g, seq, gate_blk, final_norm, tm=1024):
    n, d = x2.shape
    tpb = seq // tm
    slot1 = n // tm
    return pl.pallas_call(
        functools.partial(_combine_kernel, final_norm=final_norm),
        grid=(n // tm,),
        in_specs=[
            pl.BlockSpec((tm, d // 2), lambda i: (i, 0)),
            pl.BlockSpec((tm, d // 2), lambda i: (i + slot1, 0)),
            pl.BlockSpec((tm, LANES), lambda i: (i, 0)),
            pl.BlockSpec((tm, d), lambda i: (i, 0)),
            pl.BlockSpec((1, 1, d), lambda i: (i // tpb, 0, gate_blk)),
            pl.BlockSpec((1, d), lambda i: (0, 0)),
        ],
        out_specs=pl.BlockSpec((tm, d), lambda i: (i, 0)),
        out_shape=jax.ShapeDtypeStruct((n, d), F32),
        compiler_params=_cparams(("parallel",)),
        name="moe_combine",
    )(yg, yg, wts, x2, mod3, final_g.reshape(1, d))


def _pos_kernel(eid_ref, ts_ref, p0_ref, p1_ref):
    eid = eid_ref[...]
    tm = eid.shape[0]
    lane = lax.broadcasted_iota(jnp.int32, (tm, LANES), 1)
    sub = lax.broadcasted_iota(jnp.int32, (tm, LANES), 0) % LANES
    for slot, out_ref in ((0, p0_ref), (1, p1_ref)):
        first_row = jnp.sum(jnp.where(lane == eid[:, slot:slot + 1], ts_ref[...], 0), axis=-1, keepdims=True)
        pos = first_row + eid[:, slot + 2:slot + 3]
        out_ref[...] = jnp.sum(jnp.where(lane == sub, pos, 0).reshape(tm // LANES, LANES, LANES), axis=1)


def _pos_call(eid, first_rows, tm=4096):
    n = eid.shape[0]
    out = jax.ShapeDtypeStruct((n // LANES, LANES), jnp.int32)
    p0, p1 = pl.pallas_call(
        _pos_kernel,
        grid=(n // tm,),
        in_specs=[pl.BlockSpec((tm, LANES), lambda i: (i, 0)), pl.BlockSpec((1, LANES), lambda i: (0, 0))],
        out_specs=[pl.BlockSpec((tm // LANES, LANES), lambda i: (i, 0))] * 2,
        out_shape=[out, out],
        compiler_params=_cparams(("parallel",)),
        name="moe_positions",
    )(eid, first_rows)
    return p0.reshape(n), p1.reshape(n)


def _moe_plan(eid, counts_f):
    n = eid.shape[0]
    nt = (2 * n) // MOE_TM + N_EXPERTS
    counts = counts_f[0, :N_EXPERTS].astype(jnp.int32)
    tiles = (counts + MOE_TM - 1) // MOE_TM
    tile_end = jnp.cumsum(tiles)
    tile_start = tile_end - tiles
    n_used = tile_end[-1:]
    tile_iota = jnp.arange(nt, dtype=jnp.int32)
    tile_expert = jnp.sum(jnp.minimum(tile_iota, n_used - 1)[:, None] >= tile_end[None, :], axis=1, dtype=jnp.int32)
    own = tile_expert[:, None] == jnp.arange(N_EXPERTS, dtype=jnp.int32)[None, :]
    count_t = jnp.sum(jnp.where(own, counts[None, :], 0), axis=1)
    start_t = jnp.sum(jnp.where(own, tile_start[None, :], 0), axis=1)
    tile_valid = jnp.clip(count_t - (tile_iota - start_t) * MOE_TM, 0, MOE_TM)
    first_rows = jnp.pad(tile_start * MOE_TM, (0, LANES - N_EXPERTS)).reshape(1, LANES)
    pos0, pos1 = _pos_call(eid, first_rows)
    return pos0, pos1, tile_expert, tile_valid, n_used


def kernel(x, c, positions, ada_w, ada_b, norm1_g, norm2_g, w_in, hg_lb_table, hg_norm_w, rw_mu, rw_w0, rw_w2,
           rw_a0, rw_a2, rw_g2, rw_k_k, rw_k_a, rw_r_k, rw_ln_w, rw_ln_b, br_hg, br_ret, br_rw, w_out,
           router_g, router_e, moe_w1, moe_w3, moe_w2, final_g):
    b, t, d = x.shape
    depth = ada_w.shape[0]
    n = b * t
    assert w_in.shape[2] == IN_COLS and d == 1024

    lb_p = jax.nn.softmax(hg_lb_table.astype(F32), axis=0)
    lower_bounds = jnp.cumsum(lb_p, axis=0) - lb_p[0]

    mod = _mod_call(c, ada_w, ada_b)
    cos2, sin2 = _rope_call(positions, RET_DK)
    x2 = x.reshape(n, d)
    for l in range(depth):
        mod3 = mod[l].reshape(b, 1, 6 * d)
        z2 = _inproj_call(x2, norm1_g[l], mod3, _wprep_call(w_in, l), t, scale_blk=1, shift_blk=0)
        z3 = z2.reshape(b, t, IN_COLS)
        o_hg = _hgrn2_call(z3, lower_bounds[l], hg_norm_w[l])
        o_ret = _ret_call(z3, cos2, sin2)
        o_rw = _rwkv_call(z3, rw_mu[l], rw_w0[l], rw_w2[l], rw_a0[l], rw_a2[l], rw_g2[l], rw_k_k[l],
                          rw_k_a[l], rw_r_k[l], rw_ln_w[l], rw_ln_b[l])
        x2 = _merge_call(o_hg.reshape(n, HG_W), o_ret.reshape(n, RET_W), o_rw.reshape(n, RW_W), z2, x2, mod3,
                         br_hg[l].astype(BF16), br_ret[l].astype(BF16), br_rw[l].astype(BF16),
                         w_out[l].astype(BF16), t, gate_blk=2)
        hp, eid, wts, counts = _route_call(x2, norm2_g[l], mod3, router_g[l], router_e[l], t, scale_blk=4,
                                           shift_blk=3)
        pos0, pos1, tile_expert, tile_valid, n_used = _moe_plan(eid, counts)
        xs = _sc_scatter2(hp, pos0, pos1, (2 * n // MOE_TM + N_EXPERTS) * MOE_TM)
        ys = _gexperts_call(xs, tile_expert + l * N_EXPERTS, tile_valid, n_used,
                            moe_w1.reshape((-1,) + moe_w1.shape[2:]), moe_w3.reshape((-1,) + moe_w3.shape[2:]),
                            moe_w2.reshape((-1,) + moe_w2.shape[2:]))
        yg = _sc_gather(ys, jnp.concatenate([pos0, pos1]))
        x2 = _combine_call(yg, wts, x2, mod3, final_g, t, gate_blk=5, final_norm=(l == depth - 1))
    return x2.reshape(b, t, d)
```

```python
import functools

import jax
import jax.numpy as jnp
from jax import lax
from jax.experimental import pallas as pl
from jax.experimental.pallas import tpu as pltpu
from jax.experimental.pallas import tpu_sc as plsc

F32 = jnp.float32
BF16 = jnp.bfloat16
HIGHEST = lax.Precision.HIGHEST

HG_HEADS = 4
HG_DK = 128
HG_W = HG_HEADS * HG_DK
RET_HEADS = 4
RET_DK = 128
RET_W = RET_HEADS * RET_DK
RW_HEADS = 8
RW_N = 64
RW_W = RW_HEADS * RW_N
RW_DECAY_LORA = 64
RW_A_LORA = 64
RW_GATE_LORA = 128
RW_COLS = 3 * RW_W + RW_DECAY_LORA + RW_A_LORA + RW_GATE_LORA
RW_GN_EPS = 64e-5
N_GROUPS = 4
EXPERTS_PER_GROUP = 8
N_EXPERTS = N_GROUPS * EXPERTS_PER_GROUP
ROPE_THETA = 10000.0
NORM_EPS = 1e-6

LANES = 128
LOG2E = 1.4426950408889634
VMEM_LIMIT = 56 * 1024 * 1024

GATE_OFF = 0
HG_OFF = 3 * 1024
RET_OFF = HG_OFF + 4 * HG_W
RW_OFF = RET_OFF + 4 * RET_W
IN_COLS = RW_OFF + RW_COLS

HG_CHUNK = 64
HG_SUB = 16
HG_SAFE_SPAN = 60.0
RW_CHUNK = 64
RW_BLK = 16
RW_NB = 4
RW_TB = 256
Z_DTYPE = BF16


def _cparams(sem):
    return pltpu.CompilerParams(dimension_semantics=sem, vmem_limit_bytes=VMEM_LIMIT)


def _dot(a, b, precision=None):
    return jnp.dot(a, b, preferred_element_type=F32, precision=precision)


def _dot_nt(a, b, precision=None):
    return lax.dot_general(a, b, (((1,), (1,)), ((), ())), preferred_element_type=F32, precision=precision)


def _dot_tn(a, b, precision=None):
    return lax.dot_general(a, b, (((0,), (0,)), ((), ())), preferred_element_type=F32, precision=precision)


def _split_bf16(x):
    hi = x.astype(BF16)
    return hi, (x - hi.astype(F32)).astype(BF16)


def _dot_x3(a, b):
    ah, al = _split_bf16(a)
    bh, bl = _split_bf16(b)
    return _dot(ah, bh) + _dot(ah, bl) + _dot(al, bh)


def _dot_x2_lhs(a, b_exact):
    ah, al = _split_bf16(a)
    return _dot(ah, b_exact) + _dot(al, b_exact)


def _dot_x2_rhs(a_exact, b):
    bh, bl = _split_bf16(b)
    return _dot(a_exact, bh) + _dot(a_exact, bl)


def _sigmoid(x):
    return 0.5 * jnp.tanh(0.5 * x) + 0.5


def _silu(x):
    return x * _sigmoid(x)


def _rms_mod(x, gain, scale, shift):
    y = x * lax.rsqrt(jnp.mean(x * x, axis=-1, keepdims=True) + NORM_EPS)
    return (y * gain) * (1.0 + scale) + shift


def _mod_kernel(c_ref, w_ref, b_ref, o_ref):
    c = c_ref[...]
    o_ref[0] = _dot(_silu(c), w_ref[0], HIGHEST) + b_ref[0]


def _mod_call(c, ada_w, ada_b):
    depth, d, d6 = ada_w.shape
    b = c.shape[0]
    nblk = d6 // d
    return pl.pallas_call(
        _mod_kernel,
        grid=(depth, nblk),
        in_specs=[
            pl.BlockSpec((b, d), lambda l, j: (0, 0)),
            pl.BlockSpec((1, d, d), lambda l, j: (l, 0, j)),
            pl.BlockSpec((1, 1, d), lambda l, j: (l, 0, j)),
        ],
        out_specs=pl.BlockSpec((1, b, d), lambda l, j: (l, 0, j)),
        out_shape=jax.ShapeDtypeStruct((depth, b, d6), F32),
        compiler_params=_cparams(("parallel", "parallel")),
        name="adaln_mod",
    )(c, ada_w, ada_b.reshape(depth, 1, d6))


def _rope_kernel(pos_ref, freq_ref, sign_ref, cos_ref, sin_ref):
    ang = pos_ref[0].astype(F32) * freq_ref[...]
    cos_ref[0] = jnp.cos(ang)
    sin_ref[0] = jnp.sin(ang) * sign_ref[...]


def _rope_call(positions, d):
    b, t = positions.shape
    tb = min(t, 512)
    inv_freq = ROPE_THETA ** (-jnp.arange(0, d, 2, dtype=F32) / d)
    freq2 = jnp.concatenate([inv_freq, inv_freq]).reshape(1, d)
    sign2 = jnp.concatenate([-jnp.ones((d // 2,), F32), jnp.ones((d // 2,), F32)]).reshape(1, d)
    out = jax.ShapeDtypeStruct((b, t, d), F32)
    return pl.pallas_call(
        _rope_kernel,
        grid=(b, t // tb),
        in_specs=[
            pl.BlockSpec((1, tb, 1), lambda i, j: (i, j, 0)),
            pl.BlockSpec((1, d), lambda i, j: (0, 0)),
            pl.BlockSpec((1, d), lambda i, j: (0, 0)),
        ],
        out_specs=[pl.BlockSpec((1, tb, d), lambda i, j: (i, j, 0))] * 2,
        out_shape=[out, out],
        compiler_params=_cparams(("parallel", "parallel")),
        name="rope_tables",
    )(positions.reshape(b, t, 1), freq2, sign2)


W_BLK = 256


def _wprep_kernel(w_ref, o_ref):
    o_ref[...] = w_ref[...].astype(o_ref.dtype)


def _wprep_call(w_in, layer):
    _, d, cols = w_in.shape
    nblk = cols // W_BLK
    first = (cols - 3 * d) // W_BLK
    return pl.pallas_call(
        _wprep_kernel,
        grid=(nblk,),
        in_specs=[pl.BlockSpec((1, d, W_BLK), lambda j: (layer, 0, (j + first) % nblk))],
        out_specs=pl.BlockSpec((1, d, W_BLK), lambda j: (0, 0, j)),
        out_shape=jax.ShapeDtypeStruct((1, d, cols), BF16),
        compiler_params=_cparams(("parallel",)),
        name="w_in_layout",
    )(w_in)


def _inproj_kernel(x_ref, g_ref, scale_ref, shift_ref, w_ref, o_ref, h_ref):
    @pl.when(pl.program_id(1) == 0)
    def _():
        h = _rms_mod(x_ref[...], g_ref[...], scale_ref[0], shift_ref[0])
        h_ref[...] = h.astype(BF16)

    o_ref[...] = _dot(h_ref[...], w_ref[0]).astype(o_ref.dtype)


def _inproj_call(x2, gain, mod3, w_bf16, seq, scale_blk, shift_blk, tm=2048, tn=1792):
    n, d = x2.shape
    cols = w_bf16.shape[2]
    tpb = seq // tm
    return pl.pallas_call(
        _inproj_kernel,
        grid=(n // tm, cols // tn),
        in_specs=[
            pl.BlockSpec((tm, d), lambda i, j: (i, 0)),
            pl.BlockSpec((1, d), lambda i, j: (0, 0)),
            pl.BlockSpec((1, 1, d), lambda i, j: (i // tpb, 0, scale_blk)),
            pl.BlockSpec((1, 1, d), lambda i, j: (i // tpb, 0, shift_blk)),
            pl.BlockSpec((1, d, tn), lambda i, j: (0, 0, j)),
        ],
        out_specs=pl.BlockSpec((tm, tn), lambda i, j: (i, j)),
        out_shape=jax.ShapeDtypeStruct((n, cols), Z_DTYPE),
        scratch_shapes=[pltpu.VMEM((tm, d), BF16)],
        compiler_params=_cparams(("parallel", "arbitrary")),
        name="norm_inproj",
    )(x2, gain.reshape(1, d), mod3, mod3, w_bf16)


def _hgrn2_block(zs, lbs, nw, sts, factored):
    hs = range(len(zs))
    tb = zs[0][0].shape[0]
    c, sub = HG_CHUNK, HG_SUB
    nc, ns, nb = tb // c, c // sub, tb // sub
    f = [lbs[h] + (1.0 - lbs[h]) * _sigmoid(zs[h][1]) for h in hs]
    logf = [jnp.log(jnp.maximum(f[h], 1e-30)) for h in hs]
    q = [_silu(zs[h][0]) * (HG_DK ** -0.5) for h in hs]
    k = [1.0 - f[h] for h in hs]
    v = [zs[h][2] for h in hs]
    v_b = [v[h].astype(BF16) for h in hs]
    row = lax.broadcasted_iota(jnp.int32, (tb, tb), 0)
    col = lax.broadcasted_iota(jnp.int32, (tb, tb), 1)
    tri = jnp.where(col >= (row // c) * c, jnp.where(row >= col, 1.0, 0.0), 0.0).astype(BF16)
    cum = [_dot_x2_rhs(tri, logf[h]) for h in hs]
    cum3 = [cum[h].reshape(nb, sub, HG_DK) for h in hs]
    ref3 = [cum3[h][:, 0:1, :] - logf[h].reshape(nb, sub, HG_DK)[:, 0:1, :] for h in hs]
    span = functools.reduce(jnp.maximum, [jnp.max(ref3[h] - cum3[h][:, sub - 1:sub, :]) for h in hs])
    qe = [(q[h] * jnp.exp(cum[h])).astype(BF16) for h in hs]

    offd = [(h, ci * c, ci * c + sub * i) for h in hs for ci in range(nc) for i in range(1, ns)]
    base = [cum[h][lo - 1:lo] for h, _, lo in offd]
    qt = [(q[h][lo:lo + sub] * jnp.exp(cum[h][lo:lo + sub] - base[j])).astype(BF16)
          for j, (h, _, lo) in enumerate(offd)]
    kt = [(k[h][r0:lo] * jnp.exp(base[j] - cum[h][r0:lo])).astype(BF16) for j, (h, r0, lo) in enumerate(offd)]
    a = [_dot_nt(qt[j], kt[j]).astype(BF16) for j in range(len(offd))]
    av = {(h, lo): _dot(a[j], v_b[h][r0:lo]) for j, (h, r0, lo) in enumerate(offd)}

    cs = [slice(ci * c, (ci + 1) * c) for ci in range(nc)]
    hc = [(h, ci) for h in hs for ci in range(nc)]
    last = {(h, ci): cum[h][(ci + 1) * c - 1:(ci + 1) * c] for h, ci in hc}
    kd = {(h, ci): (k[h][cs[ci]] * jnp.exp(last[h, ci] - cum[h][cs[ci]])).astype(BF16) for h, ci in hc}
    inc = {(h, ci): _dot_tn(v_b[h][cs[ci]], kd[h, ci]) for h, ci in hc}
    s_in = {(h, 0): sts[h] for h in hs}
    for ci in range(nc):
        for h in hs:
            s_in[h, ci + 1] = s_in[h, ci] * jnp.exp(last[h, ci]) + inc[h, ci]
    o_inter = {(h, ci): _dot_nt(qe[h][cs[ci]], s_in[h, ci].astype(BF16)) for h, ci in hc}

    if factored:
        qf = [(q[h] * jnp.exp(cum3[h] - ref3[h]).reshape(tb, HG_DK)).astype(BF16) for h in hs]
        kf = [(k[h] * jnp.exp(ref3[h] - cum3[h]).reshape(tb, HG_DK)).astype(BF16) for h in hs]
        rc = lax.broadcasted_iota(jnp.int32, (c, c), 0)
        cc = lax.broadcasted_iota(jnp.int32, (c, c), 1)
        keep = (rc >= cc) & (rc // sub == cc // sub)
        a_d = {(h, ci): jnp.where(keep, _dot_nt(qf[h][cs[ci]], kf[h][cs[ci]]), 0.0).astype(BF16) for h, ci in hc}
        dg = {(h, ci): _dot(a_d[h, ci], v_b[h][cs[ci]]) for h, ci in hc}
        diag = [jnp.concatenate([dg[h, ci] for ci in range(nc)], axis=0) for h in hs]
    else:
        gb = 4
        trow = lax.broadcasted_iota(jnp.int32, (gb, sub, HG_DK), 1)
        diag = []
        for h in hs:
            c2 = cum[h] * LOG2E
            ks2 = c2 - jnp.log2(k[h])
            parts = []
            for g0 in range(0, nb, gb):
                rws = slice(g0 * sub, (g0 + gb) * sub)
                c23, ks23, q3, v3 = (x[rws].reshape(gb, sub, HG_DK) for x in (c2, ks2, q[h], v[h]))
                acc = jnp.zeros((gb, sub, HG_DK), F32)
                for s in range(sub):
                    e = jnp.exp2(jnp.where(trow >= s, c23 - ks23[:, s:s + 1, :], -jnp.inf))
                    a_col = jnp.sum(q3 * e, axis=-1, keepdims=True)
                    acc = acc + a_col * v3[:, s:s + 1, :]
                parts.append(acc.reshape(gb * sub, HG_DK))
            diag.append(jnp.concatenate(parts, axis=0))

    outs = []
    for h in hs:
        pieces = []
        for ci in range(nc):
            for i in range(ns):
                lo = ci * c + sub * i
                piece = o_inter[h, ci][sub * i:sub * (i + 1)] + diag[h][lo:lo + sub]
                pieces.append(piece + av[h, lo] if i > 0 else piece)
        o = jnp.concatenate(pieces, axis=0)
        o = o * lax.rsqrt(jnp.mean(o * o, axis=-1, keepdims=True) + NORM_EPS)
        outs.append(o * nw * _silu(zs[h][3]))
    return outs, [s_in[h, nc] for h in hs], span


def _hgrn2_kernel(zq_ref, zf_ref, zi_ref, zg_ref, lb_ref, nw_ref, o_ref, st_ref):
    @pl.when(pl.program_id(1) == 0)
    def _():
        st_ref[...] = jnp.zeros_like(st_ref)

    hs = range(HG_HEADS)
    sl = [slice(h * HG_DK, (h + 1) * HG_DK) for h in hs]

    def run(factored):
        zs = [tuple(r[0, :, sl[h]].astype(F32) for r in (zq_ref, zf_ref, zi_ref, zg_ref)) for h in hs]
        outs, sts, span = _hgrn2_block(zs, [lb_ref[:, sl[h]] for h in hs], nw_ref[...],
                                       [st_ref[h] for h in hs], factored)
        return jnp.concatenate(outs, axis=1), sts, span

    st_old = [st_ref[h] for h in hs]
    o, st_new, span = run(True)
    for h in hs:
        st_ref[h] = st_new[h]
    o_ref[0] = o.astype(o_ref.dtype)

    @pl.when(span > HG_SAFE_SPAN)
    def _():
        for h in hs:
            st_ref[h] = st_old[h]
        o2, st2, _ = run(False)
        for h in hs:
            st_ref[h] = st2[h]
        o_ref[0] = o2.astype(o_ref.dtype)


def _hgrn2_call(z3, lower_bound, norm_w, tb=256):
    b, t, _ = z3.shape
    tb = min(tb, t)
    base = HG_OFF // HG_W

    def zspec(part):
        return pl.BlockSpec((1, tb, HG_W), lambda i, j: (i, j, base + part))

    return pl.pallas_call(
        _hgrn2_kernel,
        grid=(b, t // tb),
        in_specs=[
            zspec(0), zspec(1), zspec(2), zspec(3),
            pl.BlockSpec((1, HG_W), lambda i, j: (0, 0)),
            pl.BlockSpec((1, LANES), lambda i, j: (0, 0)),
        ],
        out_specs=pl.BlockSpec((1, tb, HG_W), lambda i, j: (i, j, 0)),
        out_shape=jax.ShapeDtypeStruct((b, t, HG_W), BF16),
        scratch_shapes=[pltpu.VMEM((HG_HEADS, HG_DK, HG_DK), F32)],
        compiler_params=_cparams(("parallel", "arbitrary")),
        name="hgrn2_mixer",
    )(z3, z3, z3, z3, lower_bound.reshape(1, HG_W), norm_w.reshape(1, HG_DK))


def _ret_kernel(zq_ref, zk_ref, zv_ref, zg_ref, cos_ref, sin_ref, o_ref, st_ref, dmask_ref, *, chunk):
    hs = range(RET_HEADS)
    sl = [slice(h * RET_DK, (h + 1) * RET_DK) for h in hs]
    lg = [jnp.log(jnp.full((1, 1), 1.0 - 2.0 ** (-5.0 - h), F32)) for h in hs]

    @pl.when(pl.program_id(1) == 0)
    def _():
        st_ref[...] = jnp.zeros_like(st_ref)
        row = lax.broadcasted_iota(jnp.int32, (chunk, chunk), 0)
        col = lax.broadcasted_iota(jnp.int32, (chunk, chunk), 1)
        rel = (row - col).astype(F32)
        for h in hs:
            dmask_ref[h] = jnp.where(rel >= 0.0, jnp.exp(jnp.maximum(rel, 0.0) * lg[h]), 0.0)

    cos2 = cos_ref[0]
    sin2 = sin_ref[0]
    half = RET_DK // 2

    def rope(z):
        return z * cos2 + pltpu.roll(z, half, 1) * sin2

    tcol = lax.broadcasted_iota(jnp.int32, (chunk, 1), 0).astype(F32)
    q = [rope(zq_ref[0, :, sl[h]].astype(F32)) * (RET_DK ** -0.5) for h in hs]
    k = [rope(zk_ref[0, :, sl[h]].astype(F32)) for h in hs]
    v_b = [zv_ref[0, :, sl[h]].astype(BF16) for h in hs]
    st = [st_ref[h] for h in hs]
    scores = [(_dot_nt(q[h].astype(BF16), k[h].astype(BF16)) * dmask_ref[h]).astype(BF16) for h in hs]
    qx = [(q[h] * jnp.exp((tcol + 1.0) * lg[h])).astype(BF16) for h in hs]
    kz = [(k[h] * jnp.exp((chunk - 1.0 - tcol) * lg[h])).astype(BF16) for h in hs]
    o = [_dot(scores[h], v_b[h]) + _dot_nt(qx[h], st[h].astype(BF16)) for h in hs]
    for h in hs:
        st_ref[h] = st[h] * jnp.exp(chunk * lg[h]) + _dot_tn(v_b[h], kz[h])
    o = [o[h] * lax.rsqrt(jnp.mean(o[h] * o[h], axis=-1, keepdims=True) + NORM_EPS) for h in hs]
    o_ref[0] = (jnp.concatenate(o, axis=1) * _silu(zg_ref[0].astype(F32))).astype(o_ref.dtype)


def _ret_call(z3, cos2, sin2, chunk=256):
    b, t, _ = z3.shape
    chunk = min(chunk, t)
    base = RET_OFF // RET_W

    def zspec(part):
        return pl.BlockSpec((1, chunk, RET_W), lambda i, j: (i, j, base + part))

    tab = pl.BlockSpec((1, chunk, RET_DK), lambda i, j: (i, j, 0))
    return pl.pallas_call(
        functools.partial(_ret_kernel, chunk=chunk),
        grid=(b, t // chunk),
        in_specs=[zspec(0), zspec(1), zspec(2), zspec(3), tab, tab],
        out_specs=pl.BlockSpec((1, chunk, RET_W), lambda i, j: (i, j, 0)),
        out_shape=jax.ShapeDtypeStruct((b, t, RET_W), BF16),
        scratch_shapes=[pltpu.VMEM((RET_HEADS, RET_DK, RET_DK), F32), pltpu.VMEM((RET_HEADS, chunk, chunk), F32)],
        compiler_params=_cparams(("parallel", "arbitrary")),
        name="retention_mixer",
    )(z3, z3, z3, z3, cos2, sin2)


def _pair_blockdiag(y, pair_mask):
    return jnp.where(pair_mask, jnp.concatenate([y, y], axis=0), 0.0).astype(BF16)


def _pair_dot(x, y, pair_mask):
    return _dot(x.astype(BF16), _pair_blockdiag(y, pair_mask))


def _inv_unit_lower(a, eye, blk_mask, pair_mask):
    c = a[0].shape[0]
    m = range(len(a))
    a_bd = [jnp.where(blk_mask, a[i], 0.0) for i in m]
    a_off = [a[i] - a_bd[i] for i in m]
    a2 = [_pair_dot(a_bd[i], a_bd[i], pair_mask) for i in m]
    p = [eye + a_bd[i] for i in m]
    r = [_pair_dot(jnp.concatenate([p[i], a2[i]], axis=0), a2[i], pair_mask) for i in m]
    p = [p[i] + r[i][:c] for i in m]
    a4 = [r[i][c:] for i in m]
    r = [_pair_dot(jnp.concatenate([p[i], a4[i]], axis=0), a4[i], pair_mask) for i in m]
    p = [p[i] + r[i][:c] for i in m]
    a8 = [r[i][c:] for i in m]
    t_bd = [p[i] + _pair_dot(p[i], a8[i], pair_mask) for i in m]
    n = [_pair_dot(t_bd[i], a_off[i], pair_mask) for i in m]
    n2 = [_pair_dot(n[i], n[i], pair_mask) for i in m]
    z = [t_bd[i] + _pair_dot(n[i], t_bd[i], pair_mask) for i in m]
    return [z[i] + _pair_dot(n2[i], z[i], pair_mask) for i in m]


def _rwkv_kernel(z_ref, mu_ref, w0_ref, w2_ref, a0_ref, a2_ref, g2_ref, kk_ref, ka_ref, rk_ref,
                 lnw_ref, lnb_ref, seg_ref, o_ref, s_ref, prev_ref):
    c = RW_CHUNK
    nbe, tb = z_ref.shape[0], z_ref.shape[1]
    nck = tb // c
    bs = range(nbe)

    @pl.when(pl.program_id(1) == 0)
    def _():
        s_ref[...] = jnp.zeros_like(s_ref)
        prev_ref[...] = jnp.zeros_like(prev_ref)

    seg = seg_ref[...]
    rows = lax.broadcasted_iota(jnp.int32, (tb, 1), 0)
    rowb = lax.broadcasted_iota(jnp.int32, (tb, tb), 0)
    colb = lax.broadcasted_iota(jnp.int32, (tb, tb), 1)
    tri = jnp.where(colb >= (rowb // c) * c, jnp.where(rowb >= colb, 1.0, 0.0), 0.0).astype(BF16)

    def front(bi):
        z = z_ref[bi].astype(F32)
        z_prev = jnp.where(rows == 0, prev_ref[bi:bi + 1, :], pltpu.roll(z, 1, 0))
        prev_ref[bi:bi + 1, :] = z[tb - 1:tb]
        zs = z + mu_ref[...] * (z_prev - z)
        r = zs[:, 0:RW_W]
        k = zs[:, RW_W:2 * RW_W]
        v = zs[:, 2 * RW_W:3 * RW_W]
        off = 3 * RW_W
        w_lo = zs[:, off:off + RW_DECAY_LORA]
        a_lo = zs[:, off + RW_DECAY_LORA:off + RW_DECAY_LORA + RW_A_LORA]
        g_lo = zs[:, off + RW_DECAY_LORA + RW_A_LORA:]
        wx = -(w0_ref[...] + _dot_x3(jnp.tanh(w_lo), w2_ref[...]))
        softplus = jnp.maximum(wx, 0.0) + jnp.log(1.0 + jnp.exp(-jnp.abs(wx)))
        logw = -jnp.exp(-softplus - 0.5)
        a = _sigmoid(a0_ref[...] + _dot_x3(a_lo, a2_ref[...]))
        g = _dot_x3(_sigmoid(g_lo), g2_ref[...])
        kk = k * kk_ref[...]
        kk = kk * lax.rsqrt(jnp.maximum(_dot_x2_lhs(kk * kk, seg), 1e-24))
        k2 = k * (1.0 + (a - 1.0) * ka_ref[...])
        cw = _dot_x2_rhs(tri, logw)
        w_inv = jnp.exp(-cw)
        last = jnp.concatenate([jnp.broadcast_to(cw[(ci + 1) * c - 1:(ci + 1) * c], (c, RW_W)) for ci in range(nck)],
                               axis=0)
        w_rest = jnp.exp(last - cw)
        beta = a * kk
        return dict(alpha_t=-kk * jnp.exp(cw - logw), r_t=r * jnp.exp(cw), beta_h=beta * w_inv, k_h=k2 * w_inv,
                    beta_d=beta * w_rest, k_d=k2 * w_rest, v=v, g=g, rkk=r * k2 * rk_ref[...], cw=cw)

    fr = [front(bi) for bi in bs]

    pw = 2 * RW_N
    row2 = lax.broadcasted_iota(jnp.int32, (c, pw), 0)
    col2 = lax.broadcasted_iota(jnp.int32, (c, pw), 1) % c
    incl2 = row2 >= col2
    strict2 = row2 > col2
    blk_mask = (row2 // RW_BLK) == (col2 // RW_BLK)
    eye = (row2 == col2).astype(F32)
    rowp = lax.broadcasted_iota(jnp.int32, (pw, pw), 0)
    colp = lax.broadcasted_iota(jnp.int32, (pw, pw), 1)
    pair_mask = (rowp // RW_N) == (colp // RW_N)

    ps = range(RW_HEADS // 2)
    items = [(bi, ci, p) for bi in bs for ci in range(nck) for p in ps]
    where = {key: i for i, key in enumerate(items)}
    m = range(len(items))

    def slab(name, i):
        bi, ci, p = items[i]
        return fr[bi][name][ci * c:(ci + 1) * c, p * pw:(p + 1) * pw]

    v2 = [slab("v", i) for i in m]
    lhs = [jnp.concatenate([slab("alpha_t", i), slab("r_t", i)], axis=0).astype(BF16) for i in m]
    rhs = [jnp.concatenate([_pair_blockdiag(slab("beta_h", i), pair_mask),
                            _pair_blockdiag(slab("k_h", i), pair_mask)], axis=0) for i in m]
    big = [_dot_nt(lhs[i], rhs[i]) for i in m]
    a_ab = [jnp.where(strict2, big[i][:c, :pw], 0.0) for i in m]
    a_ak = [jnp.where(strict2, big[i][:c, pw:], 0.0) for i in m]
    a_rb = [jnp.where(incl2, big[i][c:, :pw], 0.0) for i in m]
    a_rk = [jnp.where(incl2, big[i][c:, pw:], 0.0) for i in m]
    t_inv = _inv_unit_lower(a_ab, eye, blk_mask, pair_mask)
    av = [_pair_dot(a_ak[i], v2[i], pair_mask) for i in m]
    u_const = [_pair_dot(t_inv[i], av[i], pair_mask) for i in m]
    lhs_s = [jnp.concatenate([_pair_dot(t_inv[i], slab("alpha_t", i), pair_mask).astype(BF16),
                              slab("r_t", i).astype(BF16)], axis=0) for i in m]
    a_r = [jnp.concatenate([a_rb[i], a_rk[i]], axis=1).astype(BF16) for i in m]
    bk_d = [jnp.concatenate([slab("beta_d", i), slab("k_d", i)], axis=0).astype(BF16) for i in m]
    bp = [(bi, p) for bi in bs for p in ps]
    s_cur = {(bi, p): s_ref[bi * len(ps) + p] for bi, p in bp}
    o_parts = {}
    for ci in range(nck):
        ix = {(bi, p): where[bi, ci, p] for bi, p in bp}
        sd = {q: _dot_nt(lhs_s[ix[q]], s_cur[q].astype(BF16)) for q in bp}
        u = {q: sd[q][:c] + u_const[ix[q]] for q in bp}
        uv = {q: jnp.concatenate([_pair_blockdiag(u[q], pair_mask), _pair_blockdiag(v2[ix[q]], pair_mask)], axis=0)
              for q in bp}
        for q in bp:
            o_parts[q[0], ci, q[1]] = sd[q][c:] + _dot(a_r[ix[q]], uv[q])
        uvt = {q: jnp.concatenate([u[q], v2[ix[q]]], axis=0).astype(BF16) for q in bp}
        for bi, p in bp:
            w_last = jnp.exp(fr[bi]["cw"][(ci + 1) * c - 1:(ci + 1) * c, p * pw:(p + 1) * pw])
            s_cur[bi, p] = s_cur[bi, p] * w_last + jnp.where(pair_mask, _dot_tn(uvt[bi, p], bk_d[ix[bi, p]]), 0.0)
    for bi, p in bp:
        s_ref[bi * len(ps) + p] = s_cur[bi, p]

    for bi in bs:
        o = jnp.concatenate([jnp.concatenate([o_parts[bi, ci, p] for p in ps], axis=1) for ci in range(nck)], axis=0)
        mean = _dot_x2_lhs(o, seg) * (1.0 / RW_N)
        dev = o - mean
        var = _dot_x2_lhs(dev * dev, seg) * (1.0 / RW_N)
        o = dev * lax.rsqrt(var + RW_GN_EPS) * lnw_ref[...] + lnb_ref[...]
        bonus = _dot_x2_lhs(fr[bi]["rkk"], seg) * fr[bi]["v"]
        o_ref[bi] = ((o + bonus) * fr[bi]["g"]).astype(o_ref.dtype)


def _rwkv_call(z3, mu, w0, w2, a0, a2, g2, k_k, k_a, r_k, ln_w, ln_b):
    b, t, _ = z3.shape
    c = min(RW_TB, t)
    hid = lax.broadcasted_iota(jnp.int32, (RW_W, RW_W), 0) // RW_N
    seg = (hid == hid.T).astype(BF16)

    def vec(n):
        return pl.BlockSpec((1, n), lambda i, j: (0, 0))

    def mat(m, n):
        return pl.BlockSpec((m, n), lambda i, j: (0, 0))

    nbe = RW_NB if b % RW_NB == 0 else 1
    return pl.pallas_call(
        _rwkv_kernel,
        grid=(b // nbe, t // c),
        in_specs=[
            pl.BlockSpec((nbe, c, RW_COLS), lambda i, j: (i, j, RW_OFF // RW_COLS)),
            vec(RW_COLS), vec(RW_W), mat(RW_DECAY_LORA, RW_W), vec(RW_W), mat(RW_A_LORA, RW_W),
            mat(RW_GATE_LORA, RW_W), vec(RW_W), vec(RW_W), vec(RW_W), vec(RW_W), vec(RW_W),
            mat(RW_W, RW_W),
        ],
        out_specs=pl.BlockSpec((nbe, c, RW_W), lambda i, j: (i, j, 0)),
        out_shape=jax.ShapeDtypeStruct((b, t, RW_W), BF16),
        scratch_shapes=[pltpu.VMEM((nbe * (RW_HEADS // 2), 2 * RW_N, 2 * RW_N), F32),
                        pltpu.VMEM((nbe, RW_COLS), F32)],
        compiler_params=_cparams(("parallel", "arbitrary")),
        name="rwkv7_mixer",
    )(z3, mu.reshape(1, -1), w0.reshape(1, -1), w2, a0.reshape(1, -1), a2, g2, k_k.reshape(1, -1),
      k_a.reshape(1, -1), r_k.reshape(1, -1), ln_w.reshape(1, -1), ln_b.reshape(1, -1), seg)


def _merge_kernel(ohg_ref, oret_ref, orw_ref, zg_ref, x_ref, gate_ref, bhg_ref, bret_ref, brw_ref,
                  wout_ref, o_ref):
    d = x_ref.shape[1]
    y = _sigmoid(zg_ref[:, 0:d].astype(F32)) * _dot(ohg_ref[...], bhg_ref[...])
    y = y + _sigmoid(zg_ref[:, d:2 * d].astype(F32)) * _dot(oret_ref[...], bret_ref[...])
    y = y + _sigmoid(zg_ref[:, 2 * d:3 * d].astype(F32)) * _dot(orw_ref[...], brw_ref[...])
    o_ref[...] = x_ref[...] + gate_ref[0] * _dot(y.astype(BF16), wout_ref[...])


def _merge_call(o_hg, o_ret, o_rw, z2, x2, mod3, br_hg, br_ret, br_rw, w_out, seq, gate_blk, tm=1024):
    n, d = x2.shape
    tpb = seq // tm

    def rows(w):
        return pl.BlockSpec((tm, w), lambda i: (i, 0))

    def full(m, k):
        return pl.BlockSpec((m, k), lambda i: (0, 0))

    return pl.pallas_call(
        _merge_kernel,
        grid=(n // tm,),
        in_specs=[
            rows(HG_W), rows(RET_W), rows(RW_W), rows(3 * d), rows(d),
            pl.BlockSpec((1, 1, d), lambda i: (i // tpb, 0, gate_blk)),
            full(HG_W, d), full(RET_W, d), full(RW_W, d), full(d, d),
        ],
        out_specs=rows(d),
        out_shape=jax.ShapeDtypeStruct((n, d), F32),
        compiler_params=_cparams(("parallel",)),
        name="merge_outproj",
    )(o_hg, o_ret, o_rw, z2, x2, mod3, br_hg, br_ret, br_rw, w_out)


def _pack_bf16_pairs(x):
    w = x.shape[1] // 2
    hi = pltpu.bitcast(x[:, :w].astype(BF16).astype(F32), jnp.uint32)
    lo = pltpu.bitcast(x[:, w:].astype(BF16).astype(F32), jnp.uint32)
    return pltpu.bitcast(hi | lax.shift_right_logical(lo, jnp.uint32(16)), jnp.int32)


def _unpack_bf16_pairs(p):
    u = pltpu.bitcast(p, jnp.uint32)
    hi = pltpu.bitcast(u & jnp.uint32(0xFFFF0000), F32)
    lo = pltpu.bitcast(lax.shift_left(u, jnp.uint32(16)), F32)
    return jnp.concatenate([hi, lo], axis=1)


def _route_kernel(x_ref, g_ref, scale_ref, shift_ref, rc_ref, hp_ref, eid_ref, wts_ref, cnt_ref):
    @pl.when(pl.program_id(0) == 0)
    def _():
        cnt_ref[...] = jnp.zeros_like(cnt_ref)

    h = _rms_mod(x_ref[...], g_ref[...], scale_ref[0], shift_ref[0])
    hp_ref[...] = _pack_bf16_pairs(h)
    tm = h.shape[0]
    lane = lax.broadcasted_iota(jnp.int32, (tm, LANES), 1)
    neg = -jnp.inf
    logits = _dot_x3(h, rc_ref[...])
    gl = jnp.where(lane < N_GROUPS, logits, neg)
    gmax = jnp.max(gl, axis=-1, keepdims=True)
    gidx = jnp.min(jnp.where(gl == gmax, lane, LANES), axis=-1, keepdims=True)
    gw = 1.0 / jnp.sum(jnp.exp(gl - gmax), axis=-1, keepdims=True)
    lo = N_GROUPS + gidx * EXPERTS_PER_GROUP
    el = jnp.where(lane >= lo, jnp.where(lane < lo + EXPERTS_PER_GROUP, logits, neg), neg)
    m1 = jnp.max(el, axis=-1, keepdims=True)
    l1 = jnp.min(jnp.where(el == m1, lane, LANES), axis=-1, keepdims=True)
    el2 = jnp.where(lane == l1, neg, el)
    m2 = jnp.max(el2, axis=-1, keepdims=True)
    l2 = jnp.min(jnp.where(el2 == m2, lane, LANES), axis=-1, keepdims=True)
    i1 = l1 - N_GROUPS
    i2 = l2 - N_GROUPS
    e2 = jnp.exp(m2 - m1)
    p1 = 1.0 / (1.0 + e2)
    p2 = e2 * p1
    oh1 = jnp.where(lane == i1, 1.0, 0.0)
    oh2 = jnp.where(lane == i2, 1.0, 0.0)
    row = lax.broadcasted_iota(jnp.int32, (tm, tm), 0)
    col = lax.broadcasted_iota(jnp.int32, (tm, tm), 1)
    earlier = jnp.where(row > col, 1.0, 0.0).astype(BF16)
    before = _dot(earlier, jnp.concatenate([oh1, oh2], axis=1).astype(BF16))
    tot1 = jnp.sum(oh1, axis=0, keepdims=True)
    carry = cnt_ref[...]
    r1 = jnp.sum(oh1 * (before[:, :LANES] + carry), axis=-1, keepdims=True).astype(jnp.int32)
    r2 = jnp.sum(oh2 * (before[:, LANES:] + (carry + tot1)), axis=-1, keepdims=True).astype(jnp.int32)
    cnt_ref[...] = carry + tot1 + jnp.sum(oh2, axis=0, keepdims=True)
    eid_ref[...] = jnp.where(lane == 0, i1, jnp.where(lane == 1, i2, jnp.where(lane == 2, r1,
                                                                             jnp.where(lane == 3, r2, 0))))
    wts_ref[...] = jnp.where(lane == 0, gw * p1, jnp.where(lane == 1, gw * p2, 0.0))


def _route_call(x2, gain, mod3, router_g, router_e, seq, scale_blk, shift_blk, tm=1024):
    n, d = x2.shape
    tpb = seq // tm
    rc = jnp.pad(jnp.concatenate([router_g, router_e], axis=1), ((0, 0), (0, LANES - N_GROUPS - N_EXPERTS)))
    return pl.pallas_call(
        _route_kernel,
        grid=(n // tm,),
        in_specs=[
            pl.BlockSpec((tm, d), lambda i: (i, 0)),
            pl.BlockSpec((1, d), lambda i: (0, 0)),
            pl.BlockSpec((1, 1, d), lambda i: (i // tpb, 0, scale_blk)),
            pl.BlockSpec((1, 1, d), lambda i: (i // tpb, 0, shift_blk)),
            pl.BlockSpec((d, LANES), lambda i: (0, 0)),
        ],
        out_specs=[pl.BlockSpec((tm, d // 2), lambda i: (i, 0)), pl.BlockSpec((tm, LANES), lambda i: (i, 0)),
                   pl.BlockSpec((tm, LANES), lambda i: (i, 0)), pl.BlockSpec((1, LANES), lambda i: (0, 0))],
        out_shape=[jax.ShapeDtypeStruct((n, d // 2), jnp.int32), jax.ShapeDtypeStruct((n, LANES), jnp.int32),
                   jax.ShapeDtypeStruct((n, LANES), F32), jax.ShapeDtypeStruct((1, LANES), F32)],
        compiler_params=_cparams(("arbitrary",)),
        name="moe_route",
    )(x2, gain.reshape(1, d), mod3, mod3, rc)


SC_CORES = 2
SC_SUBCORES = 16
SC_WORKERS = SC_CORES * SC_SUBCORES
SC_ROWS = 32
SC_STREAMS = 4


def _sc_gather(table, idx):
    m = idx.shape[0]
    w = table.shape[1]
    per_worker = m // SC_WORKERS
    steps = per_worker // SC_ROWS
    assert per_worker * SC_WORKERS == m and steps * SC_ROWS == per_worker and steps % SC_STREAMS == 0
    mesh = plsc.VectorSubcoreMesh(core_axis_name="c", subcore_axis_name="s")
    ks = range(SC_STREAMS)

    def body(table_hbm, idx_hbm, out_hbm, idx_v, *rest):
        bufs, g_sems, w_sems = rest[:SC_STREAMS], rest[SC_STREAMS:2 * SC_STREAMS], rest[2 * SC_STREAMS:]
        wid = lax.axis_index("s") * SC_CORES + lax.axis_index("c")
        pltpu.sync_copy(idx_hbm.at[wid], idx_v)

        @pl.loop(0, steps, step=SC_STREAMS)
        def _(j):
            row0 = wid * per_worker + j * SC_ROWS
            gathers = [pltpu.async_copy(table_hbm.at[idx_v.at[j + q]], bufs[q], g_sems[q]) for q in ks]
            writes = []
            for q in ks:
                gathers[q].wait()
                writes.append(pltpu.async_copy(bufs[q], out_hbm.at[pl.ds(row0 + q * SC_ROWS, SC_ROWS)], w_sems[q]))
            for q in ks:
                writes[q].wait()

    return pl.kernel(
        body,
        out_type=jax.ShapeDtypeStruct((m, w), table.dtype),
        mesh=mesh,
        scratch_types=[pltpu.VMEM((steps, SC_ROWS), jnp.int32)] + [pltpu.VMEM((SC_ROWS, w), table.dtype)] * SC_STREAMS
        + [pltpu.SemaphoreType.DMA] * (2 * SC_STREAMS),
        name="sc_row_gather",
    )(table, idx.reshape(SC_WORKERS, steps, SC_ROWS))


def _sc_scatter2(rows, idx0, idx1, p):
    n, w = rows.shape
    per_worker = n // SC_WORKERS
    steps = per_worker // SC_ROWS
    assert per_worker * SC_WORKERS == n and steps * SC_ROWS == per_worker and steps % SC_STREAMS == 0
    mesh = plsc.VectorSubcoreMesh(core_axis_name="c", subcore_axis_name="s")
    ks = range(SC_STREAMS)

    def body(rows_hbm, i0_hbm, i1_hbm, out_hbm, i0_v, i1_v, *rest):
        bufs, r_sems = rest[:SC_STREAMS], rest[SC_STREAMS:2 * SC_STREAMS]
        s0_sems, s1_sems = rest[2 * SC_STREAMS:3 * SC_STREAMS], rest[3 * SC_STREAMS:]
        wid = lax.axis_index("s") * SC_CORES + lax.axis_index("c")
        pltpu.sync_copy(i0_hbm.at[wid], i0_v)
        pltpu.sync_copy(i1_hbm.at[wid], i1_v)

        @pl.loop(0, steps, step=SC_STREAMS)
        def _(j):
            row0 = wid * per_worker + j * SC_ROWS
            reads = [pltpu.async_copy(rows_hbm.at[pl.ds(row0 + q * SC_ROWS, SC_ROWS)], bufs[q], r_sems[q]) for q in ks]
            writes = []
            for q in ks:
                reads[q].wait()
                writes.append(pltpu.async_copy(bufs[q], out_hbm.at[i0_v.at[j + q]], s0_sems[q]))
                writes.append(pltpu.async_copy(bufs[q], out_hbm.at[i1_v.at[j + q]], s1_sems[q]))
            for wr in writes:
                wr.wait()

    index_block = pltpu.VMEM((steps, SC_ROWS), jnp.int32)
    return pl.kernel(
        body,
        out_type=jax.ShapeDtypeStruct((p, w), rows.dtype),
        mesh=mesh,
        scratch_types=[index_block, index_block] + [pltpu.VMEM((SC_ROWS, w), rows.dtype)] * SC_STREAMS
        + [pltpu.SemaphoreType.DMA] * (3 * SC_STREAMS),
        name="sc_row_scatter",
    )(rows, idx0.reshape(SC_WORKERS, steps, SC_ROWS), idx1.reshape(SC_WORKERS, steps, SC_ROWS))


MOE_TM = 512


def _gexperts_kernel(te_ref, tv_ref, nu_ref, xs_ref, w1_ref, w3_ref, w2_ref, ys_ref, w1b_ref, w3b_ref, w2b_ref):
    i = pl.program_id(0)

    @pl.when((i == 0) | (te_ref[i] != te_ref[jnp.maximum(i - 1, 0)]))
    def _():
        w1b_ref[...] = w1_ref[0].astype(BF16)
        w3b_ref[...] = w3_ref[0].astype(BF16)
        w2b_ref[...] = w2_ref[0].astype(BF16)

    @pl.when(i < nu_ref[0])
    def _():
        rid = lax.broadcasted_iota(jnp.int32, xs_ref.shape, 0)
        xb = _unpack_bf16_pairs(jnp.where(rid < tv_ref[i], xs_ref[...], 0)).astype(BF16)
        act = (_silu(_dot(xb, w1b_ref[...])) * _dot(xb, w3b_ref[...])).astype(BF16)
        ys_ref[...] = _pack_bf16_pairs(_dot(act, w2b_ref[...]))


def _gexperts_call(xs, tile_expert, tile_valid, n_used, w1, w3, w2):
    p, half = xs.shape
    ne, d, de = w1.shape
    nt = p // MOE_TM

    def rows(i, te, tv, nu):
        return (jnp.minimum(i, nu[0] - 1), 0)

    def wsel(i, te, tv, nu):
        return (te[i], 0, 0)

    return pl.pallas_call(
        _gexperts_kernel,
        grid_spec=pltpu.PrefetchScalarGridSpec(
            num_scalar_prefetch=3,
            grid=(nt,),
            in_specs=[
                pl.BlockSpec((MOE_TM, half), rows),
                pl.BlockSpec((1, d, de), wsel),
                pl.BlockSpec((1, d, de), wsel),
                pl.BlockSpec((1, de, d), wsel),
            ],
            out_specs=pl.BlockSpec((MOE_TM, half), rows),
            scratch_shapes=[pltpu.VMEM((d, de), BF16), pltpu.VMEM((d, de), BF16), pltpu.VMEM((de, d), BF16)],
        ),
        out_shape=jax.ShapeDtypeStruct((p, half), jnp.int32),
        compiler_params=_cparams(("arbitrary",)),
        name="moe_experts",
    )(tile_expert, tile_valid, n_used, xs, w1, w3, w2)


def _combine_kernel(y0_ref, y1_ref, wts_ref, x_ref, gate_ref, fg_ref, o_ref, *, final_norm):
    wts = wts_ref[...]
    moe = wts[:, 0:1] * _unpack_bf16_pairs(y0_ref[...]) + wts[:, 1:2] * _unpack_bf16_pairs(y1_ref[...])
    xn = x_ref[...] + gate_ref[0] * moe
    if final_norm:
        xn = xn * lax.rsqrt(jnp.mean(xn * xn, axis=-1, keepdims=True) + NORM_EPS) * fg_ref[...]
    o_ref[...] = xn


def _combine_call(yg, wts, x2, mod3, final_g, seq, gate_blk, final_norm, tm=1024):
    n, d = x2.shape
    tpb = seq // tm
    slot1 = n // tm
    return pl.pallas_call(
        functools.partial(_combine_kernel, final_norm=final_norm),
        grid=(n // tm,),
        in_specs=[
            pl.BlockSpec((tm, d // 2), lambda i: (i, 0)),
            pl.BlockSpec((tm, d // 2), lambda i: (i + slot1, 0)),
            pl.BlockSpec((tm, LANES), lambda i: (i, 0)),
            pl.BlockSpec((tm, d), lambda i: (i, 0)),
            pl.BlockSpec((1, 1, d), lambda i: (i // tpb, 0, gate_blk)),
            pl.BlockSpec((1, d), lambda i: (0, 0)),
        ],
        out_specs=pl.BlockSpec((tm, d), lambda i: (i, 0)),
        out_shape=jax.ShapeDtypeStruct((n, d), F32),
        compiler_params=_cparams(("parallel",)),
        name="moe_combine",
    )(yg, yg, wts, x2, mod3, final_g.reshape(1, d))


def _pos_kernel(eid_ref, ts_ref, p0_ref, p1_ref):
    eid = eid_ref[...]
    tm = eid.shape[0]
    lane = lax.broadcasted_iota(jnp.int32, (tm, LANES), 1)
    sub = lax.broadcasted_iota(jnp.int32, (tm, LANES), 0) % LANES
    for slot, out_ref in ((0, p0_ref), (1, p1_ref)):
        first_row = jnp.sum(jnp.where(lane == eid[:, slot:slot + 1], ts_ref[...], 0), axis=-1, keepdims=True)
        pos = first_row + eid[:, slot + 2:slot + 3]
        out_ref[...] = jnp.sum(jnp.where(lane == sub, pos, 0).reshape(tm // LANES, LANES, LANES), axis=1)


def _pos_call(eid, first_rows, tm=4096):
    n = eid.shape[0]
    tm = min(tm, n)
    out = jax.ShapeDtypeStruct((n // LANES, LANES), jnp.int32)
    p0, p1 = pl.pallas_call(
        _pos_kernel,
        grid=(n // tm,),
        in_specs=[pl.BlockSpec((tm, LANES), lambda i: (i, 0)), pl.BlockSpec((1, LANES), lambda i: (0, 0))],
        out_specs=[pl.BlockSpec((tm // LANES, LANES), lambda i: (i, 0))] * 2,
        out_shape=[out, out],
        compiler_params=_cparams(("parallel",)),
        name="moe_positions",
    )(eid, first_rows)
    return p0.reshape(n), p1.reshape(n)


def _moe_plan(eid, counts_f):
    n = eid.shape[0]
    nt = (2 * n) // MOE_TM + N_EXPERTS
    counts = counts_f[0, :N_EXPERTS].astype(jnp.int32)
    tiles = (counts + MOE_TM - 1) // MOE_TM
    tile_end = jnp.cumsum(tiles)
    tile_start = tile_end - tiles
    n_used = tile_end[-1:]
    tile_iota = jnp.arange(nt, dtype=jnp.int32)
    tile_expert = jnp.sum(jnp.minimum(tile_iota, n_used - 1)[:, None] >= tile_end[None, :], axis=1, dtype=jnp.int32)
    own = tile_expert[:, None] == jnp.arange(N_EXPERTS, dtype=jnp.int32)[None, :]
    count_t = jnp.sum(jnp.where(own, counts[None, :], 0), axis=1)
    start_t = jnp.sum(jnp.where(own, tile_start[None, :], 0), axis=1)
    tile_valid = jnp.clip(count_t - (tile_iota - start_t) * MOE_TM, 0, MOE_TM)
    first_rows = jnp.pad(tile_start * MOE_TM, (0, LANES - N_EXPERTS)).reshape(1, LANES)
    pos0, pos1 = _pos_call(eid, first_rows)
    return pos0, pos1, tile_expert, tile_valid, n_used


def kernel(x, c, positions, ada_w, ada_b, norm1_g, norm2_g, w_in, hg_lb_table, hg_norm_w, rw_mu, rw_w0, rw_w2,
           rw_a0, rw_a2, rw_g2, rw_k_k, rw_k_a, rw_r_k, rw_ln_w, rw_ln_b, br_hg, br_ret, br_rw, w_out,
           router_g, router_e, moe_w1, moe_w3, moe_w2, final_g):
    b, t, d = x.shape
    depth = ada_w.shape[0]
    n = b * t
    assert w_in.shape[2] == IN_COLS and d == 1024

    lb_p = jax.nn.softmax(hg_lb_table.astype(F32), axis=0)
    lower_bounds = jnp.cumsum(lb_p, axis=0) - lb_p[0]

    mod = _mod_call(c, ada_w, ada_b)
    cos2, sin2 = _rope_call(positions, RET_DK)
    x2 = x.reshape(n, d)
    for l in range(depth):
        mod3 = mod[l].reshape(b, 1, 6 * d)
        z2 = _inproj_call(x2, norm1_g[l], mod3, _wprep_call(w_in, l), t, scale_blk=1, shift_blk=0)
        z3 = z2.reshape(b, t, IN_COLS)
        o_hg = _hgrn2_call(z3, lower_bounds[l], hg_norm_w[l])
        o_ret = _ret_call(z3, cos2, sin2)
        o_rw = _rwkv_call(z3, rw_mu[l], rw_w0[l], rw_w2[l], rw_a0[l], rw_a2[l], rw_g2[l], rw_k_k[l],
                          rw_k_a[l], rw_r_k[l], rw_ln_w[l], rw_ln_b[l])
        x2 = _merge_call(o_hg.reshape(n, HG_W), o_ret.reshape(n, RET_W), o_rw.reshape(n, RW_W), z2, x2, mod3,
                         br_hg[l].astype(BF16), br_ret[l].astype(BF16), br_rw[l].astype(BF16),
                         w_out[l].astype(BF16), t, gate_blk=2)
        hp, eid, wts, counts = _route_call(x2, norm2_g[l], mod3, router_g[l], router_e[l], t, scale_blk=4,
                                           shift_blk=3)
        pos0, pos1, tile_expert, tile_valid, n_used = _moe_plan(eid, counts)
        xs = _sc_scatter2(hp, pos0, pos1, (2 * n // MOE_TM + N_EXPERTS) * MOE_TM)
        ys = _gexperts_call(xs, tile_expert + l * N_EXPERTS, tile_valid, n_used,
                            moe_w1.reshape((-1,) + moe_w1.shape[2:]), moe_w3.reshape((-1,) + moe_w3.shape[2:]),
                            moe_w2.reshape((-1,) + moe_w2.shape[2:]))
        yg = _sc_gather(ys, jnp.concatenate([pos0, pos1]))
        x2 = _combine_call(yg, wts, x2, mod3, final_g, t, gate_blk=5, final_norm=(l == depth - 1))
    return x2.reshape(b, t, d)
```

```python
import functools

import jax
import jax.numpy as jnp
from jax import lax
from jax.experimental import pallas as pl
from jax.experimental.pallas import tpu as pltpu
from jax.experimental.pallas import tpu_sc as plsc

F32 = jnp.float32
BF16 = jnp.bfloat16
HIGHEST = lax.Precision.HIGHEST

HG_HEADS = 4
HG_DK = 128
HG_W = HG_HEADS * HG_DK
RET_HEADS = 4
RET_DK = 128
RET_W = RET_HEADS * RET_DK
RW_HEADS = 8
RW_N = 64
RW_W = RW_HEADS * RW_N
RW_DECAY_LORA = 64
RW_A_LORA = 64
RW_GATE_LORA = 128
RW_COLS = 3 * RW_W + RW_DECAY_LORA + RW_A_LORA + RW_GATE_LORA
RW_GN_EPS = 64e-5
N_GROUPS = 4
EXPERTS_PER_GROUP = 8
N_EXPERTS = N_GROUPS * EXPERTS_PER_GROUP
ROPE_THETA = 10000.0
NORM_EPS = 1e-6

LANES = 128
LOG2E = 1.4426950408889634
VMEM_LIMIT = 56 * 1024 * 1024

GATE_OFF = 0
HG_OFF = 3 * 1024
RET_OFF = HG_OFF + 4 * HG_W
RW_OFF = RET_OFF + 4 * RET_W
IN_COLS = RW_OFF + RW_COLS

HG_CHUNK = 64
HG_SUB = 16
HG_NB = 2
HG_SAFE_SPAN = 60.0
RW_CHUNK = 64
RW_BLK = 16
RW_NB = 4
RW_TB = 256
Z_DTYPE = BF16


def _cparams(sem):
    return pltpu.CompilerParams(dimension_semantics=sem, vmem_limit_bytes=VMEM_LIMIT)


def _dot(a, b, precision=None):
    return jnp.dot(a, b, preferred_element_type=F32, precision=precision)


def _dot_nt(a, b, precision=None):
    return lax.dot_general(a, b, (((1,), (1,)), ((), ())), preferred_element_type=F32, precision=precision)


def _dot_tn(a, b, precision=None):
    return lax.dot_general(a, b, (((0,), (0,)), ((), ())), preferred_element_type=F32, precision=precision)


def _split_bf16(x):
    hi = x.astype(BF16)
    return hi, (x - hi.astype(F32)).astype(BF16)


def _dot_x3(a, b):
    ah, al = _split_bf16(a)
    bh, bl = _split_bf16(b)
    return _dot(ah, bh) + _dot(ah, bl) + _dot(al, bh)


def _dot_x2_lhs(a, b_exact):
    ah, al = _split_bf16(a)
    return _dot(ah, b_exact) + _dot(al, b_exact)


def _dot_x2_rhs(a_exact, b):
    bh, bl = _split_bf16(b)
    return _dot(a_exact, bh) + _dot(a_exact, bl)


def _sigmoid(x):
    return 0.5 * jnp.tanh(0.5 * x) + 0.5


def _silu(x):
    return x * _sigmoid(x)


def _rms_mod(x, gain, scale, shift):
    y = x * lax.rsqrt(jnp.mean(x * x, axis=-1, keepdims=True) + NORM_EPS)
    return (y * gain) * (1.0 + scale) + shift


def _mod_kernel(c_ref, w_ref, b_ref, o_ref):
    c = c_ref[...]
    o_ref[0] = _dot(_silu(c), w_ref[0], HIGHEST) + b_ref[0]


def _mod_call(c, ada_w, ada_b):
    depth, d, d6 = ada_w.shape
    b = c.shape[0]
    nblk = d6 // d
    return pl.pallas_call(
        _mod_kernel,
        grid=(depth, nblk),
        in_specs=[
            pl.BlockSpec((b, d), lambda l, j: (0, 0)),
            pl.BlockSpec((1, d, d), lambda l, j: (l, 0, j)),
            pl.BlockSpec((1, 1, d), lambda l, j: (l, 0, j)),
        ],
        out_specs=pl.BlockSpec((1, b, d), lambda l, j: (l, 0, j)),
        out_shape=jax.ShapeDtypeStruct((depth, b, d6), F32),
        compiler_params=_cparams(("parallel", "parallel")),
        name="adaln_mod",
    )(c, ada_w, ada_b.reshape(depth, 1, d6))


def _rope_kernel(pos_ref, freq_ref, sign_ref, cos_ref, sin_ref):
    ang = pos_ref[0].astype(F32) * freq_ref[...]
    cos_ref[0] = jnp.cos(ang)
    sin_ref[0] = jnp.sin(ang) * sign_ref[...]


def _rope_call(positions, d):
    b, t = positions.shape
    tb = min(t, 512)
    inv_freq = ROPE_THETA ** (-jnp.arange(0, d, 2, dtype=F32) / d)
    freq2 = jnp.concatenate([inv_freq, inv_freq]).reshape(1, d)
    sign2 = jnp.concatenate([-jnp.ones((d // 2,), F32), jnp.ones((d // 2,), F32)]).reshape(1, d)
    out = jax.ShapeDtypeStruct((b, t, d), F32)
    return pl.pallas_call(
        _rope_kernel,
        grid=(b, t // tb),
        in_specs=[
            pl.BlockSpec((1, tb, 1), lambda i, j: (i, j, 0)),
            pl.BlockSpec((1, d), lambda i, j: (0, 0)),
            pl.BlockSpec((1, d), lambda i, j: (0, 0)),
        ],
        out_specs=[pl.BlockSpec((1, tb, d), lambda i, j: (i, j, 0))] * 2,
        out_shape=[out, out],
        compiler_params=_cparams(("parallel", "parallel")),
        name="rope_tables",
    )(positions.reshape(b, t, 1), freq2, sign2)


W_BLK = 256


def _wprep_kernel(w_ref, o_ref):
    o_ref[...] = w_ref[...].astype(o_ref.dtype)


def _wprep_call(w_in, layer):
    _, d, cols = w_in.shape
    nblk = cols // W_BLK
    first = (cols - 3 * d) // W_BLK
    return pl.pallas_call(
        _wprep_kernel,
        grid=(nblk,),
        in_specs=[pl.BlockSpec((1, d, W_BLK), lambda j: (layer, 0, (j + first) % nblk))],
        out_specs=pl.BlockSpec((1, d, W_BLK), lambda j: (0, 0, j)),
        out_shape=jax.ShapeDtypeStruct((1, d, cols), BF16),
        compiler_params=_cparams(("parallel",)),
        name="w_in_layout",
    )(w_in)


def _inproj_kernel(x_ref, g_ref, scale_ref, shift_ref, w_ref, o_ref, h_ref):
    @pl.when(pl.program_id(1) == 0)
    def _():
        h = _rms_mod(x_ref[...], g_ref[...], scale_ref[0], shift_ref[0])
        h_ref[...] = h.astype(BF16)

    o_ref[...] = _dot(h_ref[...], w_ref[0]).astype(o_ref.dtype)


def _inproj_call(x2, gain, mod3, w_bf16, seq, scale_blk, shift_blk, tm=2048, tn=1792):
    n, d = x2.shape
    cols = w_bf16.shape[2]
    tpb = seq // tm
    return pl.pallas_call(
        _inproj_kernel,
        grid=(n // tm, cols // tn),
        in_specs=[
            pl.BlockSpec((tm, d), lambda i, j: (i, 0)),
            pl.BlockSpec((1, d), lambda i, j: (0, 0)),
            pl.BlockSpec((1, 1, d), lambda i, j: (i // tpb, 0, scale_blk)),
            pl.BlockSpec((1, 1, d), lambda i, j: (i // tpb, 0, shift_blk)),
            pl.BlockSpec((1, d, tn), lambda i, j: (0, 0, j)),
        ],
        out_specs=pl.BlockSpec((tm, tn), lambda i, j: (i, j)),
        out_shape=jax.ShapeDtypeStruct((n, cols), Z_DTYPE),
        scratch_shapes=[pltpu.VMEM((tm, d), BF16)],
        compiler_params=_cparams(("parallel", "arbitrary")),
        name="norm_inproj",
    )(x2, gain.reshape(1, d), mod3, mod3, w_bf16)


def _hgrn2_block(zs, lbs, nw, sts, factored):
    hs = range(len(zs))
    tb = zs[0][0].shape[0]
    c, sub = HG_CHUNK, HG_SUB
    nc, ns, nb = tb // c, c // sub, tb // sub
    f = [lbs[h] + (1.0 - lbs[h]) * _sigmoid(zs[h][1]) for h in hs]
    logf = [jnp.log(jnp.maximum(f[h], 1e-30)) for h in hs]
    q = [_silu(zs[h][0]) * (HG_DK ** -0.5) for h in hs]
    k = [1.0 - f[h] for h in hs]
    v = [zs[h][2] for h in hs]
    v_b = [v[h].astype(BF16) for h in hs]
    row = lax.broadcasted_iota(jnp.int32, (tb, tb), 0)
    col = lax.broadcasted_iota(jnp.int32, (tb, tb), 1)
    tri = jnp.where(col >= (row // c) * c, jnp.where(row >= col, 1.0, 0.0), 0.0).astype(BF16)
    cum = [_dot_x2_rhs(tri, logf[h]) for h in hs]
    cum3 = [cum[h].reshape(nb, sub, HG_DK) for h in hs]
    ref3 = [cum3[h][:, 0:1, :] - logf[h].reshape(nb, sub, HG_DK)[:, 0:1, :] for h in hs]
    span = functools.reduce(jnp.maximum, [jnp.max(ref3[h] - cum3[h][:, sub - 1:sub, :]) for h in hs])
    qe = [(q[h] * jnp.exp(cum[h])).astype(BF16) for h in hs]

    offd = [(h, ci * c, ci * c + sub * i) for h in hs for ci in range(nc) for i in range(1, ns)]
    base = [cum[h][lo - 1:lo] for h, _, lo in offd]
    qt = [(q[h][lo:lo + sub] * jnp.exp(cum[h][lo:lo + sub] - base[j])).astype(BF16)
          for j, (h, _, lo) in enumerate(offd)]
    kt = [(k[h][r0:lo] * jnp.exp(base[j] - cum[h][r0:lo])).astype(BF16) for j, (h, r0, lo) in enumerate(offd)]
    a = [_dot_nt(qt[j], kt[j]).astype(BF16) for j in range(len(offd))]
    av = {(h, lo): _dot(a[j], v_b[h][r0:lo]) for j, (h, r0, lo) in enumerate(offd)}

    cs = [slice(ci * c, (ci + 1) * c) for ci in range(nc)]
    hc = [(h, ci) for h in hs for ci in range(nc)]
    last = {(h, ci): cum[h][(ci + 1) * c - 1:(ci + 1) * c] for h, ci in hc}
    kd = {(h, ci): (k[h][cs[ci]] * jnp.exp(last[h, ci] - cum[h][cs[ci]])).astype(BF16) for h, ci in hc}
    inc = {(h, ci): _dot_tn(v_b[h][cs[ci]], kd[h, ci]) for h, ci in hc}
    s_in = {(h, 0): sts[h] for h in hs}
    for ci in range(nc):
        for h in hs:
            s_in[h, ci + 1] = s_in[h, ci] * jnp.exp(last[h, ci]) + inc[h, ci]
    o_inter = {(h, ci): _dot_nt(qe[h][cs[ci]], s_in[h, ci].astype(BF16)) for h, ci in hc}

    if factored:
        qf = [(q[h] * jnp.exp(cum3[h] - ref3[h]).reshape(tb, HG_DK)).astype(BF16) for h in hs]
        kf = [(k[h] * jnp.exp(ref3[h] - cum3[h]).reshape(tb, HG_DK)).astype(BF16) for h in hs]
        rc = lax.broadcasted_iota(jnp.int32, (c, c), 0)
        cc = lax.broadcasted_iota(jnp.int32, (c, c), 1)
        keep = (rc >= cc) & (rc // sub == cc // sub)
        a_d = {(h, ci): jnp.where(keep, _dot_nt(qf[h][cs[ci]], kf[h][cs[ci]]), 0.0).astype(BF16) for h, ci in hc}
        dg = {(h, ci): _dot(a_d[h, ci], v_b[h][cs[ci]]) for h, ci in hc}
        diag = [jnp.concatenate([dg[h, ci] for ci in range(nc)], axis=0) for h in hs]
    else:
        gb = 4
        trow = lax.broadcasted_iota(jnp.int32, (gb, sub, HG_DK), 1)
        diag = []
        for h in hs:
            c2 = cum[h] * LOG2E
            ks2 = c2 - jnp.log2(k[h])
            parts = []
            for g0 in range(0, nb, gb):
                rws = slice(g0 * sub, (g0 + gb) * sub)
                c23, ks23, q3, v3 = (x[rws].reshape(gb, sub, HG_DK) for x in (c2, ks2, q[h], v[h]))
                acc = jnp.zeros((gb, sub, HG_DK), F32)
                for s in range(sub):
                    e = jnp.exp2(jnp.where(trow >= s, c23 - ks23[:, s:s + 1, :], -jnp.inf))
                    a_col = jnp.sum(q3 * e, axis=-1, keepdims=True)
                    acc = acc + a_col * v3[:, s:s + 1, :]
                parts.append(acc.reshape(gb * sub, HG_DK))
            diag.append(jnp.concatenate(parts, axis=0))

    outs = []
    for h in hs:
        pieces = []
        for ci in range(nc):
            for i in range(ns):
                lo = ci * c + sub * i
                piece = o_inter[h, ci][sub * i:sub * (i + 1)] + diag[h][lo:lo + sub]
                pieces.append(piece + av[h, lo] if i > 0 else piece)
        o = jnp.concatenate(pieces, axis=0)
        o = o * lax.rsqrt(jnp.mean(o * o, axis=-1, keepdims=True) + NORM_EPS)
        outs.append(o * nw * _silu(zs[h][3]))
    return outs, [s_in[h, nc] for h in hs], span


def _hgrn2_kernel(zq_ref, zf_ref, zi_ref, zg_ref, lb_ref, nw_ref, o_ref, st_ref):
    @pl.when(pl.program_id(1) == 0)
    def _():
        st_ref[...] = jnp.zeros_like(st_ref)

    nbe = zq_ref.shape[0]
    hs = range(HG_HEADS)
    sl = [slice(h * HG_DK, (h + 1) * HG_DK) for h in hs]
    items = [(bi, h) for bi in range(nbe) for h in hs]

    def run(factored):
        zs = [tuple(r[bi, :, sl[h]].astype(F32) for r in (zq_ref, zf_ref, zi_ref, zg_ref)) for bi, h in items]
        outs, sts, span = _hgrn2_block(zs, [lb_ref[:, sl[h]] for _, h in items], nw_ref[...],
                                       [st_ref[i] for i in range(len(items))], factored)
        return outs, sts, span

    def put(outs, sts):
        for i in range(len(items)):
            st_ref[i] = sts[i]
        for bi in range(nbe):
            o_ref[bi] = jnp.concatenate(outs[bi * HG_HEADS:(bi + 1) * HG_HEADS], axis=1).astype(o_ref.dtype)

    st_old = [st_ref[i] for i in range(len(items))]
    outs, st_new, span = run(True)
    put(outs, st_new)

    @pl.when(span > HG_SAFE_SPAN)
    def _():
        for i in range(len(items)):
            st_ref[i] = st_old[i]
        outs2, st2, _ = run(False)
        put(outs2, st2)


def _hgrn2_call(z3, lower_bound, norm_w, tb=256):
    b, t, _ = z3.shape
    tb = min(tb, t)
    base = HG_OFF // HG_W

    nbe = HG_NB if b % HG_NB == 0 else 1

    def zspec(part):
        return pl.BlockSpec((nbe, tb, HG_W), lambda i, j: (i, j, base + part))

    return pl.pallas_call(
        _hgrn2_kernel,
        grid=(b // nbe, t // tb),
        in_specs=[
            zspec(0), zspec(1), zspec(2), zspec(3),
            pl.BlockSpec((1, HG_W), lambda i, j: (0, 0)),
            pl.BlockSpec((1, LANES), lambda i, j: (0, 0)),
        ],
        out_specs=pl.BlockSpec((nbe, tb, HG_W), lambda i, j: (i, j, 0)),
        out_shape=jax.ShapeDtypeStruct((b, t, HG_W), BF16),
        scratch_shapes=[pltpu.VMEM((nbe * HG_HEADS, HG_DK, HG_DK), F32)],
        compiler_params=_cparams(("parallel", "arbitrary")),
        name="hgrn2_mixer",
    )(z3, z3, z3, z3, lower_bound.reshape(1, HG_W), norm_w.reshape(1, HG_DK))


def _ret_kernel(zq_ref, zk_ref, zv_ref, zg_ref, cos_ref, sin_ref, o_ref, st_ref, dmask_ref, *, chunk):
    hs = range(RET_HEADS)
    sl = [slice(h * RET_DK, (h + 1) * RET_DK) for h in hs]
    lg = [jnp.log(jnp.full((1, 1), 1.0 - 2.0 ** (-5.0 - h), F32)) for h in hs]

    @pl.when(pl.program_id(1) == 0)
    def _():
        st_ref[...] = jnp.zeros_like(st_ref)
        row = lax.broadcasted_iota(jnp.int32, (chunk, chunk), 0)
        col = lax.broadcasted_iota(jnp.int32, (chunk, chunk), 1)
        rel = (row - col).astype(F32)
        for h in hs:
            dmask_ref[h] = jnp.where(rel >= 0.0, jnp.exp(jnp.maximum(rel, 0.0) * lg[h]), 0.0)

    cos2 = cos_ref[0]
    sin2 = sin_ref[0]
    half = RET_DK // 2

    def rope(z):
        return z * cos2 + pltpu.roll(z, half, 1) * sin2

    tcol = lax.broadcasted_iota(jnp.int32, (chunk, 1), 0).astype(F32)
    q = [rope(zq_ref[0, :, sl[h]].astype(F32)) * (RET_DK ** -0.5) for h in hs]
    k = [rope(zk_ref[0, :, sl[h]].astype(F32)) for h in hs]
    v_b = [zv_ref[0, :, sl[h]].astype(BF16) for h in hs]
    st = [st_ref[h] for h in hs]
    scores = [(_dot_nt(q[h].astype(BF16), k[h].astype(BF16)) * dmask_ref[h]).astype(BF16) for h in hs]
    qx = [(q[h] * jnp.exp((tcol + 1.0) * lg[h])).astype(BF16) for h in hs]
    kz = [(k[h] * jnp.exp((chunk - 1.0 - tcol) * lg[h])).astype(BF16) for h in hs]
    o = [_dot(scores[h], v_b[h]) + _dot_nt(qx[h], st[h].astype(BF16)) for h in hs]
    for h in hs:
        st_ref[h] = st[h] * jnp.exp(chunk * lg[h]) + _dot_tn(v_b[h], kz[h])
    o = [o[h] * lax.rsqrt(jnp.mean(o[h] * o[h], axis=-1, keepdims=True) + NORM_EPS) for h in hs]
    o_ref[0] = (jnp.concatenate(o, axis=1) * _silu(zg_ref[0].astype(F32))).astype(o_ref.dtype)


def _ret_call(z3, cos2, sin2, chunk=256):
    b, t, _ = z3.shape
    chunk = min(chunk, t)
    base = RET_OFF // RET_W

    def zspec(part):
        return pl.BlockSpec((1, chunk, RET_W), lambda i, j: (i, j, base + part))

    tab = pl.BlockSpec((1, chunk, RET_DK), lambda i, j: (i, j, 0))
    return pl.pallas_call(
        functools.partial(_ret_kernel, chunk=chunk),
        grid=(b, t // chunk),
        in_specs=[zspec(0), zspec(1), zspec(2), zspec(3), tab, tab],
        out_specs=pl.BlockSpec((1, chunk, RET_W), lambda i, j: (i, j, 0)),
        out_shape=jax.ShapeDtypeStruct((b, t, RET_W), BF16),
        scratch_shapes=[pltpu.VMEM((RET_HEADS, RET_DK, RET_DK), F32), pltpu.VMEM((RET_HEADS, chunk, chunk), F32)],
        compiler_params=_cparams(("parallel", "arbitrary")),
        name="retention_mixer",
    )(z3, z3, z3, z3, cos2, sin2)


def _pair_blockdiag(y, pair_mask):
    return jnp.where(pair_mask, jnp.concatenate([y, y], axis=0), 0.0).astype(BF16)


def _pair_dot(x, y, pair_mask):
    return _dot(x.astype(BF16), _pair_blockdiag(y, pair_mask))


def _inv_unit_lower(a, eye, blk_mask, pair_mask):
    c = a[0].shape[0]
    m = range(len(a))
    a_bd = [jnp.where(blk_mask, a[i], 0.0) for i in m]
    a_off = [a[i] - a_bd[i] for i in m]
    a2 = [_pair_dot(a_bd[i], a_bd[i], pair_mask) for i in m]
    p = [eye + a_bd[i] for i in m]
    r = [_pair_dot(jnp.concatenate([p[i], a2[i]], axis=0), a2[i], pair_mask) for i in m]
    p = [p[i] + r[i][:c] for i in m]
    a4 = [r[i][c:] for i in m]
    r = [_pair_dot(jnp.concatenate([p[i], a4[i]], axis=0), a4[i], pair_mask) for i in m]
    p = [p[i] + r[i][:c] for i in m]
    a8 = [r[i][c:] for i in m]
    t_bd = [p[i] + _pair_dot(p[i], a8[i], pair_mask) for i in m]
    n = [_pair_dot(t_bd[i], a_off[i], pair_mask) for i in m]
    n2 = [_pair_dot(n[i], n[i], pair_mask) for i in m]
    z = [t_bd[i] + _pair_dot(n[i], t_bd[i], pair_mask) for i in m]
    return [z[i] + _pair_dot(n2[i], z[i], pair_mask) for i in m]


def _rwkv_kernel(z_ref, mu_ref, w0_ref, w2_ref, a0_ref, a2_ref, g2_ref, kk_ref, ka_ref, rk_ref,
                 lnw_ref, lnb_ref, seg_ref, o_ref, s_ref, prev_ref):
    c = RW_CHUNK
    nbe, tb = z_ref.shape[0], z_ref.shape[1]
    nck = tb // c
    bs = range(nbe)

    @pl.when(pl.program_id(1) == 0)
    def _():
        s_ref[...] = jnp.zeros_like(s_ref)
        prev_ref[...] = jnp.zeros_like(prev_ref)

    seg = seg_ref[...]
    rows = lax.broadcasted_iota(jnp.int32, (tb, 1), 0)
    rowb = lax.broadcasted_iota(jnp.int32, (tb, tb), 0)
    colb = lax.broadcasted_iota(jnp.int32, (tb, tb), 1)
    tri = jnp.where(colb >= (rowb // c) * c, jnp.where(rowb >= colb, 1.0, 0.0), 0.0).astype(BF16)

    def front(bi):
        z = z_ref[bi].astype(F32)
        z_prev = jnp.where(rows == 0, prev_ref[bi:bi + 1, :], pltpu.roll(z, 1, 0))
        prev_ref[bi:bi + 1, :] = z[tb - 1:tb]
        zs = z + mu_ref[...] * (z_prev - z)
        r = zs[:, 0:RW_W]
        k = zs[:, RW_W:2 * RW_W]
        v = zs[:, 2 * RW_W:3 * RW_W]
        off = 3 * RW_W
        w_lo = zs[:, off:off + RW_DECAY_LORA]
        a_lo = zs[:, off + RW_DECAY_LORA:off + RW_DECAY_LORA + RW_A_LORA]
        g_lo = zs[:, off + RW_DECAY_LORA + RW_A_LORA:]
        wx = -(w0_ref[...] + _dot_x3(jnp.tanh(w_lo), w2_ref[...]))
        softplus = jnp.maximum(wx, 0.0) + jnp.log(1.0 + jnp.exp(-jnp.abs(wx)))
        logw = -jnp.exp(-softplus - 0.5)
        a = _sigmoid(a0_ref[...] + _dot_x3(a_lo, a2_ref[...]))
        g = _dot_x3(_sigmoid(g_lo), g2_ref[...])
        kk = k * kk_ref[...]
        kk = kk * lax.rsqrt(jnp.maximum(_dot_x2_lhs(kk * kk, seg), 1e-24))
        k2 = k * (1.0 + (a - 1.0) * ka_ref[...])
        cw = _dot_x2_rhs(tri, logw)
        w_inv = jnp.exp(-cw)
        last = jnp.concatenate([jnp.broadcast_to(cw[(ci + 1) * c - 1:(ci + 1) * c], (c, RW_W)) for ci in range(nck)],
                               axis=0)
        w_rest = jnp.exp(last - cw)
        beta = a * kk
        return dict(alpha_t=-kk * jnp.exp(cw - logw), r_t=r * jnp.exp(cw), beta_h=beta * w_inv, k_h=k2 * w_inv,
                    beta_d=beta * w_rest, k_d=k2 * w_rest, v=v, g=g, rkk=r * k2 * rk_ref[...], cw=cw)

    fr = [front(bi) for bi in bs]

    pw = 2 * RW_N
    row2 = lax.broadcasted_iota(jnp.int32, (c, pw), 0)
    col2 = lax.broadcasted_iota(jnp.int32, (c, pw), 1) % c
    incl2 = row2 >= col2
    strict2 = row2 > col2
    blk_mask = (row2 // RW_BLK) == (col2 // RW_BLK)
    eye = (row2 == col2).astype(F32)
    rowp = lax.broadcasted_iota(jnp.int32, (pw, pw), 0)
    colp = lax.broadcasted_iota(jnp.int32, (pw, pw), 1)
    pair_mask = (rowp // RW_N) == (colp // RW_N)

    ps = range(RW_HEADS // 2)
    items = [(bi, ci, p) for bi in bs for ci in range(nck) for p in ps]
    where = {key: i for i, key in enumerate(items)}
    m = range(len(items))

    def slab(name, i):
        bi, ci, p = items[i]
        return fr[bi][name][ci * c:(ci + 1) * c, p * pw:(p + 1) * pw]

    v2 = [slab("v", i) for i in m]
    lhs = [jnp.concatenate([slab("alpha_t", i), slab("r_t", i)], axis=0).astype(BF16) for i in m]
    rhs = [jnp.concatenate([_pair_blockdiag(slab("beta_h", i), pair_mask),
                            _pair_blockdiag(slab("k_h", i), pair_mask)], axis=0) for i in m]
    big = [_dot_nt(lhs[i], rhs[i]) for i in m]
    a_ab = [jnp.where(strict2, big[i][:c, :pw], 0.0) for i in m]
    a_ak = [jnp.where(strict2, big[i][:c, pw:], 0.0) for i in m]
    a_rb = [jnp.where(incl2, big[i][c:, :pw], 0.0) for i in m]
    a_rk = [jnp.where(incl2, big[i][c:, pw:], 0.0) for i in m]
    t_inv = _inv_unit_lower(a_ab, eye, blk_mask, pair_mask)
    av = [_pair_dot(a_ak[i], v2[i], pair_mask) for i in m]
    u_const = [_pair_dot(t_inv[i], av[i], pair_mask) for i in m]
    lhs_s = [jnp.concatenate([_pair_dot(t_inv[i], slab("alpha_t", i), pair_mask).astype(BF16),
                              slab("r_t", i).astype(BF16)], axis=0) for i in m]
    a_r = [jnp.concatenate([a_rb[i], a_rk[i]], axis=1).astype(BF16) for i in m]
    bk_d = [jnp.concatenate([slab("beta_d", i), slab("k_d", i)], axis=0).astype(BF16) for i in m]
    bp = [(bi, p) for bi in bs for p in ps]
    s_cur = {(bi, p): s_ref[bi * len(ps) + p] for bi, p in bp}
    o_parts = {}
    for ci in range(nck):
        ix = {(bi, p): where[bi, ci, p] for bi, p in bp}
        sd = {q: _dot_nt(lhs_s[ix[q]], s_cur[q].astype(BF16)) for q in bp}
        u = {q: sd[q][:c] + u_const[ix[q]] for q in bp}
        uv = {q: jnp.concatenate([_pair_blockdiag(u[q], pair_mask), _pair_blockdiag(v2[ix[q]], pair_mask)], axis=0)
              for q in bp}
        for q in bp:
            o_parts[q[0], ci, q[1]] = sd[q][c:] + _dot(a_r[ix[q]], uv[q])
        uvt = {q: jnp.concatenate([u[q], v2[ix[q]]], axis=0).astype(BF16) for q in bp}
        for bi, p in bp:
            w_last = jnp.exp(fr[bi]["cw"][(ci + 1) * c - 1:(ci + 1) * c, p * pw:(p + 1) * pw])
            s_cur[bi, p] = s_cur[bi, p] * w_last + jnp.where(pair_mask, _dot_tn(uvt[bi, p], bk_d[ix[bi, p]]), 0.0)
    for bi, p in bp:
        s_ref[bi * len(ps) + p] = s_cur[bi, p]

    for bi in bs:
        o = jnp.concatenate([jnp.concatenate([o_parts[bi, ci, p] for p in ps], axis=1) for ci in range(nck)], axis=0)
        mean = _dot_x2_lhs(o, seg) * (1.0 / RW_N)
        dev = o - mean
        var = _dot_x2_lhs(dev * dev, seg) * (1.0 / RW_N)
        o = dev * lax.rsqrt(var + RW_GN_EPS) * lnw_ref[...] + lnb_ref[...]
        bonus = _dot_x2_lhs(fr[bi]["rkk"], seg) * fr[bi]["v"]
        o_ref[bi] = ((o + bonus) * fr[bi]["g"]).astype(o_ref.dtype)


def _rwkv_call(z3, mu, w0, w2, a0, a2, g2, k_k, k_a, r_k, ln_w, ln_b):
    b, t, _ = z3.shape
    c = min(RW_TB, t)
    hid = lax.broadcasted_iota(jnp.int32, (RW_W, RW_W), 0) // RW_N
    seg = (hid == hid.T).astype(BF16)

    def vec(n):
        return pl.BlockSpec((1, n), lambda i, j: (0, 0))

    def mat(m, n):
        return pl.BlockSpec((m, n), lambda i, j: (0, 0))

    nbe = RW_NB if b % RW_NB == 0 else 1
    return pl.pallas_call(
        _rwkv_kernel,
        grid=(b // nbe, t // c),
        in_specs=[
            pl.BlockSpec((nbe, c, RW_COLS), lambda i, j: (i, j, RW_OFF // RW_COLS)),
            vec(RW_COLS), vec(RW_W), mat(RW_DECAY_LORA, RW_W), vec(RW_W), mat(RW_A_LORA, RW_W),
            mat(RW_GATE_LORA, RW_W), vec(RW_W), vec(RW_W), vec(RW_W), vec(RW_W), vec(RW_W),
            mat(RW_W, RW_W),
        ],
        out_specs=pl.BlockSpec((nbe, c, RW_W), lambda i, j: (i, j, 0)),
        out_shape=jax.ShapeDtypeStruct((b, t, RW_W), BF16),
        scratch_shapes=[pltpu.VMEM((nbe * (RW_HEADS // 2), 2 * RW_N, 2 * RW_N), F32),
                        pltpu.VMEM((nbe, RW_COLS), F32)],
        compiler_params=_cparams(("parallel", "arbitrary")),
        name="rwkv7_mixer",
    )(z3, mu.reshape(1, -1), w0.reshape(1, -1), w2, a0.reshape(1, -1), a2, g2, k_k.reshape(1, -1),
      k_a.reshape(1, -1), r_k.reshape(1, -1), ln_w.reshape(1, -1), ln_b.reshape(1, -1), seg)


def _merge_kernel(ohg_ref, oret_ref, orw_ref, zg_ref, x_ref, gate_ref, bhg_ref, bret_ref, brw_ref,
                  wout_ref, o_ref):
    d = x_ref.shape[1]
    y = _sigmoid(zg_ref[:, 0:d].astype(F32)) * _dot(ohg_ref[...], bhg_ref[...])
    y = y + _sigmoid(zg_ref[:, d:2 * d].astype(F32)) * _dot(oret_ref[...], bret_ref[...])
    y = y + _sigmoid(zg_ref[:, 2 * d:3 * d].astype(F32)) * _dot(orw_ref[...], brw_ref[...])
    o_ref[...] = x_ref[...] + gate_ref[0] * _dot(y.astype(BF16), wout_ref[...])


def _merge_call(o_hg, o_ret, o_rw, z2, x2, mod3, br_hg, br_ret, br_rw, w_out, seq, gate_blk, tm=1024):
    n, d = x2.shape
    tpb = seq // tm

    def rows(w):
        return pl.BlockSpec((tm, w), lambda i: (i, 0))

    def full(m, k):
        return pl.BlockSpec((m, k), lambda i: (0, 0))

    return pl.pallas_call(
        _merge_kernel,
        grid=(n // tm,),
        in_specs=[
            rows(HG_W), rows(RET_W), rows(RW_W), rows(3 * d), rows(d),
            pl.BlockSpec((1, 1, d), lambda i: (i // tpb, 0, gate_blk)),
            full(HG_W, d), full(RET_W, d), full(RW_W, d), full(d, d),
        ],
        out_specs=rows(d),
        out_shape=jax.ShapeDtypeStruct((n, d), F32),
        compiler_params=_cparams(("parallel",)),
        name="merge_outproj",
    )(o_hg, o_ret, o_rw, z2, x2, mod3, br_hg, br_ret, br_rw, w_out)


def _pack_bf16_pairs(x):
    w = x.shape[1] // 2
    hi = pltpu.bitcast(x[:, :w].astype(BF16).astype(F32), jnp.uint32)
    lo = pltpu.bitcast(x[:, w:].astype(BF16).astype(F32), jnp.uint32)
    return pltpu.bitcast(hi | lax.shift_right_logical(lo, jnp.uint32(16)), jnp.int32)


def _unpack_bf16_pairs(p):
    u = pltpu.bitcast(p, jnp.uint32)
    hi = pltpu.bitcast(u & jnp.uint32(0xFFFF0000), F32)
    lo = pltpu.bitcast(lax.shift_left(u, jnp.uint32(16)), F32)
    return jnp.concatenate([hi, lo], axis=1)


def _route_kernel(x_ref, g_ref, scale_ref, shift_ref, rc_ref, hp_ref, eid_ref, wts_ref, cnt_ref):
    @pl.when(pl.program_id(0) == 0)
    def _():
        cnt_ref[...] = jnp.zeros_like(cnt_ref)

    h = _rms_mod(x_ref[...], g_ref[...], scale_ref[0], shift_ref[0])
    hp_ref[...] = _pack_bf16_pairs(h)
    tm = h.shape[0]
    lane = lax.broadcasted_iota(jnp.int32, (tm, LANES), 1)
    neg = -jnp.inf
    logits = _dot_x3(h, rc_ref[...])
    gl = jnp.where(lane < N_GROUPS, logits, neg)
    gmax = jnp.max(gl, axis=-1, keepdims=True)
    gidx = jnp.min(jnp.where(gl == gmax, lane, LANES), axis=-1, keepdims=True)
    gw = 1.0 / jnp.sum(jnp.exp(gl - gmax), axis=-1, keepdims=True)
    lo = N_GROUPS + gidx * EXPERTS_PER_GROUP
    el = jnp.where(lane >= lo, jnp.where(lane < lo + EXPERTS_PER_GROUP, logits, neg), neg)
    m1 = jnp.max(el, axis=-1, keepdims=True)
    l1 = jnp.min(jnp.where(el == m1, lane, LANES), axis=-1, keepdims=True)
    el2 = jnp.where(lane == l1, neg, el)
    m2 = jnp.max(el2, axis=-1, keepdims=True)
    l2 = jnp.min(jnp.where(el2 == m2, lane, LANES), axis=-1, keepdims=True)
    i1 = l1 - N_GROUPS
    i2 = l2 - N_GROUPS
    e2 = jnp.exp(m2 - m1)
    p1 = 1.0 / (1.0 + e2)
    p2 = e2 * p1
    oh1 = jnp.where(lane == i1, 1.0, 0.0)
    oh2 = jnp.where(lane == i2, 1.0, 0.0)
    row = lax.broadcasted_iota(jnp.int32, (tm, tm), 0)
    col = lax.broadcasted_iota(jnp.int32, (tm, tm), 1)
    earlier = jnp.where(row > col, 1.0, 0.0).astype(BF16)
    before = _dot(earlier, jnp.concatenate([oh1, oh2], axis=1).astype(BF16))
    tot1 = jnp.sum(oh1, axis=0, keepdims=True)
    carry = cnt_ref[...]
    r1 = jnp.sum(oh1 * (before[:, :LANES] + carry), axis=-1, keepdims=True).astype(jnp.int32)
    r2 = jnp.sum(oh2 * (before[:, LANES:] + (carry + tot1)), axis=-1, keepdims=True).astype(jnp.int32)
    cnt_ref[...] = carry + tot1 + jnp.sum(oh2, axis=0, keepdims=True)
    eid_ref[...] = jnp.where(lane == 0, i1, jnp.where(lane == 1, i2, jnp.where(lane == 2, r1,
                                                                             jnp.where(lane == 3, r2, 0))))
    wts_ref[...] = jnp.where(lane == 0, gw * p1, jnp.where(lane == 1, gw * p2, 0.0))


def _route_call(x2, gain, mod3, router_g, router_e, seq, scale_blk, shift_blk, tm=1024):
    n, d = x2.shape
    tpb = seq // tm
    rc = jnp.pad(jnp.concatenate([router_g, router_e], axis=1), ((0, 0), (0, LANES - N_GROUPS - N_EXPERTS)))
    return pl.pallas_call(
        _route_kernel,
        grid=(n // tm,),
        in_specs=[
            pl.BlockSpec((tm, d), lambda i: (i, 0)),
            pl.BlockSpec((1, d), lambda i: (0, 0)),
            pl.BlockSpec((1, 1, d), lambda i: (i // tpb, 0, scale_blk)),
            pl.BlockSpec((1, 1, d), lambda i: (i // tpb, 0, shift_blk)),
            pl.BlockSpec((d, LANES), lambda i: (0, 0)),
        ],
        out_specs=[pl.BlockSpec((tm, d // 2), lambda i: (i, 0)), pl.BlockSpec((tm, LANES), lambda i: (i, 0)),
                   pl.BlockSpec((tm, LANES), lambda i: (i, 0)), pl.BlockSpec((1, LANES), lambda i: (0, 0))],
        out_shape=[jax.ShapeDtypeStruct((n, d // 2), jnp.int32), jax.ShapeDtypeStruct((n, LANES), jnp.int32),
                   jax.ShapeDtypeStruct((n, LANES), F32), jax.ShapeDtypeStruct((1, LANES), F32)],
        compiler_params=_cparams(("arbitrary",)),
        name="moe_route",
    )(x2, gain.reshape(1, d), mod3, mod3, rc)


SC_CORES = 2
SC_SUBCORES = 16
SC_WORKERS = SC_CORES * SC_SUBCORES
SC_ROWS = 32
SC_STREAMS = 4


def _sc_gather(table, idx):
    m = idx.shape[0]
    w = table.shape[1]
    per_worker = m // SC_WORKERS
    steps = per_worker // SC_ROWS
    assert per_worker * SC_WORKERS == m and steps * SC_ROWS == per_worker and steps % SC_STREAMS == 0
    mesh = plsc.VectorSubcoreMesh(core_axis_name="c", subcore_axis_name="s")
    ks = range(SC_STREAMS)

    def body(table_hbm, idx_hbm, out_hbm, idx_v, *rest):
        bufs, g_sems, w_sems = rest[:SC_STREAMS], rest[SC_STREAMS:2 * SC_STREAMS], rest[2 * SC_STREAMS:]
        wid = lax.axis_index("s") * SC_CORES + lax.axis_index("c")
        pltpu.sync_copy(idx_hbm.at[wid], idx_v)

        @pl.loop(0, steps, step=SC_STREAMS)
        def _(j):
            row0 = wid * per_worker + j * SC_ROWS
            gathers = [pltpu.async_copy(table_hbm.at[idx_v.at[j + q]], bufs[q], g_sems[q]) for q in ks]
            writes = []
            for q in ks:
                gathers[q].wait()
                writes.append(pltpu.async_copy(bufs[q], out_hbm.at[pl.ds(row0 + q * SC_ROWS, SC_ROWS)], w_sems[q]))
            for q in ks:
                writes[q].wait()

    return pl.kernel(
        body,
        out_type=jax.ShapeDtypeStruct((m, w), table.dtype),
        mesh=mesh,
        scratch_types=[pltpu.VMEM((steps, SC_ROWS), jnp.int32)] + [pltpu.VMEM((SC_ROWS, w), table.dtype)] * SC_STREAMS
        + [pltpu.SemaphoreType.DMA] * (2 * SC_STREAMS),
        name="sc_row_gather",
    )(table, idx.reshape(SC_WORKERS, steps, SC_ROWS))


def _sc_scatter2(rows, idx0, idx1, p):
    n, w = rows.shape
    per_worker = n // SC_WORKERS
    steps = per_worker // SC_ROWS
    assert per_worker * SC_WORKERS == n and steps * SC_ROWS == per_worker and steps % SC_STREAMS == 0
    mesh = plsc.VectorSubcoreMesh(core_axis_name="c", subcore_axis_name="s")
    ks = range(SC_STREAMS)

    def body(rows_hbm, i0_hbm, i1_hbm, out_hbm, i0_v, i1_v, *rest):
        bufs, r_sems = rest[:SC_STREAMS], rest[SC_STREAMS:2 * SC_STREAMS]
        s0_sems, s1_sems = rest[2 * SC_STREAMS:3 * SC_STREAMS], rest[3 * SC_STREAMS:]
        wid = lax.axis_index("s") * SC_CORES + lax.axis_index("c")
        pltpu.sync_copy(i0_hbm.at[wid], i0_v)
        pltpu.sync_copy(i1_hbm.at[wid], i1_v)

        @pl.loop(0, steps, step=SC_STREAMS)
        def _(j):
            row0 = wid * per_worker + j * SC_ROWS
            reads = [pltpu.async_copy(rows_hbm.at[pl.ds(row0 + q * SC_ROWS, SC_ROWS)], bufs[q], r_sems[q]) for q in ks]
            writes = []
            for q in ks:
                reads[q].wait()
                writes.append(pltpu.async_copy(bufs[q], out_hbm.at[i0_v.at[j + q]], s0_sems[q]))
                writes.append(pltpu.async_copy(bufs[q], out_hbm.at[i1_v.at[j + q]], s1_sems[q]))
            for wr in writes:
                wr.wait()

    index_block = pltpu.VMEM((steps, SC_ROWS), jnp.int32)
    return pl.kernel(
        body,
        out_type=jax.ShapeDtypeStruct((p, w), rows.dtype),
        mesh=mesh,
        scratch_types=[index_block, index_block] + [pltpu.VMEM((SC_ROWS, w), rows.dtype)] * SC_STREAMS
        + [pltpu.SemaphoreType.DMA] * (3 * SC_STREAMS),
        name="sc_row_scatter",
    )(rows, idx0.reshape(SC_WORKERS, steps, SC_ROWS), idx1.reshape(SC_WORKERS, steps, SC_ROWS))


MOE_TM = 512


def _gexperts_kernel(te_ref, tv_ref, nu_ref, xs_ref, w1_ref, w3_ref, w2_ref, ys_ref, w1b_ref, w3b_ref, w2b_ref):
    i = pl.program_id(0)

    @pl.when((i == 0) | (te_ref[i] != te_ref[jnp.maximum(i - 1, 0)]))
    def _():
        w1b_ref[...] = w1_ref[0].astype(BF16)
        w3b_ref[...] = w3_ref[0].astype(BF16)
        w2b_ref[...] = w2_ref[0].astype(BF16)

    @pl.when(i < nu_ref[0])
    def _():
        rid = lax.broadcasted_iota(jnp.int32, xs_ref.shape, 0)
        xb = _unpack_bf16_pairs(jnp.where(rid < tv_ref[i], xs_ref[...], 0)).astype(BF16)
        act = (_silu(_dot(xb, w1b_ref[...])) * _dot(xb, w3b_ref[...])).astype(BF16)
        ys_ref[...] = _pack_bf16_pairs(_dot(act, w2b_ref[...]))


def _gexperts_call(xs, tile_expert, tile_valid, n_used, w1, w3, w2):
    p, half = xs.shape
    ne, d, de = w1.shape
    nt = p // MOE_TM

    def rows(i, te, tv, nu):
        return (jnp.minimum(i, nu[0] - 1), 0)

    def wsel(i, te, tv, nu):
        return (te[i], 0, 0)

    return pl.pallas_call(
        _gexperts_kernel,
        grid_spec=pltpu.PrefetchScalarGridSpec(
            num_scalar_prefetch=3,
            grid=(nt,),
            in_specs=[
                pl.BlockSpec((MOE_TM, half), rows),
                pl.BlockSpec((1, d, de), wsel),
                pl.BlockSpec((1, d, de), wsel),
                pl.BlockSpec((1, de, d), wsel),
            ],
            out_specs=pl.BlockSpec((MOE_TM, half), rows),
            scratch_shapes=[pltpu.VMEM((d, de), BF16), pltpu.VMEM((d, de), BF16), pltpu.VMEM((de, d), BF16)],
        ),
        out_shape=jax.ShapeDtypeStruct((p, half), jnp.int32),
        compiler_params=_cparams(("arbitrary",)),
        name="moe_experts",
    )(tile_expert, tile_valid, n_used, xs, w1, w3, w2)


def _combine_kernel(y0_ref, y1_ref, wts_ref, x_ref, gate_ref, fg_ref, o_ref, *, final_norm):
    wts = wts_ref[...]
    moe = wts[:, 0:1] * _unpack_bf16_pairs(y0_ref[...]) + wts[:, 1:2] * _unpack_bf16_pairs(y1_ref[...])
    xn = x_ref[...] + gate_ref[0] * moe
    if final_norm:
        xn = xn * lax.rsqrt(jnp.mean(xn * xn, axis=-1, keepdims=True) + NORM_EPS) * fg_ref[...]
    o_ref[...] = xn


def _combine_call(yg, wts, x2, mod3, final_g, seq, gate_blk, final_norm, tm=1024):
    n, d = x2.shape
    tpb = seq // tm
    slot1 = n // tm
    return pl.pallas_call(
        functools.partial(_combine_kernel, final_norm=final_norm),
        grid=(n // tm,),
        in_specs=[
            pl.BlockSpec((tm, d // 2), lambda i: (i, 0)),
            pl.BlockSpec((tm, d // 2), lambda i: (i + slot1, 0)),
            pl.BlockSpec((tm, LANES), lambda i: (i, 0)),
            pl.BlockSpec((tm, d), lambda i: (i, 0)),
            pl.BlockSpec((1, 1, d), lambda i: (i // tpb, 0, gate_blk)),
            pl.BlockSpec((1, d), lambda i: (0, 0)),
        ],
        out_specs=pl.BlockSpec((tm, d), lambda i: (i, 0)),
        out_shape=jax.ShapeDtypeStruct((n, d), F32),
        compiler_params=_cparams(("parallel",)),
        name="moe_combine",
    )(yg, yg, wts, x2, mod3, final_g.reshape(1, d))


def _pos_kernel(eid_ref, ts_ref, p0_ref, p1_ref):
    eid = eid_ref[...]
    tm = eid.shape[0]
    lane = lax.broadcasted_iota(jnp.int32, (tm, LANES), 1)
    sub = lax.broadcasted_iota(jnp.int32, (tm, LANES), 0) % LANES
    for slot, out_ref in ((0, p0_ref), (1, p1_ref)):
        first_row = jnp.sum(jnp.where(lane == eid[:, slot:slot + 1], ts_ref[...], 0), axis=-1, keepdims=True)
        pos = first_row + eid[:, slot + 2:slot + 3]
        out_ref[...] = jnp.sum(jnp.where(lane == sub, pos, 0).reshape(tm // LANES, LANES, LANES), axis=1)


def _pos_call(eid, first_rows, tm=4096):
    n = eid.shape[0]
    tm = min(tm, n)
    out = jax.ShapeDtypeStruct((n // LANES, LANES), jnp.int32)
    p0, p1 = pl.pallas_call(
        _pos_kernel,
        grid=(n // tm,),
        in_specs=[pl.BlockSpec((tm, LANES), lambda i: (i, 0)), pl.BlockSpec((1, LANES), lambda i: (0, 0))],
        out_specs=[pl.BlockSpec((tm // LANES, LANES), lambda i: (i, 0))] * 2,
        out_shape=[out, out],
        compiler_params=_cparams(("parallel",)),
        name="moe_positions",
    )(eid, first_rows)
    return p0.reshape(n), p1.reshape(n)


def _moe_plan(eid, counts_f):
    n = eid.shape[0]
    nt = (2 * n) // MOE_TM + N_EXPERTS
    counts = counts_f[0, :N_EXPERTS].astype(jnp.int32)
    tiles = (counts + MOE_TM - 1) // MOE_TM
    tile_end = jnp.cumsum(tiles)
    tile_start = tile_end - tiles
    n_used = tile_end[-1:]
    tile_iota = jnp.arange(nt, dtype=jnp.int32)
    tile_expert = jnp.sum(jnp.minimum(tile_iota, n_used - 1)[:, None] >= tile_end[None, :], axis=1, dtype=jnp.int32)
    own = tile_expert[:, None] == jnp.arange(N_EXPERTS, dtype=jnp.int32)[None, :]
    count_t = jnp.sum(jnp.where(own, counts[None, :], 0), axis=1)
    start_t = jnp.sum(jnp.where(own, tile_start[None, :], 0), axis=1)
    tile_valid = jnp.clip(count_t - (tile_iota - start_t) * MOE_TM, 0, MOE_TM)
    first_rows = jnp.pad(tile_start * MOE_TM, (0, LANES - N_EXPERTS)).reshape(1, LANES)
    pos0, pos1 = _pos_call(eid, first_rows)
    return pos0, pos1, tile_expert, tile_valid, n_used


def kernel(x, c, positions, ada_w, ada_b, norm1_g, norm2_g, w_in, hg_lb_table, hg_norm_w, rw_mu, rw_w0, rw_w2,
           rw_a0, rw_a2, rw_g2, rw_k_k, rw_k_a, rw_r_k, rw_ln_w, rw_ln_b, br_hg, br_ret, br_rw, w_out,
           router_g, router_e, moe_w1, moe_w3, moe_w2, final_g):
    b, t, d = x.shape
    depth = ada_w.shape[0]
    n = b * t
    assert w_in.shape[2] == IN_COLS and d == 1024

    lb_p = jax.nn.softmax(hg_lb_table.astype(F32), axis=0)
    lower_bounds = jnp.cumsum(lb_p, axis=0) - lb_p[0]

    mod = _mod_call(c, ada_w, ada_b)
    cos2, sin2 = _rope_call(positions, RET_DK)
    x2 = x.reshape(n, d)
    for l in range(depth):
        mod3 = mod[l].reshape(b, 1, 6 * d)
        z2 = _inproj_call(x2, norm1_g[l], mod3, _wprep_call(w_in, l), t, scale_blk=1, shift_blk=0)
        z3 = z2.reshape(b, t, IN_COLS)
        o_hg = _hgrn2_call(z3, lower_bounds[l], hg_norm_w[l])
        o_ret = _ret_call(z3, cos2, sin2)
        o_rw = _rwkv_call(z3, rw_mu[l], rw_w0[l], rw_w2[l], rw_a0[l], rw_a2[l], rw_g2[l], rw_k_k[l],
                          rw_k_a[l], rw_r_k[l], rw_ln_w[l], rw_ln_b[l])
        x2 = _merge_call(o_hg.reshape(n, HG_W), o_ret.reshape(n, RET_W), o_rw.reshape(n, RW_W), z2, x2, mod3,
                         br_hg[l].astype(BF16), br_ret[l].astype(BF16), br_rw[l].astype(BF16),
                         w_out[l].astype(BF16), t, gate_blk=2)
        hp, eid, wts, counts = _route_call(x2, norm2_g[l], mod3, router_g[l], router_e[l], t, scale_blk=4,
                                           shift_blk=3)
        pos0, pos1, tile_expert, tile_valid, n_used = _moe_plan(eid, counts)
        xs = _sc_scatter2(hp, pos0, pos1, (2 * n // MOE_TM + N_EXPERTS) * MOE_TM)
        ys = _gexperts_call(xs, tile_expert + l * N_EXPERTS, tile_valid, n_used,
                            moe_w1.reshape((-1,) + moe_w1.shape[2:]), moe_w3.reshape((-1,) + moe_w3.shape[2:]),
                            moe_w2.reshape((-1,) + moe_w2.shape[2:]))
        yg = _sc_gather(ys, jnp.concatenate([pos0, pos1]))
        x2 = _combine_call(yg, wts, x2, mod3, final_g, t, gate_blk=5, final_norm=(l == depth - 1))
    return x2.reshape(b, t, d)
```

```python
import functools

import jax
import jax.numpy as jnp
from jax import lax
from jax.experimental import pallas as pl
from jax.experimental.pallas import tpu as pltpu
from jax.experimental.pallas import tpu_sc as plsc

F32 = jnp.float32
BF16 = jnp.bfloat16
HIGHEST = lax.Precision.HIGHEST

HG_HEADS = 4
HG_DK = 128
HG_W = HG_HEADS * HG_DK
RET_HEADS = 4
RET_DK = 128
RET_W = RET_HEADS * RET_DK
RW_HEADS = 8
RW_N = 64
RW_W = RW_HEADS * RW_N
RW_DECAY_LORA = 64
RW_A_LORA = 64
RW_GATE_LORA = 128
RW_COLS = 3 * RW_W + RW_DECAY_LORA + RW_A_LORA + RW_GATE_LORA
RW_GN_EPS = 64e-5
N_GROUPS = 4
EXPERTS_PER_GROUP = 8
N_EXPERTS = N_GROUPS * EXPERTS_PER_GROUP
ROPE_THETA = 10000.0
NORM_EPS = 1e-6

LANES = 128
LOG2E = 1.4426950408889634
VMEM_LIMIT = 56 * 1024 * 1024

GATE_OFF = 0
HG_OFF = 3 * 1024
RET_OFF = HG_OFF + 4 * HG_W
RW_OFF = RET_OFF + 4 * RET_W
IN_COLS = RW_OFF + RW_COLS

HG_CHUNK = 64
HG_SUB = 16
HG_NB = 2
HG_SAFE_SPAN = 60.0
RW_CHUNK = 64
RW_BLK = 16
RW_NB = 4
RW_TB = 256
Z_DTYPE = BF16


def _cparams(sem):
    return pltpu.CompilerParams(dimension_semantics=sem, vmem_limit_bytes=VMEM_LIMIT)


def _dot(a, b, precision=None):
    return jnp.dot(a, b, preferred_element_type=F32, precision=precision)


def _dot_nt(a, b, precision=None):
    return lax.dot_general(a, b, (((1,), (1,)), ((), ())), preferred_element_type=F32, precision=precision)


def _dot_tn(a, b, precision=None):
    return lax.dot_general(a, b, (((0,), (0,)), ((), ())), preferred_element_type=F32, precision=precision)


def _split_bf16(x):
    hi = x.astype(BF16)
    return hi, (x - hi.astype(F32)).astype(BF16)


def _dot_x3(a, b):
    ah, al = _split_bf16(a)
    bh, bl = _split_bf16(b)
    return _dot(ah, bh) + _dot(ah, bl) + _dot(al, bh)


def _seg_sum(x, seg):
    return _dot(x.astype(BF16), seg)


def _dot_x2_rhs(a_exact, b):
    bh, bl = _split_bf16(b)
    return _dot(a_exact, bh) + _dot(a_exact, bl)


def _sigmoid(x):
    return 0.5 * jnp.tanh(0.5 * x) + 0.5


def _silu(x):
    return x * _sigmoid(x)


def _rms_mod(x, gain, scale, shift):
    y = x * lax.rsqrt(jnp.mean(x * x, axis=-1, keepdims=True) + NORM_EPS)
    return (y * gain) * (1.0 + scale) + shift


def _mod_kernel(c_ref, w_ref, b_ref, o_ref):
    c = c_ref[...]
    o_ref[0] = _dot(_silu(c), w_ref[0], HIGHEST) + b_ref[0]


def _mod_call(c, ada_w, ada_b):
    depth, d, d6 = ada_w.shape
    b = c.shape[0]
    nblk = d6 // d
    return pl.pallas_call(
        _mod_kernel,
        grid=(depth, nblk),
        in_specs=[
            pl.BlockSpec((b, d), lambda l, j: (0, 0)),
            pl.BlockSpec((1, d, d), lambda l, j: (l, 0, j)),
            pl.BlockSpec((1, 1, d), lambda l, j: (l, 0, j)),
        ],
        out_specs=pl.BlockSpec((1, b, d), lambda l, j: (l, 0, j)),
        out_shape=jax.ShapeDtypeStruct((depth, b, d6), F32),
        compiler_params=_cparams(("parallel", "parallel")),
        name="adaln_mod",
    )(c, ada_w, ada_b.reshape(depth, 1, d6))


def _rope_kernel(pos_ref, freq_ref, sign_ref, cos_ref, sin_ref):
    ang = pos_ref[0].astype(F32) * freq_ref[...]
    cos_ref[0] = jnp.cos(ang)
    sin_ref[0] = jnp.sin(ang) * sign_ref[...]


def _rope_call(positions, d):
    b, t = positions.shape
    tb = min(t, 512)
    inv_freq = ROPE_THETA ** (-jnp.arange(0, d, 2, dtype=F32) / d)
    freq2 = jnp.concatenate([inv_freq, inv_freq]).reshape(1, d)
    sign2 = jnp.concatenate([-jnp.ones((d // 2,), F32), jnp.ones((d // 2,), F32)]).reshape(1, d)
    out = jax.ShapeDtypeStruct((b, t, d), F32)
    return pl.pallas_call(
        _rope_kernel,
        grid=(b, t // tb),
        in_specs=[
            pl.BlockSpec((1, tb, 1), lambda i, j: (i, j, 0)),
            pl.BlockSpec((1, d), lambda i, j: (0, 0)),
            pl.BlockSpec((1, d), lambda i, j: (0, 0)),
        ],
        out_specs=[pl.BlockSpec((1, tb, d), lambda i, j: (i, j, 0))] * 2,
        out_shape=[out, out],
        compiler_params=_cparams(("parallel", "parallel")),
        name="rope_tables",
    )(positions.reshape(b, t, 1), freq2, sign2)


W_BLK = 256


def _wprep_kernel(w_ref, o_ref):
    o_ref[...] = w_ref[...].astype(o_ref.dtype)


def _wprep_call(w_in, layer):
    _, d, cols = w_in.shape
    nblk = cols // W_BLK
    first = (cols - 3 * d) // W_BLK
    return pl.pallas_call(
        _wprep_kernel,
        grid=(nblk,),
        in_specs=[pl.BlockSpec((1, d, W_BLK), lambda j: (layer, 0, (j + first) % nblk))],
        out_specs=pl.BlockSpec((1, d, W_BLK), lambda j: (0, 0, j)),
        out_shape=jax.ShapeDtypeStruct((1, d, cols), BF16),
        compiler_params=_cparams(("parallel",)),
        name="w_in_layout",
    )(w_in)


def _inproj_kernel(x_ref, g_ref, scale_ref, shift_ref, w_ref, o_ref, h_ref):
    @pl.when(pl.program_id(1) == 0)
    def _():
        h = _rms_mod(x_ref[...], g_ref[...], scale_ref[0], shift_ref[0])
        h_ref[...] = h.astype(BF16)

    o_ref[...] = _dot(h_ref[...], w_ref[0]).astype(o_ref.dtype)


def _inproj_call(x2, gain, mod3, w_bf16, seq, scale_blk, shift_blk, tm=2048, tn=1792):
    n, d = x2.shape
    cols = w_bf16.shape[2]
    tpb = seq // tm
    return pl.pallas_call(
        _inproj_kernel,
        grid=(n // tm, cols // tn),
        in_specs=[
            pl.BlockSpec((tm, d), lambda i, j: (i, 0)),
            pl.BlockSpec((1, d), lambda i, j: (0, 0)),
            pl.BlockSpec((1, 1, d), lambda i, j: (i // tpb, 0, scale_blk)),
            pl.BlockSpec((1, 1, d), lambda i, j: (i // tpb, 0, shift_blk)),
            pl.BlockSpec((1, d, tn), lambda i, j: (0, 0, j)),
        ],
        out_specs=pl.BlockSpec((tm, tn), lambda i, j: (i, j)),
        out_shape=jax.ShapeDtypeStruct((n, cols), Z_DTYPE),
        scratch_shapes=[pltpu.VMEM((tm, d), BF16)],
        compiler_params=_cparams(("parallel", "arbitrary")),
        name="norm_inproj",
    )(x2, gain.reshape(1, d), mod3, mod3, w_bf16)


def _hgrn2_block(zs, lbs, nw, sts, factored):
    hs = range(len(zs))
    tb = zs[0][0].shape[0]
    c, sub = HG_CHUNK, HG_SUB
    nc, ns, nb = tb // c, c // sub, tb // sub
    f = [lbs[h] + (1.0 - lbs[h]) * _sigmoid(zs[h][1]) for h in hs]
    logf = [jnp.log(jnp.maximum(f[h], 1e-30)) for h in hs]
    q = [_silu(zs[h][0]) * (HG_DK ** -0.5) for h in hs]
    k = [1.0 - f[h] for h in hs]
    v = [zs[h][2] for h in hs]
    v_b = [v[h].astype(BF16) for h in hs]
    row = lax.broadcasted_iota(jnp.int32, (tb, tb), 0)
    col = lax.broadcasted_iota(jnp.int32, (tb, tb), 1)
    tri = jnp.where(col >= (row // c) * c, jnp.where(row >= col, 1.0, 0.0), 0.0).astype(BF16)
    cum = [_dot_x2_rhs(tri, logf[h]) for h in hs]
    cum3 = [cum[h].reshape(nb, sub, HG_DK) for h in hs]
    ref3 = [cum3[h][:, 0:1, :] - logf[h].reshape(nb, sub, HG_DK)[:, 0:1, :] for h in hs]
    span = functools.reduce(jnp.maximum, [jnp.max(ref3[h] - cum3[h][:, sub - 1:sub, :]) for h in hs])
    qe = [(q[h] * jnp.exp(cum[h])).astype(BF16) for h in hs]

    offd = [(h, ci * c, ci * c + sub * i) for h in hs for ci in range(nc) for i in range(1, ns)]
    base = [cum[h][lo - 1:lo] for h, _, lo in offd]
    qt = [(q[h][lo:lo + sub] * jnp.exp(cum[h][lo:lo + sub] - base[j])).astype(BF16)
          for j, (h, _, lo) in enumerate(offd)]
    kt = [(k[h][r0:lo] * jnp.exp(base[j] - cum[h][r0:lo])).astype(BF16) for j, (h, r0, lo) in enumerate(offd)]
    a = [_dot_nt(qt[j], kt[j]).astype(BF16) for j in range(len(offd))]
    av = {(h, lo): _dot(a[j], v_b[h][r0:lo]) for j, (h, r0, lo) in enumerate(offd)}

    cs = [slice(ci * c, (ci + 1) * c) for ci in range(nc)]
    hc = [(h, ci) for h in hs for ci in range(nc)]
    last = {(h, ci): cum[h][(ci + 1) * c - 1:(ci + 1) * c] for h, ci in hc}
    kd = {(h, ci): (k[h][cs[ci]] * jnp.exp(last[h, ci] - cum[h][cs[ci]])).astype(BF16) for h, ci in hc}
    inc = {(h, ci): _dot_tn(v_b[h][cs[ci]], kd[h, ci]) for h, ci in hc}
    s_in = {(h, 0): sts[h] for h in hs}
    for ci in range(nc):
        for h in hs:
            s_in[h, ci + 1] = s_in[h, ci] * jnp.exp(last[h, ci]) + inc[h, ci]
    o_inter = {(h, ci): _dot_nt(qe[h][cs[ci]], s_in[h, ci].astype(BF16)) for h, ci in hc}

    if factored:
        qf = [(q[h] * jnp.exp(cum3[h] - ref3[h]).reshape(tb, HG_DK)).astype(BF16) for h in hs]
        kf = [(k[h] * jnp.exp(ref3[h] - cum3[h]).reshape(tb, HG_DK)).astype(BF16) for h in hs]
        rc = lax.broadcasted_iota(jnp.int32, (c, c), 0)
        cc = lax.broadcasted_iota(jnp.int32, (c, c), 1)
        keep = (rc >= cc) & (rc // sub == cc // sub)
        a_d = {(h, ci): jnp.where(keep, _dot_nt(qf[h][cs[ci]], kf[h][cs[ci]]), 0.0).astype(BF16) for h, ci in hc}
        dg = {(h, ci): _dot(a_d[h, ci], v_b[h][cs[ci]]) for h, ci in hc}
        diag = [jnp.concatenate([dg[h, ci] for ci in range(nc)], axis=0) for h in hs]
    else:
        gb = 4
        trow = lax.broadcasted_iota(jnp.int32, (gb, sub, HG_DK), 1)
        diag = []
        for h in hs:
            c2 = cum[h] * LOG2E
            ks2 = c2 - jnp.log2(k[h])
            parts = []
            for g0 in range(0, nb, gb):
                rws = slice(g0 * sub, (g0 + gb) * sub)
                c23, ks23, q3, v3 = (x[rws].reshape(gb, sub, HG_DK) for x in (c2, ks2, q[h], v[h]))
                acc = jnp.zeros((gb, sub, HG_DK), F32)
                for s in range(sub):
                    e = jnp.exp2(jnp.where(trow >= s, c23 - ks23[:, s:s + 1, :], -jnp.inf))
                    a_col = jnp.sum(q3 * e, axis=-1, keepdims=True)
                    acc = acc + a_col * v3[:, s:s + 1, :]
                parts.append(acc.reshape(gb * sub, HG_DK))
            diag.append(jnp.concatenate(parts, axis=0))

    outs = []
    for h in hs:
        pieces = []
        for ci in range(nc):
            for i in range(ns):
                lo = ci * c + sub * i
                piece = o_inter[h, ci][sub * i:sub * (i + 1)] + diag[h][lo:lo + sub]
                pieces.append(piece + av[h, lo] if i > 0 else piece)
        o = jnp.concatenate(pieces, axis=0)
        o = o * lax.rsqrt(jnp.mean(o * o, axis=-1, keepdims=True) + NORM_EPS)
        outs.append(o * nw * _silu(zs[h][3]))
    return outs, [s_in[h, nc] for h in hs], span


def _hgrn2_kernel(zq_ref, zf_ref, zi_ref, zg_ref, lb_ref, nw_ref, o_ref, st_ref):
    @pl.when(pl.program_id(1) == 0)
    def _():
        st_ref[...] = jnp.zeros_like(st_ref)

    nbe = zq_ref.shape[0]
    hs = range(HG_HEADS)
    sl = [slice(h * HG_DK, (h + 1) * HG_DK) for h in hs]
    items = [(bi, h) for bi in range(nbe) for h in hs]

    def run(factored):
        zs = [tuple(r[bi, :, sl[h]].astype(F32) for r in (zq_ref, zf_ref, zi_ref, zg_ref)) for bi, h in items]
        outs, sts, span = _hgrn2_block(zs, [lb_ref[:, sl[h]] for _, h in items], nw_ref[...],
                                       [st_ref[i] for i in range(len(items))], factored)
        return outs, sts, span

    def put(outs, sts):
        for i in range(len(items)):
            st_ref[i] = sts[i]
        for bi in range(nbe):
            o_ref[bi] = jnp.concatenate(outs[bi * HG_HEADS:(bi + 1) * HG_HEADS], axis=1).astype(o_ref.dtype)

    st_old = [st_ref[i] for i in range(len(items))]
    outs, st_new, span = run(True)
    put(outs, st_new)

    @pl.when(span > HG_SAFE_SPAN)
    def _():
        for i in range(len(items)):
            st_ref[i] = st_old[i]
        outs2, st2, _ = run(False)
        put(outs2, st2)


def _hgrn2_call(z3, lower_bound, norm_w, tb=256):
    b, t, _ = z3.shape
    tb = min(tb, t)
    base = HG_OFF // HG_W

    nbe = HG_NB if b % HG_NB == 0 else 1

    def zspec(part):
        return pl.BlockSpec((nbe, tb, HG_W), lambda i, j: (i, j, base + part))

    return pl.pallas_call(
        _hgrn2_kernel,
        grid=(b // nbe, t // tb),
        in_specs=[
            zspec(0), zspec(1), zspec(2), zspec(3),
            pl.BlockSpec((1, HG_W), lambda i, j: (0, 0)),
            pl.BlockSpec((1, LANES), lambda i, j: (0, 0)),
        ],
        out_specs=pl.BlockSpec((nbe, tb, HG_W), lambda i, j: (i, j, 0)),
        out_shape=jax.ShapeDtypeStruct((b, t, HG_W), BF16),
        scratch_shapes=[pltpu.VMEM((nbe * HG_HEADS, HG_DK, HG_DK), F32)],
        compiler_params=_cparams(("parallel", "arbitrary")),
        name="hgrn2_mixer",
    )(z3, z3, z3, z3, lower_bound.reshape(1, HG_W), norm_w.reshape(1, HG_DK))


def _ret_kernel(zq_ref, zk_ref, zv_ref, zg_ref, cos_ref, sin_ref, o_ref, st_ref, dmask_ref, *, chunk):
    hs = range(RET_HEADS)
    sl = [slice(h * RET_DK, (h + 1) * RET_DK) for h in hs]
    lg = [jnp.log(jnp.full((1, 1), 1.0 - 2.0 ** (-5.0 - h), F32)) for h in hs]

    @pl.when(pl.program_id(1) == 0)
    def _():
        st_ref[...] = jnp.zeros_like(st_ref)
        row = lax.broadcasted_iota(jnp.int32, (chunk, chunk), 0)
        col = lax.broadcasted_iota(jnp.int32, (chunk, chunk), 1)
        rel = (row - col).astype(F32)
        for h in hs:
            dmask_ref[h] = jnp.where(rel >= 0.0, jnp.exp(jnp.maximum(rel, 0.0) * lg[h]), 0.0)

    cos2 = cos_ref[0]
    sin2 = sin_ref[0]
    half = RET_DK // 2

    def rope(z):
        return z * cos2 + pltpu.roll(z, half, 1) * sin2

    tcol = lax.broadcasted_iota(jnp.int32, (chunk, 1), 0).astype(F32)
    q = [rope(zq_ref[0, :, sl[h]].astype(F32)) * (RET_DK ** -0.5) for h in hs]
    k = [rope(zk_ref[0, :, sl[h]].astype(F32)) for h in hs]
    v_b = [zv_ref[0, :, sl[h]].astype(BF16) for h in hs]
    st = [st_ref[h] for h in hs]
    scores = [(_dot_nt(q[h].astype(BF16), k[h].astype(BF16)) * dmask_ref[h]).astype(BF16) for h in hs]
    qx = [(q[h] * jnp.exp((tcol + 1.0) * lg[h])).astype(BF16) for h in hs]
    kz = [(k[h] * jnp.exp((chunk - 1.0 - tcol) * lg[h])).astype(BF16) for h in hs]
    o = [_dot(scores[h], v_b[h]) + _dot_nt(qx[h], st[h].astype(BF16)) for h in hs]
    for h in hs:
        st_ref[h] = st[h] * jnp.exp(chunk * lg[h]) + _dot_tn(v_b[h], kz[h])
    o = [o[h] * lax.rsqrt(jnp.mean(o[h] * o[h], axis=-1, keepdims=True) + NORM_EPS) for h in hs]
    o_ref[0] = (jnp.concatenate(o, axis=1) * _silu(zg_ref[0].astype(F32))).astype(o_ref.dtype)


def _ret_call(z3, cos2, sin2, chunk=256):
    b, t, _ = z3.shape
    chunk = min(chunk, t)
    base = RET_OFF // RET_W

    def zspec(part):
        return pl.BlockSpec((1, chunk, RET_W), lambda i, j: (i, j, base + part))

    tab = pl.BlockSpec((1, chunk, RET_DK), lambda i, j: (i, j, 0))
    return pl.pallas_call(
        functools.partial(_ret_kernel, chunk=chunk),
        grid=(b, t // chunk),
        in_specs=[zspec(0), zspec(1), zspec(2), zspec(3), tab, tab],
        out_specs=pl.BlockSpec((1, chunk, RET_W), lambda i, j: (i, j, 0)),
        out_shape=jax.ShapeDtypeStruct((b, t, RET_W), BF16),
        scratch_shapes=[pltpu.VMEM((RET_HEADS, RET_DK, RET_DK), F32), pltpu.VMEM((RET_HEADS, chunk, chunk), F32)],
        compiler_params=_cparams(("parallel", "arbitrary")),
        name="retention_mixer",
    )(z3, z3, z3, z3, cos2, sin2)


def _pair_blockdiag(y, pair_mask):
    return jnp.where(pair_mask, jnp.concatenate([y, y], axis=0), 0.0).astype(BF16)


def _pair_dot(x, y, pair_mask):
    return _dot(x.astype(BF16), _pair_blockdiag(y, pair_mask))


def _inv_unit_lower(a, eye, blk_mask, pair_mask):
    c = a[0].shape[0]
    m = range(len(a))
    a_bd = [jnp.where(blk_mask, a[i], 0.0) for i in m]
    a_off = [a[i] - a_bd[i] for i in m]
    a2 = [_pair_dot(a_bd[i], a_bd[i], pair_mask) for i in m]
    p = [eye + a_bd[i] for i in m]
    r = [_pair_dot(jnp.concatenate([p[i], a2[i]], axis=0), a2[i], pair_mask) for i in m]
    p = [p[i] + r[i][:c] for i in m]
    a4 = [r[i][c:] for i in m]
    r = [_pair_dot(jnp.concatenate([p[i], a4[i]], axis=0), a4[i], pair_mask) for i in m]
    p = [p[i] + r[i][:c] for i in m]
    a8 = [r[i][c:] for i in m]
    t_bd = [p[i] + _pair_dot(p[i], a8[i], pair_mask) for i in m]
    n = [_pair_dot(t_bd[i], a_off[i], pair_mask) for i in m]
    n2 = [_pair_dot(n[i], n[i], pair_mask) for i in m]
    z = [t_bd[i] + _pair_dot(n[i], t_bd[i], pair_mask) for i in m]
    return [z[i] + _pair_dot(n2[i], z[i], pair_mask) for i in m]


def _rwkv_kernel(z_ref, mu_ref, w0_ref, w2_ref, a0_ref, a2_ref, g2_ref, kk_ref, ka_ref, rk_ref,
                 lnw_ref, lnb_ref, seg_ref, o_ref, s_ref, prev_ref):
    c = RW_CHUNK
    nbe, tb = z_ref.shape[0], z_ref.shape[1]
    nck = tb // c
    bs = range(nbe)

    @pl.when(pl.program_id(1) == 0)
    def _():
        s_ref[...] = jnp.zeros_like(s_ref)
        prev_ref[...] = jnp.zeros_like(prev_ref)

    seg = seg_ref[...]
    rows = lax.broadcasted_iota(jnp.int32, (tb, 1), 0)
    rowb = lax.broadcasted_iota(jnp.int32, (tb, tb), 0)
    colb = lax.broadcasted_iota(jnp.int32, (tb, tb), 1)
    tri = jnp.where(colb >= (rowb // c) * c, jnp.where(rowb >= colb, 1.0, 0.0), 0.0).astype(BF16)

    def front(bi):
        z = z_ref[bi].astype(F32)
        z_prev = jnp.where(rows == 0, prev_ref[bi:bi + 1, :], pltpu.roll(z, 1, 0))
        prev_ref[bi:bi + 1, :] = z[tb - 1:tb]
        zs = z + mu_ref[...] * (z_prev - z)
        r = zs[:, 0:RW_W]
        k = zs[:, RW_W:2 * RW_W]
        v = zs[:, 2 * RW_W:3 * RW_W]
        off = 3 * RW_W
        w_lo = zs[:, off:off + RW_DECAY_LORA]
        a_lo = zs[:, off + RW_DECAY_LORA:off + RW_DECAY_LORA + RW_A_LORA]
        g_lo = zs[:, off + RW_DECAY_LORA + RW_A_LORA:]
        wx = -(w0_ref[...] + _dot_x3(jnp.tanh(w_lo), w2_ref[...]))
        softplus = jnp.maximum(wx, 0.0) + jnp.log(1.0 + jnp.exp(-jnp.abs(wx)))
        logw = -jnp.exp(-softplus - 0.5)
        a = _sigmoid(a0_ref[...] + _dot(a_lo.astype(BF16), a2_ref[...].astype(BF16)))
        g = _dot(_sigmoid(g_lo).astype(BF16), g2_ref[...].astype(BF16))
        kk = k * kk_ref[...]
        kk = kk * lax.rsqrt(jnp.maximum(_seg_sum(kk * kk, seg), 1e-24))
        k2 = k * (1.0 + (a - 1.0) * ka_ref[...])
        cw = _dot_x2_rhs(tri, logw)
        w_inv = jnp.exp(-cw)
        last = jnp.concatenate([jnp.broadcast_to(cw[(ci + 1) * c - 1:(ci + 1) * c], (c, RW_W)) for ci in range(nck)],
                               axis=0)
        w_rest = jnp.exp(last - cw)
        beta = a * kk
        return dict(alpha_t=-kk * jnp.exp(cw - logw), r_t=r * jnp.exp(cw), beta_h=beta * w_inv, k_h=k2 * w_inv,
                    beta_d=beta * w_rest, k_d=k2 * w_rest, v=v, g=g, rkk=r * k2 * rk_ref[...], cw=cw)

    fr = [front(bi) for bi in bs]

    pw = 2 * RW_N
    row2 = lax.broadcasted_iota(jnp.int32, (c, pw), 0)
    col2 = lax.broadcasted_iota(jnp.int32, (c, pw), 1) % c
    incl2 = row2 >= col2
    strict2 = row2 > col2
    blk_mask = (row2 // RW_BLK) == (col2 // RW_BLK)
    eye = (row2 == col2).astype(F32)
    rowp = lax.broadcasted_iota(jnp.int32, (pw, pw), 0)
    colp = lax.broadcasted_iota(jnp.int32, (pw, pw), 1)
    pair_mask = (rowp // RW_N) == (colp // RW_N)

    ps = range(RW_HEADS // 2)
    items = [(bi, ci, p) for bi in bs for ci in range(nck) for p in ps]
    where = {key: i for i, key in enumerate(items)}
    m = range(len(items))

    def slab(name, i):
        bi, ci, p = items[i]
        return fr[bi][name][ci * c:(ci + 1) * c, p * pw:(p + 1) * pw]

    v2 = [slab("v", i) for i in m]
    lhs = [jnp.concatenate([slab("alpha_t", i), slab("r_t", i)], axis=0).astype(BF16) for i in m]
    rhs = [jnp.concatenate([_pair_blockdiag(slab("beta_h", i), pair_mask),
                            _pair_blockdiag(slab("k_h", i), pair_mask)], axis=0) for i in m]
    big = [_dot_nt(lhs[i], rhs[i]) for i in m]
    a_ab = [jnp.where(strict2, big[i][:c, :pw], 0.0) for i in m]
    a_ak = [jnp.where(strict2, big[i][:c, pw:], 0.0) for i in m]
    a_rb = [jnp.where(incl2, big[i][c:, :pw], 0.0) for i in m]
    a_rk = [jnp.where(incl2, big[i][c:, pw:], 0.0) for i in m]
    t_inv = _inv_unit_lower(a_ab, eye, blk_mask, pair_mask)
    av = [_pair_dot(a_ak[i], v2[i], pair_mask) for i in m]
    u_const = [_pair_dot(t_inv[i], av[i], pair_mask) for i in m]
    lhs_s = [jnp.concatenate([_pair_dot(t_inv[i], slab("alpha_t", i), pair_mask).astype(BF16),
                              slab("r_t", i).astype(BF16)], axis=0) for i in m]
    a_r = [jnp.concatenate([a_rb[i], a_rk[i]], axis=1).astype(BF16) for i in m]
    bk_d = [jnp.concatenate([slab("beta_d", i), slab("k_d", i)], axis=0).astype(BF16) for i in m]
    bp = [(bi, p) for bi in bs for p in ps]
    s_cur = {(bi, p): s_ref[bi * len(ps) + p] for bi, p in bp}
    o_parts = {}
    for ci in range(nck):
        ix = {(bi, p): where[bi, ci, p] for bi, p in bp}
        sd = {q: _dot_nt(lhs_s[ix[q]], s_cur[q].astype(BF16)) for q in bp}
        u = {q: sd[q][:c] + u_const[ix[q]] for q in bp}
        uv = {q: jnp.concatenate([_pair_blockdiag(u[q], pair_mask), _pair_blockdiag(v2[ix[q]], pair_mask)], axis=0)
              for q in bp}
        for q in bp:
            o_parts[q[0], ci, q[1]] = sd[q][c:] + _dot(a_r[ix[q]], uv[q])
        uvt = {q: jnp.concatenate([u[q], v2[ix[q]]], axis=0).astype(BF16) for q in bp}
        for bi, p in bp:
            w_last = jnp.exp(fr[bi]["cw"][(ci + 1) * c - 1:(ci + 1) * c, p * pw:(p + 1) * pw])
            s_cur[bi, p] = s_cur[bi, p] * w_last + jnp.where(pair_mask, _dot_tn(uvt[bi, p], bk_d[ix[bi, p]]), 0.0)
    for bi, p in bp:
        s_ref[bi * len(ps) + p] = s_cur[bi, p]

    for bi in bs:
        o = jnp.concatenate([jnp.concatenate([o_parts[bi, ci, p] for p in ps], axis=1) for ci in range(nck)], axis=0)
        mean = _seg_sum(o, seg) * (1.0 / RW_N)
        dev = o - mean
        var = _seg_sum(dev * dev, seg) * (1.0 / RW_N)
        o = dev * lax.rsqrt(var + RW_GN_EPS) * lnw_ref[...] + lnb_ref[...]
        bonus = _seg_sum(fr[bi]["rkk"], seg) * fr[bi]["v"]
        o_ref[bi] = ((o + bonus) * fr[bi]["g"]).astype(o_ref.dtype)


def _rwkv_call(z3, mu, w0, w2, a0, a2, g2, k_k, k_a, r_k, ln_w, ln_b):
    b, t, _ = z3.shape
    c = min(RW_TB, t)
    hid = lax.broadcasted_iota(jnp.int32, (RW_W, RW_W), 0) // RW_N
    seg = (hid == hid.T).astype(BF16)

    def vec(n):
        return pl.BlockSpec((1, n), lambda i, j: (0, 0))

    def mat(m, n):
        return pl.BlockSpec((m, n), lambda i, j: (0, 0))

    nbe = RW_NB if b % RW_NB == 0 else 1
    return pl.pallas_call(
        _rwkv_kernel,
        grid=(b // nbe, t // c),
        in_specs=[
            pl.BlockSpec((nbe, c, RW_COLS), lambda i, j: (i, j, RW_OFF // RW_COLS)),
            vec(RW_COLS), vec(RW_W), mat(RW_DECAY_LORA, RW_W), vec(RW_W), mat(RW_A_LORA, RW_W),
            mat(RW_GATE_LORA, RW_W), vec(RW_W), vec(RW_W), vec(RW_W), vec(RW_W), vec(RW_W),
            mat(RW_W, RW_W),
        ],
        out_specs=pl.BlockSpec((nbe, c, RW_W), lambda i, j: (i, j, 0)),
        out_shape=jax.ShapeDtypeStruct((b, t, RW_W), BF16),
        scratch_shapes=[pltpu.VMEM((nbe * (RW_HEADS // 2), 2 * RW_N, 2 * RW_N), F32),
                        pltpu.VMEM((nbe, RW_COLS), F32)],
        compiler_params=_cparams(("parallel", "arbitrary")),
        name="rwkv7_mixer",
    )(z3, mu.reshape(1, -1), w0.reshape(1, -1), w2, a0.reshape(1, -1), a2, g2, k_k.reshape(1, -1),
      k_a.reshape(1, -1), r_k.reshape(1, -1), ln_w.reshape(1, -1), ln_b.reshape(1, -1), seg)


def _merge_kernel(ohg_ref, oret_ref, orw_ref, zg_ref, x_ref, gate_ref, bhg_ref, bret_ref, brw_ref,
                  wout_ref, o_ref):
    d = x_ref.shape[1]
    y = _sigmoid(zg_ref[:, 0:d].astype(F32)) * _dot(ohg_ref[...], bhg_ref[...])
    y = y + _sigmoid(zg_ref[:, d:2 * d].astype(F32)) * _dot(oret_ref[...], bret_ref[...])
    y = y + _sigmoid(zg_ref[:, 2 * d:3 * d].astype(F32)) * _dot(orw_ref[...], brw_ref[...])
    o_ref[...] = x_ref[...] + gate_ref[0] * _dot(y.astype(BF16), wout_ref[...])


def _merge_call(o_hg, o_ret, o_rw, z2, x2, mod3, br_hg, br_ret, br_rw, w_out, seq, gate_blk, tm=1024):
    n, d = x2.shape
    tpb = seq // tm

    def rows(w):
        return pl.BlockSpec((tm, w), lambda i: (i, 0))

    def full(m, k):
        return pl.BlockSpec((m, k), lambda i: (0, 0))

    return pl.pallas_call(
        _merge_kernel,
        grid=(n // tm,),
        in_specs=[
            rows(HG_W), rows(RET_W), rows(RW_W), rows(3 * d), rows(d),
            pl.BlockSpec((1, 1, d), lambda i: (i // tpb, 0, gate_blk)),
            full(HG_W, d), full(RET_W, d), full(RW_W, d), full(d, d),
        ],
        out_specs=rows(d),
        out_shape=jax.ShapeDtypeStruct((n, d), F32),
        compiler_params=_cparams(("parallel",)),
        name="merge_outproj",
    )(o_hg, o_ret, o_rw, z2, x2, mod3, br_hg, br_ret, br_rw, w_out)


def _pack_bf16_pairs(x):
    w = x.shape[1] // 2
    hi = pltpu.bitcast(x[:, :w].astype(BF16).astype(F32), jnp.uint32)
    lo = pltpu.bitcast(x[:, w:].astype(BF16).astype(F32), jnp.uint32)
    return pltpu.bitcast(hi | lax.shift_right_logical(lo, jnp.uint32(16)), jnp.int32)


def _unpack_bf16_pairs(p):
    u = pltpu.bitcast(p, jnp.uint32)
    hi = pltpu.bitcast(u & jnp.uint32(0xFFFF0000), F32)
    lo = pltpu.bitcast(lax.shift_left(u, jnp.uint32(16)), F32)
    return jnp.concatenate([hi, lo], axis=1)


def _route_kernel(x_ref, g_ref, scale_ref, shift_ref, rc_ref, hp_ref, eid_ref, wts_ref, cnt_ref):
    @pl.when(pl.program_id(0) == 0)
    def _():
        cnt_ref[...] = jnp.zeros_like(cnt_ref)

    h = _rms_mod(x_ref[...], g_ref[...], scale_ref[0], shift_ref[0])
    hp_ref[...] = _pack_bf16_pairs(h)
    tm = h.shape[0]
    lane = lax.broadcasted_iota(jnp.int32, (tm, LANES), 1)
    neg = -jnp.inf
    logits = _dot_x3(h, rc_ref[...])
    gl = jnp.where(lane < N_GROUPS, logits, neg)
    gmax = jnp.max(gl, axis=-1, keepdims=True)
    gidx = jnp.min(jnp.where(gl == gmax, lane, LANES), axis=-1, keepdims=True)
    gw = 1.0 / jnp.sum(jnp.exp(gl - gmax), axis=-1, keepdims=True)
    lo = N_GROUPS + gidx * EXPERTS_PER_GROUP
    el = jnp.where(lane >= lo, jnp.where(lane < lo + EXPERTS_PER_GROUP, logits, neg), neg)
    m1 = jnp.max(el, axis=-1, keepdims=True)
    l1 = jnp.min(jnp.where(el == m1, lane, LANES), axis=-1, keepdims=True)
    el2 = jnp.where(lane == l1, neg, el)
    m2 = jnp.max(el2, axis=-1, keepdims=True)
    l2 = jnp.min(jnp.where(el2 == m2, lane, LANES), axis=-1, keepdims=True)
    i1 = l1 - N_GROUPS
    i2 = l2 - N_GROUPS
    e2 = jnp.exp(m2 - m1)
    p1 = 1.0 / (1.0 + e2)
    p2 = e2 * p1
    oh1 = jnp.where(lane == i1, 1.0, 0.0)
    oh2 = jnp.where(lane == i2, 1.0, 0.0)
    row = lax.broadcasted_iota(jnp.int32, (tm, tm), 0)
    col = lax.broadcasted_iota(jnp.int32, (tm, tm), 1)
    earlier = jnp.where(row > col, 1.0, 0.0).astype(BF16)
    before = _dot(earlier, jnp.concatenate([oh1, oh2], axis=1).astype(BF16))
    tot1 = jnp.sum(oh1, axis=0, keepdims=True)
    carry = cnt_ref[...]
    r1 = jnp.sum(oh1 * (before[:, :LANES] + carry), axis=-1, keepdims=True).astype(jnp.int32)
    r2 = jnp.sum(oh2 * (before[:, LANES:] + (carry + tot1)), axis=-1, keepdims=True).astype(jnp.int32)
    cnt_ref[...] = carry + tot1 + jnp.sum(oh2, axis=0, keepdims=True)
    eid_ref[...] = jnp.where(lane == 0, i1, jnp.where(lane == 1, i2, jnp.where(lane == 2, r1,
                                                                             jnp.where(lane == 3, r2, 0))))
    wts_ref[...] = jnp.where(lane == 0, gw * p1, jnp.where(lane == 1, gw * p2, 0.0))


def _route_call(x2, gain, mod3, router_g, router_e, seq, scale_blk, shift_blk, tm=1024):
    n, d = x2.shape
    tpb = seq // tm
    rc = jnp.pad(jnp.concatenate([router_g, router_e], axis=1), ((0, 0), (0, LANES - N_GROUPS - N_EXPERTS)))
    return pl.pallas_call(
        _route_kernel,
        grid=(n // tm,),
        in_specs=[
            pl.BlockSpec((tm, d), lambda i: (i, 0)),
            pl.BlockSpec((1, d), lambda i: (0, 0)),
            pl.BlockSpec((1, 1, d), lambda i: (i // tpb, 0, scale_blk)),
            pl.BlockSpec((1, 1, d), lambda i: (i // tpb, 0, shift_blk)),
            pl.BlockSpec((d, LANES), lambda i: (0, 0)),
        ],
        out_specs=[pl.BlockSpec((tm, d // 2), lambda i: (i, 0)), pl.BlockSpec((tm, LANES), lambda i: (i, 0)),
                   pl.BlockSpec((tm, LANES), lambda i: (i, 0)), pl.BlockSpec((1, LANES), lambda i: (0, 0))],
        out_shape=[jax.ShapeDtypeStruct((n, d // 2), jnp.int32), jax.ShapeDtypeStruct((n, LANES), jnp.int32),
                   jax.ShapeDtypeStruct((n, LANES), F32), jax.ShapeDtypeStruct((1, LANES), F32)],
        compiler_params=_cparams(("arbitrary",)),
        name="moe_route",
    )(x2, gain.reshape(1, d), mod3, mod3, rc)


SC_CORES = 2
SC_SUBCORES = 16
SC_WORKERS = SC_CORES * SC_SUBCORES
SC_ROWS = 32
SC_STREAMS = 4


def _sc_gather(table, idx):
    m = idx.shape[0]
    w = table.shape[1]
    per_worker = m // SC_WORKERS
    steps = per_worker // SC_ROWS
    assert per_worker * SC_WORKERS == m and steps * SC_ROWS == per_worker and steps % SC_STREAMS == 0
    mesh = plsc.VectorSubcoreMesh(core_axis_name="c", subcore_axis_name="s")
    ks = range(SC_STREAMS)

    def body(table_hbm, idx_hbm, out_hbm, idx_v, *rest):
        bufs, g_sems, w_sems = rest[:SC_STREAMS], rest[SC_STREAMS:2 * SC_STREAMS], rest[2 * SC_STREAMS:]
        wid = lax.axis_index("s") * SC_CORES + lax.axis_index("c")
        pltpu.sync_copy(idx_hbm.at[wid], idx_v)

        @pl.loop(0, steps, step=SC_STREAMS)
        def _(j):
            row0 = wid * per_worker + j * SC_ROWS
            gathers = [pltpu.async_copy(table_hbm.at[idx_v.at[j + q]], bufs[q], g_sems[q]) for q in ks]
            writes = []
            for q in ks:
                gathers[q].wait()
                writes.append(pltpu.async_copy(bufs[q], out_hbm.at[pl.ds(row0 + q * SC_ROWS, SC_ROWS)], w_sems[q]))
            for q in ks:
                writes[q].wait()

    return pl.kernel(
        body,
        out_type=jax.ShapeDtypeStruct((m, w), table.dtype),
        mesh=mesh,
        scratch_types=[pltpu.VMEM((steps, SC_ROWS), jnp.int32)] + [pltpu.VMEM((SC_ROWS, w), table.dtype)] * SC_STREAMS
        + [pltpu.SemaphoreType.DMA] * (2 * SC_STREAMS),
        name="sc_row_gather",
    )(table, idx.reshape(SC_WORKERS, steps, SC_ROWS))


def _sc_scatter2(rows, idx0, idx1, p):
    n, w = rows.shape
    per_worker = n // SC_WORKERS
    steps = per_worker // SC_ROWS
    assert per_worker * SC_WORKERS == n and steps * SC_ROWS == per_worker and steps % SC_STREAMS == 0
    mesh = plsc.VectorSubcoreMesh(core_axis_name="c", subcore_axis_name="s")
    ks = range(SC_STREAMS)

    def body(rows_hbm, i0_hbm, i1_hbm, out_hbm, i0_v, i1_v, *rest):
        bufs, r_sems = rest[:SC_STREAMS], rest[SC_STREAMS:2 * SC_STREAMS]
        s0_sems, s1_sems = rest[2 * SC_STREAMS:3 * SC_STREAMS], rest[3 * SC_STREAMS:]
        wid = lax.axis_index("s") * SC_CORES + lax.axis_index("c")
        pltpu.sync_copy(i0_hbm.at[wid], i0_v)
        pltpu.sync_copy(i1_hbm.at[wid], i1_v)

        @pl.loop(0, steps, step=SC_STREAMS)
        def _(j):
            row0 = wid * per_worker + j * SC_ROWS
            reads = [pltpu.async_copy(rows_hbm.at[pl.ds(row0 + q * SC_ROWS, SC_ROWS)], bufs[q], r_sems[q]) for q in ks]
            writes = []
            for q in ks:
                reads[q].wait()
                writes.append(pltpu.async_copy(bufs[q], out_hbm.at[i0_v.at[j + q]], s0_sems[q]))
                writes.append(pltpu.async_copy(bufs[q], out_hbm.at[i1_v.at[j + q]], s1_sems[q]))
            for wr in writes:
                wr.wait()

    index_block = pltpu.VMEM((steps, SC_ROWS), jnp.int32)
    return pl.kernel(
        body,
        out_type=jax.ShapeDtypeStruct((p, w), rows.dtype),
        mesh=mesh,
        scratch_types=[index_block, index_block] + [pltpu.VMEM((SC_ROWS, w), rows.dtype)] * SC_STREAMS
        + [pltpu.SemaphoreType.DMA] * (3 * SC_STREAMS),
        name="sc_row_scatter",
    )(rows, idx0.reshape(SC_WORKERS, steps, SC_ROWS), idx1.reshape(SC_WORKERS, steps, SC_ROWS))


MOE_TM = 512


def _gexperts_kernel(te_ref, tv_ref, nu_ref, xs_ref, w1_ref, w3_ref, w2_ref, ys_ref, w1b_ref, w3b_ref, w2b_ref):
    i = pl.program_id(0)

    @pl.when((i == 0) | (te_ref[i] != te_ref[jnp.maximum(i - 1, 0)]))
    def _():
        w1b_ref[...] = w1_ref[0].astype(BF16)
        w3b_ref[...] = w3_ref[0].astype(BF16)
        w2b_ref[...] = w2_ref[0].astype(BF16)

    @pl.when(i < nu_ref[0])
    def _():
        rid = lax.broadcasted_iota(jnp.int32, xs_ref.shape, 0)
        xb = _unpack_bf16_pairs(jnp.where(rid < tv_ref[i], xs_ref[...], 0)).astype(BF16)
        act = (_silu(_dot(xb, w1b_ref[...])) * _dot(xb, w3b_ref[...])).astype(BF16)
        ys_ref[...] = _pack_bf16_pairs(_dot(act, w2b_ref[...]))


def _gexperts_call(xs, tile_expert, tile_valid, n_used, w1, w3, w2):
    p, half = xs.shape
    ne, d, de = w1.shape
    nt = p // MOE_TM

    def rows(i, te, tv, nu):
        return (jnp.minimum(i, nu[0] - 1), 0)

    def wsel(i, te, tv, nu):
        return (te[i], 0, 0)

    return pl.pallas_call(
        _gexperts_kernel,
        grid_spec=pltpu.PrefetchScalarGridSpec(
            num_scalar_prefetch=3,
            grid=(nt,),
            in_specs=[
                pl.BlockSpec((MOE_TM, half), rows),
                pl.BlockSpec((1, d, de), wsel),
                pl.BlockSpec((1, d, de), wsel),
                pl.BlockSpec((1, de, d), wsel),
            ],
            out_specs=pl.BlockSpec((MOE_TM, half), rows),
            scratch_shapes=[pltpu.VMEM((d, de), BF16), pltpu.VMEM((d, de), BF16), pltpu.VMEM((de, d), BF16)],
        ),
        out_shape=jax.ShapeDtypeStruct((p, half), jnp.int32),
        compiler_params=_cparams(("arbitrary",)),
        name="moe_experts",
    )(tile_expert, tile_valid, n_used, xs, w1, w3, w2)


def _combine_kernel(y0_ref, y1_ref, wts_ref, x_ref, gate_ref, fg_ref, o_ref, *, final_norm):
    wts = wts_ref[...]
    moe = wts[:, 0:1] * _unpack_bf16_pairs(y0_ref[...]) + wts[:, 1:2] * _unpack_bf16_pairs(y1_ref[...])
    xn = x_ref[...] + gate_ref[0] * moe
    if final_norm:
        xn = xn * lax.rsqrt(jnp.mean(xn * xn, axis=-1, keepdims=True) + NORM_EPS) * fg_ref[...]
    o_ref[...] = xn


def _combine_call(yg, wts, x2, mod3, final_g, seq, gate_blk, final_norm, tm=1024):
    n, d = x2.shape
    tpb = seq // tm
    slot1 = n // tm
    return pl.pallas_call(
        functools.partial(_combine_kernel, final_norm=final_norm),
        grid=(n // tm,),
        in_specs=[
            pl.BlockSpec((tm, d // 2), lambda i: (i, 0)),
            pl.BlockSpec((tm, d // 2), lambda i: (i + slot1, 0)),
            pl.BlockSpec((tm, LANES), lambda i: (i, 0)),
            pl.BlockSpec((tm, d), lambda i: (i, 0)),
            pl.BlockSpec((1, 1, d), lambda i: (i // tpb, 0, gate_blk)),
            pl.BlockSpec((1, d), lambda i: (0, 0)),
        ],
        out_specs=pl.BlockSpec((tm, d), lambda i: (i, 0)),
        out_shape=jax.ShapeDtypeStruct((n, d), F32),
        compiler_params=_cparams(("parallel",)),
        name="moe_combine",
    )(yg, yg, wts, x2, mod3, final_g.reshape(1, d))


def _pos_kernel(eid_ref, ts_ref, p0_ref, p1_ref):
    eid = eid_ref[...]
    tm = eid.shape[0]
    lane = lax.broadcasted_iota(jnp.int32, (tm, LANES), 1)
    sub = lax.broadcasted_iota(jnp.int32, (tm, LANES), 0) % LANES
    for slot, out_ref in ((0, p0_ref), (1, p1_ref)):
        first_row = jnp.sum(jnp.where(lane == eid[:, slot:slot + 1], ts_ref[...], 0), axis=-1, keepdims=True)
        pos = first_row + eid[:, slot + 2:slot + 3]
        out_ref[...] = jnp.sum(jnp.where(lane == sub, pos, 0).reshape(tm // LANES, LANES, LANES), axis=1)


def _pos_call(eid, first_rows, tm=4096):
    n = eid.shape[0]
    tm = min(tm, n)
    out = jax.ShapeDtypeStruct((n // LANES, LANES), jnp.int32)
    p0, p1 = pl.pallas_call(
        _pos_kernel,
        grid=(n // tm,),
        in_specs=[pl.BlockSpec((tm, LANES), lambda i: (i, 0)), pl.BlockSpec((1, LANES), lambda i: (0, 0))],
        out_specs=[pl.BlockSpec((tm // LANES, LANES), lambda i: (i, 0))] * 2,
        out_shape=[out, out],
        compiler_params=_cparams(("parallel",)),
        name="moe_positions",
    )(eid, first_rows)
    return p0.reshape(n), p1.reshape(n)


def _moe_plan(eid, counts_f):
    n = eid.shape[0]
    nt = (2 * n) // MOE_TM + N_EXPERTS
    counts = counts_f[0, :N_EXPERTS].astype(jnp.int32)
    tiles = (counts + MOE_TM - 1) // MOE_TM
    tile_end = jnp.cumsum(tiles)
    tile_start = tile_end - tiles
    n_used = tile_end[-1:]
    tile_iota = jnp.arange(nt, dtype=jnp.int32)
    tile_expert = jnp.sum(jnp.minimum(tile_iota, n_used - 1)[:, None] >= tile_end[None, :], axis=1, dtype=jnp.int32)
    own = tile_expert[:, None] == jnp.arange(N_EXPERTS, dtype=jnp.int32)[None, :]
    count_t = jnp.sum(jnp.where(own, counts[None, :], 0), axis=1)
    start_t = jnp.sum(jnp.where(own, tile_start[None, :], 0), axis=1)
    tile_valid = jnp.clip(count_t - (tile_iota - start_t) * MOE_TM, 0, MOE_TM)
    first_rows = jnp.pad(tile_start * MOE_TM, (0, LANES - N_EXPERTS)).reshape(1, LANES)
    pos0, pos1 = _pos_call(eid, first_rows)
    return pos0, pos1, tile_expert, tile_valid, n_used


def kernel(x, c, positions, ada_w, ada_b, norm1_g, norm2_g, w_in, hg_lb_table, hg_norm_w, rw_mu, rw_w0, rw_w2,
           rw_a0, rw_a2, rw_g2, rw_k_k, rw_k_a, rw_r_k, rw_ln_w, rw_ln_b, br_hg, br_ret, br_rw, w_out,
           router_g, router_e, moe_w1, moe_w3, moe_w2, final_g):
    b, t, d = x.shape
    depth = ada_w.shape[0]
    n = b * t
    assert w_in.shape[2] == IN_COLS and d == 1024

    lb_p = jax.nn.softmax(hg_lb_table.astype(F32), axis=0)
    lower_bounds = jnp.cumsum(lb_p, axis=0) - lb_p[0]

    mod = _mod_call(c, ada_w, ada_b)
    cos2, sin2 = _rope_call(positions, RET_DK)
    x2 = x.reshape(n, d)
    for l in range(depth):
        mod3 = mod[l].reshape(b, 1, 6 * d)
        z2 = _inproj_call(x2, norm1_g[l], mod3, _wprep_call(w_in, l), t, scale_blk=1, shift_blk=0)
        z3 = z2.reshape(b, t, IN_COLS)
        o_hg = _hgrn2_call(z3, lower_bounds[l], hg_norm_w[l])
        o_ret = _ret_call(z3, cos2, sin2)
        o_rw = _rwkv_call(z3, rw_mu[l], rw_w0[l], rw_w2[l], rw_a0[l], rw_a2[l], rw_g2[l], rw_k_k[l],
                          rw_k_a[l], rw_r_k[l], rw_ln_w[l], rw_ln_b[l])
        x2 = _merge_call(o_hg.reshape(n, HG_W), o_ret.reshape(n, RET_W), o_rw.reshape(n, RW_W), z2, x2, mod3,
                         br_hg[l].astype(BF16), br_ret[l].astype(BF16), br_rw[l].astype(BF16),
                         w_out[l].astype(BF16), t, gate_blk=2)
        hp, eid, wts, counts = _route_call(x2, norm2_g[l], mod3, router_g[l], router_e[l], t, scale_blk=4,
                                           shift_blk=3)
        pos0, pos1, tile_expert, tile_valid, n_used = _moe_plan(eid, counts)
        xs = _sc_scatter2(hp, pos0, pos1, (2 * n // MOE_TM + N_EXPERTS) * MOE_TM)
        ys = _gexperts_call(xs, tile_expert + l * N_EXPERTS, tile_valid, n_used,
                            moe_w1.reshape((-1,) + moe_w1.shape[2:]), moe_w3.reshape((-1,) + moe_w3.shape[2:]),
                            moe_w2.reshape((-1,) + moe_w2.shape[2:]))
        yg = _sc_gather(ys, jnp.concatenate([pos0, pos1]))
        x2 = _combine_call(yg, wts, x2, mod3, final_g, t, gate_blk=5, final_norm=(l == depth - 1))
    return x2.reshape(b, t, d)
```

```python
import functools

import jax
import jax.numpy as jnp
from jax import lax
from jax.experimental import pallas as pl
from jax.experimental.pallas import tpu as pltpu
from jax.experimental.pallas import tpu_sc as plsc

F32 = jnp.float32
BF16 = jnp.bfloat16
HIGHEST = lax.Precision.HIGHEST

HG_HEADS = 4
HG_DK = 128
HG_W = HG_HEADS * HG_DK
RET_HEADS = 4
RET_DK = 128
RET_W = RET_HEADS * RET_DK
RW_HEADS = 8
RW_N = 64
RW_W = RW_HEADS * RW_N
RW_DECAY_LORA = 64
RW_A_LORA = 64
RW_GATE_LORA = 128
RW_COLS = 3 * RW_W + RW_DECAY_LORA + RW_A_LORA + RW_GATE_LORA
RW_GN_EPS = 64e-5
N_GROUPS = 4
EXPERTS_PER_GROUP = 8
N_EXPERTS = N_GROUPS * EXPERTS_PER_GROUP
ROPE_THETA = 10000.0
NORM_EPS = 1e-6

LANES = 128
LOG2E = 1.4426950408889634
VMEM_LIMIT = 56 * 1024 * 1024

GATE_OFF = 0
HG_OFF = 3 * 1024
RET_OFF = HG_OFF + 4 * HG_W
RW_OFF = RET_OFF + 4 * RET_W
IN_COLS = RW_OFF + RW_COLS

HG_CHUNK = 64
HG_SUB = 16
HG_NB = 2
HG_SAFE_SPAN = 60.0
RW_CHUNK = 64
RW_BLK = 16
RW_NB = 4
RW_TB = 256
Z_DTYPE = BF16


def _cparams(sem):
    return pltpu.CompilerParams(dimension_semantics=sem, vmem_limit_bytes=VMEM_LIMIT)


def _dot(a, b, precision=None):
    return jnp.dot(a, b, preferred_element_type=F32, precision=precision)


def _dot_nt(a, b, precision=None):
    return lax.dot_general(a, b, (((1,), (1,)), ((), ())), preferred_element_type=F32, precision=precision)


def _dot_tn(a, b, precision=None):
    return lax.dot_general(a, b, (((0,), (0,)), ((), ())), preferred_element_type=F32, precision=precision)


def _split_bf16(x):
    hi = x.astype(BF16)
    return hi, (x - hi.astype(F32)).astype(BF16)


def _dot_x3(a, b):
    ah, al = _split_bf16(a)
    bh, bl = _split_bf16(b)
    return _dot(ah, bh) + _dot(ah, bl) + _dot(al, bh)


def _seg_sum(x, seg):
    return _dot(x.astype(BF16), seg)


def _dot_x2_rhs(a_exact, b):
    bh, bl = _split_bf16(b)
    return _dot(a_exact, bh) + _dot(a_exact, bl)


def _sigmoid(x):
    return 0.5 * jnp.tanh(0.5 * x) + 0.5


def _silu(x):
    return x * _sigmoid(x)


def _rms_mod(x, gain, scale, shift):
    y = x * lax.rsqrt(jnp.mean(x * x, axis=-1, keepdims=True) + NORM_EPS)
    return (y * gain) * (1.0 + scale) + shift


def _mod_kernel(c_ref, w_ref, b_ref, o_ref):
    c = c_ref[...]
    o_ref[0] = _dot(_silu(c), w_ref[0], HIGHEST) + b_ref[0]


def _mod_call(c, ada_w, ada_b):
    depth, d, d6 = ada_w.shape
    b = c.shape[0]
    nblk = d6 // d
    return pl.pallas_call(
        _mod_kernel,
        grid=(depth, nblk),
        in_specs=[
            pl.BlockSpec((b, d), lambda l, j: (0, 0)),
            pl.BlockSpec((1, d, d), lambda l, j: (l, 0, j)),
            pl.BlockSpec((1, 1, d), lambda l, j: (l, 0, j)),
        ],
        out_specs=pl.BlockSpec((1, b, d), lambda l, j: (l, 0, j)),
        out_shape=jax.ShapeDtypeStruct((depth, b, d6), F32),
        compiler_params=_cparams(("parallel", "parallel")),
        name="adaln_mod",
    )(c, ada_w, ada_b.reshape(depth, 1, d6))


def _rope_kernel(pos_ref, freq_ref, sign_ref, cos_ref, sin_ref):
    ang = pos_ref[0].astype(F32) * freq_ref[...]
    cos_ref[0] = jnp.cos(ang)
    sin_ref[0] = jnp.sin(ang) * sign_ref[...]


def _rope_call(positions, d):
    b, t = positions.shape
    tb = min(t, 512)
    inv_freq = ROPE_THETA ** (-jnp.arange(0, d, 2, dtype=F32) / d)
    freq2 = jnp.concatenate([inv_freq, inv_freq]).reshape(1, d)
    sign2 = jnp.concatenate([-jnp.ones((d // 2,), F32), jnp.ones((d // 2,), F32)]).reshape(1, d)
    out = jax.ShapeDtypeStruct((b, t, d), F32)
    return pl.pallas_call(
        _rope_kernel,
        grid=(b, t // tb),
        in_specs=[
            pl.BlockSpec((1, tb, 1), lambda i, j: (i, j, 0)),
            pl.BlockSpec((1, d), lambda i, j: (0, 0)),
            pl.BlockSpec((1, d), lambda i, j: (0, 0)),
        ],
        out_specs=[pl.BlockSpec((1, tb, d), lambda i, j: (i, j, 0))] * 2,
        out_shape=[out, out],
        compiler_params=_cparams(("parallel", "parallel")),
        name="rope_tables",
    )(positions.reshape(b, t, 1), freq2, sign2)


W_BLK = 256


def _wprep_kernel(w_ref, o_ref):
    o_ref[...] = w_ref[...].astype(o_ref.dtype)


def _wprep_call(w_in, layer):
    _, d, cols = w_in.shape
    nblk = cols // W_BLK
    first = (cols - 3 * d) // W_BLK
    return pl.pallas_call(
        _wprep_kernel,
        grid=(nblk,),
        in_specs=[pl.BlockSpec((1, d, W_BLK), lambda j: (layer, 0, (j + first) % nblk))],
        out_specs=pl.BlockSpec((1, d, W_BLK), lambda j: (0, 0, j)),
        out_shape=jax.ShapeDtypeStruct((1, d, cols), BF16),
        compiler_params=_cparams(("parallel",)),
        name="w_in_layout",
    )(w_in)


def _inproj_kernel(x_ref, g_ref, scale_ref, shift_ref, w_ref, o_ref, h_ref):
    @pl.when(pl.program_id(1) == 0)
    def _():
        h = _rms_mod(x_ref[...], g_ref[...], scale_ref[0], shift_ref[0])
        h_ref[...] = h.astype(BF16)

    o_ref[...] = _dot(h_ref[...], w_ref[0]).astype(o_ref.dtype)


def _inproj_call(x2, gain, mod3, w_bf16, seq, scale_blk, shift_blk, tm=2048, tn=1792):
    n, d = x2.shape
    cols = w_bf16.shape[2]
    tpb = seq // tm
    return pl.pallas_call(
        _inproj_kernel,
        grid=(n // tm, cols // tn),
        in_specs=[
            pl.BlockSpec((tm, d), lambda i, j: (i, 0)),
            pl.BlockSpec((1, d), lambda i, j: (0, 0)),
            pl.BlockSpec((1, 1, d), lambda i, j: (i // tpb, 0, scale_blk)),
            pl.BlockSpec((1, 1, d), lambda i, j: (i // tpb, 0, shift_blk)),
            pl.BlockSpec((1, d, tn), lambda i, j: (0, 0, j)),
        ],
        out_specs=pl.BlockSpec((tm, tn), lambda i, j: (i, j)),
        out_shape=jax.ShapeDtypeStruct((n, cols), Z_DTYPE),
        scratch_shapes=[pltpu.VMEM((tm, d), BF16)],
        compiler_params=_cparams(("parallel", "arbitrary")),
        name="norm_inproj",
    )(x2, gain.reshape(1, d), mod3, mod3, w_bf16)


def _hgrn2_block(zs, lbs, nw, sts, factored):
    hs = range(len(zs))
    tb = zs[0][0].shape[0]
    c, sub = HG_CHUNK, HG_SUB
    nc, ns, nb = tb // c, c // sub, tb // sub
    f = [lbs[h] + (1.0 - lbs[h]) * _sigmoid(zs[h][1]) for h in hs]
    logf = [jnp.log(jnp.maximum(f[h], 1e-30)) for h in hs]
    q = [_silu(zs[h][0]) * (HG_DK ** -0.5) for h in hs]
    k = [1.0 - f[h] for h in hs]
    v = [zs[h][2] for h in hs]
    v_b = [v[h].astype(BF16) for h in hs]
    row = lax.broadcasted_iota(jnp.int32, (tb, tb), 0)
    col = lax.broadcasted_iota(jnp.int32, (tb, tb), 1)
    tri = jnp.where(col >= (row // c) * c, jnp.where(row >= col, 1.0, 0.0), 0.0).astype(BF16)
    cum = [_dot_x2_rhs(tri, logf[h]) for h in hs]
    cum3 = [cum[h].reshape(nb, sub, HG_DK) for h in hs]
    ref3 = [cum3[h][:, 0:1, :] - logf[h].reshape(nb, sub, HG_DK)[:, 0:1, :] for h in hs]
    span = functools.reduce(jnp.maximum, [jnp.max(ref3[h] - cum3[h][:, sub - 1:sub, :]) for h in hs])
    qe = [(q[h] * jnp.exp(cum[h])).astype(BF16) for h in hs]

    offd = [(h, ci * c, ci * c + sub * i) for h in hs for ci in range(nc) for i in range(1, ns)]
    base = [cum[h][lo - 1:lo] for h, _, lo in offd]
    qt = [(q[h][lo:lo + sub] * jnp.exp(cum[h][lo:lo + sub] - base[j])).astype(BF16)
          for j, (h, _, lo) in enumerate(offd)]
    kt = [(k[h][r0:lo] * jnp.exp(base[j] - cum[h][r0:lo])).astype(BF16) for j, (h, r0, lo) in enumerate(offd)]
    a = [_dot_nt(qt[j], kt[j]).astype(BF16) for j in range(len(offd))]
    av = {(h, lo): _dot(a[j], v_b[h][r0:lo]) for j, (h, r0, lo) in enumerate(offd)}

    cs = [slice(ci * c, (ci + 1) * c) for ci in range(nc)]
    hc = [(h, ci) for h in hs for ci in range(nc)]
    last = {(h, ci): cum[h][(ci + 1) * c - 1:(ci + 1) * c] for h, ci in hc}
    kd = {(h, ci): (k[h][cs[ci]] * jnp.exp(last[h, ci] - cum[h][cs[ci]])).astype(BF16) for h, ci in hc}
    inc = {(h, ci): _dot_tn(v_b[h][cs[ci]], kd[h, ci]) for h, ci in hc}
    s_in = {(h, 0): sts[h] for h in hs}
    for ci in range(nc):
        for h in hs:
            s_in[h, ci + 1] = s_in[h, ci] * jnp.exp(last[h, ci]) + inc[h, ci]
    o_inter = {(h, ci): _dot_nt(qe[h][cs[ci]], s_in[h, ci].astype(BF16)) for h, ci in hc}

    if factored:
        qf = [(q[h] * jnp.exp(cum3[h] - ref3[h]).reshape(tb, HG_DK)).astype(BF16) for h in hs]
        kf = [(k[h] * jnp.exp(ref3[h] - cum3[h]).reshape(tb, HG_DK)).astype(BF16) for h in hs]
        rc = lax.broadcasted_iota(jnp.int32, (c, c), 0)
        cc = lax.broadcasted_iota(jnp.int32, (c, c), 1)
        keep = (rc >= cc) & (rc // sub == cc // sub)
        a_d = {(h, ci): jnp.where(keep, _dot_nt(qf[h][cs[ci]], kf[h][cs[ci]]), 0.0).astype(BF16) for h, ci in hc}
        dg = {(h, ci): _dot(a_d[h, ci], v_b[h][cs[ci]]) for h, ci in hc}
        diag = [jnp.concatenate([dg[h, ci] for ci in range(nc)], axis=0) for h in hs]
    else:
        gb = 4
        trow = lax.broadcasted_iota(jnp.int32, (gb, sub, HG_DK), 1)
        diag = []
        for h in hs:
            c2 = cum[h] * LOG2E
            ks2 = c2 - jnp.log2(k[h])
            parts = []
            for g0 in range(0, nb, gb):
                rws = slice(g0 * sub, (g0 + gb) * sub)
                c23, ks23, q3, v3 = (x[rws].reshape(gb, sub, HG_DK) for x in (c2, ks2, q[h], v[h]))
                acc = jnp.zeros((gb, sub, HG_DK), F32)
                for s in range(sub):
                    e = jnp.exp2(jnp.where(trow >= s, c23 - ks23[:, s:s + 1, :], -jnp.inf))
                    a_col = jnp.sum(q3 * e, axis=-1, keepdims=True)
                    acc = acc + a_col * v3[:, s:s + 1, :]
                parts.append(acc.reshape(gb * sub, HG_DK))
            diag.append(jnp.concatenate(parts, axis=0))

    outs = []
    for h in hs:
        pieces = []
        for ci in range(nc):
            for i in range(ns):
                lo = ci * c + sub * i
                piece = o_inter[h, ci][sub * i:sub * (i + 1)] + diag[h][lo:lo + sub]
                pieces.append(piece + av[h, lo] if i > 0 else piece)
        o = jnp.concatenate(pieces, axis=0)
        o = o * lax.rsqrt(jnp.mean(o * o, axis=-1, keepdims=True) + NORM_EPS)
        outs.append(o * nw * _silu(zs[h][3]))
    return outs, [s_in[h, nc] for h in hs], span


def _hgrn2_kernel(zq_ref, zf_ref, zi_ref, zg_ref, lb_ref, nw_ref, o_ref, st_ref):
    @pl.when(pl.program_id(1) == 0)
    def _():
        st_ref[...] = jnp.zeros_like(st_ref)

    nbe = zq_ref.shape[0]
    hs = range(HG_HEADS)
    sl = [slice(h * HG_DK, (h + 1) * HG_DK) for h in hs]
    items = [(bi, h) for bi in range(nbe) for h in hs]

    def run(factored):
        zs = [tuple(r[bi, :, sl[h]].astype(F32) for r in (zq_ref, zf_ref, zi_ref, zg_ref)) for bi, h in items]
        outs, sts, span = _hgrn2_block(zs, [lb_ref[:, sl[h]] for _, h in items], nw_ref[...],
                                       [st_ref[i] for i in range(len(items))], factored)
        return outs, sts, span

    def put(outs, sts):
        for i in range(len(items)):
            st_ref[i] = sts[i]
        for bi in range(nbe):
            o_ref[bi] = jnp.concatenate(outs[bi * HG_HEADS:(bi + 1) * HG_HEADS], axis=1).astype(o_ref.dtype)

    st_old = [st_ref[i] for i in range(len(items))]
    outs, st_new, span = run(True)
    put(outs, st_new)

    @pl.when(span > HG_SAFE_SPAN)
    def _():
        for i in range(len(items)):
            st_ref[i] = st_old[i]
        outs2, st2, _ = run(False)
        put(outs2, st2)


def _hgrn2_call(z3, lower_bound, norm_w, tb=256):
    b, t, _ = z3.shape
    tb = min(tb, t)
    base = HG_OFF // HG_W

    nbe = HG_NB if b % HG_NB == 0 else 1

    def zspec(part):
        return pl.BlockSpec((nbe, tb, HG_W), lambda i, j: (i, j, base + part))

    return pl.pallas_call(
        _hgrn2_kernel,
        grid=(b // nbe, t // tb),
        in_specs=[
            zspec(0), zspec(1), zspec(2), zspec(3),
            pl.BlockSpec((1, HG_W), lambda i, j: (0, 0)),
            pl.BlockSpec((1, LANES), lambda i, j: (0, 0)),
        ],
        out_specs=pl.BlockSpec((nbe, tb, HG_W), lambda i, j: (i, j, 0)),
        out_shape=jax.ShapeDtypeStruct((b, t, HG_W), BF16),
        scratch_shapes=[pltpu.VMEM((nbe * HG_HEADS, HG_DK, HG_DK), F32)],
        compiler_params=_cparams(("parallel", "arbitrary")),
        name="hgrn2_mixer",
    )(z3, z3, z3, z3, lower_bound.reshape(1, HG_W), norm_w.reshape(1, HG_DK))


def _ret_kernel(zq_ref, zk_ref, zv_ref, zg_ref, cos_ref, sin_ref, o_ref, st_ref, dmask_ref, *, chunk):
    hs = range(RET_HEADS)
    sl = [slice(h * RET_DK, (h + 1) * RET_DK) for h in hs]
    lg = [jnp.log(jnp.full((1, 1), 1.0 - 2.0 ** (-5.0 - h), F32)) for h in hs]

    @pl.when(pl.program_id(1) == 0)
    def _():
        st_ref[...] = jnp.zeros_like(st_ref)
        row = lax.broadcasted_iota(jnp.int32, (chunk, chunk), 0)
        col = lax.broadcasted_iota(jnp.int32, (chunk, chunk), 1)
        rel = (row - col).astype(F32)
        for h in hs:
            dmask_ref[h] = jnp.where(rel >= 0.0, jnp.exp(jnp.maximum(rel, 0.0) * lg[h]), 0.0)

    cos2 = cos_ref[0]
    sin2 = sin_ref[0]
    half = RET_DK // 2

    def rope(z):
        return z * cos2 + pltpu.roll(z, half, 1) * sin2

    tcol = lax.broadcasted_iota(jnp.int32, (chunk, 1), 0).astype(F32)
    q = [rope(zq_ref[0, :, sl[h]].astype(F32)) * (RET_DK ** -0.5) for h in hs]
    k = [rope(zk_ref[0, :, sl[h]].astype(F32)) for h in hs]
    v_b = [zv_ref[0, :, sl[h]].astype(BF16) for h in hs]
    st = [st_ref[h] for h in hs]
    scores = [(_dot_nt(q[h].astype(BF16), k[h].astype(BF16)) * dmask_ref[h]).astype(BF16) for h in hs]
    qx = [(q[h] * jnp.exp((tcol + 1.0) * lg[h])).astype(BF16) for h in hs]
    kz = [(k[h] * jnp.exp((chunk - 1.0 - tcol) * lg[h])).astype(BF16) for h in hs]
    o = [_dot(scores[h], v_b[h]) + _dot_nt(qx[h], st[h].astype(BF16)) for h in hs]
    for h in hs:
        st_ref[h] = st[h] * jnp.exp(chunk * lg[h]) + _dot_tn(v_b[h], kz[h])
    o = [o[h] * lax.rsqrt(jnp.mean(o[h] * o[h], axis=-1, keepdims=True) + NORM_EPS) for h in hs]
    o_ref[0] = (jnp.concatenate(o, axis=1) * _silu(zg_ref[0].astype(F32))).astype(o_ref.dtype)


def _ret_call(z3, cos2, sin2, chunk=256):
    b, t, _ = z3.shape
    chunk = min(chunk, t)
    base = RET_OFF // RET_W

    def zspec(part):
        return pl.BlockSpec((1, chunk, RET_W), lambda i, j: (i, j, base + part))

    tab = pl.BlockSpec((1, chunk, RET_DK), lambda i, j: (i, j, 0))
    return pl.pallas_call(
        functools.partial(_ret_kernel, chunk=chunk),
        grid=(b, t // chunk),
        in_specs=[zspec(0), zspec(1), zspec(2), zspec(3), tab, tab],
        out_specs=pl.BlockSpec((1, chunk, RET_W), lambda i, j: (i, j, 0)),
        out_shape=jax.ShapeDtypeStruct((b, t, RET_W), BF16),
        scratch_shapes=[pltpu.VMEM((RET_HEADS, RET_DK, RET_DK), F32), pltpu.VMEM((RET_HEADS, chunk, chunk), F32)],
        compiler_params=_cparams(("parallel", "arbitrary")),
        name="retention_mixer",
    )(z3, z3, z3, z3, cos2, sin2)


def _pair_blockdiag(y, pair_mask):
    return jnp.where(pair_mask, jnp.concatenate([y, y], axis=0), 0.0).astype(BF16)


def _pair_dot(x, y, pair_mask):
    return _dot(x.astype(BF16), _pair_blockdiag(y, pair_mask))


def _inv_unit_lower(a, eye, blk_mask, pair_mask):
    c = a[0].shape[0]
    m = range(len(a))
    a_bd = [jnp.where(blk_mask, a[i], 0.0) for i in m]
    a_off = [a[i] - a_bd[i] for i in m]
    a2 = [_pair_dot(a_bd[i], a_bd[i], pair_mask) for i in m]
    p = [eye + a_bd[i] for i in m]
    r = [_pair_dot(jnp.concatenate([p[i], a2[i]], axis=0), a2[i], pair_mask) for i in m]
    p = [p[i] + r[i][:c] for i in m]
    a4 = [r[i][c:] for i in m]
    r = [_pair_dot(jnp.concatenate([p[i], a4[i]], axis=0), a4[i], pair_mask) for i in m]
    p = [p[i] + r[i][:c] for i in m]
    a8 = [r[i][c:] for i in m]
    t_bd = [p[i] + _pair_dot(p[i], a8[i], pair_mask) for i in m]
    n = [_pair_dot(t_bd[i], a_off[i], pair_mask) for i in m]
    n2 = [_pair_dot(n[i], n[i], pair_mask) for i in m]
    z = [t_bd[i] + _pair_dot(n[i], t_bd[i], pair_mask) for i in m]
    return [z[i] + _pair_dot(n2[i], z[i], pair_mask) for i in m]


def _rwkv_kernel(z_ref, mu_ref, w0_ref, w2_ref, a0_ref, a2_ref, g2_ref, kk_ref, ka_ref, rk_ref,
                 lnw_ref, lnb_ref, seg_ref, o_ref, s_ref, prev_ref):
    c = RW_CHUNK
    nbe, tb = z_ref.shape[0], z_ref.shape[1]
    nck = tb // c
    bs = range(nbe)

    @pl.when(pl.program_id(1) == 0)
    def _():
        s_ref[...] = jnp.zeros_like(s_ref)
        prev_ref[...] = jnp.zeros_like(prev_ref)

    seg = seg_ref[...]
    rows = lax.broadcasted_iota(jnp.int32, (tb, 1), 0)
    rowb = lax.broadcasted_iota(jnp.int32, (tb, tb), 0)
    colb = lax.broadcasted_iota(jnp.int32, (tb, tb), 1)
    tri = jnp.where(colb >= (rowb // c) * c, jnp.where(rowb >= colb, 1.0, 0.0), 0.0).astype(BF16)

    def front(bi):
        z = z_ref[bi].astype(F32)
        z_prev = jnp.where(rows == 0, prev_ref[bi:bi + 1, :], pltpu.roll(z, 1, 0))
        prev_ref[bi:bi + 1, :] = z[tb - 1:tb]
        zs = z + mu_ref[...] * (z_prev - z)
        r = zs[:, 0:RW_W]
        k = zs[:, RW_W:2 * RW_W]
        v = zs[:, 2 * RW_W:3 * RW_W]
        off = 3 * RW_W
        w_lo = zs[:, off:off + RW_DECAY_LORA]
        a_lo = zs[:, off + RW_DECAY_LORA:off + RW_DECAY_LORA + RW_A_LORA]
        g_lo = zs[:, off + RW_DECAY_LORA + RW_A_LORA:]
        wx = -(w0_ref[...] + _dot_x3(jnp.tanh(w_lo), w2_ref[...]))
        softplus = jnp.maximum(wx, 0.0) + jnp.log(1.0 + jnp.exp(-jnp.abs(wx)))
        logw = -jnp.exp(-softplus - 0.5)
        a = _sigmoid(a0_ref[...] + _dot(a_lo.astype(BF16), a2_ref[...].astype(BF16)))
        g = _dot(_sigmoid(g_lo).astype(BF16), g2_ref[...].astype(BF16))
        kk = k * kk_ref[...]
        kk = kk * lax.rsqrt(jnp.maximum(_seg_sum(kk * kk, seg), 1e-24))
        k2 = k * (1.0 + (a - 1.0) * ka_ref[...])
        cw = _dot_x2_rhs(tri, logw)
        w_inv = jnp.exp(-cw)
        last = jnp.concatenate([jnp.broadcast_to(cw[(ci + 1) * c - 1:(ci + 1) * c], (c, RW_W)) for ci in range(nck)],
                               axis=0)
        w_rest = jnp.exp(last - cw)
        beta = a * kk
        return dict(alpha_t=-kk * jnp.exp(cw - logw), r_t=r * jnp.exp(cw), beta_h=beta * w_inv, k_h=k2 * w_inv,
                    beta_d=beta * w_rest, k_d=k2 * w_rest, v=v, g=g, rkk=r * k2 * rk_ref[...], cw=cw)

    fr = [front(bi) for bi in bs]

    pw = 2 * RW_N
    row2 = lax.broadcasted_iota(jnp.int32, (c, pw), 0)
    col2 = lax.broadcasted_iota(jnp.int32, (c, pw), 1) % c
    incl2 = row2 >= col2
    strict2 = row2 > col2
    blk_mask = (row2 // RW_BLK) == (col2 // RW_BLK)
    eye = (row2 == col2).astype(F32)
    rowp = lax.broadcasted_iota(jnp.int32, (pw, pw), 0)
    colp = lax.broadcasted_iota(jnp.int32, (pw, pw), 1)
    pair_mask = (rowp // RW_N) == (colp // RW_N)

    ps = range(RW_HEADS // 2)
    items = [(bi, ci, p) for bi in bs for ci in range(nck) for p in ps]
    where = {key: i for i, key in enumerate(items)}
    m = range(len(items))

    def slab(name, i):
        bi, ci, p = items[i]
        return fr[bi][name][ci * c:(ci + 1) * c, p * pw:(p + 1) * pw]

    v2 = [slab("v", i) for i in m]
    lhs = [jnp.concatenate([slab("alpha_t", i), slab("r_t", i)], axis=0).astype(BF16) for i in m]
    rhs = [jnp.concatenate([_pair_blockdiag(slab("beta_h", i), pair_mask),
                            _pair_blockdiag(slab("k_h", i), pair_mask)], axis=0) for i in m]
    big = [_dot_nt(lhs[i], rhs[i]) for i in m]
    a_ab = [jnp.where(strict2, big[i][:c, :pw], 0.0) for i in m]
    a_ak = [jnp.where(strict2, big[i][:c, pw:], 0.0) for i in m]
    a_rb = [jnp.where(incl2, big[i][c:, :pw], 0.0) for i in m]
    a_rk = [jnp.where(incl2, big[i][c:, pw:], 0.0) for i in m]
    t_inv = _inv_unit_lower(a_ab, eye, blk_mask, pair_mask)
    av = [_pair_dot(a_ak[i], v2[i], pair_mask) for i in m]
    u_const = [_pair_dot(t_inv[i], av[i], pair_mask) for i in m]
    lhs_s = [jnp.concatenate([_pair_dot(t_inv[i], slab("alpha_t", i), pair_mask).astype(BF16),
                              slab("r_t", i).astype(BF16)], axis=0) for i in m]
    a_r = [jnp.concatenate([a_rb[i], a_rk[i]], axis=1).astype(BF16) for i in m]
    bk_d = [jnp.concatenate([slab("beta_d", i), slab("k_d", i)], axis=0).astype(BF16) for i in m]
    bp = [(bi, p) for bi in bs for p in ps]
    s_cur = {(bi, p): s_ref[bi * len(ps) + p] for bi, p in bp}
    o_parts = {}
    for ci in range(nck):
        ix = {(bi, p): where[bi, ci, p] for bi, p in bp}
        sd = {q: _dot_nt(lhs_s[ix[q]], s_cur[q].astype(BF16)) for q in bp}
        u = {q: sd[q][:c] + u_const[ix[q]] for q in bp}
        uv = {q: jnp.concatenate([_pair_blockdiag(u[q], pair_mask), _pair_blockdiag(v2[ix[q]], pair_mask)], axis=0)
              for q in bp}
        for q in bp:
            o_parts[q[0], ci, q[1]] = sd[q][c:] + _dot(a_r[ix[q]], uv[q])
        uvt = {q: jnp.concatenate([u[q], v2[ix[q]]], axis=0).astype(BF16) for q in bp}
        for bi, p in bp:
            w_last = jnp.exp(fr[bi]["cw"][(ci + 1) * c - 1:(ci + 1) * c, p * pw:(p + 1) * pw])
            s_cur[bi, p] = s_cur[bi, p] * w_last + jnp.where(pair_mask, _dot_tn(uvt[bi, p], bk_d[ix[bi, p]]), 0.0)
    for bi, p in bp:
        s_ref[bi * len(ps) + p] = s_cur[bi, p]

    for bi in bs:
        o = jnp.concatenate([jnp.concatenate([o_parts[bi, ci, p] for p in ps], axis=1) for ci in range(nck)], axis=0)
        mean = _seg_sum(o, seg) * (1.0 / RW_N)
        dev = o - mean
        var = _seg_sum(dev * dev, seg) * (1.0 / RW_N)
        o = dev * lax.rsqrt(var + RW_GN_EPS) * lnw_ref[...] + lnb_ref[...]
        bonus = _seg_sum(fr[bi]["rkk"], seg) * fr[bi]["v"]
        o_ref[bi] = ((o + bonus) * fr[bi]["g"]).astype(o_ref.dtype)


def _rwkv_call(z3, mu, w0, w2, a0, a2, g2, k_k, k_a, r_k, ln_w, ln_b):
    b, t, _ = z3.shape
    c = min(RW_TB, t)
    hid = lax.broadcasted_iota(jnp.int32, (RW_W, RW_W), 0) // RW_N
    seg = (hid == hid.T).astype(BF16)

    def vec(n):
        return pl.BlockSpec((1, n), lambda i, j: (0, 0))

    def mat(m, n):
        return pl.BlockSpec((m, n), lambda i, j: (0, 0))

    nbe = RW_NB if b % RW_NB == 0 else 1
    return pl.pallas_call(
        _rwkv_kernel,
        grid=(b // nbe, t // c),
        in_specs=[
            pl.BlockSpec((nbe, c, RW_COLS), lambda i, j: (i, j, RW_OFF // RW_COLS)),
            vec(RW_COLS), vec(RW_W), mat(RW_DECAY_LORA, RW_W), vec(RW_W), mat(RW_A_LORA, RW_W),
            mat(RW_GATE_LORA, RW_W), vec(RW_W), vec(RW_W), vec(RW_W), vec(RW_W), vec(RW_W),
            mat(RW_W, RW_W),
        ],
        out_specs=pl.BlockSpec((nbe, c, RW_W), lambda i, j: (i, j, 0)),
        out_shape=jax.ShapeDtypeStruct((b, t, RW_W), BF16),
        scratch_shapes=[pltpu.VMEM((nbe * (RW_HEADS // 2), 2 * RW_N, 2 * RW_N), F32),
                        pltpu.VMEM((nbe, RW_COLS), F32)],
        compiler_params=_cparams(("parallel", "arbitrary")),
        name="rwkv7_mixer",
    )(z3, mu.reshape(1, -1), w0.reshape(1, -1), w2, a0.reshape(1, -1), a2, g2, k_k.reshape(1, -1),
      k_a.reshape(1, -1), r_k.reshape(1, -1), ln_w.reshape(1, -1), ln_b.reshape(1, -1), seg)


def _merge_kernel(ohg_ref, oret_ref, orw_ref, zg_ref, x_ref, gate_ref, bhg_ref, bret_ref, brw_ref,
                  wout_ref, o_ref):
    d = x_ref.shape[1]
    y = _sigmoid(zg_ref[:, 0:d].astype(F32)) * _dot(ohg_ref[...], bhg_ref[...])
    y = y + _sigmoid(zg_ref[:, d:2 * d].astype(F32)) * _dot(oret_ref[...], bret_ref[...])
    y = y + _sigmoid(zg_ref[:, 2 * d:3 * d].astype(F32)) * _dot(orw_ref[...], brw_ref[...])
    o_ref[...] = x_ref[...] + gate_ref[0] * _dot(y.astype(BF16), wout_ref[...])


def _merge_call(o_hg, o_ret, o_rw, z2, x2, mod3, br_hg, br_ret, br_rw, w_out, seq, gate_blk, tm=1024):
    n, d = x2.shape
    tpb = seq // tm

    def rows(w):
        return pl.BlockSpec((tm, w), lambda i: (i, 0))

    def full(m, k):
        return pl.BlockSpec((m, k), lambda i: (0, 0))

    return pl.pallas_call(
        _merge_kernel,
        grid=(n // tm,),
        in_specs=[
            rows(HG_W), rows(RET_W), rows(RW_W), rows(3 * d), rows(d),
            pl.BlockSpec((1, 1, d), lambda i: (i // tpb, 0, gate_blk)),
            full(HG_W, d), full(RET_W, d), full(RW_W, d), full(d, d),
        ],
        out_specs=rows(d),
        out_shape=jax.ShapeDtypeStruct((n, d), F32),
        compiler_params=_cparams(("parallel",)),
        name="merge_outproj",
    )(o_hg, o_ret, o_rw, z2, x2, mod3, br_hg, br_ret, br_rw, w_out)


def _pack_bf16_pairs(x):
    w = x.shape[1] // 2
    hi = pltpu.bitcast(x[:, :w].astype(BF16).astype(F32), jnp.uint32)
    lo = pltpu.bitcast(x[:, w:].astype(BF16).astype(F32), jnp.uint32)
    return pltpu.bitcast(hi | lax.shift_right_logical(lo, jnp.uint32(16)), jnp.int32)


def _unpack_bf16_pairs(p):
    u = pltpu.bitcast(p, jnp.uint32)
    hi = pltpu.bitcast(u & jnp.uint32(0xFFFF0000), F32)
    lo = pltpu.bitcast(lax.shift_left(u, jnp.uint32(16)), F32)
    return jnp.concatenate([hi, lo], axis=1)


def _route_kernel(x_ref, g_ref, scale_ref, shift_ref, rc_ref, hp_ref, eid_ref, wts_ref, cnt_ref):
    @pl.when(pl.program_id(0) == 0)
    def _():
        cnt_ref[...] = jnp.zeros_like(cnt_ref)

    h = _rms_mod(x_ref[...], g_ref[...], scale_ref[0], shift_ref[0])
    hp_ref[...] = _pack_bf16_pairs(h)
    tm = h.shape[0]
    lane = lax.broadcasted_iota(jnp.int32, (tm, LANES), 1)
    neg = -jnp.inf
    logits = _dot_x3(h, rc_ref[...])
    gl = jnp.where(lane < N_GROUPS, logits, neg)
    gmax = jnp.max(gl, axis=-1, keepdims=True)
    gidx = jnp.min(jnp.where(gl == gmax, lane, LANES), axis=-1, keepdims=True)
    gw = 1.0 / jnp.sum(jnp.exp(gl - gmax), axis=-1, keepdims=True)
    lo = N_GROUPS + gidx * EXPERTS_PER_GROUP
    el = jnp.where(lane >= lo, jnp.where(lane < lo + EXPERTS_PER_GROUP, logits, neg), neg)
    m1 = jnp.max(el, axis=-1, keepdims=True)
    l1 = jnp.min(jnp.where(el == m1, lane, LANES), axis=-1, keepdims=True)
    el2 = jnp.where(lane == l1, neg, el)
    m2 = jnp.max(el2, axis=-1, keepdims=True)
    l2 = jnp.min(jnp.where(el2 == m2, lane, LANES), axis=-1, keepdims=True)
    i1 = l1 - N_GROUPS
    i2 = l2 - N_GROUPS
    e2 = jnp.exp(m2 - m1)
    p1 = 1.0 / (1.0 + e2)
    p2 = e2 * p1
    oh1 = jnp.where(lane == i1, 1.0, 0.0)
    oh2 = jnp.where(lane == i2, 1.0, 0.0)
    row = lax.broadcasted_iota(jnp.int32, (tm, tm), 0)
    col = lax.broadcasted_iota(jnp.int32, (tm, tm), 1)
    earlier = jnp.where(row > col, 1.0, 0.0).astype(BF16)
    before = _dot(earlier, jnp.concatenate([oh1, oh2], axis=1).astype(BF16))
    tot1 = jnp.sum(oh1, axis=0, keepdims=True)
    carry = cnt_ref[...]
    r1 = jnp.sum(oh1 * (before[:, :LANES] + carry), axis=-1, keepdims=True).astype(jnp.int32)
    r2 = jnp.sum(oh2 * (before[:, LANES:] + (carry + tot1)), axis=-1, keepdims=True).astype(jnp.int32)
    cnt_ref[...] = carry + tot1 + jnp.sum(oh2, axis=0, keepdims=True)
    eid_ref[...] = jnp.where(lane == 0, i1, jnp.where(lane == 1, i2, jnp.where(lane == 2, r1,
                                                                             jnp.where(lane == 3, r2, 0))))
    wts_ref[...] = jnp.where(lane == 0, gw * p1, jnp.where(lane == 1, gw * p2, 0.0))


def _route_call(x2, gain, mod3, router_g, router_e, seq, scale_blk, shift_blk, tm=1024):
    n, d = x2.shape
    tpb = seq // tm
    rc = jnp.pad(jnp.concatenate([router_g, router_e], axis=1), ((0, 0), (0, LANES - N_GROUPS - N_EXPERTS)))
    return pl.pallas_call(
        _route_kernel,
        grid=(n // tm,),
        in_specs=[
            pl.BlockSpec((tm, d), lambda i: (i, 0)),
            pl.BlockSpec((1, d), lambda i: (0, 0)),
            pl.BlockSpec((1, 1, d), lambda i: (i // tpb, 0, scale_blk)),
            pl.BlockSpec((1, 1, d), lambda i: (i // tpb, 0, shift_blk)),
            pl.BlockSpec((d, LANES), lambda i: (0, 0)),
        ],
        out_specs=[pl.BlockSpec((tm, d // 2), lambda i: (i, 0)), pl.BlockSpec((tm, LANES), lambda i: (i, 0)),
                   pl.BlockSpec((tm, LANES), lambda i: (i, 0)), pl.BlockSpec((1, LANES), lambda i: (0, 0))],
        out_shape=[jax.ShapeDtypeStruct((n, d // 2), jnp.int32), jax.ShapeDtypeStruct((n, LANES), jnp.int32),
                   jax.ShapeDtypeStruct((n, LANES), F32), jax.ShapeDtypeStruct((1, LANES), F32)],
        compiler_params=_cparams(("arbitrary",)),
        name="moe_route",
    )(x2, gain.reshape(1, d), mod3, mod3, rc)


SC_CORES = 2
SC_SUBCORES = 16
SC_WORKERS = SC_CORES * SC_SUBCORES
SC_ROWS = 32
SC_STREAMS = 4


def _sc_gather(table, idx):
    m = idx.shape[0]
    w = table.shape[1]
    per_worker = m // SC_WORKERS
    steps = per_worker // SC_ROWS
    assert per_worker * SC_WORKERS == m and steps * SC_ROWS == per_worker and steps % SC_STREAMS == 0
    mesh = plsc.VectorSubcoreMesh(core_axis_name="c", subcore_axis_name="s")
    ks = range(SC_STREAMS)

    def body(table_hbm, idx_hbm, out_hbm, idx_v, *rest):
        bufs, g_sems, w_sems = rest[:SC_STREAMS], rest[SC_STREAMS:2 * SC_STREAMS], rest[2 * SC_STREAMS:]
        wid = lax.axis_index("s") * SC_CORES + lax.axis_index("c")
        pltpu.sync_copy(idx_hbm.at[wid], idx_v)

        @pl.loop(0, steps, step=SC_STREAMS)
        def _(j):
            row0 = wid * per_worker + j * SC_ROWS
            gathers = [pltpu.async_copy(table_hbm.at[idx_v.at[j + q]], bufs[q], g_sems[q]) for q in ks]
            writes = []
            for q in ks:
                gathers[q].wait()
                writes.append(pltpu.async_copy(bufs[q], out_hbm.at[pl.ds(row0 + q * SC_ROWS, SC_ROWS)], w_sems[q]))
            for q in ks:
                writes[q].wait()

    return pl.kernel(
        body,
        out_type=jax.ShapeDtypeStruct((m, w), table.dtype),
        mesh=mesh,
        scratch_types=[pltpu.VMEM((steps, SC_ROWS), jnp.int32)] + [pltpu.VMEM((SC_ROWS, w), table.dtype)] * SC_STREAMS
        + [pltpu.SemaphoreType.DMA] * (2 * SC_STREAMS),
        name="sc_row_gather",
    )(table, idx.reshape(SC_WORKERS, steps, SC_ROWS))


def _sc_scatter2(rows, idx0, idx1, p):
    n, w = rows.shape
    per_worker = n // SC_WORKERS
    steps = per_worker // SC_ROWS
    assert per_worker * SC_WORKERS == n and steps * SC_ROWS == per_worker and steps % SC_STREAMS == 0
    mesh = plsc.VectorSubcoreMesh(core_axis_name="c", subcore_axis_name="s")
    ks = range(SC_STREAMS)

    def body(rows_hbm, i0_hbm, i1_hbm, out_hbm, i0_v, i1_v, *rest):
        bufs, r_sems = rest[:SC_STREAMS], rest[SC_STREAMS:2 * SC_STREAMS]
        s0_sems, s1_sems = rest[2 * SC_STREAMS:3 * SC_STREAMS], rest[3 * SC_STREAMS:]
        wid = lax.axis_index("s") * SC_CORES + lax.axis_index("c")
        pltpu.sync_copy(i0_hbm.at[wid], i0_v)
        pltpu.sync_copy(i1_hbm.at[wid], i1_v)

        @pl.loop(0, steps, step=SC_STREAMS)
        def _(j):
            row0 = wid * per_worker + j * SC_ROWS
            reads = [pltpu.async_copy(rows_hbm.at[pl.ds(row0 + q * SC_ROWS, SC_ROWS)], bufs[q], r_sems[q]) for q in ks]
            writes = []
            for q in ks:
                reads[q].wait()
                writes.append(pltpu.async_copy(bufs[q], out_hbm.at[i0_v.at[j + q]], s0_sems[q]))
                writes.append(pltpu.async_copy(bufs[q], out_hbm.at[i1_v.at[j + q]], s1_sems[q]))
            for wr in writes:
                wr.wait()

    index_block = pltpu.VMEM((steps, SC_ROWS), jnp.int32)
    return pl.kernel(
        body,
        out_type=jax.ShapeDtypeStruct((p, w), rows.dtype),
        mesh=mesh,
        scratch_types=[index_block, index_block] + [pltpu.VMEM((SC_ROWS, w), rows.dtype)] * SC_STREAMS
        + [pltpu.SemaphoreType.DMA] * (3 * SC_STREAMS),
        name="sc_row_scatter",
    )(rows, idx0.reshape(SC_WORKERS, steps, SC_ROWS), idx1.reshape(SC_WORKERS, steps, SC_ROWS))


MOE_TM = 512


def _gexperts_kernel(te_ref, tv_ref, nu_ref, xs_ref, w1_ref, w3_ref, w2_ref, ys_ref, w1b_ref, w3b_ref, w2b_ref):
    i = pl.program_id(0)

    @pl.when((i == 0) | (te_ref[i] != te_ref[jnp.maximum(i - 1, 0)]))
    def _():
        w1b_ref[...] = w1_ref[0].astype(BF16)
        w3b_ref[...] = w3_ref[0].astype(BF16)
        w2b_ref[...] = w2_ref[0].astype(BF16)

    @pl.when(i < nu_ref[0])
    def _():
        rid = lax.broadcasted_iota(jnp.int32, xs_ref.shape, 0)
        xb = _unpack_bf16_pairs(jnp.where(rid < tv_ref[i], xs_ref[...], 0)).astype(BF16)
        act = (_silu(_dot(xb, w1b_ref[...])) * _dot(xb, w3b_ref[...])).astype(BF16)
        ys_ref[...] = _pack_bf16_pairs(_dot(act, w2b_ref[...]))


def _gexperts_call(xs, tile_expert, tile_valid, n_used, w1, w3, w2):
    p, half = xs.shape
    ne, d, de = w1.shape
    nt = p // MOE_TM

    def rows(i, te, tv, nu):
        return (jnp.minimum(i, nu[0] - 1), 0)

    def wsel(i, te, tv, nu):
        return (te[i], 0, 0)

    return pl.pallas_call(
        _gexperts_kernel,
        grid_spec=pltpu.PrefetchScalarGridSpec(
            num_scalar_prefetch=3,
            grid=(nt,),
            in_specs=[
                pl.BlockSpec((MOE_TM, half), rows),
                pl.BlockSpec((1, d, de), wsel),
                pl.BlockSpec((1, d, de), wsel),
                pl.BlockSpec((1, de, d), wsel),
            ],
            out_specs=pl.BlockSpec((MOE_TM, half), rows),
            scratch_shapes=[pltpu.VMEM((d, de), BF16), pltpu.VMEM((d, de), BF16), pltpu.VMEM((de, d), BF16)],
        ),
        out_shape=jax.ShapeDtypeStruct((p, half), jnp.int32),
        compiler_params=_cparams(("arbitrary",)),
        name="moe_experts",
    )(tile_expert, tile_valid, n_used, xs, w1, w3, w2)


def _combine_kernel(y0_ref, y1_ref, wts_ref, x_ref, gate_ref, fg_ref, *rest, final_norm):
    o_ref = rest[-1]
    wts = wts_ref[...]
    moe = wts[:, 0:1] * _unpack_bf16_pairs(y0_ref[...]) + wts[:, 1:2] * _unpack_bf16_pairs(y1_ref[...])
    xn = x_ref[...] + gate_ref[0] * moe
    if final_norm:
        xn = xn * lax.rsqrt(jnp.mean(xn * xn, axis=-1, keepdims=True) + NORM_EPS) * fg_ref[...]
    o_ref[...] = xn


def _combine_call(yg, wts, x2, mod3, final_g, seq, gate_blk, final_norm, part, parts, partial=None, tm=1024):
    n, d = x2.shape
    tpb = seq // tm
    steps = n // parts // tm
    off = part * steps
    in_specs = [
        pl.BlockSpec((tm, d // 2), lambda i: (i, 0)),
        pl.BlockSpec((tm, d // 2), lambda i: (i + steps, 0)),
        pl.BlockSpec((tm, LANES), lambda i: (i + off, 0)),
        pl.BlockSpec((tm, d), lambda i: (i + off, 0)),
        pl.BlockSpec((1, 1, d), lambda i: ((i + off) // tpb, 0, gate_blk)),
        pl.BlockSpec((1, d), lambda i: (0, 0)),
    ]
    args = [yg, yg, wts, x2, mod3, final_g.reshape(1, d)]
    aliases = {}
    if partial is not None:
        in_specs.append(pl.BlockSpec(memory_space=pl.ANY))
        args.append(partial)
        aliases = {len(args) - 1: 0}
    return pl.pallas_call(
        functools.partial(_combine_kernel, final_norm=final_norm),
        grid=(steps,),
        in_specs=in_specs,
        out_specs=pl.BlockSpec((tm, d), lambda i: (i + off, 0)),
        out_shape=jax.ShapeDtypeStruct((n, d), F32),
        input_output_aliases=aliases,
        compiler_params=_cparams(("parallel",)),
        name="moe_combine",
    )(*args)


def _pos_kernel(eid_ref, ts_ref, p0_ref, p1_ref):
    eid = eid_ref[...]
    tm = eid.shape[0]
    lane = lax.broadcasted_iota(jnp.int32, (tm, LANES), 1)
    sub = lax.broadcasted_iota(jnp.int32, (tm, LANES), 0) % LANES
    for slot, out_ref in ((0, p0_ref), (1, p1_ref)):
        first_row = jnp.sum(jnp.where(lane == eid[:, slot:slot + 1], ts_ref[...], 0), axis=-1, keepdims=True)
        pos = first_row + eid[:, slot + 2:slot + 3]
        out_ref[...] = jnp.sum(jnp.where(lane == sub, pos, 0).reshape(tm // LANES, LANES, LANES), axis=1)


def _pos_call(eid, first_rows, tm=4096):
    n = eid.shape[0]
    tm = min(tm, n)
    out = jax.ShapeDtypeStruct((n // LANES, LANES), jnp.int32)
    p0, p1 = pl.pallas_call(
        _pos_kernel,
        grid=(n // tm,),
        in_specs=[pl.BlockSpec((tm, LANES), lambda i: (i, 0)), pl.BlockSpec((1, LANES), lambda i: (0, 0))],
        out_specs=[pl.BlockSpec((tm // LANES, LANES), lambda i: (i, 0))] * 2,
        out_shape=[out, out],
        compiler_params=_cparams(("parallel",)),
        name="moe_positions",
    )(eid, first_rows)
    return p0.reshape(n), p1.reshape(n)


def _moe_plan(eid, counts_f):
    n = eid.shape[0]
    nt = (2 * n) // MOE_TM + N_EXPERTS
    counts = counts_f[0, :N_EXPERTS].astype(jnp.int32)
    tiles = (counts + MOE_TM - 1) // MOE_TM
    tile_end = jnp.cumsum(tiles)
    tile_start = tile_end - tiles
    n_used = tile_end[-1:]
    tile_iota = jnp.arange(nt, dtype=jnp.int32)
    tile_expert = jnp.sum(jnp.minimum(tile_iota, n_used - 1)[:, None] >= tile_end[None, :], axis=1, dtype=jnp.int32)
    own = tile_expert[:, None] == jnp.arange(N_EXPERTS, dtype=jnp.int32)[None, :]
    count_t = jnp.sum(jnp.where(own, counts[None, :], 0), axis=1)
    start_t = jnp.sum(jnp.where(own, tile_start[None, :], 0), axis=1)
    tile_valid = jnp.clip(count_t - (tile_iota - start_t) * MOE_TM, 0, MOE_TM)
    first_rows = jnp.pad(tile_start * MOE_TM, (0, LANES - N_EXPERTS)).reshape(1, LANES)
    pos0, pos1 = _pos_call(eid, first_rows)
    return pos0, pos1, tile_expert, tile_valid, n_used


def kernel(x, c, positions, ada_w, ada_b, norm1_g, norm2_g, w_in, hg_lb_table, hg_norm_w, rw_mu, rw_w0, rw_w2,
           rw_a0, rw_a2, rw_g2, rw_k_k, rw_k_a, rw_r_k, rw_ln_w, rw_ln_b, br_hg, br_ret, br_rw, w_out,
           router_g, router_e, moe_w1, moe_w3, moe_w2, final_g):
    b, t, d = x.shape
    depth = ada_w.shape[0]
    n = b * t
    assert w_in.shape[2] == IN_COLS and d == 1024

    lb_p = jax.nn.softmax(hg_lb_table.astype(F32), axis=0)
    lower_bounds = jnp.cumsum(lb_p, axis=0) - lb_p[0]

    mod = _mod_call(c, ada_w, ada_b)
    cos2, sin2 = _rope_call(positions, RET_DK)
    x2 = x.reshape(n, d)
    for l in range(depth):
        mod3 = mod[l].reshape(b, 1, 6 * d)
        z2 = _inproj_call(x2, norm1_g[l], mod3, _wprep_call(w_in, l), t, scale_blk=1, shift_blk=0)
        z3 = z2.reshape(b, t, IN_COLS)
        o_hg = _hgrn2_call(z3, lower_bounds[l], hg_norm_w[l])
        o_ret = _ret_call(z3, cos2, sin2)
        o_rw = _rwkv_call(z3, rw_mu[l], rw_w0[l], rw_w2[l], rw_a0[l], rw_a2[l], rw_g2[l], rw_k_k[l],
                          rw_k_a[l], rw_r_k[l], rw_ln_w[l], rw_ln_b[l])
        x2 = _merge_call(o_hg.reshape(n, HG_W), o_ret.reshape(n, RET_W), o_rw.reshape(n, RW_W), z2, x2, mod3,
                         br_hg[l].astype(BF16), br_ret[l].astype(BF16), br_rw[l].astype(BF16),
                         w_out[l].astype(BF16), t, gate_blk=2)
        hp, eid, wts, counts = _route_call(x2, norm2_g[l], mod3, router_g[l], router_e[l], t, scale_blk=4,
                                           shift_blk=3)
        pos0, pos1, tile_expert, tile_valid, n_used = _moe_plan(eid, counts)
        xs = _sc_scatter2(hp, pos0, pos1, (2 * n // MOE_TM + N_EXPERTS) * MOE_TM)
        ys = _gexperts_call(xs, tile_expert + l * N_EXPERTS, tile_valid, n_used,
                            moe_w1.reshape((-1,) + moe_w1.shape[2:]), moe_w3.reshape((-1,) + moe_w3.shape[2:]),
                            moe_w2.reshape((-1,) + moe_w2.shape[2:]))
        nh = n // 2
        x_in, x2 = x2, None
        for part in range(2):
            rows = slice(part * nh, (part + 1) * nh)
            yg = _sc_gather(ys, jnp.concatenate([pos0[rows], pos1[rows]]))
            x2 = _combine_call(yg, wts, x_in, mod3, final_g, t, gate_blk=5, final_norm=(l == depth - 1),
                               part=part, parts=2, partial=x2)
    return x2.reshape(b, t, d)
```

```python
import functools

import jax
import jax.numpy as jnp
from jax import lax
from jax.experimental import pallas as pl
from jax.experimental.pallas import tpu as pltpu
from jax.experimental.pallas import tpu_sc as plsc

F32 = jnp.float32
BF16 = jnp.bfloat16
HIGHEST = lax.Precision.HIGHEST

HG_HEADS = 4
HG_DK = 128
HG_W = HG_HEADS * HG_DK
RET_HEADS = 4
RET_DK = 128
RET_W = RET_HEADS * RET_DK
RW_HEADS = 8
RW_N = 64
RW_W = RW_HEADS * RW_N
RW_DECAY_LORA = 64
RW_A_LORA = 64
RW_GATE_LORA = 128
RW_COLS = 3 * RW_W + RW_DECAY_LORA + RW_A_LORA + RW_GATE_LORA
RW_GN_EPS = 64e-5
N_GROUPS = 4
EXPERTS_PER_GROUP = 8
N_EXPERTS = N_GROUPS * EXPERTS_PER_GROUP
ROPE_THETA = 10000.0
NORM_EPS = 1e-6

LANES = 128
LOG2E = 1.4426950408889634
VMEM_LIMIT = 56 * 1024 * 1024

GATE_OFF = 0
HG_OFF = 3 * 1024
RET_OFF = HG_OFF + 4 * HG_W
RW_OFF = RET_OFF + 4 * RET_W
IN_COLS = RW_OFF + RW_COLS

HG_CHUNK = 64
HG_SUB = 16
HG_NB = 2
HG_SAFE_SPAN = 60.0
RW_CHUNK = 64
RW_BLK = 16
RW_NB = 4
RW_TB = 256
Z_DTYPE = BF16


def _cparams(sem):
    return pltpu.CompilerParams(dimension_semantics=sem, vmem_limit_bytes=VMEM_LIMIT)


def _dot(a, b, precision=None):
    return jnp.dot(a, b, preferred_element_type=F32, precision=precision)


def _dot_nt(a, b, precision=None):
    return lax.dot_general(a, b, (((1,), (1,)), ((), ())), preferred_element_type=F32, precision=precision)


def _dot_tn(a, b, precision=None):
    return lax.dot_general(a, b, (((0,), (0,)), ((), ())), preferred_element_type=F32, precision=precision)


def _split_bf16(x):
    hi = x.astype(BF16)
    return hi, (x - hi.astype(F32)).astype(BF16)


def _dot_x3(a, b):
    ah, al = _split_bf16(a)
    bh, bl = _split_bf16(b)
    return _dot(ah, bh) + _dot(ah, bl) + _dot(al, bh)


def _seg_sum(x, seg):
    return _dot(x.astype(BF16), seg)


def _dot_x2_rhs(a_exact, b):
    bh, bl = _split_bf16(b)
    return _dot(a_exact, bh) + _dot(a_exact, bl)


def _sigmoid(x):
    return 0.5 * jnp.tanh(0.5 * x) + 0.5


def _silu(x):
    return x * _sigmoid(x)


def _rms_mod(x, gain, scale, shift):
    y = x * lax.rsqrt(jnp.mean(x * x, axis=-1, keepdims=True) + NORM_EPS)
    return (y * gain) * (1.0 + scale) + shift


def _mod_kernel(c_ref, w_ref, b_ref, o_ref):
    c = c_ref[...]
    o_ref[0] = _dot(_silu(c), w_ref[0], HIGHEST) + b_ref[0]


def _mod_call(c, ada_w, ada_b):
    depth, d, d6 = ada_w.shape
    b = c.shape[0]
    nblk = d6 // d
    return pl.pallas_call(
        _mod_kernel,
        grid=(depth, nblk),
        in_specs=[
            pl.BlockSpec((b, d), lambda l, j: (0, 0)),
            pl.BlockSpec((1, d, d), lambda l, j: (l, 0, j)),
            pl.BlockSpec((1, 1, d), lambda l, j: (l, 0, j)),
        ],
        out_specs=pl.BlockSpec((1, b, d), lambda l, j: (l, 0, j)),
        out_shape=jax.ShapeDtypeStruct((depth, b, d6), F32),
        compiler_params=_cparams(("parallel", "parallel")),
        name="adaln_mod",
    )(c, ada_w, ada_b.reshape(depth, 1, d6))


def _rope_kernel(pos_ref, freq_ref, sign_ref, cos_ref, sin_ref):
    ang = pos_ref[0].astype(F32) * freq_ref[...]
    cos_ref[0] = jnp.cos(ang)
    sin_ref[0] = jnp.sin(ang) * sign_ref[...]


def _rope_call(positions, d):
    b, t = positions.shape
    tb = min(t, 512)
    inv_freq = ROPE_THETA ** (-jnp.arange(0, d, 2, dtype=F32) / d)
    freq2 = jnp.concatenate([inv_freq, inv_freq]).reshape(1, d)
    sign2 = jnp.concatenate([-jnp.ones((d // 2,), F32), jnp.ones((d // 2,), F32)]).reshape(1, d)
    out = jax.ShapeDtypeStruct((b, t, d), F32)
    return pl.pallas_call(
        _rope_kernel,
        grid=(b, t // tb),
        in_specs=[
            pl.BlockSpec((1, tb, 1), lambda i, j: (i, j, 0)),
            pl.BlockSpec((1, d), lambda i, j: (0, 0)),
            pl.BlockSpec((1, d), lambda i, j: (0, 0)),
        ],
        out_specs=[pl.BlockSpec((1, tb, d), lambda i, j: (i, j, 0))] * 2,
        out_shape=[out, out],
        compiler_params=_cparams(("parallel", "parallel")),
        name="rope_tables",
    )(positions.reshape(b, t, 1), freq2, sign2)


W_BLK = 256


def _wprep_kernel(w_ref, o_ref):
    o_ref[...] = w_ref[...].astype(o_ref.dtype)


def _wprep_call(w_in, layer):
    _, d, cols = w_in.shape
    nblk = cols // W_BLK
    first = (cols - 3 * d) // W_BLK
    return pl.pallas_call(
        _wprep_kernel,
        grid=(nblk,),
        in_specs=[pl.BlockSpec((1, d, W_BLK), lambda j: (layer, 0, (j + first) % nblk))],
        out_specs=pl.BlockSpec((1, d, W_BLK), lambda j: (0, 0, j)),
        out_shape=jax.ShapeDtypeStruct((1, d, cols), BF16),
        compiler_params=_cparams(("parallel",)),
        name="w_in_layout",
    )(w_in)


def _inproj_kernel(x_ref, g_ref, scale_ref, shift_ref, w_ref, o_ref, h_ref):
    @pl.when(pl.program_id(1) == 0)
    def _():
        h = _rms_mod(x_ref[...], g_ref[...], scale_ref[0], shift_ref[0])
        h_ref[...] = h.astype(BF16)

    o_ref[...] = _dot(h_ref[...], w_ref[0]).astype(o_ref.dtype)


def _inproj_call(x2, gain, mod3, w_bf16, seq, scale_blk, shift_blk, tm=2048, tn=1792):
    n, d = x2.shape
    cols = w_bf16.shape[2]
    tpb = seq // tm
    return pl.pallas_call(
        _inproj_kernel,
        grid=(n // tm, cols // tn),
        in_specs=[
            pl.BlockSpec((tm, d), lambda i, j: (i, 0)),
            pl.BlockSpec((1, d), lambda i, j: (0, 0)),
            pl.BlockSpec((1, 1, d), lambda i, j: (i // tpb, 0, scale_blk)),
            pl.BlockSpec((1, 1, d), lambda i, j: (i // tpb, 0, shift_blk)),
            pl.BlockSpec((1, d, tn), lambda i, j: (0, 0, j)),
        ],
        out_specs=pl.BlockSpec((tm, tn), lambda i, j: (i, j)),
        out_shape=jax.ShapeDtypeStruct((n, cols), Z_DTYPE),
        scratch_shapes=[pltpu.VMEM((tm, d), BF16)],
        compiler_params=_cparams(("parallel", "arbitrary")),
        name="norm_inproj",
    )(x2, gain.reshape(1, d), mod3, mod3, w_bf16)


def _hgrn2_block(zs, lbs, nw, sts, factored):
    hs = range(len(zs))
    tb = zs[0][0].shape[0]
    c, sub = HG_CHUNK, HG_SUB
    nc, ns, nb = tb // c, c // sub, tb // sub
    f = [lbs[h] + (1.0 - lbs[h]) * _sigmoid(zs[h][1]) for h in hs]
    logf = [jnp.log(jnp.maximum(f[h], 1e-30)) for h in hs]
    q = [_silu(zs[h][0]) * (HG_DK ** -0.5) for h in hs]
    k = [1.0 - f[h] for h in hs]
    v = [zs[h][2] for h in hs]
    v_b = [v[h].astype(BF16) for h in hs]
    row = lax.broadcasted_iota(jnp.int32, (tb, tb), 0)
    col = lax.broadcasted_iota(jnp.int32, (tb, tb), 1)
    tri = jnp.where(col >= (row // c) * c, jnp.where(row >= col, 1.0, 0.0), 0.0).astype(BF16)
    cum = [_dot_x2_rhs(tri, logf[h]) for h in hs]
    cum3 = [cum[h].reshape(nb, sub, HG_DK) for h in hs]
    ref3 = [cum3[h][:, 0:1, :] - logf[h].reshape(nb, sub, HG_DK)[:, 0:1, :] for h in hs]
    span = functools.reduce(jnp.maximum, [jnp.max(ref3[h] - cum3[h][:, sub - 1:sub, :]) for h in hs])
    qe = [(q[h] * jnp.exp(cum[h])).astype(BF16) for h in hs]

    offd = [(h, ci * c, ci * c + sub * i) for h in hs for ci in range(nc) for i in range(1, ns)]
    base = [cum[h][lo - 1:lo] for h, _, lo in offd]
    qt = [(q[h][lo:lo + sub] * jnp.exp(cum[h][lo:lo + sub] - base[j])).astype(BF16)
          for j, (h, _, lo) in enumerate(offd)]
    kt = [(k[h][r0:lo] * jnp.exp(base[j] - cum[h][r0:lo])).astype(BF16) for j, (h, r0, lo) in enumerate(offd)]
    a = [_dot_nt(qt[j], kt[j]).astype(BF16) for j in range(len(offd))]
    av = {(h, lo): _dot(a[j], v_b[h][r0:lo]) for j, (h, r0, lo) in enumerate(offd)}

    cs = [slice(ci * c, (ci + 1) * c) for ci in range(nc)]
    hc = [(h, ci) for h in hs for ci in range(nc)]
    last = {(h, ci): cum[h][(ci + 1) * c - 1:(ci + 1) * c] for h, ci in hc}
    kd = {(h, ci): (k[h][cs[ci]] * jnp.exp(last[h, ci] - cum[h][cs[ci]])).astype(BF16) for h, ci in hc}
    inc = {(h, ci): _dot_tn(v_b[h][cs[ci]], kd[h, ci]) for h, ci in hc}
    s_in = {(h, 0): sts[h] for h in hs}
    for ci in range(nc):
        for h in hs:
            s_in[h, ci + 1] = s_in[h, ci] * jnp.exp(last[h, ci]) + inc[h, ci]
    o_inter = {(h, ci): _dot_nt(qe[h][cs[ci]], s_in[h, ci].astype(BF16)) for h, ci in hc}

    if factored:
        qf = [(q[h] * jnp.exp(cum3[h] - ref3[h]).reshape(tb, HG_DK)).astype(BF16) for h in hs]
        kf = [(k[h] * jnp.exp(ref3[h] - cum3[h]).reshape(tb, HG_DK)).astype(BF16) for h in hs]
        rc = lax.broadcasted_iota(jnp.int32, (c, c), 0)
        cc = lax.broadcasted_iota(jnp.int32, (c, c), 1)
        keep = (rc >= cc) & (rc // sub == cc // sub)
        a_d = {(h, ci): jnp.where(keep, _dot_nt(qf[h][cs[ci]], kf[h][cs[ci]]), 0.0).astype(BF16) for h, ci in hc}
        dg = {(h, ci): _dot(a_d[h, ci], v_b[h][cs[ci]]) for h, ci in hc}
        diag = [jnp.concatenate([dg[h, ci] for ci in range(nc)], axis=0) for h in hs]
    else:
        gb = 4
        trow = lax.broadcasted_iota(jnp.int32, (gb, sub, HG_DK), 1)
        diag = []
        for h in hs:
            c2 = cum[h] * LOG2E
            ks2 = c2 - jnp.log2(k[h])
            parts = []
            for g0 in range(0, nb, gb):
                rws = slice(g0 * sub, (g0 + gb) * sub)
                c23, ks23, q3, v3 = (x[rws].reshape(gb, sub, HG_DK) for x in (c2, ks2, q[h], v[h]))
                acc = jnp.zeros((gb, sub, HG_DK), F32)
                for s in range(sub):
                    e = jnp.exp2(jnp.where(trow >= s, c23 - ks23[:, s:s + 1, :], -jnp.inf))
                    a_col = jnp.sum(q3 * e, axis=-1, keepdims=True)
                    acc = acc + a_col * v3[:, s:s + 1, :]
                parts.append(acc.reshape(gb * sub, HG_DK))
            diag.append(jnp.concatenate(parts, axis=0))

    outs = []
    for h in hs:
        pieces = []
        for ci in range(nc):
            for i in range(ns):
                lo = ci * c + sub * i
                piece = o_inter[h, ci][sub * i:sub * (i + 1)] + diag[h][lo:lo + sub]
                pieces.append(piece + av[h, lo] if i > 0 else piece)
        o = jnp.concatenate(pieces, axis=0)
        o = o * lax.rsqrt(jnp.mean(o * o, axis=-1, keepdims=True) + NORM_EPS)
        outs.append(o * nw * _silu(zs[h][3]))
    return outs, [s_in[h, nc] for h in hs], span


def _hgrn2_kernel(zq_ref, zf_ref, zi_ref, zg_ref, lb_ref, nw_ref, o_ref, st_ref):
    @pl.when(pl.program_id(1) == 0)
    def _():
        st_ref[...] = jnp.zeros_like(st_ref)

    nbe = zq_ref.shape[0]
    hs = range(HG_HEADS)
    sl = [slice(h * HG_DK, (h + 1) * HG_DK) for h in hs]
    items = [(bi, h) for bi in range(nbe) for h in hs]

    def run(factored):
        zs = [tuple(r[bi, :, sl[h]].astype(F32) for r in (zq_ref, zf_ref, zi_ref, zg_ref)) for bi, h in items]
        outs, sts, span = _hgrn2_block(zs, [lb_ref[:, sl[h]] for _, h in items], nw_ref[...],
                                       [st_ref[i] for i in range(len(items))], factored)
        return outs, sts, span

    def put(outs, sts):
        for i in range(len(items)):
            st_ref[i] = sts[i]
        for bi in range(nbe):
            o_ref[bi] = jnp.concatenate(outs[bi * HG_HEADS:(bi + 1) * HG_HEADS], axis=1).astype(o_ref.dtype)

    st_old = [st_ref[i] for i in range(len(items))]
    outs, st_new, span = run(True)
    put(outs, st_new)

    @pl.when(span > HG_SAFE_SPAN)
    def _():
        for i in range(len(items)):
            st_ref[i] = st_old[i]
        outs2, st2, _ = run(False)
        put(outs2, st2)


def _hgrn2_call(z3, lower_bound, norm_w, tb=256):
    b, t, _ = z3.shape
    tb = min(tb, t)
    base = HG_OFF // HG_W

    nbe = HG_NB if b % HG_NB == 0 else 1

    def zspec(part):
        return pl.BlockSpec((nbe, tb, HG_W), lambda i, j: (i, j, base + part))

    return pl.pallas_call(
        _hgrn2_kernel,
        grid=(b // nbe, t // tb),
        in_specs=[
            zspec(0), zspec(1), zspec(2), zspec(3),
            pl.BlockSpec((1, HG_W), lambda i, j: (0, 0)),
            pl.BlockSpec((1, LANES), lambda i, j: (0, 0)),
        ],
        out_specs=pl.BlockSpec((nbe, tb, HG_W), lambda i, j: (i, j, 0)),
        out_shape=jax.ShapeDtypeStruct((b, t, HG_W), BF16),
        scratch_shapes=[pltpu.VMEM((nbe * HG_HEADS, HG_DK, HG_DK), F32)],
        compiler_params=_cparams(("parallel", "arbitrary")),
        name="hgrn2_mixer",
    )(z3, z3, z3, z3, lower_bound.reshape(1, HG_W), norm_w.reshape(1, HG_DK))


def _ret_kernel(zq_ref, zk_ref, zv_ref, zg_ref, cos_ref, sin_ref, o_ref, st_ref, dmask_ref, *, chunk):
    hs = range(RET_HEADS)
    sl = [slice(h * RET_DK, (h + 1) * RET_DK) for h in hs]
    lg = [jnp.log(jnp.full((1, 1), 1.0 - 2.0 ** (-5.0 - h), F32)) for h in hs]

    @pl.when(pl.program_id(1) == 0)
    def _():
        st_ref[...] = jnp.zeros_like(st_ref)
        row = lax.broadcasted_iota(jnp.int32, (chunk, chunk), 0)
        col = lax.broadcasted_iota(jnp.int32, (chunk, chunk), 1)
        rel = (row - col).astype(F32)
        for h in hs:
            dmask_ref[h] = jnp.where(rel >= 0.0, jnp.exp(jnp.maximum(rel, 0.0) * lg[h]), 0.0)

    cos2 = cos_ref[0]
    sin2 = sin_ref[0]
    half = RET_DK // 2

    def rope(z):
        return z * cos2 + pltpu.roll(z, half, 1) * sin2

    tcol = lax.broadcasted_iota(jnp.int32, (chunk, 1), 0).astype(F32)
    q = [rope(zq_ref[0, :, sl[h]].astype(F32)) * (RET_DK ** -0.5) for h in hs]
    k = [rope(zk_ref[0, :, sl[h]].astype(F32)) for h in hs]
    v_b = [zv_ref[0, :, sl[h]].astype(BF16) for h in hs]
    st = [st_ref[h] for h in hs]
    scores = [(_dot_nt(q[h].astype(BF16), k[h].astype(BF16)) * dmask_ref[h]).astype(BF16) for h in hs]
    qx = [(q[h] * jnp.exp((tcol + 1.0) * lg[h])).astype(BF16) for h in hs]
    kz = [(k[h] * jnp.exp((chunk - 1.0 - tcol) * lg[h])).astype(BF16) for h in hs]
    o = [_dot(scores[h], v_b[h]) + _dot_nt(qx[h], st[h].astype(BF16)) for h in hs]
    for h in hs:
        st_ref[h] = st[h] * jnp.exp(chunk * lg[h]) + _dot_tn(v_b[h], kz[h])
    o = [o[h] * lax.rsqrt(jnp.mean(o[h] * o[h], axis=-1, keepdims=True) + NORM_EPS) for h in hs]
    o_ref[0] = (jnp.concatenate(o, axis=1) * _silu(zg_ref[0].astype(F32))).astype(o_ref.dtype)


def _ret_call(z3, cos2, sin2, chunk=256):
    b, t, _ = z3.shape
    chunk = min(chunk, t)
    base = RET_OFF // RET_W

    def zspec(part):
        return pl.BlockSpec((1, chunk, RET_W), lambda i, j: (i, j, base + part))

    tab = pl.BlockSpec((1, chunk, RET_DK), lambda i, j: (i, j, 0))
    return pl.pallas_call(
        functools.partial(_ret_kernel, chunk=chunk),
        grid=(b, t // chunk),
        in_specs=[zspec(0), zspec(1), zspec(2), zspec(3), tab, tab],
        out_specs=pl.BlockSpec((1, chunk, RET_W), lambda i, j: (i, j, 0)),
        out_shape=jax.ShapeDtypeStruct((b, t, RET_W), BF16),
        scratch_shapes=[pltpu.VMEM((RET_HEADS, RET_DK, RET_DK), F32), pltpu.VMEM((RET_HEADS, chunk, chunk), F32)],
        compiler_params=_cparams(("parallel", "arbitrary")),
        name="retention_mixer",
    )(z3, z3, z3, z3, cos2, sin2)


def _pair_blockdiag(y, pair_mask):
    return jnp.where(pair_mask, jnp.concatenate([y, y], axis=0), 0.0).astype(BF16)


def _pair_dot(x, y, pair_mask):
    return _dot(x.astype(BF16), _pair_blockdiag(y, pair_mask))


def _inv_unit_lower(a, eye, blk_mask, pair_mask):
    c = a[0].shape[0]
    m = range(len(a))
    a_bd = [jnp.where(blk_mask, a[i], 0.0) for i in m]
    a_off = [a[i] - a_bd[i] for i in m]
    a2 = [_pair_dot(a_bd[i], a_bd[i], pair_mask) for i in m]
    p = [eye + a_bd[i] for i in m]
    r = [_pair_dot(jnp.concatenate([p[i], a2[i]], axis=0), a2[i], pair_mask) for i in m]
    p = [p[i] + r[i][:c] for i in m]
    a4 = [r[i][c:] for i in m]
    r = [_pair_dot(jnp.concatenate([p[i], a4[i]], axis=0), a4[i], pair_mask) for i in m]
    p = [p[i] + r[i][:c] for i in m]
    a8 = [r[i][c:] for i in m]
    t_bd = [p[i] + _pair_dot(p[i], a8[i], pair_mask) for i in m]
    n = [_pair_dot(t_bd[i], a_off[i], pair_mask) for i in m]
    n2 = [_pair_dot(n[i], n[i], pair_mask) for i in m]
    z = [t_bd[i] + _pair_dot(n[i], t_bd[i], pair_mask) for i in m]
    return [z[i] + _pair_dot(n2[i], z[i], pair_mask) for i in m]


def _rwkv_kernel(z_ref, mu_ref, w0_ref, w2_ref, a0_ref, a2_ref, g2_ref, kk_ref, ka_ref, rk_ref,
                 lnw_ref, lnb_ref, seg_ref, o_ref, s_ref, prev_ref):
    c = RW_CHUNK
    nbe, tb = z_ref.shape[0], z_ref.shape[1]
    nck = tb // c
    bs = range(nbe)

    @pl.when(pl.program_id(1) == 0)
    def _():
        s_ref[...] = jnp.zeros_like(s_ref)
        prev_ref[...] = jnp.zeros_like(prev_ref)

    seg = seg_ref[...]
    rows = lax.broadcasted_iota(jnp.int32, (tb, 1), 0)
    rowb = lax.broadcasted_iota(jnp.int32, (tb, tb), 0)
    colb = lax.broadcasted_iota(jnp.int32, (tb, tb), 1)
    tri = jnp.where(colb >= (rowb // c) * c, jnp.where(rowb >= colb, 1.0, 0.0), 0.0).astype(BF16)

    def front(bi):
        z = z_ref[bi].astype(F32)
        z_prev = jnp.where(rows == 0, prev_ref[bi:bi + 1, :], pltpu.roll(z, 1, 0))
        prev_ref[bi:bi + 1, :] = z[tb - 1:tb]
        zs = z + mu_ref[...] * (z_prev - z)
        r = zs[:, 0:RW_W]
        k = zs[:, RW_W:2 * RW_W]
        v = zs[:, 2 * RW_W:3 * RW_W]
        off = 3 * RW_W
        w_lo = zs[:, off:off + RW_DECAY_LORA]
        a_lo = zs[:, off + RW_DECAY_LORA:off + RW_DECAY_LORA + RW_A_LORA]
        g_lo = zs[:, off + RW_DECAY_LORA + RW_A_LORA:]
        wx = -(w0_ref[...] + _dot_x3(jnp.tanh(w_lo), w2_ref[...]))
        softplus = jnp.maximum(wx, 0.0) + jnp.log(1.0 + jnp.exp(-jnp.abs(wx)))
        logw = -jnp.exp(-softplus - 0.5)
        a = _sigmoid(a0_ref[...] + _dot(a_lo.astype(BF16), a2_ref[...].astype(BF16)))
        g = _dot(_sigmoid(g_lo).astype(BF16), g2_ref[...].astype(BF16))
        kk = k * kk_ref[...]
        kk = kk * lax.rsqrt(jnp.maximum(_seg_sum(kk * kk, seg), 1e-24))
        k2 = k * (1.0 + (a - 1.0) * ka_ref[...])
        cw = _dot_x2_rhs(tri, logw)
        w_inv = jnp.exp(-cw)
        last = jnp.concatenate([jnp.broadcast_to(cw[(ci + 1) * c - 1:(ci + 1) * c], (c, RW_W)) for ci in range(nck)],
                               axis=0)
        w_rest = jnp.exp(last - cw)
        beta = a * kk
        return dict(alpha_t=-kk * jnp.exp(cw - logw), r_t=r * jnp.exp(cw), beta_h=beta * w_inv, k_h=k2 * w_inv,
                    beta_d=beta * w_rest, k_d=k2 * w_rest, v=v, g=g, rkk=r * k2 * rk_ref[...], cw=cw)

    fr = [front(bi) for bi in bs]

    pw = 2 * RW_N
    row2 = lax.broadcasted_iota(jnp.int32, (c, pw), 0)
    col2 = lax.broadcasted_iota(jnp.int32, (c, pw), 1) % c
    incl2 = row2 >= col2
    strict2 = row2 > col2
    blk_mask = (row2 // RW_BLK) == (col2 // RW_BLK)
    eye = (row2 == col2).astype(F32)
    rowp = lax.broadcasted_iota(jnp.int32, (pw, pw), 0)
    colp = lax.broadcasted_iota(jnp.int32, (pw, pw), 1)
    pair_mask = (rowp // RW_N) == (colp // RW_N)

    ps = range(RW_HEADS // 2)
    items = [(bi, ci, p) for bi in bs for ci in range(nck) for p in ps]
    where = {key: i for i, key in enumerate(items)}
    m = range(len(items))

    def slab(name, i):
        bi, ci, p = items[i]
        return fr[bi][name][ci * c:(ci + 1) * c, p * pw:(p + 1) * pw]

    v2 = [slab("v", i) for i in m]
    lhs = [jnp.concatenate([slab("alpha_t", i), slab("r_t", i)], axis=0).astype(BF16) for i in m]
    rhs = [jnp.concatenate([_pair_blockdiag(slab("beta_h", i), pair_mask),
                            _pair_blockdiag(slab("k_h", i), pair_mask)], axis=0) for i in m]
    big = [_dot_nt(lhs[i], rhs[i]) for i in m]
    a_ab = [jnp.where(strict2, big[i][:c, :pw], 0.0) for i in m]
    a_ak = [jnp.where(strict2, big[i][:c, pw:], 0.0) for i in m]
    a_rb = [jnp.where(incl2, big[i][c:, :pw], 0.0) for i in m]
    a_rk = [jnp.where(incl2, big[i][c:, pw:], 0.0) for i in m]
    t_inv = _inv_unit_lower(a_ab, eye, blk_mask, pair_mask)
    av = [_pair_dot(a_ak[i], v2[i], pair_mask) for i in m]
    u_const = [_pair_dot(t_inv[i], av[i], pair_mask) for i in m]
    lhs_s = [jnp.concatenate([_pair_dot(t_inv[i], slab("alpha_t", i), pair_mask).astype(BF16),
                              slab("r_t", i).astype(BF16)], axis=0) for i in m]
    a_r = [jnp.concatenate([a_rb[i], a_rk[i]], axis=1).astype(BF16) for i in m]
    bk_d = [jnp.concatenate([slab("beta_d", i), slab("k_d", i)], axis=0).astype(BF16) for i in m]
    bp = [(bi, p) for bi in bs for p in ps]
    s_cur = {(bi, p): s_ref[bi * len(ps) + p] for bi, p in bp}
    o_parts = {}
    for ci in range(nck):
        ix = {(bi, p): where[bi, ci, p] for bi, p in bp}
        sd = {q: _dot_nt(lhs_s[ix[q]], s_cur[q].astype(BF16)) for q in bp}
        u = {q: sd[q][:c] + u_const[ix[q]] for q in bp}
        uv = {q: jnp.concatenate([_pair_blockdiag(u[q], pair_mask), _pair_blockdiag(v2[ix[q]], pair_mask)], axis=0)
              for q in bp}
        for q in bp:
            o_parts[q[0], ci, q[1]] = sd[q][c:] + _dot(a_r[ix[q]], uv[q])
        uvt = {q: jnp.concatenate([u[q], v2[ix[q]]], axis=0).astype(BF16) for q in bp}
        for bi, p in bp:
            w_last = jnp.exp(fr[bi]["cw"][(ci + 1) * c - 1:(ci + 1) * c, p * pw:(p + 1) * pw])
            s_cur[bi, p] = s_cur[bi, p] * w_last + jnp.where(pair_mask, _dot_tn(uvt[bi, p], bk_d[ix[bi, p]]), 0.0)
    for bi, p in bp:
        s_ref[bi * len(ps) + p] = s_cur[bi, p]

    for bi in bs:
        o = jnp.concatenate([jnp.concatenate([o_parts[bi, ci, p] for p in ps], axis=1) for ci in range(nck)], axis=0)
        mean = _seg_sum(o, seg) * (1.0 / RW_N)
        dev = o - mean
        var = _seg_sum(dev * dev, seg) * (1.0 / RW_N)
        o = dev * lax.rsqrt(var + RW_GN_EPS) * lnw_ref[...] + lnb_ref[...]
        bonus = _seg_sum(fr[bi]["rkk"], seg) * fr[bi]["v"]
        o_ref[bi] = ((o + bonus) * fr[bi]["g"]).astype(o_ref.dtype)


def _rwkv_call(z3, mu, w0, w2, a0, a2, g2, k_k, k_a, r_k, ln_w, ln_b):
    b, t, _ = z3.shape
    c = min(RW_TB, t)
    hid = lax.broadcasted_iota(jnp.int32, (RW_W, RW_W), 0) // RW_N
    seg = (hid == hid.T).astype(BF16)

    def vec(n):
        return pl.BlockSpec((1, n), lambda i, j: (0, 0))

    def mat(m, n):
        return pl.BlockSpec((m, n), lambda i, j: (0, 0))

    nbe = RW_NB if b % RW_NB == 0 else 1
    return pl.pallas_call(
        _rwkv_kernel,
        grid=(b // nbe, t // c),
        in_specs=[
            pl.BlockSpec((nbe, c, RW_COLS), lambda i, j: (i, j, RW_OFF // RW_COLS)),
            vec(RW_COLS), vec(RW_W), mat(RW_DECAY_LORA, RW_W), vec(RW_W), mat(RW_A_LORA, RW_W),
            mat(RW_GATE_LORA, RW_W), vec(RW_W), vec(RW_W), vec(RW_W), vec(RW_W), vec(RW_W),
            mat(RW_W, RW_W),
        ],
        out_specs=pl.BlockSpec((nbe, c, RW_W), lambda i, j: (i, j, 0)),
        out_shape=jax.ShapeDtypeStruct((b, t, RW_W), BF16),
        scratch_shapes=[pltpu.VMEM((nbe * (RW_HEADS // 2), 2 * RW_N, 2 * RW_N), F32),
                        pltpu.VMEM((nbe, RW_COLS), F32)],
        compiler_params=_cparams(("parallel", "arbitrary")),
        name="rwkv7_mixer",
    )(z3, mu.reshape(1, -1), w0.reshape(1, -1), w2, a0.reshape(1, -1), a2, g2, k_k.reshape(1, -1),
      k_a.reshape(1, -1), r_k.reshape(1, -1), ln_w.reshape(1, -1), ln_b.reshape(1, -1), seg)


def _merge_kernel(ohg_ref, oret_ref, orw_ref, zg_ref, x_ref, gate_ref, bhg_ref, bret_ref, brw_ref,
                  wout_ref, o_ref):
    d = x_ref.shape[1]
    y = _sigmoid(zg_ref[:, 0:d].astype(F32)) * _dot(ohg_ref[...], bhg_ref[...])
    y = y + _sigmoid(zg_ref[:, d:2 * d].astype(F32)) * _dot(oret_ref[...], bret_ref[...])
    y = y + _sigmoid(zg_ref[:, 2 * d:3 * d].astype(F32)) * _dot(orw_ref[...], brw_ref[...])
    o_ref[...] = x_ref[...] + gate_ref[0] * _dot(y.astype(BF16), wout_ref[...])


def _merge_call(o_hg, o_ret, o_rw, z2, x2, mod3, br_hg, br_ret, br_rw, w_out, seq, gate_blk, tm=1024):
    n, d = x2.shape
    tpb = seq // tm

    def rows(w):
        return pl.BlockSpec((tm, w), lambda i: (i, 0))

    def full(m, k):
        return pl.BlockSpec((m, k), lambda i: (0, 0))

    return pl.pallas_call(
        _merge_kernel,
        grid=(n // tm,),
        in_specs=[
            rows(HG_W), rows(RET_W), rows(RW_W), rows(3 * d), rows(d),
            pl.BlockSpec((1, 1, d), lambda i: (i // tpb, 0, gate_blk)),
            full(HG_W, d), full(RET_W, d), full(RW_W, d), full(d, d),
        ],
        out_specs=rows(d),
        out_shape=jax.ShapeDtypeStruct((n, d), F32),
        compiler_params=_cparams(("parallel",)),
        name="merge_outproj",
    )(o_hg, o_ret, o_rw, z2, x2, mod3, br_hg, br_ret, br_rw, w_out)


def _pack_bf16_pairs(x):
    w = x.shape[1] // 2
    hi = pltpu.bitcast(x[:, :w].astype(BF16).astype(F32), jnp.uint32)
    lo = pltpu.bitcast(x[:, w:].astype(BF16).astype(F32), jnp.uint32)
    return pltpu.bitcast(hi | lax.shift_right_logical(lo, jnp.uint32(16)), jnp.int32)


def _unpack_bf16_pairs(p):
    u = pltpu.bitcast(p, jnp.uint32)
    hi = pltpu.bitcast(u & jnp.uint32(0xFFFF0000), F32)
    lo = pltpu.bitcast(lax.shift_left(u, jnp.uint32(16)), F32)
    return jnp.concatenate([hi, lo], axis=1)


def _route_kernel(x_ref, g_ref, scale_ref, shift_ref, rc_ref, hp_ref, eid_ref, wts_ref, cnt_ref):
    @pl.when(pl.program_id(0) == 0)
    def _():
        cnt_ref[...] = jnp.zeros_like(cnt_ref)

    h = _rms_mod(x_ref[...], g_ref[...], scale_ref[0], shift_ref[0])
    hp_ref[...] = _pack_bf16_pairs(h)
    tm = h.shape[0]
    lane = lax.broadcasted_iota(jnp.int32, (tm, LANES), 1)
    neg = -jnp.inf
    logits = _dot_x3(h, rc_ref[...])
    gl = jnp.where(lane < N_GROUPS, logits, neg)
    gmax = jnp.max(gl, axis=-1, keepdims=True)
    gidx = jnp.min(jnp.where(gl == gmax, lane, LANES), axis=-1, keepdims=True)
    gw = 1.0 / jnp.sum(jnp.exp(gl - gmax), axis=-1, keepdims=True)
    lo = N_GROUPS + gidx * EXPERTS_PER_GROUP
    el = jnp.where(lane >= lo, jnp.where(lane < lo + EXPERTS_PER_GROUP, logits, neg), neg)
    m1 = jnp.max(el, axis=-1, keepdims=True)
    l1 = jnp.min(jnp.where(el == m1, lane, LANES), axis=-1, keepdims=True)
    el2 = jnp.where(lane == l1, neg, el)
    m2 = jnp.max(el2, axis=-1, keepdims=True)
    l2 = jnp.min(jnp.where(el2 == m2, lane, LANES), axis=-1, keepdims=True)
    i1 = l1 - N_GROUPS
    i2 = l2 - N_GROUPS
    e2 = jnp.exp(m2 - m1)
    p1 = 1.0 / (1.0 + e2)
    p2 = e2 * p1
    oh1 = jnp.where(lane == i1, 1.0, 0.0)
    oh2 = jnp.where(lane == i2, 1.0, 0.0)
    row = lax.broadcasted_iota(jnp.int32, (tm, tm), 0)
    col = lax.broadcasted_iota(jnp.int32, (tm, tm), 1)
    earlier = jnp.where(row > col, 1.0, 0.0).astype(BF16)
    before = _dot(earlier, jnp.concatenate([oh1, oh2], axis=1).astype(BF16))
    tot1 = jnp.sum(oh1, axis=0, keepdims=True)
    carry = cnt_ref[...]
    r1 = jnp.sum(oh1 * (before[:, :LANES] + carry), axis=-1, keepdims=True).astype(jnp.int32)
    r2 = jnp.sum(oh2 * (before[:, LANES:] + (carry + tot1)), axis=-1, keepdims=True).astype(jnp.int32)
    cnt_ref[...] = carry + tot1 + jnp.sum(oh2, axis=0, keepdims=True)
    eid_ref[...] = jnp.where(lane == 0, i1, jnp.where(lane == 1, i2, jnp.where(lane == 2, r1,
                                                                             jnp.where(lane == 3, r2, 0))))
    wts_ref[...] = jnp.where(lane == 0, gw * p1, jnp.where(lane == 1, gw * p2, 0.0))


def _route_call(x2, gain, mod3, router_g, router_e, seq, scale_blk, shift_blk, tm=1024):
    n, d = x2.shape
    tpb = seq // tm
    rc = jnp.pad(jnp.concatenate([router_g, router_e], axis=1), ((0, 0), (0, LANES - N_GROUPS - N_EXPERTS)))
    return pl.pallas_call(
        _route_kernel,
        grid=(n // tm,),
        in_specs=[
            pl.BlockSpec((tm, d), lambda i: (i, 0)),
            pl.BlockSpec((1, d), lambda i: (0, 0)),
            pl.BlockSpec((1, 1, d), lambda i: (i // tpb, 0, scale_blk)),
            pl.BlockSpec((1, 1, d), lambda i: (i // tpb, 0, shift_blk)),
            pl.BlockSpec((d, LANES), lambda i: (0, 0)),
        ],
        out_specs=[pl.BlockSpec((tm, d // 2), lambda i: (i, 0)), pl.BlockSpec((tm, LANES), lambda i: (i, 0)),
                   pl.BlockSpec((tm, LANES), lambda i: (i, 0)), pl.BlockSpec((1, LANES), lambda i: (0, 0))],
        out_shape=[jax.ShapeDtypeStruct((n, d // 2), jnp.int32), jax.ShapeDtypeStruct((n, LANES), jnp.int32),
                   jax.ShapeDtypeStruct((n, LANES), F32), jax.ShapeDtypeStruct((1, LANES), F32)],
        compiler_params=_cparams(("arbitrary",)),
        name="moe_route",
    )(x2, gain.reshape(1, d), mod3, mod3, rc)


SC_CORES = 2
SC_SUBCORES = 16
SC_WORKERS = SC_CORES * SC_SUBCORES
SC_ROWS = 32
SC_STREAMS = 4


def _sc_gather(table, idx):
    m = idx.shape[0]
    w = table.shape[1]
    per_worker = m // SC_WORKERS
    steps = per_worker // SC_ROWS
    assert per_worker * SC_WORKERS == m and steps * SC_ROWS == per_worker and steps % SC_STREAMS == 0
    mesh = plsc.VectorSubcoreMesh(core_axis_name="c", subcore_axis_name="s")
    ks = range(SC_STREAMS)

    def body(table_hbm, idx_hbm, out_hbm, idx_v, *rest):
        bufs, g_sems, w_sems = rest[:SC_STREAMS], rest[SC_STREAMS:2 * SC_STREAMS], rest[2 * SC_STREAMS:]
        wid = lax.axis_index("s") * SC_CORES + lax.axis_index("c")
        pltpu.sync_copy(idx_hbm.at[wid], idx_v)

        @pl.loop(0, steps, step=SC_STREAMS)
        def _(j):
            row0 = wid * per_worker + j * SC_ROWS
            gathers = [pltpu.async_copy(table_hbm.at[idx_v.at[j + q]], bufs[q], g_sems[q]) for q in ks]
            writes = []
            for q in ks:
                gathers[q].wait()
                writes.append(pltpu.async_copy(bufs[q], out_hbm.at[pl.ds(row0 + q * SC_ROWS, SC_ROWS)], w_sems[q]))
            for q in ks:
                writes[q].wait()

    return pl.kernel(
        body,
        out_type=jax.ShapeDtypeStruct((m, w), table.dtype),
        mesh=mesh,
        scratch_types=[pltpu.VMEM((steps, SC_ROWS), jnp.int32)] + [pltpu.VMEM((SC_ROWS, w), table.dtype)] * SC_STREAMS
        + [pltpu.SemaphoreType.DMA] * (2 * SC_STREAMS),
        name="sc_row_gather",
    )(table, idx.reshape(SC_WORKERS, steps, SC_ROWS))


def _sc_scatter2(rows, idx0, idx1, p):
    n, w = rows.shape
    per_worker = n // SC_WORKERS
    steps = per_worker // SC_ROWS
    assert per_worker * SC_WORKERS == n and steps * SC_ROWS == per_worker and steps % SC_STREAMS == 0
    mesh = plsc.VectorSubcoreMesh(core_axis_name="c", subcore_axis_name="s")
    ks = range(SC_STREAMS)

    def body(rows_hbm, i0_hbm, i1_hbm, out_hbm, i0_v, i1_v, *rest):
        bufs, r_sems = rest[:SC_STREAMS], rest[SC_STREAMS:2 * SC_STREAMS]
        s0_sems, s1_sems = rest[2 * SC_STREAMS:3 * SC_STREAMS], rest[3 * SC_STREAMS:]
        wid = lax.axis_index("s") * SC_CORES + lax.axis_index("c")
        pltpu.sync_copy(i0_hbm.at[wid], i0_v)
        pltpu.sync_copy(i1_hbm.at[wid], i1_v)

        @pl.loop(0, steps, step=SC_STREAMS)
        def _(j):
            row0 = wid * per_worker + j * SC_ROWS
            reads = [pltpu.async_copy(rows_hbm.at[pl.ds(row0 + q * SC_ROWS, SC_ROWS)], bufs[q], r_sems[q]) for q in ks]
            writes = []
            for q in ks:
                reads[q].wait()
                writes.append(pltpu.async_copy(bufs[q], out_hbm.at[i0_v.at[j + q]], s0_sems[q]))
                writes.append(pltpu.async_copy(bufs[q], out_hbm.at[i1_v.at[j + q]], s1_sems[q]))
            for wr in writes:
                wr.wait()

    index_block = pltpu.VMEM((steps, SC_ROWS), jnp.int32)
    return pl.kernel(
        body,
        out_type=jax.ShapeDtypeStruct((p, w), rows.dtype),
        mesh=mesh,
        scratch_types=[index_block, index_block] + [pltpu.VMEM((SC_ROWS, w), rows.dtype)] * SC_STREAMS
        + [pltpu.SemaphoreType.DMA] * (3 * SC_STREAMS),
        name="sc_row_scatter",
    )(rows, idx0.reshape(SC_WORKERS, steps, SC_ROWS), idx1.reshape(SC_WORKERS, steps, SC_ROWS))


MOE_TM = 512


def _gexperts_kernel(te_ref, tv_ref, nu_ref, st_ref, sl_ref, nx_ref, xs_ref, w1_hbm, w3_hbm, w2_hbm, ys_ref,
                     f1_ref, f3_ref, f2_ref, w1b_ref, w3b_ref, w2b_ref, sem):
    i = pl.program_id(0)

    def copies(e, slot):
        return [pltpu.make_async_copy(w1_hbm.at[e], f1_ref.at[slot], sem.at[slot, 0]),
                pltpu.make_async_copy(w3_hbm.at[e], f3_ref.at[slot], sem.at[slot, 1]),
                pltpu.make_async_copy(w2_hbm.at[e], f2_ref.at[slot], sem.at[slot, 2])]

    @pl.when(i == 0)
    def _():
        for cp in copies(te_ref[0], sl_ref[0]):
            cp.start()

    @pl.when(st_ref[i] == 1)
    def _():
        slot = sl_ref[i]
        for cp in copies(te_ref[i], slot):
            cp.wait()

        @pl.when(nx_ref[i] >= 0)
        def _():
            for cp in copies(nx_ref[i], 1 - slot):
                cp.start()

        w1b_ref[...] = f1_ref[slot].astype(BF16)
        w3b_ref[...] = f3_ref[slot].astype(BF16)
        w2b_ref[...] = f2_ref[slot].astype(BF16)

    @pl.when(i < nu_ref[0])
    def _():
        rid = lax.broadcasted_iota(jnp.int32, xs_ref.shape, 0)
        xb = _unpack_bf16_pairs(jnp.where(rid < tv_ref[i], xs_ref[...], 0)).astype(BF16)
        act = (_silu(_dot(xb, w1b_ref[...])) * _dot(xb, w3b_ref[...])).astype(BF16)
        ys_ref[...] = _pack_bf16_pairs(_dot(act, w2b_ref[...]))


def _gexperts_call(xs, tile_expert, tile_valid, n_used, w1, w3, w2):
    p, half = xs.shape
    ne, d, de = w1.shape
    nt = p // MOE_TM
    te = tile_expert
    starts = jnp.concatenate([jnp.ones((1,), jnp.int32), (te[1:] != te[:-1]).astype(jnp.int32)])
    slot = (jnp.cumsum(starts) - 1) % 2
    later = jnp.where(te[None, :] > te[:, None], te[None, :], jnp.iinfo(jnp.int32).max)
    nxt = jnp.min(later, axis=1)
    nxt = jnp.where(nxt == jnp.iinfo(jnp.int32).max, -1, nxt)

    def rows(i, *_):
        return (jnp.minimum(i, _[2][0] - 1), 0)

    return pl.pallas_call(
        _gexperts_kernel,
        grid_spec=pltpu.PrefetchScalarGridSpec(
            num_scalar_prefetch=6,
            grid=(nt,),
            in_specs=[
                pl.BlockSpec((MOE_TM, half), rows),
                pl.BlockSpec(memory_space=pl.ANY),
                pl.BlockSpec(memory_space=pl.ANY),
                pl.BlockSpec(memory_space=pl.ANY),
            ],
            out_specs=pl.BlockSpec((MOE_TM, half), rows),
            scratch_shapes=[pltpu.VMEM((2, d, de), F32), pltpu.VMEM((2, d, de), F32), pltpu.VMEM((2, de, d), F32),
                            pltpu.VMEM((d, de), BF16), pltpu.VMEM((d, de), BF16), pltpu.VMEM((de, d), BF16),
                            pltpu.SemaphoreType.DMA((2, 3))],
        ),
        out_shape=jax.ShapeDtypeStruct((p, half), jnp.int32),
        compiler_params=_cparams(("arbitrary",)),
        name="moe_experts",
    )(tile_expert, tile_valid, n_used, starts, slot.astype(jnp.int32), nxt.astype(jnp.int32), xs, w1, w3, w2)


def _combine_kernel(y0_ref, y1_ref, wts_ref, x_ref, gate_ref, fg_ref, o_ref, *, final_norm):
    wts = wts_ref[...]
    moe = wts[:, 0:1] * _unpack_bf16_pairs(y0_ref[...]) + wts[:, 1:2] * _unpack_bf16_pairs(y1_ref[...])
    xn = x_ref[...] + gate_ref[0] * moe
    if final_norm:
        xn = xn * lax.rsqrt(jnp.mean(xn * xn, axis=-1, keepdims=True) + NORM_EPS) * fg_ref[...]
    o_ref[...] = xn


def _combine_call(yg, wts, x2, mod3, final_g, seq, gate_blk, final_norm, tm=1024):
    n, d = x2.shape
    tpb = seq // tm
    slot1 = n // tm
    return pl.pallas_call(
        functools.partial(_combine_kernel, final_norm=final_norm),
        grid=(n // tm,),
        in_specs=[
            pl.BlockSpec((tm, d // 2), lambda i: (i, 0)),
            pl.BlockSpec((tm, d // 2), lambda i: (i + slot1, 0)),
            pl.BlockSpec((tm, LANES), lambda i: (i, 0)),
            pl.BlockSpec((tm, d), lambda i: (i, 0)),
            pl.BlockSpec((1, 1, d), lambda i: (i // tpb, 0, gate_blk)),
            pl.BlockSpec((1, d), lambda i: (0, 0)),
        ],
        out_specs=pl.BlockSpec((tm, d), lambda i: (i, 0)),
        out_shape=jax.ShapeDtypeStruct((n, d), F32),
        compiler_params=_cparams(("parallel",)),
        name="moe_combine",
    )(yg, yg, wts, x2, mod3, final_g.reshape(1, d))


def _pos_kernel(eid_ref, ts_ref, p0_ref, p1_ref):
    eid = eid_ref[...]
    tm = eid.shape[0]
    lane = lax.broadcasted_iota(jnp.int32, (tm, LANES), 1)
    sub = lax.broadcasted_iota(jnp.int32, (tm, LANES), 0) % LANES
    for slot, out_ref in ((0, p0_ref), (1, p1_ref)):
        first_row = jnp.sum(jnp.where(lane == eid[:, slot:slot + 1], ts_ref[...], 0), axis=-1, keepdims=True)
        pos = first_row + eid[:, slot + 2:slot + 3]
        out_ref[...] = jnp.sum(jnp.where(lane == sub, pos, 0).reshape(tm // LANES, LANES, LANES), axis=1)


def _pos_call(eid, first_rows, tm=4096):
    n = eid.shape[0]
    tm = min(tm, n)
    out = jax.ShapeDtypeStruct((n // LANES, LANES), jnp.int32)
    p0, p1 = pl.pallas_call(
        _pos_kernel,
        grid=(n // tm,),
        in_specs=[pl.BlockSpec((tm, LANES), lambda i: (i, 0)), pl.BlockSpec((1, LANES), lambda i: (0, 0))],
        out_specs=[pl.BlockSpec((tm // LANES, LANES), lambda i: (i, 0))] * 2,
        out_shape=[out, out],
        compiler_params=_cparams(("parallel",)),
        name="moe_positions",
    )(eid, first_rows)
    return p0.reshape(n), p1.reshape(n)


def _moe_plan(eid, counts_f):
    n = eid.shape[0]
    nt = (2 * n) // MOE_TM + N_EXPERTS
    counts = counts_f[0, :N_EXPERTS].astype(jnp.int32)
    tiles = (counts + MOE_TM - 1) // MOE_TM
    tile_end = jnp.cumsum(tiles)
    tile_start = tile_end - tiles
    n_used = tile_end[-1:]
    tile_iota = jnp.arange(nt, dtype=jnp.int32)
    tile_expert = jnp.sum(jnp.minimum(tile_iota, n_used - 1)[:, None] >= tile_end[None, :], axis=1, dtype=jnp.int32)
    own = tile_expert[:, None] == jnp.arange(N_EXPERTS, dtype=jnp.int32)[None, :]
    count_t = jnp.sum(jnp.where(own, counts[None, :], 0), axis=1)
    start_t = jnp.sum(jnp.where(own, tile_start[None, :], 0), axis=1)
    tile_valid = jnp.clip(count_t - (tile_iota - start_t) * MOE_TM, 0, MOE_TM)
    first_rows = jnp.pad(tile_start * MOE_TM, (0, LANES - N_EXPERTS)).reshape(1, LANES)
    pos0, pos1 = _pos_call(eid, first_rows)
    return pos0, pos1, tile_expert, tile_valid, n_used


def kernel(x, c, positions, ada_w, ada_b, norm1_g, norm2_g, w_in, hg_lb_table, hg_norm_w, rw_mu, rw_w0, rw_w2,
           rw_a0, rw_a2, rw_g2, rw_k_k, rw_k_a, rw_r_k, rw_ln_w, rw_ln_b, br_hg, br_ret, br_rw, w_out,
           router_g, router_e, moe_w1, moe_w3, moe_w2, final_g):
    b, t, d = x.shape
    depth = ada_w.shape[0]
    n = b * t
    assert w_in.shape[2] == IN_COLS and d == 1024

    lb_p = jax.nn.softmax(hg_lb_table.astype(F32), axis=0)
    lower_bounds = jnp.cumsum(lb_p, axis=0) - lb_p[0]

    mod = _mod_call(c, ada_w, ada_b)
    cos2, sin2 = _rope_call(positions, RET_DK)
    x2 = x.reshape(n, d)
    for l in range(depth):
        mod3 = mod[l].reshape(b, 1, 6 * d)
        z2 = _inproj_call(x2, norm1_g[l], mod3, _wprep_call(w_in, l), t, scale_blk=1, shift_blk=0)
        z3 = z2.reshape(b, t, IN_COLS)
        o_hg = _hgrn2_call(z3, lower_bounds[l], hg_norm_w[l])
        o_ret = _ret_call(z3, cos2, sin2)
        o_rw = _rwkv_call(z3, rw_mu[l], rw_w0[l], rw_w2[l], rw_a0[l], rw_a2[l], rw_g2[l], rw_k_k[l],
                          rw_k_a[l], rw_r_k[l], rw_ln_w[l], rw_ln_b[l])
        x2 = _merge_call(o_hg.reshape(n, HG_W), o_ret.reshape(n, RET_W), o_rw.reshape(n, RW_W), z2, x2, mod3,
                         br_hg[l].astype(BF16), br_ret[l].astype(BF16), br_rw[l].astype(BF16),
                         w_out[l].astype(BF16), t, gate_blk=2)
        hp, eid, wts, counts = _route_call(x2, norm2_g[l], mod3, router_g[l], router_e[l], t, scale_blk=4,
                                           shift_blk=3)
        pos0, pos1, tile_expert, tile_valid, n_used = _moe_plan(eid, counts)
        xs = _sc_scatter2(hp, pos0, pos1, (2 * n // MOE_TM + N_EXPERTS) * MOE_TM)
        ys = _gexperts_call(xs, tile_expert + l * N_EXPERTS, tile_valid, n_used,
                            moe_w1.reshape((-1,) + moe_w1.shape[2:]), moe_w3.reshape((-1,) + moe_w3.shape[2:]),
                            moe_w2.reshape((-1,) + moe_w2.shape[2:]))
        yg = _sc_gather(ys, jnp.concatenate([pos0, pos1]))
        x2 = _combine_call(yg, wts, x2, mod3, final_g, t, gate_blk=5, final_norm=(l == depth - 1))
    return x2.reshape(b, t, d)
```

```python
import functools

import jax
import jax.numpy as jnp
from jax import lax
from jax.experimental import pallas as pl
from jax.experimental.pallas import tpu as pltpu
from jax.experimental.pallas import tpu_sc as plsc

F32 = jnp.float32
BF16 = jnp.bfloat16
HIGHEST = lax.Precision.HIGHEST

HG_HEADS = 4
HG_DK = 128
HG_W = HG_HEADS * HG_DK
RET_HEADS = 4
RET_DK = 128
RET_W = RET_HEADS * RET_DK
RW_HEADS = 8
RW_N = 64
RW_W = RW_HEADS * RW_N
RW_DECAY_LORA = 64
RW_A_LORA = 64
RW_GATE_LORA = 128
RW_COLS = 3 * RW_W + RW_DECAY_LORA + RW_A_LORA + RW_GATE_LORA
RW_GN_EPS = 64e-5
N_GROUPS = 4
EXPERTS_PER_GROUP = 8
N_EXPERTS = N_GROUPS * EXPERTS_PER_GROUP
ROPE_THETA = 10000.0
NORM_EPS = 1e-6

LANES = 128
LOG2E = 1.4426950408889634
VMEM_LIMIT = 56 * 1024 * 1024

GATE_OFF = 0
HG_OFF = 3 * 1024
RET_OFF = HG_OFF + 4 * HG_W
RW_OFF = RET_OFF + 4 * RET_W
IN_COLS = RW_OFF + RW_COLS

HG_CHUNK = 64
HG_SUB = 16
HG_NB = 4
HG_SAFE_SPAN = 60.0
RW_CHUNK = 64
RW_BLK = 16
RW_NB = 4
RW_TB = 256
Z_DTYPE = BF16


def _cparams(sem):
    return pltpu.CompilerParams(dimension_semantics=sem, vmem_limit_bytes=VMEM_LIMIT)


def _dot(a, b, precision=None):
    return jnp.dot(a, b, preferred_element_type=F32, precision=precision)


def _dot_nt(a, b, precision=None):
    return lax.dot_general(a, b, (((1,), (1,)), ((), ())), preferred_element_type=F32, precision=precision)


def _dot_tn(a, b, precision=None):
    return lax.dot_general(a, b, (((0,), (0,)), ((), ())), preferred_element_type=F32, precision=precision)


def _split_bf16(x):
    hi = x.astype(BF16)
    return hi, (x - hi.astype(F32)).astype(BF16)


def _dot_x3(a, b):
    ah, al = _split_bf16(a)
    bh, bl = _split_bf16(b)
    return _dot(ah, bh) + _dot(ah, bl) + _dot(al, bh)


def _seg_sum(x, seg):
    return _dot(x.astype(BF16), seg)


def _dot_x2_rhs(a_exact, b):
    bh, bl = _split_bf16(b)
    return _dot(a_exact, bh) + _dot(a_exact, bl)


def _sigmoid(x):
    return 0.5 * jnp.tanh(0.5 * x) + 0.5


def _silu(x):
    return x * _sigmoid(x)


def _rms_mod(x, gain, scale, shift):
    y = x * lax.rsqrt(jnp.mean(x * x, axis=-1, keepdims=True) + NORM_EPS)
    return (y * gain) * (1.0 + scale) + shift


def _mod_kernel(c_ref, w_ref, b_ref, o_ref):
    c = c_ref[...]
    o_ref[0] = _dot(_silu(c), w_ref[0], HIGHEST) + b_ref[0]


def _mod_call(c, ada_w, ada_b):
    depth, d, d6 = ada_w.shape
    b = c.shape[0]
    nblk = d6 // d
    return pl.pallas_call(
        _mod_kernel,
        grid=(depth, nblk),
        in_specs=[
            pl.BlockSpec((b, d), lambda l, j: (0, 0)),
            pl.BlockSpec((1, d, d), lambda l, j: (l, 0, j)),
            pl.BlockSpec((1, 1, d), lambda l, j: (l, 0, j)),
        ],
        out_specs=pl.BlockSpec((1, b, d), lambda l, j: (l, 0, j)),
        out_shape=jax.ShapeDtypeStruct((depth, b, d6), F32),
        compiler_params=_cparams(("parallel", "parallel")),
        name="adaln_mod",
    )(c, ada_w, ada_b.reshape(depth, 1, d6))


def _rope_kernel(pos_ref, freq_ref, sign_ref, cos_ref, sin_ref):
    ang = pos_ref[0].astype(F32) * freq_ref[...]
    cos_ref[0] = jnp.cos(ang)
    sin_ref[0] = jnp.sin(ang) * sign_ref[...]


def _rope_call(positions, d):
    b, t = positions.shape
    tb = min(t, 512)
    inv_freq = ROPE_THETA ** (-jnp.arange(0, d, 2, dtype=F32) / d)
    freq2 = jnp.concatenate([inv_freq, inv_freq]).reshape(1, d)
    sign2 = jnp.concatenate([-jnp.ones((d // 2,), F32), jnp.ones((d // 2,), F32)]).reshape(1, d)
    out = jax.ShapeDtypeStruct((b, t, d), F32)
    return pl.pallas_call(
        _rope_kernel,
        grid=(b, t // tb),
        in_specs=[
            pl.BlockSpec((1, tb, 1), lambda i, j: (i, j, 0)),
            pl.BlockSpec((1, d), lambda i, j: (0, 0)),
            pl.BlockSpec((1, d), lambda i, j: (0, 0)),
        ],
        out_specs=[pl.BlockSpec((1, tb, d), lambda i, j: (i, j, 0))] * 2,
        out_shape=[out, out],
        compiler_params=_cparams(("parallel", "parallel")),
        name="rope_tables",
    )(positions.reshape(b, t, 1), freq2, sign2)


W_BLK = 256


def _wprep_kernel(w_ref, o_ref):
    o_ref[...] = w_ref[...].astype(o_ref.dtype)


def _wprep_call(w_in, layer):
    _, d, cols = w_in.shape
    nblk = cols // W_BLK
    first = (cols - 3 * d) // W_BLK
    return pl.pallas_call(
        _wprep_kernel,
        grid=(nblk,),
        in_specs=[pl.BlockSpec((1, d, W_BLK), lambda j: (layer, 0, (j + first) % nblk))],
        out_specs=pl.BlockSpec((1, d, W_BLK), lambda j: (0, 0, j)),
        out_shape=jax.ShapeDtypeStruct((1, d, cols), BF16),
        compiler_params=_cparams(("parallel",)),
        name="w_in_layout",
    )(w_in)


def _inproj_kernel(x_ref, g_ref, scale_ref, shift_ref, w_ref, o_ref, h_ref):
    @pl.when(pl.program_id(1) == 0)
    def _():
        h = _rms_mod(x_ref[...], g_ref[...], scale_ref[0], shift_ref[0])
        h_ref[...] = h.astype(BF16)

    o_ref[...] = _dot(h_ref[...], w_ref[0]).astype(o_ref.dtype)


def _inproj_call(x2, gain, mod3, w_bf16, seq, scale_blk, shift_blk, tm=2048, tn=1792):
    n, d = x2.shape
    cols = w_bf16.shape[2]
    tpb = seq // tm
    return pl.pallas_call(
        _inproj_kernel,
        grid=(n // tm, cols // tn),
        in_specs=[
            pl.BlockSpec((tm, d), lambda i, j: (i, 0)),
            pl.BlockSpec((1, d), lambda i, j: (0, 0)),
            pl.BlockSpec((1, 1, d), lambda i, j: (i // tpb, 0, scale_blk)),
            pl.BlockSpec((1, 1, d), lambda i, j: (i // tpb, 0, shift_blk)),
            pl.BlockSpec((1, d, tn), lambda i, j: (0, 0, j)),
        ],
        out_specs=pl.BlockSpec((tm, tn), lambda i, j: (i, j)),
        out_shape=jax.ShapeDtypeStruct((n, cols), Z_DTYPE),
        scratch_shapes=[pltpu.VMEM((tm, d), BF16)],
        compiler_params=_cparams(("parallel", "arbitrary")),
        name="norm_inproj",
    )(x2, gain.reshape(1, d), mod3, mod3, w_bf16)


def _hgrn2_block(zs, lbs, nw, sts, factored):
    hs = range(len(zs))
    tb = zs[0][0].shape[0]
    c, sub = HG_CHUNK, HG_SUB
    nc, ns, nb = tb // c, c // sub, tb // sub
    f = [lbs[h] + (1.0 - lbs[h]) * _sigmoid(zs[h][1]) for h in hs]
    logf = [jnp.log(jnp.maximum(f[h], 1e-30)) for h in hs]
    q = [_silu(zs[h][0]) * (HG_DK ** -0.5) for h in hs]
    k = [1.0 - f[h] for h in hs]
    v = [zs[h][2] for h in hs]
    v_b = [v[h].astype(BF16) for h in hs]
    row = lax.broadcasted_iota(jnp.int32, (tb, tb), 0)
    col = lax.broadcasted_iota(jnp.int32, (tb, tb), 1)
    tri = jnp.where(col >= (row // c) * c, jnp.where(row >= col, 1.0, 0.0), 0.0).astype(BF16)
    cum = [_dot_x2_rhs(tri, logf[h]) for h in hs]
    cum3 = [cum[h].reshape(nb, sub, HG_DK) for h in hs]
    ref3 = [cum3[h][:, 0:1, :] - logf[h].reshape(nb, sub, HG_DK)[:, 0:1, :] for h in hs]
    span = functools.reduce(jnp.maximum, [jnp.max(ref3[h] - cum3[h][:, sub - 1:sub, :]) for h in hs])
    qe = [(q[h] * jnp.exp(cum[h])).astype(BF16) for h in hs]

    offd = [(h, ci * c, ci * c + sub * i) for h in hs for ci in range(nc) for i in range(1, ns)]
    base = [cum[h][lo - 1:lo] for h, _, lo in offd]
    qt = [(q[h][lo:lo + sub] * jnp.exp(cum[h][lo:lo + sub] - base[j])).astype(BF16)
          for j, (h, _, lo) in enumerate(offd)]
    kt = [(k[h][r0:lo] * jnp.exp(base[j] - cum[h][r0:lo])).astype(BF16) for j, (h, r0, lo) in enumerate(offd)]
    a = [_dot_nt(qt[j], kt[j]).astype(BF16) for j in range(len(offd))]
    av = {(h, lo): _dot(a[j], v_b[h][r0:lo]) for j, (h, r0, lo) in enumerate(offd)}

    cs = [slice(ci * c, (ci + 1) * c) for ci in range(nc)]
    hc = [(h, ci) for h in hs for ci in range(nc)]
    last = {(h, ci): cum[h][(ci + 1) * c - 1:(ci + 1) * c] for h, ci in hc}
    kd = {(h, ci): (k[h][cs[ci]] * jnp.exp(last[h, ci] - cum[h][cs[ci]])).astype(BF16) for h, ci in hc}
    inc = {(h, ci): _dot_tn(v_b[h][cs[ci]], kd[h, ci]) for h, ci in hc}
    s_in = {(h, 0): sts[h] for h in hs}
    for ci in range(nc):
        for h in hs:
            s_in[h, ci + 1] = s_in[h, ci] * jnp.exp(last[h, ci]) + inc[h, ci]
    o_inter = {(h, ci): _dot_nt(qe[h][cs[ci]], s_in[h, ci].astype(BF16)) for h, ci in hc}

    if factored:
        qf = [(q[h] * jnp.exp(cum3[h] - ref3[h]).reshape(tb, HG_DK)).astype(BF16) for h in hs]
        kf = [(k[h] * jnp.exp(ref3[h] - cum3[h]).reshape(tb, HG_DK)).astype(BF16) for h in hs]
        rc = lax.broadcasted_iota(jnp.int32, (c, c), 0)
        cc = lax.broadcasted_iota(jnp.int32, (c, c), 1)
        keep = (rc >= cc) & (rc // sub == cc // sub)
        a_d = {(h, ci): jnp.where(keep, _dot_nt(qf[h][cs[ci]], kf[h][cs[ci]]), 0.0).astype(BF16) for h, ci in hc}
        dg = {(h, ci): _dot(a_d[h, ci], v_b[h][cs[ci]]) for h, ci in hc}
        diag = [jnp.concatenate([dg[h, ci] for ci in range(nc)], axis=0) for h in hs]
    else:
        gb = 4
        trow = lax.broadcasted_iota(jnp.int32, (gb, sub, HG_DK), 1)
        diag = []
        for h in hs:
            c2 = cum[h] * LOG2E
            ks2 = c2 - jnp.log2(k[h])
            parts = []
            for g0 in range(0, nb, gb):
                rws = slice(g0 * sub, (g0 + gb) * sub)
                c23, ks23, q3, v3 = (x[rws].reshape(gb, sub, HG_DK) for x in (c2, ks2, q[h], v[h]))
                acc = jnp.zeros((gb, sub, HG_DK), F32)
                for s in range(sub):
                    e = jnp.exp2(jnp.where(trow >= s, c23 - ks23[:, s:s + 1, :], -jnp.inf))
                    a_col = jnp.sum(q3 * e, axis=-1, keepdims=True)
                    acc = acc + a_col * v3[:, s:s + 1, :]
                parts.append(acc.reshape(gb * sub, HG_DK))
            diag.append(jnp.concatenate(parts, axis=0))

    outs = []
    for h in hs:
        pieces = []
        for ci in range(nc):
            for i in range(ns):
                lo = ci * c + sub * i
                piece = o_inter[h, ci][sub * i:sub * (i + 1)] + diag[h][lo:lo + sub]
                pieces.append(piece + av[h, lo] if i > 0 else piece)
        o = jnp.concatenate(pieces, axis=0)
        o = o * lax.rsqrt(jnp.mean(o * o, axis=-1, keepdims=True) + NORM_EPS)
        outs.append(o * nw * _silu(zs[h][3]))
    return outs, [s_in[h, nc] for h in hs], span


def _hgrn2_kernel(zq_ref, zf_ref, zi_ref, zg_ref, lb_ref, nw_ref, o_ref, st_ref):
    @pl.when(pl.program_id(1) == 0)
    def _():
        st_ref[...] = jnp.zeros_like(st_ref)

    nbe = zq_ref.shape[0]
    hs = range(HG_HEADS)
    sl = [slice(h * HG_DK, (h + 1) * HG_DK) for h in hs]
    items = [(bi, h) for bi in range(nbe) for h in hs]

    def run(factored):
        zs = [tuple(r[bi, :, sl[h]].astype(F32) for r in (zq_ref, zf_ref, zi_ref, zg_ref)) for bi, h in items]
        outs, sts, span = _hgrn2_block(zs, [lb_ref[:, sl[h]] for _, h in items], nw_ref[...],
                                       [st_ref[i] for i in range(len(items))], factored)
        return outs, sts, span

    def put(outs, sts):
        for i in range(len(items)):
            st_ref[i] = sts[i]
        for bi in range(nbe):
            o_ref[bi] = jnp.concatenate(outs[bi * HG_HEADS:(bi + 1) * HG_HEADS], axis=1).astype(o_ref.dtype)

    st_old = [st_ref[i] for i in range(len(items))]
    outs, st_new, span = run(True)
    put(outs, st_new)

    @pl.when(span > HG_SAFE_SPAN)
    def _():
        for i in range(len(items)):
            st_ref[i] = st_old[i]
        outs2, st2, _ = run(False)
        put(outs2, st2)


def _hgrn2_call(z3, lower_bound, norm_w, tb=256):
    b, t, _ = z3.shape
    tb = min(tb, t)
    base = HG_OFF // HG_W

    nbe = HG_NB if b % HG_NB == 0 else 1

    def zspec(part):
        return pl.BlockSpec((nbe, tb, HG_W), lambda i, j: (i, j, base + part))

    return pl.pallas_call(
        _hgrn2_kernel,
        grid=(b // nbe, t // tb),
        in_specs=[
            zspec(0), zspec(1), zspec(2), zspec(3),
            pl.BlockSpec((1, HG_W), lambda i, j: (0, 0)),
            pl.BlockSpec((1, LANES), lambda i, j: (0, 0)),
        ],
        out_specs=pl.BlockSpec((nbe, tb, HG_W), lambda i, j: (i, j, 0)),
        out_shape=jax.ShapeDtypeStruct((b, t, HG_W), BF16),
        scratch_shapes=[pltpu.VMEM((nbe * HG_HEADS, HG_DK, HG_DK), F32)],
        compiler_params=_cparams(("parallel", "arbitrary")),
        name="hgrn2_mixer",
    )(z3, z3, z3, z3, lower_bound.reshape(1, HG_W), norm_w.reshape(1, HG_DK))


def _ret_kernel(zq_ref, zk_ref, zv_ref, zg_ref, cos_ref, sin_ref, o_ref, st_ref, dmask_ref, *, chunk):
    hs = range(RET_HEADS)
    sl = [slice(h * RET_DK, (h + 1) * RET_DK) for h in hs]
    lg = [jnp.log(jnp.full((1, 1), 1.0 - 2.0 ** (-5.0 - h), F32)) for h in hs]

    @pl.when(pl.program_id(1) == 0)
    def _():
        st_ref[...] = jnp.zeros_like(st_ref)
        row = lax.broadcasted_iota(jnp.int32, (chunk, chunk), 0)
        col = lax.broadcasted_iota(jnp.int32, (chunk, chunk), 1)
        rel = (row - col).astype(F32)
        for h in hs:
            dmask_ref[h] = jnp.where(rel >= 0.0, jnp.exp(jnp.maximum(rel, 0.0) * lg[h]), 0.0)

    cos2 = cos_ref[0]
    sin2 = sin_ref[0]
    half = RET_DK // 2

    def rope(z):
        return z * cos2 + pltpu.roll(z, half, 1) * sin2

    tcol = lax.broadcasted_iota(jnp.int32, (chunk, 1), 0).astype(F32)
    q = [rope(zq_ref[0, :, sl[h]].astype(F32)) * (RET_DK ** -0.5) for h in hs]
    k = [rope(zk_ref[0, :, sl[h]].astype(F32)) for h in hs]
    v_b = [zv_ref[0, :, sl[h]].astype(BF16) for h in hs]
    st = [st_ref[h] for h in hs]
    scores = [(_dot_nt(q[h].astype(BF16), k[h].astype(BF16)) * dmask_ref[h]).astype(BF16) for h in hs]
    qx = [(q[h] * jnp.exp((tcol + 1.0) * lg[h])).astype(BF16) for h in hs]
    kz = [(k[h] * jnp.exp((chunk - 1.0 - tcol) * lg[h])).astype(BF16) for h in hs]
    o = [_dot(scores[h], v_b[h]) + _dot_nt(qx[h], st[h].astype(BF16)) for h in hs]
    for h in hs:
        st_ref[h] = st[h] * jnp.exp(chunk * lg[h]) + _dot_tn(v_b[h], kz[h])
    o = [o[h] * lax.rsqrt(jnp.mean(o[h] * o[h], axis=-1, keepdims=True) + NORM_EPS) for h in hs]
    o_ref[0] = (jnp.concatenate(o, axis=1) * _silu(zg_ref[0].astype(F32))).astype(o_ref.dtype)


def _ret_call(z3, cos2, sin2, chunk=256):
    b, t, _ = z3.shape
    chunk = min(chunk, t)
    base = RET_OFF // RET_W

    def zspec(part):
        return pl.BlockSpec((1, chunk, RET_W), lambda i, j: (i, j, base + part))

    tab = pl.BlockSpec((1, chunk, RET_DK), lambda i, j: (i, j, 0))
    return pl.pallas_call(
        functools.partial(_ret_kernel, chunk=chunk),
        grid=(b, t // chunk),
        in_specs=[zspec(0), zspec(1), zspec(2), zspec(3), tab, tab],
        out_specs=pl.BlockSpec((1, chunk, RET_W), lambda i, j: (i, j, 0)),
        out_shape=jax.ShapeDtypeStruct((b, t, RET_W), BF16),
        scratch_shapes=[pltpu.VMEM((RET_HEADS, RET_DK, RET_DK), F32), pltpu.VMEM((RET_HEADS, chunk, chunk), F32)],
        compiler_params=_cparams(("parallel", "arbitrary")),
        name="retention_mixer",
    )(z3, z3, z3, z3, cos2, sin2)


def _pair_blockdiag(y, pair_mask):
    return jnp.where(pair_mask, jnp.concatenate([y, y], axis=0), 0.0).astype(BF16)


def _pair_dot(x, y, pair_mask):
    return _dot(x.astype(BF16), _pair_blockdiag(y, pair_mask))


def _inv_unit_lower(a, eye, blk_mask, pair_mask):
    c = a[0].shape[0]
    m = range(len(a))
    a_bd = [jnp.where(blk_mask, a[i], 0.0) for i in m]
    a_off = [a[i] - a_bd[i] for i in m]
    a2 = [_pair_dot(a_bd[i], a_bd[i], pair_mask) for i in m]
    p = [eye + a_bd[i] for i in m]
    r = [_pair_dot(jnp.concatenate([p[i], a2[i]], axis=0), a2[i], pair_mask) for i in m]
    p = [p[i] + r[i][:c] for i in m]
    a4 = [r[i][c:] for i in m]
    r = [_pair_dot(jnp.concatenate([p[i], a4[i]], axis=0), a4[i], pair_mask) for i in m]
    p = [p[i] + r[i][:c] for i in m]
    a8 = [r[i][c:] for i in m]
    t_bd = [p[i] + _pair_dot(p[i], a8[i], pair_mask) for i in m]
    n = [_pair_dot(t_bd[i], a_off[i], pair_mask) for i in m]
    n2 = [_pair_dot(n[i], n[i], pair_mask) for i in m]
    z = [t_bd[i] + _pair_dot(n[i], t_bd[i], pair_mask) for i in m]
    return [z[i] + _pair_dot(n2[i], z[i], pair_mask) for i in m]


def _rwkv_kernel(z_ref, mu_ref, w0_ref, w2_ref, a0_ref, a2_ref, g2_ref, kk_ref, ka_ref, rk_ref,
                 lnw_ref, lnb_ref, seg_ref, o_ref, s_ref, prev_ref):
    c = RW_CHUNK
    nbe, tb = z_ref.shape[0], z_ref.shape[1]
    nck = tb // c
    bs = range(nbe)

    @pl.when(pl.program_id(1) == 0)
    def _():
        s_ref[...] = jnp.zeros_like(s_ref)
        prev_ref[...] = jnp.zeros_like(prev_ref)

    seg = seg_ref[...]
    rows = lax.broadcasted_iota(jnp.int32, (tb, 1), 0)
    rowb = lax.broadcasted_iota(jnp.int32, (tb, tb), 0)
    colb = lax.broadcasted_iota(jnp.int32, (tb, tb), 1)
    tri = jnp.where(colb >= (rowb // c) * c, jnp.where(rowb >= colb, 1.0, 0.0), 0.0).astype(BF16)

    def front(bi):
        z = z_ref[bi].astype(F32)
        z_prev = jnp.where(rows == 0, prev_ref[bi:bi + 1, :], pltpu.roll(z, 1, 0))
        prev_ref[bi:bi + 1, :] = z[tb - 1:tb]
        zs = z + mu_ref[...] * (z_prev - z)
        r = zs[:, 0:RW_W]
        k = zs[:, RW_W:2 * RW_W]
        v = zs[:, 2 * RW_W:3 * RW_W]
        off = 3 * RW_W
        w_lo = zs[:, off:off + RW_DECAY_LORA]
        a_lo = zs[:, off + RW_DECAY_LORA:off + RW_DECAY_LORA + RW_A_LORA]
        g_lo = zs[:, off + RW_DECAY_LORA + RW_A_LORA:]
        wx = -(w0_ref[...] + _dot_x3(jnp.tanh(w_lo), w2_ref[...]))
        softplus = jnp.maximum(wx, 0.0) + jnp.log(1.0 + jnp.exp(-jnp.abs(wx)))
        logw = -jnp.exp(-softplus - 0.5)
        a = _sigmoid(a0_ref[...] + _dot(a_lo.astype(BF16), a2_ref[...].astype(BF16)))
        g = _dot(_sigmoid(g_lo).astype(BF16), g2_ref[...].astype(BF16))
        kk = k * kk_ref[...]
        kk = kk * lax.rsqrt(jnp.maximum(_seg_sum(kk * kk, seg), 1e-24))
        k2 = k * (1.0 + (a - 1.0) * ka_ref[...])
        cw = _dot_x2_rhs(tri, logw)
        w_inv = jnp.exp(-cw)
        last = jnp.concatenate([jnp.broadcast_to(cw[(ci + 1) * c - 1:(ci + 1) * c], (c, RW_W)) for ci in range(nck)],
                               axis=0)
        w_rest = jnp.exp(last - cw)
        beta = a * kk
        return dict(alpha_t=-kk * jnp.exp(cw - logw), r_t=r * jnp.exp(cw), beta_h=beta * w_inv, k_h=k2 * w_inv,
                    beta_d=beta * w_rest, k_d=k2 * w_rest, v=v, g=g, rkk=r * k2 * rk_ref[...], cw=cw)

    fr = [front(bi) for bi in bs]

    pw = 2 * RW_N
    row2 = lax.broadcasted_iota(jnp.int32, (c, pw), 0)
    col2 = lax.broadcasted_iota(jnp.int32, (c, pw), 1) % c
    incl2 = row2 >= col2
    strict2 = row2 > col2
    blk_mask = (row2 // RW_BLK) == (col2 // RW_BLK)
    eye = (row2 == col2).astype(F32)
    rowp = lax.broadcasted_iota(jnp.int32, (pw, pw), 0)
    colp = lax.broadcasted_iota(jnp.int32, (pw, pw), 1)
    pair_mask = (rowp // RW_N) == (colp // RW_N)

    ps = range(RW_HEADS // 2)
    items = [(bi, ci, p) for bi in bs for ci in range(nck) for p in ps]
    where = {key: i for i, key in enumerate(items)}
    m = range(len(items))

    def slab(name, i):
        bi, ci, p = items[i]
        return fr[bi][name][ci * c:(ci + 1) * c, p * pw:(p + 1) * pw]

    v2 = [slab("v", i) for i in m]
    lhs = [jnp.concatenate([slab("alpha_t", i), slab("r_t", i)], axis=0).astype(BF16) for i in m]
    rhs = [jnp.concatenate([_pair_blockdiag(slab("beta_h", i), pair_mask),
                            _pair_blockdiag(slab("k_h", i), pair_mask)], axis=0) for i in m]
    big = [_dot_nt(lhs[i], rhs[i]) for i in m]
    a_ab = [jnp.where(strict2, big[i][:c, :pw], 0.0) for i in m]
    a_ak = [jnp.where(strict2, big[i][:c, pw:], 0.0) for i in m]
    a_rb = [jnp.where(incl2, big[i][c:, :pw], 0.0) for i in m]
    a_rk = [jnp.where(incl2, big[i][c:, pw:], 0.0) for i in m]
    t_inv = _inv_unit_lower(a_ab, eye, blk_mask, pair_mask)
    av = [_pair_dot(a_ak[i], v2[i], pair_mask) for i in m]
    u_const = [_pair_dot(t_inv[i], av[i], pair_mask) for i in m]
    lhs_s = [jnp.concatenate([_pair_dot(t_inv[i], slab("alpha_t", i), pair_mask).astype(BF16),
                              slab("r_t", i).astype(BF16)], axis=0) for i in m]
    a_r = [jnp.concatenate([a_rb[i], a_rk[i]], axis=1).astype(BF16) for i in m]
    bk_d = [jnp.concatenate([slab("beta_d", i), slab("k_d", i)], axis=0).astype(BF16) for i in m]
    bp = [(bi, p) for bi in bs for p in ps]
    s_cur = {(bi, p): s_ref[bi * len(ps) + p] for bi, p in bp}
    o_parts = {}
    for ci in range(nck):
        ix = {(bi, p): where[bi, ci, p] for bi, p in bp}
        sd = {q: _dot_nt(lhs_s[ix[q]], s_cur[q].astype(BF16)) for q in bp}
        u = {q: sd[q][:c] + u_const[ix[q]] for q in bp}
        uv = {q: jnp.concatenate([_pair_blockdiag(u[q], pair_mask), _pair_blockdiag(v2[ix[q]], pair_mask)], axis=0)
              for q in bp}
        for q in bp:
            o_parts[q[0], ci, q[1]] = sd[q][c:] + _dot(a_r[ix[q]], uv[q])
        uvt = {q: jnp.concatenate([u[q], v2[ix[q]]], axis=0).astype(BF16) for q in bp}
        for bi, p in bp:
            w_last = jnp.exp(fr[bi]["cw"][(ci + 1) * c - 1:(ci + 1) * c, p * pw:(p + 1) * pw])
            s_cur[bi, p] = s_cur[bi, p] * w_last + jnp.where(pair_mask, _dot_tn(uvt[bi, p], bk_d[ix[bi, p]]), 0.0)
    for bi, p in bp:
        s_ref[bi * len(ps) + p] = s_cur[bi, p]

    for bi in bs:
        o = jnp.concatenate([jnp.concatenate([o_parts[bi, ci, p] for p in ps], axis=1) for ci in range(nck)], axis=0)
        mean = _seg_sum(o, seg) * (1.0 / RW_N)
        dev = o - mean
        var = _seg_sum(dev * dev, seg) * (1.0 / RW_N)
        o = dev * lax.rsqrt(var + RW_GN_EPS) * lnw_ref[...] + lnb_ref[...]
        bonus = _seg_sum(fr[bi]["rkk"], seg) * fr[bi]["v"]
        o_ref[bi] = ((o + bonus) * fr[bi]["g"]).astype(o_ref.dtype)


def _rwkv_call(z3, mu, w0, w2, a0, a2, g2, k_k, k_a, r_k, ln_w, ln_b):
    b, t, _ = z3.shape
    c = min(RW_TB, t)
    hid = lax.broadcasted_iota(jnp.int32, (RW_W, RW_W), 0) // RW_N
    seg = (hid == hid.T).astype(BF16)

    def vec(n):
        return pl.BlockSpec((1, n), lambda i, j: (0, 0))

    def mat(m, n):
        return pl.BlockSpec((m, n), lambda i, j: (0, 0))

    nbe = RW_NB if b % RW_NB == 0 else 1
    return pl.pallas_call(
        _rwkv_kernel,
        grid=(b // nbe, t // c),
        in_specs=[
            pl.BlockSpec((nbe, c, RW_COLS), lambda i, j: (i, j, RW_OFF // RW_COLS)),
            vec(RW_COLS), vec(RW_W), mat(RW_DECAY_LORA, RW_W), vec(RW_W), mat(RW_A_LORA, RW_W),
            mat(RW_GATE_LORA, RW_W), vec(RW_W), vec(RW_W), vec(RW_W), vec(RW_W), vec(RW_W),
            mat(RW_W, RW_W),
        ],
        out_specs=pl.BlockSpec((nbe, c, RW_W), lambda i, j: (i, j, 0)),
        out_shape=jax.ShapeDtypeStruct((b, t, RW_W), BF16),
        scratch_shapes=[pltpu.VMEM((nbe * (RW_HEADS // 2), 2 * RW_N, 2 * RW_N), F32),
                        pltpu.VMEM((nbe, RW_COLS), F32)],
        compiler_params=_cparams(("parallel", "arbitrary")),
        name="rwkv7_mixer",
    )(z3, mu.reshape(1, -1), w0.reshape(1, -1), w2, a0.reshape(1, -1), a2, g2, k_k.reshape(1, -1),
      k_a.reshape(1, -1), r_k.reshape(1, -1), ln_w.reshape(1, -1), ln_b.reshape(1, -1), seg)


def _merge_kernel(ohg_ref, oret_ref, orw_ref, zg_ref, x_ref, gate_ref, bhg_ref, bret_ref, brw_ref,
                  wout_ref, o_ref):
    d = x_ref.shape[1]
    y = _sigmoid(zg_ref[:, 0:d].astype(F32)) * _dot(ohg_ref[...], bhg_ref[...])
    y = y + _sigmoid(zg_ref[:, d:2 * d].astype(F32)) * _dot(oret_ref[...], bret_ref[...])
    y = y + _sigmoid(zg_ref[:, 2 * d:3 * d].astype(F32)) * _dot(orw_ref[...], brw_ref[...])
    o_ref[...] = x_ref[...] + gate_ref[0] * _dot(y.astype(BF16), wout_ref[...])


def _merge_call(o_hg, o_ret, o_rw, z2, x2, mod3, br_hg, br_ret, br_rw, w_out, seq, gate_blk, tm=1024):
    n, d = x2.shape
    tpb = seq // tm

    def rows(w):
        return pl.BlockSpec((tm, w), lambda i: (i, 0))

    def full(m, k):
        return pl.BlockSpec((m, k), lambda i: (0, 0))

    return pl.pallas_call(
        _merge_kernel,
        grid=(n // tm,),
        in_specs=[
            rows(HG_W), rows(RET_W), rows(RW_W), rows(3 * d), rows(d),
            pl.BlockSpec((1, 1, d), lambda i: (i // tpb, 0, gate_blk)),
            full(HG_W, d), full(RET_W, d), full(RW_W, d), full(d, d),
        ],
        out_specs=rows(d),
        out_shape=jax.ShapeDtypeStruct((n, d), F32),
        compiler_params=_cparams(("parallel",)),
        name="merge_outproj",
    )(o_hg, o_ret, o_rw, z2, x2, mod3, br_hg, br_ret, br_rw, w_out)


def _pack_bf16_pairs(x):
    w = x.shape[1] // 2
    hi = pltpu.bitcast(x[:, :w].astype(BF16).astype(F32), jnp.uint32)
    lo = pltpu.bitcast(x[:, w:].astype(BF16).astype(F32), jnp.uint32)
    return pltpu.bitcast(hi | lax.shift_right_logical(lo, jnp.uint32(16)), jnp.int32)


def _unpack_bf16_pairs(p):
    u = pltpu.bitcast(p, jnp.uint32)
    hi = pltpu.bitcast(u & jnp.uint32(0xFFFF0000), F32)
    lo = pltpu.bitcast(lax.shift_left(u, jnp.uint32(16)), F32)
    return jnp.concatenate([hi, lo], axis=1)


def _route_kernel(x_ref, g_ref, scale_ref, shift_ref, rc_ref, hp_ref, eid_ref, wts_ref, cnt_ref):
    @pl.when(pl.program_id(0) == 0)
    def _():
        cnt_ref[...] = jnp.zeros_like(cnt_ref)

    h = _rms_mod(x_ref[...], g_ref[...], scale_ref[0], shift_ref[0])
    hp_ref[...] = _pack_bf16_pairs(h)
    tm = h.shape[0]
    lane = lax.broadcasted_iota(jnp.int32, (tm, LANES), 1)
    neg = -jnp.inf
    logits = _dot_x3(h, rc_ref[...])
    gl = jnp.where(lane < N_GROUPS, logits, neg)
    gmax = jnp.max(gl, axis=-1, keepdims=True)
    gidx = jnp.min(jnp.where(gl == gmax, lane, LANES), axis=-1, keepdims=True)
    gw = 1.0 / jnp.sum(jnp.exp(gl - gmax), axis=-1, keepdims=True)
    lo = N_GROUPS + gidx * EXPERTS_PER_GROUP
    el = jnp.where(lane >= lo, jnp.where(lane < lo + EXPERTS_PER_GROUP, logits, neg), neg)
    m1 = jnp.max(el, axis=-1, keepdims=True)
    l1 = jnp.min(jnp.where(el == m1, lane, LANES), axis=-1, keepdims=True)
    el2 = jnp.where(lane == l1, neg, el)
    m2 = jnp.max(el2, axis=-1, keepdims=True)
    l2 = jnp.min(jnp.where(el2 == m2, lane, LANES), axis=-1, keepdims=True)
    i1 = l1 - N_GROUPS
    i2 = l2 - N_GROUPS
    e2 = jnp.exp(m2 - m1)
    p1 = 1.0 / (1.0 + e2)
    p2 = e2 * p1
    oh1 = jnp.where(lane == i1, 1.0, 0.0)
    oh2 = jnp.where(lane == i2, 1.0, 0.0)
    row = lax.broadcasted_iota(jnp.int32, (tm, tm), 0)
    col = lax.broadcasted_iota(jnp.int32, (tm, tm), 1)
    earlier = jnp.where(row > col, 1.0, 0.0).astype(BF16)
    before = _dot(earlier, jnp.concatenate([oh1, oh2], axis=1).astype(BF16))
    tot1 = jnp.sum(oh1, axis=0, keepdims=True)
    carry = cnt_ref[...]
    r1 = jnp.sum(oh1 * (before[:, :LANES] + carry), axis=-1, keepdims=True).astype(jnp.int32)
    r2 = jnp.sum(oh2 * (before[:, LANES:] + (carry + tot1)), axis=-1, keepdims=True).astype(jnp.int32)
    cnt_ref[...] = carry + tot1 + jnp.sum(oh2, axis=0, keepdims=True)
    eid_ref[...] = jnp.where(lane == 0, i1, jnp.where(lane == 1, i2, jnp.where(lane == 2, r1,
                                                                             jnp.where(lane == 3, r2, 0))))
    wts_ref[...] = jnp.where(lane == 0, gw * p1, jnp.where(lane == 1, gw * p2, 0.0))


def _route_call(x2, gain, mod3, router_g, router_e, seq, scale_blk, shift_blk, tm=1024):
    n, d = x2.shape
    tpb = seq // tm
    rc = jnp.pad(jnp.concatenate([router_g, router_e], axis=1), ((0, 0), (0, LANES - N_GROUPS - N_EXPERTS)))
    return pl.pallas_call(
        _route_kernel,
        grid=(n // tm,),
        in_specs=[
            pl.BlockSpec((tm, d), lambda i: (i, 0)),
            pl.BlockSpec((1, d), lambda i: (0, 0)),
            pl.BlockSpec((1, 1, d), lambda i: (i // tpb, 0, scale_blk)),
            pl.BlockSpec((1, 1, d), lambda i: (i // tpb, 0, shift_blk)),
            pl.BlockSpec((d, LANES), lambda i: (0, 0)),
        ],
        out_specs=[pl.BlockSpec((tm, d // 2), lambda i: (i, 0)), pl.BlockSpec((tm, LANES), lambda i: (i, 0)),
                   pl.BlockSpec((tm, LANES), lambda i: (i, 0)), pl.BlockSpec((1, LANES), lambda i: (0, 0))],
        out_shape=[jax.ShapeDtypeStruct((n, d // 2), jnp.int32), jax.ShapeDtypeStruct((n, LANES), jnp.int32),
                   jax.ShapeDtypeStruct((n, LANES), F32), jax.ShapeDtypeStruct((1, LANES), F32)],
        compiler_params=_cparams(("arbitrary",)),
        name="moe_route",
    )(x2, gain.reshape(1, d), mod3, mod3, rc)


SC_CORES = 2
SC_SUBCORES = 16
SC_WORKERS = SC_CORES * SC_SUBCORES
SC_ROWS = 32
SC_STREAMS = 4


def _sc_gather(table, idx):
    m = idx.shape[0]
    w = table.shape[1]
    per_worker = m // SC_WORKERS
    steps = per_worker // SC_ROWS
    assert per_worker * SC_WORKERS == m and steps * SC_ROWS == per_worker and steps % SC_STREAMS == 0
    mesh = plsc.VectorSubcoreMesh(core_axis_name="c", subcore_axis_name="s")
    ks = range(SC_STREAMS)

    def body(table_hbm, idx_hbm, out_hbm, idx_v, *rest):
        bufs, g_sems, w_sems = rest[:SC_STREAMS], rest[SC_STREAMS:2 * SC_STREAMS], rest[2 * SC_STREAMS:]
        wid = lax.axis_index("s") * SC_CORES + lax.axis_index("c")
        pltpu.sync_copy(idx_hbm.at[wid], idx_v)

        @pl.loop(0, steps, step=SC_STREAMS)
        def _(j):
            row0 = wid * per_worker + j * SC_ROWS
            gathers = [pltpu.async_copy(table_hbm.at[idx_v.at[j + q]], bufs[q], g_sems[q]) for q in ks]
            writes = []
            for q in ks:
                gathers[q].wait()
                writes.append(pltpu.async_copy(bufs[q], out_hbm.at[pl.ds(row0 + q * SC_ROWS, SC_ROWS)], w_sems[q]))
            for q in ks:
                writes[q].wait()

    return pl.kernel(
        body,
        out_type=jax.ShapeDtypeStruct((m, w), table.dtype),
        mesh=mesh,
        scratch_types=[pltpu.VMEM((steps, SC_ROWS), jnp.int32)] + [pltpu.VMEM((SC_ROWS, w), table.dtype)] * SC_STREAMS
        + [pltpu.SemaphoreType.DMA] * (2 * SC_STREAMS),
        name="sc_row_gather",
    )(table, idx.reshape(SC_WORKERS, steps, SC_ROWS))


def _sc_scatter2(rows, idx0, idx1, p):
    n, w = rows.shape
    per_worker = n // SC_WORKERS
    steps = per_worker // SC_ROWS
    assert per_worker * SC_WORKERS == n and steps * SC_ROWS == per_worker and steps % SC_STREAMS == 0
    mesh = plsc.VectorSubcoreMesh(core_axis_name="c", subcore_axis_name="s")
    ks = range(SC_STREAMS)

    def body(rows_hbm, i0_hbm, i1_hbm, out_hbm, i0_v, i1_v, *rest):
        bufs, r_sems = rest[:SC_STREAMS], rest[SC_STREAMS:2 * SC_STREAMS]
        s0_sems, s1_sems = rest[2 * SC_STREAMS:3 * SC_STREAMS], rest[3 * SC_STREAMS:]
        wid = lax.axis_index("s") * SC_CORES + lax.axis_index("c")
        pltpu.sync_copy(i0_hbm.at[wid], i0_v)
        pltpu.sync_copy(i1_hbm.at[wid], i1_v)

        @pl.loop(0, steps, step=SC_STREAMS)
        def _(j):
            row0 = wid * per_worker + j * SC_ROWS
            reads = [pltpu.async_copy(rows_hbm.at[pl.ds(row0 + q * SC_ROWS, SC_ROWS)], bufs[q], r_sems[q]) for q in ks]
            writes = []
            for q in ks:
                reads[q].wait()
                writes.append(pltpu.async_copy(bufs[q], out_hbm.at[i0_v.at[j + q]], s0_sems[q]))
                writes.append(pltpu.async_copy(bufs[q], out_hbm.at[i1_v.at[j + q]], s1_sems[q]))
            for wr in writes:
                wr.wait()

    index_block = pltpu.VMEM((steps, SC_ROWS), jnp.int32)
    return pl.kernel(
        body,
        out_type=jax.ShapeDtypeStruct((p, w), rows.dtype),
        mesh=mesh,
        scratch_types=[index_block, index_block] + [pltpu.VMEM((SC_ROWS, w), rows.dtype)] * SC_STREAMS
        + [pltpu.SemaphoreType.DMA] * (3 * SC_STREAMS),
        name="sc_row_scatter",
    )(rows, idx0.reshape(SC_WORKERS, steps, SC_ROWS), idx1.reshape(SC_WORKERS, steps, SC_ROWS))


MOE_TM = 512


def _gexperts_kernel(te_ref, tv_ref, nu_ref, st_ref, sl_ref, nx_ref, xs_ref, w1_hbm, w3_hbm, w2_hbm, ys_ref,
                     f1_ref, f3_ref, f2_ref, w1b_ref, w3b_ref, w2b_ref, sem):
    i = pl.program_id(0)

    def copies(e, slot):
        return [pltpu.make_async_copy(w1_hbm.at[e], f1_ref.at[slot], sem.at[slot, 0]),
                pltpu.make_async_copy(w3_hbm.at[e], f3_ref.at[slot], sem.at[slot, 1]),
                pltpu.make_async_copy(w2_hbm.at[e], f2_ref.at[slot], sem.at[slot, 2])]

    @pl.when(i == 0)
    def _():
        for cp in copies(te_ref[0], sl_ref[0]):
            cp.start()

    @pl.when(st_ref[i] == 1)
    def _():
        slot = sl_ref[i]
        for cp in copies(te_ref[i], slot):
            cp.wait()

        @pl.when(nx_ref[i] >= 0)
        def _():
            for cp in copies(nx_ref[i], 1 - slot):
                cp.start()

        w1b_ref[...] = f1_ref[slot].astype(BF16)
        w3b_ref[...] = f3_ref[slot].astype(BF16)
        w2b_ref[...] = f2_ref[slot].astype(BF16)

    @pl.when(i < nu_ref[0])
    def _():
        rid = lax.broadcasted_iota(jnp.int32, xs_ref.shape, 0)
        xb = _unpack_bf16_pairs(jnp.where(rid < tv_ref[i], xs_ref[...], 0)).astype(BF16)
        act = (_silu(_dot(xb, w1b_ref[...])) * _dot(xb, w3b_ref[...])).astype(BF16)
        ys_ref[...] = _pack_bf16_pairs(_dot(act, w2b_ref[...]))


def _gexperts_call(xs, tile_expert, tile_valid, n_used, w1, w3, w2):
    p, half = xs.shape
    ne, d, de = w1.shape
    nt = p // MOE_TM
    te = tile_expert
    starts = jnp.concatenate([jnp.ones((1,), jnp.int32), (te[1:] != te[:-1]).astype(jnp.int32)])
    slot = (jnp.cumsum(starts) - 1) % 2
    later = jnp.where(te[None, :] > te[:, None], te[None, :], jnp.iinfo(jnp.int32).max)
    nxt = jnp.min(later, axis=1)
    nxt = jnp.where(nxt == jnp.iinfo(jnp.int32).max, -1, nxt)

    def rows(i, *_):
        return (jnp.minimum(i, _[2][0] - 1), 0)

    return pl.pallas_call(
        _gexperts_kernel,
        grid_spec=pltpu.PrefetchScalarGridSpec(
            num_scalar_prefetch=6,
            grid=(nt,),
            in_specs=[
                pl.BlockSpec((MOE_TM, half), rows),
                pl.BlockSpec(memory_space=pl.ANY),
                pl.BlockSpec(memory_space=pl.ANY),
                pl.BlockSpec(memory_space=pl.ANY),
            ],
            out_specs=pl.BlockSpec((MOE_TM, half), rows),
            scratch_shapes=[pltpu.VMEM((2, d, de), F32), pltpu.VMEM((2, d, de), F32), pltpu.VMEM((2, de, d), F32),
                            pltpu.VMEM((d, de), BF16), pltpu.VMEM((d, de), BF16), pltpu.VMEM((de, d), BF16),
                            pltpu.SemaphoreType.DMA((2, 3))],
        ),
        out_shape=jax.ShapeDtypeStruct((p, half), jnp.int32),
        compiler_params=_cparams(("arbitrary",)),
        name="moe_experts",
    )(tile_expert, tile_valid, n_used, starts, slot.astype(jnp.int32), nxt.astype(jnp.int32), xs, w1, w3, w2)


def _combine_kernel(y0_ref, y1_ref, wts_ref, x_ref, gate_ref, fg_ref, o_ref, *, final_norm):
    wts = wts_ref[...]
    moe = wts[:, 0:1] * _unpack_bf16_pairs(y0_ref[...]) + wts[:, 1:2] * _unpack_bf16_pairs(y1_ref[...])
    xn = x_ref[...] + gate_ref[0] * moe
    if final_norm:
        xn = xn * lax.rsqrt(jnp.mean(xn * xn, axis=-1, keepdims=True) + NORM_EPS) * fg_ref[...]
    o_ref[...] = xn


def _combine_call(yg, wts, x2, mod3, final_g, seq, gate_blk, final_norm, tm=1024):
    n, d = x2.shape
    tpb = seq // tm
    slot1 = n // tm
    return pl.pallas_call(
        functools.partial(_combine_kernel, final_norm=final_norm),
        grid=(n // tm,),
        in_specs=[
            pl.BlockSpec((tm, d // 2), lambda i: (i, 0)),
            pl.BlockSpec((tm, d // 2), lambda i: (i + slot1, 0)),
            pl.BlockSpec((tm, LANES), lambda i: (i, 0)),
            pl.BlockSpec((tm, d), lambda i: (i, 0)),
            pl.BlockSpec((1, 1, d), lambda i: (i // tpb, 0, gate_blk)),
            pl.BlockSpec((1, d), lambda i: (0, 0)),
        ],
        out_specs=pl.BlockSpec((tm, d), lambda i: (i, 0)),
        out_shape=jax.ShapeDtypeStruct((n, d), F32),
        compiler_params=_cparams(("parallel",)),
        name="moe_combine",
    )(yg, yg, wts, x2, mod3, final_g.reshape(1, d))


def _pos_kernel(eid_ref, ts_ref, p0_ref, p1_ref):
    eid = eid_ref[...]
    tm = eid.shape[0]
    lane = lax.broadcasted_iota(jnp.int32, (tm, LANES), 1)
    sub = lax.broadcasted_iota(jnp.int32, (tm, LANES), 0) % LANES
    for slot, out_ref in ((0, p0_ref), (1, p1_ref)):
        first_row = jnp.sum(jnp.where(lane == eid[:, slot:slot + 1], ts_ref[...], 0), axis=-1, keepdims=True)
        pos = first_row + eid[:, slot + 2:slot + 3]
        out_ref[...] = jnp.sum(jnp.where(lane == sub, pos, 0).reshape(tm // LANES, LANES, LANES), axis=1)


def _pos_call(eid, first_rows, tm=4096):
    n = eid.shape[0]
    tm = min(tm, n)
    out = jax.ShapeDtypeStruct((n // LANES, LANES), jnp.int32)
    p0, p1 = pl.pallas_call(
        _pos_kernel,
        grid=(n // tm,),
        in_specs=[pl.BlockSpec((tm, LANES), lambda i: (i, 0)), pl.BlockSpec((1, LANES), lambda i: (0, 0))],
        out_specs=[pl.BlockSpec((tm // LANES, LANES), lambda i: (i, 0))] * 2,
        out_shape=[out, out],
        compiler_params=_cparams(("parallel",)),
        name="moe_positions",
    )(eid, first_rows)
    return p0.reshape(n), p1.reshape(n)


def _moe_plan(eid, counts_f):
    n = eid.shape[0]
    nt = (2 * n) // MOE_TM + N_EXPERTS
    counts = counts_f[0, :N_EXPERTS].astype(jnp.int32)
    tiles = (counts + MOE_TM - 1) // MOE_TM
    tile_end = jnp.cumsum(tiles)
    tile_start = tile_end - tiles
    n_used = tile_end[-1:]
    tile_iota = jnp.arange(nt, dtype=jnp.int32)
    tile_expert = jnp.sum(jnp.minimum(tile_iota, n_used - 1)[:, None] >= tile_end[None, :], axis=1, dtype=jnp.int32)
    own = tile_expert[:, None] == jnp.arange(N_EXPERTS, dtype=jnp.int32)[None, :]
    count_t = jnp.sum(jnp.where(own, counts[None, :], 0), axis=1)
    start_t = jnp.sum(jnp.where(own, tile_start[None, :], 0), axis=1)
    tile_valid = jnp.clip(count_t - (tile_iota - start_t) * MOE_TM, 0, MOE_TM)
    first_rows = jnp.pad(tile_start * MOE_TM, (0, LANES - N_EXPERTS)).reshape(1, LANES)
    pos0, pos1 = _pos_call(eid, first_rows)
    return pos0, pos1, tile_expert, tile_valid, n_used


def kernel(x, c, positions, ada_w, ada_b, norm1_g, norm2_g, w_in, hg_lb_table, hg_norm_w, rw_mu, rw_w0, rw_w2,
           rw_a0, rw_a2, rw_g2, rw_k_k, rw_k_a, rw_r_k, rw_ln_w, rw_ln_b, br_hg, br_ret, br_rw, w_out,
           router_g, router_e, moe_w1, moe_w3, moe_w2, final_g):
    b, t, d = x.shape
    depth = ada_w.shape[0]
    n = b * t
    assert w_in.shape[2] == IN_COLS and d == 1024

    lb_p = jax.nn.softmax(hg_lb_table.astype(F32), axis=0)
    lower_bounds = jnp.cumsum(lb_p, axis=0) - lb_p[0]

    mod = _mod_call(c, ada_w, ada_b)
    cos2, sin2 = _rope_call(positions, RET_DK)
    x2 = x.reshape(n, d)
    for l in range(depth):
        mod3 = mod[l].reshape(b, 1, 6 * d)
        z2 = _inproj_call(x2, norm1_g[l], mod3, _wprep_call(w_in, l), t, scale_blk=1, shift_blk=0)
        z3 = z2.reshape(b, t, IN_COLS)
        o_hg = _hgrn2_call(z3, lower_bounds[l], hg_norm_w[l])
        o_ret = _ret_call(z3, cos2, sin2)
        o_rw = _rwkv_call(z3, rw_mu[l], rw_w0[l], rw_w2[l], rw_a0[l], rw_a2[l], rw_g2[l], rw_k_k[l],
                          rw_k_a[l], rw_r_k[l], rw_ln_w[l], rw_ln_b[l])
        x2 = _merge_call(o_hg.reshape(n, HG_W), o_ret.reshape(n, RET_W), o_rw.reshape(n, RW_W), z2, x2, mod3,
                         br_hg[l].astype(BF16), br_ret[l].astype(BF16), br_rw[l].astype(BF16),
                         w_out[l].astype(BF16), t, gate_blk=2)
        hp, eid, wts, counts = _route_call(x2, norm2_g[l], mod3, router_g[l], router_e[l], t, scale_blk=4,
                                           shift_blk=3)
        pos0, pos1, tile_expert, tile_valid, n_used = _moe_plan(eid, counts)
        xs = _sc_scatter2(hp, pos0, pos1, (2 * n // MOE_TM + N_EXPERTS) * MOE_TM)
        ys = _gexperts_call(xs, tile_expert + l * N_EXPERTS, tile_valid, n_used,
                            moe_w1.reshape((-1,) + moe_w1.shape[2:]), moe_w3.reshape((-1,) + moe_w3.shape[2:]),
                            moe_w2.reshape((-1,) + moe_w2.shape[2:]))
        yg = _sc_gather(ys, jnp.concatenate([pos0, pos1]))
        x2 = _combine_call(yg, wts, x2, mod3, final_g, t, gate_blk=5, final_norm=(l == depth - 1))
    return x2.reshape(b, t, d)
```
